```python
import jax, jax.numpy as jnp
from jax import lax
import numpy as np

D_MODEL = 1024
BATCH = 16
SEQ = 256
DEPTH = 2
DEC_BATCH = 2
DEC_SEQ = 1024
PAST_LEN = 256

GRID_W = 64
HEAD_DIM = 64
W_GROUP = D_MODEL // 4
H_GDN = W_GROUP // HEAD_DIM
H_RET = W_GROUP // HEAD_DIM
H_NAT = W_GROUP // HEAD_DIM
SC_WIDTH = W_GROUP
GDN_CONV = 3
SC_CONV = 3
CHUNK = 64
WIN_ROWS = 8
WIN_COLS = 16
QBLOCK = 128
ROPE_BASE = 10000.0
N_GROUPS = 4
EXPERTS_PER_GROUP = 8
N_EXPERTS = N_GROUPS * EXPERTS_PER_GROUP
TOP_K = 2
EXPERT_FF = 256
EPS = 1e-6
IN_SIZES = (3 * W_GROUP, W_GROUP, 2 * H_GDN, 2 * H_GDN, 3 * W_GROUP, W_GROUP, 3 * W_GROUP, 3 * SC_WIDTH)
IN_WIDTH = 3 * W_GROUP + W_GROUP + 4 * H_GDN + 4 * W_GROUP + 3 * W_GROUP + 3 * SC_WIDTH

kernel_name = 'hybrid_diffusion_parallel_heads_step'


def rms_norm(x, w):
    xf = x.astype(jnp.float32)
    y = xf * lax.rsqrt(jnp.mean(xf * xf, axis=-1, keepdims=True) + EPS)
    return (y * w.astype(jnp.float32)).astype(x.dtype)


def l2_normalize(x):
    xf = x.astype(jnp.float32)
    return xf * lax.rsqrt(jnp.sum(xf * xf, axis=-1, keepdims=True) + EPS)


def head_group_norm(x):
    xf = x.astype(jnp.float32)
    mu = jnp.mean(xf, axis=-1, keepdims=True)
    xc = xf - mu
    return xc * lax.rsqrt(jnp.mean(xc * xc, axis=-1, keepdims=True) + EPS)


def depthwise_conv(x, w):
    k, ch = w.shape
    return lax.conv_general_dilated(x, w[:, None, :].astype(x.dtype), window_strides=(1,),
                                    padding=[(k // 2, k // 2)], dimension_numbers=('NWC', 'WIO', 'NWC'),
                                    feature_group_count=ch)


def axial_rope(x):
    t = jnp.arange(x.shape[1])
    row = (t // GRID_W).astype(jnp.float32)
    col = (t % GRID_W).astype(jnp.float32)
    nf = HEAD_DIM // 4
    inv_freq = jnp.power(ROPE_BASE, -jnp.arange(nf, dtype=jnp.float32) / nf)

    def rotate(xh, pos):
        ang = pos[:, None] * inv_freq[None, :]
        cos = jnp.cos(ang)[:, None, :]
        sin = jnp.sin(ang)[:, None, :]
        x1 = xh[..., :nf].astype(jnp.float32)
        x2 = xh[..., nf:].astype(jnp.float32)
        return jnp.concatenate([x1 * cos - x2 * sin, x2 * cos + x1 * sin], axis=-1)

    half = HEAD_DIM // 2
    return jnp.concatenate([rotate(x[..., :half], row), rotate(x[..., half:], col)], axis=-1).astype(x.dtype)


def to_chunks(a):
    b, t, h, d = a.shape
    return a.reshape(b, t // CHUNK, CHUNK, h, d).transpose(1, 0, 3, 2, 4)


def from_chunks(a):
    n, b, h, c, d = a.shape
    return a.transpose(1, 0, 3, 2, 4).reshape(b, n * c, h, d)


def gated_delta_chunked(q, k, v, log_a, beta, s0):
    dk = q.shape[-1]
    q = to_chunks(q.astype(jnp.float32)) * (dk ** -0.5)
    k = to_chunks(k.astype(jnp.float32))
    v = to_chunks(v.astype(jnp.float32))
    g = to_chunks(log_a[..., None].astype(jnp.float32))[..., 0]
    bt = to_chunks(beta[..., None].astype(jnp.float32))[..., 0]
    gc = jnp.cumsum(g, axis=-1)
    causal = jnp.tril(jnp.ones((CHUNK, CHUNK), bool))
    strict = jnp.tril(jnp.ones((CHUNK, CHUNK), bool), -1)
    decay = jnp.exp(jnp.where(causal, gc[..., :, None] - gc[..., None, :], -jnp.inf))
    kb = k * bt[..., None]
    low = jnp.where(strict, jnp.einsum('nbhcd,nbhed->nbhce', kb, k) * decay, 0.0)
    eye = jnp.eye(CHUNK, dtype=jnp.float32)
    t_inv = lax.linalg.triangular_solve(eye + low, jnp.broadcast_to(eye, low.shape), left_side=True, lower=True)
    u = t_inv @ (v * bt[..., None])
    w = t_inv @ (kb * jnp.exp(gc)[..., None])
    attn = jnp.einsum('nbhcd,nbhed->nbhce', q, k) * decay
    q_dec = q * jnp.exp(gc)[..., None]
    k_dec = k * jnp.exp(gc[..., -1:] - gc)[..., None]
    c_dec = jnp.exp(gc[..., -1])

    def step(s, xs):
        qd, at, ui, wi, kd, cd = xs
        v_new = ui - wi @ s
        o = qd @ s + at @ v_new
        s = s * cd[..., None, None] + jnp.einsum('bhcd,bhce->bhde', kd, v_new)
        return s, o

    s_fin, o = lax.scan(step, s0.astype(jnp.float32), (q_dec, attn, u, w, k_dec, c_dec))
    return from_chunks(o), s_fin


def retention_chunked(q, k, v, log_gamma, s0):
    dk = q.shape[-1]
    q = to_chunks(q.astype(jnp.float32))
    k = to_chunks(k.astype(jnp.float32)) * (dk ** -0.5)
    v = to_chunks(v.astype(jnp.float32))
    idx = jnp.arange(CHUNK, dtype=jnp.float32)
    lg = log_gamma.astype(jnp.float32)[:, None]
    causal = jnp.tril(jnp.ones((CHUNK, CHUNK), bool))
    dmat = jnp.exp(jnp.where(causal, (idx[:, None] - idx[None, :]) * lg[:, :, None], -jnp.inf))
    inner = (jnp.einsum('nbhcd,nbhed->nbhce', q, k) * dmat) @ v
    k_dec = jnp.exp((CHUNK - 1 - idx) * lg)
    kv = jnp.einsum('nbhcd,nbhce->nbhde', k * k_dec[..., None], v)
    c_dec = jnp.exp(CHUNK * lg)[:, :, None]

    def step(s, kv_c):
        return s * c_dec + kv_c, s

    s_fin, s_prev = lax.scan(step, s0.astype(jnp.float32), kv)
    cross = jnp.exp((idx + 1.0) * lg)
    o = inner + jnp.einsum('nbhcd,nbhde->nbhce', q, s_prev) * cross[..., None]
    return from_chunks(o), s_fin


def context_attention(q, k, v):
    b, l, h, d = q.shape
    qb = q.reshape(b, l // QBLOCK, QBLOCK, h, d).transpose(1, 0, 2, 3, 4)

    def block(qi):
        s = jnp.einsum('bqhd,blhd->bhql', qi, k).astype(jnp.float32) * (d ** -0.5)
        p = jax.nn.softmax(s, axis=-1).astype(v.dtype)
        return jnp.einsum('bhql,blhd->bqhd', p, v)

    o = lax.map(block, qb)
    return o.transpose(1, 0, 2, 3, 4).reshape(b, l, h, d)


def neighbourhood_attention(q, k, v, ck, cv, rpb):
    b, n, h, d = q.shape
    rows_n = n // GRID_W
    kh = min(WIN_ROWS, rows_n)
    qg = q.reshape(b, rows_n, GRID_W, h, d).transpose(1, 0, 2, 3, 4)
    kg = k.reshape(b, rows_n, GRID_W, h, d)
    vg = v.reshape(b, rows_n, GRID_W, h, d)
    cols = jnp.arange(GRID_W)
    c0 = jnp.clip(cols - WIN_COLS // 2, 0, GRID_W - WIN_COLS)
    col_in = (cols[None, :] >= c0[:, None]) & (cols[None, :] < c0[:, None] + WIN_COLS)
    col_idx = jnp.clip(cols[None, :] - cols[:, None], -(WIN_COLS - 1), WIN_COLS - 1) + WIN_COLS - 1
    rows = jnp.arange(rows_n)
    r0 = jnp.clip(rows - kh // 2, 0, rows_n - kh)
    scale = d ** -0.5

    def one_row(xs):
        q_r, r, r0_r = xs
        kb = lax.dynamic_slice_in_dim(kg, r0_r, kh, axis=1)
        vb = lax.dynamic_slice_in_dim(vg, r0_r, kh, axis=1)
        row_idx = r0_r + jnp.arange(kh) - r + WIN_ROWS - 1
        bias = rpb[:, row_idx[None, :, None], col_idx[:, None, :]].astype(jnp.float32)
        s_loc = jnp.einsum('bqhd,bikhd->bhqik', q_r, kb).astype(jnp.float32) * scale + bias
        s_loc = jnp.where(col_in[:, None, :], s_loc, -jnp.inf)
        s_ctx = jnp.einsum('bqhd,blhd->bhql', q_r, ck).astype(jnp.float32) * scale
        s = jnp.concatenate([s_loc.reshape(b, h, GRID_W, kh * GRID_W), s_ctx], axis=-1)
        p = jax.nn.softmax(s, axis=-1).astype(v.dtype)
        p_loc = p[..., :kh * GRID_W].reshape(b, h, GRID_W, kh, GRID_W)
        p_ctx = p[..., kh * GRID_W:]
        return jnp.einsum('bhqik,bikhd->bqhd', p_loc, vb) + jnp.einsum('bhql,blhd->bqhd', p_ctx, cv)

    o = lax.map(one_row, (qg, rows, r0))
    return o.transpose(1, 0, 2, 3, 4).reshape(b, n, h, d)


def token_mixer(h, p, cache):
    b, t, _ = h.shape
    latent = cache is not None
    split_at = [int(i) for i in np.cumsum(IN_SIZES)[:-1]]
    a_qkv, a_z, a_dec, a_beta, r_qkv, r_g, n_qkv, s_bcx = jnp.split(h @ p['w_in'], split_at, axis=-1)
    heads = lambda a: a.reshape(b, t, -1, HEAD_DIM)
    flip = lambda a: jnp.flip(a, axis=1)

    a_qkv = jax.nn.silu(depthwise_conv(a_qkv, p['gdn_conv_w']))
    aq, ak, av = jnp.split(a_qkv, 3, axis=-1)
    aq, ak, av = l2_normalize(heads(aq)), l2_normalize(heads(ak)), heads(av)
    log_a = -jnp.exp(p['gdn_a_log'].astype(jnp.float32)) * jax.nn.softplus(
        a_dec.reshape(b, t, 2, H_GDN).astype(jnp.float32) + p['gdn_dt_bias'].astype(jnp.float32))
    beta = jax.nn.sigmoid(a_beta.reshape(b, t, 2, H_GDN).astype(jnp.float32))
    s_gdn = cache[0] if latent else jnp.zeros((b, 2, H_GDN, HEAD_DIM, HEAD_DIM), jnp.float32)
    o_f, gdn_f = gated_delta_chunked(aq, ak, av, log_a[:, :, 0], beta[:, :, 0], s_gdn[:, 0])
    o_b, gdn_b = gated_delta_chunked(flip(aq), flip(ak), flip(av), flip(log_a[:, :, 1]), flip(beta[:, :, 1]), s_gdn[:, 1])
    o_gdn = rms_norm(o_f + flip(o_b), p['gdn_norm_w']) * jax.nn.silu(heads(a_z).astype(jnp.float32))

    rq, rk, rv = [heads(a) for a in jnp.split(r_qkv, 3, axis=-1)]
    if latent:
        rq, rk = axial_rope(rq), axial_rope(rk)
    log_g = jax.nn.log_sigmoid(p['ret_gamma_logit'].astype(jnp.float32))
    s_ret = cache[1] if latent else jnp.zeros((b, 2, H_RET, HEAD_DIM, HEAD_DIM), jnp.float32)
    ro_f, ret_f = retention_chunked(rq, rk, rv, log_g[0], s_ret[:, 0])
    ro_b, ret_b = retention_chunked(flip(rq), flip(rk), flip(rv), log_g[1], s_ret[:, 1])
    o_ret = head_group_norm(ro_f + flip(ro_b)) * jax.nn.silu(heads(r_g).astype(jnp.float32))

    nq, nk, nv = [heads(a) for a in jnp.split(n_qkv, 3, axis=-1)]
    nq = rms_norm(nq, p['nat_q_norm_w'])
    nk = rms_norm(nk, p['nat_k_norm_w'])
    if latent:
        o_nat = neighbourhood_attention(nq, nk, nv, cache[2], cache[3], p['nat_rpb'])
    else:
        o_nat = context_attention(nq, nk, nv)

    sb, sc, sx = jnp.split(s_bcx, 3, axis=-1)
    o_sc = sb * depthwise_conv(sc * sx, p['sc_conv_w'])

    mixed = jnp.concatenate([o_gdn.reshape(b, t, -1).astype(h.dtype), o_ret.reshape(b, t, -1).astype(h.dtype),
                             o_nat.reshape(b, t, -1).astype(h.dtype), o_sc], axis=-1)
    out = mixed @ p['w_out']
    if latent:
        return out, None
    return out, (jnp.stack([gdn_f, gdn_b], axis=1), jnp.stack([ret_f, ret_b], axis=1), nk, nv)


def hier_moe(h, p):
    b, t, d = h.shape
    x = h.reshape(b * t, d)
    gl = (x @ p['router_group_w'] + p['router_group_b']).astype(jnp.float32)
    gp = jax.nn.softmax(gl, axis=-1)
    gw, gsel = lax.top_k(gp, 1)
    el = (x @ p['router_expert_w'] + p['router_expert_b']).astype(jnp.float32).reshape(-1, N_GROUPS, EXPERTS_PER_GROUP)
    el = jnp.take_along_axis(el, gsel[:, :, None], axis=1)[:, 0]
    tw, ti = lax.top_k(jax.nn.softmax(el, axis=-1), TOP_K)
    tw = tw / jnp.sum(tw, axis=-1, keepdims=True)
    eid = gsel * EXPERTS_PER_GROUP + ti
    combine = jnp.sum(jax.nn.one_hot(eid, N_EXPERTS, dtype=jnp.float32) * (gw * tw)[..., None], axis=1)
    hg = jnp.einsum('nd,edf->nef', x, p['moe_w_gate'])
    hu = jnp.einsum('nd,edf->nef', x, p['moe_w_up'])
    act = jax.nn.silu(hg) * hu * combine.astype(x.dtype)[..., None]
    y = jnp.einsum('nef,efd->nd', act, p['moe_w_down'])
    return y.reshape(b, t, d)


def trunk_layer(x, cond, p, cache):
    ada = jax.nn.silu(cond) @ p['ada_w'] + p['ada_b']
    sh_m, sc_m, g_m, sh_f, sc_f, g_f = jnp.split(ada[:, None, :], 6, axis=-1)
    hm = rms_norm(x, p['norm_mix_w']) * (1.0 + sc_m) + sh_m
    m, st = token_mixer(hm, p, cache)
    x = x + g_m * m
    hf = rms_norm(x, p['norm_ffn_w']) * (1.0 + sc_f) + sh_f
    x = x + g_f * hier_moe(hf, p)
    return x, st


def setup_inputs(seed: int = 0) -> dict:
    key = jax.random.key(seed)
    ks = iter(jax.random.split(key, 48))
    f32 = jnp.float32
    nrm = lambda shape, s: jax.random.normal(next(ks), shape, f32) * s
    x_prompt = nrm((BATCH, SEQ, D_MODEL), 1.0)
    x_sample = nrm((DEC_BATCH, DEC_SEQ, D_MODEL), 1.0)
    state_gdn = nrm((DEC_BATCH, DEPTH, 2, H_GDN, HEAD_DIM, HEAD_DIM), 0.3)
    state_ret = nrm((DEC_BATCH, DEPTH, 2, H_RET, HEAD_DIM, HEAD_DIM), 1.0)
    cache_nat_k = nrm((DEC_BATCH, DEPTH, PAST_LEN, H_NAT, HEAD_DIM), 1.0)
    cache_nat_v = nrm((DEC_BATCH, DEPTH, PAST_LEN, H_NAT, HEAD_DIM), 1.0)
    c = nrm((DEC_BATCH, D_MODEL), 1.0)
    c_ctx = nrm((D_MODEL,), 1.0)
    ada_w = nrm((DEPTH, D_MODEL, 6 * D_MODEL), 0.5 * D_MODEL ** -0.5)
    ada_b = nrm((DEPTH, 6 * D_MODEL), 0.01)
    norm_mix_w = 1.0 + nrm((DEPTH, D_MODEL), 0.02)
    norm_ffn_w = 1.0 + nrm((DEPTH, D_MODEL), 0.02)
    w_in = nrm((DEPTH, D_MODEL, IN_WIDTH), D_MODEL ** -0.5)
    gdn_conv_w = nrm((DEPTH, GDN_CONV, 3 * W_GROUP), GDN_CONV ** -0.5)
    gdn_a_log = jnp.log(jax.random.uniform(next(ks), (DEPTH, 2, H_GDN), f32, 1.0, 16.0))
    dt = jnp.exp(jax.random.uniform(next(ks), (DEPTH, 2, H_GDN), f32, float(np.log(1e-3)), float(np.log(1e-1))))
    gdn_dt_bias = dt + jnp.log(-jnp.expm1(-dt))
    gdn_norm_w = 1.0 + nrm((DEPTH, HEAD_DIM), 0.02)
    ret_logit_base = jnp.log(jnp.power(2.0, 5.0 + jnp.arange(H_RET, dtype=f32)) - 1.0)
    ret_gamma_logit = ret_logit_base + nrm((DEPTH, 2, H_RET), 0.1)
    nat_q_norm_w = 1.0 + nrm((DEPTH, HEAD_DIM), 0.02)
    nat_k_norm_w = 1.0 + nrm((DEPTH, HEAD_DIM), 0.02)
    nat_rpb = nrm((DEPTH, H_NAT, 2 * WIN_ROWS - 1, 2 * WIN_COLS - 1), 0.1)
    sc_conv_w = nrm((DEPTH, SC_CONV, SC_WIDTH), SC_CONV ** -0.5)
    w_out = nrm((DEPTH, D_MODEL, D_MODEL), D_MODEL ** -0.5)
    router_group_w = nrm((DEPTH, D_MODEL, N_GROUPS), D_MODEL ** -0.5)
    router_group_b = nrm((DEPTH, N_GROUPS), 0.01)
    router_expert_w = nrm((DEPTH, D_MODEL, N_EXPERTS), D_MODEL ** -0.5)
    router_expert_b = nrm((DEPTH, N_EXPERTS), 0.01)
    moe_w_gate = nrm((DEPTH, N_EXPERTS, D_MODEL, EXPERT_FF), D_MODEL ** -0.5)
    moe_w_up = nrm((DEPTH, N_EXPERTS, D_MODEL, EXPERT_FF), D_MODEL ** -0.5)
    moe_w_down = nrm((DEPTH, N_EXPERTS, EXPERT_FF, D_MODEL), EXPERT_FF ** -0.5)
    return {'x_prompt': x_prompt, 'x_sample': x_sample, 'state_gdn': state_gdn, 'state_ret': state_ret,
            'cache_nat_k': cache_nat_k, 'cache_nat_v': cache_nat_v, 'c': c, 'c_ctx': c_ctx,
            'ada_w': ada_w, 'ada_b': ada_b, 'norm_mix_w': norm_mix_w, 'norm_ffn_w': norm_ffn_w, 'w_in': w_in,
            'gdn_conv_w': gdn_conv_w, 'gdn_a_log': gdn_a_log, 'gdn_dt_bias': gdn_dt_bias, 'gdn_norm_w': gdn_norm_w,
            'ret_gamma_logit': ret_gamma_logit, 'nat_q_norm_w': nat_q_norm_w, 'nat_k_norm_w': nat_k_norm_w,
            'nat_rpb': nat_rpb, 'sc_conv_w': sc_conv_w, 'w_out': w_out, 'router_group_w': router_group_w,
            'router_group_b': router_group_b, 'router_expert_w': router_expert_w, 'router_expert_b': router_expert_b,
            'moe_w_gate': moe_w_gate, 'moe_w_up': moe_w_up, 'moe_w_down': moe_w_down}


def reference(x_prompt, x_sample, state_gdn, state_ret, cache_nat_k, cache_nat_v, c, c_ctx,
              ada_w, ada_b, norm_mix_w, norm_ffn_w, w_in, gdn_conv_w, gdn_a_log, gdn_dt_bias, gdn_norm_w,
              ret_gamma_logit, nat_q_norm_w, nat_k_norm_w, nat_rpb, sc_conv_w, w_out,
              router_group_w, router_group_b, router_expert_w, router_expert_b,
              moe_w_gate, moe_w_up, moe_w_down):
    y_p = x_prompt
    y_s = x_sample
    cond_ctx = c_ctx[None, :]
    gdn_list, ret_list, k_list, v_list = [], [], [], []
    for l in range(DEPTH):
        p = {'ada_w': ada_w[l], 'ada_b': ada_b[l], 'norm_mix_w': norm_mix_w[l], 'norm_ffn_w': norm_ffn_w[l],
             'w_in': w_in[l], 'gdn_conv_w': gdn_conv_w[l], 'gdn_a_log': gdn_a_log[l], 'gdn_dt_bias': gdn_dt_bias[l],
             'gdn_norm_w': gdn_norm_w[l], 'ret_gamma_logit': ret_gamma_logit[l], 'nat_q_norm_w': nat_q_norm_w[l],
             'nat_k_norm_w': nat_k_norm_w[l], 'nat_rpb': nat_rpb[l], 'sc_conv_w': sc_conv_w[l], 'w_out': w_out[l],
             'router_group_w': router_group_w[l], 'router_group_b': router_group_b[l],
             'router_expert_w': router_expert_w[l], 'router_expert_b': router_expert_b[l],
             'moe_w_gate': moe_w_gate[l], 'moe_w_up': moe_w_up[l], 'moe_w_down': moe_w_down[l]}
        y_p, (s_g, s_r, n_k, n_v) = trunk_layer(y_p, cond_ctx, p, None)
        gdn_list.append(s_g)
        ret_list.append(s_r)
        k_list.append(n_k)
        v_list.append(n_v)
        cache_l = (state_gdn[:, l], state_ret[:, l], cache_nat_k[:, l], cache_nat_v[:, l])
        y_s, _ = trunk_layer(y_s, c, p, cache_l)
    new_state_gdn = jnp.stack(gdn_list, axis=1)
    new_state_ret = jnp.stack(ret_list, axis=1)
    new_cache_nat_k = jnp.stack(k_list, axis=1)
    new_cache_nat_v = jnp.stack(v_list, axis=1)
    return (y_p, y_s, new_state_gdn, new_state_ret, new_cache_nat_k, new_cache_nat_v)
```

```python
import functools

import numpy as np
import jax
import jax.numpy as jnp
from jax import lax
from jax.experimental import pallas as pl
from jax.experimental.pallas import tpu as pltpu

D_MODEL = 1024
BATCH = 16
SEQ = 256
DEPTH = 2
DEC_BATCH = 2
DEC_SEQ = 1024
PAST_LEN = 256
GRID_W = 64
HEAD_DIM = 64
W_GROUP = D_MODEL // 4
N_HEADS = W_GROUP // HEAD_DIM
CHUNK = 64
WIN_ROWS = 8
WIN_COLS = 16
ROPE_BASE = 10000.0
N_GROUPS = 4
EXPERTS_PER_GROUP = 8
N_EXPERTS = N_GROUPS * EXPERTS_PER_GROUP
EXPERT_FF = 256
GROUP_FF = EXPERTS_PER_GROUP * EXPERT_FF
EPS = 1e-6

N_CTX = BATCH * SEQ
N_LAT = DEC_BATCH * DEC_SEQ
N_TOK = N_CTX + N_LAT
LANES = 128
SUBLANES = 8
TOK_TILE = 512
MOE_TILE = 256
MOE_ROWS = N_TOK + N_GROUPS * MOE_TILE
Q_TILE = 256
VMEM_LIMIT = 48 * 1024 * 1024
NEG_BIG = -1e30
N_GATE = 2 * N_HEADS

F32 = jnp.float32
BF16 = jnp.bfloat16
HI = lax.Precision.HIGHEST


def _mm(a, b, prec=None):
    return lax.dot_general(a, b, (((1,), (0,)), ((), ())), precision=prec, preferred_element_type=F32)


def _mm_nt(a, b, prec=None):
    return lax.dot_general(a, b, (((1,), (1,)), ((), ())), precision=prec, preferred_element_type=F32)


def _mm_tn(a, b, prec=None):
    return lax.dot_general(a, b, (((0,), (0,)), ((), ())), precision=prec, preferred_element_type=F32)


def _sigmoid(x):
    return 1.0 / (1.0 + jnp.exp(-x))


def _silu(x):
    return x * _sigmoid(x)


def _softplus(x):
    return jnp.maximum(x, 0.0) + jnp.log(1.0 + jnp.exp(-jnp.abs(x)))


def _iota(shape, dim):
    return lax.broadcasted_iota(jnp.int32, shape, dim)


def _cparams(*sem):
    return pltpu.CompilerParams(dimension_semantics=sem, vmem_limit_bytes=VMEM_LIMIT)


def _cond_of_tile(i):
    n_ctx_tiles = N_CTX // TOK_TILE
    return jnp.where(i < n_ctx_tiles, 0, 1 + (i - n_ctx_tiles) // (DEC_SEQ // TOK_TILE))


def _head_sum_matrix():
    return (_iota((W_GROUP, W_GROUP), 0) // HEAD_DIM == _iota((W_GROUP, W_GROUP), 1) // HEAD_DIM).astype(F32)


def _ada_kernel(c_ref, w_ref, b_ref, o_ref):
    o_ref[0] = _mm(_silu(c_ref[...]), w_ref[0], HI) + b_ref[0]


def _ada(cond, ada_w, ada_b):
    tn = 1536
    n_out = 6 * D_MODEL
    return pl.pallas_call(
        _ada_kernel,
        grid=(DEPTH, n_out // tn),
        in_specs=[pl.BlockSpec((SUBLANES, D_MODEL), lambda l, j: (0, 0)),
                  pl.BlockSpec((1, D_MODEL, tn), lambda l, j: (l, 0, j)),
                  pl.BlockSpec((1, 1, tn), lambda l, j: (l, 0, j))],
        out_specs=pl.BlockSpec((1, SUBLANES, tn), lambda l, j: (l, 0, j)),
        out_shape=jax.ShapeDtypeStruct((DEPTH, SUBLANES, n_out), F32),
        compiler_params=_cparams("arbitrary", "arbitrary"),
        name="ada",
    )(cond, ada_w, ada_b.reshape(DEPTH, 1, n_out))


IN_WIDTHS = (4 * W_GROUP, 4 * W_GROUP, 3 * W_GROUP, 3 * W_GROUP, LANES)
IN_PACKED = sum(IN_WIDTHS)


def _pack_w_in(w):
    sizes = (3 * W_GROUP, W_GROUP, N_GATE, N_GATE, 3 * W_GROUP, W_GROUP, 3 * W_GROUP, 3 * W_GROUP)
    offs = np.concatenate([[0], np.cumsum(sizes)])
    seg = [w[:, offs[i]:offs[i + 1]] for i in range(len(sizes))]
    gate = jnp.concatenate([seg[2], seg[3], jnp.zeros((D_MODEL, LANES - 2 * N_GATE), w.dtype)], axis=1)
    return jnp.concatenate([seg[0], seg[1], seg[4], seg[5], seg[6], seg[7], gate], axis=1).astype(BF16)


def _inproj_kernel(x_ref, mod_ref, nw_ref, w_ref, *o_refs):
    x = x_ref[...]
    y = x * lax.rsqrt(jnp.mean(x * x, axis=-1, keepdims=True) + EPS) * nw_ref[...]
    h = (y * (1.0 + mod_ref[0, 1:2, :]) + mod_ref[0, 0:1, :]).astype(BF16)
    off = 0
    for o_ref, width in zip(o_refs, IN_WIDTHS):
        o_ref[...] = _mm(h, w_ref[:, off:off + width])
        off += width


def _inproj(x, mod, norm_w, w_packed):
    return pl.pallas_call(
        _inproj_kernel,
        grid=(N_TOK // TOK_TILE,),
        in_specs=[pl.BlockSpec((TOK_TILE, D_MODEL), lambda i: (i, 0)),
                  pl.BlockSpec((1, SUBLANES, D_MODEL), lambda i: (_cond_of_tile(i), 0, 0)),
                  pl.BlockSpec((1, D_MODEL), lambda i: (0, 0)),
                  pl.BlockSpec((D_MODEL, IN_PACKED), lambda i: (0, 0))],
        out_specs=[pl.BlockSpec((TOK_TILE, w), lambda i: (i, 0)) for w in IN_WIDTHS],
        out_shape=[jax.ShapeDtypeStruct((N_TOK, w), F32) for w in IN_WIDTHS],
        compiler_params=_cparams("arbitrary"),
        name="inproj",
    )(x, mod, norm_w.reshape(1, D_MODEL), w_packed)


def _shift_rows(p, t):
    row = _iota(p.shape, 0)
    prev = jnp.where(row == 0, 0.0, pltpu.roll(p, 1, 0))
    nxt = jnp.where(row == t - 1, 0.0, pltpu.roll(p, t - 1, 0))
    return prev, nxt


def _conv3(x, w_ref, t):
    prev, nxt = _shift_rows(x, t)
    return w_ref[0:1, :] * prev + w_ref[1:2, :] * x + w_ref[2:3, :] * nxt


def _sconv_kernel(s_ref, w_ref, o_ref, *, t):
    s = s_ref[...]
    p = s[:, W_GROUP:2 * W_GROUP] * s[:, 2 * W_GROUP:]
    o_ref[...] = s[:, :W_GROUP] * _conv3(p, w_ref, t)


def _sconv(s_all, w, t, n_seq, first_block):
    return pl.pallas_call(
        functools.partial(_sconv_kernel, t=t),
        grid=(n_seq,),
        in_specs=[pl.BlockSpec((t, 3 * W_GROUP), lambda i: (i + first_block, 0)),
                  pl.BlockSpec((SUBLANES, W_GROUP), lambda i: (0, 0))],
        out_specs=pl.BlockSpec((t, W_GROUP), lambda i: (i, 0)),
        out_shape=jax.ShapeDtypeStruct((n_seq * t, W_GROUP), F32),
        compiler_params=_cparams("arbitrary"),
        name="sconv",
    )(s_all, w)


def _unit_tri_inverse(low, blk_mask, eye):
    d = jnp.where(blk_mask, low, 0.0)
    o = low - d
    d2 = _mm(d, d, HI)
    d4 = _mm(d2, d2, HI)
    d8 = _mm(d4, d4, HI)
    p = eye - d + d2 - _mm(d, d2, HI)
    p = p + _mm(p, d4, HI)
    p = p + _mm(p, d8, HI)
    n = _mm(p, o, HI)
    n2 = _mm(n, n, HI)
    r = eye - n + n2 - _mm(n, n2, HI)
    return _mm(r, p, HI)


def _gdn_kernel(a_ref, gate_ref, convw_ref, alog_ref, dtb_ref, nw_ref, s0_ref, o_ref, sfin_ref,
                q_s, k_s, v_s, gc_s, gt_s, beta_s, of_s, ob_s, *, t):
    n_chunks = t // CHUNK
    a = a_ref[...]
    qkv = _silu(_conv3(a[:, :3 * W_GROUP], convw_ref, t))
    q = qkv[:, :W_GROUP]
    k = qkv[:, W_GROUP:2 * W_GROUP]
    head_sum = _head_sum_matrix()
    q_s[...] = q * lax.rsqrt(_mm(q * q, head_sum, HI) + EPS) * (HEAD_DIM ** -0.5)
    k_s[...] = k * lax.rsqrt(_mm(k * k, head_sum, HI) + EPS)
    v_s[...] = qkv[:, 2 * W_GROUP:]

    gates = gate_ref[...]
    log_a = -jnp.exp(alog_ref[...]) * _softplus(gates + dtb_ref[...])
    beta_s[...] = _sigmoid(gates)

    ci = _iota((CHUNK, CHUNK), 0)
    cj = _iota((CHUNK, CHUNK), 1)
    eye = (ci == cj).astype(F32)
    ones = jnp.ones((CHUNK, CHUNK), F32)
    blk_mask = (ci // 16) == (cj // 16)
    fwd_lane = _iota((CHUNK, LANES), 1) < N_HEADS

    tri_f = (cj <= ci).astype(F32)
    tri_b = (cj >= ci).astype(F32)
    for c in range(n_chunks):
        g = log_a[c * CHUNK:(c + 1) * CHUNK, :]
        gc_s[c * CHUNK:(c + 1) * CHUNK, :] = jnp.where(fwd_lane, _mm(tri_f, g, HI), _mm(tri_b, g, HI))
        gt_s[c * CHUNK:(c + 1) * CHUNK, :] = _mm(ones, g, HI)

    def chunk_step(c, states):
        new_states = []
        for a_idx in range(N_GATE):
            backward = a_idx >= N_HEADS
            h = a_idx % N_HEADS
            cc = (n_chunks - 1 - c) if backward else c
            rows = pl.ds(pl.multiple_of(cc * CHUNK, CHUNK), CHUNK)
            hs = slice(h * HEAD_DIM, (h + 1) * HEAD_DIM)
            incl = (cj >= ci) if backward else (cj <= ci)
            strict = (cj > ci) if backward else (cj < ci)
            qh = q_s[rows, hs]
            kh = k_s[rows, hs]
            vh = v_s[rows, hs]
            gc = gc_s[rows, a_idx:a_idx + 1]
            gt = gt_s[rows, a_idx:a_idx + 1]
            bt = beta_s[rows, N_GATE + a_idx:N_GATE + a_idx + 1]
            gc_col = jnp.broadcast_to(gc, (CHUNK, CHUNK))
            gc_row = _mm(ones, eye * gc_col, HI)
            decay = jnp.exp(jnp.where(incl, gc_col - gc_row, NEG_BIG))
            kb = kh * bt
            kk = _mm_nt(jnp.concatenate([kb, qh], axis=0), kh, HI)
            low = jnp.where(strict, kk[:CHUNK] * decay, 0.0)
            attn = kk[CHUNK:] * decay
            t_inv = _unit_tri_inverse(low, blk_mask, eye)
            eg = jnp.exp(gc)
            uw = _mm(t_inv, jnp.concatenate([vh * bt, kb * eg], axis=1), HI)
            s = states[a_idx]
            ws_qs = _mm(jnp.concatenate([uw[:, HEAD_DIM:], qh * eg], axis=0), s, HI)
            v_new = uw[:, :HEAD_DIM] - ws_qs[:CHUNK]
            o = ws_qs[CHUNK:] + _mm(attn, v_new, HI)
            new_states.append(s * jnp.exp(gt[0:1, :]) + _mm_tn(kh * jnp.exp(gt - gc), v_new, HI))
            if backward:
                ob_s[rows, hs] = o
            else:
                of_s[rows, hs] = o
        return tuple(new_states)

    fin = lax.fori_loop(0, n_chunks, chunk_step, tuple(s0_ref[0, i] for i in range(N_GATE)))
    for i in range(N_GATE):
        sfin_ref[0, i] = fin[i]

    o = of_s[...] + ob_s[...]
    ms = _mm(o * o, head_sum, HI) * (1.0 / HEAD_DIM)
    o_ref[...] = o * lax.rsqrt(ms + EPS) * nw_ref[...] * _silu(a[:, 3 * W_GROUP:])


def _gdn(a_all, gate_all, conv_w, a_log, dt_bias, norm_w, s0, t, n_seq, first_block):
    small = lambda: pl.BlockSpec((1, LANES), lambda i: (0, 0))
    return pl.pallas_call(
        functools.partial(_gdn_kernel, t=t),
        grid=(n_seq,),
        in_specs=[pl.BlockSpec((t, 4 * W_GROUP), lambda i: (i + first_block, 0)),
                  pl.BlockSpec((t, LANES), lambda i: (i + first_block, 0)),
                  pl.BlockSpec((SUBLANES, 3 * W_GROUP), lambda i: (0, 0)),
                  small(), small(),
                  pl.BlockSpec((1, W_GROUP), lambda i: (0, 0)),
                  pl.BlockSpec((1, N_GATE, HEAD_DIM, HEAD_DIM), lambda i: (i, 0, 0, 0))],
        out_specs=[pl.BlockSpec((t, W_GROUP), lambda i: (i, 0)),
                   pl.BlockSpec((1, N_GATE, HEAD_DIM, HEAD_DIM), lambda i: (i, 0, 0, 0))],
        out_shape=[jax.ShapeDtypeStruct((n_seq * t, W_GROUP), F32),
                   jax.ShapeDtypeStruct((n_seq, N_GATE, HEAD_DIM, HEAD_DIM), F32)],
        scratch_shapes=[pltpu.VMEM((t, W_GROUP), F32)] * 3 + [pltpu.VMEM((t, LANES), F32)] * 3
                       + [pltpu.VMEM((t, W_GROUP), F32)] * 2,
        compiler_params=_cparams("arbitrary"),
        name="gdn",
    )(a_all, gate_all, conv_w, a_log, dt_bias, norm_w, s0)


def _swap16(x):
    width = x.shape[-1]
    first = (_iota(x.shape, 1) // 16) % 2 == 0
    return jnp.where(first, pltpu.roll(x, width - 16, 1), pltpu.roll(x, 16, 1))


def _ret_kernel(r_ref, lg_ref, s0_ref, cos_ref, sin_ref, o_ref, sfin_ref, *, t, latent):
    r = r_ref[...]
    q = r[:, :W_GROUP]
    k = r[:, W_GROUP:2 * W_GROUP]
    v = r[:, 2 * W_GROUP:3 * W_GROUP]
    if latent:
        q = q * cos_ref[...] + _swap16(q) * sin_ref[...]
        k = k * cos_ref[...] + _swap16(k) * sin_ref[...]
    k = k * (HEAD_DIM ** -0.5)
    logit = lg_ref[...]
    lg = -_softplus(-logit)
    pos = _iota((t, 1), 0).astype(F32)
    for h in range(N_HEADS):
        hs = slice(h * HEAD_DIM, (h + 1) * HEAD_DIM)
        qh, kh, vh = q[:, hs], k[:, hs], v[:, hs]
        lgf = lg[:, h:h + 1]
        lgb = lg[:, N_HEADS + h:N_HEADS + h + 1]
        for qt in range(t // Q_TILE):
            rows = slice(qt * Q_TILE, (qt + 1) * Q_TILE)
            diff = (_iota((Q_TILE, t), 0) + qt * Q_TILE - _iota((Q_TILE, t), 1)).astype(F32)
            dmat = (jnp.exp(jnp.where(diff >= 0, diff * lgf, NEG_BIG))
                    + jnp.exp(jnp.where(diff <= 0, -diff * lgb, NEG_BIG)))
            o = _mm(_mm_nt(qh[rows], kh, HI) * dmat, vh, HI)
            if latent:
                p = pos[rows]
                o = o + jnp.exp((p + 1.0) * lgf) * _mm(qh[rows], s0_ref[0, h], HI)
                o = o + jnp.exp((t - p) * lgb) * _mm(qh[rows], s0_ref[0, N_HEADS + h], HI)
            mu = jnp.mean(o, axis=-1, keepdims=True)
            oc = o - mu
            on = oc * lax.rsqrt(jnp.mean(oc * oc, axis=-1, keepdims=True) + EPS)
            o_ref[rows, hs] = on * _silu(r[rows, 3 * W_GROUP + h * HEAD_DIM:3 * W_GROUP + (h + 1) * HEAD_DIM])
        sf = _mm_tn(kh * jnp.exp((t - 1.0 - pos) * lgf), vh, HI)
        sb = _mm_tn(kh * jnp.exp(pos * lgb), vh, HI)
        if latent:
            sf = sf + jnp.exp(t * lgf) * s0_ref[0, h]
            sb = sb + jnp.exp(t * lgb) * s0_ref[0, N_HEADS + h]
        sfin_ref[0, h] = sf
        sfin_ref[0, N_HEADS + h] = sb


def _ret(r_all, logit, s0, cos, sin, t, n_seq, first_block, latent):
    return pl.pallas_call(
        functools.partial(_ret_kernel, t=t, latent=latent),
        grid=(n_seq,),
        in_specs=[pl.BlockSpec((t, 4 * W_GROUP), lambda i: (i + first_block, 0)),
                  pl.BlockSpec((1, LANES), lambda i: (0, 0)),
                  pl.BlockSpec((1, N_GATE, HEAD_DIM, HEAD_DIM), lambda i: (i, 0, 0, 0)),
                  pl.BlockSpec((t, W_GROUP), lambda i: (0, 0)),
                  pl.BlockSpec((t, W_GROUP), lambda i: (0, 0))],
        out_specs=[pl.BlockSpec((t, W_GROUP), lambda i: (i, 0)),
                   pl.BlockSpec((1, N_GATE, HEAD_DIM, HEAD_DIM), lambda i: (i, 0, 0, 0))],
        out_shape=[jax.ShapeDtypeStruct((n_seq * t, W_GROUP), F32),
                   jax.ShapeDtypeStruct((n_seq, N_GATE, HEAD_DIM, HEAD_DIM), F32)],
        compiler_params=_cparams("arbitrary"),
        name="ret",
    )(r_all, logit, s0, cos, sin)


def _rope_tables(t):
    pos = np.arange(t)
    row = (pos // GRID_W).astype(np.float32)
    col = (pos % GRID_W).astype(np.float32)
    nf = HEAD_DIM // 4
    inv_freq = jnp.power(ROPE_BASE, -jnp.arange(nf, dtype=F32) / nf)
    ang_r = jnp.asarray(row)[:, None] * inv_freq[None, :]
    ang_c = jnp.asarray(col)[:, None] * inv_freq[None, :]
    cos = jnp.concatenate([jnp.cos(ang_r)] * 2 + [jnp.cos(ang_c)] * 2, axis=1)
    sin = jnp.concatenate([-jnp.sin(ang_r), jnp.sin(ang_r), -jnp.sin(ang_c), jnp.sin(ang_c)], axis=1)
    return jnp.tile(cos, (1, N_HEADS)), jnp.tile(sin, (1, N_HEADS))


def _head_rms(x, w):
    return x * lax.rsqrt(jnp.mean(x * x, axis=-1, keepdims=True) + EPS) * w


def _ctx_attn_kernel(n_ref, qw_ref, kw_ref, o_ref, k_out, v_out):
    n = n_ref[...]
    v_out[0] = n[:, 2 * W_GROUP:]
    for h in range(N_HEADS):
        hs = slice(h * HEAD_DIM, (h + 1) * HEAD_DIM)
        qh = _head_rms(n[:, hs], qw_ref[...])
        kh = _head_rms(n[:, W_GROUP + h * HEAD_DIM:W_GROUP + (h + 1) * HEAD_DIM], kw_ref[...])
        vh = n[:, 2 * W_GROUP + h * HEAD_DIM:2 * W_GROUP + (h + 1) * HEAD_DIM]
        k_out[0, :, hs] = kh
        s = _mm_nt(qh, kh, HI) * (HEAD_DIM ** -0.5)
        p = jnp.exp(s - jnp.max(s, axis=-1, keepdims=True))
        o_ref[:, hs] = _mm(p, vh, HI) / jnp.sum(p, axis=-1, keepdims=True)


def _ctx_attn(n_all, qw, kw):
    return pl.pallas_call(
        _ctx_attn_kernel,
        grid=(BATCH,),
        in_specs=[pl.BlockSpec((SEQ, 3 * W_GROUP), lambda i: (i, 0)),
                  pl.BlockSpec((1, HEAD_DIM), lambda i: (0, 0)),
                  pl.BlockSpec((1, HEAD_DIM), lambda i: (0, 0))],
        out_specs=[pl.BlockSpec((SEQ, W_GROUP), lambda i: (i, 0)),
                   pl.BlockSpec((1, SEQ, W_GROUP), lambda i: (i, 0, 0)),
                   pl.BlockSpec((1, SEQ, W_GROUP), lambda i: (i, 0, 0))],
        out_shape=[jax.ShapeDtypeStruct((N_CTX, W_GROUP), F32),
                   jax.ShapeDtypeStruct((BATCH, SEQ, W_GROUP), F32),
                   jax.ShapeDtypeStruct((BATCH, SEQ, W_GROUP), F32)],
        compiler_params=_cparams("arbitrary"),
        name="ctx_attn",
    )(n_all, qw, kw)


def _nat_kernel(n_ref, ck_ref, cv_ref, bias_ref, qw_ref, kw_ref, o_ref):
    h = pl.program_id(1)
    n = n_ref[...]
    sel = (_iota((W_GROUP, HEAD_DIM), 0) == _iota((W_GROUP, HEAD_DIM), 1) + h * HEAD_DIM).astype(F32)
    qh = _head_rms(_mm(n[:, :W_GROUP], sel, HI), qw_ref[...]) * (HEAD_DIM ** -0.5)
    kh = _head_rms(_mm(n[:, W_GROUP:2 * W_GROUP], sel, HI), kw_ref[...])
    vh = _mm(n[:, 2 * W_GROUP:], sel, HI)
    ckh = _mm(ck_ref[0], sel, HI)
    cvh = _mm(cv_ref[0], sel, HI)
    for qt in range(DEC_SEQ // Q_TILE):
        rows = slice(qt * Q_TILE, (qt + 1) * Q_TILE)
        s_loc = _mm_nt(qh[rows], kh, HI) + bias_ref[0, rows, :]
        s_ctx = _mm_nt(qh[rows], ckh, HI)
        m = jnp.maximum(jnp.max(s_loc, axis=-1, keepdims=True), jnp.max(s_ctx, axis=-1, keepdims=True))
        p_loc = jnp.exp(s_loc - m)
        p_ctx = jnp.exp(s_ctx - m)
        den = jnp.sum(p_loc, axis=-1, keepdims=True) + jnp.sum(p_ctx, axis=-1, keepdims=True)
        o_ref[0, rows, :] = (_mm(p_loc, vh, HI) + _mm(p_ctx, cvh, HI)) / den


def _nat(n_all, ck, cv, bias, qw, kw):
    first_block = N_CTX // DEC_SEQ
    return pl.pallas_call(
        _nat_kernel,
        grid=(DEC_BATCH, N_HEADS),
        in_specs=[pl.BlockSpec((DEC_SEQ, 3 * W_GROUP), lambda b, h: (b + first_block, 0)),
                  pl.BlockSpec((1, PAST_LEN, W_GROUP), lambda b, h: (b, 0, 0)),
                  pl.BlockSpec((1, PAST_LEN, W_GROUP), lambda b, h: (b, 0, 0)),
                  pl.BlockSpec((1, DEC_SEQ, DEC_SEQ), lambda b, h: (h, 0, 0)),
                  pl.BlockSpec((1, HEAD_DIM), lambda b, h: (0, 0)),
                  pl.BlockSpec((1, HEAD_DIM), lambda b, h: (0, 0))],
        out_specs=pl.BlockSpec((1, DEC_SEQ, HEAD_DIM), lambda b, h: (b * N_HEADS + h, 0, 0)),
        out_shape=jax.ShapeDtypeStruct((DEC_BATCH * N_HEADS, DEC_SEQ, HEAD_DIM), F32),
        compiler_params=_cparams("arbitrary", "arbitrary"),
        name="nat",
    )(n_all, ck, cv, bias, qw, kw)


def _nat_bias(rpb):
    rows_n = DEC_SEQ // GRID_W
    kh = min(WIN_ROWS, rows_n)
    tok = np.arange(DEC_SEQ)
    r, c = tok // GRID_W, tok % GRID_W
    r0 = np.clip(r - kh // 2, 0, rows_n - kh)
    c0 = np.clip(c - WIN_COLS // 2, 0, GRID_W - WIN_COLS)
    rq, rk = r[:, None], r[None, :]
    cq, ck = c[:, None], c[None, :]
    inside = (rk >= r0[:, None]) & (rk < r0[:, None] + kh) & (ck >= c0[:, None]) & (ck < c0[:, None] + WIN_COLS)
    row_idx = np.clip(rk - rq + WIN_ROWS - 1, 0, 2 * WIN_ROWS - 2)
    col_idx = np.clip(ck - cq, -(WIN_COLS - 1), WIN_COLS - 1) + WIN_COLS - 1
    return jnp.where(jnp.asarray(inside)[None], rpb[:, row_idx, col_idx].astype(F32), NEG_BIG)


ROUTE_W = D_MODEL + LANES
GSEL_LANE = N_EXPERTS


def _pack_router(we, be, wg, bg):
    pad = LANES - N_EXPERTS - N_GROUPS
    w = jnp.concatenate([we, wg, jnp.zeros((D_MODEL, pad), F32)], axis=1)
    b = jnp.concatenate([be, bg, jnp.zeros((pad,), F32)]).reshape(1, LANES)
    return w, b


def _lane_min_where(mask, lane):
    return jnp.min(jnp.where(mask, lane, LANES), axis=-1, keepdims=True)


def _outproj_kernel(x_ref, m0, m1, m2, m3, mod_ref, nw_ref, w_ref, rw_ref, rb_ref, x_out, route_out):
    acc = None
    for i, m_ref in enumerate((m0, m1, m2, m3)):
        part = _mm(m_ref[...].astype(BF16), w_ref[i * W_GROUP:(i + 1) * W_GROUP, :])
        acc = part if acc is None else acc + part
    x = x_ref[...] + mod_ref[0, 2:3, :] * acc
    x_out[...] = x
    y = x * lax.rsqrt(jnp.mean(x * x, axis=-1, keepdims=True) + EPS) * nw_ref[...]
    hf = y * (1.0 + mod_ref[0, 4:5, :]) + mod_ref[0, 3:4, :]
    route_out[:, :D_MODEL] = hf

    logits = _mm(hf, rw_ref[...], HI) + rb_ref[...]
    lane = _iota(logits.shape, 1)
    is_g = (lane >= N_EXPERTS) & (lane < N_EXPERTS + N_GROUPS)
    gl = jnp.where(is_g, logits, NEG_BIG)
    ge = jnp.exp(gl - jnp.max(gl, axis=-1, keepdims=True))
    gp = jnp.where(is_g, ge / jnp.sum(ge, axis=-1, keepdims=True), -1.0)
    gw = jnp.max(gp, axis=-1, keepdims=True)
    gsel = _lane_min_where(gp == gw, lane) - N_EXPERTS
    in_grp = (lane // EXPERTS_PER_GROUP == gsel) & (lane < N_EXPERTS)
    el = jnp.where(in_grp, logits, NEG_BIG)
    ee = jnp.exp(el - jnp.max(el, axis=-1, keepdims=True))
    ep = jnp.where(in_grp, ee / jnp.sum(ee, axis=-1, keepdims=True), -1.0)
    t1 = jnp.max(ep, axis=-1, keepdims=True)
    i1 = _lane_min_where(ep == t1, lane)
    ep2 = jnp.where(lane == i1, -1.0, ep)
    t2 = jnp.max(ep2, axis=-1, keepdims=True)
    i2 = _lane_min_where(ep2 == t2, lane)
    tsum = t1 + t2
    combine = jnp.where(lane == i1, gw * (t1 / tsum), 0.0) + jnp.where(lane == i2, gw * (t2 / tsum), 0.0)
    route_out[:, D_MODEL:] = jnp.where(lane == GSEL_LANE, gsel.astype(F32), combine)


def _outproj(x, mixed, mod, norm_w, w_out, rw, rb):
    tile = lambda w: pl.BlockSpec((TOK_TILE, w), lambda i: (i, 0))
    whole = lambda a: pl.BlockSpec(a.shape, lambda i: (0,) * a.ndim)
    return pl.pallas_call(
        _outproj_kernel,
        grid=(N_TOK // TOK_TILE,),
        in_specs=[tile(D_MODEL)] + [tile(W_GROUP)] * 4
                 + [pl.BlockSpec((1, SUBLANES, D_MODEL), lambda i: (_cond_of_tile(i), 0, 0)),
                    pl.BlockSpec((1, D_MODEL), lambda i: (0, 0)), whole(w_out), whole(rw), whole(rb)],
        out_specs=[tile(D_MODEL), tile(ROUTE_W)],
        out_shape=[jax.ShapeDtypeStruct((N_TOK, D_MODEL), F32), jax.ShapeDtypeStruct((N_TOK, ROUTE_W), F32)],
        compiler_params=_cparams("arbitrary"),
        name="outproj",
    )(x, *mixed, mod, norm_w.reshape(1, D_MODEL), w_out, rw, rb)


def _moe_kernel(tile_group, row_token, route_hbm, wg_ref, wu_ref, wd_ref, y_hbm, xbuf, ybuf, sem_in, sem_out):
    i = pl.program_id(0)
    g = tile_group[i]
    base = i * MOE_TILE

    def gather_copy(r):
        tok = jnp.maximum(row_token[base + r], 0)
        return pltpu.make_async_copy(route_hbm.at[pl.ds(tok, 1)], xbuf.at[pl.ds(r, 1)], sem_in)

    def scatter_copy(r):
        return pltpu.make_async_copy(ybuf.at[pl.ds(r, 1)], y_hbm.at[pl.ds(row_token[base + r], 1)], sem_out)

    @pl.when(row_token[base] >= 0)
    def _():
        @pl.loop(0, MOE_TILE)
        def _(r):
            gather_copy(r).start()

        @pl.loop(0, MOE_TILE)
        def _(r):
            gather_copy(r).wait()

        x = xbuf[:, :D_MODEL].astype(BF16)
        route = xbuf[:, D_MODEL:]
        lane = _iota(route.shape, 1)
        acc = jnp.zeros((MOE_TILE, D_MODEL), F32)
        for e in range(EXPERTS_PER_GROUP):
            cols = slice(e * EXPERT_FF, (e + 1) * EXPERT_FF)
            cw = jnp.sum(jnp.where(lane == g * EXPERTS_PER_GROUP + e, route, 0.0), axis=-1, keepdims=True)
            act = _silu(_mm(x, wg_ref[0, :, cols])) * _mm(x, wu_ref[0, :, cols]) * cw
            acc = acc + _mm(act.astype(BF16), wd_ref[0, cols, :])
        ybuf[...] = acc

        @pl.loop(0, MOE_TILE)
        def _(r):
            @pl.when(row_token[base + r] >= 0)
            def _():
                scatter_copy(r).start()

        @pl.loop(0, MOE_TILE)
        def _(r):
            @pl.when(row_token[base + r] >= 0)
            def _():
                scatter_copy(r).wait()


def _moe(route, tile_group, row_token, wg, wu, wd):
    n_tiles = MOE_ROWS // MOE_TILE
    grid_spec = pltpu.PrefetchScalarGridSpec(
        num_scalar_prefetch=2,
        grid=(n_tiles,),
        in_specs=[pl.BlockSpec(memory_space=pl.ANY),
                  pl.BlockSpec((1, D_MODEL, GROUP_FF), lambda i, tg, rt: (tg[i], 0, 0)),
                  pl.BlockSpec((1, D_MODEL, GROUP_FF), lambda i, tg, rt: (tg[i], 0, 0)),
                  pl.BlockSpec((1, GROUP_FF, D_MODEL), lambda i, tg, rt: (tg[i], 0, 0))],
        out_specs=pl.BlockSpec(memory_space=pl.ANY),
        scratch_shapes=[pltpu.VMEM((MOE_TILE, ROUTE_W), F32), pltpu.VMEM((MOE_TILE, D_MODEL), F32),
                        pltpu.SemaphoreType.DMA, pltpu.SemaphoreType.DMA],
    )
    return pl.pallas_call(
        _moe_kernel,
        grid_spec=grid_spec,
        out_shape=jax.ShapeDtypeStruct((N_TOK, D_MODEL), F32),
        compiler_params=_cparams("arbitrary"),
        name="moe",
    )(tile_group, row_token, route, wg, wu, wd)


def _moe_schedule(gsel):
    onehot = (gsel[:, None] == jnp.arange(N_GROUPS)[None, :]).astype(jnp.int32)
    rank = jnp.take_along_axis(jnp.cumsum(onehot, axis=0) - onehot, gsel[:, None], axis=1)[:, 0]
    counts = jnp.sum(onehot, axis=0)
    tiles = (counts + MOE_TILE - 1) // MOE_TILE
    tile_start = jnp.cumsum(tiles) - tiles
    dest = tile_start[gsel] * MOE_TILE + rank
    row_token = jnp.full((MOE_ROWS,), -1, jnp.int32).at[dest].set(jnp.arange(N_TOK, dtype=jnp.int32))
    tile_idx = jnp.arange(MOE_ROWS // MOE_TILE)
    tile_group = jnp.clip(jnp.sum(tile_idx[:, None] >= tile_start[None, :], axis=1) - 1, 0, N_GROUPS - 1)
    return tile_group.astype(jnp.int32), row_token


def _resid_kernel(x_ref, y_ref, mod_ref, o_ref):
    o_ref[...] = x_ref[...] + mod_ref[0, 5:6, :] * y_ref[...]


def _resid(x, y, mod):
    tile = pl.BlockSpec((TOK_TILE, D_MODEL), lambda i: (i, 0))
    return pl.pallas_call(
        _resid_kernel,
        grid=(N_TOK // TOK_TILE,),
        in_specs=[tile, tile, pl.BlockSpec((1, SUBLANES, D_MODEL), lambda i: (_cond_of_tile(i), 0, 0))],
        out_specs=tile,
        out_shape=jax.ShapeDtypeStruct((N_TOK, D_MODEL), F32),
        compiler_params=_cparams("arbitrary"),
        name="resid",
    )(x, y, mod)


def _lane_row(v):
    v = v.reshape(-1).astype(F32)
    return jnp.concatenate([v, jnp.zeros((LANES - v.shape[0],), F32)]).reshape(1, LANES)


def _pad_rows(w):
    return jnp.concatenate([w, jnp.zeros((SUBLANES - w.shape[0], w.shape[1]), w.dtype)], axis=0)


def kernel(x_prompt, x_sample, state_gdn, state_ret, cache_nat_k, cache_nat_v, c, c_ctx, ada_w, ada_b, norm_mix_w, norm_ffn_w, w_in, gdn_conv_w, gdn_a_log, gdn_dt_bias, gdn_norm_w, ret_gamma_logit, nat_q_norm_w, nat_k_norm_w, nat_rpb, sc_conv_w, w_out, router_group_w, router_group_b, router_expert_w, router_expert_b, moe_w_gate, moe_w_up, moe_w_down):
    x = jnp.concatenate([x_prompt.reshape(N_CTX, D_MODEL), x_sample.reshape(N_LAT, D_MODEL)], axis=0)
    cond = jnp.concatenate([c_ctx[None, :], c, jnp.zeros((SUBLANES - 1 - DEC_BATCH, D_MODEL), F32)], axis=0)
    ada = _ada(cond, ada_w, ada_b).reshape(DEPTH, SUBLANES, 6, D_MODEL)
    cos, sin = _rope_tables(DEC_SEQ)
    zero_state = jnp.zeros((BATCH, N_GATE, HEAD_DIM, HEAD_DIM), F32)
    lat_block = N_CTX // DEC_SEQ
    gdn_list, ret_list, k_list, v_list = [], [], [], []
    for l in range(DEPTH):
        mod = jnp.concatenate([ada[l, :1 + DEC_BATCH], jnp.zeros((1 + DEC_BATCH, SUBLANES - 6, D_MODEL), F32)], axis=1)
        a_gdn, a_ret, a_nat, a_sc, a_gate = _inproj(x, mod, norm_mix_w[l], _pack_w_in(w_in[l]))

        conv_w = _pad_rows(gdn_conv_w[l])
        a_log, dt_b = _lane_row(gdn_a_log[l]), _lane_row(gdn_dt_bias[l])
        gnw = jnp.tile(gdn_norm_w[l], N_HEADS).reshape(1, W_GROUP)
        o_gdn_c, s_gdn = _gdn(a_gdn, a_gate, conv_w, a_log, dt_b, gnw, zero_state, SEQ, BATCH, 0)
        s0 = state_gdn[:, l].reshape(DEC_BATCH, N_GATE, HEAD_DIM, HEAD_DIM)
        o_gdn_l, _ = _gdn(a_gdn, a_gate, conv_w, a_log, dt_b, gnw, s0, DEC_SEQ, DEC_BATCH, lat_block)

        logit = _lane_row(ret_gamma_logit[l])
        o_ret_c, s_ret = _ret(a_ret, logit, zero_state, cos[:SEQ], sin[:SEQ], SEQ, BATCH, 0, False)
        s0 = state_ret[:, l].reshape(DEC_BATCH, N_GATE, HEAD_DIM, HEAD_DIM)
        o_ret_l, _ = _ret(a_ret, logit, s0, cos, sin, DEC_SEQ, DEC_BATCH, lat_block, True)

        qw, kw = nat_q_norm_w[l].reshape(1, HEAD_DIM), nat_k_norm_w[l].reshape(1, HEAD_DIM)
        o_nat_c, n_k, n_v = _ctx_attn(a_nat, qw, kw)
        o_nat_l = _nat(a_nat, cache_nat_k[:, l].reshape(DEC_BATCH, PAST_LEN, W_GROUP),
                       cache_nat_v[:, l].reshape(DEC_BATCH, PAST_LEN, W_GROUP), _nat_bias(nat_rpb[l]), qw, kw)
        o_nat_l = o_nat_l.reshape(DEC_BATCH, N_HEADS, DEC_SEQ, HEAD_DIM).transpose(0, 2, 1, 3).reshape(N_LAT, W_GROUP)

        sc_w = _pad_rows(sc_conv_w[l])
        o_sc_c = _sconv(a_sc, sc_w, SEQ, BATCH, 0)
        o_sc_l = _sconv(a_sc, sc_w, DEC_SEQ, DEC_BATCH, lat_block)

        mixed = [jnp.concatenate(p, axis=0) for p in
                 ((o_gdn_c, o_gdn_l), (o_ret_c, o_ret_l), (o_nat_c, o_nat_l), (o_sc_c, o_sc_l))]
        rw, rb = _pack_router(router_expert_w[l], router_expert_b[l], router_group_w[l], router_group_b[l])
        x_mid, route = _outproj(x, mixed, mod, norm_ffn_w[l], w_out[l].astype(BF16), rw, rb)

        tile_group, row_token = _moe_schedule(route[:, D_MODEL + GSEL_LANE].astype(jnp.int32))
        to_group = lambda w: (w.reshape(N_GROUPS, EXPERTS_PER_GROUP, D_MODEL, EXPERT_FF)
                              .transpose(0, 2, 1, 3).reshape(N_GROUPS, D_MODEL, GROUP_FF).astype(BF16))
        y = _moe(route, tile_group, row_token, to_group(moe_w_gate[l]), to_group(moe_w_up[l]),
                 moe_w_down[l].reshape(N_GROUPS, GROUP_FF, D_MODEL).astype(BF16))
        x = _resid(x_mid, y, mod)

        gdn_list.append(s_gdn.reshape(BATCH, 2, N_HEADS, HEAD_DIM, HEAD_DIM))
        ret_list.append(s_ret.reshape(BATCH, 2, N_HEADS, HEAD_DIM, HEAD_DIM))
        k_list.append(n_k.reshape(BATCH, SEQ, N_HEADS, HEAD_DIM))
        v_list.append(n_v.reshape(BATCH, SEQ, N_HEADS, HEAD_DIM))
    return (x[:N_CTX].reshape(BATCH, SEQ, D_MODEL), x[N_CTX:].reshape(DEC_BATCH, DEC_SEQ, D_MODEL),
            jnp.stack(gdn_list, axis=1), jnp.stack(ret_list, axis=1),
            jnp.stack(k_list, axis=1), jnp.stack(v_list, axis=1))
```

```python
import functools

import numpy as np
import jax
import jax.numpy as jnp
from jax import lax
from jax.experimental import pallas as pl
from jax.experimental.pallas import tpu as pltpu

D_MODEL = 1024
BATCH = 16
SEQ = 256
DEPTH = 2
DEC_BATCH = 2
DEC_SEQ = 1024
PAST_LEN = 256
GRID_W = 64
HEAD_DIM = 64
W_GROUP = D_MODEL // 4
N_HEADS = W_GROUP // HEAD_DIM
CHUNK = 64
WIN_ROWS = 8
WIN_COLS = 16
ROPE_BASE = 10000.0
N_GROUPS = 4
EXPERTS_PER_GROUP = 8
N_EXPERTS = N_GROUPS * EXPERTS_PER_GROUP
EXPERT_FF = 256
GROUP_FF = EXPERTS_PER_GROUP * EXPERT_FF
EPS = 1e-6

N_CTX = BATCH * SEQ
N_LAT = DEC_BATCH * DEC_SEQ
N_TOK = N_CTX + N_LAT
LANES = 128
SUBLANES = 8
TOK_TILE = 512
MOE_TILE = 256
MOE_ROWS = N_TOK + N_GROUPS * MOE_TILE
Q_TILE = 256
VMEM_LIMIT = 48 * 1024 * 1024
NEG_BIG = -1e30
N_GATE = 2 * N_HEADS

F32 = jnp.float32
BF16 = jnp.bfloat16
HI = lax.Precision.HIGHEST


def _mm(a, b, prec=None):
    return lax.dot_general(a, b, (((1,), (0,)), ((), ())), precision=prec, preferred_element_type=F32)


def _mm_nt(a, b, prec=None):
    return lax.dot_general(a, b, (((1,), (1,)), ((), ())), precision=prec, preferred_element_type=F32)


def _mm_tn(a, b, prec=None):
    return lax.dot_general(a, b, (((0,), (0,)), ((), ())), precision=prec, preferred_element_type=F32)


def _bmm(a, b):
    return _mm(a.astype(BF16), b.astype(BF16))


def _sigmoid(x):
    return 1.0 / (1.0 + jnp.exp(-x))


def _silu(x):
    return x * _sigmoid(x)


def _softplus(x):
    return jnp.maximum(x, 0.0) + jnp.log(1.0 + jnp.exp(-jnp.abs(x)))


def _iota(shape, dim):
    return lax.broadcasted_iota(jnp.int32, shape, dim)


def _cparams(*sem):
    return pltpu.CompilerParams(dimension_semantics=sem, vmem_limit_bytes=VMEM_LIMIT)


def _cond_of_tile(i):
    n_ctx_tiles = N_CTX // TOK_TILE
    return jnp.where(i < n_ctx_tiles, 0, 1 + (i - n_ctx_tiles) // (DEC_SEQ // TOK_TILE))


def _head_sum_matrix():
    return (_iota((W_GROUP, W_GROUP), 0) // HEAD_DIM == _iota((W_GROUP, W_GROUP), 1) // HEAD_DIM).astype(F32)


def _ada_kernel(c_ref, w_ref, b_ref, o_ref):
    o_ref[0] = _mm(_silu(c_ref[...]), w_ref[0], HI) + b_ref[0]


def _ada(cond, ada_w, ada_b):
    tn = 1536
    n_out = 6 * D_MODEL
    return pl.pallas_call(
        _ada_kernel,
        grid=(DEPTH, n_out // tn),
        in_specs=[pl.BlockSpec((SUBLANES, D_MODEL), lambda l, j: (0, 0)),
                  pl.BlockSpec((1, D_MODEL, tn), lambda l, j: (l, 0, j)),
                  pl.BlockSpec((1, 1, tn), lambda l, j: (l, 0, j))],
        out_specs=pl.BlockSpec((1, SUBLANES, tn), lambda l, j: (l, 0, j)),
        out_shape=jax.ShapeDtypeStruct((DEPTH, SUBLANES, n_out), F32),
        compiler_params=_cparams("arbitrary", "arbitrary"),
        name="ada",
    )(cond, ada_w, ada_b.reshape(DEPTH, 1, n_out))


IN_WIDTHS = (4 * W_GROUP, 4 * W_GROUP, 3 * W_GROUP, 3 * W_GROUP, LANES)
IN_PACKED = sum(IN_WIDTHS)


def _pack_w_in(w):
    sizes = (3 * W_GROUP, W_GROUP, N_GATE, N_GATE, 3 * W_GROUP, W_GROUP, 3 * W_GROUP, 3 * W_GROUP)
    offs = np.concatenate([[0], np.cumsum(sizes)])
    seg = [w[:, offs[i]:offs[i + 1]] for i in range(len(sizes))]
    gate = jnp.concatenate([seg[2], seg[3], jnp.zeros((D_MODEL, LANES - 2 * N_GATE), w.dtype)], axis=1)
    return jnp.concatenate([seg[0], seg[1], seg[4], seg[5], seg[6], seg[7], gate], axis=1).astype(BF16)


def _inproj_kernel(x_ref, mod_ref, nw_ref, w_ref, *o_refs):
    x = x_ref[...]
    y = x * lax.rsqrt(jnp.mean(x * x, axis=-1, keepdims=True) + EPS) * nw_ref[...]
    h = (y * (1.0 + mod_ref[0, 1:2, :]) + mod_ref[0, 0:1, :]).astype(BF16)
    off = 0
    for o_ref, width in zip(o_refs, IN_WIDTHS):
        o_ref[...] = _mm(h, w_ref[:, off:off + width])
        off += width


def _inproj(x, mod, norm_w, w_packed):
    return pl.pallas_call(
        _inproj_kernel,
        grid=(N_TOK // TOK_TILE,),
        in_specs=[pl.BlockSpec((TOK_TILE, D_MODEL), lambda i: (i, 0)),
                  pl.BlockSpec((1, SUBLANES, D_MODEL), lambda i: (_cond_of_tile(i), 0, 0)),
                  pl.BlockSpec((1, D_MODEL), lambda i: (0, 0)),
                  pl.BlockSpec((D_MODEL, IN_PACKED), lambda i: (0, 0))],
        out_specs=[pl.BlockSpec((TOK_TILE, w), lambda i: (i, 0)) for w in IN_WIDTHS],
        out_shape=[jax.ShapeDtypeStruct((N_TOK, w), F32) for w in IN_WIDTHS],
        compiler_params=_cparams("arbitrary"),
        name="inproj",
    )(x, mod, norm_w.reshape(1, D_MODEL), w_packed)


def _shift_rows(p, t):
    row = _iota(p.shape, 0)
    prev = jnp.where(row == 0, 0.0, pltpu.roll(p, 1, 0))
    nxt = jnp.where(row == t - 1, 0.0, pltpu.roll(p, t - 1, 0))
    return prev, nxt


def _conv3(x, w_ref, t):
    prev, nxt = _shift_rows(x, t)
    return w_ref[0:1, :] * prev + w_ref[1:2, :] * x + w_ref[2:3, :] * nxt


def _sconv_kernel(s_ref, w_ref, o_ref, *, t):
    s = s_ref[...]
    p = s[:, W_GROUP:2 * W_GROUP] * s[:, 2 * W_GROUP:]
    o_ref[...] = s[:, :W_GROUP] * _conv3(p, w_ref, t)


def _sconv(s_all, w, t, n_seq, first_block):
    return pl.pallas_call(
        functools.partial(_sconv_kernel, t=t),
        grid=(n_seq,),
        in_specs=[pl.BlockSpec((t, 3 * W_GROUP), lambda i: (i + first_block, 0)),
                  pl.BlockSpec((SUBLANES, W_GROUP), lambda i: (0, 0))],
        out_specs=pl.BlockSpec((t, W_GROUP), lambda i: (i, 0)),
        out_shape=jax.ShapeDtypeStruct((n_seq * t, W_GROUP), F32),
        compiler_params=_cparams("arbitrary"),
        name="sconv",
    )(s_all, w)


def _unit_tri_inverse(low, blk_mask, eye):
    d = jnp.where(blk_mask, low, 0.0)
    o = low - d
    d2 = _bmm(d, d)
    d4 = _bmm(d2, d2)
    d8 = _bmm(d4, d4)
    p = eye - d + d2 - _bmm(d, d2)
    p = p + _bmm(p, d4)
    p = p + _bmm(p, d8)
    n = _bmm(p, o)
    n2 = _bmm(n, n)
    r = eye - n + n2 - _bmm(n, n2)
    return _bmm(r, p)


def _gdn_kernel(a_ref, gate_ref, convw_ref, alog_ref, dtb_ref, nw_ref, s0_ref, o_ref, sfin_ref,
                q_s, k_s, v_s, gc_s, eg_s, ekd_s, egt_s, beta_s, gcrow_s, uw_s, attn_s, of_s, ob_s, *, t):
    n_chunks = t // CHUNK
    a = a_ref[...]
    qkv = _silu(_conv3(a[:, :3 * W_GROUP], convw_ref, t))
    q = qkv[:, :W_GROUP]
    k = qkv[:, W_GROUP:2 * W_GROUP]
    head_sum = _head_sum_matrix()
    q_s[...] = q * lax.rsqrt(_mm(q * q, head_sum, HI) + EPS) * (HEAD_DIM ** -0.5)
    k_s[...] = k * lax.rsqrt(_mm(k * k, head_sum, HI) + EPS)
    v_s[...] = qkv[:, 2 * W_GROUP:]

    gates = gate_ref[...]
    log_a = -jnp.exp(alog_ref[...]) * _softplus(gates + dtb_ref[...])
    beta_s[...] = _sigmoid(gates)

    ci = _iota((CHUNK, CHUNK), 0)
    cj = _iota((CHUNK, CHUNK), 1)
    eye = (ci == cj).astype(F32)
    blk_mask = (ci // 16) == (cj // 16)

    tri_f = (cj <= ci).astype(F32)
    tri_b = (cj >= ci).astype(F32)
    ones = jnp.ones((CHUNK, CHUNK), F32)
    fwd_lane = _iota((CHUNK, LANES), 1) < N_HEADS
    gate_rows = (_iota((N_GATE, LANES), 0) == _iota((N_GATE, LANES), 1)).astype(F32)
    for c in range(n_chunks):
        rows = slice(c * CHUNK, (c + 1) * CHUNK)
        g = log_a[rows, :]
        gc = jnp.where(fwd_lane, _mm(tri_f, g, HI), _mm(tri_b, g, HI))
        gt = _mm(ones, g, HI)
        gc_s[rows, :] = gc
        eg_s[rows, :] = jnp.exp(gc)
        ekd_s[rows, :] = jnp.exp(gt - gc)
        egt_s[rows, :] = jnp.exp(gt)
        gcrow_s[c] = _mm_nt(gate_rows, gc, HI)

    def solve_chunk(c, carry):
        rows = pl.ds(pl.multiple_of(c * CHUNK, CHUNK), CHUNK)
        for a_idx in range(N_GATE):
            backward = a_idx >= N_HEADS
            hs = slice((a_idx % N_HEADS) * HEAD_DIM, (a_idx % N_HEADS + 1) * HEAD_DIM)
            incl = (cj >= ci) if backward else (cj <= ci)
            strict = (cj > ci) if backward else (cj < ci)
            qh = q_s[rows, hs]
            kh = k_s[rows, hs]
            bt = beta_s[rows, N_GATE + a_idx:N_GATE + a_idx + 1]
            decay = jnp.exp(jnp.where(incl, gc_s[rows, a_idx:a_idx + 1] - gcrow_s[c, a_idx:a_idx + 1, :], NEG_BIG))
            kb = kh * bt
            kk = _mm_nt(jnp.concatenate([kb, qh], axis=0).astype(BF16), kh.astype(BF16))
            low = jnp.where(strict, kk[:CHUNK] * decay, 0.0)
            attn_s[a_idx, rows, :] = kk[CHUNK:] * decay
            t_inv = _unit_tri_inverse(low, blk_mask, eye)
            rhs = jnp.concatenate([v_s[rows, hs] * bt, kb * eg_s[rows, a_idx:a_idx + 1]], axis=1)
            uw_s[a_idx, rows, :] = _bmm(t_inv, rhs)
        return carry

    lax.fori_loop(0, n_chunks, solve_chunk, 0)

    def scan_chunk(c, states):
        new_states = []
        for a_idx in range(N_GATE):
            backward = a_idx >= N_HEADS
            hs = slice((a_idx % N_HEADS) * HEAD_DIM, (a_idx % N_HEADS + 1) * HEAD_DIM)
            cc = (n_chunks - 1 - c) if backward else c
            rows = pl.ds(pl.multiple_of(cc * CHUNK, CHUNK), CHUNK)
            uw = uw_s[a_idx, rows, :]
            s = states[a_idx]
            q_dec = q_s[rows, hs] * eg_s[rows, a_idx:a_idx + 1]
            ws_qs = _bmm(jnp.concatenate([uw[:, HEAD_DIM:], q_dec], axis=0), s)
            v_new = uw[:, :HEAD_DIM] - ws_qs[:CHUNK]
            o = ws_qs[CHUNK:] + _bmm(attn_s[a_idx, rows, :], v_new)
            k_dec = k_s[rows, hs] * ekd_s[rows, a_idx:a_idx + 1]
            c_dec = egt_s[rows, a_idx:a_idx + 1][0:1, :]
            new_states.append(s * c_dec + _mm_tn(k_dec.astype(BF16), v_new.astype(BF16)))
            if backward:
                ob_s[rows, hs] = o
            else:
                of_s[rows, hs] = o
        return tuple(new_states)

    fin = lax.fori_loop(0, n_chunks, scan_chunk, tuple(s0_ref[0, i] for i in range(N_GATE)))
    for i in range(N_GATE):
        sfin_ref[0, i] = fin[i]

    o = of_s[...] + ob_s[...]
    ms = _mm(o * o, head_sum, HI) * (1.0 / HEAD_DIM)
    o_ref[...] = o * lax.rsqrt(ms + EPS) * nw_ref[...] * _silu(a[:, 3 * W_GROUP:])


def _gdn(a_all, gate_all, conv_w, a_log, dt_bias, norm_w, s0, t, n_seq, first_block):
    small = lambda: pl.BlockSpec((1, LANES), lambda i: (0, 0))
    return pl.pallas_call(
        functools.partial(_gdn_kernel, t=t),
        grid=(n_seq,),
        in_specs=[pl.BlockSpec((t, 4 * W_GROUP), lambda i: (i + first_block, 0)),
                  pl.BlockSpec((t, LANES), lambda i: (i + first_block, 0)),
                  pl.BlockSpec((SUBLANES, 3 * W_GROUP), lambda i: (0, 0)),
                  small(), small(),
                  pl.BlockSpec((1, W_GROUP), lambda i: (0, 0)),
                  pl.BlockSpec((1, N_GATE, HEAD_DIM, HEAD_DIM), lambda i: (i, 0, 0, 0))],
        out_specs=[pl.BlockSpec((t, W_GROUP), lambda i: (i, 0)),
                   pl.BlockSpec((1, N_GATE, HEAD_DIM, HEAD_DIM), lambda i: (i, 0, 0, 0))],
        out_shape=[jax.ShapeDtypeStruct((n_seq * t, W_GROUP), F32),
                   jax.ShapeDtypeStruct((n_seq, N_GATE, HEAD_DIM, HEAD_DIM), F32)],
        scratch_shapes=[pltpu.VMEM((t, W_GROUP), F32)] * 3 + [pltpu.VMEM((t, LANES), F32)] * 5
                       + [pltpu.VMEM((t // CHUNK, N_GATE, CHUNK), F32),
                          pltpu.VMEM((N_GATE, t, 2 * HEAD_DIM), F32), pltpu.VMEM((N_GATE, t, CHUNK), F32)]
                       + [pltpu.VMEM((t, W_GROUP), F32)] * 2,
        compiler_params=_cparams("arbitrary"),
        name="gdn",
    )(a_all, gate_all, conv_w, a_log, dt_bias, norm_w, s0)


def _swap16(x):
    width = x.shape[-1]
    first = (_iota(x.shape, 1) // 16) % 2 == 0
    return jnp.where(first, pltpu.roll(x, width - 16, 1), pltpu.roll(x, 16, 1))


def _ret_kernel(r_ref, lg_ref, s0_ref, cos_ref, sin_ref, o_ref, sfin_ref, *, t, latent):
    r = r_ref[...]
    q = r[:, :W_GROUP]
    k = r[:, W_GROUP:2 * W_GROUP]
    v = r[:, 2 * W_GROUP:3 * W_GROUP]
    if latent:
        q = q * cos_ref[...] + _swap16(q) * sin_ref[...]
        k = k * cos_ref[...] + _swap16(k) * sin_ref[...]
    k = k * (HEAD_DIM ** -0.5)
    logit = lg_ref[...]
    lg = -_softplus(-logit)
    pos = _iota((t, 1), 0).astype(F32)
    for h in range(N_HEADS):
        hs = slice(h * HEAD_DIM, (h + 1) * HEAD_DIM)
        qh, kh, vh = q[:, hs], k[:, hs], v[:, hs]
        lgf = lg[:, h:h + 1]
        lgb = lg[:, N_HEADS + h:N_HEADS + h + 1]
        for qt in range(t // Q_TILE):
            rows = slice(qt * Q_TILE, (qt + 1) * Q_TILE)
            diff = (_iota((Q_TILE, t), 0) + qt * Q_TILE - _iota((Q_TILE, t), 1)).astype(F32)
            dmat = (jnp.exp(jnp.where(diff >= 0, diff * lgf, NEG_BIG))
                    + jnp.exp(jnp.where(diff <= 0, -diff * lgb, NEG_BIG)))
            o = _mm(_mm_nt(qh[rows], kh, HI) * dmat, vh, HI)
            if latent:
                p = pos[rows]
                o = o + jnp.exp((p + 1.0) * lgf) * _mm(qh[rows], s0_ref[0, h], HI)
                o = o + jnp.exp((t - p) * lgb) * _mm(qh[rows], s0_ref[0, N_HEADS + h], HI)
            mu = jnp.mean(o, axis=-1, keepdims=True)
            oc = o - mu
            on = oc * lax.rsqrt(jnp.mean(oc * oc, axis=-1, keepdims=True) + EPS)
            o_ref[rows, hs] = on * _silu(r[rows, 3 * W_GROUP + h * HEAD_DIM:3 * W_GROUP + (h + 1) * HEAD_DIM])
        sf = _mm_tn(kh * jnp.exp((t - 1.0 - pos) * lgf), vh, HI)
        sb = _mm_tn(kh * jnp.exp(pos * lgb), vh, HI)
        if latent:
            sf = sf + jnp.exp(t * lgf) * s0_ref[0, h]
            sb = sb + jnp.exp(t * lgb) * s0_ref[0, N_HEADS + h]
        sfin_ref[0, h] = sf
        sfin_ref[0, N_HEADS + h] = sb


def _ret(r_all, logit, s0, cos, sin, t, n_seq, first_block, latent):
    return pl.pallas_call(
        functools.partial(_ret_kernel, t=t, latent=latent),
        grid=(n_seq,),
        in_specs=[pl.BlockSpec((t, 4 * W_GROUP), lambda i: (i + first_block, 0)),
                  pl.BlockSpec((1, LANES), lambda i: (0, 0)),
                  pl.BlockSpec((1, N_GATE, HEAD_DIM, HEAD_DIM), lambda i: (i, 0, 0, 0)),
                  pl.BlockSpec((t, W_GROUP), lambda i: (0, 0)),
                  pl.BlockSpec((t, W_GROUP), lambda i: (0, 0))],
        out_specs=[pl.BlockSpec((t, W_GROUP), lambda i: (i, 0)),
                   pl.BlockSpec((1, N_GATE, HEAD_DIM, HEAD_DIM), lambda i: (i, 0, 0, 0))],
        out_shape=[jax.ShapeDtypeStruct((n_seq * t, W_GROUP), F32),
                   jax.ShapeDtypeStruct((n_seq, N_GATE, HEAD_DIM, HEAD_DIM), F32)],
        compiler_params=_cparams("arbitrary"),
        name="ret",
    )(r_all, logit, s0, cos, sin)


def _rope_tables(t):
    pos = np.arange(t)
    row = (pos // GRID_W).astype(np.float32)
    col = (pos % GRID_W).astype(np.float32)
    nf = HEAD_DIM // 4
    inv_freq = jnp.power(ROPE_BASE, -jnp.arange(nf, dtype=F32) / nf)
    ang_r = jnp.asarray(row)[:, None] * inv_freq[None, :]
    ang_c = jnp.asarray(col)[:, None] * inv_freq[None, :]
    cos = jnp.concatenate([jnp.cos(ang_r)] * 2 + [jnp.cos(ang_c)] * 2, axis=1)
    sin = jnp.concatenate([-jnp.sin(ang_r), jnp.sin(ang_r), -jnp.sin(ang_c), jnp.sin(ang_c)], axis=1)
    return jnp.tile(cos, (1, N_HEADS)), jnp.tile(sin, (1, N_HEADS))


def _head_rms(x, w):
    return x * lax.rsqrt(jnp.mean(x * x, axis=-1, keepdims=True) + EPS) * w


def _ctx_attn_kernel(n_ref, qw_ref, kw_ref, o_ref, k_out, v_out):
    n = n_ref[...]
    v_out[0] = n[:, 2 * W_GROUP:]
    for h in range(N_HEADS):
        hs = slice(h * HEAD_DIM, (h + 1) * HEAD_DIM)
        qh = _head_rms(n[:, hs], qw_ref[...])
        kh = _head_rms(n[:, W_GROUP + h * HEAD_DIM:W_GROUP + (h + 1) * HEAD_DIM], kw_ref[...])
        vh = n[:, 2 * W_GROUP + h * HEAD_DIM:2 * W_GROUP + (h + 1) * HEAD_DIM]
        k_out[0, :, hs] = kh
        s = _mm_nt(qh, kh, HI) * (HEAD_DIM ** -0.5)
        p = jnp.exp(s - jnp.max(s, axis=-1, keepdims=True))
        o_ref[:, hs] = _mm(p, vh, HI) / jnp.sum(p, axis=-1, keepdims=True)


def _ctx_attn(n_all, qw, kw):
    return pl.pallas_call(
        _ctx_attn_kernel,
        grid=(BATCH,),
        in_specs=[pl.BlockSpec((SEQ, 3 * W_GROUP), lambda i: (i, 0)),
                  pl.BlockSpec((1, HEAD_DIM), lambda i: (0, 0)),
                  pl.BlockSpec((1, HEAD_DIM), lambda i: (0, 0))],
        out_specs=[pl.BlockSpec((SEQ, W_GROUP), lambda i: (i, 0)),
                   pl.BlockSpec((1, SEQ, W_GROUP), lambda i: (i, 0, 0)),
                   pl.BlockSpec((1, SEQ, W_GROUP), lambda i: (i, 0, 0))],
        out_shape=[jax.ShapeDtypeStruct((N_CTX, W_GROUP), F32),
                   jax.ShapeDtypeStruct((BATCH, SEQ, W_GROUP), F32),
                   jax.ShapeDtypeStruct((BATCH, SEQ, W_GROUP), F32)],
        compiler_params=_cparams("arbitrary"),
        name="ctx_attn",
    )(n_all, qw, kw)


def _nat_kernel(n_ref, ck_ref, cv_ref, bias_ref, qw_ref, kw_ref, o_ref):
    h = pl.program_id(1)
    n = n_ref[...]
    sel = (_iota((W_GROUP, HEAD_DIM), 0) == _iota((W_GROUP, HEAD_DIM), 1) + h * HEAD_DIM).astype(F32)
    qh = _head_rms(_mm(n[:, :W_GROUP], sel, HI), qw_ref[...]) * (HEAD_DIM ** -0.5)
    kh = _head_rms(_mm(n[:, W_GROUP:2 * W_GROUP], sel, HI), kw_ref[...])
    vh = _mm(n[:, 2 * W_GROUP:], sel, HI)
    ckh = _mm(ck_ref[0], sel, HI)
    cvh = _mm(cv_ref[0], sel, HI)
    for qt in range(DEC_SEQ // Q_TILE):
        rows = slice(qt * Q_TILE, (qt + 1) * Q_TILE)
        s_loc = _mm_nt(qh[rows], kh, HI) + bias_ref[0, rows, :]
        s_ctx = _mm_nt(qh[rows], ckh, HI)
        m = jnp.maximum(jnp.max(s_loc, axis=-1, keepdims=True), jnp.max(s_ctx, axis=-1, keepdims=True))
        p_loc = jnp.exp(s_loc - m)
        p_ctx = jnp.exp(s_ctx - m)
        den = jnp.sum(p_loc, axis=-1, keepdims=True) + jnp.sum(p_ctx, axis=-1, keepdims=True)
        o_ref[0, rows, :] = (_mm(p_loc, vh, HI) + _mm(p_ctx, cvh, HI)) / den


def _nat(n_all, ck, cv, bias, qw, kw):
    first_block = N_CTX // DEC_SEQ
    return pl.pallas_call(
        _nat_kernel,
        grid=(DEC_BATCH, N_HEADS),
        in_specs=[pl.BlockSpec((DEC_SEQ, 3 * W_GROUP), lambda b, h: (b + first_block, 0)),
                  pl.BlockSpec((1, PAST_LEN, W_GROUP), lambda b, h: (b, 0, 0)),
                  pl.BlockSpec((1, PAST_LEN, W_GROUP), lambda b, h: (b, 0, 0)),
                  pl.BlockSpec((1, DEC_SEQ, DEC_SEQ), lambda b, h: (h, 0, 0)),
                  pl.BlockSpec((1, HEAD_DIM), lambda b, h: (0, 0)),
                  pl.BlockSpec((1, HEAD_DIM), lambda b, h: (0, 0))],
        out_specs=pl.BlockSpec((1, DEC_SEQ, HEAD_DIM), lambda b, h: (b * N_HEADS + h, 0, 0)),
        out_shape=jax.ShapeDtypeStruct((DEC_BATCH * N_HEADS, DEC_SEQ, HEAD_DIM), F32),
        compiler_params=_cparams("arbitrary", "arbitrary"),
        name="nat",
    )(n_all, ck, cv, bias, qw, kw)


def _nat_bias(rpb):
    rows_n = DEC_SEQ // GRID_W
    kh = min(WIN_ROWS, rows_n)
    r = np.arange(rows_n)
    c = np.arange(GRID_W)
    r0 = np.clip(r - kh // 2, 0, rows_n - kh)
    c0 = np.clip(c - WIN_COLS // 2, 0, GRID_W - WIN_COLS)
    row_in = (r[None, :] >= r0[:, None]) & (r[None, :] < r0[:, None] + kh)
    col_in = (c[None, :] >= c0[:, None]) & (c[None, :] < c0[:, None] + WIN_COLS)
    row_idx = np.clip(r[None, :] - r[:, None] + WIN_ROWS - 1, 0, 2 * WIN_ROWS - 2)
    col_idx = np.clip(c[None, :] - c[:, None], -(WIN_COLS - 1), WIN_COLS - 1) + WIN_COLS - 1
    row_hot = (row_idx[..., None] == np.arange(2 * WIN_ROWS - 1)).astype(np.float32)
    col_hot = (col_idx[..., None] == np.arange(2 * WIN_COLS - 1)).astype(np.float32)
    bias = jnp.einsum('hab,rsa,qkb->hrqsk', rpb.astype(F32), row_hot, col_hot, precision=HI)
    inside = row_in[:, None, :, None] & col_in[None, :, None, :]
    return jnp.where(inside[None], bias, NEG_BIG).reshape(N_HEADS, DEC_SEQ, DEC_SEQ)


ROUTE_W = D_MODEL + LANES
GSEL_LANE = N_EXPERTS


def _pack_router(we, be, wg, bg):
    pad = LANES - N_EXPERTS - N_GROUPS
    w = jnp.concatenate([we, wg, jnp.zeros((D_MODEL, pad), F32)], axis=1)
    b = jnp.concatenate([be, bg, jnp.zeros((pad,), F32)]).reshape(1, LANES)
    return w, b


def _lane_min_where(mask, lane):
    return jnp.min(jnp.where(mask, lane, LANES), axis=-1, keepdims=True)


def _outproj_kernel(x_ref, m0, m1, m2, m3, mod_ref, nw_ref, w_ref, rw_ref, rb_ref, x_out, route_out):
    acc = None
    for i, m_ref in enumerate((m0, m1, m2, m3)):
        part = _mm(m_ref[...].astype(BF16), w_ref[i * W_GROUP:(i + 1) * W_GROUP, :])
        acc = part if acc is None else acc + part
    x = x_ref[...] + mod_ref[0, 2:3, :] * acc
    x_out[...] = x
    y = x * lax.rsqrt(jnp.mean(x * x, axis=-1, keepdims=True) + EPS) * nw_ref[...]
    hf = y * (1.0 + mod_ref[0, 4:5, :]) + mod_ref[0, 3:4, :]
    route_out[:, :D_MODEL] = hf

    logits = _mm(hf, rw_ref[...], HI) + rb_ref[...]
    lane = _iota(logits.shape, 1)
    is_g = (lane >= N_EXPERTS) & (lane < N_EXPERTS + N_GROUPS)
    gl = jnp.where(is_g, logits, NEG_BIG)
    ge = jnp.exp(gl - jnp.max(gl, axis=-1, keepdims=True))
    gp = jnp.where(is_g, ge / jnp.sum(ge, axis=-1, keepdims=True), -1.0)
    gw = jnp.max(gp, axis=-1, keepdims=True)
    gsel = _lane_min_where(gp == gw, lane) - N_EXPERTS
    in_grp = (lane // EXPERTS_PER_GROUP == gsel) & (lane < N_EXPERTS)
    el = jnp.where(in_grp, logits, NEG_BIG)
    ee = jnp.exp(el - jnp.max(el, axis=-1, keepdims=True))
    ep = jnp.where(in_grp, ee / jnp.sum(ee, axis=-1, keepdims=True), -1.0)
    t1 = jnp.max(ep, axis=-1, keepdims=True)
    i1 = _lane_min_where(ep == t1, lane)
    ep2 = jnp.where(lane == i1, -1.0, ep)
    t2 = jnp.max(ep2, axis=-1, keepdims=True)
    i2 = _lane_min_where(ep2 == t2, lane)
    tsum = t1 + t2
    combine = jnp.where(lane == i1, gw * (t1 / tsum), 0.0) + jnp.where(lane == i2, gw * (t2 / tsum), 0.0)
    route_out[:, D_MODEL:] = jnp.where(lane == GSEL_LANE, gsel.astype(F32), combine)


def _outproj(x, mixed, mod, norm_w, w_out, rw, rb):
    tile = lambda w: pl.BlockSpec((TOK_TILE, w), lambda i: (i, 0))
    whole = lambda a: pl.BlockSpec(a.shape, lambda i: (0,) * a.ndim)
    return pl.pallas_call(
        _outproj_kernel,
        grid=(N_TOK // TOK_TILE,),
        in_specs=[tile(D_MODEL)] + [tile(W_GROUP)] * 4
                 + [pl.BlockSpec((1, SUBLANES, D_MODEL), lambda i: (_cond_of_tile(i), 0, 0)),
                    pl.BlockSpec((1, D_MODEL), lambda i: (0, 0)), whole(w_out), whole(rw), whole(rb)],
        out_specs=[tile(D_MODEL), tile(ROUTE_W)],
        out_shape=[jax.ShapeDtypeStruct((N_TOK, D_MODEL), F32), jax.ShapeDtypeStruct((N_TOK, ROUTE_W), F32)],
        compiler_params=_cparams("arbitrary"),
        name="outproj",
    )(x, *mixed, mod, norm_w.reshape(1, D_MODEL), w_out, rw, rb)


def _moe_kernel(tile_group, row_token, route_hbm, wg_ref, wu_ref, wd_ref, y_hbm, xbuf, ybuf, sem_in, sem_out):
    i = pl.program_id(0)
    g = tile_group[i]
    base = i * MOE_TILE

    def gather_copy(r):
        tok = jnp.maximum(row_token[base + r], 0)
        return pltpu.make_async_copy(route_hbm.at[pl.ds(tok, 1)], xbuf.at[pl.ds(r, 1)], sem_in)

    def scatter_copy(r):
        return pltpu.make_async_copy(ybuf.at[pl.ds(r, 1)], y_hbm.at[pl.ds(row_token[base + r], 1)], sem_out)

    @pl.when(row_token[base] >= 0)
    def _():
        @pl.loop(0, MOE_TILE)
        def _(r):
            gather_copy(r).start()

        @pl.loop(0, MOE_TILE)
        def _(r):
            gather_copy(r).wait()

        x = xbuf[:, :D_MODEL].astype(BF16)
        route = xbuf[:, D_MODEL:]
        lane = _iota(route.shape, 1)
        acc = jnp.zeros((MOE_TILE, D_MODEL), F32)
        for e in range(EXPERTS_PER_GROUP):
            cols = slice(e * EXPERT_FF, (e + 1) * EXPERT_FF)
            cw = jnp.sum(jnp.where(lane == g * EXPERTS_PER_GROUP + e, route, 0.0), axis=-1, keepdims=True)
            act = _silu(_mm(x, wg_ref[0, :, cols])) * _mm(x, wu_ref[0, :, cols]) * cw
            acc = acc + _mm(act.astype(BF16), wd_ref[0, cols, :])
        ybuf[...] = acc

        @pl.loop(0, MOE_TILE)
        def _(r):
            @pl.when(row_token[base + r] >= 0)
            def _():
                scatter_copy(r).start()

        @pl.loop(0, MOE_TILE)
        def _(r):
            @pl.when(row_token[base + r] >= 0)
            def _():
                scatter_copy(r).wait()


def _moe(route, tile_group, row_token, wg, wu, wd):
    n_tiles = MOE_ROWS // MOE_TILE
    grid_spec = pltpu.PrefetchScalarGridSpec(
        num_scalar_prefetch=2,
        grid=(n_tiles,),
        in_specs=[pl.BlockSpec(memory_space=pl.ANY),
                  pl.BlockSpec((1, D_MODEL, GROUP_FF), lambda i, tg, rt: (tg[i], 0, 0)),
                  pl.BlockSpec((1, D_MODEL, GROUP_FF), lambda i, tg, rt: (tg[i], 0, 0)),
                  pl.BlockSpec((1, GROUP_FF, D_MODEL), lambda i, tg, rt: (tg[i], 0, 0))],
        out_specs=pl.BlockSpec(memory_space=pl.ANY),
        scratch_shapes=[pltpu.VMEM((MOE_TILE, ROUTE_W), F32), pltpu.VMEM((MOE_TILE, D_MODEL), F32),
                        pltpu.SemaphoreType.DMA, pltpu.SemaphoreType.DMA],
    )
    return pl.pallas_call(
        _moe_kernel,
        grid_spec=grid_spec,
        out_shape=jax.ShapeDtypeStruct((N_TOK, D_MODEL), F32),
        compiler_params=_cparams("arbitrary"),
        name="moe",
    )(tile_group, row_token, route, wg, wu, wd)


def _moe_schedule(gsel):
    onehot = (gsel[:, None] == jnp.arange(N_GROUPS)[None, :]).astype(jnp.int32)
    rank = jnp.take_along_axis(jnp.cumsum(onehot, axis=0) - onehot, gsel[:, None], axis=1)[:, 0]
    counts = jnp.sum(onehot, axis=0)
    tiles = (counts + MOE_TILE - 1) // MOE_TILE
    tile_start = jnp.cumsum(tiles) - tiles
    dest = tile_start[gsel] * MOE_TILE + rank
    row_token = jnp.full((MOE_ROWS,), -1, jnp.int32).at[dest].set(jnp.arange(N_TOK, dtype=jnp.int32))
    tile_idx = jnp.arange(MOE_ROWS // MOE_TILE)
    tile_group = jnp.clip(jnp.sum(tile_idx[:, None] >= tile_start[None, :], axis=1) - 1, 0, N_GROUPS - 1)
    return tile_group.astype(jnp.int32), row_token


def _resid_kernel(x_ref, y_ref, mod_ref, o_ref):
    o_ref[...] = x_ref[...] + mod_ref[0, 5:6, :] * y_ref[...]


def _resid(x, y, mod):
    tile = pl.BlockSpec((TOK_TILE, D_MODEL), lambda i: (i, 0))
    return pl.pallas_call(
        _resid_kernel,
        grid=(N_TOK // TOK_TILE,),
        in_specs=[tile, tile, pl.BlockSpec((1, SUBLANES, D_MODEL), lambda i: (_cond_of_tile(i), 0, 0))],
        out_specs=tile,
        out_shape=jax.ShapeDtypeStruct((N_TOK, D_MODEL), F32),
        compiler_params=_cparams("arbitrary"),
        name="resid",
    )(x, y, mod)


def _lane_row(v):
    v = v.reshape(-1).astype(F32)
    return jnp.concatenate([v, jnp.zeros((LANES - v.shape[0],), F32)]).reshape(1, LANES)


def _pad_rows(w):
    return jnp.concatenate([w, jnp.zeros((SUBLANES - w.shape[0], w.shape[1]), w.dtype)], axis=0)


def kernel(x_prompt, x_sample, state_gdn, state_ret, cache_nat_k, cache_nat_v, c, c_ctx, ada_w, ada_b, norm_mix_w, norm_ffn_w, w_in, gdn_conv_w, gdn_a_log, gdn_dt_bias, gdn_norm_w, ret_gamma_logit, nat_q_norm_w, nat_k_norm_w, nat_rpb, sc_conv_w, w_out, router_group_w, router_group_b, router_expert_w, router_expert_b, moe_w_gate, moe_w_up, moe_w_down):
    x = jnp.concatenate([x_prompt.reshape(N_CTX, D_MODEL), x_sample.reshape(N_LAT, D_MODEL)], axis=0)
    cond = jnp.concatenate([c_ctx[None, :], c, jnp.zeros((SUBLANES - 1 - DEC_BATCH, D_MODEL), F32)], axis=0)
    ada = _ada(cond, ada_w, ada_b).reshape(DEPTH, SUBLANES, 6, D_MODEL)
    cos, sin = _rope_tables(DEC_SEQ)
    zero_state = jnp.zeros((BATCH, N_GATE, HEAD_DIM, HEAD_DIM), F32)
    lat_block = N_CTX // DEC_SEQ
    gdn_list, ret_list, k_list, v_list = [], [], [], []
    for l in range(DEPTH):
        mod = jnp.concatenate([ada[l, :1 + DEC_BATCH], jnp.zeros((1 + DEC_BATCH, SUBLANES - 6, D_MODEL), F32)], axis=1)
        a_gdn, a_ret, a_nat, a_sc, a_gate = _inproj(x, mod, norm_mix_w[l], _pack_w_in(w_in[l]))

        conv_w = _pad_rows(gdn_conv_w[l])
        a_log, dt_b = _lane_row(gdn_a_log[l]), _lane_row(gdn_dt_bias[l])
        gnw = jnp.tile(gdn_norm_w[l], N_HEADS).reshape(1, W_GROUP)
        o_gdn_c, s_gdn = _gdn(a_gdn, a_gate, conv_w, a_log, dt_b, gnw, zero_state, SEQ, BATCH, 0)
        s0 = state_gdn[:, l].reshape(DEC_BATCH, N_GATE, HEAD_DIM, HEAD_DIM)
        o_gdn_l, _ = _gdn(a_gdn, a_gate, conv_w, a_log, dt_b, gnw, s0, DEC_SEQ, DEC_BATCH, lat_block)

        logit = _lane_row(ret_gamma_logit[l])
        o_ret_c, s_ret = _ret(a_ret, logit, zero_state, cos[:SEQ], sin[:SEQ], SEQ, BATCH, 0, False)
        s0 = state_ret[:, l].reshape(DEC_BATCH, N_GATE, HEAD_DIM, HEAD_DIM)
        o_ret_l, _ = _ret(a_ret, logit, s0, cos, sin, DEC_SEQ, DEC_BATCH, lat_block, True)

        qw, kw = nat_q_norm_w[l].reshape(1, HEAD_DIM), nat_k_norm_w[l].reshape(1, HEAD_DIM)
        o_nat_c, n_k, n_v = _ctx_attn(a_nat, qw, kw)
        o_nat_l = _nat(a_nat, cache_nat_k[:, l].reshape(DEC_BATCH, PAST_LEN, W_GROUP),
                       cache_nat_v[:, l].reshape(DEC_BATCH, PAST_LEN, W_GROUP), _nat_bias(nat_rpb[l]), qw, kw)
        o_nat_l = o_nat_l.reshape(DEC_BATCH, N_HEADS, DEC_SEQ, HEAD_DIM).transpose(0, 2, 1, 3).reshape(N_LAT, W_GROUP)

        sc_w = _pad_rows(sc_conv_w[l])
        o_sc_c = _sconv(a_sc, sc_w, SEQ, BATCH, 0)
        o_sc_l = _sconv(a_sc, sc_w, DEC_SEQ, DEC_BATCH, lat_block)

        mixed = [jnp.concatenate(p, axis=0) for p in
                 ((o_gdn_c, o_gdn_l), (o_ret_c, o_ret_l), (o_nat_c, o_nat_l), (o_sc_c, o_sc_l))]
        rw, rb = _pack_router(router_expert_w[l], router_expert_b[l], router_group_w[l], router_group_b[l])
        x_mid, route = _outproj(x, mixed, mod, norm_ffn_w[l], w_out[l].astype(BF16), rw, rb)

        tile_group, row_token = _moe_schedule(route[:, D_MODEL + GSEL_LANE].astype(jnp.int32))
        to_group = lambda w: (w.reshape(N_GROUPS, EXPERTS_PER_GROUP, D_MODEL, EXPERT_FF)
                              .transpose(0, 2, 1, 3).reshape(N_GROUPS, D_MODEL, GROUP_FF).astype(BF16))
        y = _moe(route, tile_group, row_token, to_group(moe_w_gate[l]), to_group(moe_w_up[l]),
                 moe_w_down[l].reshape(N_GROUPS, GROUP_FF, D_MODEL).astype(BF16))
        x = _resid(x_mid, y, mod)

        gdn_list.append(s_gdn.reshape(BATCH, 2, N_HEADS, HEAD_DIM, HEAD_DIM))
        ret_list.append(s_ret.reshape(BATCH, 2, N_HEADS, HEAD_DIM, HEAD_DIM))
        k_list.append(n_k.reshape(BATCH, SEQ, N_HEADS, HEAD_DIM))
        v_list.append(n_v.reshape(BATCH, SEQ, N_HEADS, HEAD_DIM))
    return (x[:N_CTX].reshape(BATCH, SEQ, D_MODEL), x[N_CTX:].reshape(DEC_BATCH, DEC_SEQ, D_MODEL),
            jnp.stack(gdn_list, axis=1), jnp.stack(ret_list, axis=1),
            jnp.stack(k_list, axis=1), jnp.stack(v_list, axis=1))
```

```python
import functools

import numpy as np
import jax
import jax.numpy as jnp
from jax import lax
from jax.experimental import pallas as pl
from jax.experimental.pallas import tpu as pltpu

D_MODEL = 1024
BATCH = 16
SEQ = 256
DEPTH = 2
DEC_BATCH = 2
DEC_SEQ = 1024
PAST_LEN = 256
GRID_W = 64
HEAD_DIM = 64
W_GROUP = D_MODEL // 4
N_HEADS = W_GROUP // HEAD_DIM
CHUNK = 64
WIN_ROWS = 8
WIN_COLS = 16
ROPE_BASE = 10000.0
N_GROUPS = 4
EXPERTS_PER_GROUP = 8
N_EXPERTS = N_GROUPS * EXPERTS_PER_GROUP
EXPERT_FF = 256
GROUP_FF = EXPERTS_PER_GROUP * EXPERT_FF
EPS = 1e-6

N_CTX = BATCH * SEQ
N_LAT = DEC_BATCH * DEC_SEQ
N_TOK = N_CTX + N_LAT
LANES = 128
SUBLANES = 8
TOK_TILE = 512
MOE_TILE = 256
MOE_ROWS = N_TOK + N_GROUPS * MOE_TILE
DMA_UNROLL = 8
Q_TILE = 256
VMEM_LIMIT = 48 * 1024 * 1024
NEG_BIG = -1e30
N_GATE = 2 * N_HEADS

F32 = jnp.float32
BF16 = jnp.bfloat16
HI = lax.Precision.HIGHEST


def _mm(a, b, prec=None):
    return lax.dot_general(a, b, (((1,), (0,)), ((), ())), precision=prec, preferred_element_type=F32)


def _mm_nt(a, b, prec=None):
    return lax.dot_general(a, b, (((1,), (1,)), ((), ())), precision=prec, preferred_element_type=F32)


def _mm_tn(a, b, prec=None):
    return lax.dot_general(a, b, (((0,), (0,)), ((), ())), precision=prec, preferred_element_type=F32)


def _bmm(a, b):
    return _mm(a.astype(BF16), b.astype(BF16))


def _sigmoid(x):
    return 1.0 / (1.0 + jnp.exp(-x))


def _silu(x):
    return x * _sigmoid(x)


def _softplus(x):
    return jnp.maximum(x, 0.0) + jnp.log(1.0 + jnp.exp(-jnp.abs(x)))


def _iota(shape, dim):
    return lax.broadcasted_iota(jnp.int32, shape, dim)


def _cparams(*sem):
    return pltpu.CompilerParams(dimension_semantics=sem, vmem_limit_bytes=VMEM_LIMIT)


def _cond_of_tile(i):
    n_ctx_tiles = N_CTX // TOK_TILE
    return jnp.where(i < n_ctx_tiles, 0, 1 + (i - n_ctx_tiles) // (DEC_SEQ // TOK_TILE))


def _head_sum_matrix():
    return (_iota((W_GROUP, W_GROUP), 0) // HEAD_DIM == _iota((W_GROUP, W_GROUP), 1) // HEAD_DIM).astype(F32)


def _ada_kernel(c_ref, w_ref, b_ref, o_ref):
    o_ref[0] = _mm(_silu(c_ref[...]), w_ref[0], HI) + b_ref[0]


def _ada(cond, ada_w, ada_b):
    tn = 1536
    n_out = 6 * D_MODEL
    return pl.pallas_call(
        _ada_kernel,
        grid=(DEPTH, n_out // tn),
        in_specs=[pl.BlockSpec((SUBLANES, D_MODEL), lambda l, j: (0, 0)),
                  pl.BlockSpec((1, D_MODEL, tn), lambda l, j: (l, 0, j)),
                  pl.BlockSpec((1, 1, tn), lambda l, j: (l, 0, j))],
        out_specs=pl.BlockSpec((1, SUBLANES, tn), lambda l, j: (l, 0, j)),
        out_shape=jax.ShapeDtypeStruct((DEPTH, SUBLANES, n_out), F32),
        compiler_params=_cparams("arbitrary", "arbitrary"),
        name="ada",
    )(cond, ada_w, ada_b.reshape(DEPTH, 1, n_out))


IN_WIDTHS = (4 * W_GROUP, 4 * W_GROUP, 3 * W_GROUP, 3 * W_GROUP, LANES)
IN_PACKED = sum(IN_WIDTHS)


def _pack_w_in(w):
    sizes = (3 * W_GROUP, W_GROUP, N_GATE, N_GATE, 3 * W_GROUP, W_GROUP, 3 * W_GROUP, 3 * W_GROUP)
    offs = np.concatenate([[0], np.cumsum(sizes)])
    seg = [w[:, offs[i]:offs[i + 1]] for i in range(len(sizes))]
    gate = jnp.concatenate([seg[2], seg[3], jnp.zeros((D_MODEL, LANES - 2 * N_GATE), w.dtype)], axis=1)
    return jnp.concatenate([seg[0], seg[1], seg[4], seg[5], seg[6], seg[7], gate], axis=1).astype(BF16)


def _inproj_kernel(x_ref, mod_ref, nw_ref, w_ref, *o_refs):
    x = x_ref[...]
    y = x * lax.rsqrt(jnp.mean(x * x, axis=-1, keepdims=True) + EPS) * nw_ref[...]
    h = (y * (1.0 + mod_ref[0, 1:2, :]) + mod_ref[0, 0:1, :]).astype(BF16)
    off = 0
    for o_ref, width in zip(o_refs, IN_WIDTHS):
        o_ref[...] = _mm(h, w_ref[:, off:off + width])
        off += width


def _inproj(x, mod, norm_w, w_packed):
    return pl.pallas_call(
        _inproj_kernel,
        grid=(N_TOK // TOK_TILE,),
        in_specs=[pl.BlockSpec((TOK_TILE, D_MODEL), lambda i: (i, 0)),
                  pl.BlockSpec((1, SUBLANES, D_MODEL), lambda i: (_cond_of_tile(i), 0, 0)),
                  pl.BlockSpec((1, D_MODEL), lambda i: (0, 0)),
                  pl.BlockSpec((D_MODEL, IN_PACKED), lambda i: (0, 0))],
        out_specs=[pl.BlockSpec((TOK_TILE, w), lambda i: (i, 0)) for w in IN_WIDTHS],
        out_shape=[jax.ShapeDtypeStruct((N_TOK, w), F32) for w in IN_WIDTHS],
        compiler_params=_cparams("arbitrary"),
        name="inproj",
    )(x, mod, norm_w.reshape(1, D_MODEL), w_packed)


def _seq_call(kernel_fn, args, in_specs, out_specs, out_shape, into, **kwargs):
    aliases = {}
    if into is not None:
        n_in = len(args)
        inner = kernel_fn
        kernel_fn = lambda *refs: inner(*refs[:n_in], *refs[n_in + 1:])
        aliases = {n_in: 0}
        args = list(args) + [into]
        in_specs = list(in_specs) + [pl.BlockSpec(memory_space=pl.ANY)]
    return pl.pallas_call(kernel_fn, in_specs=in_specs, out_specs=out_specs, out_shape=out_shape,
                          input_output_aliases=aliases, **kwargs)(*args)


def _shift_rows(p, t):
    row = _iota(p.shape, 0)
    prev = jnp.where(row == 0, 0.0, pltpu.roll(p, 1, 0))
    nxt = jnp.where(row == t - 1, 0.0, pltpu.roll(p, t - 1, 0))
    return prev, nxt


def _conv3(x, w_ref, t):
    prev, nxt = _shift_rows(x, t)
    return w_ref[0:1, :] * prev + w_ref[1:2, :] * x + w_ref[2:3, :] * nxt


def _sconv_kernel(s_ref, w_ref, o_ref, *, t):
    s = s_ref[...]
    p = s[:, W_GROUP:2 * W_GROUP] * s[:, 2 * W_GROUP:]
    o_ref[...] = s[:, :W_GROUP] * _conv3(p, w_ref, t)


def _sconv(s_all, w, t, n_seq, first_block, into=None):
    return _seq_call(
        functools.partial(_sconv_kernel, t=t), (s_all, w),
        in_specs=[pl.BlockSpec((t, 3 * W_GROUP), lambda i: (i + first_block, 0)),
                  pl.BlockSpec((SUBLANES, W_GROUP), lambda i: (0, 0))],
        out_specs=pl.BlockSpec((t, W_GROUP), lambda i: (i + first_block, 0)),
        out_shape=jax.ShapeDtypeStruct((N_TOK, W_GROUP), F32),
        into=into, grid=(n_seq,), compiler_params=_cparams("arbitrary"), name="sconv")


GDN_GROUP_CHUNKS = 4
GDN_CHAINS = GDN_GROUP_CHUNKS * N_GATE
GDN_PAIRS = GDN_GROUP_CHUNKS * N_HEADS


def _gdn_kernel(a_ref, gate_ref, convw_ref, alog_ref, dtb_ref, nw_ref, s0_ref, o_ref, sfin_ref,
                q_s, kv_s, kt_s, gc_s, eg_s, beta_s, gcrow_s, ekdrow_s, cdec_s, uo_s, wq_s, attn_s, kdt_s,
                st_s, wsqs_s, kk_s, d_s, off_s, m1_s, m2_s, p_s, m3_s, rhs_s, px_s, *, t):
    n_chunks = t // CHUNK
    a = a_ref[...]
    qkv = _silu(_conv3(a[:, :3 * W_GROUP], convw_ref, t))
    q = qkv[:, :W_GROUP]
    k = qkv[:, W_GROUP:2 * W_GROUP]
    v = qkv[:, 2 * W_GROUP:]
    head_sum = _head_sum_matrix()
    q = q * lax.rsqrt(_mm(q * q, head_sum, HI) + EPS) * (HEAD_DIM ** -0.5)
    k = k * lax.rsqrt(_mm(k * k, head_sum, HI) + EPS)
    for h in range(N_HEADS):
        hs = slice(h * HEAD_DIM, (h + 1) * HEAD_DIM)
        q_s[h] = q[:, hs]
        kv_s[h] = jnp.concatenate([k[:, hs], v[:, hs]], axis=1)
    k_t = k.T
    for c in range(n_chunks):
        kt_s[c] = k_t[:, c * CHUNK:(c + 1) * CHUNK]

    gates = gate_ref[...]
    log_a = -jnp.exp(alog_ref[...]) * _softplus(gates + dtb_ref[...])
    beta_s[...] = _sigmoid(gates)

    ci = _iota((CHUNK, CHUNK), 0)
    cj = _iota((CHUNK, CHUNK), 1)
    eye = (ci == cj).astype(F32)
    blk_mask = (ci // 16) == (cj // 16)
    low_half = _iota((CHUNK, 2 * HEAD_DIM), 1) < HEAD_DIM

    tri_f = (cj <= ci).astype(F32)
    tri_b = (cj >= ci).astype(F32)
    ones = jnp.ones((CHUNK, CHUNK), F32)
    fwd_lane = _iota((CHUNK, LANES), 1) < N_HEADS
    gate_rows = (_iota((N_GATE, LANES), 0) == _iota((N_GATE, LANES), 1)).astype(F32)
    for c in range(n_chunks):
        rows = slice(c * CHUNK, (c + 1) * CHUNK)
        g = log_a[rows, :]
        gc = jnp.where(fwd_lane, _mm(tri_f, g, HI), _mm(tri_b, g, HI))
        gt = _mm(ones, g, HI)
        gc_s[rows, :] = gc
        eg_s[rows, :] = jnp.exp(gc)
        gcrow_s[c] = _mm_nt(gate_rows, gc, HI)
        ekdrow_s[c] = jnp.exp(_mm_nt(gate_rows, gt - gc, HI))
        cdec = jnp.exp(_mm_nt(gate_rows, gt, HI))
        cdec_s[c] = jnp.concatenate([cdec, cdec], axis=1)

    def solve_group(grp, carry):
        row0 = grp * (GDN_GROUP_CHUNKS * CHUNK)
        chains = [(cl, a_idx) for cl in range(GDN_GROUP_CHUNKS) for a_idx in range(N_GATE)]

        def rows_of(cl):
            return pl.ds(pl.multiple_of(row0 + cl * CHUNK, CHUNK), CHUNK)

        for cl in range(GDN_GROUP_CHUNKS):
            for h in range(N_HEADS):
                rows = rows_of(cl)
                kq = jnp.concatenate([kv_s[h, rows, :HEAD_DIM], q_s[h, rows, :]], axis=0)
                k_t_h = kt_s[grp * GDN_GROUP_CHUNKS + cl, h * HEAD_DIM:(h + 1) * HEAD_DIM, :]
                kk_s[cl * N_HEADS + h] = _bmm(kq, k_t_h)
        for b, (cl, a_idx) in enumerate(chains):
            backward = a_idx >= N_HEADS
            h = a_idx % N_HEADS
            rows, c = rows_of(cl), grp * GDN_GROUP_CHUNKS + cl
            incl = (cj >= ci) if backward else (cj <= ci)
            strict = (cj > ci) if backward else (cj < ci)
            bt = beta_s[rows, N_GATE + a_idx:N_GATE + a_idx + 1]
            decay = jnp.exp(jnp.where(incl, gc_s[rows, a_idx:a_idx + 1] - gcrow_s[c, a_idx:a_idx + 1, :], NEG_BIG))
            low = jnp.where(strict, kk_s[cl * N_HEADS + h, :CHUNK, :] * bt * decay, 0.0)
            attn_s[a_idx, rows, :] = (kk_s[cl * N_HEADS + h, CHUNK:, :] * decay).astype(BF16)
            d = jnp.where(blk_mask, low, 0.0)
            d_s[b] = d
            off_s[b] = low - d
            rhs_s[b] = kv_s[h, rows, :] * bt * jnp.where(low_half, eg_s[rows, a_idx:a_idx + 1], 1.0)
        for b in range(GDN_CHAINS):
            m1_s[b] = _bmm(d_s[b], d_s[b])
        for b in range(GDN_CHAINS):
            d, d2 = d_s[b], m1_s[b]
            m2_s[b] = _bmm(d2, d2)
            p_s[b] = eye - d + d2 - _bmm(d, d2)
        for b in range(GDN_CHAINS):
            d4, p = m2_s[b], p_s[b]
            m3_s[b] = _bmm(d4, d4)
            p_s[b] = p + _bmm(p, d4)
        for b in range(GDN_CHAINS):
            p = p_s[b]
            p_s[b] = p + _bmm(p, m3_s[b])
        for b in range(GDN_CHAINS):
            p = p_s[b]
            m1_s[b] = _bmm(p, off_s[b])
            px_s[b] = _bmm(p, rhs_s[b])
        for b in range(GDN_CHAINS):
            m2_s[b] = _bmm(m1_s[b], m1_s[b])
        for b in range(GDN_CHAINS):
            n, n2 = m1_s[b], m2_s[b]
            m3_s[b] = eye - n + n2 - _bmm(n, n2)
        for b, (cl, a_idx) in enumerate(chains):
            h = a_idx % N_HEADS
            rows, c = rows_of(cl), grp * GDN_GROUP_CHUNKS + cl
            wu = _bmm(m3_s[b], px_s[b])
            uo_s[a_idx, rows, :] = wu
            wq_s[a_idx, c, :CHUNK, :] = wu[:, :HEAD_DIM].astype(BF16)
            wq_s[a_idx, c, CHUNK:, :] = (q_s[h, rows, :] * eg_s[rows, a_idx:a_idx + 1]).astype(BF16)
            k_t_h = kt_s[c, h * HEAD_DIM:(h + 1) * HEAD_DIM, :]
            kdt_s[a_idx, c] = (k_t_h * ekdrow_s[c, a_idx:a_idx + 1, :]).astype(BF16)
        return carry

    lax.fori_loop(0, n_chunks // GDN_GROUP_CHUNKS, solve_group, 0)

    for i in range(N_GATE):
        st_s[i] = jnp.concatenate([jnp.zeros((HEAD_DIM, HEAD_DIM), F32), s0_ref[0, i]], axis=1)

    def scan_chunk(c, carry):
        def chunk_of(a_idx):
            return (n_chunks - 1 - c) if a_idx >= N_HEADS else c

        for a_idx in range(N_GATE):
            wsqs_s[a_idx] = _mm(wq_s[a_idx, chunk_of(a_idx)], st_s[a_idx].astype(BF16))
        for a_idx in range(N_GATE):
            cc = chunk_of(a_idx)
            rows = pl.ds(pl.multiple_of(cc * CHUNK, CHUNK), CHUNK)
            v_new = (uo_s[a_idx, rows, :] - wsqs_s[a_idx, :CHUNK, :]).astype(BF16)
            uo_s[a_idx, rows, :] = wsqs_s[a_idx, CHUNK:, :] + _mm(attn_s[a_idx, rows, :], v_new)
            st_s[a_idx] = st_s[a_idx] * cdec_s[cc, a_idx:a_idx + 1, :] + _mm(kdt_s[a_idx, cc], v_new)
        return carry

    lax.fori_loop(0, n_chunks, scan_chunk, 0)
    for i in range(N_GATE):
        sfin_ref[0, i] = st_s[i, :, HEAD_DIM:]

    o = jnp.concatenate([(uo_s[h] + uo_s[N_HEADS + h])[:, HEAD_DIM:] for h in range(N_HEADS)], axis=1)
    ms = _mm(o * o, head_sum, HI) * (1.0 / HEAD_DIM)
    o_ref[...] = o * lax.rsqrt(ms + EPS) * nw_ref[...] * _silu(a[:, 3 * W_GROUP:])


def _gdn(a_all, gate_all, conv_w, a_log, dt_bias, norm_w, s0, t, n_seq, first_block, into=None):
    small = lambda: pl.BlockSpec((1, LANES), lambda i: (0, 0))
    n_chunks = t // CHUNK
    wide = 2 * HEAD_DIM
    scratch = [pltpu.VMEM((N_HEADS, t, HEAD_DIM), F32),
               pltpu.VMEM((N_HEADS, t, wide), F32),
               pltpu.VMEM((n_chunks, W_GROUP, CHUNK), F32),
               pltpu.VMEM((t, LANES), F32), pltpu.VMEM((t, LANES), F32), pltpu.VMEM((t, LANES), F32),
               pltpu.VMEM((n_chunks, N_GATE, CHUNK), F32), pltpu.VMEM((n_chunks, N_GATE, CHUNK), F32),
               pltpu.VMEM((n_chunks, N_GATE, wide), F32),
               pltpu.VMEM((N_GATE, t, wide), F32),
               pltpu.VMEM((N_GATE, n_chunks, 2 * CHUNK, HEAD_DIM), BF16),
               pltpu.VMEM((N_GATE, t, CHUNK), BF16),
               pltpu.VMEM((N_GATE, n_chunks, HEAD_DIM, CHUNK), BF16),
               pltpu.VMEM((N_GATE, HEAD_DIM, wide), F32),
               pltpu.VMEM((N_GATE, 2 * CHUNK, wide), F32),
               pltpu.VMEM((GDN_PAIRS, 2 * CHUNK, CHUNK), F32)]
    scratch += [pltpu.VMEM((GDN_CHAINS, CHUNK, CHUNK), F32)] * 6
    scratch += [pltpu.VMEM((GDN_CHAINS, CHUNK, wide), F32)] * 2
    return _seq_call(
        functools.partial(_gdn_kernel, t=t), (a_all, gate_all, conv_w, a_log, dt_bias, norm_w, s0),
        in_specs=[pl.BlockSpec((t, 4 * W_GROUP), lambda i: (i + first_block, 0)),
                  pl.BlockSpec((t, LANES), lambda i: (i + first_block, 0)),
                  pl.BlockSpec((SUBLANES, 3 * W_GROUP), lambda i: (0, 0)),
                  small(), small(),
                  pl.BlockSpec((1, W_GROUP), lambda i: (0, 0)),
                  pl.BlockSpec((1, N_GATE, HEAD_DIM, HEAD_DIM), lambda i: (i, 0, 0, 0))],
        out_specs=[pl.BlockSpec((t, W_GROUP), lambda i: (i + first_block, 0)),
                   pl.BlockSpec((1, N_GATE, HEAD_DIM, HEAD_DIM), lambda i: (i, 0, 0, 0))],
        out_shape=[jax.ShapeDtypeStruct((N_TOK, W_GROUP), F32),
                   jax.ShapeDtypeStruct((n_seq, N_GATE, HEAD_DIM, HEAD_DIM), F32)],
        scratch_shapes=scratch,
        into=into, grid=(n_seq,), compiler_params=_cparams("arbitrary"), name="gdn")


def _swap16(x):
    width = x.shape[-1]
    first = (_iota(x.shape, 1) // 16) % 2 == 0
    return jnp.where(first, pltpu.roll(x, width - 16, 1), pltpu.roll(x, 16, 1))


def _ret_kernel(r_ref, lg_ref, s0_ref, cos_ref, sin_ref, o_ref, sfin_ref, *, t, latent):
    r = r_ref[...]
    q = r[:, :W_GROUP]
    k = r[:, W_GROUP:2 * W_GROUP]
    v = r[:, 2 * W_GROUP:3 * W_GROUP]
    if latent:
        q = q * cos_ref[...] + _swap16(q) * sin_ref[...]
        k = k * cos_ref[...] + _swap16(k) * sin_ref[...]
    k = k * (HEAD_DIM ** -0.5)
    logit = lg_ref[...]
    lg = -_softplus(-logit)
    pos = _iota((t, 1), 0).astype(F32)
    for h in range(N_HEADS):
        hs = slice(h * HEAD_DIM, (h + 1) * HEAD_DIM)
        qh, kh, vh = q[:, hs], k[:, hs], v[:, hs]
        lgf = lg[:, h:h + 1]
        lgb = lg[:, N_HEADS + h:N_HEADS + h + 1]
        for qt in range(t // Q_TILE):
            rows = slice(qt * Q_TILE, (qt + 1) * Q_TILE)
            diff = (_iota((Q_TILE, t), 0) + qt * Q_TILE - _iota((Q_TILE, t), 1)).astype(F32)
            dmat = (jnp.exp(jnp.where(diff >= 0, diff * lgf, NEG_BIG))
                    + jnp.exp(jnp.where(diff <= 0, -diff * lgb, NEG_BIG)))
            o = _bmm(_mm_nt(qh[rows].astype(BF16), kh.astype(BF16)) * dmat, vh)
            if latent:
                p = pos[rows]
                o = o + jnp.exp((p + 1.0) * lgf) * _bmm(qh[rows], s0_ref[0, h])
                o = o + jnp.exp((t - p) * lgb) * _bmm(qh[rows], s0_ref[0, N_HEADS + h])
            mu = jnp.mean(o, axis=-1, keepdims=True)
            oc = o - mu
            on = oc * lax.rsqrt(jnp.mean(oc * oc, axis=-1, keepdims=True) + EPS)
            o_ref[rows, hs] = on * _silu(r[rows, 3 * W_GROUP + h * HEAD_DIM:3 * W_GROUP + (h + 1) * HEAD_DIM])
        sf = _mm_tn((kh * jnp.exp((t - 1.0 - pos) * lgf)).astype(BF16), vh.astype(BF16))
        sb = _mm_tn((kh * jnp.exp(pos * lgb)).astype(BF16), vh.astype(BF16))
        if latent:
            sf = sf + jnp.exp(t * lgf) * s0_ref[0, h]
            sb = sb + jnp.exp(t * lgb) * s0_ref[0, N_HEADS + h]
        sfin_ref[0, h] = sf
        sfin_ref[0, N_HEADS + h] = sb


def _ret(r_all, logit, s0, cos, sin, t, n_seq, first_block, latent, into=None):
    return _seq_call(
        functools.partial(_ret_kernel, t=t, latent=latent), (r_all, logit, s0, cos, sin),
        in_specs=[pl.BlockSpec((t, 4 * W_GROUP), lambda i: (i + first_block, 0)),
                  pl.BlockSpec((1, LANES), lambda i: (0, 0)),
                  pl.BlockSpec((1, N_GATE, HEAD_DIM, HEAD_DIM), lambda i: (i, 0, 0, 0)),
                  pl.BlockSpec((t, W_GROUP), lambda i: (0, 0)),
                  pl.BlockSpec((t, W_GROUP), lambda i: (0, 0))],
        out_specs=[pl.BlockSpec((t, W_GROUP), lambda i: (i + first_block, 0)),
                   pl.BlockSpec((1, N_GATE, HEAD_DIM, HEAD_DIM), lambda i: (i, 0, 0, 0))],
        out_shape=[jax.ShapeDtypeStruct((N_TOK, W_GROUP), F32),
                   jax.ShapeDtypeStruct((n_seq, N_GATE, HEAD_DIM, HEAD_DIM), F32)],
        into=into, grid=(n_seq,), compiler_params=_cparams("arbitrary"), name="ret")


def _rope_tables(t):
    pos = np.arange(t)
    row = (pos // GRID_W).astype(np.float32)
    col = (pos % GRID_W).astype(np.float32)
    nf = HEAD_DIM // 4
    inv_freq = jnp.power(ROPE_BASE, -jnp.arange(nf, dtype=F32) / nf)
    ang_r = jnp.asarray(row)[:, None] * inv_freq[None, :]
    ang_c = jnp.asarray(col)[:, None] * inv_freq[None, :]
    cos = jnp.concatenate([jnp.cos(ang_r)] * 2 + [jnp.cos(ang_c)] * 2, axis=1)
    sin = jnp.concatenate([-jnp.sin(ang_r), jnp.sin(ang_r), -jnp.sin(ang_c), jnp.sin(ang_c)], axis=1)
    return jnp.tile(cos, (1, N_HEADS)), jnp.tile(sin, (1, N_HEADS))


def _head_rms(x, w):
    return x * lax.rsqrt(jnp.mean(x * x, axis=-1, keepdims=True) + EPS) * w


def _ctx_attn_kernel(n_ref, qw_ref, kw_ref, o_ref, k_out, v_out):
    n = n_ref[...]
    v_out[0] = n[:, 2 * W_GROUP:]
    for h in range(N_HEADS):
        hs = slice(h * HEAD_DIM, (h + 1) * HEAD_DIM)
        qh = _head_rms(n[:, hs], qw_ref[...])
        kh = _head_rms(n[:, W_GROUP + h * HEAD_DIM:W_GROUP + (h + 1) * HEAD_DIM], kw_ref[...])
        vh = n[:, 2 * W_GROUP + h * HEAD_DIM:2 * W_GROUP + (h + 1) * HEAD_DIM]
        k_out[0, :, hs] = kh
        s = _mm_nt(qh.astype(BF16), kh.astype(BF16)) * (HEAD_DIM ** -0.5)
        p = jnp.exp(s - jnp.max(s, axis=-1, keepdims=True))
        o_ref[:, hs] = _bmm(p, vh) / jnp.sum(p, axis=-1, keepdims=True)


def _ctx_attn(n_all, qw, kw, into):
    return _seq_call(
        _ctx_attn_kernel, (n_all, qw, kw),
        in_specs=[pl.BlockSpec((SEQ, 3 * W_GROUP), lambda i: (i, 0)),
                  pl.BlockSpec((1, HEAD_DIM), lambda i: (0, 0)),
                  pl.BlockSpec((1, HEAD_DIM), lambda i: (0, 0))],
        out_specs=[pl.BlockSpec((SEQ, W_GROUP), lambda i: (i, 0)),
                   pl.BlockSpec((1, SEQ, W_GROUP), lambda i: (i, 0, 0)),
                   pl.BlockSpec((1, SEQ, W_GROUP), lambda i: (i, 0, 0))],
        out_shape=[jax.ShapeDtypeStruct((N_TOK, W_GROUP), F32),
                   jax.ShapeDtypeStruct((BATCH, SEQ, W_GROUP), F32),
                   jax.ShapeDtypeStruct((BATCH, SEQ, W_GROUP), F32)],
        into=into, grid=(BATCH,), compiler_params=_cparams("arbitrary"), name="ctx_attn")


def _nat_kernel(n_ref, ck_ref, cv_ref, bias_ref, qw_ref, kw_ref, o_ref):
    h = pl.program_id(1)
    n = n_ref[...]
    sel = (_iota((W_GROUP, HEAD_DIM), 0) == _iota((W_GROUP, HEAD_DIM), 1) + h * HEAD_DIM).astype(F32)
    qh = _head_rms(_mm(n[:, :W_GROUP], sel, HI), qw_ref[...]) * (HEAD_DIM ** -0.5)
    kh = _head_rms(_mm(n[:, W_GROUP:2 * W_GROUP], sel, HI), kw_ref[...])
    vh = _mm(n[:, 2 * W_GROUP:], sel, HI)
    ckh = _mm(ck_ref[0], sel, HI)
    cvh = _mm(cv_ref[0], sel, HI)
    for qt in range(DEC_SEQ // Q_TILE):
        rows = slice(qt * Q_TILE, (qt + 1) * Q_TILE)
        s_loc = _mm_nt(qh[rows].astype(BF16), kh.astype(BF16)) + bias_ref[0, rows, :]
        s_ctx = _mm_nt(qh[rows].astype(BF16), ckh.astype(BF16))
        m = jnp.maximum(jnp.max(s_loc, axis=-1, keepdims=True), jnp.max(s_ctx, axis=-1, keepdims=True))
        p_loc = jnp.exp(s_loc - m)
        p_ctx = jnp.exp(s_ctx - m)
        den = jnp.sum(p_loc, axis=-1, keepdims=True) + jnp.sum(p_ctx, axis=-1, keepdims=True)
        o = (_bmm(p_loc, vh) + _bmm(p_ctx, cvh)) / den
        for hh in range(N_HEADS):
            @pl.when(h == hh)
            def _():
                o_ref[rows, hh * HEAD_DIM:(hh + 1) * HEAD_DIM] = o


def _nat(n_all, ck, cv, bias, qw, kw, into):
    first_block = N_CTX // DEC_SEQ
    return _seq_call(
        _nat_kernel, (n_all, ck, cv, bias, qw, kw),
        in_specs=[pl.BlockSpec((DEC_SEQ, 3 * W_GROUP), lambda b, h: (b + first_block, 0)),
                  pl.BlockSpec((1, PAST_LEN, W_GROUP), lambda b, h: (b, 0, 0)),
                  pl.BlockSpec((1, PAST_LEN, W_GROUP), lambda b, h: (b, 0, 0)),
                  pl.BlockSpec((1, DEC_SEQ, DEC_SEQ), lambda b, h: (h, 0, 0)),
                  pl.BlockSpec((1, HEAD_DIM), lambda b, h: (0, 0)),
                  pl.BlockSpec((1, HEAD_DIM), lambda b, h: (0, 0))],
        out_specs=pl.BlockSpec((DEC_SEQ, W_GROUP), lambda b, h: (b + first_block, 0)),
        out_shape=jax.ShapeDtypeStruct((N_TOK, W_GROUP), F32),
        into=into, grid=(DEC_BATCH, N_HEADS), compiler_params=_cparams("arbitrary", "arbitrary"), name="nat")


def _nat_bias(rpb):
    rows_n = DEC_SEQ // GRID_W
    kh = min(WIN_ROWS, rows_n)
    r = np.arange(rows_n)
    c = np.arange(GRID_W)
    r0 = np.clip(r - kh // 2, 0, rows_n - kh)
    c0 = np.clip(c - WIN_COLS // 2, 0, GRID_W - WIN_COLS)
    row_in = (r[None, :] >= r0[:, None]) & (r[None, :] < r0[:, None] + kh)
    col_in = (c[None, :] >= c0[:, None]) & (c[None, :] < c0[:, None] + WIN_COLS)
    row_idx = np.clip(r[None, :] - r[:, None] + WIN_ROWS - 1, 0, 2 * WIN_ROWS - 2)
    col_idx = np.clip(c[None, :] - c[:, None], -(WIN_COLS - 1), WIN_COLS - 1) + WIN_COLS - 1
    row_hot = (row_idx[..., None] == np.arange(2 * WIN_ROWS - 1)).astype(np.float32)
    col_hot = (col_idx[..., None] == np.arange(2 * WIN_COLS - 1)).astype(np.float32)
    bias = jnp.einsum('hab,rsa,qkb->hrqsk', rpb.astype(F32), row_hot, col_hot, precision=HI)
    inside = row_in[:, None, :, None] & col_in[None, :, None, :]
    return jnp.where(inside[None], bias, NEG_BIG).reshape(N_HEADS, DEC_SEQ, DEC_SEQ)


ROUTE_W = D_MODEL + LANES
GSEL_LANE = N_EXPERTS


def _pack_router(we, be, wg, bg):
    pad = LANES - N_EXPERTS - N_GROUPS
    w = jnp.concatenate([we, wg, jnp.zeros((D_MODEL, pad), F32)], axis=1)
    b = jnp.concatenate([be, bg, jnp.zeros((pad,), F32)]).reshape(1, LANES)
    return w, b


def _lane_min_where(mask, lane):
    return jnp.min(jnp.where(mask, lane, LANES), axis=-1, keepdims=True)


def _outproj_kernel(x_ref, m0, m1, m2, m3, mod_ref, nw_ref, w_ref, rw_ref, rb_ref, x_out, route_out):
    acc = None
    for i, m_ref in enumerate((m0, m1, m2, m3)):
        part = _mm(m_ref[...].astype(BF16), w_ref[i * W_GROUP:(i + 1) * W_GROUP, :])
        acc = part if acc is None else acc + part
    x = x_ref[...] + mod_ref[0, 2:3, :] * acc
    x_out[...] = x
    y = x * lax.rsqrt(jnp.mean(x * x, axis=-1, keepdims=True) + EPS) * nw_ref[...]
    hf = y * (1.0 + mod_ref[0, 4:5, :]) + mod_ref[0, 3:4, :]
    route_out[:, :D_MODEL] = hf

    logits = _mm(hf, rw_ref[...], HI) + rb_ref[...]
    lane = _iota(logits.shape, 1)
    is_g = (lane >= N_EXPERTS) & (lane < N_EXPERTS + N_GROUPS)
    gl = jnp.where(is_g, logits, NEG_BIG)
    ge = jnp.exp(gl - jnp.max(gl, axis=-1, keepdims=True))
    gp = jnp.where(is_g, ge / jnp.sum(ge, axis=-1, keepdims=True), -1.0)
    gw = jnp.max(gp, axis=-1, keepdims=True)
    gsel = _lane_min_where(gp == gw, lane) - N_EXPERTS
    in_grp = (lane // EXPERTS_PER_GROUP == gsel) & (lane < N_EXPERTS)
    el = jnp.where(in_grp, logits, NEG_BIG)
    ee = jnp.exp(el - jnp.max(el, axis=-1, keepdims=True))
    ep = jnp.where(in_grp, ee / jnp.sum(ee, axis=-1, keepdims=True), -1.0)
    t1 = jnp.max(ep, axis=-1, keepdims=True)
    i1 = _lane_min_where(ep == t1, lane)
    ep2 = jnp.where(lane == i1, -1.0, ep)
    t2 = jnp.max(ep2, axis=-1, keepdims=True)
    i2 = _lane_min_where(ep2 == t2, lane)
    tsum = t1 + t2
    combine = jnp.where(lane == i1, gw * (t1 / tsum), 0.0) + jnp.where(lane == i2, gw * (t2 / tsum), 0.0)
    route_out[:, D_MODEL:] = jnp.where(lane == GSEL_LANE, gsel.astype(F32), combine)


def _outproj(x, mixed, mod, norm_w, w_out, rw, rb):
    tile = lambda w: pl.BlockSpec((TOK_TILE, w), lambda i: (i, 0))
    whole = lambda a: pl.BlockSpec(a.shape, lambda i: (0,) * a.ndim)
    return pl.pallas_call(
        _outproj_kernel,
        grid=(N_TOK // TOK_TILE,),
        in_specs=[tile(D_MODEL)] + [tile(W_GROUP)] * 4
                 + [pl.BlockSpec((1, SUBLANES, D_MODEL), lambda i: (_cond_of_tile(i), 0, 0)),
                    pl.BlockSpec((1, D_MODEL), lambda i: (0, 0)), whole(w_out), whole(rw), whole(rb)],
        out_specs=[tile(D_MODEL), tile(ROUTE_W)],
        out_shape=[jax.ShapeDtypeStruct((N_TOK, D_MODEL), F32), jax.ShapeDtypeStruct((N_TOK, ROUTE_W), F32)],
        compiler_params=_cparams("arbitrary"),
        name="outproj",
    )(x, *mixed, mod, norm_w.reshape(1, D_MODEL), w_out, rw, rb)


def _moe_kernel(tile_group, row_token, route_hbm, wg_ref, wu_ref, wd_ref, y_hbm, xbuf, ybuf, sem_in, sem_out):
    i = pl.program_id(0)
    g = tile_group[i]
    base = i * MOE_TILE

    def gather_copy(r):
        tok = jnp.minimum(row_token[base + r], N_TOK - 1)
        return pltpu.make_async_copy(route_hbm.at[pl.ds(tok, 1)], xbuf.at[pl.ds(r, 1)], sem_in)

    def scatter_copy(r):
        return pltpu.make_async_copy(ybuf.at[pl.ds(r, 1)], y_hbm.at[pl.ds(row_token[base + r], 1)], sem_out)

    @pl.when(i == 0)
    def _():
        ybuf[...] = jnp.zeros_like(ybuf)
        spare = pltpu.make_async_copy(ybuf, y_hbm.at[pl.ds(N_TOK, MOE_TILE)], sem_out)
        spare.start()
        spare.wait()

    @pl.when(row_token[base] < N_TOK)
    def _():
        @pl.loop(0, MOE_TILE, step=DMA_UNROLL)
        def _(r0):
            for u in range(DMA_UNROLL):
                gather_copy(r0 + u).start()

        pltpu.make_async_copy(route_hbm.at[pl.ds(0, MOE_TILE)], xbuf, sem_in).wait()

        x = xbuf[:, :D_MODEL].astype(BF16)
        route = xbuf[:, D_MODEL:]
        lane = _iota(route.shape, 1)
        acc = jnp.zeros((MOE_TILE, D_MODEL), F32)
        for e in range(EXPERTS_PER_GROUP):
            cw = jnp.sum(jnp.where(lane == g * EXPERTS_PER_GROUP + e, route, 0.0), axis=-1, keepdims=True)
            act = _silu(_mm(x, wg_ref[0, e])) * _mm(x, wu_ref[0, e]) * cw
            acc = acc + _mm(act.astype(BF16), wd_ref[0, e])
        ybuf[...] = acc

        @pl.loop(0, MOE_TILE, step=DMA_UNROLL)
        def _(r0):
            for u in range(DMA_UNROLL):
                scatter_copy(r0 + u).start()

        pltpu.make_async_copy(ybuf, y_hbm.at[pl.ds(0, MOE_TILE)], sem_out).wait()


def _moe(route, tile_group, row_token, wg, wu, wd):
    n_tiles = MOE_ROWS // MOE_TILE
    grid_spec = pltpu.PrefetchScalarGridSpec(
        num_scalar_prefetch=2,
        grid=(n_tiles,),
        in_specs=[pl.BlockSpec(memory_space=pl.ANY),
                  pl.BlockSpec((1, EXPERTS_PER_GROUP, D_MODEL, EXPERT_FF), lambda i, tg, rt: (tg[i], 0, 0, 0)),
                  pl.BlockSpec((1, EXPERTS_PER_GROUP, D_MODEL, EXPERT_FF), lambda i, tg, rt: (tg[i], 0, 0, 0)),
                  pl.BlockSpec((1, EXPERTS_PER_GROUP, EXPERT_FF, D_MODEL), lambda i, tg, rt: (tg[i], 0, 0, 0))],
        out_specs=pl.BlockSpec(memory_space=pl.ANY),
        scratch_shapes=[pltpu.VMEM((MOE_TILE, ROUTE_W), F32), pltpu.VMEM((MOE_TILE, D_MODEL), F32),
                        pltpu.SemaphoreType.DMA, pltpu.SemaphoreType.DMA],
    )
    return pl.pallas_call(
        _moe_kernel,
        grid_spec=grid_spec,
        out_shape=jax.ShapeDtypeStruct((N_TOK + MOE_TILE, D_MODEL), F32),
        compiler_params=_cparams("arbitrary"),
        name="moe",
    )(tile_group, row_token, route, wg, wu, wd)


def _moe_schedule(gsel):
    onehot = (gsel[:, None] == jnp.arange(N_GROUPS)[None, :]).astype(jnp.int32)
    rank = jnp.take_along_axis(jnp.cumsum(onehot, axis=0) - onehot, gsel[:, None], axis=1)[:, 0]
    counts = jnp.sum(onehot, axis=0)
    tiles = (counts + MOE_TILE - 1) // MOE_TILE
    tile_start = jnp.cumsum(tiles) - tiles
    dest = tile_start[gsel] * MOE_TILE + rank
    spare = N_TOK + jnp.arange(MOE_ROWS, dtype=jnp.int32) % MOE_TILE
    row_token = spare.at[dest].set(jnp.arange(N_TOK, dtype=jnp.int32))
    tile_idx = jnp.arange(MOE_ROWS // MOE_TILE)
    tile_group = jnp.clip(jnp.sum(tile_idx[:, None] >= tile_start[None, :], axis=1) - 1, 0, N_GROUPS - 1)
    return tile_group.astype(jnp.int32), row_token


def _resid_kernel(x_ref, y_ref, mod_ref, o_ref):
    o_ref[...] = x_ref[...] + mod_ref[0, 5:6, :] * y_ref[...]


def _resid(x, y, mod):
    tile = pl.BlockSpec((TOK_TILE, D_MODEL), lambda i: (i, 0))
    return pl.pallas_call(
        _resid_kernel,
        grid=(N_TOK // TOK_TILE,),
        in_specs=[tile, tile, pl.BlockSpec((1, SUBLANES, D_MODEL), lambda i: (_cond_of_tile(i), 0, 0))],
        out_specs=tile,
        out_shape=jax.ShapeDtypeStruct((N_TOK, D_MODEL), F32),
        compiler_params=_cparams("arbitrary"),
        name="resid",
    )(x, y, mod)


def _lane_row(v):
    v = v.reshape(-1).astype(F32)
    return jnp.concatenate([v, jnp.zeros((LANES - v.shape[0],), F32)]).reshape(1, LANES)


def _pad_rows(w):
    return jnp.concatenate([w, jnp.zeros((SUBLANES - w.shape[0], w.shape[1]), w.dtype)], axis=0)


def kernel(x_prompt, x_sample, state_gdn, state_ret, cache_nat_k, cache_nat_v, c, c_ctx, ada_w, ada_b, norm_mix_w, norm_ffn_w, w_in, gdn_conv_w, gdn_a_log, gdn_dt_bias, gdn_norm_w, ret_gamma_logit, nat_q_norm_w, nat_k_norm_w, nat_rpb, sc_conv_w, w_out, router_group_w, router_group_b, router_expert_w, router_expert_b, moe_w_gate, moe_w_up, moe_w_down):
    x = jnp.concatenate([x_prompt.reshape(N_CTX, D_MODEL), x_sample.reshape(N_LAT, D_MODEL)], axis=0)
    cond = jnp.concatenate([c_ctx[None, :], c, jnp.zeros((SUBLANES - 1 - DEC_BATCH, D_MODEL), F32)], axis=0)
    ada = _ada(cond, ada_w, ada_b).reshape(DEPTH, SUBLANES, 6, D_MODEL)
    cos, sin = _rope_tables(DEC_SEQ)
    zero_state = jnp.zeros((BATCH, N_GATE, HEAD_DIM, HEAD_DIM), F32)
    lat_block = N_CTX // DEC_SEQ
    gdn_list, ret_list, k_list, v_list = [], [], [], []
    mixed = [jnp.zeros((N_TOK, W_GROUP), F32) for _ in range(4)]
    for l in range(DEPTH):
        mod = jnp.concatenate([ada[l, :1 + DEC_BATCH], jnp.zeros((1 + DEC_BATCH, SUBLANES - 6, D_MODEL), F32)], axis=1)
        a_gdn, a_ret, a_nat, a_sc, a_gate = _inproj(x, mod, norm_mix_w[l], _pack_w_in(w_in[l]))

        conv_w = _pad_rows(gdn_conv_w[l])
        a_log, dt_b = _lane_row(gdn_a_log[l]), _lane_row(gdn_dt_bias[l])
        gnw = jnp.tile(gdn_norm_w[l], N_HEADS).reshape(1, W_GROUP)
        o_gdn, s_gdn = _gdn(a_gdn, a_gate, conv_w, a_log, dt_b, gnw, zero_state, SEQ, BATCH, 0, into=mixed[0])
        s0 = state_gdn[:, l].reshape(DEC_BATCH, N_GATE, HEAD_DIM, HEAD_DIM)
        o_gdn, _ = _gdn(a_gdn, a_gate, conv_w, a_log, dt_b, gnw, s0, DEC_SEQ, DEC_BATCH, lat_block, into=o_gdn)

        logit = _lane_row(ret_gamma_logit[l])
        o_ret, s_ret = _ret(a_ret, logit, zero_state, cos[:SEQ], sin[:SEQ], SEQ, BATCH, 0, False, into=mixed[1])
        s0 = state_ret[:, l].reshape(DEC_BATCH, N_GATE, HEAD_DIM, HEAD_DIM)
        o_ret, _ = _ret(a_ret, logit, s0, cos, sin, DEC_SEQ, DEC_BATCH, lat_block, True, into=o_ret)

        qw, kw = nat_q_norm_w[l].reshape(1, HEAD_DIM), nat_k_norm_w[l].reshape(1, HEAD_DIM)
        o_nat, n_k, n_v = _ctx_attn(a_nat, qw, kw, mixed[2])
        o_nat = _nat(a_nat, cache_nat_k[:, l].reshape(DEC_BATCH, PAST_LEN, W_GROUP),
                     cache_nat_v[:, l].reshape(DEC_BATCH, PAST_LEN, W_GROUP), _nat_bias(nat_rpb[l]), qw, kw, o_nat)

        sc_w = _pad_rows(sc_conv_w[l])
        o_sc = _sconv(a_sc, sc_w, SEQ, BATCH, 0, into=mixed[3])
        o_sc = _sconv(a_sc, sc_w, DEC_SEQ, DEC_BATCH, lat_block, into=o_sc)

        mixed = [o_gdn, o_ret, o_nat, o_sc]
        rw, rb = _pack_router(router_expert_w[l], router_expert_b[l], router_group_w[l], router_group_b[l])
        x_mid, route = _outproj(x, mixed, mod, norm_ffn_w[l], w_out[l].astype(BF16), rw, rb)

        tile_group, row_token = _moe_schedule(route[:, D_MODEL + GSEL_LANE].astype(jnp.int32))
        to_group = lambda w: w.astype(BF16).reshape((N_GROUPS, EXPERTS_PER_GROUP) + w.shape[1:])
        y = _moe(route, tile_group, row_token, to_group(moe_w_gate[l]), to_group(moe_w_up[l]),
                 to_group(moe_w_down[l]))
        x = _resid(x_mid, y, mod)

        gdn_list.append(s_gdn.reshape(BATCH, 2, N_HEADS, HEAD_DIM, HEAD_DIM))
        ret_list.append(s_ret.reshape(BATCH, 2, N_HEADS, HEAD_DIM, HEAD_DIM))
        k_list.append(n_k.reshape(BATCH, SEQ, N_HEADS, HEAD_DIM))
        v_list.append(n_v.reshape(BATCH, SEQ, N_HEADS, HEAD_DIM))
    return (x[:N_CTX].reshape(BATCH, SEQ, D_MODEL), x[N_CTX:].reshape(DEC_BATCH, DEC_SEQ, D_MODEL),
            jnp.stack(gdn_list, axis=1), jnp.stack(ret_list, axis=1),
            jnp.stack(k_list, axis=1), jnp.stack(v_list, axis=1))
```

```python
import functools

import numpy as np
import jax
import jax.numpy as jnp
from jax import lax
from jax.experimental import pallas as pl
from jax.experimental.pallas import tpu as pltpu

D_MODEL = 1024
BATCH = 16
SEQ = 256
DEPTH = 2
DEC_BATCH = 2
DEC_SEQ = 1024
PAST_LEN = 256
GRID_W = 64
HEAD_DIM = 64
W_GROUP = D_MODEL // 4
N_HEADS = W_GROUP // HEAD_DIM
CHUNK = 64
WIN_ROWS = 8
WIN_COLS = 16
ROPE_BASE = 10000.0
N_GROUPS = 4
EXPERTS_PER_GROUP = 8
N_EXPERTS = N_GROUPS * EXPERTS_PER_GROUP
EXPERT_FF = 256
GROUP_FF = EXPERTS_PER_GROUP * EXPERT_FF
EPS = 1e-6

N_CTX = BATCH * SEQ
N_LAT = DEC_BATCH * DEC_SEQ
N_TOK = N_CTX + N_LAT
LANES = 128
SUBLANES = 8
TOK_TILE = 512
MOE_TILE = 256
Q_TILE = 256
VMEM_LIMIT = 48 * 1024 * 1024
NEG_BIG = -1e30
N_GATE = 2 * N_HEADS

F32 = jnp.float32
BF16 = jnp.bfloat16
HI = lax.Precision.HIGHEST


def _mm(a, b, prec=None):
    return lax.dot_general(a, b, (((1,), (0,)), ((), ())), precision=prec, preferred_element_type=F32)


def _mm_nt(a, b, prec=None):
    return lax.dot_general(a, b, (((1,), (1,)), ((), ())), precision=prec, preferred_element_type=F32)


def _mm_tn(a, b, prec=None):
    return lax.dot_general(a, b, (((0,), (0,)), ((), ())), precision=prec, preferred_element_type=F32)


def _bmm(a, b):
    return _mm(a.astype(BF16), b.astype(BF16))


def _sigmoid(x):
    return 1.0 / (1.0 + jnp.exp(-x))


def _silu(x):
    return x * _sigmoid(x)


def _softplus(x):
    return jnp.maximum(x, 0.0) + jnp.log(1.0 + jnp.exp(-jnp.abs(x)))


def _iota(shape, dim):
    return lax.broadcasted_iota(jnp.int32, shape, dim)


def _cparams(*sem):
    return pltpu.CompilerParams(dimension_semantics=sem, vmem_limit_bytes=VMEM_LIMIT)


def _cond_of_tile(i):
    n_ctx_tiles = N_CTX // TOK_TILE
    return jnp.where(i < n_ctx_tiles, 0, 1 + (i - n_ctx_tiles) // (DEC_SEQ // TOK_TILE))


def _head_sum_matrix():
    return (_iota((W_GROUP, W_GROUP), 0) // HEAD_DIM == _iota((W_GROUP, W_GROUP), 1) // HEAD_DIM).astype(F32)


def _ada_kernel(c_ref, w_ref, b_ref, o_ref):
    o_ref[0] = _mm(_silu(c_ref[...]), w_ref[0], HI) + b_ref[0]


def _ada(cond, ada_w, ada_b):
    tn = 1536
    n_out = 6 * D_MODEL
    return pl.pallas_call(
        _ada_kernel,
        grid=(DEPTH, n_out // tn),
        in_specs=[pl.BlockSpec((SUBLANES, D_MODEL), lambda l, j: (0, 0)),
                  pl.BlockSpec((1, D_MODEL, tn), lambda l, j: (l, 0, j)),
                  pl.BlockSpec((1, 1, tn), lambda l, j: (l, 0, j))],
        out_specs=pl.BlockSpec((1, SUBLANES, tn), lambda l, j: (l, 0, j)),
        out_shape=jax.ShapeDtypeStruct((DEPTH, SUBLANES, n_out), F32),
        compiler_params=_cparams("arbitrary", "arbitrary"),
        name="ada",
    )(cond, ada_w, ada_b.reshape(DEPTH, 1, n_out))


IN_WIDTHS = (4 * W_GROUP, 4 * W_GROUP, 3 * W_GROUP, 3 * W_GROUP, LANES)
IN_PACKED = sum(IN_WIDTHS)


def _pack_w_in(w):
    sizes = (3 * W_GROUP, W_GROUP, N_GATE, N_GATE, 3 * W_GROUP, W_GROUP, 3 * W_GROUP, 3 * W_GROUP)
    offs = np.concatenate([[0], np.cumsum(sizes)])
    seg = [w[:, offs[i]:offs[i + 1]] for i in range(len(sizes))]
    gate = jnp.concatenate([seg[2], seg[3], jnp.zeros((D_MODEL, LANES - 2 * N_GATE), w.dtype)], axis=1)
    return jnp.concatenate([seg[0], seg[1], seg[4], seg[5], seg[6], seg[7], gate], axis=1).astype(BF16)


def _inproj_kernel(x_ref, mod_ref, nw_ref, w_ref, *o_refs):
    x = x_ref[...]
    y = x * lax.rsqrt(jnp.mean(x * x, axis=-1, keepdims=True) + EPS) * nw_ref[...]
    h = (y * (1.0 + mod_ref[0, 1:2, :]) + mod_ref[0, 0:1, :]).astype(BF16)
    off = 0
    for o_ref, width in zip(o_refs, IN_WIDTHS):
        o_ref[...] = _mm(h, w_ref[:, off:off + width])
        off += width


def _inproj(x, mod, norm_w, w_packed):
    return pl.pallas_call(
        _inproj_kernel,
        grid=(N_TOK // TOK_TILE,),
        in_specs=[pl.BlockSpec((TOK_TILE, D_MODEL), lambda i: (i, 0)),
                  pl.BlockSpec((1, SUBLANES, D_MODEL), lambda i: (_cond_of_tile(i), 0, 0)),
                  pl.BlockSpec((1, D_MODEL), lambda i: (0, 0)),
                  pl.BlockSpec((D_MODEL, IN_PACKED), lambda i: (0, 0))],
        out_specs=[pl.BlockSpec((TOK_TILE, w), lambda i: (i, 0)) for w in IN_WIDTHS],
        out_shape=[jax.ShapeDtypeStruct((N_TOK, w), F32) for w in IN_WIDTHS],
        compiler_params=_cparams("arbitrary"),
        name="inproj",
    )(x, mod, norm_w.reshape(1, D_MODEL), w_packed)


def _seq_call(kernel_fn, args, in_specs, out_specs, out_shape, into, **kwargs):
    aliases = {}
    if into is not None:
        n_in = len(args)
        inner = kernel_fn
        kernel_fn = lambda *refs: inner(*refs[:n_in], *refs[n_in + 1:])
        aliases = {n_in: 0}
        args = list(args) + [into]
        in_specs = list(in_specs) + [pl.BlockSpec(memory_space=pl.ANY)]
    return pl.pallas_call(kernel_fn, in_specs=in_specs, out_specs=out_specs, out_shape=out_shape,
                          input_output_aliases=aliases, **kwargs)(*args)


def _shift_rows(p, t):
    row = _iota(p.shape, 0)
    prev = jnp.where(row == 0, 0.0, pltpu.roll(p, 1, 0))
    nxt = jnp.where(row == t - 1, 0.0, pltpu.roll(p, t - 1, 0))
    return prev, nxt


def _conv3(x, w_ref, t):
    prev, nxt = _shift_rows(x, t)
    return w_ref[0:1, :] * prev + w_ref[1:2, :] * x + w_ref[2:3, :] * nxt


def _sconv_kernel(s_ref, w_ref, o_ref, *, t):
    s = s_ref[...]
    p = s[:, W_GROUP:2 * W_GROUP] * s[:, 2 * W_GROUP:]
    o_ref[...] = s[:, :W_GROUP] * _conv3(p, w_ref, t)


def _sconv(s_all, w, t, n_seq, first_block, into=None):
    return _seq_call(
        functools.partial(_sconv_kernel, t=t), (s_all, w),
        in_specs=[pl.BlockSpec((t, 3 * W_GROUP), lambda i: (i + first_block, 0)),
                  pl.BlockSpec((SUBLANES, W_GROUP), lambda i: (0, 0))],
        out_specs=pl.BlockSpec((t, W_GROUP), lambda i: (i + first_block, 0)),
        out_shape=jax.ShapeDtypeStruct((N_TOK, W_GROUP), F32),
        into=into, grid=(n_seq,), compiler_params=_cparams("arbitrary"), name="sconv")


GDN_GROUP_CHUNKS = 4
GDN_CHAINS = GDN_GROUP_CHUNKS * N_GATE
GDN_PAIRS = GDN_GROUP_CHUNKS * N_HEADS


def _gdn_kernel(a_ref, gate_ref, convw_ref, alog_ref, dtb_ref, nw_ref, s0_ref, o_ref, sfin_ref,
                q_s, kv_s, kt_s, gc_s, eg_s, beta_s, gcrow_s, ekdrow_s, cdec_s, uo_s, wq_s, attn_s, kdt_s,
                st_s, wsqs_s, kk_s, d_s, off_s, m1_s, m2_s, p_s, m3_s, rhs_s, px_s, *, t):
    n_chunks = t // CHUNK
    a = a_ref[...]
    qkv = _silu(_conv3(a[:, :3 * W_GROUP], convw_ref, t))
    q = qkv[:, :W_GROUP]
    k = qkv[:, W_GROUP:2 * W_GROUP]
    v = qkv[:, 2 * W_GROUP:]
    head_sum = _head_sum_matrix()
    q = q * lax.rsqrt(_mm(q * q, head_sum, HI) + EPS) * (HEAD_DIM ** -0.5)
    k = k * lax.rsqrt(_mm(k * k, head_sum, HI) + EPS)
    for h in range(N_HEADS):
        hs = slice(h * HEAD_DIM, (h + 1) * HEAD_DIM)
        q_s[h] = q[:, hs]
        kv_s[h] = jnp.concatenate([k[:, hs], v[:, hs]], axis=1)
    k_t = k.T
    for c in range(n_chunks):
        kt_s[c] = k_t[:, c * CHUNK:(c + 1) * CHUNK]

    gates = gate_ref[...]
    log_a = -jnp.exp(alog_ref[...]) * _softplus(gates + dtb_ref[...])
    beta_s[...] = _sigmoid(gates)

    ci = _iota((CHUNK, CHUNK), 0)
    cj = _iota((CHUNK, CHUNK), 1)
    eye = (ci == cj).astype(F32)
    blk_mask = (ci // 16) == (cj // 16)
    low_half = _iota((CHUNK, 2 * HEAD_DIM), 1) < HEAD_DIM

    tri_f = (cj <= ci).astype(F32)
    tri_b = (cj >= ci).astype(F32)
    ones = jnp.ones((CHUNK, CHUNK), F32)
    fwd_lane = _iota((CHUNK, LANES), 1) < N_HEADS
    gate_rows = (_iota((N_GATE, LANES), 0) == _iota((N_GATE, LANES), 1)).astype(F32)
    for c in range(n_chunks):
        rows = slice(c * CHUNK, (c + 1) * CHUNK)
        g = log_a[rows, :]
        gc = jnp.where(fwd_lane, _mm(tri_f, g, HI), _mm(tri_b, g, HI))
        gt = _mm(ones, g, HI)
        gc_s[rows, :] = gc
        eg_s[rows, :] = jnp.exp(gc)
        gcrow_s[c] = _mm_nt(gate_rows, gc, HI)
        ekdrow_s[c] = jnp.exp(_mm_nt(gate_rows, gt - gc, HI))
        cdec = jnp.exp(_mm_nt(gate_rows, gt, HI))
        cdec_s[c] = jnp.concatenate([cdec, cdec], axis=1)

    def solve_group(grp, carry):
        row0 = grp * (GDN_GROUP_CHUNKS * CHUNK)
        chains = [(cl, a_idx) for cl in range(GDN_GROUP_CHUNKS) for a_idx in range(N_GATE)]

        def rows_of(cl):
            return pl.ds(pl.multiple_of(row0 + cl * CHUNK, CHUNK), CHUNK)

        for cl in range(GDN_GROUP_CHUNKS):
            for h in range(N_HEADS):
                rows = rows_of(cl)
                kq = jnp.concatenate([kv_s[h, rows, :HEAD_DIM], q_s[h, rows, :]], axis=0)
                k_t_h = kt_s[grp * GDN_GROUP_CHUNKS + cl, h * HEAD_DIM:(h + 1) * HEAD_DIM, :]
                kk_s[cl * N_HEADS + h] = _bmm(kq, k_t_h)
        for b, (cl, a_idx) in enumerate(chains):
            backward = a_idx >= N_HEADS
            h = a_idx % N_HEADS
            rows, c = rows_of(cl), grp * GDN_GROUP_CHUNKS + cl
            incl = (cj >= ci) if backward else (cj <= ci)
            strict = (cj > ci) if backward else (cj < ci)
            bt = beta_s[rows, N_GATE + a_idx:N_GATE + a_idx + 1]
            decay = jnp.exp(jnp.where(incl, gc_s[rows, a_idx:a_idx + 1] - gcrow_s[c, a_idx:a_idx + 1, :], NEG_BIG))
            low = jnp.where(strict, kk_s[cl * N_HEADS + h, :CHUNK, :] * bt * decay, 0.0)
            attn_s[a_idx, rows, :] = (kk_s[cl * N_HEADS + h, CHUNK:, :] * decay).astype(BF16)
            d = jnp.where(blk_mask, low, 0.0)
            d_s[b] = d
            off_s[b] = low - d
            rhs_s[b] = kv_s[h, rows, :] * bt * jnp.where(low_half, eg_s[rows, a_idx:a_idx + 1], 1.0)
        for b in range(GDN_CHAINS):
            m1_s[b] = _bmm(d_s[b], d_s[b])
        for b in range(GDN_CHAINS):
            d, d2 = d_s[b], m1_s[b]
            m2_s[b] = _bmm(d2, d2)
            p_s[b] = eye - d + d2 - _bmm(d, d2)
        for b in range(GDN_CHAINS):
            d4, p = m2_s[b], p_s[b]
            m3_s[b] = _bmm(d4, d4)
            p_s[b] = p + _bmm(p, d4)
        for b in range(GDN_CHAINS):
            p = p_s[b]
            p_s[b] = p + _bmm(p, m3_s[b])
        for b in range(GDN_CHAINS):
            p = p_s[b]
            m1_s[b] = _bmm(p, off_s[b])
            px_s[b] = _bmm(p, rhs_s[b])
        for b in range(GDN_CHAINS):
            m2_s[b] = _bmm(m1_s[b], m1_s[b])
        for b in range(GDN_CHAINS):
            n, n2 = m1_s[b], m2_s[b]
            m3_s[b] = eye - n + n2 - _bmm(n, n2)
        for b, (cl, a_idx) in enumerate(chains):
            h = a_idx % N_HEADS
            rows, c = rows_of(cl), grp * GDN_GROUP_CHUNKS + cl
            wu = _bmm(m3_s[b], px_s[b])
            uo_s[a_idx, rows, :] = wu
            wq_s[a_idx, c, :CHUNK, :] = wu[:, :HEAD_DIM].astype(BF16)
            wq_s[a_idx, c, CHUNK:, :] = (q_s[h, rows, :] * eg_s[rows, a_idx:a_idx + 1]).astype(BF16)
            k_t_h = kt_s[c, h * HEAD_DIM:(h + 1) * HEAD_DIM, :]
            kdt_s[a_idx, c] = (k_t_h * ekdrow_s[c, a_idx:a_idx + 1, :]).astype(BF16)
        return carry

    lax.fori_loop(0, n_chunks // GDN_GROUP_CHUNKS, solve_group, 0)

    for i in range(N_GATE):
        st_s[i] = jnp.concatenate([jnp.zeros((HEAD_DIM, HEAD_DIM), F32), s0_ref[0, i]], axis=1)

    def scan_chunk(c, carry):
        def chunk_of(a_idx):
            return (n_chunks - 1 - c) if a_idx >= N_HEADS else c

        for a_idx in range(N_GATE):
            wsqs_s[a_idx] = _mm(wq_s[a_idx, chunk_of(a_idx)], st_s[a_idx].astype(BF16))
        for a_idx in range(N_GATE):
            cc = chunk_of(a_idx)
            rows = pl.ds(pl.multiple_of(cc * CHUNK, CHUNK), CHUNK)
            v_new = (uo_s[a_idx, rows, :] - wsqs_s[a_idx, :CHUNK, :]).astype(BF16)
            uo_s[a_idx, rows, :] = wsqs_s[a_idx, CHUNK:, :] + _mm(attn_s[a_idx, rows, :], v_new)
            st_s[a_idx] = st_s[a_idx] * cdec_s[cc, a_idx:a_idx + 1, :] + _mm(kdt_s[a_idx, cc], v_new)
        return carry

    lax.fori_loop(0, n_chunks, scan_chunk, 0)
    for i in range(N_GATE):
        sfin_ref[0, i] = st_s[i, :, HEAD_DIM:]

    o = jnp.concatenate([(uo_s[h] + uo_s[N_HEADS + h])[:, HEAD_DIM:] for h in range(N_HEADS)], axis=1)
    ms = _mm(o * o, head_sum, HI) * (1.0 / HEAD_DIM)
    o_ref[...] = o * lax.rsqrt(ms + EPS) * nw_ref[...] * _silu(a[:, 3 * W_GROUP:])


def _gdn(a_all, gate_all, conv_w, a_log, dt_bias, norm_w, s0, t, n_seq, first_block, into=None):
    small = lambda: pl.BlockSpec((1, LANES), lambda i: (0, 0))
    n_chunks = t // CHUNK
    wide = 2 * HEAD_DIM
    scratch = [pltpu.VMEM((N_HEADS, t, HEAD_DIM), F32),
               pltpu.VMEM((N_HEADS, t, wide), F32),
               pltpu.VMEM((n_chunks, W_GROUP, CHUNK), F32),
               pltpu.VMEM((t, LANES), F32), pltpu.VMEM((t, LANES), F32), pltpu.VMEM((t, LANES), F32),
               pltpu.VMEM((n_chunks, N_GATE, CHUNK), F32), pltpu.VMEM((n_chunks, N_GATE, CHUNK), F32),
               pltpu.VMEM((n_chunks, N_GATE, wide), F32),
               pltpu.VMEM((N_GATE, t, wide), F32),
               pltpu.VMEM((N_GATE, n_chunks, 2 * CHUNK, HEAD_DIM), BF16),
               pltpu.VMEM((N_GATE, t, CHUNK), BF16),
               pltpu.VMEM((N_GATE, n_chunks, HEAD_DIM, CHUNK), BF16),
               pltpu.VMEM((N_GATE, HEAD_DIM, wide), F32),
               pltpu.VMEM((N_GATE, 2 * CHUNK, wide), F32),
               pltpu.VMEM((GDN_PAIRS, 2 * CHUNK, CHUNK), F32)]
    scratch += [pltpu.VMEM((GDN_CHAINS, CHUNK, CHUNK), F32)] * 6
    scratch += [pltpu.VMEM((GDN_CHAINS, CHUNK, wide), F32)] * 2
    return _seq_call(
        functools.partial(_gdn_kernel, t=t), (a_all, gate_all, conv_w, a_log, dt_bias, norm_w, s0),
        in_specs=[pl.BlockSpec((t, 4 * W_GROUP), lambda i: (i + first_block, 0)),
                  pl.BlockSpec((t, LANES), lambda i: (i + first_block, 0)),
                  pl.BlockSpec((SUBLANES, 3 * W_GROUP), lambda i: (0, 0)),
                  small(), small(),
                  pl.BlockSpec((1, W_GROUP), lambda i: (0, 0)),
                  pl.BlockSpec((1, N_GATE, HEAD_DIM, HEAD_DIM), lambda i: (i, 0, 0, 0))],
        out_specs=[pl.BlockSpec((t, W_GROUP), lambda i: (i + first_block, 0)),
                   pl.BlockSpec((1, N_GATE, HEAD_DIM, HEAD_DIM), lambda i: (i, 0, 0, 0))],
        out_shape=[jax.ShapeDtypeStruct((N_TOK, W_GROUP), F32),
                   jax.ShapeDtypeStruct((n_seq, N_GATE, HEAD_DIM, HEAD_DIM), F32)],
        scratch_shapes=scratch,
        into=into, grid=(n_seq,), compiler_params=_cparams("arbitrary"), name="gdn")


def _swap16(x):
    width = x.shape[-1]
    first = (_iota(x.shape, 1) // 16) % 2 == 0
    return jnp.where(first, pltpu.roll(x, width - 16, 1), pltpu.roll(x, 16, 1))


def _ret_kernel(r_ref, lg_ref, s0_ref, cos_ref, sin_ref, o_ref, sfin_ref, *, t, latent):
    r = r_ref[...]
    q = r[:, :W_GROUP]
    k = r[:, W_GROUP:2 * W_GROUP]
    v = r[:, 2 * W_GROUP:3 * W_GROUP]
    if latent:
        q = q * cos_ref[...] + _swap16(q) * sin_ref[...]
        k = k * cos_ref[...] + _swap16(k) * sin_ref[...]
    k = k * (HEAD_DIM ** -0.5)
    logit = lg_ref[...]
    lg = -_softplus(-logit)
    pos = _iota((t, 1), 0).astype(F32)
    for h in range(N_HEADS):
        hs = slice(h * HEAD_DIM, (h + 1) * HEAD_DIM)
        qh, kh, vh = q[:, hs], k[:, hs], v[:, hs]
        lgf = lg[:, h:h + 1]
        lgb = lg[:, N_HEADS + h:N_HEADS + h + 1]
        for qt in range(t // Q_TILE):
            rows = slice(qt * Q_TILE, (qt + 1) * Q_TILE)
            diff = (_iota((Q_TILE, t), 0) + qt * Q_TILE - _iota((Q_TILE, t), 1)).astype(F32)
            dmat = (jnp.exp(jnp.where(diff >= 0, diff * lgf, NEG_BIG))
                    + jnp.exp(jnp.where(diff <= 0, -diff * lgb, NEG_BIG)))
            o = _bmm(_mm_nt(qh[rows].astype(BF16), kh.astype(BF16)) * dmat, vh)
            if latent:
                p = pos[rows]
                o = o + jnp.exp((p + 1.0) * lgf) * _bmm(qh[rows], s0_ref[0, h])
                o = o + jnp.exp((t - p) * lgb) * _bmm(qh[rows], s0_ref[0, N_HEADS + h])
            mu = jnp.mean(o, axis=-1, keepdims=True)
            oc = o - mu
            on = oc * lax.rsqrt(jnp.mean(oc * oc, axis=-1, keepdims=True) + EPS)
            o_ref[rows, hs] = on * _silu(r[rows, 3 * W_GROUP + h * HEAD_DIM:3 * W_GROUP + (h + 1) * HEAD_DIM])
        sf = _mm_tn((kh * jnp.exp((t - 1.0 - pos) * lgf)).astype(BF16), vh.astype(BF16))
        sb = _mm_tn((kh * jnp.exp(pos * lgb)).astype(BF16), vh.astype(BF16))
        if latent:
            sf = sf + jnp.exp(t * lgf) * s0_ref[0, h]
            sb = sb + jnp.exp(t * lgb) * s0_ref[0, N_HEADS + h]
        sfin_ref[0, h] = sf
        sfin_ref[0, N_HEADS + h] = sb


def _ret(r_all, logit, s0, cos, sin, t, n_seq, first_block, latent, into=None):
    return _seq_call(
        functools.partial(_ret_kernel, t=t, latent=latent), (r_all, logit, s0, cos, sin),
        in_specs=[pl.BlockSpec((t, 4 * W_GROUP), lambda i: (i + first_block, 0)),
                  pl.BlockSpec((1, LANES), lambda i: (0, 0)),
                  pl.BlockSpec((1, N_GATE, HEAD_DIM, HEAD_DIM), lambda i: (i, 0, 0, 0)),
                  pl.BlockSpec((t, W_GROUP), lambda i: (0, 0)),
                  pl.BlockSpec((t, W_GROUP), lambda i: (0, 0))],
        out_specs=[pl.BlockSpec((t, W_GROUP), lambda i: (i + first_block, 0)),
                   pl.BlockSpec((1, N_GATE, HEAD_DIM, HEAD_DIM), lambda i: (i, 0, 0, 0))],
        out_shape=[jax.ShapeDtypeStruct((N_TOK, W_GROUP), F32),
                   jax.ShapeDtypeStruct((n_seq, N_GATE, HEAD_DIM, HEAD_DIM), F32)],
        into=into, grid=(n_seq,), compiler_params=_cparams("arbitrary"), name="ret")


def _rope_tables(t):
    pos = np.arange(t)
    row = (pos // GRID_W).astype(np.float32)
    col = (pos % GRID_W).astype(np.float32)
    nf = HEAD_DIM // 4
    inv_freq = jnp.power(ROPE_BASE, -jnp.arange(nf, dtype=F32) / nf)
    ang_r = jnp.asarray(row)[:, None] * inv_freq[None, :]
    ang_c = jnp.asarray(col)[:, None] * inv_freq[None, :]
    cos = jnp.concatenate([jnp.cos(ang_r)] * 2 + [jnp.cos(ang_c)] * 2, axis=1)
    sin = jnp.concatenate([-jnp.sin(ang_r), jnp.sin(ang_r), -jnp.sin(ang_c), jnp.sin(ang_c)], axis=1)
    return jnp.tile(cos, (1, N_HEADS)), jnp.tile(sin, (1, N_HEADS))


def _head_rms(x, w):
    return x * lax.rsqrt(jnp.mean(x * x, axis=-1, keepdims=True) + EPS) * w


def _ctx_attn_kernel(n_ref, qw_ref, kw_ref, o_ref, k_out, v_out):
    n = n_ref[...]
    v_out[0] = n[:, 2 * W_GROUP:]
    for h in range(N_HEADS):
        hs = slice(h * HEAD_DIM, (h + 1) * HEAD_DIM)
        qh = _head_rms(n[:, hs], qw_ref[...])
        kh = _head_rms(n[:, W_GROUP + h * HEAD_DIM:W_GROUP + (h + 1) * HEAD_DIM], kw_ref[...])
        vh = n[:, 2 * W_GROUP + h * HEAD_DIM:2 * W_GROUP + (h + 1) * HEAD_DIM]
        k_out[0, :, hs] = kh
        s = _mm_nt(qh.astype(BF16), kh.astype(BF16)) * (HEAD_DIM ** -0.5)
        p = jnp.exp(s - jnp.max(s, axis=-1, keepdims=True))
        o_ref[:, hs] = _bmm(p, vh) / jnp.sum(p, axis=-1, keepdims=True)


def _ctx_attn(n_all, qw, kw, into):
    return _seq_call(
        _ctx_attn_kernel, (n_all, qw, kw),
        in_specs=[pl.BlockSpec((SEQ, 3 * W_GROUP), lambda i: (i, 0)),
                  pl.BlockSpec((1, HEAD_DIM), lambda i: (0, 0)),
                  pl.BlockSpec((1, HEAD_DIM), lambda i: (0, 0))],
        out_specs=[pl.BlockSpec((SEQ, W_GROUP), lambda i: (i, 0)),
                   pl.BlockSpec((1, SEQ, W_GROUP), lambda i: (i, 0, 0)),
                   pl.BlockSpec((1, SEQ, W_GROUP), lambda i: (i, 0, 0))],
        out_shape=[jax.ShapeDtypeStruct((N_TOK, W_GROUP), F32),
                   jax.ShapeDtypeStruct((BATCH, SEQ, W_GROUP), F32),
                   jax.ShapeDtypeStruct((BATCH, SEQ, W_GROUP), F32)],
        into=into, grid=(BATCH,), compiler_params=_cparams("arbitrary"), name="ctx_attn")


def _nat_kernel(n_ref, ck_ref, cv_ref, bias_ref, qw_ref, kw_ref, o_ref):
    h = pl.program_id(1)
    n = n_ref[...]
    sel = (_iota((W_GROUP, HEAD_DIM), 0) == _iota((W_GROUP, HEAD_DIM), 1) + h * HEAD_DIM).astype(F32)
    qh = _head_rms(_mm(n[:, :W_GROUP], sel, HI), qw_ref[...]) * (HEAD_DIM ** -0.5)
    kh = _head_rms(_mm(n[:, W_GROUP:2 * W_GROUP], sel, HI), kw_ref[...])
    vh = _mm(n[:, 2 * W_GROUP:], sel, HI)
    ckh = _mm(ck_ref[0], sel, HI)
    cvh = _mm(cv_ref[0], sel, HI)
    for qt in range(DEC_SEQ // Q_TILE):
        rows = slice(qt * Q_TILE, (qt + 1) * Q_TILE)
        s_loc = _mm_nt(qh[rows].astype(BF16), kh.astype(BF16)) + bias_ref[0, rows, :]
        s_ctx = _mm_nt(qh[rows].astype(BF16), ckh.astype(BF16))
        m = jnp.maximum(jnp.max(s_loc, axis=-1, keepdims=True), jnp.max(s_ctx, axis=-1, keepdims=True))
        p_loc = jnp.exp(s_loc - m)
        p_ctx = jnp.exp(s_ctx - m)
        den = jnp.sum(p_loc, axis=-1, keepdims=True) + jnp.sum(p_ctx, axis=-1, keepdims=True)
        o = (_bmm(p_loc, vh) + _bmm(p_ctx, cvh)) / den
        for hh in range(N_HEADS):
            @pl.when(h == hh)
            def _():
                o_ref[rows, hh * HEAD_DIM:(hh + 1) * HEAD_DIM] = o


def _nat(n_all, ck, cv, bias, qw, kw, into):
    first_block = N_CTX // DEC_SEQ
    return _seq_call(
        _nat_kernel, (n_all, ck, cv, bias, qw, kw),
        in_specs=[pl.BlockSpec((DEC_SEQ, 3 * W_GROUP), lambda b, h: (b + first_block, 0)),
                  pl.BlockSpec((1, PAST_LEN, W_GROUP), lambda b, h: (b, 0, 0)),
                  pl.BlockSpec((1, PAST_LEN, W_GROUP), lambda b, h: (b, 0, 0)),
                  pl.BlockSpec((1, DEC_SEQ, DEC_SEQ), lambda b, h: (h, 0, 0)),
                  pl.BlockSpec((1, HEAD_DIM), lambda b, h: (0, 0)),
                  pl.BlockSpec((1, HEAD_DIM), lambda b, h: (0, 0))],
        out_specs=pl.BlockSpec((DEC_SEQ, W_GROUP), lambda b, h: (b + first_block, 0)),
        out_shape=jax.ShapeDtypeStruct((N_TOK, W_GROUP), F32),
        into=into, grid=(DEC_BATCH, N_HEADS), compiler_params=_cparams("arbitrary", "arbitrary"), name="nat")


def _nat_bias(rpb):
    rows_n = DEC_SEQ // GRID_W
    kh = min(WIN_ROWS, rows_n)
    r = np.arange(rows_n)
    c = np.arange(GRID_W)
    r0 = np.clip(r - kh // 2, 0, rows_n - kh)
    c0 = np.clip(c - WIN_COLS // 2, 0, GRID_W - WIN_COLS)
    row_in = (r[None, :] >= r0[:, None]) & (r[None, :] < r0[:, None] + kh)
    col_in = (c[None, :] >= c0[:, None]) & (c[None, :] < c0[:, None] + WIN_COLS)
    row_idx = np.clip(r[None, :] - r[:, None] + WIN_ROWS - 1, 0, 2 * WIN_ROWS - 2)
    col_idx = np.clip(c[None, :] - c[:, None], -(WIN_COLS - 1), WIN_COLS - 1) + WIN_COLS - 1
    row_hot = (row_idx[..., None] == np.arange(2 * WIN_ROWS - 1)).astype(np.float32)
    col_hot = (col_idx[..., None] == np.arange(2 * WIN_COLS - 1)).astype(np.float32)
    bias = jnp.einsum('hab,rsa,qkb->hrqsk', rpb.astype(F32), row_hot, col_hot, precision=HI)
    inside = row_in[:, None, :, None] & col_in[None, :, None, :]
    return jnp.where(inside[None], bias, NEG_BIG).reshape(N_HEADS, DEC_SEQ, DEC_SEQ)


GSEL_LANE = N_EXPERTS


def _pack_router(we, be, wg, bg):
    pad = LANES - N_EXPERTS - N_GROUPS
    w = jnp.concatenate([we, wg, jnp.zeros((D_MODEL, pad), F32)], axis=1)
    b = jnp.concatenate([be, bg, jnp.zeros((pad,), F32)]).reshape(1, LANES)
    return w, b


def _lane_min_where(mask, lane):
    return jnp.min(jnp.where(mask, lane, LANES), axis=-1, keepdims=True)


def _outproj_kernel(x_ref, m0, m1, m2, m3, mod_ref, nw_ref, w_ref, rw_ref, rb_ref, x_out, hf_out, route_out):
    acc = None
    for i, m_ref in enumerate((m0, m1, m2, m3)):
        part = _mm(m_ref[...].astype(BF16), w_ref[i * W_GROUP:(i + 1) * W_GROUP, :])
        acc = part if acc is None else acc + part
    x = x_ref[...] + mod_ref[0, 2:3, :] * acc
    x_out[...] = x
    y = x * lax.rsqrt(jnp.mean(x * x, axis=-1, keepdims=True) + EPS) * nw_ref[...]
    hf = y * (1.0 + mod_ref[0, 4:5, :]) + mod_ref[0, 3:4, :]
    hf_out[...] = hf.astype(BF16)

    logits = _mm(hf, rw_ref[...], HI) + rb_ref[...]
    lane = _iota(logits.shape, 1)
    is_g = (lane >= N_EXPERTS) & (lane < N_EXPERTS + N_GROUPS)
    gl = jnp.where(is_g, logits, NEG_BIG)
    ge = jnp.exp(gl - jnp.max(gl, axis=-1, keepdims=True))
    gp = jnp.where(is_g, ge / jnp.sum(ge, axis=-1, keepdims=True), -1.0)
    gw = jnp.max(gp, axis=-1, keepdims=True)
    gsel = _lane_min_where(gp == gw, lane) - N_EXPERTS
    in_grp = (lane // EXPERTS_PER_GROUP == gsel) & (lane < N_EXPERTS)
    el = jnp.where(in_grp, logits, NEG_BIG)
    ee = jnp.exp(el - jnp.max(el, axis=-1, keepdims=True))
    ep = jnp.where(in_grp, ee / jnp.sum(ee, axis=-1, keepdims=True), -1.0)
    t1 = jnp.max(ep, axis=-1, keepdims=True)
    i1 = _lane_min_where(ep == t1, lane)
    ep2 = jnp.where(lane == i1, -1.0, ep)
    t2 = jnp.max(ep2, axis=-1, keepdims=True)
    i2 = _lane_min_where(ep2 == t2, lane)
    tsum = t1 + t2
    combine = jnp.where(lane == i1, gw * (t1 / tsum), 0.0) + jnp.where(lane == i2, gw * (t2 / tsum), 0.0)
    route_out[...] = jnp.where(lane == GSEL_LANE, gsel.astype(F32), combine)


def _outproj(x, mixed, mod, norm_w, w_out, rw, rb):
    tile = lambda w: pl.BlockSpec((TOK_TILE, w), lambda i: (i, 0))
    whole = lambda a: pl.BlockSpec(a.shape, lambda i: (0,) * a.ndim)
    return pl.pallas_call(
        _outproj_kernel,
        grid=(N_TOK // TOK_TILE,),
        in_specs=[tile(D_MODEL)] + [tile(W_GROUP)] * 4
                 + [pl.BlockSpec((1, SUBLANES, D_MODEL), lambda i: (_cond_of_tile(i), 0, 0)),
                    pl.BlockSpec((1, D_MODEL), lambda i: (0, 0)), whole(w_out), whole(rw), whole(rb)],
        out_specs=[tile(D_MODEL), tile(D_MODEL), tile(LANES)],
        out_shape=[jax.ShapeDtypeStruct((N_TOK, D_MODEL), F32), jax.ShapeDtypeStruct((N_TOK, D_MODEL), BF16),
                   jax.ShapeDtypeStruct((N_TOK, LANES), F32)],
        compiler_params=_cparams("arbitrary"),
        name="outproj",
    )(x, *mixed, mod, norm_w.reshape(1, D_MODEL), w_out, rw, rb)


SEG_BLK = 32
LOCAL_ROWS = TOK_TILE + N_GROUPS * SEG_BLK
N_TOK_TILES = N_TOK // TOK_TILE
MOE_ROWS = -(-(N_TOK + N_TOK_TILES * N_GROUPS * (SEG_BLK - 1) + N_GROUPS * (MOE_TILE - 1)) // MOE_TILE) * MOE_TILE


def _moe_tables(gsel):
    groups = jnp.arange(N_GROUPS, dtype=jnp.int32)
    onehot = (gsel.reshape(N_TOK_TILES, TOK_TILE, 1) == groups).astype(jnp.int32)
    rank = jnp.cumsum(onehot, axis=1) - onehot
    nblk = (jnp.sum(onehot, axis=1) + SEG_BLK - 1) // SEG_BLK
    loc_blk = jnp.cumsum(nblk, axis=1) - nblk
    blocks_per_tile = MOE_TILE // SEG_BLK
    grp_tiles = (jnp.sum(nblk, axis=0) + blocks_per_tile - 1) // blocks_per_tile
    grp_tile_start = jnp.cumsum(grp_tiles) - grp_tiles
    dst_blk = grp_tile_start[None, :] * blocks_per_tile + jnp.cumsum(nblk, axis=0) - nblk
    local_pos = jnp.sum(onehot * (loc_blk[:, None, :] * SEG_BLK + rank), axis=2)
    tile_idx = jnp.arange(MOE_ROWS // MOE_TILE, dtype=jnp.int32)
    tile_group = jnp.clip(jnp.sum(tile_idx[:, None] >= grp_tile_start[None, :], axis=1) - 1, 0, N_GROUPS - 1)
    tile_valid = tile_idx < jnp.sum(grp_tiles)
    flat = lambda a: a.reshape(-1).astype(jnp.int32)
    return local_pos.astype(jnp.int32), flat(nblk), flat(loc_blk), flat(dst_blk), flat(tile_group), flat(tile_valid)


def _segment_copies(t, nblk, loc_blk, dst_blk, make_copies, action):
    for g in range(N_GROUPS):
        k = t * N_GROUPS + g

        @pl.loop(0, nblk[k])
        def _(b):
            local = pl.multiple_of((loc_blk[k] + b) * SEG_BLK, SEG_BLK)
            sorted_row = pl.multiple_of((dst_blk[k] + b) * SEG_BLK, SEG_BLK)
            for cp in make_copies(local, sorted_row):
                action(cp)


def _dispatch_kernel(nblk, loc_blk, dst_blk, hf_ref, rt_ref, lp_ref, xs_in, rs_in, xs_hbm, rs_hbm, xbuf, rbuf, sem):
    t = pl.program_id(0)
    onehot = _iota((LOCAL_ROWS, TOK_TILE), 0) == lp_ref[0]
    xbuf[...] = _mm(onehot.astype(BF16), hf_ref[...]).astype(BF16)
    rbuf[...] = _mm(onehot.astype(F32), rt_ref[...], HI)

    def copies(local, sorted_row):
        return (pltpu.make_async_copy(xbuf.at[pl.ds(local, SEG_BLK)], xs_hbm.at[pl.ds(sorted_row, SEG_BLK)], sem),
                pltpu.make_async_copy(rbuf.at[pl.ds(local, SEG_BLK)], rs_hbm.at[pl.ds(sorted_row, SEG_BLK)], sem))

    _segment_copies(t, nblk, loc_blk, dst_blk, copies, lambda cp: cp.start())
    _segment_copies(t, nblk, loc_blk, dst_blk, copies, lambda cp: cp.wait())


def _dispatch(hf, route, local_pos, nblk, loc_blk, dst_blk):
    grid_spec = pltpu.PrefetchScalarGridSpec(
        num_scalar_prefetch=3,
        grid=(N_TOK_TILES,),
        in_specs=[pl.BlockSpec((TOK_TILE, D_MODEL), lambda t, *_: (t, 0)),
                  pl.BlockSpec((TOK_TILE, LANES), lambda t, *_: (t, 0)),
                  pl.BlockSpec((1, 1, TOK_TILE), lambda t, *_: (t, 0, 0)),
                  pl.BlockSpec(memory_space=pl.ANY), pl.BlockSpec(memory_space=pl.ANY)],
        out_specs=[pl.BlockSpec(memory_space=pl.ANY), pl.BlockSpec(memory_space=pl.ANY)],
        scratch_shapes=[pltpu.VMEM((LOCAL_ROWS, D_MODEL), BF16), pltpu.VMEM((LOCAL_ROWS, LANES), F32),
                        pltpu.SemaphoreType.DMA],
    )
    return pl.pallas_call(
        _dispatch_kernel,
        grid_spec=grid_spec,
        out_shape=[jax.ShapeDtypeStruct((MOE_ROWS, D_MODEL), BF16), jax.ShapeDtypeStruct((MOE_ROWS, LANES), F32)],
        input_output_aliases={6: 0, 7: 1},
        compiler_params=_cparams("arbitrary"),
        name="dispatch",
    )(nblk, loc_blk, dst_blk, hf, route, local_pos.reshape(N_TOK_TILES, 1, TOK_TILE),
      jnp.zeros((MOE_ROWS, D_MODEL), BF16), jnp.zeros((MOE_ROWS, LANES), F32))


def _moe_kernel(tile_group, tile_valid, x_ref, r_ref, wg_ref, wu_ref, wd_ref, y_ref, wg_b, wu_b, wd_b):
    i = pl.program_id(0)
    g = tile_group[i]

    @pl.when((i == 0) | (g != tile_group[jnp.maximum(i - 1, 0)]))
    def _():
        for e in range(EXPERTS_PER_GROUP):
            wg_b[e] = wg_ref[0, e].astype(BF16)
            wu_b[e] = wu_ref[0, e].astype(BF16)
            wd_b[e] = wd_ref[0, e].astype(BF16)

    @pl.when(tile_valid[i] > 0)
    def _():
        x = x_ref[...]
        route = r_ref[...]
        lane = _iota(route.shape, 1)
        acc = jnp.zeros((MOE_TILE, D_MODEL), F32)
        for e in range(EXPERTS_PER_GROUP):
            cw = jnp.sum(jnp.where(lane == g * EXPERTS_PER_GROUP + e, route, 0.0), axis=-1, keepdims=True)
            act = _silu(_mm(x, wg_b[e])) * _mm(x, wu_b[e]) * cw
            acc = acc + _mm(act.astype(BF16), wd_b[e])
        y_ref[...] = acc

    @pl.when(tile_valid[i] == 0)
    def _():
        y_ref[...] = jnp.zeros_like(y_ref)


def _moe(xs, rs, tile_group, tile_valid, wg, wu, wd):
    w_spec = lambda shape: pl.BlockSpec((1, EXPERTS_PER_GROUP) + shape, lambda i, tg, tv: (tg[i], 0, 0, 0),
                                        pipeline_mode=pl.Buffered(1))
    grid_spec = pltpu.PrefetchScalarGridSpec(
        num_scalar_prefetch=2,
        grid=(MOE_ROWS // MOE_TILE,),
        in_specs=[pl.BlockSpec((MOE_TILE, D_MODEL), lambda i, tg, tv: (i, 0)),
                  pl.BlockSpec((MOE_TILE, LANES), lambda i, tg, tv: (i, 0)),
                  w_spec((D_MODEL, EXPERT_FF)), w_spec((D_MODEL, EXPERT_FF)), w_spec((EXPERT_FF, D_MODEL))],
        out_specs=pl.BlockSpec((MOE_TILE, D_MODEL), lambda i, tg, tv: (i, 0)),
        scratch_shapes=[pltpu.VMEM((EXPERTS_PER_GROUP, D_MODEL, EXPERT_FF), BF16),
                        pltpu.VMEM((EXPERTS_PER_GROUP, D_MODEL, EXPERT_FF), BF16),
                        pltpu.VMEM((EXPERTS_PER_GROUP, EXPERT_FF, D_MODEL), BF16)],
    )
    return pl.pallas_call(
        _moe_kernel,
        grid_spec=grid_spec,
        out_shape=jax.ShapeDtypeStruct((MOE_ROWS, D_MODEL), F32),
        compiler_params=_cparams("arbitrary"),
        name="moe",
    )(tile_group, tile_valid, xs, rs, wg, wu, wd)


def _combine_kernel(nblk, loc_blk, dst_blk, x_ref, lp_ref, mod_ref, ys_hbm, o_ref, ybuf, sem):
    t = pl.program_id(0)
    ybuf[...] = jnp.zeros_like(ybuf)

    def copies(local, sorted_row):
        return (pltpu.make_async_copy(ys_hbm.at[pl.ds(sorted_row, SEG_BLK)], ybuf.at[pl.ds(local, SEG_BLK)], sem),)

    _segment_copies(t, nblk, loc_blk, dst_blk, copies, lambda cp: cp.start())
    _segment_copies(t, nblk, loc_blk, dst_blk, copies, lambda cp: cp.wait())

    onehot = (_iota((TOK_TILE, LOCAL_ROWS), 1) == lp_ref[...]).astype(BF16)
    ys = ybuf[...]
    hi = ys.astype(BF16)
    lo = (ys - hi.astype(F32)).astype(BF16)
    y = _mm(onehot, hi) + _mm(onehot, lo)
    o_ref[...] = x_ref[...] + mod_ref[0, 5:6, :] * y


def _combine(x, ys, mod, local_pos, nblk, loc_blk, dst_blk):
    tile = pl.BlockSpec((TOK_TILE, D_MODEL), lambda t, *_: (t, 0))
    grid_spec = pltpu.PrefetchScalarGridSpec(
        num_scalar_prefetch=3,
        grid=(N_TOK_TILES,),
        in_specs=[tile, pl.BlockSpec((TOK_TILE, 1), lambda t, *_: (t, 0)),
                  pl.BlockSpec((1, SUBLANES, D_MODEL), lambda t, *_: (_cond_of_tile(t), 0, 0)),
                  pl.BlockSpec(memory_space=pl.ANY)],
        out_specs=tile,
        scratch_shapes=[pltpu.VMEM((LOCAL_ROWS, D_MODEL), F32), pltpu.SemaphoreType.DMA],
    )
    return pl.pallas_call(
        _combine_kernel,
        grid_spec=grid_spec,
        out_shape=jax.ShapeDtypeStruct((N_TOK, D_MODEL), F32),
        compiler_params=_cparams("arbitrary"),
        name="combine",
    )(nblk, loc_blk, dst_blk, x, local_pos.reshape(N_TOK, 1), mod, ys)


def _lane_row(v):
    v = v.reshape(-1).astype(F32)
    return jnp.concatenate([v, jnp.zeros((LANES - v.shape[0],), F32)]).reshape(1, LANES)


def _pad_rows(w):
    return jnp.concatenate([w, jnp.zeros((SUBLANES - w.shape[0], w.shape[1]), w.dtype)], axis=0)


def kernel(x_prompt, x_sample, state_gdn, state_ret, cache_nat_k, cache_nat_v, c, c_ctx, ada_w, ada_b, norm_mix_w, norm_ffn_w, w_in, gdn_conv_w, gdn_a_log, gdn_dt_bias, gdn_norm_w, ret_gamma_logit, nat_q_norm_w, nat_k_norm_w, nat_rpb, sc_conv_w, w_out, router_group_w, router_group_b, router_expert_w, router_expert_b, moe_w_gate, moe_w_up, moe_w_down):
    x = jnp.concatenate([x_prompt.reshape(N_CTX, D_MODEL), x_sample.reshape(N_LAT, D_MODEL)], axis=0)
    cond = jnp.concatenate([c_ctx[None, :], c, jnp.zeros((SUBLANES - 1 - DEC_BATCH, D_MODEL), F32)], axis=0)
    ada = _ada(cond, ada_w, ada_b).reshape(DEPTH, SUBLANES, 6, D_MODEL)
    cos, sin = _rope_tables(DEC_SEQ)
    zero_state = jnp.zeros((BATCH, N_GATE, HEAD_DIM, HEAD_DIM), F32)
    lat_block = N_CTX // DEC_SEQ
    gdn_list, ret_list, k_list, v_list = [], [], [], []
    mixed = [jnp.zeros((N_TOK, W_GROUP), F32) for _ in range(4)]
    for l in range(DEPTH):
        mod = jnp.concatenate([ada[l, :1 + DEC_BATCH], jnp.zeros((1 + DEC_BATCH, SUBLANES - 6, D_MODEL), F32)], axis=1)
        a_gdn, a_ret, a_nat, a_sc, a_gate = _inproj(x, mod, norm_mix_w[l], _pack_w_in(w_in[l]))

        conv_w = _pad_rows(gdn_conv_w[l])
        a_log, dt_b = _lane_row(gdn_a_log[l]), _lane_row(gdn_dt_bias[l])
        gnw = jnp.tile(gdn_norm_w[l], N_HEADS).reshape(1, W_GROUP)
        o_gdn, s_gdn = _gdn(a_gdn, a_gate, conv_w, a_log, dt_b, gnw, zero_state, SEQ, BATCH, 0, into=mixed[0])
        s0 = state_gdn[:, l].reshape(DEC_BATCH, N_GATE, HEAD_DIM, HEAD_DIM)
        o_gdn, _ = _gdn(a_gdn, a_gate, conv_w, a_log, dt_b, gnw, s0, DEC_SEQ, DEC_BATCH, lat_block, into=o_gdn)

        logit = _lane_row(ret_gamma_logit[l])
        o_ret, s_ret = _ret(a_ret, logit, zero_state, cos[:SEQ], sin[:SEQ], SEQ, BATCH, 0, False, into=mixed[1])
        s0 = state_ret[:, l].reshape(DEC_BATCH, N_GATE, HEAD_DIM, HEAD_DIM)
        o_ret, _ = _ret(a_ret, logit, s0, cos, sin, DEC_SEQ, DEC_BATCH, lat_block, True, into=o_ret)

        qw, kw = nat_q_norm_w[l].reshape(1, HEAD_DIM), nat_k_norm_w[l].reshape(1, HEAD_DIM)
        o_nat, n_k, n_v = _ctx_attn(a_nat, qw, kw, mixed[2])
        o_nat = _nat(a_nat, cache_nat_k[:, l].reshape(DEC_BATCH, PAST_LEN, W_GROUP),
                     cache_nat_v[:, l].reshape(DEC_BATCH, PAST_LEN, W_GROUP), _nat_bias(nat_rpb[l]), qw, kw, o_nat)

        sc_w = _pad_rows(sc_conv_w[l])
        o_sc = _sconv(a_sc, sc_w, SEQ, BATCH, 0, into=mixed[3])
        o_sc = _sconv(a_sc, sc_w, DEC_SEQ, DEC_BATCH, lat_block, into=o_sc)

        mixed = [o_gdn, o_ret, o_nat, o_sc]
        rw, rb = _pack_router(router_expert_w[l], router_expert_b[l], router_group_w[l], router_group_b[l])
        x_mid, hf, route = _outproj(x, mixed, mod, norm_ffn_w[l], w_out[l].astype(BF16), rw, rb)

        local_pos, nblk, loc_blk, dst_blk, tile_group, tile_valid = _moe_tables(route[:, GSEL_LANE].astype(jnp.int32))
        xs, rs = _dispatch(hf, route, local_pos, nblk, loc_blk, dst_blk)
        to_group = lambda w: w.reshape((N_GROUPS, EXPERTS_PER_GROUP) + w.shape[1:])
        ys = _moe(xs, rs, tile_group, tile_valid, to_group(moe_w_gate[l]), to_group(moe_w_up[l]),
                  to_group(moe_w_down[l]))
        x = _combine(x_mid, ys, mod, local_pos, nblk, loc_blk, dst_blk)

        gdn_list.append(s_gdn.reshape(BATCH, 2, N_HEADS, HEAD_DIM, HEAD_DIM))
        ret_list.append(s_ret.reshape(BATCH, 2, N_HEADS, HEAD_DIM, HEAD_DIM))
        k_list.append(n_k.reshape(BATCH, SEQ, N_HEADS, HEAD_DIM))
        v_list.append(n_v.reshape(BATCH, SEQ, N_HEADS, HEAD_DIM))
    return (x[:N_CTX].reshape(BATCH, SEQ, D_MODEL), x[N_CTX:].reshape(DEC_BATCH, DEC_SEQ, D_MODEL),
            jnp.stack(gdn_list, axis=1), jnp.stack(ret_list, axis=1),
            jnp.stack(k_list, axis=1), jnp.stack(v_list, axis=1))
```

```python
import functools

import numpy as np
import jax
import jax.numpy as jnp
from jax import lax
from jax.experimental import pallas as pl
from jax.experimental.pallas import tpu as pltpu

D_MODEL = 1024
BATCH = 16
SEQ = 256
DEPTH = 2
DEC_BATCH = 2
DEC_SEQ = 1024
PAST_LEN = 256
GRID_W = 64
HEAD_DIM = 64
W_GROUP = D_MODEL // 4
N_HEADS = W_GROUP // HEAD_DIM
CHUNK = 64
WIN_ROWS = 8
WIN_COLS = 16
ROPE_BASE = 10000.0
N_GROUPS = 4
EXPERTS_PER_GROUP = 8
N_EXPERTS = N_GROUPS * EXPERTS_PER_GROUP
EXPERT_FF = 256
GROUP_FF = EXPERTS_PER_GROUP * EXPERT_FF
EPS = 1e-6

N_CTX = BATCH * SEQ
N_LAT = DEC_BATCH * DEC_SEQ
N_TOK = N_CTX + N_LAT
LANES = 128
SUBLANES = 8
TOK_TILE = 512
MOE_TILE = 256
Q_TILE = 256
VMEM_LIMIT = 48 * 1024 * 1024
NEG_BIG = -1e30
N_GATE = 2 * N_HEADS

F32 = jnp.float32
BF16 = jnp.bfloat16
HI = lax.Precision.HIGHEST


def _mm(a, b, prec=None):
    return lax.dot_general(a, b, (((1,), (0,)), ((), ())), precision=prec, preferred_element_type=F32)


def _mm_nt(a, b, prec=None):
    return lax.dot_general(a, b, (((1,), (1,)), ((), ())), precision=prec, preferred_element_type=F32)


def _mm_tn(a, b, prec=None):
    return lax.dot_general(a, b, (((0,), (0,)), ((), ())), precision=prec, preferred_element_type=F32)


def _bmm(a, b):
    return _mm(a.astype(BF16), b.astype(BF16))


def _sigmoid(x):
    return 1.0 / (1.0 + jnp.exp(-x))


def _silu(x):
    return x * _sigmoid(x)


def _softplus(x):
    return jnp.maximum(x, 0.0) + jnp.log(1.0 + jnp.exp(-jnp.abs(x)))


def _iota(shape, dim):
    return lax.broadcasted_iota(jnp.int32, shape, dim)


def _cparams(*sem):
    return pltpu.CompilerParams(dimension_semantics=sem, vmem_limit_bytes=VMEM_LIMIT)


def _cond_of_tile(i):
    n_ctx_tiles = N_CTX // TOK_TILE
    return jnp.where(i < n_ctx_tiles, 0, 1 + (i - n_ctx_tiles) // (DEC_SEQ // TOK_TILE))


def _head_sum_matrix():
    return (_iota((W_GROUP, W_GROUP), 0) // HEAD_DIM == _iota((W_GROUP, W_GROUP), 1) // HEAD_DIM).astype(F32)


def _ada_kernel(c_ref, w_ref, b_ref, o_ref):
    o_ref[0] = _mm(_silu(c_ref[...]), w_ref[0], HI) + b_ref[0]


def _ada(cond, ada_w, ada_b):
    tn = 1536
    n_out = 6 * D_MODEL
    return pl.pallas_call(
        _ada_kernel,
        grid=(DEPTH, n_out // tn),
        in_specs=[pl.BlockSpec((SUBLANES, D_MODEL), lambda l, j: (0, 0)),
                  pl.BlockSpec((1, D_MODEL, tn), lambda l, j: (l, 0, j)),
                  pl.BlockSpec((1, 1, tn), lambda l, j: (l, 0, j))],
        out_specs=pl.BlockSpec((1, SUBLANES, tn), lambda l, j: (l, 0, j)),
        out_shape=jax.ShapeDtypeStruct((DEPTH, SUBLANES, n_out), F32),
        compiler_params=_cparams("arbitrary", "arbitrary"),
        name="ada",
    )(cond, ada_w, ada_b.reshape(DEPTH, 1, n_out))


IN_WIDTHS = (4 * W_GROUP, 4 * W_GROUP, 3 * W_GROUP, 3 * W_GROUP, LANES)
IN_PACKED = sum(IN_WIDTHS)


def _pack_w_in(w):
    sizes = (3 * W_GROUP, W_GROUP, N_GATE, N_GATE, 3 * W_GROUP, W_GROUP, 3 * W_GROUP, 3 * W_GROUP)
    offs = np.concatenate([[0], np.cumsum(sizes)])
    seg = [w[:, offs[i]:offs[i + 1]] for i in range(len(sizes))]
    gate = jnp.concatenate([seg[2], seg[3], jnp.zeros((D_MODEL, LANES - 2 * N_GATE), w.dtype)], axis=1)
    return jnp.concatenate([seg[0], seg[1], seg[4], seg[5], seg[6], seg[7], gate], axis=1).astype(BF16)


def _inproj_kernel(x_ref, mod_ref, nw_ref, w_ref, *o_refs):
    x = x_ref[...]
    y = x * lax.rsqrt(jnp.mean(x * x, axis=-1, keepdims=True) + EPS) * nw_ref[...]
    h = (y * (1.0 + mod_ref[0, 1:2, :]) + mod_ref[0, 0:1, :]).astype(BF16)
    off = 0
    for o_ref, width in zip(o_refs, IN_WIDTHS):
        o_ref[...] = _mm(h, w_ref[:, off:off + width])
        off += width


def _inproj(x, mod, norm_w, w_packed):
    return pl.pallas_call(
        _inproj_kernel,
        grid=(N_TOK // TOK_TILE,),
        in_specs=[pl.BlockSpec((TOK_TILE, D_MODEL), lambda i: (i, 0)),
                  pl.BlockSpec((1, SUBLANES, D_MODEL), lambda i: (_cond_of_tile(i), 0, 0)),
                  pl.BlockSpec((1, D_MODEL), lambda i: (0, 0)),
                  pl.BlockSpec((D_MODEL, IN_PACKED), lambda i: (0, 0))],
        out_specs=[pl.BlockSpec((TOK_TILE, w), lambda i: (i, 0)) for w in IN_WIDTHS],
        out_shape=[jax.ShapeDtypeStruct((N_TOK, w), F32) for w in IN_WIDTHS],
        compiler_params=_cparams("arbitrary"),
        name="inproj",
    )(x, mod, norm_w.reshape(1, D_MODEL), w_packed)


def _seq_call(kernel_fn, args, in_specs, out_specs, out_shape, into, **kwargs):
    aliases = {}
    if into is not None:
        n_in = len(args)
        inner = kernel_fn
        kernel_fn = lambda *refs: inner(*refs[:n_in], *refs[n_in + 1:])
        aliases = {n_in: 0}
        args = list(args) + [into]
        in_specs = list(in_specs) + [pl.BlockSpec(memory_space=pl.ANY)]
    return pl.pallas_call(kernel_fn, in_specs=in_specs, out_specs=out_specs, out_shape=out_shape,
                          input_output_aliases=aliases, **kwargs)(*args)


def _shift_rows(p, t):
    row = _iota(p.shape, 0)
    prev = jnp.where(row == 0, 0.0, pltpu.roll(p, 1, 0))
    nxt = jnp.where(row == t - 1, 0.0, pltpu.roll(p, t - 1, 0))
    return prev, nxt


def _conv3(x, w_ref, t):
    prev, nxt = _shift_rows(x, t)
    return w_ref[0:1, :] * prev + w_ref[1:2, :] * x + w_ref[2:3, :] * nxt


def _sconv_kernel(s_ref, w_ref, o_ref, *, t):
    s = s_ref[...]
    p = s[:, W_GROUP:2 * W_GROUP] * s[:, 2 * W_GROUP:]
    o_ref[...] = s[:, :W_GROUP] * _conv3(p, w_ref, t)


def _sconv(s_all, w, t, n_seq, first_block, into=None):
    return _seq_call(
        functools.partial(_sconv_kernel, t=t), (s_all, w),
        in_specs=[pl.BlockSpec((t, 3 * W_GROUP), lambda i: (i + first_block, 0)),
                  pl.BlockSpec((SUBLANES, W_GROUP), lambda i: (0, 0))],
        out_specs=pl.BlockSpec((t, W_GROUP), lambda i: (i + first_block, 0)),
        out_shape=jax.ShapeDtypeStruct((N_TOK, W_GROUP), F32),
        into=into, grid=(n_seq,), compiler_params=_cparams("arbitrary"), name="sconv")


GDN_GROUP_CHUNKS = 4
GDN_CHAINS = GDN_GROUP_CHUNKS * N_GATE
GDN_PAIRS = GDN_GROUP_CHUNKS * N_HEADS


def _gdn_kernel(a_ref, gate_ref, convw_ref, alog_ref, dtb_ref, nw_ref, s0_ref, o_ref, sfin_ref,
                q_s, kv_s, kt_s, gc_s, eg_s, beta_s, gcrow_s, ekdrow_s, cdec_s, uo_s, wq_s, attn_s, kdt_s,
                st_s, wsqs_s, kk_s, d_s, off_s, m1_s, m2_s, p_s, m3_s, rhs_s, px_s, *, t):
    n_chunks = t // CHUNK
    a = a_ref[...]
    qkv = _silu(_conv3(a[:, :3 * W_GROUP], convw_ref, t))
    q = qkv[:, :W_GROUP]
    k = qkv[:, W_GROUP:2 * W_GROUP]
    v = qkv[:, 2 * W_GROUP:]
    head_sum = _head_sum_matrix()
    q = q * lax.rsqrt(_mm(q * q, head_sum, HI) + EPS) * (HEAD_DIM ** -0.5)
    k = k * lax.rsqrt(_mm(k * k, head_sum, HI) + EPS)
    for h in range(N_HEADS):
        hs = slice(h * HEAD_DIM, (h + 1) * HEAD_DIM)
        q_s[h] = q[:, hs]
        kv_s[h] = jnp.concatenate([k[:, hs], v[:, hs]], axis=1)
    k_t = k.T
    for c in range(n_chunks):
        kt_s[c] = k_t[:, c * CHUNK:(c + 1) * CHUNK]

    gates = gate_ref[...]
    log_a = -jnp.exp(alog_ref[...]) * _softplus(gates + dtb_ref[...])
    beta_s[...] = _sigmoid(gates)

    ci = _iota((CHUNK, CHUNK), 0)
    cj = _iota((CHUNK, CHUNK), 1)
    eye = (ci == cj).astype(F32)
    blk_mask = (ci // 16) == (cj // 16)
    low_half = _iota((CHUNK, 2 * HEAD_DIM), 1) < HEAD_DIM

    tri_f = (cj <= ci).astype(F32)
    tri_b = (cj >= ci).astype(F32)
    ones = jnp.ones((CHUNK, CHUNK), F32)
    fwd_lane = _iota((CHUNK, LANES), 1) < N_HEADS
    gate_rows = (_iota((N_GATE, LANES), 0) == _iota((N_GATE, LANES), 1)).astype(F32)
    for c in range(n_chunks):
        rows = slice(c * CHUNK, (c + 1) * CHUNK)
        g = log_a[rows, :]
        gc = jnp.where(fwd_lane, _mm(tri_f, g, HI), _mm(tri_b, g, HI))
        gt = _mm(ones, g, HI)
        gc_s[rows, :] = gc
        eg_s[rows, :] = jnp.exp(gc)
        gcrow_s[c] = _mm_nt(gate_rows, gc, HI)
        ekdrow_s[c] = jnp.exp(_mm_nt(gate_rows, gt - gc, HI))
        cdec = jnp.exp(_mm_nt(gate_rows, gt, HI))
        cdec_s[c] = jnp.concatenate([cdec, cdec], axis=1)

    def solve_group(grp, carry):
        row0 = grp * (GDN_GROUP_CHUNKS * CHUNK)
        chains = [(cl, a_idx) for cl in range(GDN_GROUP_CHUNKS) for a_idx in range(N_GATE)]

        def rows_of(cl):
            return pl.ds(pl.multiple_of(row0 + cl * CHUNK, CHUNK), CHUNK)

        for cl in range(GDN_GROUP_CHUNKS):
            for h in range(N_HEADS):
                rows = rows_of(cl)
                kq = jnp.concatenate([kv_s[h, rows, :HEAD_DIM], q_s[h, rows, :]], axis=0)
                k_t_h = kt_s[grp * GDN_GROUP_CHUNKS + cl, h * HEAD_DIM:(h + 1) * HEAD_DIM, :]
                kk_s[cl * N_HEADS + h] = _bmm(kq, k_t_h)
        for b, (cl, a_idx) in enumerate(chains):
            backward = a_idx >= N_HEADS
            h = a_idx % N_HEADS
            rows, c = rows_of(cl), grp * GDN_GROUP_CHUNKS + cl
            incl = (cj >= ci) if backward else (cj <= ci)
            strict = (cj > ci) if backward else (cj < ci)
            bt = beta_s[rows, N_GATE + a_idx:N_GATE + a_idx + 1]
            decay = jnp.exp(jnp.where(incl, gc_s[rows, a_idx:a_idx + 1] - gcrow_s[c, a_idx:a_idx + 1, :], NEG_BIG))
            low = jnp.where(strict, kk_s[cl * N_HEADS + h, :CHUNK, :] * bt * decay, 0.0)
            attn_s[a_idx, rows, :] = (kk_s[cl * N_HEADS + h, CHUNK:, :] * decay).astype(BF16)
            d = jnp.where(blk_mask, low, 0.0)
            d_s[b] = d
            off_s[b] = low - d
            rhs_s[b] = kv_s[h, rows, :] * bt * jnp.where(low_half, eg_s[rows, a_idx:a_idx + 1], 1.0)
        for b in range(GDN_CHAINS):
            m1_s[b] = _bmm(d_s[b], d_s[b])
        for b in range(GDN_CHAINS):
            d, d2 = d_s[b], m1_s[b]
            m2_s[b] = _bmm(d2, d2)
            p_s[b] = eye - d + d2 - _bmm(d, d2)
        for b in range(GDN_CHAINS):
            d4, p = m2_s[b], p_s[b]
            m3_s[b] = _bmm(d4, d4)
            p_s[b] = p + _bmm(p, d4)
        for b in range(GDN_CHAINS):
            p = p_s[b]
            p_s[b] = p + _bmm(p, m3_s[b])
        for b in range(GDN_CHAINS):
            p = p_s[b]
            m1_s[b] = _bmm(p, off_s[b])
            px_s[b] = _bmm(p, rhs_s[b])
        for b in range(GDN_CHAINS):
            m2_s[b] = _bmm(m1_s[b], m1_s[b])
        for b in range(GDN_CHAINS):
            n, n2 = m1_s[b], m2_s[b]
            m3_s[b] = eye - n + n2 - _bmm(n, n2)
        for b, (cl, a_idx) in enumerate(chains):
            h = a_idx % N_HEADS
            rows, c = rows_of(cl), grp * GDN_GROUP_CHUNKS + cl
            wu = _bmm(m3_s[b], px_s[b])
            uo_s[a_idx, rows, :] = wu
            wq_s[a_idx, c, :CHUNK, :] = wu[:, :HEAD_DIM].astype(BF16)
            wq_s[a_idx, c, CHUNK:, :] = (q_s[h, rows, :] * eg_s[rows, a_idx:a_idx + 1]).astype(BF16)
            k_t_h = kt_s[c, h * HEAD_DIM:(h + 1) * HEAD_DIM, :]
            kdt_s[a_idx, c] = (k_t_h * ekdrow_s[c, a_idx:a_idx + 1, :]).astype(BF16)
        return carry

    lax.fori_loop(0, n_chunks // GDN_GROUP_CHUNKS, solve_group, 0)

    for i in range(N_GATE):
        st_s[i] = jnp.concatenate([jnp.zeros((HEAD_DIM, HEAD_DIM), F32), s0_ref[0, i]], axis=1)

    def scan_chunk(c, carry):
        def chunk_of(a_idx):
            return (n_chunks - 1 - c) if a_idx >= N_HEADS else c

        for a_idx in range(N_GATE):
            wsqs_s[a_idx] = _mm(wq_s[a_idx, chunk_of(a_idx)], st_s[a_idx].astype(BF16))
        for a_idx in range(N_GATE):
            cc = chunk_of(a_idx)
            rows = pl.ds(pl.multiple_of(cc * CHUNK, CHUNK), CHUNK)
            v_new = (uo_s[a_idx, rows, :] - wsqs_s[a_idx, :CHUNK, :]).astype(BF16)
            uo_s[a_idx, rows, :] = wsqs_s[a_idx, CHUNK:, :] + _mm(attn_s[a_idx, rows, :], v_new)
            st_s[a_idx] = st_s[a_idx] * cdec_s[cc, a_idx:a_idx + 1, :] + _mm(kdt_s[a_idx, cc], v_new)
        return carry

    lax.fori_loop(0, n_chunks, scan_chunk, 0)
    for i in range(N_GATE):
        sfin_ref[0, i] = st_s[i, :, HEAD_DIM:]

    o = jnp.concatenate([(uo_s[h] + uo_s[N_HEADS + h])[:, HEAD_DIM:] for h in range(N_HEADS)], axis=1)
    ms = _mm(o * o, head_sum, HI) * (1.0 / HEAD_DIM)
    o_ref[...] = o * lax.rsqrt(ms + EPS) * nw_ref[...] * _silu(a[:, 3 * W_GROUP:])


def _gdn(a_all, gate_all, conv_w, a_log, dt_bias, norm_w, s0, t, n_seq, first_block, into=None):
    small = lambda: pl.BlockSpec((1, LANES), lambda i: (0, 0))
    n_chunks = t // CHUNK
    wide = 2 * HEAD_DIM
    scratch = [pltpu.VMEM((N_HEADS, t, HEAD_DIM), F32),
               pltpu.VMEM((N_HEADS, t, wide), F32),
               pltpu.VMEM((n_chunks, W_GROUP, CHUNK), F32),
               pltpu.VMEM((t, LANES), F32), pltpu.VMEM((t, LANES), F32), pltpu.VMEM((t, LANES), F32),
               pltpu.VMEM((n_chunks, N_GATE, CHUNK), F32), pltpu.VMEM((n_chunks, N_GATE, CHUNK), F32),
               pltpu.VMEM((n_chunks, N_GATE, wide), F32),
               pltpu.VMEM((N_GATE, t, wide), F32),
               pltpu.VMEM((N_GATE, n_chunks, 2 * CHUNK, HEAD_DIM), BF16),
               pltpu.VMEM((N_GATE, t, CHUNK), BF16),
               pltpu.VMEM((N_GATE, n_chunks, HEAD_DIM, CHUNK), BF16),
               pltpu.VMEM((N_GATE, HEAD_DIM, wide), F32),
               pltpu.VMEM((N_GATE, 2 * CHUNK, wide), F32),
               pltpu.VMEM((GDN_PAIRS, 2 * CHUNK, CHUNK), F32)]
    scratch += [pltpu.VMEM((GDN_CHAINS, CHUNK, CHUNK), F32)] * 6
    scratch += [pltpu.VMEM((GDN_CHAINS, CHUNK, wide), F32)] * 2
    return _seq_call(
        functools.partial(_gdn_kernel, t=t), (a_all, gate_all, conv_w, a_log, dt_bias, norm_w, s0),
        in_specs=[pl.BlockSpec((t, 4 * W_GROUP), lambda i: (i + first_block, 0)),
                  pl.BlockSpec((t, LANES), lambda i: (i + first_block, 0)),
                  pl.BlockSpec((SUBLANES, 3 * W_GROUP), lambda i: (0, 0)),
                  small(), small(),
                  pl.BlockSpec((1, W_GROUP), lambda i: (0, 0)),
                  pl.BlockSpec((1, N_GATE, HEAD_DIM, HEAD_DIM), lambda i: (i, 0, 0, 0))],
        out_specs=[pl.BlockSpec((t, W_GROUP), lambda i: (i + first_block, 0)),
                   pl.BlockSpec((1, N_GATE, HEAD_DIM, HEAD_DIM), lambda i: (i, 0, 0, 0))],
        out_shape=[jax.ShapeDtypeStruct((N_TOK, W_GROUP), F32),
                   jax.ShapeDtypeStruct((n_seq, N_GATE, HEAD_DIM, HEAD_DIM), F32)],
        scratch_shapes=scratch,
        into=into, grid=(n_seq,), compiler_params=_cparams("arbitrary"), name="gdn")


def _swap16(x):
    width = x.shape[-1]
    first = (_iota(x.shape, 1) // 16) % 2 == 0
    return jnp.where(first, pltpu.roll(x, width - 16, 1), pltpu.roll(x, 16, 1))


def _block_diag_heads(s0_ref, first):
    zero = jnp.zeros((HEAD_DIM, HEAD_DIM), F32)
    return jnp.concatenate(
        [jnp.concatenate([s0_ref[0, first + h] if j == h else zero for j in range(N_HEADS)], axis=1)
         for h in range(N_HEADS)], axis=0)


def _ret_kernel(r_ref, lg_ref, s0_ref, cos_ref, sin_ref, o_ref, sfin_ref, *, t, latent):
    r = r_ref[...]
    q = r[:, :W_GROUP]
    k = r[:, W_GROUP:2 * W_GROUP]
    v = r[:, 2 * W_GROUP:3 * W_GROUP]
    if latent:
        q = q * cos_ref[...] + _swap16(q) * sin_ref[...]
        k = k * cos_ref[...] + _swap16(k) * sin_ref[...]
    k = k * (HEAD_DIM ** -0.5)
    lg = -_softplus(-lg_ref[...])
    lgf, lgb = lg[0:1, :], lg[1:2, :]
    head = _iota((1, W_GROUP), 1) // HEAD_DIM
    head_sum = _head_sum_matrix()
    pos = _iota((t, 1), 0).astype(F32)
    q_b = q.astype(BF16)
    kt_b = k.T.astype(BF16)
    v_heads = [jnp.where(head == h, v, 0.0).astype(BF16) for h in range(N_HEADS)]
    if latent:
        s0f = _block_diag_heads(s0_ref, 0)
        s0b = _block_diag_heads(s0_ref, N_HEADS)
    for qt in range(t // Q_TILE):
        rows = slice(qt * Q_TILE, (qt + 1) * Q_TILE)
        diff = (_iota((Q_TILE, t), 0) + qt * Q_TILE - _iota((Q_TILE, t), 1)).astype(F32)
        o = jnp.zeros((Q_TILE, W_GROUP), F32)
        for h in range(N_HEADS):
            lgf_h = lgf[:, h * HEAD_DIM:h * HEAD_DIM + 1]
            lgb_h = lgb[:, h * HEAD_DIM:h * HEAD_DIM + 1]
            dmat = (jnp.exp(jnp.where(diff >= 0, diff * lgf_h, NEG_BIG))
                    + jnp.exp(jnp.where(diff <= 0, -diff * lgb_h, NEG_BIG)))
            s = _mm(jnp.where(head == h, q_b[rows], 0.0), kt_b) * dmat
            o = o + _mm(s.astype(BF16), v_heads[h])
        if latent:
            p = pos[rows]
            o = o + jnp.exp((p + 1.0) * lgf) * _bmm(q_b[rows], s0f) + jnp.exp((t - p) * lgb) * _bmm(q_b[rows], s0b)
        oc = o - _mm(o, head_sum, HI) * (1.0 / HEAD_DIM)
        on = oc * lax.rsqrt(_mm(oc * oc, head_sum, HI) * (1.0 / HEAD_DIM) + EPS)
        o_ref[rows, :] = on * _silu(r[rows, 3 * W_GROUP:])
    v_b = v.astype(BF16)
    sf = _mm_tn((k * jnp.exp((t - 1.0 - pos) * lgf)).astype(BF16), v_b)
    sb = _mm_tn((k * jnp.exp(pos * lgb)).astype(BF16), v_b)
    for h in range(N_HEADS):
        hs = slice(h * HEAD_DIM, (h + 1) * HEAD_DIM)
        sf_h, sb_h = sf[hs, hs], sb[hs, hs]
        if latent:
            sf_h = sf_h + jnp.exp(t * lgf[:, h * HEAD_DIM:h * HEAD_DIM + 1]) * s0_ref[0, h]
            sb_h = sb_h + jnp.exp(t * lgb[:, h * HEAD_DIM:h * HEAD_DIM + 1]) * s0_ref[0, N_HEADS + h]
        sfin_ref[0, h] = sf_h
        sfin_ref[0, N_HEADS + h] = sb_h


def _ret(r_all, logit, s0, cos, sin, t, n_seq, first_block, latent, into=None):
    return _seq_call(
        functools.partial(_ret_kernel, t=t, latent=latent), (r_all, logit, s0, cos, sin),
        in_specs=[pl.BlockSpec((t, 4 * W_GROUP), lambda i: (i + first_block, 0)),
                  pl.BlockSpec((2, W_GROUP), lambda i: (0, 0)),
                  pl.BlockSpec((1, N_GATE, HEAD_DIM, HEAD_DIM), lambda i: (i, 0, 0, 0)),
                  pl.BlockSpec((t, W_GROUP), lambda i: (0, 0)),
                  pl.BlockSpec((t, W_GROUP), lambda i: (0, 0))],
        out_specs=[pl.BlockSpec((t, W_GROUP), lambda i: (i + first_block, 0)),
                   pl.BlockSpec((1, N_GATE, HEAD_DIM, HEAD_DIM), lambda i: (i, 0, 0, 0))],
        out_shape=[jax.ShapeDtypeStruct((N_TOK, W_GROUP), F32),
                   jax.ShapeDtypeStruct((n_seq, N_GATE, HEAD_DIM, HEAD_DIM), F32)],
        into=into, grid=(n_seq,), compiler_params=_cparams("arbitrary"), name="ret")


def _rope_tables(t):
    pos = np.arange(t)
    row = (pos // GRID_W).astype(np.float32)
    col = (pos % GRID_W).astype(np.float32)
    nf = HEAD_DIM // 4
    inv_freq = jnp.power(ROPE_BASE, -jnp.arange(nf, dtype=F32) / nf)
    ang_r = jnp.asarray(row)[:, None] * inv_freq[None, :]
    ang_c = jnp.asarray(col)[:, None] * inv_freq[None, :]
    cos = jnp.concatenate([jnp.cos(ang_r)] * 2 + [jnp.cos(ang_c)] * 2, axis=1)
    sin = jnp.concatenate([-jnp.sin(ang_r), jnp.sin(ang_r), -jnp.sin(ang_c), jnp.sin(ang_c)], axis=1)
    return jnp.tile(cos, (1, N_HEADS)), jnp.tile(sin, (1, N_HEADS))


def _head_rms(x, w, head_sum):
    return x * lax.rsqrt(_mm(x * x, head_sum, HI) * (1.0 / HEAD_DIM) + EPS) * w


def _ctx_attn_kernel(n_ref, qw_ref, kw_ref, o_ref, k_out, v_out):
    n = n_ref[...]
    head_sum = _head_sum_matrix()
    head = _iota((1, W_GROUP), 1) // HEAD_DIM
    q = (_head_rms(n[:, :W_GROUP], qw_ref[...], head_sum) * (HEAD_DIM ** -0.5)).astype(BF16)
    k = _head_rms(n[:, W_GROUP:2 * W_GROUP], kw_ref[...], head_sum)
    v = n[:, 2 * W_GROUP:]
    k_out[0] = k
    v_out[0] = v
    kt_b = k.T.astype(BF16)
    o = jnp.zeros((SEQ, W_GROUP), F32)
    for h in range(N_HEADS):
        s = _mm(jnp.where(head == h, q, 0.0), kt_b)
        p = jnp.exp(s - jnp.max(s, axis=-1, keepdims=True))
        o = o + _mm(p.astype(BF16), jnp.where(head == h, v, 0.0).astype(BF16)) / jnp.sum(p, axis=-1, keepdims=True)
    o_ref[...] = o


def _ctx_attn(n_all, qw, kw, into):
    return _seq_call(
        _ctx_attn_kernel, (n_all, qw, kw),
        in_specs=[pl.BlockSpec((SEQ, 3 * W_GROUP), lambda i: (i, 0)),
                  pl.BlockSpec((1, W_GROUP), lambda i: (0, 0)),
                  pl.BlockSpec((1, W_GROUP), lambda i: (0, 0))],
        out_specs=[pl.BlockSpec((SEQ, W_GROUP), lambda i: (i, 0)),
                   pl.BlockSpec((1, SEQ, W_GROUP), lambda i: (i, 0, 0)),
                   pl.BlockSpec((1, SEQ, W_GROUP), lambda i: (i, 0, 0))],
        out_shape=[jax.ShapeDtypeStruct((N_TOK, W_GROUP), F32),
                   jax.ShapeDtypeStruct((BATCH, SEQ, W_GROUP), F32),
                   jax.ShapeDtypeStruct((BATCH, SEQ, W_GROUP), F32)],
        into=into, grid=(BATCH,), compiler_params=_cparams("arbitrary"), name="ctx_attn")


def _nat_kernel(n_ref, ck_ref, cv_ref, bias_ref, qw_ref, kw_ref, o_ref, q_s, kt_s, ckt_s):
    h = pl.program_id(1)

    @pl.when(h == 0)
    def _():
        head_sum = _head_sum_matrix()
        n = n_ref[...]
        q_s[...] = (_head_rms(n[:, :W_GROUP], qw_ref[...], head_sum) * (HEAD_DIM ** -0.5)).astype(BF16)
        kt_s[...] = _head_rms(n[:, W_GROUP:2 * W_GROUP], kw_ref[...], head_sum).T.astype(BF16)
        ckt_s[...] = ck_ref[0].T.astype(BF16)
        o_ref[...] = jnp.zeros_like(o_ref)

    mine = _iota((1, W_GROUP), 1) // HEAD_DIM == h
    v = jnp.where(mine, n_ref[:, 2 * W_GROUP:], 0.0).astype(BF16)
    cv = jnp.where(mine, cv_ref[0], 0.0).astype(BF16)
    for qt in range(DEC_SEQ // Q_TILE):
        rows = slice(qt * Q_TILE, (qt + 1) * Q_TILE)
        q = jnp.where(mine, q_s[rows, :], 0.0)
        s_loc = _mm(q, kt_s[...]) + bias_ref[0, rows, :]
        s_ctx = _mm(q, ckt_s[...])
        m = jnp.maximum(jnp.max(s_loc, axis=-1, keepdims=True), jnp.max(s_ctx, axis=-1, keepdims=True))
        p_loc = jnp.exp(s_loc - m)
        p_ctx = jnp.exp(s_ctx - m)
        den = jnp.sum(p_loc, axis=-1, keepdims=True) + jnp.sum(p_ctx, axis=-1, keepdims=True)
        o_ref[rows, :] += (_mm(p_loc.astype(BF16), v) + _mm(p_ctx.astype(BF16), cv)) / den


def _nat(n_all, ck, cv, bias, qw, kw, into):
    first_block = N_CTX // DEC_SEQ
    return _seq_call(
        _nat_kernel, (n_all, ck, cv, bias, qw, kw),
        in_specs=[pl.BlockSpec((DEC_SEQ, 3 * W_GROUP), lambda b, h: (b + first_block, 0)),
                  pl.BlockSpec((1, PAST_LEN, W_GROUP), lambda b, h: (b, 0, 0)),
                  pl.BlockSpec((1, PAST_LEN, W_GROUP), lambda b, h: (b, 0, 0)),
                  pl.BlockSpec((1, DEC_SEQ, DEC_SEQ), lambda b, h: (h, 0, 0)),
                  pl.BlockSpec((1, W_GROUP), lambda b, h: (0, 0)),
                  pl.BlockSpec((1, W_GROUP), lambda b, h: (0, 0))],
        out_specs=pl.BlockSpec((DEC_SEQ, W_GROUP), lambda b, h: (b + first_block, 0)),
        out_shape=jax.ShapeDtypeStruct((N_TOK, W_GROUP), F32),
        scratch_shapes=[pltpu.VMEM((DEC_SEQ, W_GROUP), BF16), pltpu.VMEM((W_GROUP, DEC_SEQ), BF16),
                        pltpu.VMEM((W_GROUP, PAST_LEN), BF16)],
        into=into, grid=(DEC_BATCH, N_HEADS), compiler_params=_cparams("arbitrary", "arbitrary"), name="nat")


def _nat_bias(rpb):
    rows_n = DEC_SEQ // GRID_W
    kh = min(WIN_ROWS, rows_n)
    r = np.arange(rows_n)
    c = np.arange(GRID_W)
    r0 = np.clip(r - kh // 2, 0, rows_n - kh)
    c0 = np.clip(c - WIN_COLS // 2, 0, GRID_W - WIN_COLS)
    col_in = (c[None, :] >= c0[:, None]) & (c[None, :] < c0[:, None] + WIN_COLS)
    col_idx = np.clip(c[None, :] - c[:, None], -(WIN_COLS - 1), WIN_COLS - 1) + WIN_COLS - 1
    col_hot = (col_idx[..., None] == np.arange(2 * WIN_COLS - 1)).astype(np.float32)
    table = jnp.where(col_in, jnp.einsum('hab,qkb->haqk', rpb.astype(F32), col_hot, precision=HI), NEG_BIG)
    per_row = []
    for row in range(rows_n):
        first = r0[row] - row + WIN_ROWS - 1
        window = table[:, first:first + kh].transpose(0, 2, 1, 3)
        per_row.append(jnp.pad(window, ((0, 0), (0, 0), (r0[row], rows_n - kh - r0[row]), (0, 0)),
                               constant_values=NEG_BIG))
    return jnp.stack(per_row, axis=1).reshape(N_HEADS, DEC_SEQ, DEC_SEQ)


GSEL_LANE = N_EXPERTS


def _pack_router(we, be, wg, bg):
    pad = LANES - N_EXPERTS - N_GROUPS
    w = jnp.concatenate([we, wg, jnp.zeros((D_MODEL, pad), F32)], axis=1)
    b = jnp.concatenate([be, bg, jnp.zeros((pad,), F32)]).reshape(1, LANES)
    hi = w.astype(BF16)
    lo = (w - hi.astype(F32)).astype(BF16)
    return jnp.concatenate([hi, lo], axis=1), b


def _lane_min_where(mask, lane):
    return jnp.min(jnp.where(mask, lane, LANES), axis=-1, keepdims=True)


def _outproj_kernel(x_ref, m0, m1, m2, m3, mod_ref, nw_ref, w_ref, rw_ref, rb_ref, x_out, hf_out, route_out):
    acc = None
    for i, m_ref in enumerate((m0, m1, m2, m3)):
        part = _mm(m_ref[...].astype(BF16), w_ref[i * W_GROUP:(i + 1) * W_GROUP, :])
        acc = part if acc is None else acc + part
    x = x_ref[...] + mod_ref[0, 2:3, :] * acc
    x_out[...] = x
    y = x * lax.rsqrt(jnp.mean(x * x, axis=-1, keepdims=True) + EPS) * nw_ref[...]
    hf = y * (1.0 + mod_ref[0, 4:5, :]) + mod_ref[0, 3:4, :]
    hf_hi = hf.astype(BF16)
    hf_out[...] = hf_hi

    hf_lo = (hf - hf_hi.astype(F32)).astype(BF16)
    both = _mm(hf_hi, rw_ref[...])
    logits = both[:, :LANES] + both[:, LANES:] + _mm(hf_lo, rw_ref[:, :LANES]) + rb_ref[...]
    lane = _iota(logits.shape, 1)
    is_g = (lane >= N_EXPERTS) & (lane < N_EXPERTS + N_GROUPS)
    gl = jnp.where(is_g, logits, NEG_BIG)
    ge = jnp.exp(gl - jnp.max(gl, axis=-1, keepdims=True))
    gp = jnp.where(is_g, ge / jnp.sum(ge, axis=-1, keepdims=True), -1.0)
    gw = jnp.max(gp, axis=-1, keepdims=True)
    gsel = _lane_min_where(gp == gw, lane) - N_EXPERTS
    in_grp = (lane // EXPERTS_PER_GROUP == gsel) & (lane < N_EXPERTS)
    el = jnp.where(in_grp, logits, NEG_BIG)
    ee = jnp.exp(el - jnp.max(el, axis=-1, keepdims=True))
    ep = jnp.where(in_grp, ee / jnp.sum(ee, axis=-1, keepdims=True), -1.0)
    t1 = jnp.max(ep, axis=-1, keepdims=True)
    i1 = _lane_min_where(ep == t1, lane)
    ep2 = jnp.where(lane == i1, -1.0, ep)
    t2 = jnp.max(ep2, axis=-1, keepdims=True)
    i2 = _lane_min_where(ep2 == t2, lane)
    tsum = t1 + t2
    combine = jnp.where(lane == i1, gw * (t1 / tsum), 0.0) + jnp.where(lane == i2, gw * (t2 / tsum), 0.0)
    route_out[...] = jnp.where(lane == GSEL_LANE, gsel.astype(F32), combine)


def _outproj(x, mixed, mod, norm_w, w_out, rw, rb):
    tile = lambda w: pl.BlockSpec((TOK_TILE, w), lambda i: (i, 0))
    whole = lambda a: pl.BlockSpec(a.shape, lambda i: (0,) * a.ndim)
    return pl.pallas_call(
        _outproj_kernel,
        grid=(N_TOK // TOK_TILE,),
        in_specs=[tile(D_MODEL)] + [tile(W_GROUP)] * 4
                 + [pl.BlockSpec((1, SUBLANES, D_MODEL), lambda i: (_cond_of_tile(i), 0, 0)),
                    pl.BlockSpec((1, D_MODEL), lambda i: (0, 0)), whole(w_out), whole(rw), whole(rb)],
        out_specs=[tile(D_MODEL), tile(D_MODEL), tile(LANES)],
        out_shape=[jax.ShapeDtypeStruct((N_TOK, D_MODEL), F32), jax.ShapeDtypeStruct((N_TOK, D_MODEL), BF16),
                   jax.ShapeDtypeStruct((N_TOK, LANES), F32)],
        compiler_params=_cparams("arbitrary"),
        name="outproj",
    )(x, *mixed, mod, norm_w.reshape(1, D_MODEL), w_out, rw, rb)


SEG_BLK = 32
LOCAL_ROWS = TOK_TILE + N_GROUPS * SEG_BLK
N_TOK_TILES = N_TOK // TOK_TILE
MOE_ROWS = -(-(N_TOK + N_TOK_TILES * N_GROUPS * (SEG_BLK - 1) + N_GROUPS * (MOE_TILE - 1)) // MOE_TILE) * MOE_TILE


def _moe_tables(gsel):
    groups = jnp.arange(N_GROUPS, dtype=jnp.int32)
    onehot = (gsel.reshape(N_TOK_TILES, TOK_TILE, 1) == groups).astype(jnp.int32)
    rank = jnp.cumsum(onehot, axis=1) - onehot
    nblk = (jnp.sum(onehot, axis=1) + SEG_BLK - 1) // SEG_BLK
    loc_blk = jnp.cumsum(nblk, axis=1) - nblk
    blocks_per_tile = MOE_TILE // SEG_BLK
    grp_tiles = (jnp.sum(nblk, axis=0) + blocks_per_tile - 1) // blocks_per_tile
    grp_tile_start = jnp.cumsum(grp_tiles) - grp_tiles
    dst_blk = grp_tile_start[None, :] * blocks_per_tile + jnp.cumsum(nblk, axis=0) - nblk
    local_pos = jnp.sum(onehot * (loc_blk[:, None, :] * SEG_BLK + rank), axis=2)
    tile_idx = jnp.arange(MOE_ROWS // MOE_TILE, dtype=jnp.int32)
    tile_group = jnp.clip(jnp.sum(tile_idx[:, None] >= grp_tile_start[None, :], axis=1) - 1, 0, N_GROUPS - 1)
    tile_valid = tile_idx < jnp.sum(grp_tiles)
    flat = lambda a: a.reshape(-1).astype(jnp.int32)
    return local_pos.astype(jnp.int32), flat(nblk), flat(loc_blk), flat(dst_blk), flat(tile_group), flat(tile_valid)


def _segment_copies(t, nblk, loc_blk, dst_blk, make_copies, action):
    for g in range(N_GROUPS):
        k = t * N_GROUPS + g

        @pl.loop(0, nblk[k])
        def _(b):
            local = pl.multiple_of((loc_blk[k] + b) * SEG_BLK, SEG_BLK)
            sorted_row = pl.multiple_of((dst_blk[k] + b) * SEG_BLK, SEG_BLK)
            for cp in make_copies(local, sorted_row):
                action(cp)


def _dispatch_kernel(nblk, loc_blk, dst_blk, hf_ref, rt_ref, lp_ref, xs_in, rs_in, xs_hbm, rs_hbm, xbuf, rbuf, sem):
    t = pl.program_id(0)
    onehot = _iota((LOCAL_ROWS, TOK_TILE), 0) == lp_ref[0]
    xbuf[...] = _mm(onehot.astype(BF16), hf_ref[...]).astype(BF16)
    rbuf[...] = _mm(onehot.astype(F32), rt_ref[...], HI)

    def copies(local, sorted_row):
        return (pltpu.make_async_copy(xbuf.at[pl.ds(local, SEG_BLK)], xs_hbm.at[pl.ds(sorted_row, SEG_BLK)], sem),
                pltpu.make_async_copy(rbuf.at[pl.ds(local, SEG_BLK)], rs_hbm.at[pl.ds(sorted_row, SEG_BLK)], sem))

    _segment_copies(t, nblk, loc_blk, dst_blk, copies, lambda cp: cp.start())
    _segment_copies(t, nblk, loc_blk, dst_blk, copies, lambda cp: cp.wait())


def _dispatch(hf, route, local_pos, nblk, loc_blk, dst_blk):
    grid_spec = pltpu.PrefetchScalarGridSpec(
        num_scalar_prefetch=3,
        grid=(N_TOK_TILES,),
        in_specs=[pl.BlockSpec((TOK_TILE, D_MODEL), lambda t, *_: (t, 0)),
                  pl.BlockSpec((TOK_TILE, LANES), lambda t, *_: (t, 0)),
                  pl.BlockSpec((1, 1, TOK_TILE), lambda t, *_: (t, 0, 0)),
                  pl.BlockSpec(memory_space=pl.ANY), pl.BlockSpec(memory_space=pl.ANY)],
        out_specs=[pl.BlockSpec(memory_space=pl.ANY), pl.BlockSpec(memory_space=pl.ANY)],
        scratch_shapes=[pltpu.VMEM((LOCAL_ROWS, D_MODEL), BF16), pltpu.VMEM((LOCAL_ROWS, LANES), F32),
                        pltpu.SemaphoreType.DMA],
    )
    return pl.pallas_call(
        _dispatch_kernel,
        grid_spec=grid_spec,
        out_shape=[jax.ShapeDtypeStruct((MOE_ROWS, D_MODEL), BF16), jax.ShapeDtypeStruct((MOE_ROWS, LANES), F32)],
        input_output_aliases={6: 0, 7: 1},
        compiler_params=_cparams("arbitrary"),
        name="dispatch",
    )(nblk, loc_blk, dst_blk, hf, route, local_pos.reshape(N_TOK_TILES, 1, TOK_TILE),
      jnp.zeros((MOE_ROWS, D_MODEL), BF16), jnp.zeros((MOE_ROWS, LANES), F32))


def _moe_kernel(tile_group, tile_valid, x_ref, r_ref, wg_ref, wu_ref, wd_ref, y_ref, wg_b, wu_b, wd_b):
    i = pl.program_id(0)
    g = tile_group[i]

    @pl.when((i == 0) | (g != tile_group[jnp.maximum(i - 1, 0)]))
    def _():
        for e in range(EXPERTS_PER_GROUP):
            wg_b[e] = wg_ref[0, e].astype(BF16)
            wu_b[e] = wu_ref[0, e].astype(BF16)
            wd_b[e] = wd_ref[0, e].astype(BF16)

    @pl.when(tile_valid[i] > 0)
    def _():
        x = x_ref[...]
        route = r_ref[...]
        lane = _iota(route.shape, 1)
        acc = jnp.zeros((MOE_TILE, D_MODEL), F32)
        for e in range(EXPERTS_PER_GROUP):
            cw = jnp.sum(jnp.where(lane == g * EXPERTS_PER_GROUP + e, route, 0.0), axis=-1, keepdims=True)
            act = _silu(_mm(x, wg_b[e])) * _mm(x, wu_b[e]) * cw
            acc = acc + _mm(act.astype(BF16), wd_b[e])
        y_ref[...] = acc

    @pl.when(tile_valid[i] == 0)
    def _():
        y_ref[...] = jnp.zeros_like(y_ref)


def _moe(xs, rs, tile_group, tile_valid, wg, wu, wd, layer):
    w_spec = lambda shape: pl.BlockSpec((1, EXPERTS_PER_GROUP) + shape,
                                        lambda i, tg, tv: (layer * N_GROUPS + tg[i], 0, 0, 0),
                                        pipeline_mode=pl.Buffered(1))
    grid_spec = pltpu.PrefetchScalarGridSpec(
        num_scalar_prefetch=2,
        grid=(MOE_ROWS // MOE_TILE,),
        in_specs=[pl.BlockSpec((MOE_TILE, D_MODEL), lambda i, tg, tv: (i, 0)),
                  pl.BlockSpec((MOE_TILE, LANES), lambda i, tg, tv: (i, 0)),
                  w_spec((D_MODEL, EXPERT_FF)), w_spec((D_MODEL, EXPERT_FF)), w_spec((EXPERT_FF, D_MODEL))],
        out_specs=pl.BlockSpec((MOE_TILE, D_MODEL), lambda i, tg, tv: (i, 0)),
        scratch_shapes=[pltpu.VMEM((EXPERTS_PER_GROUP, D_MODEL, EXPERT_FF), BF16),
                        pltpu.VMEM((EXPERTS_PER_GROUP, D_MODEL, EXPERT_FF), BF16),
                        pltpu.VMEM((EXPERTS_PER_GROUP, EXPERT_FF, D_MODEL), BF16)],
    )
    return pl.pallas_call(
        _moe_kernel,
        grid_spec=grid_spec,
        out_shape=jax.ShapeDtypeStruct((MOE_ROWS, D_MODEL), F32),
        compiler_params=_cparams("arbitrary"),
        name="moe",
    )(tile_group, tile_valid, xs, rs, wg, wu, wd)


def _combine_kernel(nblk, loc_blk, dst_blk, x_ref, lp_ref, mod_ref, ys_hbm, o_ref, ybuf, sem):
    t = pl.program_id(0)
    ybuf[...] = jnp.zeros_like(ybuf)

    def copies(local, sorted_row):
        return (pltpu.make_async_copy(ys_hbm.at[pl.ds(sorted_row, SEG_BLK)], ybuf.at[pl.ds(local, SEG_BLK)], sem),)

    _segment_copies(t, nblk, loc_blk, dst_blk, copies, lambda cp: cp.start())
    _segment_copies(t, nblk, loc_blk, dst_blk, copies, lambda cp: cp.wait())

    onehot = (_iota((TOK_TILE, LOCAL_ROWS), 1) == lp_ref[...]).astype(BF16)
    ys = ybuf[...]
    hi = ys.astype(BF16)
    lo = (ys - hi.astype(F32)).astype(BF16)
    y = _mm(onehot, hi) + _mm(onehot, lo)
    o_ref[...] = x_ref[...] + mod_ref[0, 5:6, :] * y


def _combine(x, ys, mod, local_pos, nblk, loc_blk, dst_blk):
    tile = pl.BlockSpec((TOK_TILE, D_MODEL), lambda t, *_: (t, 0))
    grid_spec = pltpu.PrefetchScalarGridSpec(
        num_scalar_prefetch=3,
        grid=(N_TOK_TILES,),
        in_specs=[tile, pl.BlockSpec((TOK_TILE, 1), lambda t, *_: (t, 0)),
                  pl.BlockSpec((1, SUBLANES, D_MODEL), lambda t, *_: (_cond_of_tile(t), 0, 0)),
                  pl.BlockSpec(memory_space=pl.ANY)],
        out_specs=tile,
        scratch_shapes=[pltpu.VMEM((LOCAL_ROWS, D_MODEL), F32), pltpu.SemaphoreType.DMA],
    )
    return pl.pallas_call(
        _combine_kernel,
        grid_spec=grid_spec,
        out_shape=jax.ShapeDtypeStruct((N_TOK, D_MODEL), F32),
        compiler_params=_cparams("arbitrary"),
        name="combine",
    )(nblk, loc_blk, dst_blk, x, local_pos.reshape(N_TOK, 1), mod, ys)


def _lane_row(v):
    v = v.reshape(-1).astype(F32)
    return jnp.concatenate([v, jnp.zeros((LANES - v.shape[0],), F32)]).reshape(1, LANES)


def _pad_rows(w):
    return jnp.concatenate([w, jnp.zeros((SUBLANES - w.shape[0], w.shape[1]), w.dtype)], axis=0)


def kernel(x_prompt, x_sample, state_gdn, state_ret, cache_nat_k, cache_nat_v, c, c_ctx, ada_w, ada_b, norm_mix_w, norm_ffn_w, w_in, gdn_conv_w, gdn_a_log, gdn_dt_bias, gdn_norm_w, ret_gamma_logit, nat_q_norm_w, nat_k_norm_w, nat_rpb, sc_conv_w, w_out, router_group_w, router_group_b, router_expert_w, router_expert_b, moe_w_gate, moe_w_up, moe_w_down):
    x = jnp.concatenate([x_prompt.reshape(N_CTX, D_MODEL), x_sample.reshape(N_LAT, D_MODEL)], axis=0)
    cond = jnp.concatenate([c_ctx[None, :], c, jnp.zeros((SUBLANES - 1 - DEC_BATCH, D_MODEL), F32)], axis=0)
    ada = _ada(cond, ada_w, ada_b).reshape(DEPTH, SUBLANES, 6, D_MODEL)
    cos, sin = _rope_tables(DEC_SEQ)
    zero_state = jnp.zeros((BATCH, N_GATE, HEAD_DIM, HEAD_DIM), F32)
    lat_block = N_CTX // DEC_SEQ
    gdn_list, ret_list, k_list, v_list = [], [], [], []
    mixed = [jnp.zeros((N_TOK, W_GROUP), F32) for _ in range(4)]
    for l in range(DEPTH):
        mod = jnp.concatenate([ada[l, :1 + DEC_BATCH], jnp.zeros((1 + DEC_BATCH, SUBLANES - 6, D_MODEL), F32)], axis=1)
        a_gdn, a_ret, a_nat, a_sc, a_gate = _inproj(x, mod, norm_mix_w[l], _pack_w_in(w_in[l]))

        conv_w = _pad_rows(gdn_conv_w[l])
        a_log, dt_b = _lane_row(gdn_a_log[l]), _lane_row(gdn_dt_bias[l])
        gnw = jnp.tile(gdn_norm_w[l], N_HEADS).reshape(1, W_GROUP)
        o_gdn, s_gdn = _gdn(a_gdn, a_gate, conv_w, a_log, dt_b, gnw, zero_state, SEQ, BATCH, 0, into=mixed[0])
        s0 = state_gdn[:, l].reshape(DEC_BATCH, N_GATE, HEAD_DIM, HEAD_DIM)
        o_gdn, _ = _gdn(a_gdn, a_gate, conv_w, a_log, dt_b, gnw, s0, DEC_SEQ, DEC_BATCH, lat_block, into=o_gdn)

        logit = jnp.repeat(ret_gamma_logit[l].astype(F32), HEAD_DIM, axis=1)
        o_ret, s_ret = _ret(a_ret, logit, zero_state, cos[:SEQ], sin[:SEQ], SEQ, BATCH, 0, False, into=mixed[1])
        s0 = state_ret[:, l].reshape(DEC_BATCH, N_GATE, HEAD_DIM, HEAD_DIM)
        o_ret, _ = _ret(a_ret, logit, s0, cos, sin, DEC_SEQ, DEC_BATCH, lat_block, True, into=o_ret)

        qw = jnp.tile(nat_q_norm_w[l], N_HEADS).reshape(1, W_GROUP)
        kw = jnp.tile(nat_k_norm_w[l], N_HEADS).reshape(1, W_GROUP)
        o_nat, n_k, n_v = _ctx_attn(a_nat, qw, kw, mixed[2])
        o_nat = _nat(a_nat, cache_nat_k[:, l].reshape(DEC_BATCH, PAST_LEN, W_GROUP),
                     cache_nat_v[:, l].reshape(DEC_BATCH, PAST_LEN, W_GROUP), _nat_bias(nat_rpb[l]), qw, kw, o_nat)

        sc_w = _pad_rows(sc_conv_w[l])
        o_sc = _sconv(a_sc, sc_w, SEQ, BATCH, 0, into=mixed[3])
        o_sc = _sconv(a_sc, sc_w, DEC_SEQ, DEC_BATCH, lat_block, into=o_sc)

        mixed = [o_gdn, o_ret, o_nat, o_sc]
        rw, rb = _pack_router(router_expert_w[l], router_expert_b[l], router_group_w[l], router_group_b[l])
        x_mid, hf, route = _outproj(x, mixed, mod, norm_ffn_w[l], w_out[l].astype(BF16), rw, rb)

        local_pos, nblk, loc_blk, dst_blk, tile_group, tile_valid = _moe_tables(route[:, GSEL_LANE].astype(jnp.int32))
        xs, rs = _dispatch(hf, route, local_pos, nblk, loc_blk, dst_blk)
        to_group = lambda w: w.reshape((DEPTH * N_GROUPS, EXPERTS_PER_GROUP) + w.shape[2:])
        ys = _moe(xs, rs, tile_group, tile_valid, to_group(moe_w_gate), to_group(moe_w_up), to_group(moe_w_down), l)
        x = _combine(x_mid, ys, mod, local_pos, nblk, loc_blk, dst_blk)

        gdn_list.append(s_gdn.reshape(BATCH, 2, N_HEADS, HEAD_DIM, HEAD_DIM))
        ret_list.append(s_ret.reshape(BATCH, 2, N_HEADS, HEAD_DIM, HEAD_DIM))
        k_list.append(n_k.reshape(BATCH, SEQ, N_HEADS, HEAD_DIM))
        v_list.append(n_v.reshape(BATCH, SEQ, N_HEADS, HEAD_DIM))
    return (x[:N_CTX].reshape(BATCH, SEQ, D_MODEL), x[N_CTX:].reshape(DEC_BATCH, DEC_SEQ, D_MODEL),
            jnp.stack(gdn_list, axis=1), jnp.stack(ret_list, axis=1),
            jnp.stack(k_list, axis=1), jnp.stack(v_list, axis=1))
```

```python
import functools

import numpy as np
import jax
import jax.numpy as jnp
from jax import lax
from jax.experimental import pallas as pl
from jax.experimental.pallas import tpu as pltpu

D_MODEL = 1024
BATCH = 16
SEQ = 256
DEPTH = 2
DEC_BATCH = 2
DEC_SEQ = 1024
PAST_LEN = 256
GRID_W = 64
HEAD_DIM = 64
W_GROUP = D_MODEL // 4
N_HEADS = W_GROUP // HEAD_DIM
CHUNK = 64
WIN_ROWS = 8
WIN_COLS = 16
ROPE_BASE = 10000.0
N_GROUPS = 4
EXPERTS_PER_GROUP = 8
N_EXPERTS = N_GROUPS * EXPERTS_PER_GROUP
EXPERT_FF = 256
GROUP_FF = EXPERTS_PER_GROUP * EXPERT_FF
EPS = 1e-6

N_CTX = BATCH * SEQ
N_LAT = DEC_BATCH * DEC_SEQ
N_TOK = N_CTX + N_LAT
LANES = 128
SUBLANES = 8
TOK_TILE = 512
MOE_TILE = 256
Q_TILE = 256
VMEM_LIMIT = 48 * 1024 * 1024
NEG_BIG = -1e30
N_GATE = 2 * N_HEADS

F32 = jnp.float32
BF16 = jnp.bfloat16
HI = lax.Precision.HIGHEST


def _mm(a, b, prec=None):
    return lax.dot_general(a, b, (((1,), (0,)), ((), ())), precision=prec, preferred_element_type=F32)


def _mm_nt(a, b, prec=None):
    return lax.dot_general(a, b, (((1,), (1,)), ((), ())), precision=prec, preferred_element_type=F32)


def _mm_tn(a, b, prec=None):
    return lax.dot_general(a, b, (((0,), (0,)), ((), ())), precision=prec, preferred_element_type=F32)


def _bmm(a, b):
    return _mm(a.astype(BF16), b.astype(BF16))


def _sigmoid(x):
    return 1.0 / (1.0 + jnp.exp(-x))


def _silu(x):
    return x * _sigmoid(x)


def _softplus(x):
    return jnp.maximum(x, 0.0) + jnp.log(1.0 + jnp.exp(-jnp.abs(x)))


def _iota(shape, dim):
    return lax.broadcasted_iota(jnp.int32, shape, dim)


def _cparams(*sem):
    return pltpu.CompilerParams(dimension_semantics=sem, vmem_limit_bytes=VMEM_LIMIT)


def _cond_of_tile(i):
    n_ctx_tiles = N_CTX // TOK_TILE
    return jnp.where(i < n_ctx_tiles, 0, 1 + (i - n_ctx_tiles) // (DEC_SEQ // TOK_TILE))


def _head_sum_matrix():
    return (_iota((W_GROUP, W_GROUP), 0) // HEAD_DIM == _iota((W_GROUP, W_GROUP), 1) // HEAD_DIM).astype(F32)


def _ada_kernel(c_ref, w_ref, b_ref, o_ref):
    o_ref[0] = _mm(_silu(c_ref[...]), w_ref[0], HI) + b_ref[0]


def _ada(cond, ada_w, ada_b):
    tn = 1536
    n_out = 6 * D_MODEL
    return pl.pallas_call(
        _ada_kernel,
        grid=(DEPTH, n_out // tn),
        in_specs=[pl.BlockSpec((SUBLANES, D_MODEL), lambda l, j: (0, 0)),
                  pl.BlockSpec((1, D_MODEL, tn), lambda l, j: (l, 0, j)),
                  pl.BlockSpec((1, 1, tn), lambda l, j: (l, 0, j))],
        out_specs=pl.BlockSpec((1, SUBLANES, tn), lambda l, j: (l, 0, j)),
        out_shape=jax.ShapeDtypeStruct((DEPTH, SUBLANES, n_out), F32),
        compiler_params=_cparams("arbitrary", "arbitrary"),
        name="ada",
    )(cond, ada_w, ada_b.reshape(DEPTH, 1, n_out))


IN_WIDTHS = (4 * W_GROUP, 4 * W_GROUP, 3 * W_GROUP, 3 * W_GROUP, LANES)
IN_PACKED = sum(IN_WIDTHS)


def _pack_w_in(w):
    sizes = (3 * W_GROUP, W_GROUP, N_GATE, N_GATE, 3 * W_GROUP, W_GROUP, 3 * W_GROUP, 3 * W_GROUP)
    offs = np.concatenate([[0], np.cumsum(sizes)])
    seg = [w[:, offs[i]:offs[i + 1]] for i in range(len(sizes))]
    gate = jnp.concatenate([seg[2], seg[3], jnp.zeros((D_MODEL, LANES - 2 * N_GATE), w.dtype)], axis=1)
    return jnp.concatenate([seg[0], seg[1], seg[4], seg[5], seg[6], seg[7], gate], axis=1).astype(BF16)


def _inproj_kernel(x_ref, mod_ref, nw_ref, w_ref, *o_refs):
    x = x_ref[...]
    y = x * lax.rsqrt(jnp.mean(x * x, axis=-1, keepdims=True) + EPS) * nw_ref[...]
    h = (y * (1.0 + mod_ref[0, 1:2, :]) + mod_ref[0, 0:1, :]).astype(BF16)
    off = 0
    for o_ref, width in zip(o_refs, IN_WIDTHS):
        o_ref[...] = _mm(h, w_ref[:, off:off + width])
        off += width


def _inproj(x, mod, norm_w, w_packed):
    return pl.pallas_call(
        _inproj_kernel,
        grid=(N_TOK // TOK_TILE,),
        in_specs=[pl.BlockSpec((TOK_TILE, D_MODEL), lambda i: (i, 0)),
                  pl.BlockSpec((1, SUBLANES, D_MODEL), lambda i: (_cond_of_tile(i), 0, 0)),
                  pl.BlockSpec((1, D_MODEL), lambda i: (0, 0)),
                  pl.BlockSpec((D_MODEL, IN_PACKED), lambda i: (0, 0))],
        out_specs=[pl.BlockSpec((TOK_TILE, w), lambda i: (i, 0)) for w in IN_WIDTHS],
        out_shape=[jax.ShapeDtypeStruct((N_TOK, w), F32) for w in IN_WIDTHS],
        compiler_params=_cparams("arbitrary"),
        name="inproj",
    )(x, mod, norm_w.reshape(1, D_MODEL), w_packed)


def _seq_call(kernel_fn, args, in_specs, out_specs, out_shape, into, **kwargs):
    aliases = {}
    if into is not None:
        n_in = len(args)
        inner = kernel_fn
        kernel_fn = lambda *refs: inner(*refs[:n_in], *refs[n_in + 1:])
        aliases = {n_in: 0}
        args = list(args) + [into]
        in_specs = list(in_specs) + [pl.BlockSpec(memory_space=pl.ANY)]
    return pl.pallas_call(kernel_fn, in_specs=in_specs, out_specs=out_specs, out_shape=out_shape,
                          input_output_aliases=aliases, **kwargs)(*args)


def _shift_rows(p, t):
    row = _iota(p.shape, 0)
    prev = jnp.where(row == 0, 0.0, pltpu.roll(p, 1, 0))
    nxt = jnp.where(row == t - 1, 0.0, pltpu.roll(p, t - 1, 0))
    return prev, nxt


def _conv3(x, w_ref, t):
    prev, nxt = _shift_rows(x, t)
    return w_ref[0:1, :] * prev + w_ref[1:2, :] * x + w_ref[2:3, :] * nxt


def _sconv_kernel(s_ref, w_ref, o_ref, *, t):
    s = s_ref[...]
    p = s[:, W_GROUP:2 * W_GROUP] * s[:, 2 * W_GROUP:]
    o_ref[...] = s[:, :W_GROUP] * _conv3(p, w_ref, t)


def _sconv(s_all, w, t, n_seq, first_block, into=None):
    return _seq_call(
        functools.partial(_sconv_kernel, t=t), (s_all, w),
        in_specs=[pl.BlockSpec((t, 3 * W_GROUP), lambda i: (i + first_block, 0)),
                  pl.BlockSpec((SUBLANES, W_GROUP), lambda i: (0, 0))],
        out_specs=pl.BlockSpec((t, W_GROUP), lambda i: (i + first_block, 0)),
        out_shape=jax.ShapeDtypeStruct((N_TOK, W_GROUP), F32),
        into=into, grid=(n_seq,), compiler_params=_cparams("arbitrary"), name="sconv")


GDN_GROUP_CHUNKS = 4
GDN_CHAINS = GDN_GROUP_CHUNKS * N_GATE
GDN_PAIRS = GDN_GROUP_CHUNKS * N_HEADS


def _gdn_kernel(a_ref, gate_ref, convw_ref, alog_ref, dtb_ref, nw_ref, s0_ref, o_ref, sfin_ref,
                q_s, kv_s, kt_s, gc_s, eg_s, beta_s, gcrow_s, ekdrow_s, cdec_s, uo_s, wq_s, attn_s, kdt_s,
                st_s, wsqs_s, kk_s, d_s, off_s, m1_s, m2_s, p_s, m3_s, rhs_s, px_s, *, t):
    n_chunks = t // CHUNK
    a = a_ref[...]
    qkv = _silu(_conv3(a[:, :3 * W_GROUP], convw_ref, t))
    q = qkv[:, :W_GROUP]
    k = qkv[:, W_GROUP:2 * W_GROUP]
    v = qkv[:, 2 * W_GROUP:]
    head_sum = _head_sum_matrix()
    q = q * lax.rsqrt(_mm(q * q, head_sum, HI) + EPS) * (HEAD_DIM ** -0.5)
    k = k * lax.rsqrt(_mm(k * k, head_sum, HI) + EPS)
    for h in range(N_HEADS):
        hs = slice(h * HEAD_DIM, (h + 1) * HEAD_DIM)
        q_s[h] = q[:, hs]
        kv_s[h] = jnp.concatenate([k[:, hs], v[:, hs]], axis=1)
    k_t = k.T
    for c in range(n_chunks):
        kt_s[c] = k_t[:, c * CHUNK:(c + 1) * CHUNK]

    gates = gate_ref[...]
    log_a = -jnp.exp(alog_ref[...]) * _softplus(gates + dtb_ref[...])
    beta_s[...] = _sigmoid(gates)

    ci = _iota((CHUNK, CHUNK), 0)
    cj = _iota((CHUNK, CHUNK), 1)
    eye = (ci == cj).astype(F32)
    blk_mask = (ci // 16) == (cj // 16)
    low_half = _iota((CHUNK, 2 * HEAD_DIM), 1) < HEAD_DIM

    tri_f = (cj <= ci).astype(F32)
    tri_b = (cj >= ci).astype(F32)
    ones = jnp.ones((CHUNK, CHUNK), F32)
    fwd_lane = _iota((CHUNK, LANES), 1) < N_HEADS
    gate_rows = (_iota((N_GATE, LANES), 0) == _iota((N_GATE, LANES), 1)).astype(F32)
    for c in range(n_chunks):
        rows = slice(c * CHUNK, (c + 1) * CHUNK)
        g = log_a[rows, :]
        gc = jnp.where(fwd_lane, _mm(tri_f, g, HI), _mm(tri_b, g, HI))
        gt = _mm(ones, g, HI)
        gc_s[rows, :] = gc
        eg_s[rows, :] = jnp.exp(gc)
        gcrow_s[c] = _mm_nt(gate_rows, gc, HI)
        ekdrow_s[c] = jnp.exp(_mm_nt(gate_rows, gt - gc, HI))
        cdec = jnp.exp(_mm_nt(gate_rows, gt, HI))
        cdec_s[c] = jnp.concatenate([cdec, cdec], axis=1)

    def solve_group(grp, carry):
        row0 = grp * (GDN_GROUP_CHUNKS * CHUNK)
        chains = [(cl, a_idx) for cl in range(GDN_GROUP_CHUNKS) for a_idx in range(N_GATE)]

        def rows_of(cl):
            return pl.ds(pl.multiple_of(row0 + cl * CHUNK, CHUNK), CHUNK)

        for cl in range(GDN_GROUP_CHUNKS):
            for h in range(N_HEADS):
                rows = rows_of(cl)
                kq = jnp.concatenate([kv_s[h, rows, :HEAD_DIM], q_s[h, rows, :]], axis=0)
                k_t_h = kt_s[grp * GDN_GROUP_CHUNKS + cl, h * HEAD_DIM:(h + 1) * HEAD_DIM, :]
                kk_s[cl * N_HEADS + h] = _bmm(kq, k_t_h)
        for b, (cl, a_idx) in enumerate(chains):
            backward = a_idx >= N_HEADS
            h = a_idx % N_HEADS
            rows, c = rows_of(cl), grp * GDN_GROUP_CHUNKS + cl
            incl = (cj >= ci) if backward else (cj <= ci)
            strict = (cj > ci) if backward else (cj < ci)
            bt = beta_s[rows, N_GATE + a_idx:N_GATE + a_idx + 1]
            decay = jnp.exp(jnp.where(incl, gc_s[rows, a_idx:a_idx + 1] - gcrow_s[c, a_idx:a_idx + 1, :], NEG_BIG))
            low = jnp.where(strict, kk_s[cl * N_HEADS + h, :CHUNK, :] * bt * decay, 0.0)
            attn_s[a_idx, rows, :] = (kk_s[cl * N_HEADS + h, CHUNK:, :] * decay).astype(BF16)
            d = jnp.where(blk_mask, low, 0.0)
            d_s[b] = d
            off_s[b] = low - d
            rhs_s[b] = kv_s[h, rows, :] * bt * jnp.where(low_half, eg_s[rows, a_idx:a_idx + 1], 1.0)
        for b in range(GDN_CHAINS):
            m1_s[b] = _bmm(d_s[b], d_s[b])
        for b in range(GDN_CHAINS):
            d, d2 = d_s[b], m1_s[b]
            m2_s[b] = _bmm(d2, d2)
            p_s[b] = eye - d + d2 - _bmm(d, d2)
        for b in range(GDN_CHAINS):
            d4, p = m2_s[b], p_s[b]
            m3_s[b] = _bmm(d4, d4)
            p_s[b] = p + _bmm(p, d4)
        for b in range(GDN_CHAINS):
            p = p_s[b]
            p_s[b] = p + _bmm(p, m3_s[b])
        for b in range(GDN_CHAINS):
            p = p_s[b]
            m1_s[b] = _bmm(p, off_s[b])
            px_s[b] = _bmm(p, rhs_s[b])
        for b in range(GDN_CHAINS):
            m2_s[b] = _bmm(m1_s[b], m1_s[b])
        for b in range(GDN_CHAINS):
            n, n2 = m1_s[b], m2_s[b]
            m3_s[b] = eye - n + n2 - _bmm(n, n2)
        for b, (cl, a_idx) in enumerate(chains):
            h = a_idx % N_HEADS
            rows, c = rows_of(cl), grp * GDN_GROUP_CHUNKS + cl
            wu = _bmm(m3_s[b], px_s[b])
            uo_s[a_idx, rows, :] = wu
            wq_s[a_idx, c, :CHUNK, :] = wu[:, :HEAD_DIM].astype(BF16)
            wq_s[a_idx, c, CHUNK:, :] = (q_s[h, rows, :] * eg_s[rows, a_idx:a_idx + 1]).astype(BF16)
            k_t_h = kt_s[c, h * HEAD_DIM:(h + 1) * HEAD_DIM, :]
            kdt_s[a_idx, c] = (k_t_h * ekdrow_s[c, a_idx:a_idx + 1, :]).astype(BF16)
        return carry

    lax.fori_loop(0, n_chunks // GDN_GROUP_CHUNKS, solve_group, 0)

    for i in range(N_GATE):
        st_s[i] = jnp.concatenate([jnp.zeros((HEAD_DIM, HEAD_DIM), F32), s0_ref[0, i]], axis=1)

    def scan_chunk(c, carry):
        def chunk_of(a_idx):
            return (n_chunks - 1 - c) if a_idx >= N_HEADS else c

        for a_idx in range(N_GATE):
            wsqs_s[a_idx] = _mm(wq_s[a_idx, chunk_of(a_idx)], st_s[a_idx].astype(BF16))
        for a_idx in range(N_GATE):
            cc = chunk_of(a_idx)
            rows = pl.ds(pl.multiple_of(cc * CHUNK, CHUNK), CHUNK)
            v_new = (uo_s[a_idx, rows, :] - wsqs_s[a_idx, :CHUNK, :]).astype(BF16)
            uo_s[a_idx, rows, :] = wsqs_s[a_idx, CHUNK:, :] + _mm(attn_s[a_idx, rows, :], v_new)
            st_s[a_idx] = st_s[a_idx] * cdec_s[cc, a_idx:a_idx + 1, :] + _mm(kdt_s[a_idx, cc], v_new)
        return carry

    lax.fori_loop(0, n_chunks, scan_chunk, 0)
    for i in range(N_GATE):
        sfin_ref[0, i] = st_s[i, :, HEAD_DIM:]

    o = jnp.concatenate([(uo_s[h] + uo_s[N_HEADS + h])[:, HEAD_DIM:] for h in range(N_HEADS)], axis=1)
    ms = _mm(o * o, head_sum, HI) * (1.0 / HEAD_DIM)
    o_ref[...] = o * lax.rsqrt(ms + EPS) * nw_ref[...] * _silu(a[:, 3 * W_GROUP:])


def _gdn(a_all, gate_all, conv_w, a_log, dt_bias, norm_w, s0, t, n_seq, first_block, into=None):
    small = lambda: pl.BlockSpec((1, LANES), lambda i: (0, 0))
    n_chunks = t // CHUNK
    wide = 2 * HEAD_DIM
    scratch = [pltpu.VMEM((N_HEADS, t, HEAD_DIM), F32),
               pltpu.VMEM((N_HEADS, t, wide), F32),
               pltpu.VMEM((n_chunks, W_GROUP, CHUNK), F32),
               pltpu.VMEM((t, LANES), F32), pltpu.VMEM((t, LANES), F32), pltpu.VMEM((t, LANES), F32),
               pltpu.VMEM((n_chunks, N_GATE, CHUNK), F32), pltpu.VMEM((n_chunks, N_GATE, CHUNK), F32),
               pltpu.VMEM((n_chunks, N_GATE, wide), F32),
               pltpu.VMEM((N_GATE, t, wide), F32),
               pltpu.VMEM((N_GATE, n_chunks, 2 * CHUNK, HEAD_DIM), BF16),
               pltpu.VMEM((N_GATE, t, CHUNK), BF16),
               pltpu.VMEM((N_GATE, n_chunks, HEAD_DIM, CHUNK), BF16),
               pltpu.VMEM((N_GATE, HEAD_DIM, wide), F32),
               pltpu.VMEM((N_GATE, 2 * CHUNK, wide), F32),
               pltpu.VMEM((GDN_PAIRS, 2 * CHUNK, CHUNK), F32)]
    scratch += [pltpu.VMEM((GDN_CHAINS, CHUNK, CHUNK), F32)] * 6
    scratch += [pltpu.VMEM((GDN_CHAINS, CHUNK, wide), F32)] * 2
    return _seq_call(
        functools.partial(_gdn_kernel, t=t), (a_all, gate_all, conv_w, a_log, dt_bias, norm_w, s0),
        in_specs=[pl.BlockSpec((t, 4 * W_GROUP), lambda i: (i + first_block, 0)),
                  pl.BlockSpec((t, LANES), lambda i: (i + first_block, 0)),
                  pl.BlockSpec((SUBLANES, 3 * W_GROUP), lambda i: (0, 0)),
                  small(), small(),
                  pl.BlockSpec((1, W_GROUP), lambda i: (0, 0)),
                  pl.BlockSpec((1, N_GATE, HEAD_DIM, HEAD_DIM), lambda i: (i, 0, 0, 0))],
        out_specs=[pl.BlockSpec((t, W_GROUP), lambda i: (i + first_block, 0)),
                   pl.BlockSpec((1, N_GATE, HEAD_DIM, HEAD_DIM), lambda i: (i, 0, 0, 0))],
        out_shape=[jax.ShapeDtypeStruct((N_TOK, W_GROUP), F32),
                   jax.ShapeDtypeStruct((n_seq, N_GATE, HEAD_DIM, HEAD_DIM), F32)],
        scratch_shapes=scratch,
        into=into, grid=(n_seq,), compiler_params=_cparams("arbitrary"), name="gdn")


def _swap16(x):
    width = x.shape[-1]
    first = (_iota(x.shape, 1) // 16) % 2 == 0
    return jnp.where(first, pltpu.roll(x, width - 16, 1), pltpu.roll(x, 16, 1))


def _block_diag_heads(s0_ref, first):
    zero = jnp.zeros((HEAD_DIM, HEAD_DIM), F32)
    return jnp.concatenate(
        [jnp.concatenate([s0_ref[0, first + h] if j == h else zero for j in range(N_HEADS)], axis=1)
         for h in range(N_HEADS)], axis=0)


def _ret_kernel(r_ref, lg_ref, s0_ref, cos_ref, sin_ref, o_ref, sfin_ref, *, t, latent):
    r = r_ref[...]
    q = r[:, :W_GROUP]
    k = r[:, W_GROUP:2 * W_GROUP]
    v = r[:, 2 * W_GROUP:3 * W_GROUP]
    if latent:
        q = q * cos_ref[...] + _swap16(q) * sin_ref[...]
        k = k * cos_ref[...] + _swap16(k) * sin_ref[...]
    k = k * (HEAD_DIM ** -0.5)
    lg = -_softplus(-lg_ref[...])
    lgf, lgb = lg[0:1, :], lg[1:2, :]
    head = _iota((1, W_GROUP), 1) // HEAD_DIM
    head_sum = _head_sum_matrix()
    pos = _iota((t, 1), 0).astype(F32)
    q_b = q.astype(BF16)
    kt_b = k.T.astype(BF16)
    v_heads = [jnp.where(head == h, v, 0.0).astype(BF16) for h in range(N_HEADS)]
    if latent:
        s0f = _block_diag_heads(s0_ref, 0)
        s0b = _block_diag_heads(s0_ref, N_HEADS)
    for qt in range(t // Q_TILE):
        rows = slice(qt * Q_TILE, (qt + 1) * Q_TILE)
        diff = (_iota((Q_TILE, t), 0) + qt * Q_TILE - _iota((Q_TILE, t), 1)).astype(F32)
        o = jnp.zeros((Q_TILE, W_GROUP), F32)
        for h in range(N_HEADS):
            lgf_h = lgf[:, h * HEAD_DIM:h * HEAD_DIM + 1]
            lgb_h = lgb[:, h * HEAD_DIM:h * HEAD_DIM + 1]
            dmat = (jnp.exp(jnp.where(diff >= 0, diff * lgf_h, NEG_BIG))
                    + jnp.exp(jnp.where(diff <= 0, -diff * lgb_h, NEG_BIG)))
            s = _mm(jnp.where(head == h, q_b[rows], 0.0), kt_b) * dmat
            o = o + _mm(s.astype(BF16), v_heads[h])
        if latent:
            p = pos[rows]
            o = o + jnp.exp((p + 1.0) * lgf) * _bmm(q_b[rows], s0f) + jnp.exp((t - p) * lgb) * _bmm(q_b[rows], s0b)
        oc = o - _mm(o, head_sum, HI) * (1.0 / HEAD_DIM)
        on = oc * lax.rsqrt(_mm(oc * oc, head_sum, HI) * (1.0 / HEAD_DIM) + EPS)
        o_ref[rows, :] = on * _silu(r[rows, 3 * W_GROUP:])
    v_b = v.astype(BF16)
    sf = _mm_tn((k * jnp.exp((t - 1.0 - pos) * lgf)).astype(BF16), v_b)
    sb = _mm_tn((k * jnp.exp(pos * lgb)).astype(BF16), v_b)
    for h in range(N_HEADS):
        hs = slice(h * HEAD_DIM, (h + 1) * HEAD_DIM)
        sf_h, sb_h = sf[hs, hs], sb[hs, hs]
        if latent:
            sf_h = sf_h + jnp.exp(t * lgf[:, h * HEAD_DIM:h * HEAD_DIM + 1]) * s0_ref[0, h]
            sb_h = sb_h + jnp.exp(t * lgb[:, h * HEAD_DIM:h * HEAD_DIM + 1]) * s0_ref[0, N_HEADS + h]
        sfin_ref[0, h] = sf_h
        sfin_ref[0, N_HEADS + h] = sb_h


def _ret(r_all, logit, s0, cos, sin, t, n_seq, first_block, latent, into=None):
    return _seq_call(
        functools.partial(_ret_kernel, t=t, latent=latent), (r_all, logit, s0, cos, sin),
        in_specs=[pl.BlockSpec((t, 4 * W_GROUP), lambda i: (i + first_block, 0)),
                  pl.BlockSpec((2, W_GROUP), lambda i: (0, 0)),
                  pl.BlockSpec((1, N_GATE, HEAD_DIM, HEAD_DIM), lambda i: (i, 0, 0, 0)),
                  pl.BlockSpec((t, W_GROUP), lambda i: (0, 0)),
                  pl.BlockSpec((t, W_GROUP), lambda i: (0, 0))],
        out_specs=[pl.BlockSpec((t, W_GROUP), lambda i: (i + first_block, 0)),
                   pl.BlockSpec((1, N_GATE, HEAD_DIM, HEAD_DIM), lambda i: (i, 0, 0, 0))],
        out_shape=[jax.ShapeDtypeStruct((N_TOK, W_GROUP), F32),
                   jax.ShapeDtypeStruct((n_seq, N_GATE, HEAD_DIM, HEAD_DIM), F32)],
        into=into, grid=(n_seq,), compiler_params=_cparams("arbitrary"), name="ret")


def _rope_tables(t):
    pos = np.arange(t)
    row = (pos // GRID_W).astype(np.float32)
    col = (pos % GRID_W).astype(np.float32)
    nf = HEAD_DIM // 4
    inv_freq = jnp.power(ROPE_BASE, -jnp.arange(nf, dtype=F32) / nf)
    ang_r = jnp.asarray(row)[:, None] * inv_freq[None, :]
    ang_c = jnp.asarray(col)[:, None] * inv_freq[None, :]
    cos = jnp.concatenate([jnp.cos(ang_r)] * 2 + [jnp.cos(ang_c)] * 2, axis=1)
    sin = jnp.concatenate([-jnp.sin(ang_r), jnp.sin(ang_r), -jnp.sin(ang_c), jnp.sin(ang_c)], axis=1)
    return jnp.tile(cos, (1, N_HEADS)), jnp.tile(sin, (1, N_HEADS))


def _head_rms(x, w, head_sum):
    return x * lax.rsqrt(_mm(x * x, head_sum, HI) * (1.0 / HEAD_DIM) + EPS) * w


def _ctx_attn_kernel(n_ref, qw_ref, kw_ref, o_ref, k_out, v_out):
    n = n_ref[...]
    head_sum = _head_sum_matrix()
    head = _iota((1, W_GROUP), 1) // HEAD_DIM
    q = (_head_rms(n[:, :W_GROUP], qw_ref[...], head_sum) * (HEAD_DIM ** -0.5)).astype(BF16)
    k = _head_rms(n[:, W_GROUP:2 * W_GROUP], kw_ref[...], head_sum)
    v = n[:, 2 * W_GROUP:]
    k_out[0] = k
    v_out[0] = v
    kt_b = k.T.astype(BF16)
    o = jnp.zeros((SEQ, W_GROUP), F32)
    for h in range(N_HEADS):
        s = _mm(jnp.where(head == h, q, 0.0), kt_b)
        p = jnp.exp(s - jnp.max(s, axis=-1, keepdims=True))
        o = o + _mm(p.astype(BF16), jnp.where(head == h, v, 0.0).astype(BF16)) / jnp.sum(p, axis=-1, keepdims=True)
    o_ref[...] = o


def _ctx_attn(n_all, qw, kw, into):
    return _seq_call(
        _ctx_attn_kernel, (n_all, qw, kw),
        in_specs=[pl.BlockSpec((SEQ, 3 * W_GROUP), lambda i: (i, 0)),
                  pl.BlockSpec((1, W_GROUP), lambda i: (0, 0)),
                  pl.BlockSpec((1, W_GROUP), lambda i: (0, 0))],
        out_specs=[pl.BlockSpec((SEQ, W_GROUP), lambda i: (i, 0)),
                   pl.BlockSpec((1, SEQ, W_GROUP), lambda i: (i, 0, 0)),
                   pl.BlockSpec((1, SEQ, W_GROUP), lambda i: (i, 0, 0))],
        out_shape=[jax.ShapeDtypeStruct((N_TOK, W_GROUP), F32),
                   jax.ShapeDtypeStruct((BATCH, SEQ, W_GROUP), F32),
                   jax.ShapeDtypeStruct((BATCH, SEQ, W_GROUP), F32)],
        into=into, grid=(BATCH,), compiler_params=_cparams("arbitrary"), name="ctx_attn")


def _nat_kernel(n_ref, ck_ref, cv_ref, bias_ref, qw_ref, kw_ref, o_ref, q_s, kt_s, ckt_s):
    h = pl.program_id(1)

    @pl.when(h == 0)
    def _():
        head_sum = _head_sum_matrix()
        n = n_ref[...]
        q_s[...] = (_head_rms(n[:, :W_GROUP], qw_ref[...], head_sum) * (HEAD_DIM ** -0.5)).astype(BF16)
        kt_s[...] = _head_rms(n[:, W_GROUP:2 * W_GROUP], kw_ref[...], head_sum).T.astype(BF16)
        ckt_s[...] = ck_ref[0].T.astype(BF16)
        o_ref[...] = jnp.zeros_like(o_ref)

    mine = _iota((1, W_GROUP), 1) // HEAD_DIM == h
    v = jnp.where(mine, n_ref[:, 2 * W_GROUP:], 0.0).astype(BF16)
    cv = jnp.where(mine, cv_ref[0], 0.0).astype(BF16)
    for qt in range(DEC_SEQ // Q_TILE):
        rows = slice(qt * Q_TILE, (qt + 1) * Q_TILE)
        q = jnp.where(mine, q_s[rows, :], 0.0)
        s_loc = _mm(q, kt_s[...]) + bias_ref[0, rows, :]
        s_ctx = _mm(q, ckt_s[...])
        m = jnp.maximum(jnp.max(s_loc, axis=-1, keepdims=True), jnp.max(s_ctx, axis=-1, keepdims=True))
        p_loc = jnp.exp(s_loc - m)
        p_ctx = jnp.exp(s_ctx - m)
        den = jnp.sum(p_loc, axis=-1, keepdims=True) + jnp.sum(p_ctx, axis=-1, keepdims=True)
        o_ref[rows, :] += (_mm(p_loc.astype(BF16), v) + _mm(p_ctx.astype(BF16), cv)) / den


def _nat(n_all, ck, cv, bias, qw, kw, into):
    first_block = N_CTX // DEC_SEQ
    return _seq_call(
        _nat_kernel, (n_all, ck, cv, bias, qw, kw),
        in_specs=[pl.BlockSpec((DEC_SEQ, 3 * W_GROUP), lambda b, h: (b + first_block, 0)),
                  pl.BlockSpec((1, PAST_LEN, W_GROUP), lambda b, h: (b, 0, 0)),
                  pl.BlockSpec((1, PAST_LEN, W_GROUP), lambda b, h: (b, 0, 0)),
                  pl.BlockSpec((1, DEC_SEQ, DEC_SEQ), lambda b, h: (h, 0, 0)),
                  pl.BlockSpec((1, W_GROUP), lambda b, h: (0, 0)),
                  pl.BlockSpec((1, W_GROUP), lambda b, h: (0, 0))],
        out_specs=pl.BlockSpec((DEC_SEQ, W_GROUP), lambda b, h: (b + first_block, 0)),
        out_shape=jax.ShapeDtypeStruct((N_TOK, W_GROUP), F32),
        scratch_shapes=[pltpu.VMEM((DEC_SEQ, W_GROUP), BF16), pltpu.VMEM((W_GROUP, DEC_SEQ), BF16),
                        pltpu.VMEM((W_GROUP, PAST_LEN), BF16)],
        into=into, grid=(DEC_BATCH, N_HEADS), compiler_params=_cparams("arbitrary", "arbitrary"), name="nat")


def _nat_bias(rpb):
    rows_n = DEC_SEQ // GRID_W
    kh = min(WIN_ROWS, rows_n)
    r = np.arange(rows_n)
    c = np.arange(GRID_W)
    r0 = np.clip(r - kh // 2, 0, rows_n - kh)
    c0 = np.clip(c - WIN_COLS // 2, 0, GRID_W - WIN_COLS)
    col_in = (c[None, :] >= c0[:, None]) & (c[None, :] < c0[:, None] + WIN_COLS)
    col_idx = np.clip(c[None, :] - c[:, None], -(WIN_COLS - 1), WIN_COLS - 1) + WIN_COLS - 1
    col_hot = (col_idx[..., None] == np.arange(2 * WIN_COLS - 1)).astype(np.float32)
    table = jnp.where(col_in, jnp.einsum('hab,qkb->haqk', rpb.astype(F32), col_hot, precision=HI), NEG_BIG)
    per_row = []
    for row in range(rows_n):
        first = r0[row] - row + WIN_ROWS - 1
        window = table[:, first:first + kh].transpose(0, 2, 1, 3)
        per_row.append(jnp.pad(window, ((0, 0), (0, 0), (r0[row], rows_n - kh - r0[row]), (0, 0)),
                               constant_values=NEG_BIG))
    return jnp.stack(per_row, axis=1).reshape(N_HEADS, DEC_SEQ, DEC_SEQ)


GSEL_LANE = N_EXPERTS


def _pack_router(we, be, wg, bg):
    pad = LANES - N_EXPERTS - N_GROUPS
    w = jnp.concatenate([we, wg, jnp.zeros((D_MODEL, pad), F32)], axis=1)
    b = jnp.concatenate([be, bg, jnp.zeros((pad,), F32)]).reshape(1, LANES)
    hi = w.astype(BF16)
    lo = (w - hi.astype(F32)).astype(BF16)
    return jnp.concatenate([hi, lo], axis=1), b


def _lane_min_where(mask, lane):
    return jnp.min(jnp.where(mask, lane, LANES), axis=-1, keepdims=True)


def _outproj_kernel(x_ref, m0, m1, m2, m3, mod_ref, nw_ref, w_ref, rw_ref, rb_ref, x_out, hf_out, route_out):
    acc = None
    for i, m_ref in enumerate((m0, m1, m2, m3)):
        part = _mm(m_ref[...].astype(BF16), w_ref[i * W_GROUP:(i + 1) * W_GROUP, :])
        acc = part if acc is None else acc + part
    x = x_ref[...] + mod_ref[0, 2:3, :] * acc
    x_out[...] = x
    y = x * lax.rsqrt(jnp.mean(x * x, axis=-1, keepdims=True) + EPS) * nw_ref[...]
    hf = y * (1.0 + mod_ref[0, 4:5, :]) + mod_ref[0, 3:4, :]
    hf_hi = hf.astype(BF16)
    hf_out[...] = hf_hi

    hf_lo = (hf - hf_hi.astype(F32)).astype(BF16)
    both = _mm(hf_hi, rw_ref[...])
    logits = both[:, :LANES] + both[:, LANES:] + _mm(hf_lo, rw_ref[:, :LANES]) + rb_ref[...]
    lane = _iota(logits.shape, 1)
    is_g = (lane >= N_EXPERTS) & (lane < N_EXPERTS + N_GROUPS)
    gl = jnp.where(is_g, logits, NEG_BIG)
    ge = jnp.exp(gl - jnp.max(gl, axis=-1, keepdims=True))
    gp = jnp.where(is_g, ge / jnp.sum(ge, axis=-1, keepdims=True), -1.0)
    gw = jnp.max(gp, axis=-1, keepdims=True)
    gsel = _lane_min_where(gp == gw, lane) - N_EXPERTS
    in_grp = (lane // EXPERTS_PER_GROUP == gsel) & (lane < N_EXPERTS)
    el = jnp.where(in_grp, logits, NEG_BIG)
    ee = jnp.exp(el - jnp.max(el, axis=-1, keepdims=True))
    ep = jnp.where(in_grp, ee / jnp.sum(ee, axis=-1, keepdims=True), -1.0)
    t1 = jnp.max(ep, axis=-1, keepdims=True)
    i1 = _lane_min_where(ep == t1, lane)
    ep2 = jnp.where(lane == i1, -1.0, ep)
    t2 = jnp.max(ep2, axis=-1, keepdims=True)
    i2 = _lane_min_where(ep2 == t2, lane)
    tsum = t1 + t2
    combine = jnp.where(lane == i1, gw * (t1 / tsum), 0.0) + jnp.where(lane == i2, gw * (t2 / tsum), 0.0)
    route_out[...] = jnp.where(lane == GSEL_LANE, gsel.astype(F32), combine)


def _outproj(x, mixed, mod, norm_w, w_out, rw, rb):
    tile = lambda w: pl.BlockSpec((TOK_TILE, w), lambda i: (i, 0))
    whole = lambda a: pl.BlockSpec(a.shape, lambda i: (0,) * a.ndim)
    return pl.pallas_call(
        _outproj_kernel,
        grid=(N_TOK // TOK_TILE,),
        in_specs=[tile(D_MODEL)] + [tile(W_GROUP)] * 4
                 + [pl.BlockSpec((1, SUBLANES, D_MODEL), lambda i: (_cond_of_tile(i), 0, 0)),
                    pl.BlockSpec((1, D_MODEL), lambda i: (0, 0)), whole(w_out), whole(rw), whole(rb)],
        out_specs=[tile(D_MODEL), tile(D_MODEL), tile(LANES)],
        out_shape=[jax.ShapeDtypeStruct((N_TOK, D_MODEL), F32), jax.ShapeDtypeStruct((N_TOK, D_MODEL), BF16),
                   jax.ShapeDtypeStruct((N_TOK, LANES), F32)],
        compiler_params=_cparams("arbitrary"),
        name="outproj",
    )(x, *mixed, mod, norm_w.reshape(1, D_MODEL), w_out, rw, rb)


SEG_BLK = 32
LOCAL_ROWS = TOK_TILE + N_GROUPS * SEG_BLK
N_TOK_TILES = N_TOK // TOK_TILE
MOE_ROWS = -(-(N_TOK + N_TOK_TILES * N_GROUPS * (SEG_BLK - 1) + N_GROUPS * (MOE_TILE - 1)) // MOE_TILE) * MOE_TILE


def _moe_tables(gsel):
    groups = jnp.arange(N_GROUPS, dtype=jnp.int32)
    onehot = (gsel.reshape(N_TOK_TILES, TOK_TILE, 1) == groups).astype(jnp.int32)
    rank = jnp.cumsum(onehot, axis=1) - onehot
    nblk = (jnp.sum(onehot, axis=1) + SEG_BLK - 1) // SEG_BLK
    loc_blk = jnp.cumsum(nblk, axis=1) - nblk
    blocks_per_tile = MOE_TILE // SEG_BLK
    grp_tiles = (jnp.sum(nblk, axis=0) + blocks_per_tile - 1) // blocks_per_tile
    grp_tile_start = jnp.cumsum(grp_tiles) - grp_tiles
    dst_blk = grp_tile_start[None, :] * blocks_per_tile + jnp.cumsum(nblk, axis=0) - nblk
    local_pos = jnp.sum(onehot * (loc_blk[:, None, :] * SEG_BLK + rank), axis=2)
    tile_idx = jnp.arange(MOE_ROWS // MOE_TILE, dtype=jnp.int32)
    tile_group = jnp.clip(jnp.sum(tile_idx[:, None] >= grp_tile_start[None, :], axis=1) - 1, 0, N_GROUPS - 1)
    tile_valid = tile_idx < jnp.sum(grp_tiles)
    flat = lambda a: a.reshape(-1).astype(jnp.int32)
    return local_pos.astype(jnp.int32), flat(nblk), flat(loc_blk), flat(dst_blk), flat(tile_group), flat(tile_valid)


def _segment_copies(t, nblk, loc_blk, dst_blk, make_copies, action):
    for g in range(N_GROUPS):
        k = t * N_GROUPS + g

        @pl.loop(0, nblk[k])
        def _(b):
            local = pl.multiple_of((loc_blk[k] + b) * SEG_BLK, SEG_BLK)
            sorted_row = pl.multiple_of((dst_blk[k] + b) * SEG_BLK, SEG_BLK)
            for cp in make_copies(local, sorted_row):
                action(cp)


def _dispatch_kernel(nblk, loc_blk, dst_blk, hf_ref, rt_ref, lp_ref, xs_in, rs_in, xs_hbm, rs_hbm, xbuf, rbuf, sem):
    t = pl.program_id(0)
    slot = t % 2
    onehot = _iota((LOCAL_ROWS, TOK_TILE), 0) == lp_ref[0]
    xbuf[slot] = _mm(onehot.astype(BF16), hf_ref[...]).astype(BF16)
    rbuf[slot] = _mm(onehot.astype(F32), rt_ref[...], HI)

    def copies_of(s):
        def copies(local, sorted_row):
            return (pltpu.make_async_copy(xbuf.at[s, pl.ds(local, SEG_BLK)], xs_hbm.at[pl.ds(sorted_row, SEG_BLK)], sem.at[s]),
                    pltpu.make_async_copy(rbuf.at[s, pl.ds(local, SEG_BLK)], rs_hbm.at[pl.ds(sorted_row, SEG_BLK)], sem.at[s]))
        return copies

    @pl.when(t > 0)
    def _():
        _segment_copies(t - 1, nblk, loc_blk, dst_blk, copies_of(1 - slot), lambda cp: cp.wait())

    _segment_copies(t, nblk, loc_blk, dst_blk, copies_of(slot), lambda cp: cp.start())

    @pl.when(t == N_TOK_TILES - 1)
    def _():
        _segment_copies(t, nblk, loc_blk, dst_blk, copies_of(slot), lambda cp: cp.wait())


def _dispatch(hf, route, local_pos, nblk, loc_blk, dst_blk):
    grid_spec = pltpu.PrefetchScalarGridSpec(
        num_scalar_prefetch=3,
        grid=(N_TOK_TILES,),
        in_specs=[pl.BlockSpec((TOK_TILE, D_MODEL), lambda t, *_: (t, 0)),
                  pl.BlockSpec((TOK_TILE, LANES), lambda t, *_: (t, 0)),
                  pl.BlockSpec((1, 1, TOK_TILE), lambda t, *_: (t, 0, 0)),
                  pl.BlockSpec(memory_space=pl.ANY), pl.BlockSpec(memory_space=pl.ANY)],
        out_specs=[pl.BlockSpec(memory_space=pl.ANY), pl.BlockSpec(memory_space=pl.ANY)],
        scratch_shapes=[pltpu.VMEM((2, LOCAL_ROWS, D_MODEL), BF16), pltpu.VMEM((2, LOCAL_ROWS, LANES), F32),
                        pltpu.SemaphoreType.DMA((2,))],
    )
    return pl.pallas_call(
        _dispatch_kernel,
        grid_spec=grid_spec,
        out_shape=[jax.ShapeDtypeStruct((MOE_ROWS, D_MODEL), BF16), jax.ShapeDtypeStruct((MOE_ROWS, LANES), F32)],
        input_output_aliases={6: 0, 7: 1},
        compiler_params=_cparams("arbitrary"),
        name="dispatch",
    )(nblk, loc_blk, dst_blk, hf, route, local_pos.reshape(N_TOK_TILES, 1, TOK_TILE),
      jnp.zeros((MOE_ROWS, D_MODEL), BF16), jnp.zeros((MOE_ROWS, LANES), F32))


def _moe_kernel(tile_group, tile_valid, x_ref, r_ref, wg_hbm, wu_hbm, wd_hbm, y_ref,
                wg_b, wu_b, wd_b, stage_g, stage_u, stage_d, sem, *, layer):
    i = pl.program_id(0)
    g = tile_group[i]
    group_row = layer * N_GROUPS + g
    new_group = (i == 0) | (g != tile_group[jnp.maximum(i - 1, 0)])
    valid = tile_valid[i] > 0

    def weight_copies(e):
        slot = e % 2
        return (pltpu.make_async_copy(wg_hbm.at[group_row, e], stage_g.at[slot], sem.at[slot]),
                pltpu.make_async_copy(wu_hbm.at[group_row, e], stage_u.at[slot], sem.at[slot]),
                pltpu.make_async_copy(wd_hbm.at[group_row, e], stage_d.at[slot], sem.at[slot]))

    def run(load_weights):
        x = x_ref[...]
        route = r_ref[...]
        lane = _iota(route.shape, 1)
        acc = jnp.zeros((MOE_TILE, D_MODEL), F32)
        if load_weights:
            for e in range(2):
                for cp in weight_copies(e):
                    cp.start()
        for e in range(EXPERTS_PER_GROUP):
            if load_weights:
                for cp in weight_copies(e):
                    cp.wait()
                wg_b[e] = stage_g[e % 2].astype(BF16)
                wu_b[e] = stage_u[e % 2].astype(BF16)
                wd_b[e] = stage_d[e % 2].astype(BF16)
                if e + 2 < EXPERTS_PER_GROUP:
                    for cp in weight_copies(e + 2):
                        cp.start()
            cw = jnp.sum(jnp.where(lane == g * EXPERTS_PER_GROUP + e, route, 0.0), axis=-1, keepdims=True)
            act = _silu(_mm(x, wg_b[e])) * _mm(x, wu_b[e]) * cw
            acc = acc + _mm(act.astype(BF16), wd_b[e])
        y_ref[...] = acc

    @pl.when(valid & new_group)
    def _():
        run(True)

    @pl.when(valid & jnp.logical_not(new_group))
    def _():
        run(False)

    @pl.when(jnp.logical_not(valid))
    def _():
        y_ref[...] = jnp.zeros_like(y_ref)


def _moe(xs, rs, tile_group, tile_valid, wg, wu, wd, layer):
    any_spec = pl.BlockSpec(memory_space=pl.ANY)
    grid_spec = pltpu.PrefetchScalarGridSpec(
        num_scalar_prefetch=2,
        grid=(MOE_ROWS // MOE_TILE,),
        in_specs=[pl.BlockSpec((MOE_TILE, D_MODEL), lambda i, tg, tv: (i, 0)),
                  pl.BlockSpec((MOE_TILE, LANES), lambda i, tg, tv: (i, 0)),
                  any_spec, any_spec, any_spec],
        out_specs=pl.BlockSpec((MOE_TILE, D_MODEL), lambda i, tg, tv: (i, 0)),
        scratch_shapes=[pltpu.VMEM((EXPERTS_PER_GROUP, D_MODEL, EXPERT_FF), BF16),
                        pltpu.VMEM((EXPERTS_PER_GROUP, D_MODEL, EXPERT_FF), BF16),
                        pltpu.VMEM((EXPERTS_PER_GROUP, EXPERT_FF, D_MODEL), BF16),
                        pltpu.VMEM((2, D_MODEL, EXPERT_FF), F32), pltpu.VMEM((2, D_MODEL, EXPERT_FF), F32),
                        pltpu.VMEM((2, EXPERT_FF, D_MODEL), F32), pltpu.SemaphoreType.DMA((2,))],
    )
    return pl.pallas_call(
        functools.partial(_moe_kernel, layer=layer),
        grid_spec=grid_spec,
        out_shape=jax.ShapeDtypeStruct((MOE_ROWS, D_MODEL), F32),
        compiler_params=_cparams("arbitrary"),
        name="moe",
    )(tile_group, tile_valid, xs, rs, wg, wu, wd)


def _combine_kernel(nblk, loc_blk, dst_blk, x_ref, lp_ref, mod_ref, ys_hbm, o_ref, ybuf, sem):
    t = pl.program_id(0)
    slot = t % 2

    def fetch(tile, s):
        def copies(local, sorted_row):
            return (pltpu.make_async_copy(ys_hbm.at[pl.ds(sorted_row, SEG_BLK)], ybuf.at[s, pl.ds(local, SEG_BLK)], sem.at[s]),)
        ybuf[s] = jnp.zeros((LOCAL_ROWS, D_MODEL), F32)
        _segment_copies(tile, nblk, loc_blk, dst_blk, copies, lambda cp: cp.start())

    @pl.when(t == 0)
    def _():
        fetch(0, 0)

    @pl.when(t + 1 < N_TOK_TILES)
    def _():
        fetch(t + 1, 1 - slot)

    def copies_now(local, sorted_row):
        return (pltpu.make_async_copy(ys_hbm.at[pl.ds(sorted_row, SEG_BLK)], ybuf.at[slot, pl.ds(local, SEG_BLK)], sem.at[slot]),)

    _segment_copies(t, nblk, loc_blk, dst_blk, copies_now, lambda cp: cp.wait())

    onehot = (_iota((TOK_TILE, LOCAL_ROWS), 1) == lp_ref[...]).astype(BF16)
    ys = ybuf[slot]
    hi = ys.astype(BF16)
    lo = (ys - hi.astype(F32)).astype(BF16)
    y = _mm(onehot, hi) + _mm(onehot, lo)
    o_ref[...] = x_ref[...] + mod_ref[0, 5:6, :] * y


def _combine(x, ys, mod, local_pos, nblk, loc_blk, dst_blk):
    tile = pl.BlockSpec((TOK_TILE, D_MODEL), lambda t, *_: (t, 0))
    grid_spec = pltpu.PrefetchScalarGridSpec(
        num_scalar_prefetch=3,
        grid=(N_TOK_TILES,),
        in_specs=[tile, pl.BlockSpec((TOK_TILE, 1), lambda t, *_: (t, 0)),
                  pl.BlockSpec((1, SUBLANES, D_MODEL), lambda t, *_: (_cond_of_tile(t), 0, 0)),
                  pl.BlockSpec(memory_space=pl.ANY)],
        out_specs=tile,
        scratch_shapes=[pltpu.VMEM((2, LOCAL_ROWS, D_MODEL), F32), pltpu.SemaphoreType.DMA((2,))],
    )
    return pl.pallas_call(
        _combine_kernel,
        grid_spec=grid_spec,
        out_shape=jax.ShapeDtypeStruct((N_TOK, D_MODEL), F32),
        compiler_params=_cparams("arbitrary"),
        name="combine",
    )(nblk, loc_blk, dst_blk, x, local_pos.reshape(N_TOK, 1), mod, ys)


def _lane_row(v):
    v = v.reshape(-1).astype(F32)
    return jnp.concatenate([v, jnp.zeros((LANES - v.shape[0],), F32)]).reshape(1, LANES)


def _pad_rows(w):
    return jnp.concatenate([w, jnp.zeros((SUBLANES - w.shape[0], w.shape[1]), w.dtype)], axis=0)


def kernel(x_prompt, x_sample, state_gdn, state_ret, cache_nat_k, cache_nat_v, c, c_ctx, ada_w, ada_b, norm_mix_w, norm_ffn_w, w_in, gdn_conv_w, gdn_a_log, gdn_dt_bias, gdn_norm_w, ret_gamma_logit, nat_q_norm_w, nat_k_norm_w, nat_rpb, sc_conv_w, w_out, router_group_w, router_group_b, router_expert_w, router_expert_b, moe_w_gate, moe_w_up, moe_w_down):
    x = jnp.concatenate([x_prompt.reshape(N_CTX, D_MODEL), x_sample.reshape(N_LAT, D_MODEL)], axis=0)
    cond = jnp.concatenate([c_ctx[None, :], c, jnp.zeros((SUBLANES - 1 - DEC_BATCH, D_MODEL), F32)], axis=0)
    ada = _ada(cond, ada_w, ada_b).reshape(DEPTH, SUBLANES, 6, D_MODEL)
    cos, sin = _rope_tables(DEC_SEQ)
    zero_state = jnp.zeros((BATCH, N_GATE, HEAD_DIM, HEAD_DIM), F32)
    lat_block = N_CTX // DEC_SEQ
    gdn_list, ret_list, k_list, v_list = [], [], [], []
    mixed = [jnp.zeros((N_TOK, W_GROUP), F32) for _ in range(4)]
    for l in range(DEPTH):
        mod = jnp.concatenate([ada[l, :1 + DEC_BATCH], jnp.zeros((1 + DEC_BATCH, SUBLANES - 6, D_MODEL), F32)], axis=1)
        a_gdn, a_ret, a_nat, a_sc, a_gate = _inproj(x, mod, norm_mix_w[l], _pack_w_in(w_in[l]))

        conv_w = _pad_rows(gdn_conv_w[l])
        a_log, dt_b = _lane_row(gdn_a_log[l]), _lane_row(gdn_dt_bias[l])
        gnw = jnp.tile(gdn_norm_w[l], N_HEADS).reshape(1, W_GROUP)
        o_gdn, s_gdn = _gdn(a_gdn, a_gate, conv_w, a_log, dt_b, gnw, zero_state, SEQ, BATCH, 0, into=mixed[0])
        s0 = state_gdn[:, l].reshape(DEC_BATCH, N_GATE, HEAD_DIM, HEAD_DIM)
        o_gdn, _ = _gdn(a_gdn, a_gate, conv_w, a_log, dt_b, gnw, s0, DEC_SEQ, DEC_BATCH, lat_block, into=o_gdn)

        logit = jnp.repeat(ret_gamma_logit[l].astype(F32), HEAD_DIM, axis=1)
        o_ret, s_ret = _ret(a_ret, logit, zero_state, cos[:SEQ], sin[:SEQ], SEQ, BATCH, 0, False, into=mixed[1])
        s0 = state_ret[:, l].reshape(DEC_BATCH, N_GATE, HEAD_DIM, HEAD_DIM)
        o_ret, _ = _ret(a_ret, logit, s0, cos, sin, DEC_SEQ, DEC_BATCH, lat_block, True, into=o_ret)

        qw = jnp.tile(nat_q_norm_w[l], N_HEADS).reshape(1, W_GROUP)
        kw = jnp.tile(nat_k_norm_w[l], N_HEADS).reshape(1, W_GROUP)
        o_nat, n_k, n_v = _ctx_attn(a_nat, qw, kw, mixed[2])
        o_nat = _nat(a_nat, cache_nat_k[:, l].reshape(DEC_BATCH, PAST_LEN, W_GROUP),
                     cache_nat_v[:, l].reshape(DEC_BATCH, PAST_LEN, W_GROUP), _nat_bias(nat_rpb[l]), qw, kw, o_nat)

        sc_w = _pad_rows(sc_conv_w[l])
        o_sc = _sconv(a_sc, sc_w, SEQ, BATCH, 0, into=mixed[3])
        o_sc = _sconv(a_sc, sc_w, DEC_SEQ, DEC_BATCH, lat_block, into=o_sc)

        mixed = [o_gdn, o_ret, o_nat, o_sc]
        rw, rb = _pack_router(router_expert_w[l], router_expert_b[l], router_group_w[l], router_group_b[l])
        x_mid, hf, route = _outproj(x, mixed, mod, norm_ffn_w[l], w_out[l].astype(BF16), rw, rb)

        local_pos, nblk, loc_blk, dst_blk, tile_group, tile_valid = _moe_tables(route[:, GSEL_LANE].astype(jnp.int32))
        xs, rs = _dispatch(hf, route, local_pos, nblk, loc_blk, dst_blk)
        to_group = lambda w: w.reshape((DEPTH * N_GROUPS, EXPERTS_PER_GROUP) + w.shape[2:])
        ys = _moe(xs, rs, tile_group, tile_valid, to_group(moe_w_gate), to_group(moe_w_up), to_group(moe_w_down), l)
        x = _combine(x_mid, ys, mod, local_pos, nblk, loc_blk, dst_blk)

        gdn_list.append(s_gdn.reshape(BATCH, 2, N_HEADS, HEAD_DIM, HEAD_DIM))
        ret_list.append(s_ret.reshape(BATCH, 2, N_HEADS, HEAD_DIM, HEAD_DIM))
        k_list.append(n_k.reshape(BATCH, SEQ, N_HEADS, HEAD_DIM))
        v_list.append(n_v.reshape(BATCH, SEQ, N_HEADS, HEAD_DIM))
    return (x[:N_CTX].reshape(BATCH, SEQ, D_MODEL), x[N_CTX:].reshape(DEC_BATCH, DEC_SEQ, D_MODEL),
            jnp.stack(gdn_list, axis=1), jnp.stack(ret_list, axis=1),
            jnp.stack(k_list, axis=1), jnp.stack(v_list, axis=1))
```

```python
import functools

import numpy as np
import jax
import jax.numpy as jnp
from jax import lax
from jax.experimental import pallas as pl
from jax.experimental.pallas import tpu as pltpu

D_MODEL = 1024
BATCH = 16
SEQ = 256
DEPTH = 2
DEC_BATCH = 2
DEC_SEQ = 1024
PAST_LEN = 256
GRID_W = 64
HEAD_DIM = 64
W_GROUP = D_MODEL // 4
N_HEADS = W_GROUP // HEAD_DIM
CHUNK = 64
WIN_ROWS = 8
WIN_COLS = 16
ROPE_BASE = 10000.0
N_GROUPS = 4
EXPERTS_PER_GROUP = 8
N_EXPERTS = N_GROUPS * EXPERTS_PER_GROUP
EXPERT_FF = 256
GROUP_FF = EXPERTS_PER_GROUP * EXPERT_FF
EPS = 1e-6

N_CTX = BATCH * SEQ
N_LAT = DEC_BATCH * DEC_SEQ
N_TOK = N_CTX + N_LAT
LANES = 128
SUBLANES = 8
TOK_TILE = 512
MOE_TILE = 256
Q_TILE = 256
VMEM_LIMIT = 48 * 1024 * 1024
NEG_BIG = -1e30
N_GATE = 2 * N_HEADS

F32 = jnp.float32
BF16 = jnp.bfloat16
HI = lax.Precision.HIGHEST


def _mm(a, b, prec=None):
    return lax.dot_general(a, b, (((1,), (0,)), ((), ())), precision=prec, preferred_element_type=F32)


def _mm_nt(a, b, prec=None):
    return lax.dot_general(a, b, (((1,), (1,)), ((), ())), precision=prec, preferred_element_type=F32)


def _mm_tn(a, b, prec=None):
    return lax.dot_general(a, b, (((0,), (0,)), ((), ())), precision=prec, preferred_element_type=F32)


def _bmm(a, b):
    return _mm(a.astype(BF16), b.astype(BF16))


def _sigmoid(x):
    return 1.0 / (1.0 + jnp.exp(-x))


def _silu(x):
    return x * _sigmoid(x)


def _softplus(x):
    return jnp.maximum(x, 0.0) + jnp.log(1.0 + jnp.exp(-jnp.abs(x)))


def _iota(shape, dim):
    return lax.broadcasted_iota(jnp.int32, shape, dim)


def _cparams(*sem):
    return pltpu.CompilerParams(dimension_semantics=sem, vmem_limit_bytes=VMEM_LIMIT)


def _cond_of_tile(i):
    n_ctx_tiles = N_CTX // TOK_TILE
    return jnp.where(i < n_ctx_tiles, 0, 1 + (i - n_ctx_tiles) // (DEC_SEQ // TOK_TILE))


def _head_sum_matrix():
    return (_iota((W_GROUP, W_GROUP), 0) // HEAD_DIM == _iota((W_GROUP, W_GROUP), 1) // HEAD_DIM).astype(F32)


def _ada_kernel(c_ref, w_ref, b_ref, o_ref):
    o_ref[0] = _mm(_silu(c_ref[...]), w_ref[0], HI) + b_ref[0]


def _ada(cond, ada_w, ada_b):
    tn = 1536
    n_out = 6 * D_MODEL
    return pl.pallas_call(
        _ada_kernel,
        grid=(DEPTH, n_out // tn),
        in_specs=[pl.BlockSpec((SUBLANES, D_MODEL), lambda l, j: (0, 0)),
                  pl.BlockSpec((1, D_MODEL, tn), lambda l, j: (l, 0, j)),
                  pl.BlockSpec((1, 1, tn), lambda l, j: (l, 0, j))],
        out_specs=pl.BlockSpec((1, SUBLANES, tn), lambda l, j: (l, 0, j)),
        out_shape=jax.ShapeDtypeStruct((DEPTH, SUBLANES, n_out), F32),
        compiler_params=_cparams("arbitrary", "arbitrary"),
        name="ada",
    )(cond, ada_w, ada_b.reshape(DEPTH, 1, n_out))


IN_WIDTHS = (4 * W_GROUP, 4 * W_GROUP, 3 * W_GROUP, 3 * W_GROUP, LANES)
IN_PACKED = sum(IN_WIDTHS)


def _pack_w_in(w):
    sizes = (3 * W_GROUP, W_GROUP, N_GATE, N_GATE, 3 * W_GROUP, W_GROUP, 3 * W_GROUP, 3 * W_GROUP)
    offs = np.concatenate([[0], np.cumsum(sizes)])
    seg = [w[:, offs[i]:offs[i + 1]] for i in range(len(sizes))]
    gate = jnp.concatenate([seg[2], seg[3], jnp.zeros((D_MODEL, LANES - 2 * N_GATE), w.dtype)], axis=1)
    return jnp.concatenate([seg[0], seg[1], seg[4], seg[5], seg[6], seg[7], gate], axis=1).astype(BF16)


def _inproj_kernel(x_ref, mod_ref, nw_ref, w_ref, *o_refs):
    x = x_ref[...]
    y = x * lax.rsqrt(jnp.mean(x * x, axis=-1, keepdims=True) + EPS) * nw_ref[...]
    h = (y * (1.0 + mod_ref[0, 1:2, :]) + mod_ref[0, 0:1, :]).astype(BF16)
    off = 0
    for o_ref, width in zip(o_refs, IN_WIDTHS):
        o_ref[...] = _mm(h, w_ref[:, off:off + width])
        off += width


def _inproj(x, mod, norm_w, w_packed):
    return pl.pallas_call(
        _inproj_kernel,
        grid=(N_TOK // TOK_TILE,),
        in_specs=[pl.BlockSpec((TOK_TILE, D_MODEL), lambda i: (i, 0)),
                  pl.BlockSpec((1, SUBLANES, D_MODEL), lambda i: (_cond_of_tile(i), 0, 0)),
                  pl.BlockSpec((1, D_MODEL), lambda i: (0, 0)),
                  pl.BlockSpec((D_MODEL, IN_PACKED), lambda i: (0, 0))],
        out_specs=[pl.BlockSpec((TOK_TILE, w), lambda i: (i, 0)) for w in IN_WIDTHS],
        out_shape=[jax.ShapeDtypeStruct((N_TOK, w), F32) for w in IN_WIDTHS],
        compiler_params=_cparams("arbitrary"),
        name="inproj",
    )(x, mod, norm_w.reshape(1, D_MODEL), w_packed)


def _seq_call(kernel_fn, args, in_specs, out_specs, out_shape, into, **kwargs):
    aliases = {}
    if into is not None:
        n_in = len(args)
        inner = kernel_fn
        kernel_fn = lambda *refs: inner(*refs[:n_in], *refs[n_in + 1:])
        aliases = {n_in: 0}
        args = list(args) + [into]
        in_specs = list(in_specs) + [pl.BlockSpec(memory_space=pl.ANY)]
    return pl.pallas_call(kernel_fn, in_specs=in_specs, out_specs=out_specs, out_shape=out_shape,
                          input_output_aliases=aliases, **kwargs)(*args)


def _shift_rows(p, t):
    row = _iota(p.shape, 0)
    prev = jnp.where(row == 0, 0.0, pltpu.roll(p, 1, 0))
    nxt = jnp.where(row == t - 1, 0.0, pltpu.roll(p, t - 1, 0))
    return prev, nxt


def _conv3(x, w_ref, t):
    prev, nxt = _shift_rows(x, t)
    return w_ref[0:1, :] * prev + w_ref[1:2, :] * x + w_ref[2:3, :] * nxt


def _sconv_kernel(s_ref, w_ref, o_ref, *, t):
    s = s_ref[...]
    p = s[:, W_GROUP:2 * W_GROUP] * s[:, 2 * W_GROUP:]
    o_ref[...] = s[:, :W_GROUP] * _conv3(p, w_ref, t)


def _sconv(s_all, w, t, n_seq, first_block, into=None):
    return _seq_call(
        functools.partial(_sconv_kernel, t=t), (s_all, w),
        in_specs=[pl.BlockSpec((t, 3 * W_GROUP), lambda i: (i + first_block, 0)),
                  pl.BlockSpec((SUBLANES, W_GROUP), lambda i: (0, 0))],
        out_specs=pl.BlockSpec((t, W_GROUP), lambda i: (i + first_block, 0)),
        out_shape=jax.ShapeDtypeStruct((N_TOK, W_GROUP), F32),
        into=into, grid=(n_seq,), compiler_params=_cparams("arbitrary"), name="sconv")


def _chunk_scan(x, reverse):
    t = x.shape[0]
    pos = _iota(x.shape, 0) % CHUNK
    step = 1
    while step < CHUNK:
        if reverse:
            x = x + jnp.where(pos < CHUNK - step, pltpu.roll(x, t - step, 0), 0.0)
        else:
            x = x + jnp.where(pos >= step, pltpu.roll(x, step, 0), 0.0)
        step *= 2
    return x


GDN_GROUP_CHUNKS = 4
GDN_CHAINS = GDN_GROUP_CHUNKS * N_GATE
GDN_PAIRS = GDN_GROUP_CHUNKS * N_HEADS


def _gdn_kernel(a_ref, gate_ref, convw_ref, alog_ref, dtb_ref, nw_ref, s0_ref, o_ref, sfin_ref,
                q_s, kv_s, kt_s, gc_s, eg_s, beta_s, gcrow_s, ekdrow_s, cdec_s, uo_s, wq_s, attn_s, kdt_s,
                st_s, wsqs_s, kk_s, d_s, off_s, m1_s, m2_s, p_s, m3_s, rhs_s, px_s, *, t):
    n_chunks = t // CHUNK
    a = a_ref[...]
    qkv = _silu(_conv3(a[:, :3 * W_GROUP], convw_ref, t))
    q = qkv[:, :W_GROUP]
    k = qkv[:, W_GROUP:2 * W_GROUP]
    v = qkv[:, 2 * W_GROUP:]
    head_sum = _head_sum_matrix()
    q = q * lax.rsqrt(_mm(q * q, head_sum, HI) + EPS) * (HEAD_DIM ** -0.5)
    k = k * lax.rsqrt(_mm(k * k, head_sum, HI) + EPS)
    for h in range(N_HEADS):
        hs = slice(h * HEAD_DIM, (h + 1) * HEAD_DIM)
        q_s[h] = q[:, hs]
        kv_s[h] = jnp.concatenate([k[:, hs], v[:, hs]], axis=1)
    k_t = k.T
    for c in range(n_chunks):
        kt_s[c] = k_t[:, c * CHUNK:(c + 1) * CHUNK]

    gates = gate_ref[...]
    log_a = -jnp.exp(alog_ref[...]) * _softplus(gates + dtb_ref[...])
    beta_s[...] = _sigmoid(gates)

    ci = _iota((CHUNK, CHUNK), 0)
    cj = _iota((CHUNK, CHUNK), 1)
    eye = (ci == cj).astype(F32)
    blk_mask = (ci // 16) == (cj // 16)
    low_half = _iota((CHUNK, 2 * HEAD_DIM), 1) < HEAD_DIM

    prefix = _chunk_scan(log_a, reverse=False)
    suffix = _chunk_scan(log_a, reverse=True)
    gc = jnp.where(_iota((t, LANES), 1) < N_HEADS, prefix, suffix)
    gt = prefix + suffix - log_a
    gc_s[...] = gc
    eg_s[...] = jnp.exp(gc)
    gc_t = gc.T
    ekd_t = jnp.exp(gt - gc).T
    cdec_t = jnp.exp(gt).T
    for c in range(n_chunks):
        lanes = slice(c * CHUNK, (c + 1) * CHUNK)
        gcrow_s[c] = gc_t[:N_GATE, lanes]
        ekdrow_s[c] = ekd_t[:N_GATE, lanes]
        cdec_s[c] = jnp.concatenate([cdec_t[:N_GATE, lanes]] * 2, axis=1)

    def solve_group(grp, carry):
        row0 = grp * (GDN_GROUP_CHUNKS * CHUNK)
        chains = [(cl, a_idx) for cl in range(GDN_GROUP_CHUNKS) for a_idx in range(N_GATE)]

        def rows_of(cl):
            return pl.ds(pl.multiple_of(row0 + cl * CHUNK, CHUNK), CHUNK)

        for cl in range(GDN_GROUP_CHUNKS):
            for h in range(N_HEADS):
                rows = rows_of(cl)
                kq = jnp.concatenate([kv_s[h, rows, :HEAD_DIM], q_s[h, rows, :]], axis=0)
                k_t_h = kt_s[grp * GDN_GROUP_CHUNKS + cl, h * HEAD_DIM:(h + 1) * HEAD_DIM, :]
                kk_s[cl * N_HEADS + h] = _bmm(kq, k_t_h)
        for b, (cl, a_idx) in enumerate(chains):
            backward = a_idx >= N_HEADS
            h = a_idx % N_HEADS
            rows, c = rows_of(cl), grp * GDN_GROUP_CHUNKS + cl
            incl = (cj >= ci) if backward else (cj <= ci)
            strict = (cj > ci) if backward else (cj < ci)
            bt = beta_s[rows, N_GATE + a_idx:N_GATE + a_idx + 1]
            decay = jnp.exp(jnp.where(incl, gc_s[rows, a_idx:a_idx + 1] - gcrow_s[c, a_idx:a_idx + 1, :], NEG_BIG))
            low = jnp.where(strict, kk_s[cl * N_HEADS + h, :CHUNK, :] * bt * decay, 0.0)
            attn_s[a_idx, rows, :] = (kk_s[cl * N_HEADS + h, CHUNK:, :] * decay).astype(BF16)
            d = jnp.where(blk_mask, low, 0.0)
            d_s[b] = d
            off_s[b] = low - d
            rhs_s[b] = kv_s[h, rows, :] * bt * jnp.where(low_half, eg_s[rows, a_idx:a_idx + 1], 1.0)
        for b in range(GDN_CHAINS):
            m1_s[b] = _bmm(d_s[b], d_s[b])
        for b in range(GDN_CHAINS):
            d, d2 = d_s[b], m1_s[b]
            m2_s[b] = _bmm(d2, d2)
            p_s[b] = eye - d + d2 - _bmm(d, d2)
        for b in range(GDN_CHAINS):
            d4, p = m2_s[b], p_s[b]
            m3_s[b] = _bmm(d4, d4)
            p_s[b] = p + _bmm(p, d4)
        for b in range(GDN_CHAINS):
            p = p_s[b]
            p_s[b] = p + _bmm(p, m3_s[b])
        for b in range(GDN_CHAINS):
            p = p_s[b]
            m1_s[b] = _bmm(p, off_s[b])
            px_s[b] = _bmm(p, rhs_s[b])
        for b in range(GDN_CHAINS):
            m2_s[b] = _bmm(m1_s[b], m1_s[b])
        for b in range(GDN_CHAINS):
            n, n2 = m1_s[b], m2_s[b]
            m3_s[b] = eye - n + n2 - _bmm(n, n2)
        for b, (cl, a_idx) in enumerate(chains):
            h = a_idx % N_HEADS
            rows, c = rows_of(cl), grp * GDN_GROUP_CHUNKS + cl
            wu = _bmm(m3_s[b], px_s[b])
            uo_s[a_idx, rows, :] = wu
            wq_s[a_idx, c, :CHUNK, :] = wu[:, :HEAD_DIM].astype(BF16)
            wq_s[a_idx, c, CHUNK:, :] = (q_s[h, rows, :] * eg_s[rows, a_idx:a_idx + 1]).astype(BF16)
            k_t_h = kt_s[c, h * HEAD_DIM:(h + 1) * HEAD_DIM, :]
            kdt_s[a_idx, c] = (k_t_h * ekdrow_s[c, a_idx:a_idx + 1, :]).astype(BF16)
        return carry

    lax.fori_loop(0, n_chunks // GDN_GROUP_CHUNKS, solve_group, 0)

    for i in range(N_GATE):
        st_s[i] = jnp.concatenate([jnp.zeros((HEAD_DIM, HEAD_DIM), F32), s0_ref[0, i]], axis=1)

    def scan_chunk(c, carry):
        def chunk_of(a_idx):
            return (n_chunks - 1 - c) if a_idx >= N_HEADS else c

        for a_idx in range(N_GATE):
            wsqs_s[a_idx] = _mm(wq_s[a_idx, chunk_of(a_idx)], st_s[a_idx].astype(BF16))
        for a_idx in range(N_GATE):
            cc = chunk_of(a_idx)
            rows = pl.ds(pl.multiple_of(cc * CHUNK, CHUNK), CHUNK)
            v_new = (uo_s[a_idx, rows, :] - wsqs_s[a_idx, :CHUNK, :]).astype(BF16)
            uo_s[a_idx, rows, :] = wsqs_s[a_idx, CHUNK:, :] + _mm(attn_s[a_idx, rows, :], v_new)
            st_s[a_idx] = st_s[a_idx] * cdec_s[cc, a_idx:a_idx + 1, :] + _mm(kdt_s[a_idx, cc], v_new)
        return carry

    lax.fori_loop(0, n_chunks, scan_chunk, 0)
    for i in range(N_GATE):
        sfin_ref[0, i] = st_s[i, :, HEAD_DIM:]

    o = jnp.concatenate([(uo_s[h] + uo_s[N_HEADS + h])[:, HEAD_DIM:] for h in range(N_HEADS)], axis=1)
    ms = _mm(o * o, head_sum, HI) * (1.0 / HEAD_DIM)
    o_ref[...] = o * lax.rsqrt(ms + EPS) * nw_ref[...] * _silu(a[:, 3 * W_GROUP:])


def _gdn(a_all, gate_all, conv_w, a_log, dt_bias, norm_w, s0, t, n_seq, first_block, into=None):
    small = lambda: pl.BlockSpec((1, LANES), lambda i: (0, 0))
    n_chunks = t // CHUNK
    wide = 2 * HEAD_DIM
    scratch = [pltpu.VMEM((N_HEADS, t, HEAD_DIM), F32),
               pltpu.VMEM((N_HEADS, t, wide), F32),
               pltpu.VMEM((n_chunks, W_GROUP, CHUNK), F32),
               pltpu.VMEM((t, LANES), F32), pltpu.VMEM((t, LANES), F32), pltpu.VMEM((t, LANES), F32),
               pltpu.VMEM((n_chunks, N_GATE, CHUNK), F32), pltpu.VMEM((n_chunks, N_GATE, CHUNK), F32),
               pltpu.VMEM((n_chunks, N_GATE, wide), F32),
               pltpu.VMEM((N_GATE, t, wide), F32),
               pltpu.VMEM((N_GATE, n_chunks, 2 * CHUNK, HEAD_DIM), BF16),
               pltpu.VMEM((N_GATE, t, CHUNK), BF16),
               pltpu.VMEM((N_GATE, n_chunks, HEAD_DIM, CHUNK), BF16),
               pltpu.VMEM((N_GATE, HEAD_DIM, wide), F32),
               pltpu.VMEM((N_GATE, 2 * CHUNK, wide), F32),
               pltpu.VMEM((GDN_PAIRS, 2 * CHUNK, CHUNK), F32)]
    scratch += [pltpu.VMEM((GDN_CHAINS, CHUNK, CHUNK), F32)] * 6
    scratch += [pltpu.VMEM((GDN_CHAINS, CHUNK, wide), F32)] * 2
    return _seq_call(
        functools.partial(_gdn_kernel, t=t), (a_all, gate_all, conv_w, a_log, dt_bias, norm_w, s0),
        in_specs=[pl.BlockSpec((t, 4 * W_GROUP), lambda i: (i + first_block, 0)),
                  pl.BlockSpec((t, LANES), lambda i: (i + first_block, 0)),
                  pl.BlockSpec((SUBLANES, 3 * W_GROUP), lambda i: (0, 0)),
                  small(), small(),
                  pl.BlockSpec((1, W_GROUP), lambda i: (0, 0)),
                  pl.BlockSpec((1, N_GATE, HEAD_DIM, HEAD_DIM), lambda i: (i, 0, 0, 0))],
        out_specs=[pl.BlockSpec((t, W_GROUP), lambda i: (i + first_block, 0)),
                   pl.BlockSpec((1, N_GATE, HEAD_DIM, HEAD_DIM), lambda i: (i, 0, 0, 0))],
        out_shape=[jax.ShapeDtypeStruct((N_TOK, W_GROUP), F32),
                   jax.ShapeDtypeStruct((n_seq, N_GATE, HEAD_DIM, HEAD_DIM), F32)],
        scratch_shapes=scratch,
        into=into, grid=(n_seq,), compiler_params=_cparams("arbitrary"), name="gdn")


def _swap16(x):
    width = x.shape[-1]
    first = (_iota(x.shape, 1) // 16) % 2 == 0
    return jnp.where(first, pltpu.roll(x, width - 16, 1), pltpu.roll(x, 16, 1))


def _block_diag_heads(s0_ref, first):
    zero = jnp.zeros((HEAD_DIM, HEAD_DIM), F32)
    return jnp.concatenate(
        [jnp.concatenate([s0_ref[0, first + h] if j == h else zero for j in range(N_HEADS)], axis=1)
         for h in range(N_HEADS)], axis=0)


def _ret_kernel(r_ref, lg_ref, s0_ref, cos_ref, sin_ref, o_ref, sfin_ref, *, t, latent):
    r = r_ref[...]
    q = r[:, :W_GROUP]
    k = r[:, W_GROUP:2 * W_GROUP]
    v = r[:, 2 * W_GROUP:3 * W_GROUP]
    if latent:
        q = q * cos_ref[...] + _swap16(q) * sin_ref[...]
        k = k * cos_ref[...] + _swap16(k) * sin_ref[...]
    k = k * (HEAD_DIM ** -0.5)
    lg = -_softplus(-lg_ref[...])
    lgf, lgb = lg[0:1, :], lg[1:2, :]
    head = _iota((1, W_GROUP), 1) // HEAD_DIM
    head_sum = _head_sum_matrix()
    pos = _iota((t, 1), 0).astype(F32)
    q_b = q.astype(BF16)
    kt_b = k.T.astype(BF16)
    v_heads = [jnp.where(head == h, v, 0.0).astype(BF16) for h in range(N_HEADS)]
    if latent:
        s0f = _block_diag_heads(s0_ref, 0)
        s0b = _block_diag_heads(s0_ref, N_HEADS)
    for qt in range(t // Q_TILE):
        rows = slice(qt * Q_TILE, (qt + 1) * Q_TILE)
        diff = (_iota((Q_TILE, t), 0) + qt * Q_TILE - _iota((Q_TILE, t), 1)).astype(F32)
        o = jnp.zeros((Q_TILE, W_GROUP), F32)
        for h in range(N_HEADS):
            lgf_h = lgf[:, h * HEAD_DIM:h * HEAD_DIM + 1]
            lgb_h = lgb[:, h * HEAD_DIM:h * HEAD_DIM + 1]
            dmat = (jnp.exp(jnp.where(diff >= 0, diff * lgf_h, NEG_BIG))
                    + jnp.exp(jnp.where(diff <= 0, -diff * lgb_h, NEG_BIG)))
            s = _mm(jnp.where(head == h, q_b[rows], 0.0), kt_b) * dmat
            o = o + _mm(s.astype(BF16), v_heads[h])
        if latent:
            p = pos[rows]
            o = o + jnp.exp((p + 1.0) * lgf) * _bmm(q_b[rows], s0f) + jnp.exp((t - p) * lgb) * _bmm(q_b[rows], s0b)
        oc = o - _mm(o, head_sum, HI) * (1.0 / HEAD_DIM)
        on = oc * lax.rsqrt(_mm(oc * oc, head_sum, HI) * (1.0 / HEAD_DIM) + EPS)
        o_ref[rows, :] = on * _silu(r[rows, 3 * W_GROUP:])
    v_b = v.astype(BF16)
    sf = _mm_tn((k * jnp.exp((t - 1.0 - pos) * lgf)).astype(BF16), v_b)
    sb = _mm_tn((k * jnp.exp(pos * lgb)).astype(BF16), v_b)
    for h in range(N_HEADS):
        hs = slice(h * HEAD_DIM, (h + 1) * HEAD_DIM)
        sf_h, sb_h = sf[hs, hs], sb[hs, hs]
        if latent:
            sf_h = sf_h + jnp.exp(t * lgf[:, h * HEAD_DIM:h * HEAD_DIM + 1]) * s0_ref[0, h]
            sb_h = sb_h + jnp.exp(t * lgb[:, h * HEAD_DIM:h * HEAD_DIM + 1]) * s0_ref[0, N_HEADS + h]
        sfin_ref[0, h] = sf_h
        sfin_ref[0, N_HEADS + h] = sb_h


def _ret(r_all, logit, s0, cos, sin, t, n_seq, first_block, latent, into=None):
    return _seq_call(
        functools.partial(_ret_kernel, t=t, latent=latent), (r_all, logit, s0, cos, sin),
        in_specs=[pl.BlockSpec((t, 4 * W_GROUP), lambda i: (i + first_block, 0)),
                  pl.BlockSpec((2, W_GROUP), lambda i: (0, 0)),
                  pl.BlockSpec((1, N_GATE, HEAD_DIM, HEAD_DIM), lambda i: (i, 0, 0, 0)),
                  pl.BlockSpec((t, W_GROUP), lambda i: (0, 0)),
                  pl.BlockSpec((t, W_GROUP), lambda i: (0, 0))],
        out_specs=[pl.BlockSpec((t, W_GROUP), lambda i: (i + first_block, 0)),
                   pl.BlockSpec((1, N_GATE, HEAD_DIM, HEAD_DIM), lambda i: (i, 0, 0, 0))],
        out_shape=[jax.ShapeDtypeStruct((N_TOK, W_GROUP), F32),
                   jax.ShapeDtypeStruct((n_seq, N_GATE, HEAD_DIM, HEAD_DIM), F32)],
        into=into, grid=(n_seq,), compiler_params=_cparams("arbitrary"), name="ret")


def _rope_tables(t):
    pos = np.arange(t)
    row = (pos // GRID_W).astype(np.float32)
    col = (pos % GRID_W).astype(np.float32)
    nf = HEAD_DIM // 4
    inv_freq = jnp.power(ROPE_BASE, -jnp.arange(nf, dtype=F32) / nf)
    ang_r = jnp.asarray(row)[:, None] * inv_freq[None, :]
    ang_c = jnp.asarray(col)[:, None] * inv_freq[None, :]
    cos = jnp.concatenate([jnp.cos(ang_r)] * 2 + [jnp.cos(ang_c)] * 2, axis=1)
    sin = jnp.concatenate([-jnp.sin(ang_r), jnp.sin(ang_r), -jnp.sin(ang_c), jnp.sin(ang_c)], axis=1)
    return jnp.tile(cos, (1, N_HEADS)), jnp.tile(sin, (1, N_HEADS))


def _head_rms(x, w, head_sum):
    return x * lax.rsqrt(_mm(x * x, head_sum, HI) * (1.0 / HEAD_DIM) + EPS) * w


def _ctx_attn_kernel(n_ref, qw_ref, kw_ref, o_ref, k_out, v_out):
    n = n_ref[...]
    head_sum = _head_sum_matrix()
    head = _iota((1, W_GROUP), 1) // HEAD_DIM
    q = (_head_rms(n[:, :W_GROUP], qw_ref[...], head_sum) * (HEAD_DIM ** -0.5)).astype(BF16)
    k = _head_rms(n[:, W_GROUP:2 * W_GROUP], kw_ref[...], head_sum)
    v = n[:, 2 * W_GROUP:]
    k_out[0] = k
    v_out[0] = v
    kt_b = k.T.astype(BF16)
    o = jnp.zeros((SEQ, W_GROUP), F32)
    for h in range(N_HEADS):
        s = _mm(jnp.where(head == h, q, 0.0), kt_b)
        p = jnp.exp(s - jnp.max(s, axis=-1, keepdims=True))
        o = o + _mm(p.astype(BF16), jnp.where(head == h, v, 0.0).astype(BF16)) / jnp.sum(p, axis=-1, keepdims=True)
    o_ref[...] = o


def _ctx_attn(n_all, qw, kw, into):
    return _seq_call(
        _ctx_attn_kernel, (n_all, qw, kw),
        in_specs=[pl.BlockSpec((SEQ, 3 * W_GROUP), lambda i: (i, 0)),
                  pl.BlockSpec((1, W_GROUP), lambda i: (0, 0)),
                  pl.BlockSpec((1, W_GROUP), lambda i: (0, 0))],
        out_specs=[pl.BlockSpec((SEQ, W_GROUP), lambda i: (i, 0)),
                   pl.BlockSpec((1, SEQ, W_GROUP), lambda i: (i, 0, 0)),
                   pl.BlockSpec((1, SEQ, W_GROUP), lambda i: (i, 0, 0))],
        out_shape=[jax.ShapeDtypeStruct((N_TOK, W_GROUP), F32),
                   jax.ShapeDtypeStruct((BATCH, SEQ, W_GROUP), F32),
                   jax.ShapeDtypeStruct((BATCH, SEQ, W_GROUP), F32)],
        into=into, grid=(BATCH,), compiler_params=_cparams("arbitrary"), name="ctx_attn")


def _nat_kernel(n_ref, ck_ref, cv_ref, bias_ref, qw_ref, kw_ref, o_ref, q_s, kt_s, ckt_s):
    h = pl.program_id(1)

    @pl.when(h == 0)
    def _():
        head_sum = _head_sum_matrix()
        n = n_ref[...]
        q_s[...] = (_head_rms(n[:, :W_GROUP], qw_ref[...], head_sum) * (HEAD_DIM ** -0.5)).astype(BF16)
        kt_s[...] = _head_rms(n[:, W_GROUP:2 * W_GROUP], kw_ref[...], head_sum).T.astype(BF16)
        ckt_s[...] = ck_ref[0].T.astype(BF16)
        o_ref[...] = jnp.zeros_like(o_ref)

    mine = _iota((1, W_GROUP), 1) // HEAD_DIM == h
    v = jnp.where(mine, n_ref[:, 2 * W_GROUP:], 0.0).astype(BF16)
    cv = jnp.where(mine, cv_ref[0], 0.0).astype(BF16)
    for qt in range(DEC_SEQ // Q_TILE):
        rows = slice(qt * Q_TILE, (qt + 1) * Q_TILE)
        q = jnp.where(mine, q_s[rows, :], 0.0)
        grid_rows = range(qt * Q_TILE // GRID_W, (qt + 1) * Q_TILE // GRID_W)
        s_loc = _mm(q, kt_s[...]) + jnp.concatenate([_nat_bias_strip(bias_ref, row) for row in grid_rows], axis=0)
        s_ctx = _mm(q, ckt_s[...])
        m = jnp.maximum(jnp.max(s_loc, axis=-1, keepdims=True), jnp.max(s_ctx, axis=-1, keepdims=True))
        p_loc = jnp.exp(s_loc - m)
        p_ctx = jnp.exp(s_ctx - m)
        den = jnp.sum(p_loc, axis=-1, keepdims=True) + jnp.sum(p_ctx, axis=-1, keepdims=True)
        o_ref[rows, :] += (_mm(p_loc.astype(BF16), v) + _mm(p_ctx.astype(BF16), cv)) / den


def _nat(n_all, ck, cv, bias, qw, kw, into):
    first_block = N_CTX // DEC_SEQ
    return _seq_call(
        _nat_kernel, (n_all, ck, cv, bias, qw, kw),
        in_specs=[pl.BlockSpec((DEC_SEQ, 3 * W_GROUP), lambda b, h: (b + first_block, 0)),
                  pl.BlockSpec((1, PAST_LEN, W_GROUP), lambda b, h: (b, 0, 0)),
                  pl.BlockSpec((1, PAST_LEN, W_GROUP), lambda b, h: (b, 0, 0)),
                  pl.BlockSpec((1, 3 * N_ROW_OFF - 1, GRID_W, 2 * GRID_W), lambda b, h: (h, 0, 0, 0)),
                  pl.BlockSpec((1, W_GROUP), lambda b, h: (0, 0)),
                  pl.BlockSpec((1, W_GROUP), lambda b, h: (0, 0))],
        out_specs=pl.BlockSpec((DEC_SEQ, W_GROUP), lambda b, h: (b + first_block, 0)),
        out_shape=jax.ShapeDtypeStruct((N_TOK, W_GROUP), F32),
        scratch_shapes=[pltpu.VMEM((DEC_SEQ, W_GROUP), BF16), pltpu.VMEM((W_GROUP, DEC_SEQ), BF16),
                        pltpu.VMEM((W_GROUP, PAST_LEN), BF16)],
        into=into, grid=(DEC_BATCH, N_HEADS), compiler_params=_cparams("arbitrary", "arbitrary"), name="nat")


N_ROW_OFF = 2 * WIN_ROWS - 1
NAT_ROWS = DEC_SEQ // GRID_W
NAT_KH = min(WIN_ROWS, NAT_ROWS)
NAT_PAIR, NAT_LOW, NAT_HIGH = 0, N_ROW_OFF - 1, 2 * N_ROW_OFF - 1


def _nat_tables(rpb):
    c = np.arange(GRID_W)
    c0 = np.clip(c - WIN_COLS // 2, 0, GRID_W - WIN_COLS)
    col_in = (c[None, :] >= c0[:, None]) & (c[None, :] < c0[:, None] + WIN_COLS)
    col_idx = np.clip(c[None, :] - c[:, None], -(WIN_COLS - 1), WIN_COLS - 1) + WIN_COLS - 1
    col_hot = (col_idx[..., None] == np.arange(2 * WIN_COLS - 1)).astype(np.float32)
    tz = jnp.where(col_in, jnp.einsum('hab,qkb->haqk', rpb.astype(F32), col_hot, precision=HI), NEG_BIG)
    neg = jnp.full_like(tz, NEG_BIG)
    return jnp.concatenate([jnp.concatenate([tz[:, :-1], tz[:, 1:]], axis=-1),
                            jnp.concatenate([tz, neg], axis=-1), jnp.concatenate([neg, tz], axis=-1)], axis=1)


def _nat_bias_strip(tab_ref, row):
    first_key = min(max(row - NAT_KH // 2, 0), NAT_ROWS - NAT_KH)
    off = first_key - row + WIN_ROWS - 1
    tiles = {}
    done, key = 0, first_key
    if key % 2 == 1:
        tiles[key // 2] = tab_ref[0, NAT_HIGH + off]
        done, key = 1, key + 1
    while done + 1 < NAT_KH:
        tiles[key // 2] = tab_ref[0, NAT_PAIR + off + done]
        done, key = done + 2, key + 2
    if done < NAT_KH:
        tiles[key // 2] = tab_ref[0, NAT_LOW + off + done]
    outside = jnp.full((GRID_W, 2 * GRID_W), NEG_BIG, F32)
    return jnp.concatenate([tiles.get(i, outside) for i in range(NAT_ROWS // 2)], axis=1)


GSEL_LANE = N_EXPERTS


def _pack_router(we, be, wg, bg):
    pad = LANES - N_EXPERTS - N_GROUPS
    w = jnp.concatenate([we, wg, jnp.zeros((D_MODEL, pad), F32)], axis=1)
    b = jnp.concatenate([be, bg, jnp.zeros((pad,), F32)]).reshape(1, LANES)
    hi = w.astype(BF16)
    lo = (w - hi.astype(F32)).astype(BF16)
    return jnp.concatenate([hi, lo], axis=1), b


def _lane_min_where(mask, lane):
    return jnp.min(jnp.where(mask, lane, LANES), axis=-1, keepdims=True)


def _outproj_kernel(x_ref, m0, m1, m2, m3, mod_ref, nw_ref, w_ref, rw_ref, rb_ref, x_out, hf_out, route_out):
    acc = None
    for i, m_ref in enumerate((m0, m1, m2, m3)):
        part = _mm(m_ref[...].astype(BF16), w_ref[i * W_GROUP:(i + 1) * W_GROUP, :])
        acc = part if acc is None else acc + part
    x = x_ref[...] + mod_ref[0, 2:3, :] * acc
    x_out[...] = x
    y = x * lax.rsqrt(jnp.mean(x * x, axis=-1, keepdims=True) + EPS) * nw_ref[...]
    hf = y * (1.0 + mod_ref[0, 4:5, :]) + mod_ref[0, 3:4, :]
    hf_hi = hf.astype(BF16)
    hf_out[...] = hf_hi

    hf_lo = (hf - hf_hi.astype(F32)).astype(BF16)
    both = _mm(hf_hi, rw_ref[...])
    logits = both[:, :LANES] + both[:, LANES:] + _mm(hf_lo, rw_ref[:, :LANES]) + rb_ref[...]
    lane = _iota(logits.shape, 1)
    is_g = (lane >= N_EXPERTS) & (lane < N_EXPERTS + N_GROUPS)
    gl = jnp.where(is_g, logits, NEG_BIG)
    ge = jnp.exp(gl - jnp.max(gl, axis=-1, keepdims=True))
    gp = jnp.where(is_g, ge / jnp.sum(ge, axis=-1, keepdims=True), -1.0)
    gw = jnp.max(gp, axis=-1, keepdims=True)
    gsel = _lane_min_where(gp == gw, lane) - N_EXPERTS
    in_grp = (lane // EXPERTS_PER_GROUP == gsel) & (lane < N_EXPERTS)
    el = jnp.where(in_grp, logits, NEG_BIG)
    ee = jnp.exp(el - jnp.max(el, axis=-1, keepdims=True))
    ep = jnp.where(in_grp, ee / jnp.sum(ee, axis=-1, keepdims=True), -1.0)
    t1 = jnp.max(ep, axis=-1, keepdims=True)
    i1 = _lane_min_where(ep == t1, lane)
    ep2 = jnp.where(lane == i1, -1.0, ep)
    t2 = jnp.max(ep2, axis=-1, keepdims=True)
    i2 = _lane_min_where(ep2 == t2, lane)
    tsum = t1 + t2
    combine = jnp.where(lane == i1, gw * (t1 / tsum), 0.0) + jnp.where(lane == i2, gw * (t2 / tsum), 0.0)
    route_out[...] = jnp.where(lane == GSEL_LANE, gsel.astype(F32), combine)


def _outproj(x, mixed, mod, norm_w, w_out, rw, rb):
    tile = lambda w: pl.BlockSpec((TOK_TILE, w), lambda i: (i, 0))
    whole = lambda a: pl.BlockSpec(a.shape, lambda i: (0,) * a.ndim)
    return pl.pallas_call(
        _outproj_kernel,
        grid=(N_TOK // TOK_TILE,),
        in_specs=[tile(D_MODEL)] + [tile(W_GROUP)] * 4
                 + [pl.BlockSpec((1, SUBLANES, D_MODEL), lambda i: (_cond_of_tile(i), 0, 0)),
                    pl.BlockSpec((1, D_MODEL), lambda i: (0, 0)), whole(w_out), whole(rw), whole(rb)],
        out_specs=[tile(D_MODEL), tile(D_MODEL), tile(LANES)],
        out_shape=[jax.ShapeDtypeStruct((N_TOK, D_MODEL), F32), jax.ShapeDtypeStruct((N_TOK, D_MODEL), BF16),
                   jax.ShapeDtypeStruct((N_TOK, LANES), F32)],
        compiler_params=_cparams("arbitrary"),
        name="outproj",
    )(x, *mixed, mod, norm_w.reshape(1, D_MODEL), w_out, rw, rb)


SEG_BLK = 16
LOCAL_ROWS = TOK_TILE + N_GROUPS * SEG_BLK
N_TOK_TILES = N_TOK // TOK_TILE
MOE_ROWS = -(-(N_TOK + N_TOK_TILES * N_GROUPS * (SEG_BLK - 1) + N_GROUPS * (MOE_TILE - 1)) // MOE_TILE) * MOE_TILE


def _moe_tables(gsel):
    groups = jnp.arange(N_GROUPS, dtype=jnp.int32)
    onehot = (gsel.reshape(N_TOK_TILES, TOK_TILE, 1) == groups).astype(jnp.int32)
    rank = jnp.cumsum(onehot, axis=1) - onehot
    nblk = (jnp.sum(onehot, axis=1) + SEG_BLK - 1) // SEG_BLK
    loc_blk = jnp.cumsum(nblk, axis=1) - nblk
    blocks_per_tile = MOE_TILE // SEG_BLK
    grp_tiles = (jnp.sum(nblk, axis=0) + blocks_per_tile - 1) // blocks_per_tile
    grp_tile_start = jnp.cumsum(grp_tiles) - grp_tiles
    dst_blk = grp_tile_start[None, :] * blocks_per_tile + jnp.cumsum(nblk, axis=0) - nblk
    local_pos = jnp.sum(onehot * (loc_blk[:, None, :] * SEG_BLK + rank), axis=2)
    tile_idx = jnp.arange(MOE_ROWS // MOE_TILE, dtype=jnp.int32)
    tile_group = jnp.clip(jnp.sum(tile_idx[:, None] >= grp_tile_start[None, :], axis=1) - 1, 0, N_GROUPS - 1)
    tile_valid = tile_idx < jnp.sum(grp_tiles)
    flat = lambda a: a.reshape(-1).astype(jnp.int32)
    return local_pos.astype(jnp.int32), flat(nblk), flat(loc_blk), flat(dst_blk), flat(tile_group), flat(tile_valid)


def _segment_copies(t, nblk, loc_blk, dst_blk, make_copies, action):
    for g in range(N_GROUPS):
        k = t * N_GROUPS + g

        @pl.loop(0, nblk[k])
        def _(b):
            local = pl.multiple_of((loc_blk[k] + b) * SEG_BLK, SEG_BLK)
            sorted_row = pl.multiple_of((dst_blk[k] + b) * SEG_BLK, SEG_BLK)
            for cp in make_copies(local, sorted_row):
                action(cp)


def _dispatch_kernel(nblk, loc_blk, dst_blk, hf_ref, rt_ref, lp_ref, xs_in, rs_in, xs_hbm, rs_hbm, xbuf, rbuf, sem):
    t = pl.program_id(0)
    slot = t % 2
    onehot = _iota((LOCAL_ROWS, TOK_TILE), 0) == lp_ref[0]
    xbuf[slot] = _mm(onehot.astype(BF16), hf_ref[...]).astype(BF16)
    rbuf[slot] = _mm(onehot.astype(F32), rt_ref[...], HI)

    def copies_of(s):
        def copies(local, sorted_row):
            return (pltpu.make_async_copy(xbuf.at[s, pl.ds(local, SEG_BLK)], xs_hbm.at[pl.ds(sorted_row, SEG_BLK)], sem.at[s]),
                    pltpu.make_async_copy(rbuf.at[s, pl.ds(local, SEG_BLK)], rs_hbm.at[pl.ds(sorted_row, SEG_BLK)], sem.at[s]))
        return copies

    @pl.when(t > 0)
    def _():
        _segment_copies(t - 1, nblk, loc_blk, dst_blk, copies_of(1 - slot), lambda cp: cp.wait())

    _segment_copies(t, nblk, loc_blk, dst_blk, copies_of(slot), lambda cp: cp.start())

    @pl.when(t == N_TOK_TILES - 1)
    def _():
        _segment_copies(t, nblk, loc_blk, dst_blk, copies_of(slot), lambda cp: cp.wait())


def _dispatch(hf, route, local_pos, nblk, loc_blk, dst_blk):
    grid_spec = pltpu.PrefetchScalarGridSpec(
        num_scalar_prefetch=3,
        grid=(N_TOK_TILES,),
        in_specs=[pl.BlockSpec((TOK_TILE, D_MODEL), lambda t, *_: (t, 0)),
                  pl.BlockSpec((TOK_TILE, LANES), lambda t, *_: (t, 0)),
                  pl.BlockSpec((1, 1, TOK_TILE), lambda t, *_: (t, 0, 0)),
                  pl.BlockSpec(memory_space=pl.ANY), pl.BlockSpec(memory_space=pl.ANY)],
        out_specs=[pl.BlockSpec(memory_space=pl.ANY), pl.BlockSpec(memory_space=pl.ANY)],
        scratch_shapes=[pltpu.VMEM((2, LOCAL_ROWS, D_MODEL), BF16), pltpu.VMEM((2, LOCAL_ROWS, LANES), F32),
                        pltpu.SemaphoreType.DMA((2,))],
    )
    return pl.pallas_call(
        _dispatch_kernel,
        grid_spec=grid_spec,
        out_shape=[jax.ShapeDtypeStruct((MOE_ROWS, D_MODEL), BF16), jax.ShapeDtypeStruct((MOE_ROWS, LANES), F32)],
        input_output_aliases={6: 0, 7: 1},
        compiler_params=_cparams("arbitrary"),
        name="dispatch",
    )(nblk, loc_blk, dst_blk, hf, route, local_pos.reshape(N_TOK_TILES, 1, TOK_TILE),
      jnp.zeros((MOE_ROWS, D_MODEL), BF16), jnp.zeros((MOE_ROWS, LANES), F32))


def _moe_kernel(tile_group, tile_valid, x_ref, r_ref, wg_hbm, wu_hbm, wd_hbm, y_ref,
                wg_b, wu_b, wd_b, stage_g, stage_u, stage_d, sem, *, layer):
    i = pl.program_id(0)
    g = tile_group[i]
    group_row = layer * N_GROUPS + g
    new_group = (i == 0) | (g != tile_group[jnp.maximum(i - 1, 0)])
    valid = tile_valid[i] > 0

    def weight_copies(e):
        slot = e % 2
        return (pltpu.make_async_copy(wg_hbm.at[group_row, e], stage_g.at[slot], sem.at[slot]),
                pltpu.make_async_copy(wu_hbm.at[group_row, e], stage_u.at[slot], sem.at[slot]),
                pltpu.make_async_copy(wd_hbm.at[group_row, e], stage_d.at[slot], sem.at[slot]))

    def run(load_weights):
        x = x_ref[...]
        route = r_ref[...]
        lane = _iota(route.shape, 1)
        acc = jnp.zeros((MOE_TILE, D_MODEL), F32)
        if load_weights:
            for e in range(2):
                for cp in weight_copies(e):
                    cp.start()
        for e in range(EXPERTS_PER_GROUP):
            if load_weights:
                for cp in weight_copies(e):
                    cp.wait()
                wg_b[e] = stage_g[e % 2].astype(BF16)
                wu_b[e] = stage_u[e % 2].astype(BF16)
                wd_b[e] = stage_d[e % 2].astype(BF16)
                if e + 2 < EXPERTS_PER_GROUP:
                    for cp in weight_copies(e + 2):
                        cp.start()
            cw = jnp.sum(jnp.where(lane == g * EXPERTS_PER_GROUP + e, route, 0.0), axis=-1, keepdims=True)
            act = _silu(_mm(x, wg_b[e])) * _mm(x, wu_b[e]) * cw
            acc = acc + _mm(act.astype(BF16), wd_b[e])
        y_ref[...] = acc

    @pl.when(valid & new_group)
    def _():
        run(True)

    @pl.when(valid & jnp.logical_not(new_group))
    def _():
        run(False)

    @pl.when(jnp.logical_not(valid))
    def _():
        y_ref[...] = jnp.zeros_like(y_ref)


def _moe(xs, rs, tile_group, tile_valid, wg, wu, wd, layer):
    any_spec = pl.BlockSpec(memory_space=pl.ANY)
    grid_spec = pltpu.PrefetchScalarGridSpec(
        num_scalar_prefetch=2,
        grid=(MOE_ROWS // MOE_TILE,),
        in_specs=[pl.BlockSpec((MOE_TILE, D_MODEL), lambda i, tg, tv: (i, 0)),
                  pl.BlockSpec((MOE_TILE, LANES), lambda i, tg, tv: (i, 0)),
                  any_spec, any_spec, any_spec],
        out_specs=pl.BlockSpec((MOE_TILE, D_MODEL), lambda i, tg, tv: (i, 0)),
        scratch_shapes=[pltpu.VMEM((EXPERTS_PER_GROUP, D_MODEL, EXPERT_FF), BF16),
                        pltpu.VMEM((EXPERTS_PER_GROUP, D_MODEL, EXPERT_FF), BF16),
                        pltpu.VMEM((EXPERTS_PER_GROUP, EXPERT_FF, D_MODEL), BF16),
                        pltpu.VMEM((2, D_MODEL, EXPERT_FF), F32), pltpu.VMEM((2, D_MODEL, EXPERT_FF), F32),
                        pltpu.VMEM((2, EXPERT_FF, D_MODEL), F32), pltpu.SemaphoreType.DMA((2,))],
    )
    return pl.pallas_call(
        functools.partial(_moe_kernel, layer=layer),
        grid_spec=grid_spec,
        out_shape=jax.ShapeDtypeStruct((MOE_ROWS, D_MODEL), F32),
        compiler_params=_cparams("arbitrary"),
        name="moe",
    )(tile_group, tile_valid, xs, rs, wg, wu, wd)


def _combine_kernel(nblk, loc_blk, dst_blk, x_ref, lp_ref, mod_ref, ys_hbm, o_ref, ybuf, sem):
    t = pl.program_id(0)
    slot = t % 2

    def fetch(tile, s):
        def copies(local, sorted_row):
            return (pltpu.make_async_copy(ys_hbm.at[pl.ds(sorted_row, SEG_BLK)], ybuf.at[s, pl.ds(local, SEG_BLK)], sem.at[s]),)
        ybuf[s] = jnp.zeros((LOCAL_ROWS, D_MODEL), F32)
        _segment_copies(tile, nblk, loc_blk, dst_blk, copies, lambda cp: cp.start())

    @pl.when(t == 0)
    def _():
        fetch(0, 0)

    @pl.when(t + 1 < N_TOK_TILES)
    def _():
        fetch(t + 1, 1 - slot)

    def copies_now(local, sorted_row):
        return (pltpu.make_async_copy(ys_hbm.at[pl.ds(sorted_row, SEG_BLK)], ybuf.at[slot, pl.ds(local, SEG_BLK)], sem.at[slot]),)

    _segment_copies(t, nblk, loc_blk, dst_blk, copies_now, lambda cp: cp.wait())

    onehot = (_iota((TOK_TILE, LOCAL_ROWS), 1) == lp_ref[...]).astype(BF16)
    ys = ybuf[slot]
    hi = ys.astype(BF16)
    lo = (ys - hi.astype(F32)).astype(BF16)
    y = _mm(onehot, hi) + _mm(onehot, lo)
    o_ref[...] = x_ref[...] + mod_ref[0, 5:6, :] * y


def _combine(x, ys, mod, local_pos, nblk, loc_blk, dst_blk):
    tile = pl.BlockSpec((TOK_TILE, D_MODEL), lambda t, *_: (t, 0))
    grid_spec = pltpu.PrefetchScalarGridSpec(
        num_scalar_prefetch=3,
        grid=(N_TOK_TILES,),
        in_specs=[tile, pl.BlockSpec((TOK_TILE, 1), lambda t, *_: (t, 0)),
                  pl.BlockSpec((1, SUBLANES, D_MODEL), lambda t, *_: (_cond_of_tile(t), 0, 0)),
                  pl.BlockSpec(memory_space=pl.ANY)],
        out_specs=tile,
        scratch_shapes=[pltpu.VMEM((2, LOCAL_ROWS, D_MODEL), F32), pltpu.SemaphoreType.DMA((2,))],
    )
    return pl.pallas_call(
        _combine_kernel,
        grid_spec=grid_spec,
        out_shape=jax.ShapeDtypeStruct((N_TOK, D_MODEL), F32),
        compiler_params=_cparams("arbitrary"),
        name="combine",
    )(nblk, loc_blk, dst_blk, x, local_pos.reshape(N_TOK, 1), mod, ys)


def _lane_row(v):
    v = v.reshape(-1).astype(F32)
    return jnp.concatenate([v, jnp.zeros((LANES - v.shape[0],), F32)]).reshape(1, LANES)


def _pad_rows(w):
    return jnp.concatenate([w, jnp.zeros((SUBLANES - w.shape[0], w.shape[1]), w.dtype)], axis=0)


def kernel(x_prompt, x_sample, state_gdn, state_ret, cache_nat_k, cache_nat_v, c, c_ctx, ada_w, ada_b, norm_mix_w, norm_ffn_w, w_in, gdn_conv_w, gdn_a_log, gdn_dt_bias, gdn_norm_w, ret_gamma_logit, nat_q_norm_w, nat_k_norm_w, nat_rpb, sc_conv_w, w_out, router_group_w, router_group_b, router_expert_w, router_expert_b, moe_w_gate, moe_w_up, moe_w_down):
    x = jnp.concatenate([x_prompt.reshape(N_CTX, D_MODEL), x_sample.reshape(N_LAT, D_MODEL)], axis=0)
    cond = jnp.concatenate([c_ctx[None, :], c, jnp.zeros((SUBLANES - 1 - DEC_BATCH, D_MODEL), F32)], axis=0)
    ada = _ada(cond, ada_w, ada_b).reshape(DEPTH, SUBLANES, 6, D_MODEL)
    cos, sin = _rope_tables(DEC_SEQ)
    zero_state = jnp.zeros((BATCH, N_GATE, HEAD_DIM, HEAD_DIM), F32)
    lat_block = N_CTX // DEC_SEQ
    gdn_list, ret_list, k_list, v_list = [], [], [], []
    mixed = [jnp.zeros((N_TOK, W_GROUP), F32) for _ in range(4)]
    for l in range(DEPTH):
        mod = jnp.concatenate([ada[l, :1 + DEC_BATCH], jnp.zeros((1 + DEC_BATCH, SUBLANES - 6, D_MODEL), F32)], axis=1)
        a_gdn, a_ret, a_nat, a_sc, a_gate = _inproj(x, mod, norm_mix_w[l], _pack_w_in(w_in[l]))

        conv_w = _pad_rows(gdn_conv_w[l])
        a_log, dt_b = _lane_row(gdn_a_log[l]), _lane_row(gdn_dt_bias[l])
        gnw = jnp.tile(gdn_norm_w[l], N_HEADS).reshape(1, W_GROUP)
        o_gdn, s_gdn = _gdn(a_gdn, a_gate, conv_w, a_log, dt_b, gnw, zero_state, SEQ, BATCH, 0, into=mixed[0])
        s0 = state_gdn[:, l].reshape(DEC_BATCH, N_GATE, HEAD_DIM, HEAD_DIM)
        o_gdn, _ = _gdn(a_gdn, a_gate, conv_w, a_log, dt_b, gnw, s0, DEC_SEQ, DEC_BATCH, lat_block, into=o_gdn)

        logit = jnp.repeat(ret_gamma_logit[l].astype(F32), HEAD_DIM, axis=1)
        o_ret, s_ret = _ret(a_ret, logit, zero_state, cos[:SEQ], sin[:SEQ], SEQ, BATCH, 0, False, into=mixed[1])
        s0 = state_ret[:, l].reshape(DEC_BATCH, N_GATE, HEAD_DIM, HEAD_DIM)
        o_ret, _ = _ret(a_ret, logit, s0, cos, sin, DEC_SEQ, DEC_BATCH, lat_block, True, into=o_ret)

        qw = jnp.tile(nat_q_norm_w[l], N_HEADS).reshape(1, W_GROUP)
        kw = jnp.tile(nat_k_norm_w[l], N_HEADS).reshape(1, W_GROUP)
        o_nat, n_k, n_v = _ctx_attn(a_nat, qw, kw, mixed[2])
        o_nat = _nat(a_nat, cache_nat_k[:, l].reshape(DEC_BATCH, PAST_LEN, W_GROUP),
                     cache_nat_v[:, l].reshape(DEC_BATCH, PAST_LEN, W_GROUP), _nat_tables(nat_rpb[l]), qw, kw, o_nat)

        sc_w = _pad_rows(sc_conv_w[l])
        o_sc = _sconv(a_sc, sc_w, SEQ, BATCH, 0, into=mixed[3])
        o_sc = _sconv(a_sc, sc_w, DEC_SEQ, DEC_BATCH, lat_block, into=o_sc)

        mixed = [o_gdn, o_ret, o_nat, o_sc]
        rw, rb = _pack_router(router_expert_w[l], router_expert_b[l], router_group_w[l], router_group_b[l])
        x_mid, hf, route = _outproj(x, mixed, mod, norm_ffn_w[l], w_out[l].astype(BF16), rw, rb)

        local_pos, nblk, loc_blk, dst_blk, tile_group, tile_valid = _moe_tables(route[:, GSEL_LANE].astype(jnp.int32))
        xs, rs = _dispatch(hf, route, local_pos, nblk, loc_blk, dst_blk)
        to_group = lambda w: w.reshape((DEPTH * N_GROUPS, EXPERTS_PER_GROUP) + w.shape[2:])
        ys = _moe(xs, rs, tile_group, tile_valid, to_group(moe_w_gate), to_group(moe_w_up), to_group(moe_w_down), l)
        x = _combine(x_mid, ys, mod, local_pos, nblk, loc_blk, dst_blk)

        gdn_list.append(s_gdn.reshape(BATCH, 2, N_HEADS, HEAD_DIM, HEAD_DIM))
        ret_list.append(s_ret.reshape(BATCH, 2, N_HEADS, HEAD_DIM, HEAD_DIM))
        k_list.append(n_k.reshape(BATCH, SEQ, N_HEADS, HEAD_DIM))
        v_list.append(n_v.reshape(BATCH, SEQ, N_HEADS, HEAD_DIM))
    return (x[:N_CTX].reshape(BATCH, SEQ, D_MODEL), x[N_CTX:].reshape(DEC_BATCH, DEC_SEQ, D_MODEL),
            jnp.stack(gdn_list, axis=1), jnp.stack(ret_list, axis=1),
            jnp.stack(k_list, axis=1), jnp.stack(v_list, axis=1))
```

```python
import functools

import numpy as np
import jax
import jax.numpy as jnp
from jax import lax
from jax.experimental import pallas as pl
from jax.experimental.pallas import tpu as pltpu

D_MODEL = 1024
BATCH = 16
SEQ = 256
DEPTH = 2
DEC_BATCH = 2
DEC_SEQ = 1024
PAST_LEN = 256
GRID_W = 64
HEAD_DIM = 64
W_GROUP = D_MODEL // 4
N_HEADS = W_GROUP // HEAD_DIM
CHUNK = 64
WIN_ROWS = 8
WIN_COLS = 16
ROPE_BASE = 10000.0
N_GROUPS = 4
EXPERTS_PER_GROUP = 8
N_EXPERTS = N_GROUPS * EXPERTS_PER_GROUP
EXPERT_FF = 256
GROUP_FF = EXPERTS_PER_GROUP * EXPERT_FF
EPS = 1e-6

N_CTX = BATCH * SEQ
N_LAT = DEC_BATCH * DEC_SEQ
N_TOK = N_CTX + N_LAT
LANES = 128
SUBLANES = 8
TOK_TILE = 512
MOE_TILE = 256
Q_TILE = 256
VMEM_LIMIT = 48 * 1024 * 1024
NEG_BIG = -1e30
N_GATE = 2 * N_HEADS

F32 = jnp.float32
BF16 = jnp.bfloat16
HI = lax.Precision.HIGHEST


def _mm(a, b, prec=None):
    return lax.dot_general(a, b, (((1,), (0,)), ((), ())), precision=prec, preferred_element_type=F32)


def _mm_nt(a, b, prec=None):
    return lax.dot_general(a, b, (((1,), (1,)), ((), ())), precision=prec, preferred_element_type=F32)


def _mm_tn(a, b, prec=None):
    return lax.dot_general(a, b, (((0,), (0,)), ((), ())), precision=prec, preferred_element_type=F32)


def _bmm(a, b):
    return _mm(a.astype(BF16), b.astype(BF16))


def _sigmoid(x):
    return 1.0 / (1.0 + jnp.exp(-x))


def _silu(x):
    return x * _sigmoid(x)


def _softplus(x):
    return jnp.maximum(x, 0.0) + jnp.log(1.0 + jnp.exp(-jnp.abs(x)))


def _iota(shape, dim):
    return lax.broadcasted_iota(jnp.int32, shape, dim)


def _cparams(*sem):
    return pltpu.CompilerParams(dimension_semantics=sem, vmem_limit_bytes=VMEM_LIMIT)


def _cond_of_tile(i):
    n_ctx_tiles = N_CTX // TOK_TILE
    return jnp.where(i < n_ctx_tiles, 0, 1 + (i - n_ctx_tiles) // (DEC_SEQ // TOK_TILE))


def _head_sum_matrix():
    return (_iota((W_GROUP, W_GROUP), 0) // HEAD_DIM == _iota((W_GROUP, W_GROUP), 1) // HEAD_DIM).astype(F32)


def _ada_kernel(c_ref, w_ref, b_ref, o_ref):
    o_ref[0] = _mm(_silu(c_ref[...]), w_ref[0], HI) + b_ref[0]


def _ada(cond, ada_w, ada_b):
    tn = 1536
    n_out = 6 * D_MODEL
    return pl.pallas_call(
        _ada_kernel,
        grid=(DEPTH, n_out // tn),
        in_specs=[pl.BlockSpec((SUBLANES, D_MODEL), lambda l, j: (0, 0)),
                  pl.BlockSpec((1, D_MODEL, tn), lambda l, j: (l, 0, j)),
                  pl.BlockSpec((1, 1, tn), lambda l, j: (l, 0, j))],
        out_specs=pl.BlockSpec((1, SUBLANES, tn), lambda l, j: (l, 0, j)),
        out_shape=jax.ShapeDtypeStruct((DEPTH, SUBLANES, n_out), F32),
        compiler_params=_cparams("arbitrary", "arbitrary"),
        name="ada",
    )(cond, ada_w, ada_b.reshape(DEPTH, 1, n_out))


IN_WIDTHS = (4 * W_GROUP, 4 * W_GROUP, 3 * W_GROUP, 3 * W_GROUP, LANES)
IN_PACKED = sum(IN_WIDTHS)


IN_TOTAL = 3 * W_GROUP + W_GROUP + 2 * N_GATE + 4 * W_GROUP + 3 * W_GROUP + 3 * W_GROUP
IN_GATE_SRC = 4 * W_GROUP
IN_SRC = (0, IN_GATE_SRC + 2 * N_GATE, IN_GATE_SRC + 2 * N_GATE + 4 * W_GROUP,
          IN_GATE_SRC + 2 * N_GATE + 7 * W_GROUP)
N_CTX_TILES = N_CTX // TOK_TILE


def _token_specs(split):
    lat0 = 0 if split else N_CTX_TILES
    return [pl.BlockSpec((TOK_TILE, D_MODEL), lambda i, *_: (jnp.minimum(i, N_CTX_TILES - 1), 0)),
            pl.BlockSpec((TOK_TILE, D_MODEL), lambda i, *_: (jnp.maximum(i, N_CTX_TILES) - N_CTX_TILES + lat0, 0))]


def _token_tile(i, ctx_ref, lat_ref):
    return jnp.where(i < N_CTX_TILES, ctx_ref[...], lat_ref[...])


def _inproj_kernel(xc_ref, xl_ref, mod_ref, nw_ref, w_ref, *refs):
    o_refs, w_s = refs[:-1], refs[-1]
    i = pl.program_id(0)

    @pl.when(i == 0)
    def _():
        rows_per_step = 128
        for r in range(0, D_MODEL, rows_per_step):
            rows = slice(r, r + rows_per_step)
            off = 0
            for src, width in zip(IN_SRC, IN_WIDTHS[:-1]):
                w_s[rows, off:off + width] = w_ref[0, rows, src:src + width].astype(BF16)
                off += width
            gate = w_ref[0, rows, IN_GATE_SRC:IN_GATE_SRC + 2 * N_GATE]
            w_s[rows, off:] = jnp.concatenate(
                [gate, jnp.zeros((rows_per_step, LANES - 2 * N_GATE), F32)], axis=1).astype(BF16)

    x = _token_tile(i, xc_ref, xl_ref)
    y = x * lax.rsqrt(jnp.mean(x * x, axis=-1, keepdims=True) + EPS) * nw_ref[...]
    h = (y * (1.0 + mod_ref[0, 1:2, :]) + mod_ref[0, 0:1, :]).astype(BF16)
    off = 0
    for o_ref, width in zip(o_refs, IN_WIDTHS):
        o_ref[...] = _mm(h, w_s[:, off:off + width])
        off += width


def _inproj(x_ctx, x_lat, split, mod, norm_w, w_in, layer):
    return pl.pallas_call(
        _inproj_kernel,
        grid=(N_TOK // TOK_TILE,),
        in_specs=_token_specs(split)
                 + [pl.BlockSpec((1, SUBLANES, D_MODEL), lambda i: (_cond_of_tile(i), 0, 0)),
                    pl.BlockSpec((1, D_MODEL), lambda i: (0, 0)),
                    pl.BlockSpec((1, D_MODEL, IN_TOTAL), lambda i: (layer, 0, 0), pipeline_mode=pl.Buffered(1))],
        out_specs=[pl.BlockSpec((TOK_TILE, w), lambda i: (i, 0)) for w in IN_WIDTHS],
        out_shape=[jax.ShapeDtypeStruct((N_TOK, w), F32) for w in IN_WIDTHS],
        scratch_shapes=[pltpu.VMEM((D_MODEL, IN_PACKED), BF16)],
        compiler_params=_cparams("arbitrary"),
        name="inproj",
    )(x_ctx, x_lat, mod, norm_w.reshape(1, D_MODEL), w_in)


def _seq_call(kernel_fn, args, in_specs, out_specs, out_shape, into, more_into=(), **kwargs):
    donors = ([] if into is None else [(into, 0)]) + list(more_into)
    n_in = len(args)
    inner = kernel_fn
    kernel_fn = lambda *refs: inner(*refs[:n_in], *refs[n_in + len(donors):])
    aliases = {n_in + j: out_idx for j, (_, out_idx) in enumerate(donors)}
    args = list(args) + [a for a, _ in donors]
    in_specs = list(in_specs) + [pl.BlockSpec(memory_space=pl.ANY)] * len(donors)
    return pl.pallas_call(kernel_fn, in_specs=in_specs, out_specs=out_specs, out_shape=out_shape,
                          input_output_aliases=aliases, **kwargs)(*args)


def _shift_rows(p, t):
    row = _iota(p.shape, 0)
    prev = jnp.where(row == 0, 0.0, pltpu.roll(p, 1, 0))
    nxt = jnp.where(row == t - 1, 0.0, pltpu.roll(p, t - 1, 0))
    return prev, nxt


def _conv3(x, w_ref, t):
    prev, nxt = _shift_rows(x, t)
    return w_ref[0:1, :] * prev + w_ref[1:2, :] * x + w_ref[2:3, :] * nxt


def _sconv_kernel(s_ref, w_ref, o_ref, *, t):
    s = s_ref[...]
    p = s[:, W_GROUP:2 * W_GROUP] * s[:, 2 * W_GROUP:]
    o_ref[...] = s[:, :W_GROUP] * _conv3(p, w_ref, t)


def _sconv(s_all, w, t, n_seq, first_block, into=None):
    return _seq_call(
        functools.partial(_sconv_kernel, t=t), (s_all, w),
        in_specs=[pl.BlockSpec((t, 3 * W_GROUP), lambda i: (i + first_block, 0)),
                  pl.BlockSpec((SUBLANES, W_GROUP), lambda i: (0, 0))],
        out_specs=pl.BlockSpec((t, W_GROUP), lambda i: (i + first_block, 0)),
        out_shape=jax.ShapeDtypeStruct((N_TOK, W_GROUP), F32),
        into=into, grid=(n_seq,), compiler_params=_cparams("arbitrary"), name="sconv")


def _chunk_scan(x, reverse):
    t = x.shape[0]
    pos = _iota(x.shape, 0) % CHUNK
    step = 1
    while step < CHUNK:
        if reverse:
            x = x + jnp.where(pos < CHUNK - step, pltpu.roll(x, t - step, 0), 0.0)
        else:
            x = x + jnp.where(pos >= step, pltpu.roll(x, step, 0), 0.0)
        step *= 2
    return x


GDN_GROUP_CHUNKS = 4
GDN_CHAINS = GDN_GROUP_CHUNKS * N_GATE
GDN_PAIRS = GDN_GROUP_CHUNKS * N_HEADS


def _gdn_kernel(a_ref, gate_ref, convw_ref, alog_ref, dtb_ref, nw_ref, s0_ref, o_ref, sfin_ref,
                q_s, kv_s, kt_s, gc_s, eg_s, beta_s, gcrow_s, ekdrow_s, cdec_s, uo_s, wq_s, attn_s, kdt_s,
                st_s, wsqs_s, kk_s, d_s, off_s, m1_s, m2_s, p_s, m3_s, rhs_s, px_s, *, t):
    n_chunks = t // CHUNK
    a = a_ref[...]
    qkv = _silu(_conv3(a[:, :3 * W_GROUP], convw_ref, t))
    q = qkv[:, :W_GROUP]
    k = qkv[:, W_GROUP:2 * W_GROUP]
    v = qkv[:, 2 * W_GROUP:]
    head_sum = _head_sum_matrix()
    q = q * lax.rsqrt(_mm(q * q, head_sum, HI) + EPS) * (HEAD_DIM ** -0.5)
    k = k * lax.rsqrt(_mm(k * k, head_sum, HI) + EPS)
    for h in range(N_HEADS):
        hs = slice(h * HEAD_DIM, (h + 1) * HEAD_DIM)
        q_s[h] = q[:, hs]
        kv_s[h] = jnp.concatenate([k[:, hs], v[:, hs]], axis=1)
    k_t = k.T
    for c in range(n_chunks):
        kt_s[c] = k_t[:, c * CHUNK:(c + 1) * CHUNK]

    gates = gate_ref[...]
    log_a = -jnp.exp(alog_ref[...]) * _softplus(gates + dtb_ref[...])
    beta_s[...] = _sigmoid(gates)

    ci = _iota((CHUNK, CHUNK), 0)
    cj = _iota((CHUNK, CHUNK), 1)
    eye = (ci == cj).astype(F32)
    blk_mask = (ci // 16) == (cj // 16)
    low_half = _iota((CHUNK, 2 * HEAD_DIM), 1) < HEAD_DIM

    prefix = _chunk_scan(log_a, reverse=False)
    suffix = _chunk_scan(log_a, reverse=True)
    gc = jnp.where(_iota((t, LANES), 1) < N_HEADS, prefix, suffix)
    gt = prefix + suffix - log_a
    gc_s[...] = gc
    eg_s[...] = jnp.exp(gc)
    gc_t = gc.T
    ekd_t = jnp.exp(gt - gc).T
    cdec_t = jnp.exp(gt).T
    for c in range(n_chunks):
        lanes = slice(c * CHUNK, (c + 1) * CHUNK)
        gcrow_s[c] = gc_t[:N_GATE, lanes]
        ekdrow_s[c] = ekd_t[:N_GATE, lanes]
        cdec_s[c] = jnp.concatenate([cdec_t[:N_GATE, lanes]] * 2, axis=1)

    def solve_group(grp, carry):
        row0 = grp * (GDN_GROUP_CHUNKS * CHUNK)
        chains = [(cl, a_idx) for cl in range(GDN_GROUP_CHUNKS) for a_idx in range(N_GATE)]

        def rows_of(cl):
            return pl.ds(pl.multiple_of(row0 + cl * CHUNK, CHUNK), CHUNK)

        for cl in range(GDN_GROUP_CHUNKS):
            for h in range(N_HEADS):
                rows = rows_of(cl)
                kq = jnp.concatenate([kv_s[h, rows, :HEAD_DIM], q_s[h, rows, :]], axis=0)
                k_t_h = kt_s[grp * GDN_GROUP_CHUNKS + cl, h * HEAD_DIM:(h + 1) * HEAD_DIM, :]
                kk_s[cl * N_HEADS + h] = _bmm(kq, k_t_h)
        for b, (cl, a_idx) in enumerate(chains):
            backward = a_idx >= N_HEADS
            h = a_idx % N_HEADS
            rows, c = rows_of(cl), grp * GDN_GROUP_CHUNKS + cl
            incl = (cj >= ci) if backward else (cj <= ci)
            strict = (cj > ci) if backward else (cj < ci)
            bt = beta_s[rows, N_GATE + a_idx:N_GATE + a_idx + 1]
            decay = jnp.exp(jnp.where(incl, gc_s[rows, a_idx:a_idx + 1] - gcrow_s[c, a_idx:a_idx + 1, :], NEG_BIG))
            low = jnp.where(strict, kk_s[cl * N_HEADS + h, :CHUNK, :] * bt * decay, 0.0)
            attn_s[a_idx, rows, :] = (kk_s[cl * N_HEADS + h, CHUNK:, :] * decay).astype(BF16)
            d = jnp.where(blk_mask, low, 0.0)
            d_s[b] = d
            off_s[b] = low - d
            rhs_s[b] = kv_s[h, rows, :] * bt * jnp.where(low_half, eg_s[rows, a_idx:a_idx + 1], 1.0)
        for b in range(GDN_CHAINS):
            m1_s[b] = _bmm(d_s[b], d_s[b])
        for b in range(GDN_CHAINS):
            d, d2 = d_s[b], m1_s[b]
            m2_s[b] = _bmm(d2, d2)
            p_s[b] = eye - d + d2 - _bmm(d, d2)
        for b in range(GDN_CHAINS):
            d4, p = m2_s[b], p_s[b]
            m3_s[b] = _bmm(d4, d4)
            p_s[b] = p + _bmm(p, d4)
        for b in range(GDN_CHAINS):
            p = p_s[b]
            p_s[b] = p + _bmm(p, m3_s[b])
        for b in range(GDN_CHAINS):
            p = p_s[b]
            m1_s[b] = _bmm(p, off_s[b])
            px_s[b] = _bmm(p, rhs_s[b])
        for b in range(GDN_CHAINS):
            m2_s[b] = _bmm(m1_s[b], m1_s[b])
        for b in range(GDN_CHAINS):
            n, n2 = m1_s[b], m2_s[b]
            m3_s[b] = eye - n + n2 - _bmm(n, n2)
        for b, (cl, a_idx) in enumerate(chains):
            h = a_idx % N_HEADS
            rows, c = rows_of(cl), grp * GDN_GROUP_CHUNKS + cl
            wu = _bmm(m3_s[b], px_s[b])
            uo_s[a_idx, rows, :] = wu
            wq_s[a_idx, c, :CHUNK, :] = wu[:, :HEAD_DIM].astype(BF16)
            wq_s[a_idx, c, CHUNK:, :] = (q_s[h, rows, :] * eg_s[rows, a_idx:a_idx + 1]).astype(BF16)
            k_t_h = kt_s[c, h * HEAD_DIM:(h + 1) * HEAD_DIM, :]
            kdt_s[a_idx, c] = (k_t_h * ekdrow_s[c, a_idx:a_idx + 1, :]).astype(BF16)
        return carry

    lax.fori_loop(0, n_chunks // GDN_GROUP_CHUNKS, solve_group, 0)

    for i in range(N_GATE):
        st_s[i] = jnp.concatenate([jnp.zeros((HEAD_DIM, HEAD_DIM), F32), s0_ref[0, i]], axis=1)

    def scan_chunk(c, carry):
        def chunk_of(a_idx):
            return (n_chunks - 1 - c) if a_idx >= N_HEADS else c

        for a_idx in range(N_GATE):
            wsqs_s[a_idx] = _mm(wq_s[a_idx, chunk_of(a_idx)], st_s[a_idx].astype(BF16))
        for a_idx in range(N_GATE):
            cc = chunk_of(a_idx)
            rows = pl.ds(pl.multiple_of(cc * CHUNK, CHUNK), CHUNK)
            v_new = (uo_s[a_idx, rows, :] - wsqs_s[a_idx, :CHUNK, :]).astype(BF16)
            uo_s[a_idx, rows, :] = wsqs_s[a_idx, CHUNK:, :] + _mm(attn_s[a_idx, rows, :], v_new)
            st_s[a_idx] = st_s[a_idx] * cdec_s[cc, a_idx:a_idx + 1, :] + _mm(kdt_s[a_idx, cc], v_new)
        return carry

    lax.fori_loop(0, n_chunks, scan_chunk, 0)
    for i in range(N_GATE):
        sfin_ref[0, i] = st_s[i, :, HEAD_DIM:]

    o = jnp.concatenate([(uo_s[h] + uo_s[N_HEADS + h])[:, HEAD_DIM:] for h in range(N_HEADS)], axis=1)
    ms = _mm(o * o, head_sum, HI) * (1.0 / HEAD_DIM)
    o_ref[...] = o * lax.rsqrt(ms + EPS) * nw_ref[...] * _silu(a[:, 3 * W_GROUP:])


def _gdn(a_all, gate_all, conv_w, a_log, dt_bias, norm_w, s0, t, n_seq, first_block, into=None):
    small = lambda: pl.BlockSpec((1, LANES), lambda i: (0, 0))
    n_chunks = t // CHUNK
    wide = 2 * HEAD_DIM
    scratch = [pltpu.VMEM((N_HEADS, t, HEAD_DIM), F32),
               pltpu.VMEM((N_HEADS, t, wide), F32),
               pltpu.VMEM((n_chunks, W_GROUP, CHUNK), F32),
               pltpu.VMEM((t, LANES), F32), pltpu.VMEM((t, LANES), F32), pltpu.VMEM((t, LANES), F32),
               pltpu.VMEM((n_chunks, N_GATE, CHUNK), F32), pltpu.VMEM((n_chunks, N_GATE, CHUNK), F32),
               pltpu.VMEM((n_chunks, N_GATE, wide), F32),
               pltpu.VMEM((N_GATE, t, wide), F32),
               pltpu.VMEM((N_GATE, n_chunks, 2 * CHUNK, HEAD_DIM), BF16),
               pltpu.VMEM((N_GATE, t, CHUNK), BF16),
               pltpu.VMEM((N_GATE, n_chunks, HEAD_DIM, CHUNK), BF16),
               pltpu.VMEM((N_GATE, HEAD_DIM, wide), F32),
               pltpu.VMEM((N_GATE, 2 * CHUNK, wide), F32),
               pltpu.VMEM((GDN_PAIRS, 2 * CHUNK, CHUNK), F32)]
    scratch += [pltpu.VMEM((GDN_CHAINS, CHUNK, CHUNK), F32)] * 6
    scratch += [pltpu.VMEM((GDN_CHAINS, CHUNK, wide), F32)] * 2
    return _seq_call(
        functools.partial(_gdn_kernel, t=t), (a_all, gate_all, conv_w, a_log, dt_bias, norm_w, s0),
        in_specs=[pl.BlockSpec((t, 4 * W_GROUP), lambda i: (i + first_block, 0)),
                  pl.BlockSpec((t, LANES), lambda i: (i + first_block, 0)),
                  pl.BlockSpec((SUBLANES, 3 * W_GROUP), lambda i: (0, 0)),
                  small(), small(),
                  pl.BlockSpec((1, W_GROUP), lambda i: (0, 0)),
                  pl.BlockSpec((1, N_GATE, HEAD_DIM, HEAD_DIM), lambda i: (i, 0, 0, 0))],
        out_specs=[pl.BlockSpec((t, W_GROUP), lambda i: (i + first_block, 0)),
                   pl.BlockSpec((1, N_GATE, HEAD_DIM, HEAD_DIM), lambda i: (i, 0, 0, 0))],
        out_shape=[jax.ShapeDtypeStruct((N_TOK, W_GROUP), F32),
                   jax.ShapeDtypeStruct((n_seq, N_GATE, HEAD_DIM, HEAD_DIM), F32)],
        scratch_shapes=scratch,
        into=into, grid=(n_seq,), compiler_params=_cparams("arbitrary"), name="gdn")


def _swap16(x):
    width = x.shape[-1]
    first = (_iota(x.shape, 1) // 16) % 2 == 0
    return jnp.where(first, pltpu.roll(x, width - 16, 1), pltpu.roll(x, 16, 1))


def _block_diag_heads(s0_ref, first):
    zero = jnp.zeros((HEAD_DIM, HEAD_DIM), F32)
    return jnp.concatenate(
        [jnp.concatenate([s0_ref[0, first + h] if j == h else zero for j in range(N_HEADS)], axis=1)
         for h in range(N_HEADS)], axis=0)


def _ret_kernel(r_ref, lg_ref, s0_ref, cos_ref, sin_ref, o_ref, sfin_ref, *, t, latent):
    r = r_ref[...]
    q = r[:, :W_GROUP]
    k = r[:, W_GROUP:2 * W_GROUP]
    v = r[:, 2 * W_GROUP:3 * W_GROUP]
    if latent:
        q = q * cos_ref[...] + _swap16(q) * sin_ref[...]
        k = k * cos_ref[...] + _swap16(k) * sin_ref[...]
    k = k * (HEAD_DIM ** -0.5)
    lg = -_softplus(-lg_ref[...])
    lgf, lgb = lg[0:1, :], lg[1:2, :]
    head = _iota((1, W_GROUP), 1) // HEAD_DIM
    head_sum = _head_sum_matrix()
    pos = _iota((t, 1), 0).astype(F32)
    q_b = q.astype(BF16)
    kt_b = k.T.astype(BF16)
    v_heads = [jnp.where(head == h, v, 0.0).astype(BF16) for h in range(N_HEADS)]
    if latent:
        s0f = _block_diag_heads(s0_ref, 0)
        s0b = _block_diag_heads(s0_ref, N_HEADS)
    for qt in range(t // Q_TILE):
        rows = slice(qt * Q_TILE, (qt + 1) * Q_TILE)
        diff = (_iota((Q_TILE, t), 0) + qt * Q_TILE - _iota((Q_TILE, t), 1)).astype(F32)
        o = jnp.zeros((Q_TILE, W_GROUP), F32)
        for h in range(N_HEADS):
            lgf_h = lgf[:, h * HEAD_DIM:h * HEAD_DIM + 1]
            lgb_h = lgb[:, h * HEAD_DIM:h * HEAD_DIM + 1]
            dmat = (jnp.exp(jnp.where(diff >= 0, diff * lgf_h, NEG_BIG))
                    + jnp.exp(jnp.where(diff <= 0, -diff * lgb_h, NEG_BIG)))
            s = _mm(jnp.where(head == h, q_b[rows], 0.0), kt_b) * dmat
            o = o + _mm(s.astype(BF16), v_heads[h])
        if latent:
            p = pos[rows]
            o = o + jnp.exp((p + 1.0) * lgf) * _bmm(q_b[rows], s0f) + jnp.exp((t - p) * lgb) * _bmm(q_b[rows], s0b)
        oc = o - _mm(o, head_sum, HI) * (1.0 / HEAD_DIM)
        on = oc * lax.rsqrt(_mm(oc * oc, head_sum, HI) * (1.0 / HEAD_DIM) + EPS)
        o_ref[rows, :] = on * _silu(r[rows, 3 * W_GROUP:])
    v_b = v.astype(BF16)
    sf = _mm_tn((k * jnp.exp((t - 1.0 - pos) * lgf)).astype(BF16), v_b)
    sb = _mm_tn((k * jnp.exp(pos * lgb)).astype(BF16), v_b)
    for h in range(N_HEADS):
        hs = slice(h * HEAD_DIM, (h + 1) * HEAD_DIM)
        sf_h, sb_h = sf[hs, hs], sb[hs, hs]
        if latent:
            sf_h = sf_h + jnp.exp(t * lgf[:, h * HEAD_DIM:h * HEAD_DIM + 1]) * s0_ref[0, h]
            sb_h = sb_h + jnp.exp(t * lgb[:, h * HEAD_DIM:h * HEAD_DIM + 1]) * s0_ref[0, N_HEADS + h]
        sfin_ref[0, h] = sf_h
        sfin_ref[0, N_HEADS + h] = sb_h


def _ret(r_all, logit, s0, cos, sin, t, n_seq, first_block, latent, into=None):
    return _seq_call(
        functools.partial(_ret_kernel, t=t, latent=latent), (r_all, logit, s0, cos, sin),
        in_specs=[pl.BlockSpec((t, 4 * W_GROUP), lambda i: (i + first_block, 0)),
                  pl.BlockSpec((2, W_GROUP), lambda i: (0, 0)),
                  pl.BlockSpec((1, N_GATE, HEAD_DIM, HEAD_DIM), lambda i: (i, 0, 0, 0)),
                  pl.BlockSpec((t, W_GROUP), lambda i: (0, 0)),
                  pl.BlockSpec((t, W_GROUP), lambda i: (0, 0))],
        out_specs=[pl.BlockSpec((t, W_GROUP), lambda i: (i + first_block, 0)),
                   pl.BlockSpec((1, N_GATE, HEAD_DIM, HEAD_DIM), lambda i: (i, 0, 0, 0))],
        out_shape=[jax.ShapeDtypeStruct((N_TOK, W_GROUP), F32),
                   jax.ShapeDtypeStruct((n_seq, N_GATE, HEAD_DIM, HEAD_DIM), F32)],
        into=into, grid=(n_seq,), compiler_params=_cparams("arbitrary"), name="ret")


def _rope_tables(t):
    pos = np.arange(t)
    row = (pos // GRID_W).astype(np.float32)
    col = (pos % GRID_W).astype(np.float32)
    nf = HEAD_DIM // 4
    inv_freq = jnp.power(ROPE_BASE, -jnp.arange(nf, dtype=F32) / nf)
    ang_r = jnp.asarray(row)[:, None] * inv_freq[None, :]
    ang_c = jnp.asarray(col)[:, None] * inv_freq[None, :]
    cos = jnp.concatenate([jnp.cos(ang_r)] * 2 + [jnp.cos(ang_c)] * 2, axis=1)
    sin = jnp.concatenate([-jnp.sin(ang_r), jnp.sin(ang_r), -jnp.sin(ang_c), jnp.sin(ang_c)], axis=1)
    return jnp.tile(cos, (1, N_HEADS)), jnp.tile(sin, (1, N_HEADS))


def _head_rms(x, w, head_sum):
    return x * lax.rsqrt(_mm(x * x, head_sum, HI) * (1.0 / HEAD_DIM) + EPS) * w


def _ctx_attn_kernel(n_ref, qw_ref, kw_ref, o_ref, k_out, v_out, *, n_fill):
    n = n_ref[...]
    head_sum = _head_sum_matrix()
    head = _iota((1, W_GROUP), 1) // HEAD_DIM
    q = (_head_rms(n[:, :W_GROUP], qw_ref[...], head_sum) * (HEAD_DIM ** -0.5)).astype(BF16)
    k = _head_rms(n[:, W_GROUP:2 * W_GROUP], kw_ref[...], head_sum)
    v = n[:, 2 * W_GROUP:]
    k_t = k.T
    k_out[0, 0] = k_t
    v_out[0, 0] = v.T
    for later in range(1, 1 + n_fill):
        k_out[0, later] = jnp.zeros((W_GROUP, SEQ), F32)
        v_out[0, later] = jnp.zeros((W_GROUP, SEQ), F32)
    kt_b = k_t.astype(BF16)
    o = jnp.zeros((SEQ, W_GROUP), F32)
    for h in range(N_HEADS):
        s = _mm(jnp.where(head == h, q, 0.0), kt_b)
        p = jnp.exp(s - jnp.max(s, axis=-1, keepdims=True))
        o = o + _mm(p.astype(BF16), jnp.where(head == h, v, 0.0).astype(BF16)) / jnp.sum(p, axis=-1, keepdims=True)
    o_ref[...] = o


def _ctx_attn(n_all, qw, kw, into, layer, caches):
    slots = DEPTH if caches is None else 1
    cache_spec = pl.BlockSpec((1, slots, W_GROUP, SEQ), lambda i: (i, layer if caches is not None else 0, 0, 0))
    cache_shape = jax.ShapeDtypeStruct((BATCH, DEPTH, W_GROUP, SEQ), F32)
    return _seq_call(
        functools.partial(_ctx_attn_kernel, n_fill=slots - 1), (n_all, qw, kw),
        in_specs=[pl.BlockSpec((SEQ, 3 * W_GROUP), lambda i: (i, 0)),
                  pl.BlockSpec((1, W_GROUP), lambda i: (0, 0)),
                  pl.BlockSpec((1, W_GROUP), lambda i: (0, 0))],
        out_specs=[pl.BlockSpec((SEQ, W_GROUP), lambda i: (i, 0)), cache_spec, cache_spec],
        out_shape=[jax.ShapeDtypeStruct((N_TOK, W_GROUP), F32), cache_shape, cache_shape],
        into=into, more_into=() if caches is None else ((caches[0], 1), (caches[1], 2)),
        grid=(BATCH,), compiler_params=_cparams("arbitrary"), name="ctx_attn")


def _nat_kernel(n_ref, ck_ref, cv_ref, bias_ref, qw_ref, kw_ref, o_ref, q_s, kt_s, ckt_s):
    h = pl.program_id(1)

    @pl.when(h == 0)
    def _():
        head_sum = _head_sum_matrix()
        n = n_ref[...]
        q_s[...] = (_head_rms(n[:, :W_GROUP], qw_ref[...], head_sum) * (HEAD_DIM ** -0.5)).astype(BF16)
        kt_s[...] = _head_rms(n[:, W_GROUP:2 * W_GROUP], kw_ref[...], head_sum).T.astype(BF16)
        ckt_s[...] = ck_ref[0].T.astype(BF16)
        o_ref[...] = jnp.zeros_like(o_ref)

    mine = _iota((1, W_GROUP), 1) // HEAD_DIM == h
    v = jnp.where(mine, n_ref[:, 2 * W_GROUP:], 0.0).astype(BF16)
    cv = jnp.where(mine, cv_ref[0], 0.0).astype(BF16)
    for qt in range(DEC_SEQ // Q_TILE):
        rows = slice(qt * Q_TILE, (qt + 1) * Q_TILE)
        q = jnp.where(mine, q_s[rows, :], 0.0)
        grid_rows = range(qt * Q_TILE // GRID_W, (qt + 1) * Q_TILE // GRID_W)
        s_loc = _mm(q, kt_s[...]) + jnp.concatenate([_nat_bias_strip(bias_ref, row) for row in grid_rows], axis=0)
        s_ctx = _mm(q, ckt_s[...])
        m = jnp.maximum(jnp.max(s_loc, axis=-1, keepdims=True), jnp.max(s_ctx, axis=-1, keepdims=True))
        p_loc = jnp.exp(s_loc - m)
        p_ctx = jnp.exp(s_ctx - m)
        den = jnp.sum(p_loc, axis=-1, keepdims=True) + jnp.sum(p_ctx, axis=-1, keepdims=True)
        o_ref[rows, :] += (_mm(p_loc.astype(BF16), v) + _mm(p_ctx.astype(BF16), cv)) / den


def _nat(n_all, ck, cv, bias, qw, kw, into):
    first_block = N_CTX // DEC_SEQ
    return _seq_call(
        _nat_kernel, (n_all, ck, cv, bias, qw, kw),
        in_specs=[pl.BlockSpec((DEC_SEQ, 3 * W_GROUP), lambda b, h: (b + first_block, 0)),
                  pl.BlockSpec((1, PAST_LEN, W_GROUP), lambda b, h: (b, 0, 0)),
                  pl.BlockSpec((1, PAST_LEN, W_GROUP), lambda b, h: (b, 0, 0)),
                  pl.BlockSpec((1, 3 * N_ROW_OFF - 1, GRID_W, 2 * GRID_W), lambda b, h: (h, 0, 0, 0)),
                  pl.BlockSpec((1, W_GROUP), lambda b, h: (0, 0)),
                  pl.BlockSpec((1, W_GROUP), lambda b, h: (0, 0))],
        out_specs=pl.BlockSpec((DEC_SEQ, W_GROUP), lambda b, h: (b + first_block, 0)),
        out_shape=jax.ShapeDtypeStruct((N_TOK, W_GROUP), F32),
        scratch_shapes=[pltpu.VMEM((DEC_SEQ, W_GROUP), BF16), pltpu.VMEM((W_GROUP, DEC_SEQ), BF16),
                        pltpu.VMEM((W_GROUP, PAST_LEN), BF16)],
        into=into, grid=(DEC_BATCH, N_HEADS), compiler_params=_cparams("arbitrary", "arbitrary"), name="nat")


N_ROW_OFF = 2 * WIN_ROWS - 1
NAT_ROWS = DEC_SEQ // GRID_W
NAT_KH = min(WIN_ROWS, NAT_ROWS)
NAT_PAIR, NAT_LOW, NAT_HIGH = 0, N_ROW_OFF - 1, 2 * N_ROW_OFF - 1


def _nat_tables(rpb):
    c = np.arange(GRID_W)
    c0 = np.clip(c - WIN_COLS // 2, 0, GRID_W - WIN_COLS)
    col_in = (c[None, :] >= c0[:, None]) & (c[None, :] < c0[:, None] + WIN_COLS)
    col_idx = np.clip(c[None, :] - c[:, None], -(WIN_COLS - 1), WIN_COLS - 1) + WIN_COLS - 1
    col_hot = (col_idx[..., None] == np.arange(2 * WIN_COLS - 1)).astype(np.float32)
    tz = jnp.where(col_in, jnp.einsum('hab,qkb->haqk', rpb.astype(F32), col_hot, precision=HI), NEG_BIG)
    neg = jnp.full_like(tz, NEG_BIG)
    return jnp.concatenate([jnp.concatenate([tz[:, :-1], tz[:, 1:]], axis=-1),
                            jnp.concatenate([tz, neg], axis=-1), jnp.concatenate([neg, tz], axis=-1)], axis=1)


def _nat_bias_strip(tab_ref, row):
    first_key = min(max(row - NAT_KH // 2, 0), NAT_ROWS - NAT_KH)
    off = first_key - row + WIN_ROWS - 1
    tiles = {}
    done, key = 0, first_key
    if key % 2 == 1:
        tiles[key // 2] = tab_ref[0, NAT_HIGH + off]
        done, key = 1, key + 1
    while done + 1 < NAT_KH:
        tiles[key // 2] = tab_ref[0, NAT_PAIR + off + done]
        done, key = done + 2, key + 2
    if done < NAT_KH:
        tiles[key // 2] = tab_ref[0, NAT_LOW + off + done]
    outside = jnp.full((GRID_W, 2 * GRID_W), NEG_BIG, F32)
    return jnp.concatenate([tiles.get(i, outside) for i in range(NAT_ROWS // 2)], axis=1)


GSEL_LANE = N_EXPERTS


def _pack_router(we, be, wg, bg):
    pad = LANES - N_EXPERTS - N_GROUPS
    w = jnp.concatenate([we, wg, jnp.zeros((D_MODEL, pad), F32)], axis=1)
    b = jnp.concatenate([be, bg, jnp.zeros((pad,), F32)]).reshape(1, LANES)
    hi = w.astype(BF16)
    lo = (w - hi.astype(F32)).astype(BF16)
    return jnp.concatenate([hi, lo], axis=1), b


def _lane_min_where(mask, lane):
    return jnp.min(jnp.where(mask, lane, LANES), axis=-1, keepdims=True)


def _outproj_kernel(xc_ref, xl_ref, m0, m1, m2, m3, mod_ref, nw_ref, w_ref, rw_ref, rb_ref, x_out, hf_out, route_out):
    acc = None
    for i, m_ref in enumerate((m0, m1, m2, m3)):
        part = _mm(m_ref[...].astype(BF16), w_ref[i * W_GROUP:(i + 1) * W_GROUP, :])
        acc = part if acc is None else acc + part
    x = _token_tile(pl.program_id(0), xc_ref, xl_ref) + mod_ref[0, 2:3, :] * acc
    x_out[...] = x
    y = x * lax.rsqrt(jnp.mean(x * x, axis=-1, keepdims=True) + EPS) * nw_ref[...]
    hf = y * (1.0 + mod_ref[0, 4:5, :]) + mod_ref[0, 3:4, :]
    hf_hi = hf.astype(BF16)
    hf_out[...] = hf_hi

    hf_lo = (hf - hf_hi.astype(F32)).astype(BF16)
    both = _mm(hf_hi, rw_ref[...])
    logits = both[:, :LANES] + both[:, LANES:] + _mm(hf_lo, rw_ref[:, :LANES]) + rb_ref[...]
    lane = _iota(logits.shape, 1)
    is_g = (lane >= N_EXPERTS) & (lane < N_EXPERTS + N_GROUPS)
    gl = jnp.where(is_g, logits, NEG_BIG)
    ge = jnp.exp(gl - jnp.max(gl, axis=-1, keepdims=True))
    gp = jnp.where(is_g, ge / jnp.sum(ge, axis=-1, keepdims=True), -1.0)
    gw = jnp.max(gp, axis=-1, keepdims=True)
    gsel = _lane_min_where(gp == gw, lane) - N_EXPERTS
    in_grp = (lane // EXPERTS_PER_GROUP == gsel) & (lane < N_EXPERTS)
    el = jnp.where(in_grp, logits, NEG_BIG)
    ee = jnp.exp(el - jnp.max(el, axis=-1, keepdims=True))
    ep = jnp.where(in_grp, ee / jnp.sum(ee, axis=-1, keepdims=True), -1.0)
    t1 = jnp.max(ep, axis=-1, keepdims=True)
    i1 = _lane_min_where(ep == t1, lane)
    ep2 = jnp.where(lane == i1, -1.0, ep)
    t2 = jnp.max(ep2, axis=-1, keepdims=True)
    i2 = _lane_min_where(ep2 == t2, lane)
    tsum = t1 + t2
    combine = jnp.where(lane == i1, gw * (t1 / tsum), 0.0) + jnp.where(lane == i2, gw * (t2 / tsum), 0.0)
    route_out[...] = jnp.where(lane == GSEL_LANE, gsel.astype(F32), combine)


def _outproj(x_ctx, x_lat, split, mixed, mod, norm_w, w_out, rw, rb):
    tile = lambda w: pl.BlockSpec((TOK_TILE, w), lambda i: (i, 0))
    whole = lambda a: pl.BlockSpec(a.shape, lambda i: (0,) * a.ndim)
    return pl.pallas_call(
        _outproj_kernel,
        grid=(N_TOK // TOK_TILE,),
        in_specs=_token_specs(split) + [tile(W_GROUP)] * 4
                 + [pl.BlockSpec((1, SUBLANES, D_MODEL), lambda i: (_cond_of_tile(i), 0, 0)),
                    pl.BlockSpec((1, D_MODEL), lambda i: (0, 0)), whole(w_out), whole(rw), whole(rb)],
        out_specs=[tile(D_MODEL), tile(D_MODEL), tile(LANES)],
        out_shape=[jax.ShapeDtypeStruct((N_TOK, D_MODEL), F32), jax.ShapeDtypeStruct((N_TOK, D_MODEL), BF16),
                   jax.ShapeDtypeStruct((N_TOK, LANES), F32)],
        compiler_params=_cparams("arbitrary"),
        name="outproj",
    )(x_ctx, x_lat, *mixed, mod, norm_w.reshape(1, D_MODEL), w_out, rw, rb)


SEG_BLK = 16
LOCAL_ROWS = TOK_TILE + N_GROUPS * SEG_BLK
N_TOK_TILES = N_TOK // TOK_TILE
MOE_ROWS = -(-(N_TOK + N_TOK_TILES * N_GROUPS * (SEG_BLK - 1) + N_GROUPS * (MOE_TILE - 1)) // MOE_TILE) * MOE_TILE


def _moe_tables(gsel):
    groups = jnp.arange(N_GROUPS, dtype=jnp.int32)
    onehot = (gsel.reshape(N_TOK_TILES, TOK_TILE, 1) == groups).astype(jnp.int32)
    rank = jnp.cumsum(onehot, axis=1) - onehot
    nblk = (jnp.sum(onehot, axis=1) + SEG_BLK - 1) // SEG_BLK
    loc_blk = jnp.cumsum(nblk, axis=1) - nblk
    blocks_per_tile = MOE_TILE // SEG_BLK
    grp_tiles = (jnp.sum(nblk, axis=0) + blocks_per_tile - 1) // blocks_per_tile
    grp_tile_start = jnp.cumsum(grp_tiles) - grp_tiles
    dst_blk = grp_tile_start[None, :] * blocks_per_tile + jnp.cumsum(nblk, axis=0) - nblk
    local_pos = jnp.sum(onehot * (loc_blk[:, None, :] * SEG_BLK + rank), axis=2)
    tile_idx = jnp.arange(MOE_ROWS // MOE_TILE, dtype=jnp.int32)
    tile_group = jnp.clip(jnp.sum(tile_idx[:, None] >= grp_tile_start[None, :], axis=1) - 1, 0, N_GROUPS - 1)
    tile_valid = tile_idx < jnp.sum(grp_tiles)
    flat = lambda a: a.reshape(-1).astype(jnp.int32)
    return local_pos.astype(jnp.int32), flat(nblk), flat(loc_blk), flat(dst_blk), flat(tile_group), flat(tile_valid)


def _segment_copies(t, nblk, loc_blk, dst_blk, make_copies, action):
    for g in range(N_GROUPS):
        k = t * N_GROUPS + g

        @pl.loop(0, nblk[k])
        def _(b):
            local = pl.multiple_of((loc_blk[k] + b) * SEG_BLK, SEG_BLK)
            sorted_row = pl.multiple_of((dst_blk[k] + b) * SEG_BLK, SEG_BLK)
            for cp in make_copies(local, sorted_row):
                action(cp)


def _dispatch_kernel(nblk, loc_blk, dst_blk, hf_ref, rt_ref, lp_ref, xs_in, rs_in, xs_hbm, rs_hbm, xbuf, rbuf, sem):
    t = pl.program_id(0)
    slot = t % 2
    onehot = _iota((LOCAL_ROWS, TOK_TILE), 0) == lp_ref[0]
    xbuf[slot] = _mm(onehot.astype(BF16), hf_ref[...]).astype(BF16)
    rbuf[slot] = _mm(onehot.astype(F32), rt_ref[...], HI)

    def copies_of(s):
        def copies(local, sorted_row):
            return (pltpu.make_async_copy(xbuf.at[s, pl.ds(local, SEG_BLK)], xs_hbm.at[pl.ds(sorted_row, SEG_BLK)], sem.at[s]),
                    pltpu.make_async_copy(rbuf.at[s, pl.ds(local, SEG_BLK)], rs_hbm.at[pl.ds(sorted_row, SEG_BLK)], sem.at[s]))
        return copies

    @pl.when(t > 0)
    def _():
        _segment_copies(t - 1, nblk, loc_blk, dst_blk, copies_of(1 - slot), lambda cp: cp.wait())

    _segment_copies(t, nblk, loc_blk, dst_blk, copies_of(slot), lambda cp: cp.start())

    @pl.when(t == N_TOK_TILES - 1)
    def _():
        _segment_copies(t, nblk, loc_blk, dst_blk, copies_of(slot), lambda cp: cp.wait())


def _dispatch(hf, route, local_pos, nblk, loc_blk, dst_blk):
    grid_spec = pltpu.PrefetchScalarGridSpec(
        num_scalar_prefetch=3,
        grid=(N_TOK_TILES,),
        in_specs=[pl.BlockSpec((TOK_TILE, D_MODEL), lambda t, *_: (t, 0)),
                  pl.BlockSpec((TOK_TILE, LANES), lambda t, *_: (t, 0)),
                  pl.BlockSpec((1, 1, TOK_TILE), lambda t, *_: (t, 0, 0)),
                  pl.BlockSpec(memory_space=pl.ANY), pl.BlockSpec(memory_space=pl.ANY)],
        out_specs=[pl.BlockSpec(memory_space=pl.ANY), pl.BlockSpec(memory_space=pl.ANY)],
        scratch_shapes=[pltpu.VMEM((2, LOCAL_ROWS, D_MODEL), BF16), pltpu.VMEM((2, LOCAL_ROWS, LANES), F32),
                        pltpu.SemaphoreType.DMA((2,))],
    )
    return pl.pallas_call(
        _dispatch_kernel,
        grid_spec=grid_spec,
        out_shape=[jax.ShapeDtypeStruct((MOE_ROWS, D_MODEL), BF16), jax.ShapeDtypeStruct((MOE_ROWS, LANES), F32)],
        input_output_aliases={6: 0, 7: 1},
        compiler_params=_cparams("arbitrary"),
        name="dispatch",
    )(nblk, loc_blk, dst_blk, hf, route, local_pos.reshape(N_TOK_TILES, 1, TOK_TILE),
      jnp.zeros((MOE_ROWS, D_MODEL), BF16), jnp.zeros((MOE_ROWS, LANES), F32))


def _moe_kernel(tile_group, tile_valid, x_ref, r_ref, wg_hbm, wu_hbm, wd_hbm, y_ref,
                wg_b, wu_b, wd_b, stage_g, stage_u, stage_d, sem, *, layer):
    i = pl.program_id(0)
    g = tile_group[i]
    group_row = layer * N_GROUPS + g
    new_group = (i == 0) | (g != tile_group[jnp.maximum(i - 1, 0)])
    valid = tile_valid[i] > 0

    def weight_copies(e):
        slot = e % 2
        return (pltpu.make_async_copy(wg_hbm.at[group_row, e], stage_g.at[slot], sem.at[slot]),
                pltpu.make_async_copy(wu_hbm.at[group_row, e], stage_u.at[slot], sem.at[slot]),
                pltpu.make_async_copy(wd_hbm.at[group_row, e], stage_d.at[slot], sem.at[slot]))

    def run(load_weights):
        x = x_ref[...]
        route = r_ref[...]
        lane = _iota(route.shape, 1)
        acc = jnp.zeros((MOE_TILE, D_MODEL), F32)
        if load_weights:
            for e in range(2):
                for cp in weight_copies(e):
                    cp.start()
        for e in range(EXPERTS_PER_GROUP):
            if load_weights:
                for cp in weight_copies(e):
                    cp.wait()
                wg_b[e] = stage_g[e % 2].astype(BF16)
                wu_b[e] = stage_u[e % 2].astype(BF16)
                wd_b[e] = stage_d[e % 2].astype(BF16)
                if e + 2 < EXPERTS_PER_GROUP:
                    for cp in weight_copies(e + 2):
                        cp.start()
            cw = jnp.sum(jnp.where(lane == g * EXPERTS_PER_GROUP + e, route, 0.0), axis=-1, keepdims=True)
            act = _silu(_mm(x, wg_b[e])) * _mm(x, wu_b[e]) * cw
            acc = acc + _mm(act.astype(BF16), wd_b[e])
        y_ref[...] = acc

    @pl.when(valid & new_group)
    def _():
        run(True)

    @pl.when(valid & jnp.logical_not(new_group))
    def _():
        run(False)

    @pl.when(jnp.logical_not(valid))
    def _():
        y_ref[...] = jnp.zeros_like(y_ref)


def _moe(xs, rs, tile_group, tile_valid, wg, wu, wd, layer):
    any_spec = pl.BlockSpec(memory_space=pl.ANY)
    grid_spec = pltpu.PrefetchScalarGridSpec(
        num_scalar_prefetch=2,
        grid=(MOE_ROWS // MOE_TILE,),
        in_specs=[pl.BlockSpec((MOE_TILE, D_MODEL), lambda i, tg, tv: (i, 0)),
                  pl.BlockSpec((MOE_TILE, LANES), lambda i, tg, tv: (i, 0)),
                  any_spec, any_spec, any_spec],
        out_specs=pl.BlockSpec((MOE_TILE, D_MODEL), lambda i, tg, tv: (i, 0)),
        scratch_shapes=[pltpu.VMEM((EXPERTS_PER_GROUP, D_MODEL, EXPERT_FF), BF16),
                        pltpu.VMEM((EXPERTS_PER_GROUP, D_MODEL, EXPERT_FF), BF16),
                        pltpu.VMEM((EXPERTS_PER_GROUP, EXPERT_FF, D_MODEL), BF16),
                        pltpu.VMEM((2, D_MODEL, EXPERT_FF), F32), pltpu.VMEM((2, D_MODEL, EXPERT_FF), F32),
                        pltpu.VMEM((2, EXPERT_FF, D_MODEL), F32), pltpu.SemaphoreType.DMA((2,))],
    )
    return pl.pallas_call(
        functools.partial(_moe_kernel, layer=layer),
        grid_spec=grid_spec,
        out_shape=jax.ShapeDtypeStruct((MOE_ROWS, D_MODEL), F32),
        compiler_params=_cparams("arbitrary"),
        name="moe",
    )(tile_group, tile_valid, xs, rs, wg, wu, wd)


def _combine_kernel(nblk, loc_blk, dst_blk, x_ref, lp_ref, mod_ref, ys_hbm, *refs, split):
    o_refs, ybuf, sem = refs[:-2], refs[-2], refs[-1]
    t = pl.program_id(0)
    slot = t % 2

    def fetch(tile, s):
        def copies(local, sorted_row):
            return (pltpu.make_async_copy(ys_hbm.at[pl.ds(sorted_row, SEG_BLK)], ybuf.at[s, pl.ds(local, SEG_BLK)], sem.at[s]),)
        ybuf[s] = jnp.zeros((LOCAL_ROWS, D_MODEL), F32)
        _segment_copies(tile, nblk, loc_blk, dst_blk, copies, lambda cp: cp.start())

    @pl.when(t == 0)
    def _():
        fetch(0, 0)

    @pl.when(t + 1 < N_TOK_TILES)
    def _():
        fetch(t + 1, 1 - slot)

    def copies_now(local, sorted_row):
        return (pltpu.make_async_copy(ys_hbm.at[pl.ds(sorted_row, SEG_BLK)], ybuf.at[slot, pl.ds(local, SEG_BLK)], sem.at[slot]),)

    _segment_copies(t, nblk, loc_blk, dst_blk, copies_now, lambda cp: cp.wait())

    onehot = (_iota((TOK_TILE, LOCAL_ROWS), 1) == lp_ref[...]).astype(BF16)
    ys = ybuf[slot]
    hi = ys.astype(BF16)
    lo = (ys - hi.astype(F32)).astype(BF16)
    y = _mm(onehot, hi) + _mm(onehot, lo)
    out = x_ref[...] + mod_ref[0, 5:6, :] * y
    if split:
        @pl.when(t < N_CTX_TILES)
        def _():
            o_refs[0][...] = out

        @pl.when(t >= N_CTX_TILES)
        def _():
            o_refs[1][...] = out
    else:
        o_refs[0][...] = out


def _combine(x, ys, mod, local_pos, nblk, loc_blk, dst_blk, split):
    tile = pl.BlockSpec((TOK_TILE, D_MODEL), lambda t, *_: (t, 0))
    if split:
        out_specs = _token_specs(True)
        out_shape = [jax.ShapeDtypeStruct((N_CTX, D_MODEL), F32), jax.ShapeDtypeStruct((N_LAT, D_MODEL), F32)]
    else:
        out_specs, out_shape = [tile], [jax.ShapeDtypeStruct((N_TOK, D_MODEL), F32)]
    grid_spec = pltpu.PrefetchScalarGridSpec(
        num_scalar_prefetch=3,
        grid=(N_TOK_TILES,),
        in_specs=[tile, pl.BlockSpec((TOK_TILE, 1), lambda t, *_: (t, 0)),
                  pl.BlockSpec((1, SUBLANES, D_MODEL), lambda t, *_: (_cond_of_tile(t), 0, 0)),
                  pl.BlockSpec(memory_space=pl.ANY)],
        out_specs=out_specs,
        scratch_shapes=[pltpu.VMEM((2, LOCAL_ROWS, D_MODEL), F32), pltpu.SemaphoreType.DMA((2,))],
    )
    return pl.pallas_call(
        functools.partial(_combine_kernel, split=split),
        grid_spec=grid_spec,
        out_shape=out_shape,
        compiler_params=_cparams("arbitrary"),
        name="combine",
    )(nblk, loc_blk, dst_blk, x, local_pos.reshape(N_TOK, 1), mod, ys)


def _lane_row(v):
    v = v.reshape(-1).astype(F32)
    return jnp.concatenate([v, jnp.zeros((LANES - v.shape[0],), F32)]).reshape(1, LANES)


def _pad_rows(w):
    return jnp.concatenate([w, jnp.zeros((SUBLANES - w.shape[0], w.shape[1]), w.dtype)], axis=0)


def kernel(x_prompt, x_sample, state_gdn, state_ret, cache_nat_k, cache_nat_v, c, c_ctx, ada_w, ada_b, norm_mix_w, norm_ffn_w, w_in, gdn_conv_w, gdn_a_log, gdn_dt_bias, gdn_norm_w, ret_gamma_logit, nat_q_norm_w, nat_k_norm_w, nat_rpb, sc_conv_w, w_out, router_group_w, router_group_b, router_expert_w, router_expert_b, moe_w_gate, moe_w_up, moe_w_down):
    x_ctx, x_lat, split = x_prompt.reshape(N_CTX, D_MODEL), x_sample.reshape(N_LAT, D_MODEL), True
    cond = jnp.concatenate([c_ctx[None, :], c, jnp.zeros((SUBLANES - 1 - DEC_BATCH, D_MODEL), F32)], axis=0)
    ada = _ada(cond, ada_w, ada_b).reshape(DEPTH, SUBLANES, 6, D_MODEL)
    cos, sin = _rope_tables(DEC_SEQ)
    zero_state = jnp.zeros((BATCH, N_GATE, HEAD_DIM, HEAD_DIM), F32)
    lat_block = N_CTX // DEC_SEQ
    gdn_list, ret_list, caches = [], [], None
    mixed = [jnp.zeros((N_TOK, W_GROUP), F32) for _ in range(4)]
    for l in range(DEPTH):
        mod = jnp.concatenate([ada[l, :1 + DEC_BATCH], jnp.zeros((1 + DEC_BATCH, SUBLANES - 6, D_MODEL), F32)], axis=1)
        a_gdn, a_ret, a_nat, a_sc, a_gate = _inproj(x_ctx, x_lat, split, mod, norm_mix_w[l], w_in, l)

        conv_w = _pad_rows(gdn_conv_w[l])
        a_log, dt_b = _lane_row(gdn_a_log[l]), _lane_row(gdn_dt_bias[l])
        gnw = jnp.tile(gdn_norm_w[l], N_HEADS).reshape(1, W_GROUP)
        o_gdn, s_gdn = _gdn(a_gdn, a_gate, conv_w, a_log, dt_b, gnw, zero_state, SEQ, BATCH, 0, into=mixed[0])
        s0 = state_gdn[:, l].reshape(DEC_BATCH, N_GATE, HEAD_DIM, HEAD_DIM)
        o_gdn, _ = _gdn(a_gdn, a_gate, conv_w, a_log, dt_b, gnw, s0, DEC_SEQ, DEC_BATCH, lat_block, into=o_gdn)

        logit = jnp.repeat(ret_gamma_logit[l].astype(F32), HEAD_DIM, axis=1)
        o_ret, s_ret = _ret(a_ret, logit, zero_state, cos[:SEQ], sin[:SEQ], SEQ, BATCH, 0, False, into=mixed[1])
        s0 = state_ret[:, l].reshape(DEC_BATCH, N_GATE, HEAD_DIM, HEAD_DIM)
        o_ret, _ = _ret(a_ret, logit, s0, cos, sin, DEC_SEQ, DEC_BATCH, lat_block, True, into=o_ret)

        qw = jnp.tile(nat_q_norm_w[l], N_HEADS).reshape(1, W_GROUP)
        kw = jnp.tile(nat_k_norm_w[l], N_HEADS).reshape(1, W_GROUP)
        o_nat, *caches = _ctx_attn(a_nat, qw, kw, mixed[2], l, caches)
        o_nat = _nat(a_nat, cache_nat_k[:, l].reshape(DEC_BATCH, PAST_LEN, W_GROUP),
                     cache_nat_v[:, l].reshape(DEC_BATCH, PAST_LEN, W_GROUP), _nat_tables(nat_rpb[l]), qw, kw, o_nat)

        sc_w = _pad_rows(sc_conv_w[l])
        o_sc = _sconv(a_sc, sc_w, SEQ, BATCH, 0, into=mixed[3])
        o_sc = _sconv(a_sc, sc_w, DEC_SEQ, DEC_BATCH, lat_block, into=o_sc)

        mixed = [o_gdn, o_ret, o_nat, o_sc]
        rw, rb = _pack_router(router_expert_w[l], router_expert_b[l], router_group_w[l], router_group_b[l])
        x_mid, hf, route = _outproj(x_ctx, x_lat, split, mixed, mod, norm_ffn_w[l], w_out[l].astype(BF16), rw, rb)

        local_pos, nblk, loc_blk, dst_blk, tile_group, tile_valid = _moe_tables(route[:, GSEL_LANE].astype(jnp.int32))
        xs, rs = _dispatch(hf, route, local_pos, nblk, loc_blk, dst_blk)
        to_group = lambda w: w.reshape((DEPTH * N_GROUPS, EXPERTS_PER_GROUP) + w.shape[2:])
        ys = _moe(xs, rs, tile_group, tile_valid, to_group(moe_w_gate), to_group(moe_w_up), to_group(moe_w_down), l)
        last = l == DEPTH - 1
        out = _combine(x_mid, ys, mod, local_pos, nblk, loc_blk, dst_blk, split=last)
        x_ctx, x_lat, split = (out[0], out[1], True) if last else (out[0], out[0], False)

        gdn_list.append(s_gdn.reshape(BATCH, 2, N_HEADS, HEAD_DIM, HEAD_DIM))
        ret_list.append(s_ret.reshape(BATCH, 2, N_HEADS, HEAD_DIM, HEAD_DIM))
    new_k, new_v = [a.reshape(BATCH, DEPTH, N_HEADS, HEAD_DIM, SEQ).transpose(0, 1, 4, 2, 3) for a in caches]
    return (x_ctx.reshape(BATCH, SEQ, D_MODEL), x_lat.reshape(DEC_BATCH, DEC_SEQ, D_MODEL),
            jnp.stack(gdn_list, axis=1), jnp.stack(ret_list, axis=1), new_k, new_v)
```

```python
import functools

import numpy as np
import jax
import jax.numpy as jnp
from jax import lax
from jax.experimental import pallas as pl
from jax.experimental.pallas import tpu as pltpu

D_MODEL = 1024
BATCH = 16
SEQ = 256
DEPTH = 2
DEC_BATCH = 2
DEC_SEQ = 1024
PAST_LEN = 256
GRID_W = 64
HEAD_DIM = 64
W_GROUP = D_MODEL // 4
N_HEADS = W_GROUP // HEAD_DIM
CHUNK = 64
WIN_ROWS = 8
WIN_COLS = 16
ROPE_BASE = 10000.0
N_GROUPS = 4
EXPERTS_PER_GROUP = 8
N_EXPERTS = N_GROUPS * EXPERTS_PER_GROUP
EXPERT_FF = 256
GROUP_FF = EXPERTS_PER_GROUP * EXPERT_FF
EPS = 1e-6

N_CTX = BATCH * SEQ
N_LAT = DEC_BATCH * DEC_SEQ
N_TOK = N_CTX + N_LAT
LANES = 128
SUBLANES = 8
TOK_TILE = 512
MOE_TILE = 256
Q_TILE = 256
VMEM_LIMIT = 48 * 1024 * 1024
NEG_BIG = -1e30
N_GATE = 2 * N_HEADS

F32 = jnp.float32
BF16 = jnp.bfloat16
HI = lax.Precision.HIGHEST


def _mm(a, b, prec=None):
    return lax.dot_general(a, b, (((1,), (0,)), ((), ())), precision=prec, preferred_element_type=F32)


def _mm_nt(a, b, prec=None):
    return lax.dot_general(a, b, (((1,), (1,)), ((), ())), precision=prec, preferred_element_type=F32)


def _mm_tn(a, b, prec=None):
    return lax.dot_general(a, b, (((0,), (0,)), ((), ())), precision=prec, preferred_element_type=F32)


def _bmm(a, b):
    return _mm(a.astype(BF16), b.astype(BF16))


def _sigmoid(x):
    return 1.0 / (1.0 + jnp.exp(-x))


def _silu(x):
    return x * _sigmoid(x)


def _softplus(x):
    return jnp.maximum(x, 0.0) + jnp.log(1.0 + jnp.exp(-jnp.abs(x)))


def _iota(shape, dim):
    return lax.broadcasted_iota(jnp.int32, shape, dim)


def _cparams(*sem):
    return pltpu.CompilerParams(dimension_semantics=sem, vmem_limit_bytes=VMEM_LIMIT)


def _cond_of_tile(i):
    n_ctx_tiles = N_CTX // TOK_TILE
    return jnp.where(i < n_ctx_tiles, 0, 1 + (i - n_ctx_tiles) // (DEC_SEQ // TOK_TILE))


def _head_sum_matrix():
    return (_iota((W_GROUP, W_GROUP), 0) // HEAD_DIM == _iota((W_GROUP, W_GROUP), 1) // HEAD_DIM).astype(BF16)


def _head_sums(x, head_sum):
    hi = x.astype(BF16)
    lo = (x - hi.astype(F32)).astype(BF16)
    return _mm(hi, head_sum) + _mm(lo, head_sum)


def _ada_kernel(c_ref, w_ref, b_ref, o_ref):
    o_ref[0] = _mm(_silu(c_ref[...]), w_ref[0], HI) + b_ref[0]


def _ada(cond, ada_w, ada_b):
    tn = 1536
    n_out = 6 * D_MODEL
    return pl.pallas_call(
        _ada_kernel,
        grid=(DEPTH, n_out // tn),
        in_specs=[pl.BlockSpec((SUBLANES, D_MODEL), lambda l, j: (0, 0)),
                  pl.BlockSpec((1, D_MODEL, tn), lambda l, j: (l, 0, j)),
                  pl.BlockSpec((1, 1, tn), lambda l, j: (l, 0, j))],
        out_specs=pl.BlockSpec((1, SUBLANES, tn), lambda l, j: (l, 0, j)),
        out_shape=jax.ShapeDtypeStruct((DEPTH, SUBLANES, n_out), F32),
        compiler_params=_cparams("arbitrary", "arbitrary"),
        name="ada",
    )(cond, ada_w, ada_b.reshape(DEPTH, 1, n_out))


IN_WIDTHS = (4 * W_GROUP, 4 * W_GROUP, 3 * W_GROUP, 3 * W_GROUP, LANES)
IN_PACKED = sum(IN_WIDTHS)


IN_TOTAL = 3 * W_GROUP + W_GROUP + 2 * N_GATE + 4 * W_GROUP + 3 * W_GROUP + 3 * W_GROUP
IN_GATE_SRC = 4 * W_GROUP
IN_SRC = (0, IN_GATE_SRC + 2 * N_GATE, IN_GATE_SRC + 2 * N_GATE + 4 * W_GROUP,
          IN_GATE_SRC + 2 * N_GATE + 7 * W_GROUP)
N_CTX_TILES = N_CTX // TOK_TILE


def _token_specs(split):
    lat0 = 0 if split else N_CTX_TILES
    return [pl.BlockSpec((TOK_TILE, D_MODEL), lambda i, *_: (jnp.minimum(i, N_CTX_TILES - 1), 0)),
            pl.BlockSpec((TOK_TILE, D_MODEL), lambda i, *_: (jnp.maximum(i, N_CTX_TILES) - N_CTX_TILES + lat0, 0))]


def _token_tile(i, ctx_ref, lat_ref):
    return jnp.where(i < N_CTX_TILES, ctx_ref[...], lat_ref[...])


def _inproj_kernel(xc_ref, xl_ref, mod_ref, nw_ref, w_ref, *refs):
    o_refs, w_s = refs[:-1], refs[-1]
    i = pl.program_id(0)

    @pl.when(i == 0)
    def _():
        piece = 256
        off = 0
        for src, width in zip(IN_SRC, IN_WIDTHS[:-1]):
            for c in range(0, width, piece):
                w_s[:, off + c:off + c + piece] = w_ref[0, src + c:src + c + piece, :].T.astype(BF16)
            off += width
        gate = w_ref[0, IN_GATE_SRC:IN_GATE_SRC + LANES, :].T
        w_s[:, off:] = jnp.where(_iota((D_MODEL, LANES), 1) < 2 * N_GATE, gate, 0.0).astype(BF16)

    x = _token_tile(i, xc_ref, xl_ref)
    y = x * lax.rsqrt(jnp.mean(x * x, axis=-1, keepdims=True) + EPS) * nw_ref[...]
    h = (y * (1.0 + mod_ref[0, 1:2, :]) + mod_ref[0, 0:1, :]).astype(BF16)
    off = 0
    for o_ref, width in zip(o_refs, IN_WIDTHS):
        o_ref[...] = _mm(h, w_s[:, off:off + width])
        off += width


def _inproj(x_ctx, x_lat, split, mod, norm_w, w_in, layer):
    return pl.pallas_call(
        _inproj_kernel,
        grid=(N_TOK // TOK_TILE,),
        in_specs=_token_specs(split)
                 + [pl.BlockSpec((1, SUBLANES, D_MODEL), lambda i: (_cond_of_tile(i), 0, 0)),
                    pl.BlockSpec((1, D_MODEL), lambda i: (0, 0)),
                    pl.BlockSpec((1, IN_TOTAL, D_MODEL), lambda i: (layer, 0, 0), pipeline_mode=pl.Buffered(1))],
        out_specs=[pl.BlockSpec((TOK_TILE, w), lambda i: (i, 0)) for w in IN_WIDTHS],
        out_shape=[jax.ShapeDtypeStruct((N_TOK, w), F32) for w in IN_WIDTHS],
        scratch_shapes=[pltpu.VMEM((D_MODEL, IN_PACKED), BF16)],
        compiler_params=_cparams("arbitrary"),
        name="inproj",
    )(x_ctx, x_lat, mod, norm_w.reshape(1, D_MODEL), jnp.swapaxes(w_in, 1, 2))


def _seq_call(kernel_fn, args, in_specs, out_specs, out_shape, into, more_into=(), **kwargs):
    donors = ([] if into is None else [(into, 0)]) + list(more_into)
    n_in = len(args)
    inner = kernel_fn
    kernel_fn = lambda *refs: inner(*refs[:n_in], *refs[n_in + len(donors):])
    aliases = {n_in + j: out_idx for j, (_, out_idx) in enumerate(donors)}
    args = list(args) + [a for a, _ in donors]
    in_specs = list(in_specs) + [pl.BlockSpec(memory_space=pl.ANY)] * len(donors)
    return pl.pallas_call(kernel_fn, in_specs=in_specs, out_specs=out_specs, out_shape=out_shape,
                          input_output_aliases=aliases, **kwargs)(*args)


def _shift_rows(p, t):
    row = _iota(p.shape, 0)
    prev = jnp.where(row == 0, 0.0, pltpu.roll(p, 1, 0))
    nxt = jnp.where(row == t - 1, 0.0, pltpu.roll(p, t - 1, 0))
    return prev, nxt


def _conv3(x, w_ref, t):
    prev, nxt = _shift_rows(x, t)
    return w_ref[0:1, :] * prev + w_ref[1:2, :] * x + w_ref[2:3, :] * nxt


def _sconv_kernel(s_ref, w_ref, o_ref, *, t):
    s = s_ref[...]
    p = s[:, W_GROUP:2 * W_GROUP] * s[:, 2 * W_GROUP:]
    o_ref[...] = s[:, :W_GROUP] * _conv3(p, w_ref, t)


def _sconv(s_all, w, t, n_seq, first_block, into=None):
    return _seq_call(
        functools.partial(_sconv_kernel, t=t), (s_all, w),
        in_specs=[pl.BlockSpec((t, 3 * W_GROUP), lambda i: (i + first_block, 0)),
                  pl.BlockSpec((SUBLANES, W_GROUP), lambda i: (0, 0))],
        out_specs=pl.BlockSpec((t, W_GROUP), lambda i: (i + first_block, 0)),
        out_shape=jax.ShapeDtypeStruct((N_TOK, W_GROUP), F32),
        into=into, grid=(n_seq,), compiler_params=_cparams("arbitrary"), name="sconv")


def _chunk_scan(x, reverse):
    t = x.shape[0]
    pos = _iota(x.shape, 0) % CHUNK
    step = 1
    while step < CHUNK:
        if reverse:
            x = x + jnp.where(pos < CHUNK - step, pltpu.roll(x, t - step, 0), 0.0)
        else:
            x = x + jnp.where(pos >= step, pltpu.roll(x, step, 0), 0.0)
        step *= 2
    return x


GDN_GROUP_CHUNKS = 4
GDN_CHAINS = GDN_GROUP_CHUNKS * N_GATE
GDN_PAIRS = GDN_GROUP_CHUNKS * N_HEADS


def _gdn_kernel(a_ref, gate_ref, convw_ref, alog_ref, dtb_ref, nw_ref, s0_ref, o_ref, sfin_ref,
                q_s, kv_s, kt_s, gc_s, eg_s, beta_s, gcrow_s, ekdrow_s, cdec_s, uo_s, wq_s, attn_s, kdt_s,
                st_s, wsqs_s, kk_s, d_s, off_s, m1_s, m2_s, p_s, m3_s, rhs_s, px_s, *, t):
    n_chunks = t // CHUNK
    a = a_ref[...]
    qkv = _silu(_conv3(a[:, :3 * W_GROUP], convw_ref, t))
    q = qkv[:, :W_GROUP]
    k = qkv[:, W_GROUP:2 * W_GROUP]
    v = qkv[:, 2 * W_GROUP:]
    head_sum = _head_sum_matrix()
    q = q * lax.rsqrt(_head_sums(q * q, head_sum) + EPS) * (HEAD_DIM ** -0.5)
    k = k * lax.rsqrt(_head_sums(k * k, head_sum) + EPS)
    for h in range(N_HEADS):
        hs = slice(h * HEAD_DIM, (h + 1) * HEAD_DIM)
        q_s[h] = q[:, hs]
        kv_s[h] = jnp.concatenate([k[:, hs], v[:, hs]], axis=1)
    k_t = k.T
    for c in range(n_chunks):
        kt_s[c] = k_t[:, c * CHUNK:(c + 1) * CHUNK]

    gates = gate_ref[...]
    log_a = -jnp.exp(alog_ref[...]) * _softplus(gates + dtb_ref[...])
    beta_s[...] = _sigmoid(gates)

    ci = _iota((CHUNK, CHUNK), 0)
    cj = _iota((CHUNK, CHUNK), 1)
    eye = (ci == cj).astype(F32)
    blk_mask = (ci // 16) == (cj // 16)
    low_half = _iota((CHUNK, 2 * HEAD_DIM), 1) < HEAD_DIM

    prefix = _chunk_scan(log_a, reverse=False)
    suffix = _chunk_scan(log_a, reverse=True)
    gc = jnp.where(_iota((t, LANES), 1) < N_HEADS, prefix, suffix)
    gt = prefix + suffix - log_a
    gc_s[...] = gc
    eg_s[...] = jnp.exp(gc)
    gc_t = gc.T
    ekd_t = jnp.exp(gt - gc).T
    cdec_t = jnp.exp(gt).T
    for c in range(n_chunks):
        lanes = slice(c * CHUNK, (c + 1) * CHUNK)
        gcrow_s[c] = gc_t[:N_GATE, lanes]
        ekdrow_s[c] = ekd_t[:N_GATE, lanes]
        cdec_s[c] = jnp.concatenate([cdec_t[:N_GATE, lanes]] * 2, axis=1)

    def solve_group(grp, carry):
        row0 = grp * (GDN_GROUP_CHUNKS * CHUNK)
        chains = [(cl, a_idx) for cl in range(GDN_GROUP_CHUNKS) for a_idx in range(N_GATE)]

        def rows_of(cl):
            return pl.ds(pl.multiple_of(row0 + cl * CHUNK, CHUNK), CHUNK)

        for cl in range(GDN_GROUP_CHUNKS):
            for h in range(N_HEADS):
                rows = rows_of(cl)
                kq = jnp.concatenate([kv_s[h, rows, :HEAD_DIM], q_s[h, rows, :]], axis=0)
                k_t_h = kt_s[grp * GDN_GROUP_CHUNKS + cl, h * HEAD_DIM:(h + 1) * HEAD_DIM, :]
                kk_s[cl * N_HEADS + h] = _bmm(kq, k_t_h)
        for b, (cl, a_idx) in enumerate(chains):
            backward = a_idx >= N_HEADS
            h = a_idx % N_HEADS
            rows, c = rows_of(cl), grp * GDN_GROUP_CHUNKS + cl
            incl = (cj >= ci) if backward else (cj <= ci)
            strict = (cj > ci) if backward else (cj < ci)
            bt = beta_s[rows, N_GATE + a_idx:N_GATE + a_idx + 1]
            decay = jnp.exp(jnp.where(incl, gc_s[rows, a_idx:a_idx + 1] - gcrow_s[c, a_idx:a_idx + 1, :], NEG_BIG))
            low = jnp.where(strict, kk_s[cl * N_HEADS + h, :CHUNK, :] * bt * decay, 0.0)
            attn_s[a_idx, rows, :] = (kk_s[cl * N_HEADS + h, CHUNK:, :] * decay).astype(BF16)
            d = jnp.where(blk_mask, low, 0.0)
            d_s[b] = d
            off_s[b] = low - d
            rhs_s[b] = kv_s[h, rows, :] * bt * jnp.where(low_half, eg_s[rows, a_idx:a_idx + 1], 1.0)
        for b in range(GDN_CHAINS):
            m1_s[b] = _bmm(d_s[b], d_s[b])
        for b in range(GDN_CHAINS):
            d, d2 = d_s[b], m1_s[b]
            m2_s[b] = _bmm(d2, d2)
            p_s[b] = eye - d + d2 - _bmm(d, d2)
        for b in range(GDN_CHAINS):
            d4, p = m2_s[b], p_s[b]
            m3_s[b] = _bmm(d4, d4)
            p_s[b] = p + _bmm(p, d4)
        for b in range(GDN_CHAINS):
            p = p_s[b]
            p_s[b] = p + _bmm(p, m3_s[b])
        for b in range(GDN_CHAINS):
            p = p_s[b]
            m1_s[b] = _bmm(p, off_s[b])
            px_s[b] = _bmm(p, rhs_s[b])
        for b in range(GDN_CHAINS):
            m2_s[b] = _bmm(m1_s[b], m1_s[b])
        for b in range(GDN_CHAINS):
            n, n2 = m1_s[b], m2_s[b]
            m3_s[b] = eye - n + n2 - _bmm(n, n2)
        for b, (cl, a_idx) in enumerate(chains):
            h = a_idx % N_HEADS
            rows, c = rows_of(cl), grp * GDN_GROUP_CHUNKS + cl
            wu = _bmm(m3_s[b], px_s[b])
            uo_s[a_idx, rows, :] = wu
            wq_s[a_idx, c, :CHUNK, :] = wu[:, :HEAD_DIM].astype(BF16)
            wq_s[a_idx, c, CHUNK:, :] = (q_s[h, rows, :] * eg_s[rows, a_idx:a_idx + 1]).astype(BF16)
            k_t_h = kt_s[c, h * HEAD_DIM:(h + 1) * HEAD_DIM, :]
            kdt_s[a_idx, c] = (k_t_h * ekdrow_s[c, a_idx:a_idx + 1, :]).astype(BF16)
        return carry

    lax.fori_loop(0, n_chunks // GDN_GROUP_CHUNKS, solve_group, 0)

    for i in range(N_GATE):
        st_s[i] = jnp.concatenate([jnp.zeros((HEAD_DIM, HEAD_DIM), F32), s0_ref[0, i]], axis=1)

    def scan_chunk(c, carry):
        def chunk_of(a_idx):
            return (n_chunks - 1 - c) if a_idx >= N_HEADS else c

        for a_idx in range(N_GATE):
            wsqs_s[a_idx] = _mm(wq_s[a_idx, chunk_of(a_idx)], st_s[a_idx].astype(BF16))
        for a_idx in range(N_GATE):
            cc = chunk_of(a_idx)
            rows = pl.ds(pl.multiple_of(cc * CHUNK, CHUNK), CHUNK)
            v_new = (uo_s[a_idx, rows, :] - wsqs_s[a_idx, :CHUNK, :]).astype(BF16)
            uo_s[a_idx, rows, :] = wsqs_s[a_idx, CHUNK:, :] + _mm(attn_s[a_idx, rows, :], v_new)
            st_s[a_idx] = st_s[a_idx] * cdec_s[cc, a_idx:a_idx + 1, :] + _mm(kdt_s[a_idx, cc], v_new)
        return carry

    lax.fori_loop(0, n_chunks, scan_chunk, 0)
    for i in range(N_GATE):
        sfin_ref[0, i] = st_s[i, :, HEAD_DIM:]

    o = jnp.concatenate([(uo_s[h] + uo_s[N_HEADS + h])[:, HEAD_DIM:] for h in range(N_HEADS)], axis=1)
    ms = _head_sums(o * o, head_sum) * (1.0 / HEAD_DIM)
    o_ref[...] = o * lax.rsqrt(ms + EPS) * nw_ref[...] * _silu(a[:, 3 * W_GROUP:])


def _gdn(a_all, gate_all, conv_w, a_log, dt_bias, norm_w, s0, t, n_seq, first_block, into=None):
    small = lambda: pl.BlockSpec((1, LANES), lambda i: (0, 0))
    n_chunks = t // CHUNK
    wide = 2 * HEAD_DIM
    scratch = [pltpu.VMEM((N_HEADS, t, HEAD_DIM), F32),
               pltpu.VMEM((N_HEADS, t, wide), F32),
               pltpu.VMEM((n_chunks, W_GROUP, CHUNK), F32),
               pltpu.VMEM((t, LANES), F32), pltpu.VMEM((t, LANES), F32), pltpu.VMEM((t, LANES), F32),
               pltpu.VMEM((n_chunks, N_GATE, CHUNK), F32), pltpu.VMEM((n_chunks, N_GATE, CHUNK), F32),
               pltpu.VMEM((n_chunks, N_GATE, wide), F32),
               pltpu.VMEM((N_GATE, t, wide), F32),
               pltpu.VMEM((N_GATE, n_chunks, 2 * CHUNK, HEAD_DIM), BF16),
               pltpu.VMEM((N_GATE, t, CHUNK), BF16),
               pltpu.VMEM((N_GATE, n_chunks, HEAD_DIM, CHUNK), BF16),
               pltpu.VMEM((N_GATE, HEAD_DIM, wide), F32),
               pltpu.VMEM((N_GATE, 2 * CHUNK, wide), F32),
               pltpu.VMEM((GDN_PAIRS, 2 * CHUNK, CHUNK), F32)]
    scratch += [pltpu.VMEM((GDN_CHAINS, CHUNK, CHUNK), F32)] * 6
    scratch += [pltpu.VMEM((GDN_CHAINS, CHUNK, wide), F32)] * 2
    return _seq_call(
        functools.partial(_gdn_kernel, t=t), (a_all, gate_all, conv_w, a_log, dt_bias, norm_w, s0),
        in_specs=[pl.BlockSpec((t, 4 * W_GROUP), lambda i: (i + first_block, 0)),
                  pl.BlockSpec((t, LANES), lambda i: (i + first_block, 0)),
                  pl.BlockSpec((SUBLANES, 3 * W_GROUP), lambda i: (0, 0)),
                  small(), small(),
                  pl.BlockSpec((1, W_GROUP), lambda i: (0, 0)),
                  pl.BlockSpec((1, N_GATE, HEAD_DIM, HEAD_DIM), lambda i: (i, 0, 0, 0))],
        out_specs=[pl.BlockSpec((t, W_GROUP), lambda i: (i + first_block, 0)),
                   pl.BlockSpec((1, N_GATE, HEAD_DIM, HEAD_DIM), lambda i: (i, 0, 0, 0))],
        out_shape=[jax.ShapeDtypeStruct((N_TOK, W_GROUP), F32),
                   jax.ShapeDtypeStruct((n_seq, N_GATE, HEAD_DIM, HEAD_DIM), F32)],
        scratch_shapes=scratch,
        into=into, grid=(n_seq,), compiler_params=_cparams("arbitrary"), name="gdn")


def _swap16(x):
    width = x.shape[-1]
    first = (_iota(x.shape, 1) // 16) % 2 == 0
    return jnp.where(first, pltpu.roll(x, width - 16, 1), pltpu.roll(x, 16, 1))


def _block_diag_heads(s0_ref, first):
    zero = jnp.zeros((HEAD_DIM, HEAD_DIM), F32)
    return jnp.concatenate(
        [jnp.concatenate([s0_ref[0, first + h] if j == h else zero for j in range(N_HEADS)], axis=1)
         for h in range(N_HEADS)], axis=0)


def _ret_kernel(r_ref, lg_ref, s0_ref, cos_ref, sin_ref, o_ref, sfin_ref, *, t, latent):
    r = r_ref[...]
    q = r[:, :W_GROUP]
    k = r[:, W_GROUP:2 * W_GROUP]
    v = r[:, 2 * W_GROUP:3 * W_GROUP]
    if latent:
        q = q * cos_ref[...] + _swap16(q) * sin_ref[...]
        k = k * cos_ref[...] + _swap16(k) * sin_ref[...]
    k = k * (HEAD_DIM ** -0.5)
    lg = -_softplus(-lg_ref[...])
    lgf, lgb = lg[0:1, :], lg[1:2, :]
    head = _iota((1, W_GROUP), 1) // HEAD_DIM
    head_sum = _head_sum_matrix()
    pos = _iota((t, 1), 0).astype(F32)
    q_b = q.astype(BF16)
    kt_b = k.T.astype(BF16)
    v_heads = [jnp.where(head == h, v, 0.0).astype(BF16) for h in range(N_HEADS)]
    if latent:
        s0f = _block_diag_heads(s0_ref, 0)
        s0b = _block_diag_heads(s0_ref, N_HEADS)
    for qt in range(t // Q_TILE):
        rows = slice(qt * Q_TILE, (qt + 1) * Q_TILE)
        diff = (_iota((Q_TILE, t), 0) + qt * Q_TILE - _iota((Q_TILE, t), 1)).astype(F32)
        o = jnp.zeros((Q_TILE, W_GROUP), F32)
        for h in range(N_HEADS):
            lgf_h = lgf[:, h * HEAD_DIM:h * HEAD_DIM + 1]
            lgb_h = lgb[:, h * HEAD_DIM:h * HEAD_DIM + 1]
            dmat = (jnp.exp(jnp.where(diff >= 0, diff * lgf_h, NEG_BIG))
                    + jnp.exp(jnp.where(diff <= 0, -diff * lgb_h, NEG_BIG)))
            s = _mm(jnp.where(head == h, q_b[rows], 0.0), kt_b) * dmat
            o = o + _mm(s.astype(BF16), v_heads[h])
        if latent:
            p = pos[rows]
            o = o + jnp.exp((p + 1.0) * lgf) * _bmm(q_b[rows], s0f) + jnp.exp((t - p) * lgb) * _bmm(q_b[rows], s0b)
        oc = o - _head_sums(o, head_sum) * (1.0 / HEAD_DIM)
        on = oc * lax.rsqrt(_head_sums(oc * oc, head_sum) * (1.0 / HEAD_DIM) + EPS)
        o_ref[rows, :] = on * _silu(r[rows, 3 * W_GROUP:])
    v_b = v.astype(BF16)
    sf = _mm_tn((k * jnp.exp((t - 1.0 - pos) * lgf)).astype(BF16), v_b)
    sb = _mm_tn((k * jnp.exp(pos * lgb)).astype(BF16), v_b)
    for h in range(N_HEADS):
        hs = slice(h * HEAD_DIM, (h + 1) * HEAD_DIM)
        sf_h, sb_h = sf[hs, hs], sb[hs, hs]
        if latent:
            sf_h = sf_h + jnp.exp(t * lgf[:, h * HEAD_DIM:h * HEAD_DIM + 1]) * s0_ref[0, h]
            sb_h = sb_h + jnp.exp(t * lgb[:, h * HEAD_DIM:h * HEAD_DIM + 1]) * s0_ref[0, N_HEADS + h]
        sfin_ref[0, h] = sf_h
        sfin_ref[0, N_HEADS + h] = sb_h


def _ret(r_all, logit, s0, cos, sin, t, n_seq, first_block, latent, into=None):
    return _seq_call(
        functools.partial(_ret_kernel, t=t, latent=latent), (r_all, logit, s0, cos, sin),
        in_specs=[pl.BlockSpec((t, 4 * W_GROUP), lambda i: (i + first_block, 0)),
                  pl.BlockSpec((2, W_GROUP), lambda i: (0, 0)),
                  pl.BlockSpec((1, N_GATE, HEAD_DIM, HEAD_DIM), lambda i: (i, 0, 0, 0)),
                  pl.BlockSpec((t, W_GROUP), lambda i: (0, 0)),
                  pl.BlockSpec((t, W_GROUP), lambda i: (0, 0))],
        out_specs=[pl.BlockSpec((t, W_GROUP), lambda i: (i + first_block, 0)),
                   pl.BlockSpec((1, N_GATE, HEAD_DIM, HEAD_DIM), lambda i: (i, 0, 0, 0))],
        out_shape=[jax.ShapeDtypeStruct((N_TOK, W_GROUP), F32),
                   jax.ShapeDtypeStruct((n_seq, N_GATE, HEAD_DIM, HEAD_DIM), F32)],
        into=into, grid=(n_seq,), compiler_params=_cparams("arbitrary"), name="ret")


def _rope_tables(t):
    pos = np.arange(t)
    row = (pos // GRID_W).astype(np.float32)
    col = (pos % GRID_W).astype(np.float32)
    nf = HEAD_DIM // 4
    inv_freq = jnp.power(ROPE_BASE, -jnp.arange(nf, dtype=F32) / nf)
    ang_r = jnp.asarray(row)[:, None] * inv_freq[None, :]
    ang_c = jnp.asarray(col)[:, None] * inv_freq[None, :]
    cos = jnp.concatenate([jnp.cos(ang_r)] * 2 + [jnp.cos(ang_c)] * 2, axis=1)
    sin = jnp.concatenate([-jnp.sin(ang_r), jnp.sin(ang_r), -jnp.sin(ang_c), jnp.sin(ang_c)], axis=1)
    return jnp.tile(cos, (1, N_HEADS)), jnp.tile(sin, (1, N_HEADS))


def _head_rms(x, w, head_sum):
    return x * lax.rsqrt(_head_sums(x * x, head_sum) * (1.0 / HEAD_DIM) + EPS) * w


def _ctx_attn_kernel(n_ref, qw_ref, kw_ref, o_ref, k_out, v_out, *, n_fill):
    n = n_ref[...]
    head_sum = _head_sum_matrix()
    head = _iota((1, W_GROUP), 1) // HEAD_DIM
    q = (_head_rms(n[:, :W_GROUP], qw_ref[...], head_sum) * (HEAD_DIM ** -0.5)).astype(BF16)
    k = _head_rms(n[:, W_GROUP:2 * W_GROUP], kw_ref[...], head_sum)
    v = n[:, 2 * W_GROUP:]
    k_t = k.T
    k_out[0, 0] = k_t
    v_out[0, 0] = v.T
    for later in range(1, 1 + n_fill):
        k_out[0, later] = jnp.zeros((W_GROUP, SEQ), F32)
        v_out[0, later] = jnp.zeros((W_GROUP, SEQ), F32)
    kt_b = k_t.astype(BF16)
    o = jnp.zeros((SEQ, W_GROUP), F32)
    for h in range(N_HEADS):
        s = _mm(jnp.where(head == h, q, 0.0), kt_b)
        p = jnp.exp(s - jnp.max(s, axis=-1, keepdims=True))
        o = o + _mm(p.astype(BF16), jnp.where(head == h, v, 0.0).astype(BF16)) / jnp.sum(p, axis=-1, keepdims=True)
    o_ref[...] = o


def _ctx_attn(n_all, qw, kw, into, layer, caches):
    slots = DEPTH if caches is None else 1
    cache_spec = pl.BlockSpec((1, slots, W_GROUP, SEQ), lambda i: (i, layer if caches is not None else 0, 0, 0))
    cache_shape = jax.ShapeDtypeStruct((BATCH, DEPTH, W_GROUP, SEQ), F32)
    return _seq_call(
        functools.partial(_ctx_attn_kernel, n_fill=slots - 1), (n_all, qw, kw),
        in_specs=[pl.BlockSpec((SEQ, 3 * W_GROUP), lambda i: (i, 0)),
                  pl.BlockSpec((1, W_GROUP), lambda i: (0, 0)),
                  pl.BlockSpec((1, W_GROUP), lambda i: (0, 0))],
        out_specs=[pl.BlockSpec((SEQ, W_GROUP), lambda i: (i, 0)), cache_spec, cache_spec],
        out_shape=[jax.ShapeDtypeStruct((N_TOK, W_GROUP), F32), cache_shape, cache_shape],
        into=into, more_into=() if caches is None else ((caches[0], 1), (caches[1], 2)),
        grid=(BATCH,), compiler_params=_cparams("arbitrary"), name="ctx_attn")


def _nat_kernel(n_ref, ck_ref, cv_ref, bias_ref, qw_ref, kw_ref, o_ref, q_s, kt_s, ckt_s):
    h = pl.program_id(1)

    @pl.when(h == 0)
    def _():
        head_sum = _head_sum_matrix()
        n = n_ref[...]
        q_s[...] = (_head_rms(n[:, :W_GROUP], qw_ref[...], head_sum) * (HEAD_DIM ** -0.5)).astype(BF16)
        kt_s[...] = _head_rms(n[:, W_GROUP:2 * W_GROUP], kw_ref[...], head_sum).T.astype(BF16)
        ckt_s[...] = ck_ref[0].T.astype(BF16)
        o_ref[...] = jnp.zeros_like(o_ref)

    mine = _iota((1, W_GROUP), 1) // HEAD_DIM == h
    v = jnp.where(mine, n_ref[:, 2 * W_GROUP:], 0.0).astype(BF16)
    cv = jnp.where(mine, cv_ref[0], 0.0).astype(BF16)
    for qt in range(DEC_SEQ // Q_TILE):
        rows = slice(qt * Q_TILE, (qt + 1) * Q_TILE)
        q = jnp.where(mine, q_s[rows, :], 0.0)
        grid_rows = range(qt * Q_TILE // GRID_W, (qt + 1) * Q_TILE // GRID_W)
        s_loc = _mm(q, kt_s[...]) + jnp.concatenate([_nat_bias_strip(bias_ref, row) for row in grid_rows], axis=0)
        s_ctx = _mm(q, ckt_s[...])
        m = jnp.maximum(jnp.max(s_loc, axis=-1, keepdims=True), jnp.max(s_ctx, axis=-1, keepdims=True))
        p_loc = jnp.exp(s_loc - m)
        p_ctx = jnp.exp(s_ctx - m)
        den = jnp.sum(p_loc, axis=-1, keepdims=True) + jnp.sum(p_ctx, axis=-1, keepdims=True)
        o_ref[rows, :] += (_mm(p_loc.astype(BF16), v) + _mm(p_ctx.astype(BF16), cv)) / den


def _nat(n_all, ck, cv, bias, qw, kw, into):
    first_block = N_CTX // DEC_SEQ
    return _seq_call(
        _nat_kernel, (n_all, ck, cv, bias, qw, kw),
        in_specs=[pl.BlockSpec((DEC_SEQ, 3 * W_GROUP), lambda b, h: (b + first_block, 0)),
                  pl.BlockSpec((1, PAST_LEN, W_GROUP), lambda b, h: (b, 0, 0)),
                  pl.BlockSpec((1, PAST_LEN, W_GROUP), lambda b, h: (b, 0, 0)),
                  pl.BlockSpec((1, 3 * N_ROW_OFF - 1, GRID_W, 2 * GRID_W), lambda b, h: (h, 0, 0, 0)),
                  pl.BlockSpec((1, W_GROUP), lambda b, h: (0, 0)),
                  pl.BlockSpec((1, W_GROUP), lambda b, h: (0, 0))],
        out_specs=pl.BlockSpec((DEC_SEQ, W_GROUP), lambda b, h: (b + first_block, 0)),
        out_shape=jax.ShapeDtypeStruct((N_TOK, W_GROUP), F32),
        scratch_shapes=[pltpu.VMEM((DEC_SEQ, W_GROUP), BF16), pltpu.VMEM((W_GROUP, DEC_SEQ), BF16),
                        pltpu.VMEM((W_GROUP, PAST_LEN), BF16)],
        into=into, grid=(DEC_BATCH, N_HEADS), compiler_params=_cparams("arbitrary", "arbitrary"), name="nat")


N_ROW_OFF = 2 * WIN_ROWS - 1
NAT_ROWS = DEC_SEQ // GRID_W
NAT_KH = min(WIN_ROWS, NAT_ROWS)
NAT_PAIR, NAT_LOW, NAT_HIGH = 0, N_ROW_OFF - 1, 2 * N_ROW_OFF - 1


def _nat_tables(rpb):
    c = np.arange(GRID_W)
    c0 = np.clip(c - WIN_COLS // 2, 0, GRID_W - WIN_COLS)
    col_in = (c[None, :] >= c0[:, None]) & (c[None, :] < c0[:, None] + WIN_COLS)
    col_idx = np.clip(c[None, :] - c[:, None], -(WIN_COLS - 1), WIN_COLS - 1) + WIN_COLS - 1
    col_hot = (col_idx[..., None] == np.arange(2 * WIN_COLS - 1)).astype(np.float32)
    tz = jnp.where(col_in, jnp.einsum('hab,qkb->haqk', rpb.astype(F32), col_hot, precision=HI), NEG_BIG)
    neg = jnp.full_like(tz, NEG_BIG)
    return jnp.concatenate([jnp.concatenate([tz[:, :-1], tz[:, 1:]], axis=-1),
                            jnp.concatenate([tz, neg], axis=-1), jnp.concatenate([neg, tz], axis=-1)], axis=1)


def _nat_bias_strip(tab_ref, row):
    first_key = min(max(row - NAT_KH // 2, 0), NAT_ROWS - NAT_KH)
    off = first_key - row + WIN_ROWS - 1
    tiles = {}
    done, key = 0, first_key
    if key % 2 == 1:
        tiles[key // 2] = tab_ref[0, NAT_HIGH + off]
        done, key = 1, key + 1
    while done + 1 < NAT_KH:
        tiles[key // 2] = tab_ref[0, NAT_PAIR + off + done]
        done, key = done + 2, key + 2
    if done < NAT_KH:
        tiles[key // 2] = tab_ref[0, NAT_LOW + off + done]
    outside = jnp.full((GRID_W, 2 * GRID_W), NEG_BIG, F32)
    return jnp.concatenate([tiles.get(i, outside) for i in range(NAT_ROWS // 2)], axis=1)


GSEL_LANE = N_EXPERTS


def _pack_router(we, be, wg, bg):
    pad = LANES - N_EXPERTS - N_GROUPS
    w = jnp.concatenate([we, wg, jnp.zeros((D_MODEL, pad), F32)], axis=1)
    b = jnp.concatenate([be, bg, jnp.zeros((pad,), F32)]).reshape(1, LANES)
    hi = w.astype(BF16)
    lo = (w - hi.astype(F32)).astype(BF16)
    return jnp.concatenate([hi, lo], axis=1), b


def _lane_min_where(mask, lane):
    return jnp.min(jnp.where(mask, lane, LANES), axis=-1, keepdims=True)


def _outproj_kernel(xc_ref, xl_ref, m0, m1, m2, m3, mod_ref, nw_ref, w_ref, rw_ref, rb_ref, x_out, hf_out, route_out):
    acc = None
    for i, m_ref in enumerate((m0, m1, m2, m3)):
        part = _mm(m_ref[...].astype(BF16), w_ref[i * W_GROUP:(i + 1) * W_GROUP, :])
        acc = part if acc is None else acc + part
    x = _token_tile(pl.program_id(0), xc_ref, xl_ref) + mod_ref[0, 2:3, :] * acc
    x_out[...] = x
    y = x * lax.rsqrt(jnp.mean(x * x, axis=-1, keepdims=True) + EPS) * nw_ref[...]
    hf = y * (1.0 + mod_ref[0, 4:5, :]) + mod_ref[0, 3:4, :]
    hf_hi = hf.astype(BF16)
    hf_out[...] = hf_hi

    hf_lo = (hf - hf_hi.astype(F32)).astype(BF16)
    both = _mm(hf_hi, rw_ref[...])
    logits = both[:, :LANES] + both[:, LANES:] + _mm(hf_lo, rw_ref[:, :LANES]) + rb_ref[...]
    lane = _iota(logits.shape, 1)
    is_g = (lane >= N_EXPERTS) & (lane < N_EXPERTS + N_GROUPS)
    gl = jnp.where(is_g, logits, NEG_BIG)
    ge = jnp.exp(gl - jnp.max(gl, axis=-1, keepdims=True))
    gp = jnp.where(is_g, ge / jnp.sum(ge, axis=-1, keepdims=True), -1.0)
    gw = jnp.max(gp, axis=-1, keepdims=True)
    gsel = _lane_min_where(gp == gw, lane) - N_EXPERTS
    in_grp = (lane // EXPERTS_PER_GROUP == gsel) & (lane < N_EXPERTS)
    el = jnp.where(in_grp, logits, NEG_BIG)
    ee = jnp.exp(el - jnp.max(el, axis=-1, keepdims=True))
    ep = jnp.where(in_grp, ee / jnp.sum(ee, axis=-1, keepdims=True), -1.0)
    t1 = jnp.max(ep, axis=-1, keepdims=True)
    i1 = _lane_min_where(ep == t1, lane)
    ep2 = jnp.where(lane == i1, -1.0, ep)
    t2 = jnp.max(ep2, axis=-1, keepdims=True)
    i2 = _lane_min_where(ep2 == t2, lane)
    tsum = t1 + t2
    combine = jnp.where(lane == i1, gw * (t1 / tsum), 0.0) + jnp.where(lane == i2, gw * (t2 / tsum), 0.0)
    route_out[...] = jnp.where(lane == GSEL_LANE, gsel.astype(F32), combine)


def _outproj(x_ctx, x_lat, split, mixed, mod, norm_w, w_out, rw, rb):
    tile = lambda w: pl.BlockSpec((TOK_TILE, w), lambda i: (i, 0))
    whole = lambda a: pl.BlockSpec(a.shape, lambda i: (0,) * a.ndim)
    return pl.pallas_call(
        _outproj_kernel,
        grid=(N_TOK // TOK_TILE,),
        in_specs=_token_specs(split) + [tile(W_GROUP)] * 4
                 + [pl.BlockSpec((1, SUBLANES, D_MODEL), lambda i: (_cond_of_tile(i), 0, 0)),
                    pl.BlockSpec((1, D_MODEL), lambda i: (0, 0)), whole(w_out), whole(rw), whole(rb)],
        out_specs=[tile(D_MODEL), tile(D_MODEL), tile(LANES)],
        out_shape=[jax.ShapeDtypeStruct((N_TOK, D_MODEL), F32), jax.ShapeDtypeStruct((N_TOK, D_MODEL), BF16),
                   jax.ShapeDtypeStruct((N_TOK, LANES), F32)],
        compiler_params=_cparams("arbitrary"),
        name="outproj",
    )(x_ctx, x_lat, *mixed, mod, norm_w.reshape(1, D_MODEL), w_out, rw, rb)


SEG_BLK = 16
LOCAL_ROWS = TOK_TILE + N_GROUPS * SEG_BLK
N_TOK_TILES = N_TOK // TOK_TILE
MOE_ROWS = -(-(N_TOK + N_TOK_TILES * N_GROUPS * (SEG_BLK - 1) + N_GROUPS * (MOE_TILE - 1)) // MOE_TILE) * MOE_TILE


def _moe_tables(gsel):
    groups = jnp.arange(N_GROUPS, dtype=jnp.int32)
    onehot = (gsel.reshape(N_TOK_TILES, TOK_TILE, 1) == groups).astype(jnp.int32)
    rank = jnp.cumsum(onehot, axis=1) - onehot
    nblk = (jnp.sum(onehot, axis=1) + SEG_BLK - 1) // SEG_BLK
    loc_blk = jnp.cumsum(nblk, axis=1) - nblk
    blocks_per_tile = MOE_TILE // SEG_BLK
    grp_tiles = (jnp.sum(nblk, axis=0) + blocks_per_tile - 1) // blocks_per_tile
    grp_tile_start = jnp.cumsum(grp_tiles) - grp_tiles
    dst_blk = grp_tile_start[None, :] * blocks_per_tile + jnp.cumsum(nblk, axis=0) - nblk
    local_pos = jnp.sum(onehot * (loc_blk[:, None, :] * SEG_BLK + rank), axis=2)
    tile_idx = jnp.arange(MOE_ROWS // MOE_TILE, dtype=jnp.int32)
    tile_group = jnp.clip(jnp.sum(tile_idx[:, None] >= grp_tile_start[None, :], axis=1) - 1, 0, N_GROUPS - 1)
    tile_valid = tile_idx < jnp.sum(grp_tiles)
    flat = lambda a: a.reshape(-1).astype(jnp.int32)
    return local_pos.astype(jnp.int32), flat(nblk), flat(loc_blk), flat(dst_blk), flat(tile_group), flat(tile_valid)


def _segment_copies(t, nblk, loc_blk, dst_blk, make_copies, action):
    for g in range(N_GROUPS):
        k = t * N_GROUPS + g

        @pl.loop(0, nblk[k])
        def _(b):
            local = pl.multiple_of((loc_blk[k] + b) * SEG_BLK, SEG_BLK)
            sorted_row = pl.multiple_of((dst_blk[k] + b) * SEG_BLK, SEG_BLK)
            for cp in make_copies(local, sorted_row):
                action(cp)


def _dispatch_kernel(nblk, loc_blk, dst_blk, hf_ref, rt_ref, lp_ref, xs_in, rs_in, xs_hbm, rs_hbm, xbuf, rbuf, sem):
    t = pl.program_id(0)
    slot = t % 2
    onehot = _iota((LOCAL_ROWS, TOK_TILE), 0) == lp_ref[0]
    xbuf[slot] = _mm(onehot.astype(BF16), hf_ref[...]).astype(BF16)
    rbuf[slot] = _mm(onehot.astype(F32), rt_ref[...], HI)

    def copies_of(s):
        def copies(local, sorted_row):
            return (pltpu.make_async_copy(xbuf.at[s, pl.ds(local, SEG_BLK)], xs_hbm.at[pl.ds(sorted_row, SEG_BLK)], sem.at[s]),
                    pltpu.make_async_copy(rbuf.at[s, pl.ds(local, SEG_BLK)], rs_hbm.at[pl.ds(sorted_row, SEG_BLK)], sem.at[s]))
        return copies

    @pl.when(t > 0)
    def _():
        _segment_copies(t - 1, nblk, loc_blk, dst_blk, copies_of(1 - slot), lambda cp: cp.wait())

    _segment_copies(t, nblk, loc_blk, dst_blk, copies_of(slot), lambda cp: cp.start())

    @pl.when(t == N_TOK_TILES - 1)
    def _():
        _segment_copies(t, nblk, loc_blk, dst_blk, copies_of(slot), lambda cp: cp.wait())


def _dispatch(hf, route, local_pos, nblk, loc_blk, dst_blk):
    grid_spec = pltpu.PrefetchScalarGridSpec(
        num_scalar_prefetch=3,
        grid=(N_TOK_TILES,),
        in_specs=[pl.BlockSpec((TOK_TILE, D_MODEL), lambda t, *_: (t, 0)),
                  pl.BlockSpec((TOK_TILE, LANES), lambda t, *_: (t, 0)),
                  pl.BlockSpec((1, 1, TOK_TILE), lambda t, *_: (t, 0, 0)),
                  pl.BlockSpec(memory_space=pl.ANY), pl.BlockSpec(memory_space=pl.ANY)],
        out_specs=[pl.BlockSpec(memory_space=pl.ANY), pl.BlockSpec(memory_space=pl.ANY)],
        scratch_shapes=[pltpu.VMEM((2, LOCAL_ROWS, D_MODEL), BF16), pltpu.VMEM((2, LOCAL_ROWS, LANES), F32),
                        pltpu.SemaphoreType.DMA((2,))],
    )
    return pl.pallas_call(
        _dispatch_kernel,
        grid_spec=grid_spec,
        out_shape=[jax.ShapeDtypeStruct((MOE_ROWS, D_MODEL), BF16), jax.ShapeDtypeStruct((MOE_ROWS, LANES), F32)],
        input_output_aliases={6: 0, 7: 1},
        compiler_params=_cparams("arbitrary"),
        name="dispatch",
    )(nblk, loc_blk, dst_blk, hf, route, local_pos.reshape(N_TOK_TILES, 1, TOK_TILE),
      jnp.zeros((MOE_ROWS, D_MODEL), BF16), jnp.zeros((MOE_ROWS, LANES), F32))


def _moe_kernel(tile_group, tile_valid, x_ref, r_ref, wg_hbm, wu_hbm, wd_hbm, y_ref,
                wg_b, wu_b, wd_b, stage_g, stage_u, stage_d, sem, *, layer):
    i = pl.program_id(0)
    g = tile_group[i]
    group_row = layer * N_GROUPS + g
    new_group = (i == 0) | (g != tile_group[jnp.maximum(i - 1, 0)])
    valid = tile_valid[i] > 0

    def weight_copies(e):
        slot = e % 2
        return (pltpu.make_async_copy(wg_hbm.at[group_row, e], stage_g.at[slot], sem.at[slot]),
                pltpu.make_async_copy(wu_hbm.at[group_row, e], stage_u.at[slot], sem.at[slot]),
                pltpu.make_async_copy(wd_hbm.at[group_row, e], stage_d.at[slot], sem.at[slot]))

    def run(load_weights):
        x = x_ref[...]
        route = r_ref[...]
        lane = _iota(route.shape, 1)
        acc = jnp.zeros((MOE_TILE, D_MODEL), F32)
        if load_weights:
            for e in range(2):
                for cp in weight_copies(e):
                    cp.start()
        for e in range(EXPERTS_PER_GROUP):
            if load_weights:
                for cp in weight_copies(e):
                    cp.wait()
                wg_b[e] = stage_g[e % 2].astype(BF16)
                wu_b[e] = stage_u[e % 2].astype(BF16)
                wd_b[e] = stage_d[e % 2].astype(BF16)
                if e + 2 < EXPERTS_PER_GROUP:
                    for cp in weight_copies(e + 2):
                        cp.start()
            cw = jnp.sum(jnp.where(lane == g * EXPERTS_PER_GROUP + e, route, 0.0), axis=-1, keepdims=True)
            act = _silu(_mm(x, wg_b[e])) * _mm(x, wu_b[e]) * cw
            acc = acc + _mm(act.astype(BF16), wd_b[e])
        y_ref[...] = acc

    @pl.when(valid & new_group)
    def _():
        run(True)

    @pl.when(valid & jnp.logical_not(new_group))
    def _():
        run(False)

    @pl.when(jnp.logical_not(valid))
    def _():
        y_ref[...] = jnp.zeros_like(y_ref)


def _moe(xs, rs, tile_group, tile_valid, wg, wu, wd, layer):
    any_spec = pl.BlockSpec(memory_space=pl.ANY)
    grid_spec = pltpu.PrefetchScalarGridSpec(
        num_scalar_prefetch=2,
        grid=(MOE_ROWS // MOE_TILE,),
        in_specs=[pl.BlockSpec((MOE_TILE, D_MODEL), lambda i, tg, tv: (i, 0)),
                  pl.BlockSpec((MOE_TILE, LANES), lambda i, tg, tv: (i, 0)),
                  any_spec, any_spec, any_spec],
        out_specs=pl.BlockSpec((MOE_TILE, D_MODEL), lambda i, tg, tv: (i, 0)),
        scratch_shapes=[pltpu.VMEM((EXPERTS_PER_GROUP, D_MODEL, EXPERT_FF), BF16),
                        pltpu.VMEM((EXPERTS_PER_GROUP, D_MODEL, EXPERT_FF), BF16),
                        pltpu.VMEM((EXPERTS_PER_GROUP, EXPERT_FF, D_MODEL), BF16),
                        pltpu.VMEM((2, D_MODEL, EXPERT_FF), F32), pltpu.VMEM((2, D_MODEL, EXPERT_FF), F32),
                        pltpu.VMEM((2, EXPERT_FF, D_MODEL), F32), pltpu.SemaphoreType.DMA((2,))],
    )
    return pl.pallas_call(
        functools.partial(_moe_kernel, layer=layer),
        grid_spec=grid_spec,
        out_shape=jax.ShapeDtypeStruct((MOE_ROWS, D_MODEL), F32),
        compiler_params=_cparams("arbitrary"),
        name="moe",
    )(tile_group, tile_valid, xs, rs, wg, wu, wd)


def _combine_kernel(nblk, loc_blk, dst_blk, x_ref, lp_ref, mod_ref, ys_hbm, *refs, split):
    o_refs, ybuf, sem = refs[:-2], refs[-2], refs[-1]
    t = pl.program_id(0)
    slot = t % 2

    def fetch(tile, s):
        def copies(local, sorted_row):
            return (pltpu.make_async_copy(ys_hbm.at[pl.ds(sorted_row, SEG_BLK)], ybuf.at[s, pl.ds(local, SEG_BLK)], sem.at[s]),)
        ybuf[s] = jnp.zeros((LOCAL_ROWS, D_MODEL), F32)
        _segment_copies(tile, nblk, loc_blk, dst_blk, copies, lambda cp: cp.start())

    @pl.when(t == 0)
    def _():
        fetch(0, 0)

    @pl.when(t + 1 < N_TOK_TILES)
    def _():
        fetch(t + 1, 1 - slot)

    def copies_now(local, sorted_row):
        return (pltpu.make_async_copy(ys_hbm.at[pl.ds(sorted_row, SEG_BLK)], ybuf.at[slot, pl.ds(local, SEG_BLK)], sem.at[slot]),)

    _segment_copies(t, nblk, loc_blk, dst_blk, copies_now, lambda cp: cp.wait())

    onehot = (_iota((TOK_TILE, LOCAL_ROWS), 1) == lp_ref[...]).astype(BF16)
    ys = ybuf[slot]
    hi = ys.astype(BF16)
    lo = (ys - hi.astype(F32)).astype(BF16)
    y = _mm(onehot, hi) + _mm(onehot, lo)
    out = x_ref[...] + mod_ref[0, 5:6, :] * y
    if split:
        @pl.when(t < N_CTX_TILES)
        def _():
            o_refs[0][...] = out

        @pl.when(t >= N_CTX_TILES)
        def _():
            o_refs[1][...] = out
    else:
        o_refs[0][...] = out


def _combine(x, ys, mod, local_pos, nblk, loc_blk, dst_blk, split):
    tile = pl.BlockSpec((TOK_TILE, D_MODEL), lambda t, *_: (t, 0))
    if split:
        out_specs = _token_specs(True)
        out_shape = [jax.ShapeDtypeStruct((N_CTX, D_MODEL), F32), jax.ShapeDtypeStruct((N_LAT, D_MODEL), F32)]
    else:
        out_specs, out_shape = [tile], [jax.ShapeDtypeStruct((N_TOK, D_MODEL), F32)]
    grid_spec = pltpu.PrefetchScalarGridSpec(
        num_scalar_prefetch=3,
        grid=(N_TOK_TILES,),
        in_specs=[tile, pl.BlockSpec((TOK_TILE, 1), lambda t, *_: (t, 0)),
                  pl.BlockSpec((1, SUBLANES, D_MODEL), lambda t, *_: (_cond_of_tile(t), 0, 0)),
                  pl.BlockSpec(memory_space=pl.ANY)],
        out_specs=out_specs,
        scratch_shapes=[pltpu.VMEM((2, LOCAL_ROWS, D_MODEL), F32), pltpu.SemaphoreType.DMA((2,))],
    )
    return pl.pallas_call(
        functools.partial(_combine_kernel, split=split),
        grid_spec=grid_spec,
        out_shape=out_shape,
        compiler_params=_cparams("arbitrary"),
        name="combine",
    )(nblk, loc_blk, dst_blk, x, local_pos.reshape(N_TOK, 1), mod, ys)


def _lane_row(v):
    v = v.reshape(-1).astype(F32)
    return jnp.concatenate([v, jnp.zeros((LANES - v.shape[0],), F32)]).reshape(1, LANES)


def _pad_rows(w):
    return jnp.concatenate([w, jnp.zeros((SUBLANES - w.shape[0], w.shape[1]), w.dtype)], axis=0)


def kernel(x_prompt, x_sample, state_gdn, state_ret, cache_nat_k, cache_nat_v, c, c_ctx, ada_w, ada_b, norm_mix_w, norm_ffn_w, w_in, gdn_conv_w, gdn_a_log, gdn_dt_bias, gdn_norm_w, ret_gamma_logit, nat_q_norm_w, nat_k_norm_w, nat_rpb, sc_conv_w, w_out, router_group_w, router_group_b, router_expert_w, router_expert_b, moe_w_gate, moe_w_up, moe_w_down):
    x_ctx, x_lat, split = x_prompt.reshape(N_CTX, D_MODEL), x_sample.reshape(N_LAT, D_MODEL), True
    cond = jnp.concatenate([c_ctx[None, :], c, jnp.zeros((SUBLANES - 1 - DEC_BATCH, D_MODEL), F32)], axis=0)
    ada = _ada(cond, ada_w, ada_b).reshape(DEPTH, SUBLANES, 6, D_MODEL)
    cos, sin = _rope_tables(DEC_SEQ)
    zero_state = jnp.zeros((BATCH, N_GATE, HEAD_DIM, HEAD_DIM), F32)
    lat_block = N_CTX // DEC_SEQ
    gdn_list, ret_list, caches = [], [], None
    mixed = [jnp.zeros((N_TOK, W_GROUP), F32) for _ in range(4)]
    for l in range(DEPTH):
        mod = jnp.concatenate([ada[l, :1 + DEC_BATCH], jnp.zeros((1 + DEC_BATCH, SUBLANES - 6, D_MODEL), F32)], axis=1)
        a_gdn, a_ret, a_nat, a_sc, a_gate = _inproj(x_ctx, x_lat, split, mod, norm_mix_w[l], w_in, l)

        conv_w = _pad_rows(gdn_conv_w[l])
        a_log, dt_b = _lane_row(gdn_a_log[l]), _lane_row(gdn_dt_bias[l])
        gnw = jnp.tile(gdn_norm_w[l], N_HEADS).reshape(1, W_GROUP)
        o_gdn, s_gdn = _gdn(a_gdn, a_gate, conv_w, a_log, dt_b, gnw, zero_state, SEQ, BATCH, 0, into=mixed[0])
        s0 = state_gdn[:, l].reshape(DEC_BATCH, N_GATE, HEAD_DIM, HEAD_DIM)
        o_gdn, _ = _gdn(a_gdn, a_gate, conv_w, a_log, dt_b, gnw, s0, DEC_SEQ, DEC_BATCH, lat_block, into=o_gdn)

        logit = jnp.repeat(ret_gamma_logit[l].astype(F32), HEAD_DIM, axis=1)
        o_ret, s_ret = _ret(a_ret, logit, zero_state, cos[:SEQ], sin[:SEQ], SEQ, BATCH, 0, False, into=mixed[1])
        s0 = state_ret[:, l].reshape(DEC_BATCH, N_GATE, HEAD_DIM, HEAD_DIM)
        o_ret, _ = _ret(a_ret, logit, s0, cos, sin, DEC_SEQ, DEC_BATCH, lat_block, True, into=o_ret)

        qw = jnp.tile(nat_q_norm_w[l], N_HEADS).reshape(1, W_GROUP)
        kw = jnp.tile(nat_k_norm_w[l], N_HEADS).reshape(1, W_GROUP)
        o_nat, *caches = _ctx_attn(a_nat, qw, kw, mixed[2], l, caches)
        o_nat = _nat(a_nat, cache_nat_k[:, l].reshape(DEC_BATCH, PAST_LEN, W_GROUP),
                     cache_nat_v[:, l].reshape(DEC_BATCH, PAST_LEN, W_GROUP), _nat_tables(nat_rpb[l]), qw, kw, o_nat)

        sc_w = _pad_rows(sc_conv_w[l])
        o_sc = _sconv(a_sc, sc_w, SEQ, BATCH, 0, into=mixed[3])
        o_sc = _sconv(a_sc, sc_w, DEC_SEQ, DEC_BATCH, lat_block, into=o_sc)

        mixed = [o_gdn, o_ret, o_nat, o_sc]
        rw, rb = _pack_router(router_expert_w[l], router_expert_b[l], router_group_w[l], router_group_b[l])
        x_mid, hf, route = _outproj(x_ctx, x_lat, split, mixed, mod, norm_ffn_w[l], w_out[l].astype(BF16), rw, rb)

        local_pos, nblk, loc_blk, dst_blk, tile_group, tile_valid = _moe_tables(route[:, GSEL_LANE].astype(jnp.int32))
        xs, rs = _dispatch(hf, route, local_pos, nblk, loc_blk, dst_blk)
        to_group = lambda w: w.reshape((DEPTH * N_GROUPS, EXPERTS_PER_GROUP) + w.shape[2:])
        ys = _moe(xs, rs, tile_group, tile_valid, to_group(moe_w_gate), to_group(moe_w_up), to_group(moe_w_down), l)
        last = l == DEPTH - 1
        out = _combine(x_mid, ys, mod, local_pos, nblk, loc_blk, dst_blk, split=last)
        x_ctx, x_lat, split = (out[0], out[1], True) if last else (out[0], out[0], False)

        gdn_list.append(s_gdn.reshape(BATCH, 2, N_HEADS, HEAD_DIM, HEAD_DIM))
        ret_list.append(s_ret.reshape(BATCH, 2, N_HEADS, HEAD_DIM, HEAD_DIM))
    new_k, new_v = [a.reshape(BATCH, DEPTH, N_HEADS, HEAD_DIM, SEQ).transpose(0, 1, 4, 2, 3) for a in caches]
    return (x_ctx.reshape(BATCH, SEQ, D_MODEL), x_lat.reshape(DEC_BATCH, DEC_SEQ, D_MODEL),
            jnp.stack(gdn_list, axis=1), jnp.stack(ret_list, axis=1), new_k, new_v)
```

```python
import functools

import numpy as np
import jax
import jax.numpy as jnp
from jax import lax
from jax.experimental import pallas as pl
from jax.experimental.pallas import tpu as pltpu

D_MODEL = 1024
BATCH = 16
SEQ = 256
DEPTH = 2
DEC_BATCH = 2
DEC_SEQ = 1024
PAST_LEN = 256
GRID_W = 64
HEAD_DIM = 64
W_GROUP = D_MODEL // 4
N_HEADS = W_GROUP // HEAD_DIM
CHUNK = 64
WIN_ROWS = 8
WIN_COLS = 16
ROPE_BASE = 10000.0
N_GROUPS = 4
EXPERTS_PER_GROUP = 8
N_EXPERTS = N_GROUPS * EXPERTS_PER_GROUP
EXPERT_FF = 256
GROUP_FF = EXPERTS_PER_GROUP * EXPERT_FF
EPS = 1e-6

N_CTX = BATCH * SEQ
N_LAT = DEC_BATCH * DEC_SEQ
N_TOK = N_CTX + N_LAT
LANES = 128
SUBLANES = 8
TOK_TILE = 512
MOE_TILE = 512
Q_TILE = 256
VMEM_LIMIT = 48 * 1024 * 1024
NEG_BIG = -1e30
N_GATE = 2 * N_HEADS

F32 = jnp.float32
BF16 = jnp.bfloat16
HI = lax.Precision.HIGHEST


def _mm(a, b, prec=None):
    return lax.dot_general(a, b, (((1,), (0,)), ((), ())), precision=prec, preferred_element_type=F32)


def _mm_nt(a, b, prec=None):
    return lax.dot_general(a, b, (((1,), (1,)), ((), ())), precision=prec, preferred_element_type=F32)


def _mm_tn(a, b, prec=None):
    return lax.dot_general(a, b, (((0,), (0,)), ((), ())), precision=prec, preferred_element_type=F32)


def _bmm(a, b):
    return _mm(a.astype(BF16), b.astype(BF16))


def _sigmoid(x):
    return 1.0 / (1.0 + jnp.exp(-x))


def _silu(x):
    return x * _sigmoid(x)


def _softplus(x):
    return jnp.maximum(x, 0.0) + jnp.log(1.0 + jnp.exp(-jnp.abs(x)))


def _iota(shape, dim):
    return lax.broadcasted_iota(jnp.int32, shape, dim)


def _cparams(*sem):
    return pltpu.CompilerParams(dimension_semantics=sem, vmem_limit_bytes=VMEM_LIMIT)


def _cond_of_tile(i):
    n_ctx_tiles = N_CTX // TOK_TILE
    return jnp.where(i < n_ctx_tiles, 0, 1 + (i - n_ctx_tiles) // (DEC_SEQ // TOK_TILE))


def _head_sum_matrix():
    return (_iota((W_GROUP, W_GROUP), 0) // HEAD_DIM == _iota((W_GROUP, W_GROUP), 1) // HEAD_DIM).astype(BF16)


def _head_sums(x, head_sum):
    hi = x.astype(BF16)
    lo = (x - hi.astype(F32)).astype(BF16)
    return _mm(hi, head_sum) + _mm(lo, head_sum)


def _ada_kernel(c_ref, w_ref, b_ref, o_ref):
    o_ref[0] = _mm(_silu(c_ref[...]), w_ref[0], HI) + b_ref[0]


def _ada(cond, ada_w, ada_b):
    tn = 1536
    n_out = 6 * D_MODEL
    return pl.pallas_call(
        _ada_kernel,
        grid=(DEPTH, n_out // tn),
        in_specs=[pl.BlockSpec((SUBLANES, D_MODEL), lambda l, j: (0, 0)),
                  pl.BlockSpec((1, D_MODEL, tn), lambda l, j: (l, 0, j)),
                  pl.BlockSpec((1, 1, tn), lambda l, j: (l, 0, j))],
        out_specs=pl.BlockSpec((1, SUBLANES, tn), lambda l, j: (l, 0, j)),
        out_shape=jax.ShapeDtypeStruct((DEPTH, SUBLANES, n_out), F32),
        compiler_params=_cparams("arbitrary", "arbitrary"),
        name="ada",
    )(cond, ada_w, ada_b.reshape(DEPTH, 1, n_out))


IN_WIDTHS = (4 * W_GROUP, 4 * W_GROUP, 3 * W_GROUP, 3 * W_GROUP, LANES)
IN_PACKED = sum(IN_WIDTHS)


IN_TOTAL = 3 * W_GROUP + W_GROUP + 2 * N_GATE + 4 * W_GROUP + 3 * W_GROUP + 3 * W_GROUP
IN_GATE_SRC = 4 * W_GROUP
IN_SRC = (0, IN_GATE_SRC + 2 * N_GATE, IN_GATE_SRC + 2 * N_GATE + 4 * W_GROUP,
          IN_GATE_SRC + 2 * N_GATE + 7 * W_GROUP)
N_CTX_TILES = N_CTX // TOK_TILE


def _token_specs(split):
    lat0 = 0 if split else N_CTX_TILES
    return [pl.BlockSpec((TOK_TILE, D_MODEL), lambda i, *_: (jnp.minimum(i, N_CTX_TILES - 1), 0)),
            pl.BlockSpec((TOK_TILE, D_MODEL), lambda i, *_: (jnp.maximum(i, N_CTX_TILES) - N_CTX_TILES + lat0, 0))]


def _token_tile(i, ctx_ref, lat_ref):
    return jnp.where(i < N_CTX_TILES, ctx_ref[...], lat_ref[...])


def _inproj_kernel(xc_ref, xl_ref, mod_ref, nw_ref, w_ref, *refs):
    o_refs, w_s = refs[:-1], refs[-1]
    i = pl.program_id(0)

    @pl.when(i == 0)
    def _():
        piece = 256
        off = 0
        for src, width in zip(IN_SRC, IN_WIDTHS[:-1]):
            for c in range(0, width, piece):
                w_s[:, off + c:off + c + piece] = w_ref[0, src + c:src + c + piece, :].T.astype(BF16)
            off += width
        gate = w_ref[0, IN_GATE_SRC:IN_GATE_SRC + LANES, :].T
        w_s[:, off:] = jnp.where(_iota((D_MODEL, LANES), 1) < 2 * N_GATE, gate, 0.0).astype(BF16)

    x = _token_tile(i, xc_ref, xl_ref)
    y = x * lax.rsqrt(jnp.mean(x * x, axis=-1, keepdims=True) + EPS) * nw_ref[...]
    h = (y * (1.0 + mod_ref[0, 1:2, :]) + mod_ref[0, 0:1, :]).astype(BF16)
    off = 0
    for o_ref, width in zip(o_refs, IN_WIDTHS):
        o_ref[...] = _mm(h, w_s[:, off:off + width])
        off += width


def _inproj(x_ctx, x_lat, split, mod, norm_w, w_in, layer):
    return pl.pallas_call(
        _inproj_kernel,
        grid=(N_TOK // TOK_TILE,),
        in_specs=_token_specs(split)
                 + [pl.BlockSpec((1, SUBLANES, D_MODEL), lambda i: (_cond_of_tile(i), 0, 0)),
                    pl.BlockSpec((1, D_MODEL), lambda i: (0, 0)),
                    pl.BlockSpec((1, IN_TOTAL, D_MODEL), lambda i: (layer, 0, 0), pipeline_mode=pl.Buffered(1))],
        out_specs=[pl.BlockSpec((TOK_TILE, w), lambda i: (i, 0)) for w in IN_WIDTHS],
        out_shape=[jax.ShapeDtypeStruct((N_TOK, w), F32) for w in IN_WIDTHS],
        scratch_shapes=[pltpu.VMEM((D_MODEL, IN_PACKED), BF16)],
        compiler_params=_cparams("arbitrary"),
        name="inproj",
    )(x_ctx, x_lat, mod, norm_w.reshape(1, D_MODEL), jnp.swapaxes(w_in, 1, 2))


def _seq_call(kernel_fn, args, in_specs, out_specs, out_shape, into, more_into=(), **kwargs):
    donors = ([] if into is None else [(into, 0)]) + list(more_into)
    n_in = len(args)
    inner = kernel_fn
    kernel_fn = lambda *refs: inner(*refs[:n_in], *refs[n_in + len(donors):])
    aliases = {n_in + j: out_idx for j, (_, out_idx) in enumerate(donors)}
    args = list(args) + [a for a, _ in donors]
    in_specs = list(in_specs) + [pl.BlockSpec(memory_space=pl.ANY)] * len(donors)
    return pl.pallas_call(kernel_fn, in_specs=in_specs, out_specs=out_specs, out_shape=out_shape,
                          input_output_aliases=aliases, **kwargs)(*args)


def _shift_rows(p, t):
    row = _iota(p.shape, 0)
    prev = jnp.where(row == 0, 0.0, pltpu.roll(p, 1, 0))
    nxt = jnp.where(row == t - 1, 0.0, pltpu.roll(p, t - 1, 0))
    return prev, nxt


def _conv3(x, w_ref, t):
    prev, nxt = _shift_rows(x, t)
    return w_ref[0:1, :] * prev + w_ref[1:2, :] * x + w_ref[2:3, :] * nxt


def _sconv_kernel(s_ref, w_ref, o_ref, *, t):
    s = s_ref[...]
    p = s[:, W_GROUP:2 * W_GROUP] * s[:, 2 * W_GROUP:]
    o_ref[...] = s[:, :W_GROUP] * _conv3(p, w_ref, t)


def _sconv(s_all, w, t, n_seq, first_block, into=None):
    return _seq_call(
        functools.partial(_sconv_kernel, t=t), (s_all, w),
        in_specs=[pl.BlockSpec((t, 3 * W_GROUP), lambda i: (i + first_block, 0)),
                  pl.BlockSpec((SUBLANES, W_GROUP), lambda i: (0, 0))],
        out_specs=pl.BlockSpec((t, W_GROUP), lambda i: (i + first_block, 0)),
        out_shape=jax.ShapeDtypeStruct((N_TOK, W_GROUP), F32),
        into=into, grid=(n_seq,), compiler_params=_cparams("arbitrary"), name="sconv")


def _chunk_scan(x, reverse):
    t = x.shape[0]
    pos = _iota(x.shape, 0) % CHUNK
    step = 1
    while step < CHUNK:
        if reverse:
            x = x + jnp.where(pos < CHUNK - step, pltpu.roll(x, t - step, 0), 0.0)
        else:
            x = x + jnp.where(pos >= step, pltpu.roll(x, step, 0), 0.0)
        step *= 2
    return x


GDN_GROUP_CHUNKS = 4
GDN_CHAINS = GDN_GROUP_CHUNKS * N_GATE
GDN_PAIRS = GDN_GROUP_CHUNKS * N_HEADS


def _gdn_kernel(a_ref, gate_ref, convw_ref, alog_ref, dtb_ref, nw_ref, s0_ref, o_ref, sfin_ref,
                q_s, kv_s, kt_s, gc_s, eg_s, beta_s, gcrow_s, ekdrow_s, cdec_s, uo_s, wq_s, attn_s, kdt_s,
                st_s, wsqs_s, kk_s, d_s, m1_s, p_s, n2_s, ob_s, d4_s, m3_s, rhs_s, px_s, *, t):
    n_chunks = t // CHUNK
    a = a_ref[...]
    qkv = _silu(_conv3(a[:, :3 * W_GROUP], convw_ref, t))
    q = qkv[:, :W_GROUP]
    k = qkv[:, W_GROUP:2 * W_GROUP]
    v = qkv[:, 2 * W_GROUP:]
    head_sum = _head_sum_matrix()
    q = q * lax.rsqrt(_head_sums(q * q, head_sum) + EPS) * (HEAD_DIM ** -0.5)
    k = k * lax.rsqrt(_head_sums(k * k, head_sum) + EPS)
    for h in range(N_HEADS):
        hs = slice(h * HEAD_DIM, (h + 1) * HEAD_DIM)
        q_s[h] = q[:, hs]
        kv_s[h] = jnp.concatenate([k[:, hs], v[:, hs]], axis=1)
    k_t = k.T
    for c in range(n_chunks):
        kt_s[c] = k_t[:, c * CHUNK:(c + 1) * CHUNK]

    gates = gate_ref[...]
    log_a = -jnp.exp(alog_ref[...]) * _softplus(gates + dtb_ref[...])
    beta_s[...] = _sigmoid(gates)

    ci = _iota((CHUNK, CHUNK), 0)
    cj = _iota((CHUNK, CHUNK), 1)
    eye = (ci == cj).astype(F32)
    blk_mask = (ci // 16) == (cj // 16)
    low_half = _iota((CHUNK, 2 * HEAD_DIM), 1) < HEAD_DIM

    prefix = _chunk_scan(log_a, reverse=False)
    suffix = _chunk_scan(log_a, reverse=True)
    gc = jnp.where(_iota((t, LANES), 1) < N_HEADS, prefix, suffix)
    gt = prefix + suffix - log_a
    gc_s[...] = gc
    eg_s[...] = jnp.exp(gc)
    gc_t = gc.T
    ekd_t = jnp.exp(gt - gc).T
    cdec_t = jnp.exp(gt).T
    for c in range(n_chunks):
        lanes = slice(c * CHUNK, (c + 1) * CHUNK)
        gcrow_s[c] = gc_t[:N_GATE, lanes]
        ekdrow_s[c] = ekd_t[:N_GATE, lanes]
        cdec_s[c] = jnp.concatenate([cdec_t[:N_GATE, lanes]] * 2, axis=1)

    def solve_group(grp, carry):
        row0 = grp * (GDN_GROUP_CHUNKS * CHUNK)
        chains = [(cl, a_idx) for cl in range(GDN_GROUP_CHUNKS) for a_idx in range(N_GATE)]

        def rows_of(cl):
            return pl.ds(pl.multiple_of(row0 + cl * CHUNK, CHUNK), CHUNK)

        for cl in range(GDN_GROUP_CHUNKS):
            for h in range(N_HEADS):
                rows = rows_of(cl)
                kq = jnp.concatenate([kv_s[h, rows, :HEAD_DIM], q_s[h, rows, :]], axis=0)
                k_t_h = kt_s[grp * GDN_GROUP_CHUNKS + cl, h * HEAD_DIM:(h + 1) * HEAD_DIM, :]
                kk_s[cl * N_HEADS + h] = _bmm(kq, k_t_h)
        for b, (cl, a_idx) in enumerate(chains):
            backward = a_idx >= N_HEADS
            h = a_idx % N_HEADS
            rows, c = rows_of(cl), grp * GDN_GROUP_CHUNKS + cl
            incl = (cj >= ci) if backward else (cj <= ci)
            strict = (cj > ci) if backward else (cj < ci)
            bt = beta_s[rows, N_GATE + a_idx:N_GATE + a_idx + 1]
            decay = jnp.exp(jnp.where(incl, gc_s[rows, a_idx:a_idx + 1] - gcrow_s[c, a_idx:a_idx + 1, :], NEG_BIG))
            low = jnp.where(strict, kk_s[cl * N_HEADS + h, :CHUNK, :] * bt * decay, 0.0)
            attn_s[a_idx, rows, :] = (kk_s[cl * N_HEADS + h, CHUNK:, :] * decay).astype(BF16)
            d = jnp.where(blk_mask, low, 0.0)
            d_s[b] = d
            ob_s[b] = (low - d).astype(BF16)
            rhs_s[b] = (kv_s[h, rows, :] * bt
                        * jnp.where(low_half, eg_s[rows, a_idx:a_idx + 1], 1.0)).astype(BF16)
        for b in range(GDN_CHAINS):
            m1_s[b] = _bmm(d_s[b], d_s[b])
        for b in range(GDN_CHAINS):
            d, d2 = d_s[b], m1_s[b]
            d4_s[b] = _bmm(d2, d2).astype(BF16)
            p_s[b] = eye - d + d2 - _bmm(d, d2)
        for b in range(GDN_CHAINS):
            d4, p = d4_s[b], p_s[b]
            m3_s[b] = _mm(d4, d4).astype(BF16)
            p_s[b] = p + _mm(p.astype(BF16), d4)
        for b in range(GDN_CHAINS):
            p = p_s[b]
            p_s[b] = p + _mm(p.astype(BF16), m3_s[b])
        for b in range(GDN_CHAINS):
            p = p_s[b].astype(BF16)
            m1_s[b] = _mm(p, ob_s[b])
            px_s[b] = _mm(p, rhs_s[b]).astype(BF16)
        for b in range(GDN_CHAINS):
            n2_s[b] = _bmm(m1_s[b], m1_s[b])
        for b in range(GDN_CHAINS):
            n, n2 = m1_s[b], n2_s[b]
            m3_s[b] = (eye - n + n2 - _bmm(n, n2)).astype(BF16)
        for b, (cl, a_idx) in enumerate(chains):
            h = a_idx % N_HEADS
            rows, c = rows_of(cl), grp * GDN_GROUP_CHUNKS + cl
            wu = _mm(m3_s[b], px_s[b])
            uo_s[a_idx, rows, :] = wu
            wq_s[a_idx, c, :CHUNK, :] = wu[:, :HEAD_DIM].astype(BF16)
            wq_s[a_idx, c, CHUNK:, :] = (q_s[h, rows, :] * eg_s[rows, a_idx:a_idx + 1]).astype(BF16)
            k_t_h = kt_s[c, h * HEAD_DIM:(h + 1) * HEAD_DIM, :]
            kdt_s[a_idx, c] = (k_t_h * ekdrow_s[c, a_idx:a_idx + 1, :]).astype(BF16)
        return carry

    lax.fori_loop(0, n_chunks // GDN_GROUP_CHUNKS, solve_group, 0)

    for i in range(N_GATE):
        st_s[i] = jnp.concatenate([jnp.zeros((HEAD_DIM, HEAD_DIM), F32), s0_ref[0, i]], axis=1)

    def scan_chunk(c, carry):
        def chunk_of(a_idx):
            return (n_chunks - 1 - c) if a_idx >= N_HEADS else c

        for a_idx in range(N_GATE):
            wsqs_s[a_idx] = _mm(wq_s[a_idx, chunk_of(a_idx)], st_s[a_idx].astype(BF16))
        for a_idx in range(N_GATE):
            cc = chunk_of(a_idx)
            rows = pl.ds(pl.multiple_of(cc * CHUNK, CHUNK), CHUNK)
            v_new = (uo_s[a_idx, rows, :] - wsqs_s[a_idx, :CHUNK, :]).astype(BF16)
            uo_s[a_idx, rows, :] = wsqs_s[a_idx, CHUNK:, :] + _mm(attn_s[a_idx, rows, :], v_new)
            st_s[a_idx] = st_s[a_idx] * cdec_s[cc, a_idx:a_idx + 1, :] + _mm(kdt_s[a_idx, cc], v_new)
        return carry

    lax.fori_loop(0, n_chunks, scan_chunk, 0)
    for i in range(N_GATE):
        sfin_ref[0, i] = st_s[i, :, HEAD_DIM:]

    o = jnp.concatenate([(uo_s[h] + uo_s[N_HEADS + h])[:, HEAD_DIM:] for h in range(N_HEADS)], axis=1)
    ms = _head_sums(o * o, head_sum) * (1.0 / HEAD_DIM)
    o_ref[...] = o * lax.rsqrt(ms + EPS) * nw_ref[...] * _silu(a[:, 3 * W_GROUP:])


def _gdn(a_all, gate_all, conv_w, a_log, dt_bias, norm_w, s0, t, n_seq, first_block, into=None):
    small = lambda: pl.BlockSpec((1, LANES), lambda i: (0, 0))
    n_chunks = t // CHUNK
    wide = 2 * HEAD_DIM
    scratch = [pltpu.VMEM((N_HEADS, t, HEAD_DIM), F32),
               pltpu.VMEM((N_HEADS, t, wide), F32),
               pltpu.VMEM((n_chunks, W_GROUP, CHUNK), F32),
               pltpu.VMEM((t, LANES), F32), pltpu.VMEM((t, LANES), F32), pltpu.VMEM((t, LANES), F32),
               pltpu.VMEM((n_chunks, N_GATE, CHUNK), F32), pltpu.VMEM((n_chunks, N_GATE, CHUNK), F32),
               pltpu.VMEM((n_chunks, N_GATE, wide), F32),
               pltpu.VMEM((N_GATE, t, wide), F32),
               pltpu.VMEM((N_GATE, n_chunks, 2 * CHUNK, HEAD_DIM), BF16),
               pltpu.VMEM((N_GATE, t, CHUNK), BF16),
               pltpu.VMEM((N_GATE, n_chunks, HEAD_DIM, CHUNK), BF16),
               pltpu.VMEM((N_GATE, HEAD_DIM, wide), F32),
               pltpu.VMEM((N_GATE, 2 * CHUNK, wide), F32),
               pltpu.VMEM((GDN_PAIRS, 2 * CHUNK, CHUNK), F32)]
    scratch += [pltpu.VMEM((GDN_CHAINS, CHUNK, CHUNK), F32)] * 4
    scratch += [pltpu.VMEM((GDN_CHAINS, CHUNK, CHUNK), BF16)] * 3
    scratch += [pltpu.VMEM((GDN_CHAINS, CHUNK, wide), BF16)] * 2
    return _seq_call(
        functools.partial(_gdn_kernel, t=t), (a_all, gate_all, conv_w, a_log, dt_bias, norm_w, s0),
        in_specs=[pl.BlockSpec((t, 4 * W_GROUP), lambda i: (i + first_block, 0)),
                  pl.BlockSpec((t, LANES), lambda i: (i + first_block, 0)),
                  pl.BlockSpec((SUBLANES, 3 * W_GROUP), lambda i: (0, 0)),
                  small(), small(),
                  pl.BlockSpec((1, W_GROUP), lambda i: (0, 0)),
                  pl.BlockSpec((1, N_GATE, HEAD_DIM, HEAD_DIM), lambda i: (i, 0, 0, 0))],
        out_specs=[pl.BlockSpec((t, W_GROUP), lambda i: (i + first_block, 0)),
                   pl.BlockSpec((1, N_GATE, HEAD_DIM, HEAD_DIM), lambda i: (i, 0, 0, 0))],
        out_shape=[jax.ShapeDtypeStruct((N_TOK, W_GROUP), F32),
                   jax.ShapeDtypeStruct((n_seq, N_GATE, HEAD_DIM, HEAD_DIM), F32)],
        scratch_shapes=scratch,
        into=into, grid=(n_seq,), compiler_params=_cparams("arbitrary"), name="gdn")


def _swap16(x):
    width = x.shape[-1]
    first = (_iota(x.shape, 1) // 16) % 2 == 0
    return jnp.where(first, pltpu.roll(x, width - 16, 1), pltpu.roll(x, 16, 1))


def _block_diag_heads(s0_ref, first):
    zero = jnp.zeros((HEAD_DIM, HEAD_DIM), F32)
    return jnp.concatenate(
        [jnp.concatenate([s0_ref[0, first + h] if j == h else zero for j in range(N_HEADS)], axis=1)
         for h in range(N_HEADS)], axis=0)


def _ret_kernel(r_ref, lg_ref, s0_ref, cos_ref, sin_ref, o_ref, sfin_ref, *, t, latent):
    r = r_ref[...]
    q = r[:, :W_GROUP]
    k = r[:, W_GROUP:2 * W_GROUP]
    v = r[:, 2 * W_GROUP:3 * W_GROUP]
    if latent:
        q = q * cos_ref[...] + _swap16(q) * sin_ref[...]
        k = k * cos_ref[...] + _swap16(k) * sin_ref[...]
    k = k * (HEAD_DIM ** -0.5)
    lg = -_softplus(-lg_ref[...])
    lgf, lgb = lg[0:1, :], lg[1:2, :]
    head = _iota((1, W_GROUP), 1) // HEAD_DIM
    head_sum = _head_sum_matrix()
    pos = _iota((t, 1), 0).astype(F32)
    q_b = q.astype(BF16)
    kt_b = k.T.astype(BF16)
    v_heads = [jnp.where(head == h, v, 0.0).astype(BF16) for h in range(N_HEADS)]
    if latent:
        s0f = _block_diag_heads(s0_ref, 0)
        s0b = _block_diag_heads(s0_ref, N_HEADS)
    for qt in range(t // Q_TILE):
        rows = slice(qt * Q_TILE, (qt + 1) * Q_TILE)
        diff = (_iota((Q_TILE, t), 0) + qt * Q_TILE - _iota((Q_TILE, t), 1)).astype(F32)
        o = jnp.zeros((Q_TILE, W_GROUP), F32)
        for h in range(N_HEADS):
            lgf_h = lgf[:, h * HEAD_DIM:h * HEAD_DIM + 1]
            lgb_h = lgb[:, h * HEAD_DIM:h * HEAD_DIM + 1]
            dmat = (jnp.exp(jnp.where(diff >= 0, diff * lgf_h, NEG_BIG))
                    + jnp.exp(jnp.where(diff <= 0, -diff * lgb_h, NEG_BIG)))
            s = _mm(jnp.where(head == h, q_b[rows], 0.0), kt_b) * dmat
            o = o + _mm(s.astype(BF16), v_heads[h])
        if latent:
            p = pos[rows]
            o = o + jnp.exp((p + 1.0) * lgf) * _bmm(q_b[rows], s0f) + jnp.exp((t - p) * lgb) * _bmm(q_b[rows], s0b)
        oc = o - _head_sums(o, head_sum) * (1.0 / HEAD_DIM)
        on = oc * lax.rsqrt(_head_sums(oc * oc, head_sum) * (1.0 / HEAD_DIM) + EPS)
        o_ref[rows, :] = on * _silu(r[rows, 3 * W_GROUP:])
    v_b = v.astype(BF16)
    sf = _mm_tn((k * jnp.exp((t - 1.0 - pos) * lgf)).astype(BF16), v_b)
    sb = _mm_tn((k * jnp.exp(pos * lgb)).astype(BF16), v_b)
    for h in range(N_HEADS):
        hs = slice(h * HEAD_DIM, (h + 1) * HEAD_DIM)
        sf_h, sb_h = sf[hs, hs], sb[hs, hs]
        if latent:
            sf_h = sf_h + jnp.exp(t * lgf[:, h * HEAD_DIM:h * HEAD_DIM + 1]) * s0_ref[0, h]
            sb_h = sb_h + jnp.exp(t * lgb[:, h * HEAD_DIM:h * HEAD_DIM + 1]) * s0_ref[0, N_HEADS + h]
        sfin_ref[0, h] = sf_h
        sfin_ref[0, N_HEADS + h] = sb_h


def _ret(r_all, logit, s0, cos, sin, t, n_seq, first_block, latent, into=None):
    return _seq_call(
        functools.partial(_ret_kernel, t=t, latent=latent), (r_all, logit, s0, cos, sin),
        in_specs=[pl.BlockSpec((t, 4 * W_GROUP), lambda i: (i + first_block, 0)),
                  pl.BlockSpec((2, W_GROUP), lambda i: (0, 0)),
                  pl.BlockSpec((1, N_GATE, HEAD_DIM, HEAD_DIM), lambda i: (i, 0, 0, 0)),
                  pl.BlockSpec((t, W_GROUP), lambda i: (0, 0)),
                  pl.BlockSpec((t, W_GROUP), lambda i: (0, 0))],
        out_specs=[pl.BlockSpec((t, W_GROUP), lambda i: (i + first_block, 0)),
                   pl.BlockSpec((1, N_GATE, HEAD_DIM, HEAD_DIM), lambda i: (i, 0, 0, 0))],
        out_shape=[jax.ShapeDtypeStruct((N_TOK, W_GROUP), F32),
                   jax.ShapeDtypeStruct((n_seq, N_GATE, HEAD_DIM, HEAD_DIM), F32)],
        into=into, grid=(n_seq,), compiler_params=_cparams("arbitrary"), name="ret")


def _rope_tables(t):
    pos = np.arange(t)
    row = (pos // GRID_W).astype(np.float32)
    col = (pos % GRID_W).astype(np.float32)
    nf = HEAD_DIM // 4
    inv_freq = jnp.power(ROPE_BASE, -jnp.arange(nf, dtype=F32) / nf)
    ang_r = jnp.asarray(row)[:, None] * inv_freq[None, :]
    ang_c = jnp.asarray(col)[:, None] * inv_freq[None, :]
    cos = jnp.concatenate([jnp.cos(ang_r)] * 2 + [jnp.cos(ang_c)] * 2, axis=1)
    sin = jnp.concatenate([-jnp.sin(ang_r), jnp.sin(ang_r), -jnp.sin(ang_c), jnp.sin(ang_c)], axis=1)
    return jnp.tile(cos, (1, N_HEADS)), jnp.tile(sin, (1, N_HEADS))


def _head_rms(x, w, head_sum):
    return x * lax.rsqrt(_head_sums(x * x, head_sum) * (1.0 / HEAD_DIM) + EPS) * w


def _ctx_attn_kernel(n_ref, qw_ref, kw_ref, o_ref, k_out, v_out, *, n_fill):
    n = n_ref[...]
    head_sum = _head_sum_matrix()
    head = _iota((1, W_GROUP), 1) // HEAD_DIM
    q = (_head_rms(n[:, :W_GROUP], qw_ref[...], head_sum) * (HEAD_DIM ** -0.5)).astype(BF16)
    k = _head_rms(n[:, W_GROUP:2 * W_GROUP], kw_ref[...], head_sum)
    v = n[:, 2 * W_GROUP:]
    k_t = k.T
    k_out[0, 0] = k_t
    v_out[0, 0] = v.T
    for later in range(1, 1 + n_fill):
        k_out[0, later] = jnp.zeros((W_GROUP, SEQ), F32)
        v_out[0, later] = jnp.zeros((W_GROUP, SEQ), F32)
    kt_b = k_t.astype(BF16)
    o = jnp.zeros((SEQ, W_GROUP), F32)
    for h in range(N_HEADS):
        s = _mm(jnp.where(head == h, q, 0.0), kt_b)
        p = jnp.exp(s - jnp.max(s, axis=-1, keepdims=True))
        o = o + _mm(p.astype(BF16), jnp.where(head == h, v, 0.0).astype(BF16)) / jnp.sum(p, axis=-1, keepdims=True)
    o_ref[...] = o


def _ctx_attn(n_all, qw, kw, into, layer, caches):
    slots = DEPTH if caches is None else 1
    cache_spec = pl.BlockSpec((1, slots, W_GROUP, SEQ), lambda i: (i, layer if caches is not None else 0, 0, 0))
    cache_shape = jax.ShapeDtypeStruct((BATCH, DEPTH, W_GROUP, SEQ), F32)
    return _seq_call(
        functools.partial(_ctx_attn_kernel, n_fill=slots - 1), (n_all, qw, kw),
        in_specs=[pl.BlockSpec((SEQ, 3 * W_GROUP), lambda i: (i, 0)),
                  pl.BlockSpec((1, W_GROUP), lambda i: (0, 0)),
                  pl.BlockSpec((1, W_GROUP), lambda i: (0, 0))],
        out_specs=[pl.BlockSpec((SEQ, W_GROUP), lambda i: (i, 0)), cache_spec, cache_spec],
        out_shape=[jax.ShapeDtypeStruct((N_TOK, W_GROUP), F32), cache_shape, cache_shape],
        into=into, more_into=() if caches is None else ((caches[0], 1), (caches[1], 2)),
        grid=(BATCH,), compiler_params=_cparams("arbitrary"), name="ctx_attn")


def _nat_kernel(n_ref, ck_ref, cv_ref, bias_ref, qw_ref, kw_ref, o_ref, q_s, kt_s, ckt_s):
    h = pl.program_id(1)

    @pl.when(h == 0)
    def _():
        head_sum = _head_sum_matrix()
        n = n_ref[...]
        q_s[...] = (_head_rms(n[:, :W_GROUP], qw_ref[...], head_sum) * (HEAD_DIM ** -0.5)).astype(BF16)
        kt_s[...] = _head_rms(n[:, W_GROUP:2 * W_GROUP], kw_ref[...], head_sum).T.astype(BF16)
        ckt_s[...] = ck_ref[0].T.astype(BF16)
        o_ref[...] = jnp.zeros_like(o_ref)

    mine = _iota((1, W_GROUP), 1) // HEAD_DIM == h
    v = jnp.where(mine, n_ref[:, 2 * W_GROUP:], 0.0).astype(BF16)
    cv = jnp.where(mine, cv_ref[0], 0.0).astype(BF16)
    for qt in range(DEC_SEQ // Q_TILE):
        rows = slice(qt * Q_TILE, (qt + 1) * Q_TILE)
        q = jnp.where(mine, q_s[rows, :], 0.0)
        grid_rows = range(qt * Q_TILE // GRID_W, (qt + 1) * Q_TILE // GRID_W)
        s_loc = _mm(q, kt_s[...]) + jnp.concatenate([_nat_bias_strip(bias_ref, row) for row in grid_rows], axis=0)
        s_ctx = _mm(q, ckt_s[...])
        m = jnp.maximum(jnp.max(s_loc, axis=-1, keepdims=True), jnp.max(s_ctx, axis=-1, keepdims=True))
        p_loc = jnp.exp(s_loc - m)
        p_ctx = jnp.exp(s_ctx - m)
        den = jnp.sum(p_loc, axis=-1, keepdims=True) + jnp.sum(p_ctx, axis=-1, keepdims=True)
        o_ref[rows, :] += (_mm(p_loc.astype(BF16), v) + _mm(p_ctx.astype(BF16), cv)) / den


def _nat(n_all, ck, cv, bias, qw, kw, into):
    first_block = N_CTX // DEC_SEQ
    return _seq_call(
        _nat_kernel, (n_all, ck, cv, bias, qw, kw),
        in_specs=[pl.BlockSpec((DEC_SEQ, 3 * W_GROUP), lambda b, h: (b + first_block, 0)),
                  pl.BlockSpec((1, PAST_LEN, W_GROUP), lambda b, h: (b, 0, 0)),
                  pl.BlockSpec((1, PAST_LEN, W_GROUP), lambda b, h: (b, 0, 0)),
                  pl.BlockSpec((1, 3 * N_ROW_OFF - 1, GRID_W, 2 * GRID_W), lambda b, h: (h, 0, 0, 0)),
                  pl.BlockSpec((1, W_GROUP), lambda b, h: (0, 0)),
                  pl.BlockSpec((1, W_GROUP), lambda b, h: (0, 0))],
        out_specs=pl.BlockSpec((DEC_SEQ, W_GROUP), lambda b, h: (b + first_block, 0)),
        out_shape=jax.ShapeDtypeStruct((N_TOK, W_GROUP), F32),
        scratch_shapes=[pltpu.VMEM((DEC_SEQ, W_GROUP), BF16), pltpu.VMEM((W_GROUP, DEC_SEQ), BF16),
                        pltpu.VMEM((W_GROUP, PAST_LEN), BF16)],
        into=into, grid=(DEC_BATCH, N_HEADS), compiler_params=_cparams("arbitrary", "arbitrary"), name="nat")


N_ROW_OFF = 2 * WIN_ROWS - 1
NAT_ROWS = DEC_SEQ // GRID_W
NAT_KH = min(WIN_ROWS, NAT_ROWS)
NAT_PAIR, NAT_LOW, NAT_HIGH = 0, N_ROW_OFF - 1, 2 * N_ROW_OFF - 1


def _nat_tables(rpb):
    c = np.arange(GRID_W)
    c0 = np.clip(c - WIN_COLS // 2, 0, GRID_W - WIN_COLS)
    col_in = (c[None, :] >= c0[:, None]) & (c[None, :] < c0[:, None] + WIN_COLS)
    col_idx = np.clip(c[None, :] - c[:, None], -(WIN_COLS - 1), WIN_COLS - 1) + WIN_COLS - 1
    col_hot = (col_idx[..., None] == np.arange(2 * WIN_COLS - 1)).astype(np.float32)
    tz = jnp.where(col_in, jnp.einsum('hab,qkb->haqk', rpb.astype(F32), col_hot, precision=HI), NEG_BIG)
    neg = jnp.full_like(tz, NEG_BIG)
    return jnp.concatenate([jnp.concatenate([tz[:, :-1], tz[:, 1:]], axis=-1),
                            jnp.concatenate([tz, neg], axis=-1), jnp.concatenate([neg, tz], axis=-1)], axis=1)


def _nat_bias_strip(tab_ref, row):
    first_key = min(max(row - NAT_KH // 2, 0), NAT_ROWS - NAT_KH)
    off = first_key - row + WIN_ROWS - 1
    tiles = {}
    done, key = 0, first_key
    if key % 2 == 1:
        tiles[key // 2] = tab_ref[0, NAT_HIGH + off]
        done, key = 1, key + 1
    while done + 1 < NAT_KH:
        tiles[key // 2] = tab_ref[0, NAT_PAIR + off + done]
        done, key = done + 2, key + 2
    if done < NAT_KH:
        tiles[key // 2] = tab_ref[0, NAT_LOW + off + done]
    outside = jnp.full((GRID_W, 2 * GRID_W), NEG_BIG, F32)
    return jnp.concatenate([tiles.get(i, outside) for i in range(NAT_ROWS // 2)], axis=1)


GSEL_LANE = N_EXPERTS


def _pack_router(we, be, wg, bg):
    pad = LANES - N_EXPERTS - N_GROUPS
    w = jnp.concatenate([we, wg, jnp.zeros((D_MODEL, pad), F32)], axis=1)
    b = jnp.concatenate([be, bg, jnp.zeros((pad,), F32)]).reshape(1, LANES)
    hi = w.astype(BF16)
    lo = (w - hi.astype(F32)).astype(BF16)
    return jnp.concatenate([hi, lo], axis=1), b


def _lane_min_where(mask, lane):
    return jnp.min(jnp.where(mask, lane, LANES), axis=-1, keepdims=True)


def _outproj_kernel(xc_ref, xl_ref, m0, m1, m2, m3, mod_ref, nw_ref, w_ref, rw_ref, rb_ref, x_out, hf_out, route_out):
    acc = None
    for i, m_ref in enumerate((m0, m1, m2, m3)):
        part = _mm(m_ref[...].astype(BF16), w_ref[i * W_GROUP:(i + 1) * W_GROUP, :])
        acc = part if acc is None else acc + part
    x = _token_tile(pl.program_id(0), xc_ref, xl_ref) + mod_ref[0, 2:3, :] * acc
    x_out[...] = x
    y = x * lax.rsqrt(jnp.mean(x * x, axis=-1, keepdims=True) + EPS) * nw_ref[...]
    hf = y * (1.0 + mod_ref[0, 4:5, :]) + mod_ref[0, 3:4, :]
    hf_hi = hf.astype(BF16)
    hf_out[...] = hf_hi

    hf_lo = (hf - hf_hi.astype(F32)).astype(BF16)
    both = _mm(hf_hi, rw_ref[...])
    logits = both[:, :LANES] + both[:, LANES:] + _mm(hf_lo, rw_ref[:, :LANES]) + rb_ref[...]
    lane = _iota(logits.shape, 1)
    is_g = (lane >= N_EXPERTS) & (lane < N_EXPERTS + N_GROUPS)
    gl = jnp.where(is_g, logits, NEG_BIG)
    ge = jnp.exp(gl - jnp.max(gl, axis=-1, keepdims=True))
    gp = jnp.where(is_g, ge / jnp.sum(ge, axis=-1, keepdims=True), -1.0)
    gw = jnp.max(gp, axis=-1, keepdims=True)
    gsel = _lane_min_where(gp == gw, lane) - N_EXPERTS
    in_grp = (lane // EXPERTS_PER_GROUP == gsel) & (lane < N_EXPERTS)
    el = jnp.where(in_grp, logits, NEG_BIG)
    ee = jnp.exp(el - jnp.max(el, axis=-1, keepdims=True))
    ep = jnp.where(in_grp, ee / jnp.sum(ee, axis=-1, keepdims=True), -1.0)
    t1 = jnp.max(ep, axis=-1, keepdims=True)
    i1 = _lane_min_where(ep == t1, lane)
    ep2 = jnp.where(lane == i1, -1.0, ep)
    t2 = jnp.max(ep2, axis=-1, keepdims=True)
    i2 = _lane_min_where(ep2 == t2, lane)
    tsum = t1 + t2
    combine = jnp.where(lane == i1, gw * (t1 / tsum), 0.0) + jnp.where(lane == i2, gw * (t2 / tsum), 0.0)
    route_out[...] = jnp.where(lane == GSEL_LANE, gsel.astype(F32), combine)


def _outproj(x_ctx, x_lat, split, mixed, mod, norm_w, w_out, rw, rb):
    tile = lambda w: pl.BlockSpec((TOK_TILE, w), lambda i: (i, 0))
    whole = lambda a: pl.BlockSpec(a.shape, lambda i: (0,) * a.ndim)
    return pl.pallas_call(
        _outproj_kernel,
        grid=(N_TOK // TOK_TILE,),
        in_specs=_token_specs(split) + [tile(W_GROUP)] * 4
                 + [pl.BlockSpec((1, SUBLANES, D_MODEL), lambda i: (_cond_of_tile(i), 0, 0)),
                    pl.BlockSpec((1, D_MODEL), lambda i: (0, 0)), whole(w_out), whole(rw), whole(rb)],
        out_specs=[tile(D_MODEL), tile(D_MODEL), tile(LANES)],
        out_shape=[jax.ShapeDtypeStruct((N_TOK, D_MODEL), F32), jax.ShapeDtypeStruct((N_TOK, D_MODEL), BF16),
                   jax.ShapeDtypeStruct((N_TOK, LANES), F32)],
        compiler_params=_cparams("arbitrary"),
        name="outproj",
    )(x_ctx, x_lat, *mixed, mod, norm_w.reshape(1, D_MODEL), w_out, rw, rb)


SEG_BLK = 16
LOCAL_ROWS = TOK_TILE + N_GROUPS * SEG_BLK
N_TOK_TILES = N_TOK // TOK_TILE
MOE_ROWS = -(-(N_TOK + N_TOK_TILES * N_GROUPS * (SEG_BLK - 1) + N_GROUPS * (MOE_TILE - 1)) // MOE_TILE) * MOE_TILE


def _moe_tables(gsel):
    groups = jnp.arange(N_GROUPS, dtype=jnp.int32)
    onehot = (gsel.reshape(N_TOK_TILES, TOK_TILE, 1) == groups).astype(jnp.int32)
    rank = jnp.cumsum(onehot, axis=1) - onehot
    nblk = (jnp.sum(onehot, axis=1) + SEG_BLK - 1) // SEG_BLK
    loc_blk = jnp.cumsum(nblk, axis=1) - nblk
    blocks_per_tile = MOE_TILE // SEG_BLK
    grp_tiles = (jnp.sum(nblk, axis=0) + blocks_per_tile - 1) // blocks_per_tile
    grp_tile_start = jnp.cumsum(grp_tiles) - grp_tiles
    dst_blk = grp_tile_start[None, :] * blocks_per_tile + jnp.cumsum(nblk, axis=0) - nblk
    local_pos = jnp.sum(onehot * (loc_blk[:, None, :] * SEG_BLK + rank), axis=2)
    tile_idx = jnp.arange(MOE_ROWS // MOE_TILE, dtype=jnp.int32)
    tile_group = jnp.clip(jnp.sum(tile_idx[:, None] >= grp_tile_start[None, :], axis=1) - 1, 0, N_GROUPS - 1)
    tile_valid = tile_idx < jnp.sum(grp_tiles)
    flat = lambda a: a.reshape(-1).astype(jnp.int32)
    return local_pos.astype(jnp.int32), flat(nblk), flat(loc_blk), flat(dst_blk), flat(tile_group), flat(tile_valid)


def _segment_copies(t, nblk, loc_blk, dst_blk, make_copies, action):
    for g in range(N_GROUPS):
        k = t * N_GROUPS + g

        @pl.loop(0, nblk[k])
        def _(b):
            local = pl.multiple_of((loc_blk[k] + b) * SEG_BLK, SEG_BLK)
            sorted_row = pl.multiple_of((dst_blk[k] + b) * SEG_BLK, SEG_BLK)
            for cp in make_copies(local, sorted_row):
                action(cp)


def _dispatch_kernel(nblk, loc_blk, dst_blk, hf_ref, rt_ref, lp_ref, xs_in, rs_in, xs_hbm, rs_hbm, xbuf, rbuf, sem):
    t = pl.program_id(0)
    slot = t % 2
    onehot = _iota((LOCAL_ROWS, TOK_TILE), 0) == lp_ref[0]
    xbuf[slot] = _mm(onehot.astype(BF16), hf_ref[...]).astype(BF16)
    rbuf[slot] = _mm(onehot.astype(F32), rt_ref[...], HI)

    def copies_of(s):
        def copies(local, sorted_row):
            return (pltpu.make_async_copy(xbuf.at[s, pl.ds(local, SEG_BLK)], xs_hbm.at[pl.ds(sorted_row, SEG_BLK)], sem.at[s]),
                    pltpu.make_async_copy(rbuf.at[s, pl.ds(local, SEG_BLK)], rs_hbm.at[pl.ds(sorted_row, SEG_BLK)], sem.at[s]))
        return copies

    @pl.when(t > 0)
    def _():
        _segment_copies(t - 1, nblk, loc_blk, dst_blk, copies_of(1 - slot), lambda cp: cp.wait())

    _segment_copies(t, nblk, loc_blk, dst_blk, copies_of(slot), lambda cp: cp.start())

    @pl.when(t == N_TOK_TILES - 1)
    def _():
        _segment_copies(t, nblk, loc_blk, dst_blk, copies_of(slot), lambda cp: cp.wait())


def _dispatch(hf, route, local_pos, nblk, loc_blk, dst_blk):
    grid_spec = pltpu.PrefetchScalarGridSpec(
        num_scalar_prefetch=3,
        grid=(N_TOK_TILES,),
        in_specs=[pl.BlockSpec((TOK_TILE, D_MODEL), lambda t, *_: (t, 0)),
                  pl.BlockSpec((TOK_TILE, LANES), lambda t, *_: (t, 0)),
                  pl.BlockSpec((1, 1, TOK_TILE), lambda t, *_: (t, 0, 0)),
                  pl.BlockSpec(memory_space=pl.ANY), pl.BlockSpec(memory_space=pl.ANY)],
        out_specs=[pl.BlockSpec(memory_space=pl.ANY), pl.BlockSpec(memory_space=pl.ANY)],
        scratch_shapes=[pltpu.VMEM((2, LOCAL_ROWS, D_MODEL), BF16), pltpu.VMEM((2, LOCAL_ROWS, LANES), F32),
                        pltpu.SemaphoreType.DMA((2,))],
    )
    return pl.pallas_call(
        _dispatch_kernel,
        grid_spec=grid_spec,
        out_shape=[jax.ShapeDtypeStruct((MOE_ROWS, D_MODEL), BF16), jax.ShapeDtypeStruct((MOE_ROWS, LANES), F32)],
        input_output_aliases={6: 0, 7: 1},
        compiler_params=_cparams("arbitrary"),
        name="dispatch",
    )(nblk, loc_blk, dst_blk, hf, route, local_pos.reshape(N_TOK_TILES, 1, TOK_TILE),
      jnp.zeros((MOE_ROWS, D_MODEL), BF16), jnp.zeros((MOE_ROWS, LANES), F32))


def _moe_kernel(tile_group, tile_valid, x_ref, r_ref, wg_hbm, wu_hbm, wd_hbm, y_ref,
                wg_b, wu_b, wd_b, stage_g, stage_u, stage_d, sem, *, layer):
    i = pl.program_id(0)
    g = tile_group[i]
    group_row = layer * N_GROUPS + g
    new_group = (i == 0) | (g != tile_group[jnp.maximum(i - 1, 0)])
    valid = tile_valid[i] > 0

    def weight_copies(e):
        slot = e % 2
        return (pltpu.make_async_copy(wg_hbm.at[group_row, e], stage_g.at[slot], sem.at[slot]),
                pltpu.make_async_copy(wu_hbm.at[group_row, e], stage_u.at[slot], sem.at[slot]),
                pltpu.make_async_copy(wd_hbm.at[group_row, e], stage_d.at[slot], sem.at[slot]))

    def run(load_weights):
        x = x_ref[...]
        route = r_ref[...]
        lane = _iota(route.shape, 1)
        acc = jnp.zeros((MOE_TILE, D_MODEL), F32)
        if load_weights:
            for e in range(2):
                for cp in weight_copies(e):
                    cp.start()
        for e in range(EXPERTS_PER_GROUP):
            if load_weights:
                for cp in weight_copies(e):
                    cp.wait()
                wg_b[e] = stage_g[e % 2].astype(BF16)
                wu_b[e] = stage_u[e % 2].astype(BF16)
                wd_b[e] = stage_d[e % 2].astype(BF16)
                if e + 2 < EXPERTS_PER_GROUP:
                    for cp in weight_copies(e + 2):
                        cp.start()
            cw = jnp.sum(jnp.where(lane == g * EXPERTS_PER_GROUP + e, route, 0.0), axis=-1, keepdims=True)
            act = _silu(_mm(x, wg_b[e])) * _mm(x, wu_b[e]) * cw
            acc = acc + _mm(act.astype(BF16), wd_b[e])
        y_ref[...] = acc

    @pl.when(valid & new_group)
    def _():
        run(True)

    @pl.when(valid & jnp.logical_not(new_group))
    def _():
        run(False)

    @pl.when(jnp.logical_not(valid))
    def _():
        y_ref[...] = jnp.zeros_like(y_ref)


def _moe(xs, rs, tile_group, tile_valid, wg, wu, wd, layer):
    any_spec = pl.BlockSpec(memory_space=pl.ANY)
    grid_spec = pltpu.PrefetchScalarGridSpec(
        num_scalar_prefetch=2,
        grid=(MOE_ROWS // MOE_TILE,),
        in_specs=[pl.BlockSpec((MOE_TILE, D_MODEL), lambda i, tg, tv: (i, 0)),
                  pl.BlockSpec((MOE_TILE, LANES), lambda i, tg, tv: (i, 0)),
                  any_spec, any_spec, any_spec],
        out_specs=pl.BlockSpec((MOE_TILE, D_MODEL), lambda i, tg, tv: (i, 0)),
        scratch_shapes=[pltpu.VMEM((EXPERTS_PER_GROUP, D_MODEL, EXPERT_FF), BF16),
                        pltpu.VMEM((EXPERTS_PER_GROUP, D_MODEL, EXPERT_FF), BF16),
                        pltpu.VMEM((EXPERTS_PER_GROUP, EXPERT_FF, D_MODEL), BF16),
                        pltpu.VMEM((2, D_MODEL, EXPERT_FF), F32), pltpu.VMEM((2, D_MODEL, EXPERT_FF), F32),
                        pltpu.VMEM((2, EXPERT_FF, D_MODEL), F32), pltpu.SemaphoreType.DMA((2,))],
    )
    return pl.pallas_call(
        functools.partial(_moe_kernel, layer=layer),
        grid_spec=grid_spec,
        out_shape=jax.ShapeDtypeStruct((MOE_ROWS, D_MODEL), F32),
        compiler_params=_cparams("arbitrary"),
        name="moe",
    )(tile_group, tile_valid, xs, rs, wg, wu, wd)


def _combine_kernel(nblk, loc_blk, dst_blk, x_ref, lp_ref, mod_ref, ys_hbm, *refs, split):
    o_refs, ybuf, sem = refs[:-2], refs[-2], refs[-1]
    t = pl.program_id(0)
    slot = t % 2

    def fetch(tile, s):
        def copies(local, sorted_row):
            return (pltpu.make_async_copy(ys_hbm.at[pl.ds(sorted_row, SEG_BLK)], ybuf.at[s, pl.ds(local, SEG_BLK)], sem.at[s]),)
        ybuf[s] = jnp.zeros((LOCAL_ROWS, D_MODEL), F32)
        _segment_copies(tile, nblk, loc_blk, dst_blk, copies, lambda cp: cp.start())

    @pl.when(t == 0)
    def _():
        fetch(0, 0)

    @pl.when(t + 1 < N_TOK_TILES)
    def _():
        fetch(t + 1, 1 - slot)

    def copies_now(local, sorted_row):
        return (pltpu.make_async_copy(ys_hbm.at[pl.ds(sorted_row, SEG_BLK)], ybuf.at[slot, pl.ds(local, SEG_BLK)], sem.at[slot]),)

    _segment_copies(t, nblk, loc_blk, dst_blk, copies_now, lambda cp: cp.wait())

    onehot = (_iota((TOK_TILE, LOCAL_ROWS), 1) == lp_ref[...]).astype(BF16)
    ys = ybuf[slot]
    hi = ys.astype(BF16)
    lo = (ys - hi.astype(F32)).astype(BF16)
    y = _mm(onehot, hi) + _mm(onehot, lo)
    out = x_ref[...] + mod_ref[0, 5:6, :] * y
    if split:
        @pl.when(t < N_CTX_TILES)
        def _():
            o_refs[0][...] = out

        @pl.when(t >= N_CTX_TILES)
        def _():
            o_refs[1][...] = out
    else:
        o_refs[0][...] = out


def _combine(x, ys, mod, local_pos, nblk, loc_blk, dst_blk, split):
    tile = pl.BlockSpec((TOK_TILE, D_MODEL), lambda t, *_: (t, 0))
    if split:
        out_specs = _token_specs(True)
        out_shape = [jax.ShapeDtypeStruct((N_CTX, D_MODEL), F32), jax.ShapeDtypeStruct((N_LAT, D_MODEL), F32)]
    else:
        out_specs, out_shape = [tile], [jax.ShapeDtypeStruct((N_TOK, D_MODEL), F32)]
    grid_spec = pltpu.PrefetchScalarGridSpec(
        num_scalar_prefetch=3,
        grid=(N_TOK_TILES,),
        in_specs=[tile, pl.BlockSpec((TOK_TILE, 1), lambda t, *_: (t, 0)),
                  pl.BlockSpec((1, SUBLANES, D_MODEL), lambda t, *_: (_cond_of_tile(t), 0, 0)),
                  pl.BlockSpec(memory_space=pl.ANY)],
        out_specs=out_specs,
        scratch_shapes=[pltpu.VMEM((2, LOCAL_ROWS, D_MODEL), F32), pltpu.SemaphoreType.DMA((2,))],
    )
    return pl.pallas_call(
        functools.partial(_combine_kernel, split=split),
        grid_spec=grid_spec,
        out_shape=out_shape,
        compiler_params=_cparams("arbitrary"),
        name="combine",
    )(nblk, loc_blk, dst_blk, x, local_pos.reshape(N_TOK, 1), mod, ys)


def _lane_row(v):
    v = v.reshape(-1).astype(F32)
    return jnp.concatenate([v, jnp.zeros((LANES - v.shape[0],), F32)]).reshape(1, LANES)


def _pad_rows(w):
    return jnp.concatenate([w, jnp.zeros((SUBLANES - w.shape[0], w.shape[1]), w.dtype)], axis=0)


def kernel(x_prompt, x_sample, state_gdn, state_ret, cache_nat_k, cache_nat_v, c, c_ctx, ada_w, ada_b, norm_mix_w, norm_ffn_w, w_in, gdn_conv_w, gdn_a_log, gdn_dt_bias, gdn_norm_w, ret_gamma_logit, nat_q_norm_w, nat_k_norm_w, nat_rpb, sc_conv_w, w_out, router_group_w, router_group_b, router_expert_w, router_expert_b, moe_w_gate, moe_w_up, moe_w_down):
    x_ctx, x_lat, split = x_prompt.reshape(N_CTX, D_MODEL), x_sample.reshape(N_LAT, D_MODEL), True
    cond = jnp.concatenate([c_ctx[None, :], c, jnp.zeros((SUBLANES - 1 - DEC_BATCH, D_MODEL), F32)], axis=0)
    ada = _ada(cond, ada_w, ada_b).reshape(DEPTH, SUBLANES, 6, D_MODEL)
    cos, sin = _rope_tables(DEC_SEQ)
    zero_state = jnp.zeros((BATCH, N_GATE, HEAD_DIM, HEAD_DIM), F32)
    lat_block = N_CTX // DEC_SEQ
    gdn_list, ret_list, caches = [], [], None
    mixed = [jnp.zeros((N_TOK, W_GROUP), F32) for _ in range(4)]
    for l in range(DEPTH):
        mod = jnp.concatenate([ada[l, :1 + DEC_BATCH], jnp.zeros((1 + DEC_BATCH, SUBLANES - 6, D_MODEL), F32)], axis=1)
        a_gdn, a_ret, a_nat, a_sc, a_gate = _inproj(x_ctx, x_lat, split, mod, norm_mix_w[l], w_in, l)

        conv_w = _pad_rows(gdn_conv_w[l])
        a_log, dt_b = _lane_row(gdn_a_log[l]), _lane_row(gdn_dt_bias[l])
        gnw = jnp.tile(gdn_norm_w[l], N_HEADS).reshape(1, W_GROUP)
        o_gdn, s_gdn = _gdn(a_gdn, a_gate, conv_w, a_log, dt_b, gnw, zero_state, SEQ, BATCH, 0, into=mixed[0])
        s0 = state_gdn[:, l].reshape(DEC_BATCH, N_GATE, HEAD_DIM, HEAD_DIM)
        o_gdn, _ = _gdn(a_gdn, a_gate, conv_w, a_log, dt_b, gnw, s0, DEC_SEQ, DEC_BATCH, lat_block, into=o_gdn)

        logit = jnp.repeat(ret_gamma_logit[l].astype(F32), HEAD_DIM, axis=1)
        o_ret, s_ret = _ret(a_ret, logit, zero_state, cos[:SEQ], sin[:SEQ], SEQ, BATCH, 0, False, into=mixed[1])
        s0 = state_ret[:, l].reshape(DEC_BATCH, N_GATE, HEAD_DIM, HEAD_DIM)
        o_ret, _ = _ret(a_ret, logit, s0, cos, sin, DEC_SEQ, DEC_BATCH, lat_block, True, into=o_ret)

        qw = jnp.tile(nat_q_norm_w[l], N_HEADS).reshape(1, W_GROUP)
        kw = jnp.tile(nat_k_norm_w[l], N_HEADS).reshape(1, W_GROUP)
        o_nat, *caches = _ctx_attn(a_nat, qw, kw, mixed[2], l, caches)
        o_nat = _nat(a_nat, cache_nat_k[:, l].reshape(DEC_BATCH, PAST_LEN, W_GROUP),
                     cache_nat_v[:, l].reshape(DEC_BATCH, PAST_LEN, W_GROUP), _nat_tables(nat_rpb[l]), qw, kw, o_nat)

        sc_w = _pad_rows(sc_conv_w[l])
        o_sc = _sconv(a_sc, sc_w, SEQ, BATCH, 0, into=mixed[3])
        o_sc = _sconv(a_sc, sc_w, DEC_SEQ, DEC_BATCH, lat_block, into=o_sc)

        mixed = [o_gdn, o_ret, o_nat, o_sc]
        rw, rb = _pack_router(router_expert_w[l], router_expert_b[l], router_group_w[l], router_group_b[l])
        x_mid, hf, route = _outproj(x_ctx, x_lat, split, mixed, mod, norm_ffn_w[l], w_out[l].astype(BF16), rw, rb)

        local_pos, nblk, loc_blk, dst_blk, tile_group, tile_valid = _moe_tables(route[:, GSEL_LANE].astype(jnp.int32))
        xs, rs = _dispatch(hf, route, local_pos, nblk, loc_blk, dst_blk)
        to_group = lambda w: w.reshape((DEPTH * N_GROUPS, EXPERTS_PER_GROUP) + w.shape[2:])
        ys = _moe(xs, rs, tile_group, tile_valid, to_group(moe_w_gate), to_group(moe_w_up), to_group(moe_w_down), l)
        last = l == DEPTH - 1
        out = _combine(x_mid, ys, mod, local_pos, nblk, loc_blk, dst_blk, split=last)
        x_ctx, x_lat, split = (out[0], out[1], True) if last else (out[0], out[0], False)

        gdn_list.append(s_gdn.reshape(BATCH, 2, N_HEADS, HEAD_DIM, HEAD_DIM))
        ret_list.append(s_ret.reshape(BATCH, 2, N_HEADS, HEAD_DIM, HEAD_DIM))
    new_k, new_v = [a.reshape(BATCH, DEPTH, N_HEADS, HEAD_DIM, SEQ).transpose(0, 1, 4, 2, 3) for a in caches]
    return (x_ctx.reshape(BATCH, SEQ, D_MODEL), x_lat.reshape(DEC_BATCH, DEC_SEQ, D_MODEL),
            jnp.stack(gdn_list, axis=1), jnp.stack(ret_list, axis=1), new_k, new_v)
```

```python
import functools

import numpy as np
import jax
import jax.numpy as jnp
from jax import lax
from jax.experimental import pallas as pl
from jax.experimental.pallas import tpu as pltpu

D_MODEL = 1024
BATCH = 16
SEQ = 256
DEPTH = 2
DEC_BATCH = 2
DEC_SEQ = 1024
PAST_LEN = 256
GRID_W = 64
HEAD_DIM = 64
W_GROUP = D_MODEL // 4
N_HEADS = W_GROUP // HEAD_DIM
CHUNK = 64
WIN_ROWS = 8
WIN_COLS = 16
ROPE_BASE = 10000.0
N_GROUPS = 4
EXPERTS_PER_GROUP = 8
N_EXPERTS = N_GROUPS * EXPERTS_PER_GROUP
EXPERT_FF = 256
GROUP_FF = EXPERTS_PER_GROUP * EXPERT_FF
EPS = 1e-6

N_CTX = BATCH * SEQ
N_LAT = DEC_BATCH * DEC_SEQ
N_TOK = N_CTX + N_LAT
LANES = 128
SUBLANES = 8
TOK_TILE = 512
MOE_TILE = 512
Q_TILE = 256
VMEM_LIMIT = 48 * 1024 * 1024
NEG_BIG = -1e30
N_GATE = 2 * N_HEADS

F32 = jnp.float32
BF16 = jnp.bfloat16
HI = lax.Precision.HIGHEST


def _mm(a, b, prec=None):
    return lax.dot_general(a, b, (((1,), (0,)), ((), ())), precision=prec, preferred_element_type=F32)


def _mm_nt(a, b, prec=None):
    return lax.dot_general(a, b, (((1,), (1,)), ((), ())), precision=prec, preferred_element_type=F32)


def _mm_tn(a, b, prec=None):
    return lax.dot_general(a, b, (((0,), (0,)), ((), ())), precision=prec, preferred_element_type=F32)


def _bmm(a, b):
    return _mm(a.astype(BF16), b.astype(BF16))


def _sigmoid(x):
    return 1.0 / (1.0 + jnp.exp(-x))


def _silu(x):
    return x * _sigmoid(x)


def _softplus(x):
    return jnp.maximum(x, 0.0) + jnp.log(1.0 + jnp.exp(-jnp.abs(x)))


def _iota(shape, dim):
    return lax.broadcasted_iota(jnp.int32, shape, dim)


def _cparams(*sem):
    return pltpu.CompilerParams(dimension_semantics=sem, vmem_limit_bytes=VMEM_LIMIT)


def _cond_of_tile(i):
    n_ctx_tiles = N_CTX // TOK_TILE
    return jnp.where(i < n_ctx_tiles, 0, 1 + (i - n_ctx_tiles) // (DEC_SEQ // TOK_TILE))


def _head_sum_matrix():
    return (_iota((W_GROUP, W_GROUP), 0) // HEAD_DIM == _iota((W_GROUP, W_GROUP), 1) // HEAD_DIM).astype(BF16)


def _head_sums(x, head_sum):
    hi = x.astype(BF16)
    lo = (x - hi.astype(F32)).astype(BF16)
    return _mm(hi, head_sum) + _mm(lo, head_sum)


N_COND = 1 + DEC_BATCH
ADA_TN = 1536


def _ada_kernel(c_ref, w_ref, b_ref, o_ref):
    def slab(s, accs):
        rows = pl.ds(pl.multiple_of(s * SUBLANES, SUBLANES), SUBLANES)
        w = w_ref[0, rows, :]
        return tuple(acc + w * jnp.tile(_silu(c_ref[r, rows, :]), (1, ADA_TN // LANES))
                     for r, acc in enumerate(accs))

    zero = jnp.zeros((SUBLANES, ADA_TN), F32)
    accs = lax.fori_loop(0, D_MODEL // SUBLANES, slab, (zero,) * N_COND, unroll=4)
    out = jnp.concatenate([jnp.sum(acc, axis=0, keepdims=True) for acc in accs]
                          + [jnp.zeros((SUBLANES - N_COND, ADA_TN), F32)], axis=0)
    o_ref[0] = out + b_ref[0]


def _ada(cond, ada_w, ada_b):
    n_out = 6 * D_MODEL
    cond_lanes = jnp.broadcast_to(cond[:, :, None], (N_COND, D_MODEL, LANES))
    return pl.pallas_call(
        _ada_kernel,
        grid=(DEPTH, n_out // ADA_TN),
        in_specs=[pl.BlockSpec((N_COND, D_MODEL, LANES), lambda l, j: (0, 0, 0)),
                  pl.BlockSpec((1, D_MODEL, ADA_TN), lambda l, j: (l, 0, j)),
                  pl.BlockSpec((1, 1, ADA_TN), lambda l, j: (l, 0, j))],
        out_specs=pl.BlockSpec((1, SUBLANES, ADA_TN), lambda l, j: (l, 0, j)),
        out_shape=jax.ShapeDtypeStruct((DEPTH, SUBLANES, n_out), F32),
        compiler_params=_cparams("arbitrary", "arbitrary"),
        name="ada",
    )(cond_lanes, ada_w, ada_b.reshape(DEPTH, 1, n_out))


IN_WIDTHS = (4 * W_GROUP, 4 * W_GROUP, 3 * W_GROUP, 3 * W_GROUP, LANES)
IN_PACKED = sum(IN_WIDTHS)


IN_TOTAL = 3 * W_GROUP + W_GROUP + 2 * N_GATE + 4 * W_GROUP + 3 * W_GROUP + 3 * W_GROUP
IN_GATE_SRC = 4 * W_GROUP
IN_SRC = (0, IN_GATE_SRC + 2 * N_GATE, IN_GATE_SRC + 2 * N_GATE + 4 * W_GROUP,
          IN_GATE_SRC + 2 * N_GATE + 7 * W_GROUP)
N_CTX_TILES = N_CTX // TOK_TILE


def _token_specs(split):
    lat0 = 0 if split else N_CTX_TILES
    return [pl.BlockSpec((TOK_TILE, D_MODEL), lambda i, *_: (jnp.minimum(i, N_CTX_TILES - 1), 0)),
            pl.BlockSpec((TOK_TILE, D_MODEL), lambda i, *_: (jnp.maximum(i, N_CTX_TILES) - N_CTX_TILES + lat0, 0))]


def _token_tile(i, ctx_ref, lat_ref):
    return jnp.where(i < N_CTX_TILES, ctx_ref[...], lat_ref[...])


def _inproj_kernel(xc_ref, xl_ref, mod_ref, nw_ref, w_ref, *refs):
    o_refs, w_s = refs[:-1], refs[-1]
    i = pl.program_id(0)

    @pl.when(i == 0)
    def _():
        piece = 256
        off = 0
        for src, width in zip(IN_SRC, IN_WIDTHS[:-1]):
            for c in range(0, width, piece):
                w_s[:, off + c:off + c + piece] = w_ref[0, src + c:src + c + piece, :].T.astype(BF16)
            off += width
        gate = w_ref[0, IN_GATE_SRC:IN_GATE_SRC + LANES, :].T
        w_s[:, off:] = jnp.where(_iota((D_MODEL, LANES), 1) < 2 * N_GATE, gate, 0.0).astype(BF16)

    x = _token_tile(i, xc_ref, xl_ref)
    y = x * lax.rsqrt(jnp.mean(x * x, axis=-1, keepdims=True) + EPS) * nw_ref[...]
    h = (y * (1.0 + mod_ref[0, 1:2, :]) + mod_ref[0, 0:1, :]).astype(BF16)
    off = 0
    for o_ref, width in zip(o_refs, IN_WIDTHS):
        o_ref[...] = _mm(h, w_s[:, off:off + width])
        off += width


def _inproj(x_ctx, x_lat, split, mod, norm_w, w_in, layer):
    return pl.pallas_call(
        _inproj_kernel,
        grid=(N_TOK // TOK_TILE,),
        in_specs=_token_specs(split)
                 + [pl.BlockSpec((1, SUBLANES, D_MODEL), lambda i: (_cond_of_tile(i), 0, 0)),
                    pl.BlockSpec((1, D_MODEL), lambda i: (0, 0)),
                    pl.BlockSpec((1, IN_TOTAL, D_MODEL), lambda i: (layer, 0, 0), pipeline_mode=pl.Buffered(1))],
        out_specs=[pl.BlockSpec((TOK_TILE, w), lambda i: (i, 0)) for w in IN_WIDTHS],
        out_shape=[jax.ShapeDtypeStruct((N_TOK, w), F32) for w in IN_WIDTHS],
        scratch_shapes=[pltpu.VMEM((D_MODEL, IN_PACKED), BF16)],
        compiler_params=_cparams("arbitrary"),
        name="inproj",
    )(x_ctx, x_lat, mod, norm_w.reshape(1, D_MODEL), jnp.swapaxes(w_in, 1, 2))


def _seq_call(kernel_fn, args, in_specs, out_specs, out_shape, into, more_into=(), **kwargs):
    donors = ([] if into is None else [(into, 0)]) + list(more_into)
    n_in = len(args)
    inner = kernel_fn
    kernel_fn = lambda *refs: inner(*refs[:n_in], *refs[n_in + len(donors):])
    aliases = {n_in + j: out_idx for j, (_, out_idx) in enumerate(donors)}
    args = list(args) + [a for a, _ in donors]
    in_specs = list(in_specs) + [pl.BlockSpec(memory_space=pl.ANY)] * len(donors)
    return pl.pallas_call(kernel_fn, in_specs=in_specs, out_specs=out_specs, out_shape=out_shape,
                          input_output_aliases=aliases, **kwargs)(*args)


def _shift_rows(p, seq_len):
    rows = p.shape[0]
    pos = _iota(p.shape, 0) & (seq_len - 1)
    prev = jnp.where(pos == 0, 0.0, pltpu.roll(p, 1, 0))
    nxt = jnp.where(pos == seq_len - 1, 0.0, pltpu.roll(p, rows - 1, 0))
    return prev, nxt


def _conv3(x, w_ref, seq_len):
    prev, nxt = _shift_rows(x, seq_len)
    return w_ref[0:1, :] * prev + w_ref[1:2, :] * x + w_ref[2:3, :] * nxt


def _chunk_scan(x, reverse):
    t = x.shape[0]
    pos = _iota(x.shape, 0) % CHUNK
    step = 1
    while step < CHUNK:
        if reverse:
            x = x + jnp.where(pos < CHUNK - step, pltpu.roll(x, t - step, 0), 0.0)
        else:
            x = x + jnp.where(pos >= step, pltpu.roll(x, step, 0), 0.0)
        step *= 2
    return x


GDN_GROUP_CHUNKS = 4
GDN_CHAINS = GDN_GROUP_CHUNKS * N_GATE
GDN_PAIRS = GDN_GROUP_CHUNKS * N_HEADS


def _gdn_kernel(a_ref, gate_ref, convw_ref, alog_ref, dtb_ref, nw_ref, s0_ref, o_ref, sfin_ref,
                q_s, kv_s, kt_s, gc_s, eg_s, beta_s, gcrow_s, ekdrow_s, cdec_s, uo_s, wq_s, attn_s, kdt_s,
                st_s, wsqs_s, kk_s, d_s, m1_s, p_s, n2_s, ob_s, d4_s, m3_s, rhs_s, px_s, *, t, n_sub):
    n_chunks = t // CHUNK
    a = a_ref[...]
    qkv = _silu(_conv3(a[:, :3 * W_GROUP], convw_ref, t // n_sub))
    q = qkv[:, :W_GROUP]
    k = qkv[:, W_GROUP:2 * W_GROUP]
    v = qkv[:, 2 * W_GROUP:]
    head_sum = _head_sum_matrix()
    q = q * lax.rsqrt(_head_sums(q * q, head_sum) + EPS) * (HEAD_DIM ** -0.5)
    k = k * lax.rsqrt(_head_sums(k * k, head_sum) + EPS)
    for h in range(N_HEADS):
        hs = slice(h * HEAD_DIM, (h + 1) * HEAD_DIM)
        q_s[h] = q[:, hs]
        kv_s[h] = jnp.concatenate([k[:, hs], v[:, hs]], axis=1)
    k_t = k.T
    for c in range(n_chunks):
        kt_s[c] = k_t[:, c * CHUNK:(c + 1) * CHUNK]

    gates = gate_ref[...]
    log_a = -jnp.exp(alog_ref[...]) * _softplus(gates + dtb_ref[...])
    beta_s[...] = _sigmoid(gates)

    ci = _iota((CHUNK, CHUNK), 0)
    cj = _iota((CHUNK, CHUNK), 1)
    eye = (ci == cj).astype(F32)
    blk_mask = (ci // 16) == (cj // 16)
    low_half = _iota((CHUNK, 2 * HEAD_DIM), 1) < HEAD_DIM

    prefix = _chunk_scan(log_a, reverse=False)
    suffix = _chunk_scan(log_a, reverse=True)
    gc = jnp.where(_iota((t, LANES), 1) < N_HEADS, prefix, suffix)
    gt = prefix + suffix - log_a
    gc_s[...] = gc
    eg_s[...] = jnp.exp(gc)
    gc_t = gc.T
    ekd_t = jnp.exp(gt - gc).T
    cdec_t = jnp.exp(gt).T
    for c in range(n_chunks):
        lanes = slice(c * CHUNK, (c + 1) * CHUNK)
        gcrow_s[c] = gc_t[:N_GATE, lanes]
        ekdrow_s[c] = ekd_t[:N_GATE, lanes]
        cdec_s[c] = jnp.concatenate([cdec_t[:N_GATE, lanes]] * 2, axis=1)

    def solve_group(grp, carry):
        row0 = grp * (GDN_GROUP_CHUNKS * CHUNK)
        chains = [(cl, a_idx) for cl in range(GDN_GROUP_CHUNKS) for a_idx in range(N_GATE)]

        def rows_of(cl):
            return pl.ds(pl.multiple_of(row0 + cl * CHUNK, CHUNK), CHUNK)

        for cl in range(GDN_GROUP_CHUNKS):
            for h in range(N_HEADS):
                rows = rows_of(cl)
                kq = jnp.concatenate([kv_s[h, rows, :HEAD_DIM], q_s[h, rows, :]], axis=0)
                k_t_h = kt_s[grp * GDN_GROUP_CHUNKS + cl, h * HEAD_DIM:(h + 1) * HEAD_DIM, :]
                kk_s[cl * N_HEADS + h] = _bmm(kq, k_t_h)
        for b, (cl, a_idx) in enumerate(chains):
            backward = a_idx >= N_HEADS
            h = a_idx % N_HEADS
            rows, c = rows_of(cl), grp * GDN_GROUP_CHUNKS + cl
            incl = (cj >= ci) if backward else (cj <= ci)
            strict = (cj > ci) if backward else (cj < ci)
            bt = beta_s[rows, N_GATE + a_idx:N_GATE + a_idx + 1]
            decay = jnp.exp(jnp.where(incl, gc_s[rows, a_idx:a_idx + 1] - gcrow_s[c, a_idx:a_idx + 1, :], NEG_BIG))
            low = jnp.where(strict, kk_s[cl * N_HEADS + h, :CHUNK, :] * bt * decay, 0.0)
            attn_s[a_idx, rows, :] = (kk_s[cl * N_HEADS + h, CHUNK:, :] * decay).astype(BF16)
            d = jnp.where(blk_mask, low, 0.0)
            d_s[b] = d
            ob_s[b] = (low - d).astype(BF16)
            rhs_s[b] = (kv_s[h, rows, :] * bt
                        * jnp.where(low_half, eg_s[rows, a_idx:a_idx + 1], 1.0)).astype(BF16)
        for b in range(GDN_CHAINS):
            m1_s[b] = _bmm(d_s[b], d_s[b])
        for b in range(GDN_CHAINS):
            d, d2 = d_s[b], m1_s[b]
            d4_s[b] = _bmm(d2, d2).astype(BF16)
            p_s[b] = eye - d + d2 - _bmm(d, d2)
        for b in range(GDN_CHAINS):
            d4, p = d4_s[b], p_s[b]
            m3_s[b] = _mm(d4, d4).astype(BF16)
            p_s[b] = p + _mm(p.astype(BF16), d4)
        for b in range(GDN_CHAINS):
            p = p_s[b]
            p_s[b] = p + _mm(p.astype(BF16), m3_s[b])
        for b in range(GDN_CHAINS):
            p = p_s[b].astype(BF16)
            m1_s[b] = _mm(p, ob_s[b])
            px_s[b] = _mm(p, rhs_s[b]).astype(BF16)
        for b in range(GDN_CHAINS):
            n2_s[b] = _bmm(m1_s[b], m1_s[b])
        for b in range(GDN_CHAINS):
            n, n2 = m1_s[b], n2_s[b]
            m3_s[b] = (eye - n + n2 - _bmm(n, n2)).astype(BF16)
        for b, (cl, a_idx) in enumerate(chains):
            h = a_idx % N_HEADS
            rows, c = rows_of(cl), grp * GDN_GROUP_CHUNKS + cl
            wu = _mm(m3_s[b], px_s[b])
            uo_s[a_idx, rows, :] = wu
            wq_s[a_idx, c, :CHUNK, :] = wu[:, :HEAD_DIM].astype(BF16)
            wq_s[a_idx, c, CHUNK:, :] = (q_s[h, rows, :] * eg_s[rows, a_idx:a_idx + 1]).astype(BF16)
            k_t_h = kt_s[c, h * HEAD_DIM:(h + 1) * HEAD_DIM, :]
            kdt_s[a_idx, c] = (k_t_h * ekdrow_s[c, a_idx:a_idx + 1, :]).astype(BF16)
        return carry

    lax.fori_loop(0, n_chunks // GDN_GROUP_CHUNKS, solve_group, 0)

    seq_chunks = n_chunks // n_sub
    chains = [(sub, a_idx) for sub in range(n_sub) for a_idx in range(N_GATE)]
    for j, (sub, a_idx) in enumerate(chains):
        st_s[j] = jnp.concatenate([jnp.zeros((HEAD_DIM, HEAD_DIM), F32), s0_ref[sub, a_idx]], axis=1)

    def scan_chunk(c, carry):
        def chunk_of(sub, a_idx):
            return sub * seq_chunks + ((seq_chunks - 1 - c) if a_idx >= N_HEADS else c)

        for j, (sub, a_idx) in enumerate(chains):
            wsqs_s[j] = _mm(wq_s[a_idx, chunk_of(sub, a_idx)], st_s[j].astype(BF16))
        for j, (sub, a_idx) in enumerate(chains):
            cc = chunk_of(sub, a_idx)
            rows = pl.ds(pl.multiple_of(cc * CHUNK, CHUNK), CHUNK)
            v_new = (uo_s[a_idx, rows, :] - wsqs_s[j, :CHUNK, :]).astype(BF16)
            uo_s[a_idx, rows, :] = wsqs_s[j, CHUNK:, :] + _mm(attn_s[a_idx, rows, :], v_new)
            st_s[j] = st_s[j] * cdec_s[cc, a_idx:a_idx + 1, :] + _mm(kdt_s[a_idx, cc], v_new)
        return carry

    lax.fori_loop(0, seq_chunks, scan_chunk, 0)
    for j, (sub, a_idx) in enumerate(chains):
        sfin_ref[sub, a_idx] = st_s[j, :, HEAD_DIM:]

    o = jnp.concatenate([(uo_s[h] + uo_s[N_HEADS + h])[:, HEAD_DIM:] for h in range(N_HEADS)], axis=1)
    ms = _head_sums(o * o, head_sum) * (1.0 / HEAD_DIM)
    o_ref[...] = o * lax.rsqrt(ms + EPS) * nw_ref[...] * _silu(a[:, 3 * W_GROUP:])


def _gdn(a_all, gate_all, conv_w, a_log, dt_bias, norm_w, s0, seq_len, n_seq, first_block, n_sub=1, into=None):
    small = lambda: pl.BlockSpec((1, LANES), lambda i: (0, 0))
    t = n_sub * seq_len
    n_chunks = t // CHUNK
    wide = 2 * HEAD_DIM
    scratch = [pltpu.VMEM((N_HEADS, t, HEAD_DIM), F32),
               pltpu.VMEM((N_HEADS, t, wide), F32),
               pltpu.VMEM((n_chunks, W_GROUP, CHUNK), F32),
               pltpu.VMEM((t, LANES), F32), pltpu.VMEM((t, LANES), F32), pltpu.VMEM((t, LANES), F32),
               pltpu.VMEM((n_chunks, N_GATE, CHUNK), F32), pltpu.VMEM((n_chunks, N_GATE, CHUNK), F32),
               pltpu.VMEM((n_chunks, N_GATE, wide), F32),
               pltpu.VMEM((N_GATE, t, wide), F32),
               pltpu.VMEM((N_GATE, n_chunks, 2 * CHUNK, HEAD_DIM), BF16),
               pltpu.VMEM((N_GATE, t, CHUNK), BF16),
               pltpu.VMEM((N_GATE, n_chunks, HEAD_DIM, CHUNK), BF16),
               pltpu.VMEM((n_sub * N_GATE, HEAD_DIM, wide), F32),
               pltpu.VMEM((n_sub * N_GATE, 2 * CHUNK, wide), F32),
               pltpu.VMEM((GDN_PAIRS, 2 * CHUNK, CHUNK), F32)]
    scratch += [pltpu.VMEM((GDN_CHAINS, CHUNK, CHUNK), F32)] * 4
    scratch += [pltpu.VMEM((GDN_CHAINS, CHUNK, CHUNK), BF16)] * 3
    scratch += [pltpu.VMEM((GDN_CHAINS, CHUNK, wide), BF16)] * 2
    return _seq_call(
        functools.partial(_gdn_kernel, t=t, n_sub=n_sub), (a_all, gate_all, conv_w, a_log, dt_bias, norm_w, s0),
        in_specs=[pl.BlockSpec((t, 4 * W_GROUP), lambda i: (i + first_block, 0)),
                  pl.BlockSpec((t, LANES), lambda i: (i + first_block, 0)),
                  pl.BlockSpec((SUBLANES, 3 * W_GROUP), lambda i: (0, 0)),
                  small(), small(),
                  pl.BlockSpec((1, W_GROUP), lambda i: (0, 0)),
                  pl.BlockSpec((n_sub, N_GATE, HEAD_DIM, HEAD_DIM), lambda i: (i, 0, 0, 0))],
        out_specs=[pl.BlockSpec((t, W_GROUP), lambda i: (i + first_block, 0)),
                   pl.BlockSpec((n_sub, N_GATE, HEAD_DIM, HEAD_DIM), lambda i: (i, 0, 0, 0))],
        out_shape=[jax.ShapeDtypeStruct((N_TOK, W_GROUP), F32),
                   jax.ShapeDtypeStruct((n_seq, N_GATE, HEAD_DIM, HEAD_DIM), F32)],
        scratch_shapes=scratch,
        into=into, grid=(n_seq // n_sub,), compiler_params=_cparams("arbitrary"), name="gdn")


def _swap16(x):
    width = x.shape[-1]
    first = (_iota(x.shape, 1) // 16) % 2 == 0
    return jnp.where(first, pltpu.roll(x, width - 16, 1), pltpu.roll(x, 16, 1))


def _block_diag_heads(s0_ref, first):
    zero = jnp.zeros((HEAD_DIM, HEAD_DIM), F32)
    return jnp.concatenate(
        [jnp.concatenate([s0_ref[0, first + h] if j == h else zero for j in range(N_HEADS)], axis=1)
         for h in range(N_HEADS)], axis=0)


def _ret_kernel(r_ref, lg_ref, s0_ref, cos_ref, sin_ref, o_ref, sfin_ref, *, t, latent):
    r = r_ref[...]
    q = r[:, :W_GROUP]
    k = r[:, W_GROUP:2 * W_GROUP]
    v = r[:, 2 * W_GROUP:3 * W_GROUP]
    if latent:
        q = q * cos_ref[...] + _swap16(q) * sin_ref[...]
        k = k * cos_ref[...] + _swap16(k) * sin_ref[...]
    k = k * (HEAD_DIM ** -0.5)
    lg = -_softplus(-lg_ref[...])
    lgf, lgb = lg[0:1, :], lg[1:2, :]
    head = _iota((1, W_GROUP), 1) // HEAD_DIM
    head_sum = _head_sum_matrix()
    pos = _iota((t, 1), 0).astype(F32)
    q_b = q.astype(BF16)
    kt_b = k.T.astype(BF16)
    v_heads = [jnp.where(head == h, v, 0.0).astype(BF16) for h in range(N_HEADS)]
    if latent:
        s0f = _block_diag_heads(s0_ref, 0)
        s0b = _block_diag_heads(s0_ref, N_HEADS)
    for qt in range(t // Q_TILE):
        rows = slice(qt * Q_TILE, (qt + 1) * Q_TILE)
        diff = (_iota((Q_TILE, t), 0) + qt * Q_TILE - _iota((Q_TILE, t), 1)).astype(F32)
        o = jnp.zeros((Q_TILE, W_GROUP), F32)
        for h in range(N_HEADS):
            lgf_h = lgf[:, h * HEAD_DIM:h * HEAD_DIM + 1]
            lgb_h = lgb[:, h * HEAD_DIM:h * HEAD_DIM + 1]
            dmat = (jnp.exp(jnp.where(diff >= 0, diff * lgf_h, NEG_BIG))
                    + jnp.exp(jnp.where(diff <= 0, -diff * lgb_h, NEG_BIG)))
            s = _mm(jnp.where(head == h, q_b[rows], 0.0), kt_b) * dmat
            o = o + _mm(s.astype(BF16), v_heads[h])
        if latent:
            p = pos[rows]
            o = o + jnp.exp((p + 1.0) * lgf) * _bmm(q_b[rows], s0f) + jnp.exp((t - p) * lgb) * _bmm(q_b[rows], s0b)
        oc = o - _head_sums(o, head_sum) * (1.0 / HEAD_DIM)
        on = oc * lax.rsqrt(_head_sums(oc * oc, head_sum) * (1.0 / HEAD_DIM) + EPS)
        o_ref[rows, :] = on * _silu(r[rows, 3 * W_GROUP:])
    v_b = v.astype(BF16)
    sf = _mm_tn((k * jnp.exp((t - 1.0 - pos) * lgf)).astype(BF16), v_b)
    sb = _mm_tn((k * jnp.exp(pos * lgb)).astype(BF16), v_b)
    for h in range(N_HEADS):
        hs = slice(h * HEAD_DIM, (h + 1) * HEAD_DIM)
        sf_h, sb_h = sf[hs, hs], sb[hs, hs]
        if latent:
            sf_h = sf_h + jnp.exp(t * lgf[:, h * HEAD_DIM:h * HEAD_DIM + 1]) * s0_ref[0, h]
            sb_h = sb_h + jnp.exp(t * lgb[:, h * HEAD_DIM:h * HEAD_DIM + 1]) * s0_ref[0, N_HEADS + h]
        sfin_ref[0, h] = sf_h
        sfin_ref[0, N_HEADS + h] = sb_h


def _ret(r_all, logit, s0, cos, sin, t, n_seq, first_block, latent, into=None):
    return _seq_call(
        functools.partial(_ret_kernel, t=t, latent=latent), (r_all, logit, s0, cos, sin),
        in_specs=[pl.BlockSpec((t, 4 * W_GROUP), lambda i: (i + first_block, 0)),
                  pl.BlockSpec((2, W_GROUP), lambda i: (0, 0)),
                  pl.BlockSpec((1, N_GATE, HEAD_DIM, HEAD_DIM), lambda i: (i, 0, 0, 0)),
                  pl.BlockSpec((t, W_GROUP), lambda i: (0, 0)),
                  pl.BlockSpec((t, W_GROUP), lambda i: (0, 0))],
        out_specs=[pl.BlockSpec((t, W_GROUP), lambda i: (i + first_block, 0)),
                   pl.BlockSpec((1, N_GATE, HEAD_DIM, HEAD_DIM), lambda i: (i, 0, 0, 0))],
        out_shape=[jax.ShapeDtypeStruct((N_TOK, W_GROUP), F32),
                   jax.ShapeDtypeStruct((n_seq, N_GATE, HEAD_DIM, HEAD_DIM), F32)],
        into=into, grid=(n_seq,), compiler_params=_cparams("arbitrary"), name="ret")


def _rope_tables(t):
    pos = np.arange(t)
    row = (pos // GRID_W).astype(np.float32)
    col = (pos % GRID_W).astype(np.float32)
    nf = HEAD_DIM // 4
    inv_freq = jnp.power(ROPE_BASE, -jnp.arange(nf, dtype=F32) / nf)
    ang_r = jnp.asarray(row)[:, None] * inv_freq[None, :]
    ang_c = jnp.asarray(col)[:, None] * inv_freq[None, :]
    cos = jnp.concatenate([jnp.cos(ang_r)] * 2 + [jnp.cos(ang_c)] * 2, axis=1)
    sin = jnp.concatenate([-jnp.sin(ang_r), jnp.sin(ang_r), -jnp.sin(ang_c), jnp.sin(ang_c)], axis=1)
    return jnp.tile(cos, (1, N_HEADS)), jnp.tile(sin, (1, N_HEADS))


def _head_rms(x, w, head_sum):
    return x * lax.rsqrt(_head_sums(x * x, head_sum) * (1.0 / HEAD_DIM) + EPS) * w


def _ctx_attn_kernel(n_ref, qw_ref, kw_ref, o_ref, k_out, v_out, *, n_fill):
    n = n_ref[...]
    head_sum = _head_sum_matrix()
    head = _iota((1, W_GROUP), 1) // HEAD_DIM
    q = (_head_rms(n[:, :W_GROUP], qw_ref[...], head_sum) * (HEAD_DIM ** -0.5)).astype(BF16)
    k = _head_rms(n[:, W_GROUP:2 * W_GROUP], kw_ref[...], head_sum)
    v = n[:, 2 * W_GROUP:]
    k_t = k.T
    k_out[0, 0] = k_t
    v_out[0, 0] = v.T
    for later in range(1, 1 + n_fill):
        k_out[0, later] = jnp.zeros((W_GROUP, SEQ), F32)
        v_out[0, later] = jnp.zeros((W_GROUP, SEQ), F32)
    kt_b = k_t.astype(BF16)
    o = jnp.zeros((SEQ, W_GROUP), F32)
    for h in range(N_HEADS):
        s = _mm(jnp.where(head == h, q, 0.0), kt_b)
        p = jnp.exp(s - jnp.max(s, axis=-1, keepdims=True))
        o = o + _mm(p.astype(BF16), jnp.where(head == h, v, 0.0).astype(BF16)) / jnp.sum(p, axis=-1, keepdims=True)
    o_ref[...] = o


def _ctx_attn(n_all, qw, kw, into, layer, caches):
    slots = DEPTH if caches is None else 1
    cache_spec = pl.BlockSpec((1, slots, W_GROUP, SEQ), lambda i: (i, layer if caches is not None else 0, 0, 0))
    cache_shape = jax.ShapeDtypeStruct((BATCH, DEPTH, W_GROUP, SEQ), F32)
    return _seq_call(
        functools.partial(_ctx_attn_kernel, n_fill=slots - 1), (n_all, qw, kw),
        in_specs=[pl.BlockSpec((SEQ, 3 * W_GROUP), lambda i: (i, 0)),
                  pl.BlockSpec((1, W_GROUP), lambda i: (0, 0)),
                  pl.BlockSpec((1, W_GROUP), lambda i: (0, 0))],
        out_specs=[pl.BlockSpec((SEQ, W_GROUP), lambda i: (i, 0)), cache_spec, cache_spec],
        out_shape=[jax.ShapeDtypeStruct((N_TOK, W_GROUP), F32), cache_shape, cache_shape],
        into=into, more_into=() if caches is None else ((caches[0], 1), (caches[1], 2)),
        grid=(BATCH,), compiler_params=_cparams("arbitrary"), name="ctx_attn")


def _nat_kernel(n_ref, ck_ref, cv_ref, bias_ref, qw_ref, kw_ref, o_ref, q_s, kt_s, ckt_s):
    h = pl.program_id(1)

    @pl.when(h == 0)
    def _():
        head_sum = _head_sum_matrix()
        n = n_ref[...]
        q_s[...] = (_head_rms(n[:, :W_GROUP], qw_ref[...], head_sum) * (HEAD_DIM ** -0.5)).astype(BF16)
        kt_s[...] = _head_rms(n[:, W_GROUP:2 * W_GROUP], kw_ref[...], head_sum).T.astype(BF16)
        ckt_s[...] = ck_ref[0].T.astype(BF16)
        o_ref[...] = jnp.zeros_like(o_ref)

    mine = _iota((1, W_GROUP), 1) // HEAD_DIM == h
    v = jnp.where(mine, n_ref[:, 2 * W_GROUP:], 0.0).astype(BF16)
    cv = jnp.where(mine, cv_ref[0], 0.0).astype(BF16)
    for qt in range(DEC_SEQ // Q_TILE):
        rows = slice(qt * Q_TILE, (qt + 1) * Q_TILE)
        q = jnp.where(mine, q_s[rows, :], 0.0)
        grid_rows = range(qt * Q_TILE // GRID_W, (qt + 1) * Q_TILE // GRID_W)
        s_loc = _mm(q, kt_s[...]) + jnp.concatenate([_nat_bias_strip(bias_ref, row) for row in grid_rows], axis=0)
        s_ctx = _mm(q, ckt_s[...])
        m = jnp.maximum(jnp.max(s_loc, axis=-1, keepdims=True), jnp.max(s_ctx, axis=-1, keepdims=True))
        p_loc = jnp.exp(s_loc - m)
        p_ctx = jnp.exp(s_ctx - m)
        den = jnp.sum(p_loc, axis=-1, keepdims=True) + jnp.sum(p_ctx, axis=-1, keepdims=True)
        o_ref[rows, :] += (_mm(p_loc.astype(BF16), v) + _mm(p_ctx.astype(BF16), cv)) / den


def _nat(n_all, ck, cv, bias, qw, kw, into):
    first_block = N_CTX // DEC_SEQ
    return _seq_call(
        _nat_kernel, (n_all, ck, cv, bias, qw, kw),
        in_specs=[pl.BlockSpec((DEC_SEQ, 3 * W_GROUP), lambda b, h: (b + first_block, 0)),
                  pl.BlockSpec((1, PAST_LEN, W_GROUP), lambda b, h: (b, 0, 0)),
                  pl.BlockSpec((1, PAST_LEN, W_GROUP), lambda b, h: (b, 0, 0)),
                  pl.BlockSpec((1, 3 * N_ROW_OFF - 1, GRID_W, 2 * GRID_W), lambda b, h: (h, 0, 0, 0)),
                  pl.BlockSpec((1, W_GROUP), lambda b, h: (0, 0)),
                  pl.BlockSpec((1, W_GROUP), lambda b, h: (0, 0))],
        out_specs=pl.BlockSpec((DEC_SEQ, W_GROUP), lambda b, h: (b + first_block, 0)),
        out_shape=jax.ShapeDtypeStruct((N_TOK, W_GROUP), F32),
        scratch_shapes=[pltpu.VMEM((DEC_SEQ, W_GROUP), BF16), pltpu.VMEM((W_GROUP, DEC_SEQ), BF16),
                        pltpu.VMEM((W_GROUP, PAST_LEN), BF16)],
        into=into, grid=(DEC_BATCH, N_HEADS), compiler_params=_cparams("arbitrary", "arbitrary"), name="nat")


N_ROW_OFF = 2 * WIN_ROWS - 1
NAT_ROWS = DEC_SEQ // GRID_W
NAT_KH = min(WIN_ROWS, NAT_ROWS)
NAT_PAIR, NAT_LOW, NAT_HIGH = 0, N_ROW_OFF - 1, 2 * N_ROW_OFF - 1


def _nat_tables(rpb):
    c = np.arange(GRID_W)
    c0 = np.clip(c - WIN_COLS // 2, 0, GRID_W - WIN_COLS)
    col_in = (c[None, :] >= c0[:, None]) & (c[None, :] < c0[:, None] + WIN_COLS)
    col_idx = np.clip(c[None, :] - c[:, None], -(WIN_COLS - 1), WIN_COLS - 1) + WIN_COLS - 1
    col_hot = (col_idx[..., None] == np.arange(2 * WIN_COLS - 1)).astype(np.float32)
    tz = jnp.where(col_in, jnp.einsum('hab,qkb->haqk', rpb.astype(F32), col_hot, precision=HI), NEG_BIG)
    neg = jnp.full_like(tz, NEG_BIG)
    return jnp.concatenate([jnp.concatenate([tz[:, :-1], tz[:, 1:]], axis=-1),
                            jnp.concatenate([tz, neg], axis=-1), jnp.concatenate([neg, tz], axis=-1)], axis=1)


def _nat_bias_strip(tab_ref, row):
    first_key = min(max(row - NAT_KH // 2, 0), NAT_ROWS - NAT_KH)
    off = first_key - row + WIN_ROWS - 1
    tiles = {}
    done, key = 0, first_key
    if key % 2 == 1:
        tiles[key // 2] = tab_ref[0, NAT_HIGH + off]
        done, key = 1, key + 1
    while done + 1 < NAT_KH:
        tiles[key // 2] = tab_ref[0, NAT_PAIR + off + done]
        done, key = done + 2, key + 2
    if done < NAT_KH:
        tiles[key // 2] = tab_ref[0, NAT_LOW + off + done]
    outside = jnp.full((GRID_W, 2 * GRID_W), NEG_BIG, F32)
    return jnp.concatenate([tiles.get(i, outside) for i in range(NAT_ROWS // 2)], axis=1)


GSEL_LANE = N_EXPERTS


def _pack_router(we, be, wg, bg):
    pad = LANES - N_EXPERTS - N_GROUPS
    w = jnp.concatenate([we, wg, jnp.zeros((D_MODEL, pad), F32)], axis=1)
    b = jnp.concatenate([be, bg, jnp.zeros((pad,), F32)]).reshape(1, LANES)
    hi = w.astype(BF16)
    lo = (w - hi.astype(F32)).astype(BF16)
    return jnp.concatenate([hi, lo], axis=1), b


def _lane_min_where(mask, lane):
    return jnp.min(jnp.where(mask, lane, LANES), axis=-1, keepdims=True)


def _sconv_tile(i, s_ref, before_ref, after_ref, w_ref):
    s = s_ref[...]
    p = s[:, W_GROUP:2 * W_GROUP] * s[:, 2 * W_GROUP:]
    p_before = before_ref[SUBLANES - 1:, W_GROUP:2 * W_GROUP] * before_ref[SUBLANES - 1:, 2 * W_GROUP:]
    p_after = after_ref[:1, W_GROUP:2 * W_GROUP] * after_ref[:1, 2 * W_GROUP:]
    row = _iota((TOK_TILE, 1), 0)
    seq_len = jnp.where(i < N_CTX_TILES, SEQ, DEC_SEQ)
    pos = (i * TOK_TILE + row) & (seq_len - 1)
    prev = jnp.where(row == 0, p_before, pltpu.roll(p, 1, 0))
    nxt = jnp.where(row == TOK_TILE - 1, p_after, pltpu.roll(p, TOK_TILE - 1, 0))
    prev = jnp.where(pos == 0, 0.0, prev)
    nxt = jnp.where(pos == seq_len - 1, 0.0, nxt)
    return s[:, :W_GROUP] * (w_ref[0:1, :] * prev + w_ref[1:2, :] * p + w_ref[2:3, :] * nxt)


def _outproj_kernel(xc_ref, xl_ref, m0, m1, m2, s_ref, before_ref, after_ref, cw_ref, mod_ref, nw_ref, w_ref,
                    rw_ref, rb_ref, x_out, hf_out, route_out):
    tile_i = pl.program_id(0)
    mixed = [m0[...], m1[...], m2[...], _sconv_tile(tile_i, s_ref, before_ref, after_ref, cw_ref)]
    acc = None
    for i, m in enumerate(mixed):
        part = _mm(m.astype(BF16), w_ref[i * W_GROUP:(i + 1) * W_GROUP, :])
        acc = part if acc is None else acc + part
    x = _token_tile(tile_i, xc_ref, xl_ref) + mod_ref[0, 2:3, :] * acc
    x_out[...] = x
    y = x * lax.rsqrt(jnp.mean(x * x, axis=-1, keepdims=True) + EPS) * nw_ref[...]
    hf = y * (1.0 + mod_ref[0, 4:5, :]) + mod_ref[0, 3:4, :]
    hf_hi = hf.astype(BF16)
    hf_out[...] = hf_hi

    hf_lo = (hf - hf_hi.astype(F32)).astype(BF16)
    both = _mm(hf_hi, rw_ref[...])
    logits = both[:, :LANES] + both[:, LANES:] + _mm(hf_lo, rw_ref[:, :LANES]) + rb_ref[...]
    lane = _iota(logits.shape, 1)
    is_g = (lane >= N_EXPERTS) & (lane < N_EXPERTS + N_GROUPS)
    gl = jnp.where(is_g, logits, NEG_BIG)
    ge = jnp.exp(gl - jnp.max(gl, axis=-1, keepdims=True))
    gp = jnp.where(is_g, ge / jnp.sum(ge, axis=-1, keepdims=True), -1.0)
    gw = jnp.max(gp, axis=-1, keepdims=True)
    gsel = _lane_min_where(gp == gw, lane) - N_EXPERTS
    in_grp = (lane // EXPERTS_PER_GROUP == gsel) & (lane < N_EXPERTS)
    el = jnp.where(in_grp, logits, NEG_BIG)
    ee = jnp.exp(el - jnp.max(el, axis=-1, keepdims=True))
    ep = jnp.where(in_grp, ee / jnp.sum(ee, axis=-1, keepdims=True), -1.0)
    t1 = jnp.max(ep, axis=-1, keepdims=True)
    i1 = _lane_min_where(ep == t1, lane)
    ep2 = jnp.where(lane == i1, -1.0, ep)
    t2 = jnp.max(ep2, axis=-1, keepdims=True)
    i2 = _lane_min_where(ep2 == t2, lane)
    tsum = t1 + t2
    combine = jnp.where(lane == i1, gw * (t1 / tsum), 0.0) + jnp.where(lane == i2, gw * (t2 / tsum), 0.0)
    route_out[...] = jnp.where(lane == GSEL_LANE, gsel.astype(F32), combine)


def _outproj(x_ctx, x_lat, split, mixed, a_sc, sc_w, mod, norm_w, w_out, rw, rb):
    tile = lambda w: pl.BlockSpec((TOK_TILE, w), lambda i: (i, 0))
    whole = lambda a: pl.BlockSpec(a.shape, lambda i: (0,) * a.ndim)
    per_tile = TOK_TILE // SUBLANES
    halo = lambda index: pl.BlockSpec((SUBLANES, 3 * W_GROUP), lambda i: (index(i), 0))
    return pl.pallas_call(
        _outproj_kernel,
        grid=(N_TOK // TOK_TILE,),
        in_specs=_token_specs(split) + [tile(W_GROUP)] * 3
                 + [tile(3 * W_GROUP), halo(lambda i: jnp.maximum(i * per_tile - 1, 0)),
                    halo(lambda i: jnp.minimum((i + 1) * per_tile, N_TOK // SUBLANES - 1)), whole(sc_w)]
                 + [pl.BlockSpec((1, SUBLANES, D_MODEL), lambda i: (_cond_of_tile(i), 0, 0)),
                    pl.BlockSpec((1, D_MODEL), lambda i: (0, 0)), whole(w_out), whole(rw), whole(rb)],
        out_specs=[tile(D_MODEL), tile(D_MODEL), tile(LANES)],
        out_shape=[jax.ShapeDtypeStruct((N_TOK, D_MODEL), F32), jax.ShapeDtypeStruct((N_TOK, D_MODEL), BF16),
                   jax.ShapeDtypeStruct((N_TOK, LANES), F32)],
        compiler_params=_cparams("arbitrary"),
        name="outproj",
    )(x_ctx, x_lat, *mixed, a_sc, a_sc, a_sc, sc_w, mod, norm_w.reshape(1, D_MODEL), w_out, rw, rb)


SEG_BLK = 16
LOCAL_ROWS = TOK_TILE + N_GROUPS * SEG_BLK
N_TOK_TILES = N_TOK // TOK_TILE
MOE_ROWS = -(-(N_TOK + N_TOK_TILES * N_GROUPS * (SEG_BLK - 1) + N_GROUPS * (MOE_TILE - 1)) // MOE_TILE) * MOE_TILE


def _moe_tables(gsel):
    groups = jnp.arange(N_GROUPS, dtype=jnp.int32)
    onehot = (gsel.reshape(N_TOK_TILES, TOK_TILE, 1) == groups).astype(jnp.int32)
    rank = jnp.cumsum(onehot, axis=1) - onehot
    nblk = (jnp.sum(onehot, axis=1) + SEG_BLK - 1) // SEG_BLK
    loc_blk = jnp.cumsum(nblk, axis=1) - nblk
    blocks_per_tile = MOE_TILE // SEG_BLK
    grp_tiles = (jnp.sum(nblk, axis=0) + blocks_per_tile - 1) // blocks_per_tile
    grp_tile_start = jnp.cumsum(grp_tiles) - grp_tiles
    dst_blk = grp_tile_start[None, :] * blocks_per_tile + jnp.cumsum(nblk, axis=0) - nblk
    local_pos = jnp.sum(onehot * (loc_blk[:, None, :] * SEG_BLK + rank), axis=2)
    tile_idx = jnp.arange(MOE_ROWS // MOE_TILE, dtype=jnp.int32)
    tile_group = jnp.clip(jnp.sum(tile_idx[:, None] >= grp_tile_start[None, :], axis=1) - 1, 0, N_GROUPS - 1)
    tile_valid = tile_idx < jnp.sum(grp_tiles)
    flat = lambda a: a.reshape(-1).astype(jnp.int32)
    return local_pos.astype(jnp.int32), flat(nblk), flat(loc_blk), flat(dst_blk), flat(tile_group), flat(tile_valid)


def _segment_copies(t, nblk, loc_blk, dst_blk, make_copies, action):
    for g in range(N_GROUPS):
        k = t * N_GROUPS + g

        @pl.loop(0, nblk[k])
        def _(b):
            local = pl.multiple_of((loc_blk[k] + b) * SEG_BLK, SEG_BLK)
            sorted_row = pl.multiple_of((dst_blk[k] + b) * SEG_BLK, SEG_BLK)
            for cp in make_copies(local, sorted_row):
                action(cp)


def _dispatch_kernel(nblk, loc_blk, dst_blk, hf_ref, rt_ref, lp_ref, xs_in, rs_in, xs_hbm, rs_hbm, xbuf, rbuf, sem):
    t = pl.program_id(0)
    slot = t % 2
    onehot = _iota((LOCAL_ROWS, TOK_TILE), 0) == lp_ref[0]
    xbuf[slot] = _mm(onehot.astype(BF16), hf_ref[...]).astype(BF16)
    rbuf[slot] = _mm(onehot.astype(F32), rt_ref[...], HI)

    def copies_of(s):
        def copies(local, sorted_row):
            return (pltpu.make_async_copy(xbuf.at[s, pl.ds(local, SEG_BLK)], xs_hbm.at[pl.ds(sorted_row, SEG_BLK)], sem.at[s]),
                    pltpu.make_async_copy(rbuf.at[s, pl.ds(local, SEG_BLK)], rs_hbm.at[pl.ds(sorted_row, SEG_BLK)], sem.at[s]))
        return copies

    @pl.when(t > 0)
    def _():
        _segment_copies(t - 1, nblk, loc_blk, dst_blk, copies_of(1 - slot), lambda cp: cp.wait())

    _segment_copies(t, nblk, loc_blk, dst_blk, copies_of(slot), lambda cp: cp.start())

    @pl.when(t == N_TOK_TILES - 1)
    def _():
        _segment_copies(t, nblk, loc_blk, dst_blk, copies_of(slot), lambda cp: cp.wait())


def _dispatch(hf, route, local_pos, nblk, loc_blk, dst_blk):
    grid_spec = pltpu.PrefetchScalarGridSpec(
        num_scalar_prefetch=3,
        grid=(N_TOK_TILES,),
        in_specs=[pl.BlockSpec((TOK_TILE, D_MODEL), lambda t, *_: (t, 0)),
                  pl.BlockSpec((TOK_TILE, LANES), lambda t, *_: (t, 0)),
                  pl.BlockSpec((1, 1, TOK_TILE), lambda t, *_: (t, 0, 0)),
                  pl.BlockSpec(memory_space=pl.ANY), pl.BlockSpec(memory_space=pl.ANY)],
        out_specs=[pl.BlockSpec(memory_space=pl.ANY), pl.BlockSpec(memory_space=pl.ANY)],
        scratch_shapes=[pltpu.VMEM((2, LOCAL_ROWS, D_MODEL), BF16), pltpu.VMEM((2, LOCAL_ROWS, LANES), F32),
                        pltpu.SemaphoreType.DMA((2,))],
    )
    return pl.pallas_call(
        _dispatch_kernel,
        grid_spec=grid_spec,
        out_shape=[jax.ShapeDtypeStruct((MOE_ROWS, D_MODEL), BF16), jax.ShapeDtypeStruct((MOE_ROWS, LANES), F32)],
        input_output_aliases={6: 0, 7: 1},
        compiler_params=_cparams("arbitrary"),
        name="dispatch",
    )(nblk, loc_blk, dst_blk, hf, route, local_pos.reshape(N_TOK_TILES, 1, TOK_TILE),
      jnp.zeros((MOE_ROWS, D_MODEL), BF16), jnp.zeros((MOE_ROWS, LANES), F32))


def _moe_kernel(tile_group, tile_valid, x_ref, r_ref, wg_hbm, wu_hbm, wd_hbm, y_ref,
                wg_b, wu_b, wd_b, stage_g, stage_u, stage_d, sem, *, layer):
    i = pl.program_id(0)
    g = tile_group[i]
    group_row = layer * N_GROUPS + g
    new_group = (i == 0) | (g != tile_group[jnp.maximum(i - 1, 0)])
    valid = tile_valid[i] > 0

    def weight_copies(e):
        slot = e % 2
        return (pltpu.make_async_copy(wg_hbm.at[group_row, e], stage_g.at[slot], sem.at[slot]),
                pltpu.make_async_copy(wu_hbm.at[group_row, e], stage_u.at[slot], sem.at[slot]),
                pltpu.make_async_copy(wd_hbm.at[group_row, e], stage_d.at[slot], sem.at[slot]))

    def run(load_weights):
        x = x_ref[...]
        route = r_ref[...]
        lane = _iota(route.shape, 1)
        acc = jnp.zeros((MOE_TILE, D_MODEL), F32)
        if load_weights:
            for e in range(2):
                for cp in weight_copies(e):
                    cp.start()
        for e in range(EXPERTS_PER_GROUP):
            if load_weights:
                for cp in weight_copies(e):
                    cp.wait()
                wg_b[e] = stage_g[e % 2].astype(BF16)
                wu_b[e] = stage_u[e % 2].astype(BF16)
                wd_b[e] = stage_d[e % 2].astype(BF16)
                if e + 2 < EXPERTS_PER_GROUP:
                    for cp in weight_copies(e + 2):
                        cp.start()
            cw = jnp.sum(jnp.where(lane == g * EXPERTS_PER_GROUP + e, route, 0.0), axis=-1, keepdims=True)
            act = _silu(_mm(x, wg_b[e])) * _mm(x, wu_b[e]) * cw
            acc = acc + _mm(act.astype(BF16), wd_b[e])
        y_ref[...] = acc

    @pl.when(valid & new_group)
    def _():
        run(True)

    @pl.when(valid & jnp.logical_not(new_group))
    def _():
        run(False)

    @pl.when(jnp.logical_not(valid))
    def _():
        y_ref[...] = jnp.zeros_like(y_ref)


def _moe(xs, rs, tile_group, tile_valid, wg, wu, wd, layer):
    any_spec = pl.BlockSpec(memory_space=pl.ANY)
    grid_spec = pltpu.PrefetchScalarGridSpec(
        num_scalar_prefetch=2,
        grid=(MOE_ROWS // MOE_TILE,),
        in_specs=[pl.BlockSpec((MOE_TILE, D_MODEL), lambda i, tg, tv: (i, 0)),
                  pl.BlockSpec((MOE_TILE, LANES), lambda i, tg, tv: (i, 0)),
                  any_spec, any_spec, any_spec],
        out_specs=pl.BlockSpec((MOE_TILE, D_MODEL), lambda i, tg, tv: (i, 0)),
        scratch_shapes=[pltpu.VMEM((EXPERTS_PER_GROUP, D_MODEL, EXPERT_FF), BF16),
                        pltpu.VMEM((EXPERTS_PER_GROUP, D_MODEL, EXPERT_FF), BF16),
                        pltpu.VMEM((EXPERTS_PER_GROUP, EXPERT_FF, D_MODEL), BF16),
                        pltpu.VMEM((2, D_MODEL, EXPERT_FF), F32), pltpu.VMEM((2, D_MODEL, EXPERT_FF), F32),
                        pltpu.VMEM((2, EXPERT_FF, D_MODEL), F32), pltpu.SemaphoreType.DMA((2,))],
    )
    return pl.pallas_call(
        functools.partial(_moe_kernel, layer=layer),
        grid_spec=grid_spec,
        out_shape=jax.ShapeDtypeStruct((MOE_ROWS, D_MODEL), F32),
        compiler_params=_cparams("arbitrary"),
        name="moe",
    )(tile_group, tile_valid, xs, rs, wg, wu, wd)


def _combine_kernel(nblk, loc_blk, dst_blk, x_ref, lp_ref, mod_ref, ys_hbm, *refs, split):
    o_refs, ybuf, sem = refs[:-2], refs[-2], refs[-1]
    t = pl.program_id(0)
    slot = t % 2

    def fetch(tile, s):
        def copies(local, sorted_row):
            return (pltpu.make_async_copy(ys_hbm.at[pl.ds(sorted_row, SEG_BLK)], ybuf.at[s, pl.ds(local, SEG_BLK)], sem.at[s]),)
        ybuf[s] = jnp.zeros((LOCAL_ROWS, D_MODEL), F32)
        _segment_copies(tile, nblk, loc_blk, dst_blk, copies, lambda cp: cp.start())

    @pl.when(t == 0)
    def _():
        fetch(0, 0)

    @pl.when(t + 1 < N_TOK_TILES)
    def _():
        fetch(t + 1, 1 - slot)

    def copies_now(local, sorted_row):
        return (pltpu.make_async_copy(ys_hbm.at[pl.ds(sorted_row, SEG_BLK)], ybuf.at[slot, pl.ds(local, SEG_BLK)], sem.at[slot]),)

    _segment_copies(t, nblk, loc_blk, dst_blk, copies_now, lambda cp: cp.wait())

    onehot = (_iota((TOK_TILE, LOCAL_ROWS), 1) == lp_ref[...]).astype(BF16)
    ys = ybuf[slot]
    hi = ys.astype(BF16)
    lo = (ys - hi.astype(F32)).astype(BF16)
    y = _mm(onehot, hi) + _mm(onehot, lo)
    out = x_ref[...] + mod_ref[0, 5:6, :] * y
    if split:
        @pl.when(t < N_CTX_TILES)
        def _():
            o_refs[0][...] = out

        @pl.when(t >= N_CTX_TILES)
        def _():
            o_refs[1][...] = out
    else:
        o_refs[0][...] = out


def _combine(x, ys, mod, local_pos, nblk, loc_blk, dst_blk, split):
    tile = pl.BlockSpec((TOK_TILE, D_MODEL), lambda t, *_: (t, 0))
    if split:
        out_specs = _token_specs(True)
        out_shape = [jax.ShapeDtypeStruct((N_CTX, D_MODEL), F32), jax.ShapeDtypeStruct((N_LAT, D_MODEL), F32)]
    else:
        out_specs, out_shape = [tile], [jax.ShapeDtypeStruct((N_TOK, D_MODEL), F32)]
    grid_spec = pltpu.PrefetchScalarGridSpec(
        num_scalar_prefetch=3,
        grid=(N_TOK_TILES,),
        in_specs=[tile, pl.BlockSpec((TOK_TILE, 1), lambda t, *_: (t, 0)),
                  pl.BlockSpec((1, SUBLANES, D_MODEL), lambda t, *_: (_cond_of_tile(t), 0, 0)),
                  pl.BlockSpec(memory_space=pl.ANY)],
        out_specs=out_specs,
        scratch_shapes=[pltpu.VMEM((2, LOCAL_ROWS, D_MODEL), F32), pltpu.SemaphoreType.DMA((2,))],
    )
    return pl.pallas_call(
        functools.partial(_combine_kernel, split=split),
        grid_spec=grid_spec,
        out_shape=out_shape,
        compiler_params=_cparams("arbitrary"),
        name="combine",
    )(nblk, loc_blk, dst_blk, x, local_pos.reshape(N_TOK, 1), mod, ys)


def _lane_row(v):
    v = v.reshape(-1).astype(F32)
    return jnp.concatenate([v, jnp.zeros((LANES - v.shape[0],), F32)]).reshape(1, LANES)


def _pad_rows(w):
    return jnp.concatenate([w, jnp.zeros((SUBLANES - w.shape[0], w.shape[1]), w.dtype)], axis=0)


def kernel(x_prompt, x_sample, state_gdn, state_ret, cache_nat_k, cache_nat_v, c, c_ctx, ada_w, ada_b, norm_mix_w, norm_ffn_w, w_in, gdn_conv_w, gdn_a_log, gdn_dt_bias, gdn_norm_w, ret_gamma_logit, nat_q_norm_w, nat_k_norm_w, nat_rpb, sc_conv_w, w_out, router_group_w, router_group_b, router_expert_w, router_expert_b, moe_w_gate, moe_w_up, moe_w_down):
    x_ctx, x_lat, split = x_prompt.reshape(N_CTX, D_MODEL), x_sample.reshape(N_LAT, D_MODEL), True
    cond = jnp.concatenate([c_ctx[None, :], c], axis=0)
    ada = _ada(cond, ada_w, ada_b).reshape(DEPTH, SUBLANES, 6, D_MODEL)
    cos, sin = _rope_tables(DEC_SEQ)
    zero_state = jnp.zeros((BATCH, N_GATE, HEAD_DIM, HEAD_DIM), F32)
    lat_block = N_CTX // DEC_SEQ
    gdn_list, ret_list, caches = [], [], None
    mixed = [jnp.zeros((N_TOK, W_GROUP), F32) for _ in range(3)]
    for l in range(DEPTH):
        mod = jnp.concatenate([ada[l, :1 + DEC_BATCH], jnp.zeros((1 + DEC_BATCH, SUBLANES - 6, D_MODEL), F32)], axis=1)
        a_gdn, a_ret, a_nat, a_sc, a_gate = _inproj(x_ctx, x_lat, split, mod, norm_mix_w[l], w_in, l)

        conv_w = _pad_rows(gdn_conv_w[l])
        a_log, dt_b = _lane_row(gdn_a_log[l]), _lane_row(gdn_dt_bias[l])
        gnw = jnp.tile(gdn_norm_w[l], N_HEADS).reshape(1, W_GROUP)
        o_gdn, s_gdn = _gdn(a_gdn, a_gate, conv_w, a_log, dt_b, gnw, zero_state, SEQ, BATCH, 0, n_sub=2,
                            into=mixed[0])
        s0 = state_gdn[:, l].reshape(DEC_BATCH, N_GATE, HEAD_DIM, HEAD_DIM)
        o_gdn, _ = _gdn(a_gdn, a_gate, conv_w, a_log, dt_b, gnw, s0, DEC_SEQ, DEC_BATCH, lat_block, into=o_gdn)

        logit = jnp.repeat(ret_gamma_logit[l].astype(F32), HEAD_DIM, axis=1)
        o_ret, s_ret = _ret(a_ret, logit, zero_state, cos[:SEQ], sin[:SEQ], SEQ, BATCH, 0, False, into=mixed[1])
        s0 = state_ret[:, l].reshape(DEC_BATCH, N_GATE, HEAD_DIM, HEAD_DIM)
        o_ret, _ = _ret(a_ret, logit, s0, cos, sin, DEC_SEQ, DEC_BATCH, lat_block, True, into=o_ret)

        qw = jnp.tile(nat_q_norm_w[l], N_HEADS).reshape(1, W_GROUP)
        kw = jnp.tile(nat_k_norm_w[l], N_HEADS).reshape(1, W_GROUP)
        o_nat, *caches = _ctx_attn(a_nat, qw, kw, mixed[2], l, caches)
        o_nat = _nat(a_nat, cache_nat_k[:, l].reshape(DEC_BATCH, PAST_LEN, W_GROUP),
                     cache_nat_v[:, l].reshape(DEC_BATCH, PAST_LEN, W_GROUP), _nat_tables(nat_rpb[l]), qw, kw, o_nat)

        mixed = [o_gdn, o_ret, o_nat]
        rw, rb = _pack_router(router_expert_w[l], router_expert_b[l], router_group_w[l], router_group_b[l])
        x_mid, hf, route = _outproj(x_ctx, x_lat, split, mixed, a_sc, _pad_rows(sc_conv_w[l]), mod, norm_ffn_w[l],
                                    w_out[l].astype(BF16), rw, rb)

        local_pos, nblk, loc_blk, dst_blk, tile_group, tile_valid = _moe_tables(route[:, GSEL_LANE].astype(jnp.int32))
        xs, rs = _dispatch(hf, route, local_pos, nblk, loc_blk, dst_blk)
        to_group = lambda w: w.reshape((DEPTH * N_GROUPS, EXPERTS_PER_GROUP) + w.shape[2:])
        ys = _moe(xs, rs, tile_group, tile_valid, to_group(moe_w_gate), to_group(moe_w_up), to_group(moe_w_down), l)
        last = l == DEPTH - 1
        out = _combine(x_mid, ys, mod, local_pos, nblk, loc_blk, dst_blk, split=last)
        x_ctx, x_lat, split = (out[0], out[1], True) if last else (out[0], out[0], False)

        gdn_list.append(s_gdn.reshape(BATCH, 2, N_HEADS, HEAD_DIM, HEAD_DIM))
        ret_list.append(s_ret.reshape(BATCH, 2, N_HEADS, HEAD_DIM, HEAD_DIM))
    new_k, new_v = [a.reshape(BATCH, DEPTH, N_HEADS, HEAD_DIM, SEQ).transpose(0, 1, 4, 2, 3) for a in caches]
    return (x_ctx.reshape(BATCH, SEQ, D_MODEL), x_lat.reshape(DEC_BATCH, DEC_SEQ, D_MODEL),
            jnp.stack(gdn_list, axis=1), jnp.stack(ret_list, axis=1), new_k, new_v)
```

```python
import functools

import numpy as np
import jax
import jax.numpy as jnp
from jax import lax
from jax.experimental import pallas as pl
from jax.experimental.pallas import tpu as pltpu

D_MODEL = 1024
BATCH = 16
SEQ = 256
DEPTH = 2
DEC_BATCH = 2
DEC_SEQ = 1024
PAST_LEN = 256
GRID_W = 64
HEAD_DIM = 64
W_GROUP = D_MODEL // 4
N_HEADS = W_GROUP // HEAD_DIM
CHUNK = 64
WIN_ROWS = 8
WIN_COLS = 16
ROPE_BASE = 10000.0
N_GROUPS = 4
EXPERTS_PER_GROUP = 8
N_EXPERTS = N_GROUPS * EXPERTS_PER_GROUP
EXPERT_FF = 256
GROUP_FF = EXPERTS_PER_GROUP * EXPERT_FF
EPS = 1e-6

N_CTX = BATCH * SEQ
N_LAT = DEC_BATCH * DEC_SEQ
N_TOK = N_CTX + N_LAT
LANES = 128
SUBLANES = 8
TOK_TILE = 512
MOE_TILE = 512
Q_TILE = 256
VMEM_LIMIT = 48 * 1024 * 1024
NEG_BIG = -1e30
N_GATE = 2 * N_HEADS

F32 = jnp.float32
BF16 = jnp.bfloat16
HI = lax.Precision.HIGHEST


def _mm(a, b, prec=None):
    return lax.dot_general(a, b, (((1,), (0,)), ((), ())), precision=prec, preferred_element_type=F32)


def _mm_nt(a, b, prec=None):
    return lax.dot_general(a, b, (((1,), (1,)), ((), ())), precision=prec, preferred_element_type=F32)


def _mm_tn(a, b, prec=None):
    return lax.dot_general(a, b, (((0,), (0,)), ((), ())), precision=prec, preferred_element_type=F32)


def _bmm(a, b):
    return _mm(a.astype(BF16), b.astype(BF16))


def _sigmoid(x):
    return 1.0 / (1.0 + jnp.exp(-x))


def _silu(x):
    return x * _sigmoid(x)


def _softplus(x):
    return jnp.maximum(x, 0.0) + jnp.log(1.0 + jnp.exp(-jnp.abs(x)))


def _iota(shape, dim):
    return lax.broadcasted_iota(jnp.int32, shape, dim)


def _cparams(*sem):
    return pltpu.CompilerParams(dimension_semantics=sem, vmem_limit_bytes=VMEM_LIMIT)


def _cond_of_tile(i):
    n_ctx_tiles = N_CTX // TOK_TILE
    return jnp.where(i < n_ctx_tiles, 0, 1 + (i - n_ctx_tiles) // (DEC_SEQ // TOK_TILE))


def _head_sum_matrix():
    return (_iota((W_GROUP, W_GROUP), 0) // HEAD_DIM == _iota((W_GROUP, W_GROUP), 1) // HEAD_DIM).astype(BF16)


def _head_sums(x, head_sum):
    hi = x.astype(BF16)
    lo = (x - hi.astype(F32)).astype(BF16)
    return _mm(hi, head_sum) + _mm(lo, head_sum)


N_COND = 1 + DEC_BATCH
ADA_TN = 1536


def _ada_kernel(c_ref, w_ref, b_ref, o_ref):
    def slab(s, accs):
        rows = pl.ds(pl.multiple_of(s * SUBLANES, SUBLANES), SUBLANES)
        w = w_ref[0, rows, :]
        return tuple(acc + w * jnp.tile(_silu(c_ref[r, rows, :]), (1, ADA_TN // LANES))
                     for r, acc in enumerate(accs))

    zero = jnp.zeros((SUBLANES, ADA_TN), F32)
    accs = lax.fori_loop(0, D_MODEL // SUBLANES, slab, (zero,) * N_COND, unroll=4)
    out = jnp.concatenate([jnp.sum(acc, axis=0, keepdims=True) for acc in accs]
                          + [jnp.zeros((SUBLANES - N_COND, ADA_TN), F32)], axis=0)
    o_ref[0] = out + b_ref[0]


def _ada(cond, ada_w, ada_b):
    n_out = 6 * D_MODEL
    cond_lanes = jnp.broadcast_to(cond[:, :, None], (N_COND, D_MODEL, LANES))
    return pl.pallas_call(
        _ada_kernel,
        grid=(DEPTH, n_out // ADA_TN),
        in_specs=[pl.BlockSpec((N_COND, D_MODEL, LANES), lambda l, j: (0, 0, 0)),
                  pl.BlockSpec((1, D_MODEL, ADA_TN), lambda l, j: (l, 0, j)),
                  pl.BlockSpec((1, 1, ADA_TN), lambda l, j: (l, 0, j))],
        out_specs=pl.BlockSpec((1, SUBLANES, ADA_TN), lambda l, j: (l, 0, j)),
        out_shape=jax.ShapeDtypeStruct((DEPTH, SUBLANES, n_out), F32),
        compiler_params=_cparams("arbitrary", "arbitrary"),
        name="ada",
    )(cond_lanes, ada_w, ada_b.reshape(DEPTH, 1, n_out))


IN_WIDTHS = (4 * W_GROUP, 4 * W_GROUP, 3 * W_GROUP, 3 * W_GROUP, LANES)
IN_PACKED = sum(IN_WIDTHS)


IN_TOTAL = 3 * W_GROUP + W_GROUP + 2 * N_GATE + 4 * W_GROUP + 3 * W_GROUP + 3 * W_GROUP
IN_GATE_SRC = 4 * W_GROUP
IN_SRC = (0, IN_GATE_SRC + 2 * N_GATE, IN_GATE_SRC + 2 * N_GATE + 4 * W_GROUP,
          IN_GATE_SRC + 2 * N_GATE + 7 * W_GROUP)
N_CTX_TILES = N_CTX // TOK_TILE


def _token_specs(split):
    lat0 = 0 if split else N_CTX_TILES
    return [pl.BlockSpec((TOK_TILE, D_MODEL), lambda i, *_: (jnp.minimum(i, N_CTX_TILES - 1), 0)),
            pl.BlockSpec((TOK_TILE, D_MODEL), lambda i, *_: (jnp.maximum(i, N_CTX_TILES) - N_CTX_TILES + lat0, 0))]


def _token_tile(i, ctx_ref, lat_ref):
    return jnp.where(i < N_CTX_TILES, ctx_ref[...], lat_ref[...])


def _inproj_kernel(xc_ref, xl_ref, mod_ref, nw_ref, w_ref, *refs):
    o_refs, w_s = refs[:-1], refs[-1]
    i = pl.program_id(0)

    @pl.when(i == 0)
    def _():
        piece = 256
        off = 0
        for src, width in zip(IN_SRC, IN_WIDTHS[:-1]):
            for c in range(0, width, piece):
                w_s[:, off + c:off + c + piece] = w_ref[0, src + c:src + c + piece, :].T.astype(BF16)
            off += width
        gate = w_ref[0, IN_GATE_SRC:IN_GATE_SRC + LANES, :].T
        w_s[:, off:] = jnp.where(_iota((D_MODEL, LANES), 1) < 2 * N_GATE, gate, 0.0).astype(BF16)

    x = _token_tile(i, xc_ref, xl_ref)
    y = x * lax.rsqrt(jnp.mean(x * x, axis=-1, keepdims=True) + EPS) * nw_ref[...]
    h = (y * (1.0 + mod_ref[0, 1:2, :]) + mod_ref[0, 0:1, :]).astype(BF16)
    off = 0
    for o_ref, width in zip(o_refs, IN_WIDTHS):
        o_ref[...] = _mm(h, w_s[:, off:off + width])
        off += width


def _inproj(x_ctx, x_lat, split, mod, norm_w, w_in, layer):
    return pl.pallas_call(
        _inproj_kernel,
        grid=(N_TOK // TOK_TILE,),
        in_specs=_token_specs(split)
                 + [pl.BlockSpec((1, SUBLANES, D_MODEL), lambda i: (_cond_of_tile(i), 0, 0)),
                    pl.BlockSpec((1, D_MODEL), lambda i: (0, 0)),
                    pl.BlockSpec((1, IN_TOTAL, D_MODEL), lambda i: (layer, 0, 0), pipeline_mode=pl.Buffered(1))],
        out_specs=[pl.BlockSpec((TOK_TILE, w), lambda i: (i, 0)) for w in IN_WIDTHS],
        out_shape=[jax.ShapeDtypeStruct((N_TOK, w), F32) for w in IN_WIDTHS],
        scratch_shapes=[pltpu.VMEM((D_MODEL, IN_PACKED), BF16)],
        compiler_params=_cparams("arbitrary"),
        name="inproj",
    )(x_ctx, x_lat, mod, norm_w.reshape(1, D_MODEL), jnp.swapaxes(w_in, 1, 2))


def _seq_call(kernel_fn, args, in_specs, out_specs, out_shape, into, more_into=(), **kwargs):
    donors = ([] if into is None else [(into, 0)]) + list(more_into)
    n_in = len(args)
    inner = kernel_fn
    kernel_fn = lambda *refs: inner(*refs[:n_in], *refs[n_in + len(donors):])
    aliases = {n_in + j: out_idx for j, (_, out_idx) in enumerate(donors)}
    args = list(args) + [a for a, _ in donors]
    in_specs = list(in_specs) + [pl.BlockSpec(memory_space=pl.ANY)] * len(donors)
    return pl.pallas_call(kernel_fn, in_specs=in_specs, out_specs=out_specs, out_shape=out_shape,
                          input_output_aliases=aliases, **kwargs)(*args)


def _shift_rows(p, seq_len):
    rows = p.shape[0]
    pos = _iota(p.shape, 0) & (seq_len - 1)
    prev = jnp.where(pos == 0, 0.0, pltpu.roll(p, 1, 0))
    nxt = jnp.where(pos == seq_len - 1, 0.0, pltpu.roll(p, rows - 1, 0))
    return prev, nxt


def _conv3(x, w_ref, seq_len):
    prev, nxt = _shift_rows(x, seq_len)
    return w_ref[0:1, :] * prev + w_ref[1:2, :] * x + w_ref[2:3, :] * nxt


def _chunk_scan(x, reverse):
    t = x.shape[0]
    pos = _iota(x.shape, 0) % CHUNK
    step = 1
    while step < CHUNK:
        if reverse:
            x = x + jnp.where(pos < CHUNK - step, pltpu.roll(x, t - step, 0), 0.0)
        else:
            x = x + jnp.where(pos >= step, pltpu.roll(x, step, 0), 0.0)
        step *= 2
    return x


GDN_GROUP_CHUNKS = 4
GDN_CHAINS = GDN_GROUP_CHUNKS * N_GATE
GDN_PAIRS = GDN_GROUP_CHUNKS * N_HEADS


def _gdn_kernel(a_ref, gate_ref, convw_ref, alog_ref, dtb_ref, nw_ref, s0_ref, o_ref, sfin_ref,
                q_s, kv_s, kt_s, gc_s, eg_s, beta_s, gcrow_s, ekdrow_s, cdec_s, uo_s, wq_s, attn_s, kdt_s,
                st_s, wsqs_s, kk_s, d_s, m1_s, p_s, n2_s, ob_s, d4_s, m3_s, rhs_s, px_s, *, t, n_sub):
    n_chunks = t // CHUNK
    a = a_ref[...]
    qkv = _silu(_conv3(a[:, :3 * W_GROUP], convw_ref, t // n_sub))
    q = qkv[:, :W_GROUP]
    k = qkv[:, W_GROUP:2 * W_GROUP]
    v = qkv[:, 2 * W_GROUP:]
    head_sum = _head_sum_matrix()
    q = q * lax.rsqrt(_head_sums(q * q, head_sum) + EPS) * (HEAD_DIM ** -0.5)
    k = k * lax.rsqrt(_head_sums(k * k, head_sum) + EPS)
    for h in range(N_HEADS):
        hs = slice(h * HEAD_DIM, (h + 1) * HEAD_DIM)
        q_s[h] = q[:, hs]
        kv_s[h] = jnp.concatenate([k[:, hs], v[:, hs]], axis=1)
    k_t = k.T
    for c in range(n_chunks):
        kt_s[c] = k_t[:, c * CHUNK:(c + 1) * CHUNK]

    gates = gate_ref[...]
    log_a = -jnp.exp(alog_ref[...]) * _softplus(gates + dtb_ref[...])
    beta_s[...] = _sigmoid(gates)

    ci = _iota((CHUNK, CHUNK), 0)
    cj = _iota((CHUNK, CHUNK), 1)
    eye = (ci == cj).astype(F32)
    blk_mask = (ci // 16) == (cj // 16)
    low_half = _iota((CHUNK, 2 * HEAD_DIM), 1) < HEAD_DIM

    prefix = _chunk_scan(log_a, reverse=False)
    suffix = _chunk_scan(log_a, reverse=True)
    gc = jnp.where(_iota((t, LANES), 1) < N_HEADS, prefix, suffix)
    gt = prefix + suffix - log_a
    gc_s[...] = gc
    eg_s[...] = jnp.exp(gc)
    gc_t = gc.T
    ekd_t = jnp.exp(gt - gc).T
    cdec_t = jnp.exp(gt).T
    for c in range(n_chunks):
        lanes = slice(c * CHUNK, (c + 1) * CHUNK)
        gcrow_s[c] = gc_t[:N_GATE, lanes]
        ekdrow_s[c] = ekd_t[:N_GATE, lanes]
        cdec_s[c] = jnp.concatenate([cdec_t[:N_GATE, lanes]] * 2, axis=1)

    def solve_group(grp, carry):
        row0 = grp * (GDN_GROUP_CHUNKS * CHUNK)
        chains = [(cl, a_idx) for cl in range(GDN_GROUP_CHUNKS) for a_idx in range(N_GATE)]

        def rows_of(cl):
            return pl.ds(pl.multiple_of(row0 + cl * CHUNK, CHUNK), CHUNK)

        for cl in range(GDN_GROUP_CHUNKS):
            for h in range(N_HEADS):
                rows = rows_of(cl)
                kq = jnp.concatenate([kv_s[h, rows, :HEAD_DIM], q_s[h, rows, :]], axis=0)
                k_t_h = kt_s[grp * GDN_GROUP_CHUNKS + cl, h * HEAD_DIM:(h + 1) * HEAD_DIM, :]
                kk_s[cl * N_HEADS + h] = _bmm(kq, k_t_h)
        for b, (cl, a_idx) in enumerate(chains):
            backward = a_idx >= N_HEADS
            h = a_idx % N_HEADS
            rows, c = rows_of(cl), grp * GDN_GROUP_CHUNKS + cl
            incl = (cj >= ci) if backward else (cj <= ci)
            strict = (cj > ci) if backward else (cj < ci)
            bt = beta_s[rows, N_GATE + a_idx:N_GATE + a_idx + 1]
            decay = jnp.exp(jnp.where(incl, gc_s[rows, a_idx:a_idx + 1] - gcrow_s[c, a_idx:a_idx + 1, :], NEG_BIG))
            low = jnp.where(strict, kk_s[cl * N_HEADS + h, :CHUNK, :] * bt * decay, 0.0)
            attn_s[a_idx, rows, :] = (kk_s[cl * N_HEADS + h, CHUNK:, :] * decay).astype(BF16)
            d = jnp.where(blk_mask, low, 0.0)
            d_s[b] = d
            ob_s[b] = (low - d).astype(BF16)
            rhs_s[b] = (kv_s[h, rows, :] * bt
                        * jnp.where(low_half, eg_s[rows, a_idx:a_idx + 1], 1.0)).astype(BF16)
        for b in range(GDN_CHAINS):
            m1_s[b] = _bmm(d_s[b], d_s[b])
        for b in range(GDN_CHAINS):
            d, d2 = d_s[b], m1_s[b]
            d4_s[b] = _bmm(d2, d2).astype(BF16)
            p_s[b] = eye - d + d2 - _bmm(d, d2)
        for b in range(GDN_CHAINS):
            d4, p = d4_s[b], p_s[b]
            m3_s[b] = _mm(d4, d4).astype(BF16)
            p_s[b] = p + _mm(p.astype(BF16), d4)
        for b in range(GDN_CHAINS):
            p = p_s[b]
            p_s[b] = p + _mm(p.astype(BF16), m3_s[b])
        for b in range(GDN_CHAINS):
            p = p_s[b].astype(BF16)
            m1_s[b] = _mm(p, ob_s[b])
            px_s[b] = _mm(p, rhs_s[b]).astype(BF16)
        for b in range(GDN_CHAINS):
            n2_s[b] = _bmm(m1_s[b], m1_s[b])
        for b in range(GDN_CHAINS):
            n, n2 = m1_s[b], n2_s[b]
            m3_s[b] = (eye - n + n2 - _bmm(n, n2)).astype(BF16)
        for b, (cl, a_idx) in enumerate(chains):
            h = a_idx % N_HEADS
            rows, c = rows_of(cl), grp * GDN_GROUP_CHUNKS + cl
            wu = _mm(m3_s[b], px_s[b])
            uo_s[a_idx, rows, :] = wu
            wq_s[a_idx, c, :CHUNK, :] = wu[:, :HEAD_DIM].astype(BF16)
            wq_s[a_idx, c, CHUNK:, :] = (q_s[h, rows, :] * eg_s[rows, a_idx:a_idx + 1]).astype(BF16)
            k_t_h = kt_s[c, h * HEAD_DIM:(h + 1) * HEAD_DIM, :]
            kdt_s[a_idx, c] = (k_t_h * ekdrow_s[c, a_idx:a_idx + 1, :]).astype(BF16)
        return carry

    lax.fori_loop(0, n_chunks // GDN_GROUP_CHUNKS, solve_group, 0)

    seq_chunks = n_chunks // n_sub
    chains = [(sub, a_idx) for sub in range(n_sub) for a_idx in range(N_GATE)]
    for j, (sub, a_idx) in enumerate(chains):
        st_s[j] = jnp.concatenate([jnp.zeros((HEAD_DIM, HEAD_DIM), F32), s0_ref[sub, a_idx]], axis=1)

    def scan_chunk(c, carry):
        def chunk_of(sub, a_idx):
            return sub * seq_chunks + ((seq_chunks - 1 - c) if a_idx >= N_HEADS else c)

        for j, (sub, a_idx) in enumerate(chains):
            wsqs_s[j] = _mm(wq_s[a_idx, chunk_of(sub, a_idx)], st_s[j].astype(BF16))
        for j, (sub, a_idx) in enumerate(chains):
            cc = chunk_of(sub, a_idx)
            rows = pl.ds(pl.multiple_of(cc * CHUNK, CHUNK), CHUNK)
            v_new = (uo_s[a_idx, rows, :] - wsqs_s[j, :CHUNK, :]).astype(BF16)
            uo_s[a_idx, rows, :] = wsqs_s[j, CHUNK:, :] + _mm(attn_s[a_idx, rows, :], v_new)
            st_s[j] = st_s[j] * cdec_s[cc, a_idx:a_idx + 1, :] + _mm(kdt_s[a_idx, cc], v_new)
        return carry

    lax.fori_loop(0, seq_chunks, scan_chunk, 0)
    for j, (sub, a_idx) in enumerate(chains):
        sfin_ref[sub, a_idx] = st_s[j, :, HEAD_DIM:]

    o = jnp.concatenate([(uo_s[h] + uo_s[N_HEADS + h])[:, HEAD_DIM:] for h in range(N_HEADS)], axis=1)
    ms = _head_sums(o * o, head_sum) * (1.0 / HEAD_DIM)
    o_ref[...] = o * lax.rsqrt(ms + EPS) * nw_ref[...] * _silu(a[:, 3 * W_GROUP:])


def _gdn(a_all, gate_all, conv_w, a_log, dt_bias, norm_w, s0, seq_len, n_seq, first_block, n_sub=1, into=None):
    small = lambda: pl.BlockSpec((1, LANES), lambda i: (0, 0))
    t = n_sub * seq_len
    n_chunks = t // CHUNK
    wide = 2 * HEAD_DIM
    scratch = [pltpu.VMEM((N_HEADS, t, HEAD_DIM), F32),
               pltpu.VMEM((N_HEADS, t, wide), F32),
               pltpu.VMEM((n_chunks, W_GROUP, CHUNK), F32),
               pltpu.VMEM((t, LANES), F32), pltpu.VMEM((t, LANES), F32), pltpu.VMEM((t, LANES), F32),
               pltpu.VMEM((n_chunks, N_GATE, CHUNK), F32), pltpu.VMEM((n_chunks, N_GATE, CHUNK), F32),
               pltpu.VMEM((n_chunks, N_GATE, wide), F32),
               pltpu.VMEM((N_GATE, t, wide), F32),
               pltpu.VMEM((N_GATE, n_chunks, 2 * CHUNK, HEAD_DIM), BF16),
               pltpu.VMEM((N_GATE, t, CHUNK), BF16),
               pltpu.VMEM((N_GATE, n_chunks, HEAD_DIM, CHUNK), BF16),
               pltpu.VMEM((n_sub * N_GATE, HEAD_DIM, wide), F32),
               pltpu.VMEM((n_sub * N_GATE, 2 * CHUNK, wide), F32),
               pltpu.VMEM((GDN_PAIRS, 2 * CHUNK, CHUNK), F32)]
    scratch += [pltpu.VMEM((GDN_CHAINS, CHUNK, CHUNK), F32)] * 4
    scratch += [pltpu.VMEM((GDN_CHAINS, CHUNK, CHUNK), BF16)] * 3
    scratch += [pltpu.VMEM((GDN_CHAINS, CHUNK, wide), BF16)] * 2
    return _seq_call(
        functools.partial(_gdn_kernel, t=t, n_sub=n_sub), (a_all, gate_all, conv_w, a_log, dt_bias, norm_w, s0),
        in_specs=[pl.BlockSpec((t, 4 * W_GROUP), lambda i: (i + first_block, 0)),
                  pl.BlockSpec((t, LANES), lambda i: (i + first_block, 0)),
                  pl.BlockSpec((SUBLANES, 3 * W_GROUP), lambda i: (0, 0)),
                  small(), small(),
                  pl.BlockSpec((1, W_GROUP), lambda i: (0, 0)),
                  pl.BlockSpec((n_sub, N_GATE, HEAD_DIM, HEAD_DIM), lambda i: (i, 0, 0, 0))],
        out_specs=[pl.BlockSpec((t, W_GROUP), lambda i: (i + first_block, 0)),
                   pl.BlockSpec((n_sub, N_GATE, HEAD_DIM, HEAD_DIM), lambda i: (i, 0, 0, 0))],
        out_shape=[jax.ShapeDtypeStruct((N_TOK, W_GROUP), F32),
                   jax.ShapeDtypeStruct((n_seq, N_GATE, HEAD_DIM, HEAD_DIM), F32)],
        scratch_shapes=scratch,
        into=into, grid=(n_seq // n_sub,), compiler_params=_cparams("arbitrary"), name="gdn")


def _swap16(x):
    width = x.shape[-1]
    first = (_iota(x.shape, 1) // 16) % 2 == 0
    return jnp.where(first, pltpu.roll(x, width - 16, 1), pltpu.roll(x, 16, 1))


def _block_diag_heads(s0_ref, first):
    zero = jnp.zeros((HEAD_DIM, HEAD_DIM), F32)
    return jnp.concatenate(
        [jnp.concatenate([s0_ref[0, first + h] if j == h else zero for j in range(N_HEADS)], axis=1)
         for h in range(N_HEADS)], axis=0)


def _ret_kernel(r_ref, lg_ref, s0_ref, cos_ref, sin_ref, o_ref, sfin_ref, dmat_s, *, t, latent):
    @pl.when(pl.program_id(0) == 0)
    def _():
        lg0 = -_softplus(-lg_ref[...])
        for qt in range(t // Q_TILE):
            diff = (_iota((Q_TILE, t), 0) + qt * Q_TILE - _iota((Q_TILE, t), 1)).astype(F32)
            for h in range(N_HEADS):
                lgf_h = lg0[0:1, h * HEAD_DIM:h * HEAD_DIM + 1]
                lgb_h = lg0[1:2, h * HEAD_DIM:h * HEAD_DIM + 1]
                dmat_s[h, qt * Q_TILE:(qt + 1) * Q_TILE, :] = (
                    jnp.exp(jnp.where(diff >= 0, diff * lgf_h, NEG_BIG))
                    + jnp.exp(jnp.where(diff <= 0, -diff * lgb_h, NEG_BIG)))

    r = r_ref[...]
    q = r[:, :W_GROUP]
    k = r[:, W_GROUP:2 * W_GROUP]
    v = r[:, 2 * W_GROUP:3 * W_GROUP]
    if latent:
        q = q * cos_ref[...] + _swap16(q) * sin_ref[...]
        k = k * cos_ref[...] + _swap16(k) * sin_ref[...]
    k = k * (HEAD_DIM ** -0.5)
    lg = -_softplus(-lg_ref[...])
    lgf, lgb = lg[0:1, :], lg[1:2, :]
    head = _iota((1, W_GROUP), 1) // HEAD_DIM
    head_sum = _head_sum_matrix()
    pos = _iota((t, 1), 0).astype(F32)
    q_b = q.astype(BF16)
    kt_b = k.T.astype(BF16)
    v_heads = [jnp.where(head == h, v, 0.0).astype(BF16) for h in range(N_HEADS)]
    if latent:
        s0f = _block_diag_heads(s0_ref, 0)
        s0b = _block_diag_heads(s0_ref, N_HEADS)
    for qt in range(t // Q_TILE):
        rows = slice(qt * Q_TILE, (qt + 1) * Q_TILE)
        o = jnp.zeros((Q_TILE, W_GROUP), F32)
        for h in range(N_HEADS):
            s = _mm(jnp.where(head == h, q_b[rows], 0.0), kt_b) * dmat_s[h, rows, :]
            o = o + _mm(s.astype(BF16), v_heads[h])
        if latent:
            p = pos[rows]
            o = o + jnp.exp((p + 1.0) * lgf) * _bmm(q_b[rows], s0f) + jnp.exp((t - p) * lgb) * _bmm(q_b[rows], s0b)
        oc = o - _head_sums(o, head_sum) * (1.0 / HEAD_DIM)
        on = oc * lax.rsqrt(_head_sums(oc * oc, head_sum) * (1.0 / HEAD_DIM) + EPS)
        o_ref[rows, :] = on * _silu(r[rows, 3 * W_GROUP:])
    v_b = v.astype(BF16)
    sf = _mm_tn((k * jnp.exp((t - 1.0 - pos) * lgf)).astype(BF16), v_b)
    sb = _mm_tn((k * jnp.exp(pos * lgb)).astype(BF16), v_b)
    for h in range(N_HEADS):
        hs = slice(h * HEAD_DIM, (h + 1) * HEAD_DIM)
        sf_h, sb_h = sf[hs, hs], sb[hs, hs]
        if latent:
            sf_h = sf_h + jnp.exp(t * lgf[:, h * HEAD_DIM:h * HEAD_DIM + 1]) * s0_ref[0, h]
            sb_h = sb_h + jnp.exp(t * lgb[:, h * HEAD_DIM:h * HEAD_DIM + 1]) * s0_ref[0, N_HEADS + h]
        sfin_ref[0, h] = sf_h
        sfin_ref[0, N_HEADS + h] = sb_h


def _ret(r_all, logit, s0, cos, sin, t, n_seq, first_block, latent, into=None):
    return _seq_call(
        functools.partial(_ret_kernel, t=t, latent=latent), (r_all, logit, s0, cos, sin),
        in_specs=[pl.BlockSpec((t, 4 * W_GROUP), lambda i: (i + first_block, 0)),
                  pl.BlockSpec((2, W_GROUP), lambda i: (0, 0)),
                  pl.BlockSpec((1, N_GATE, HEAD_DIM, HEAD_DIM), lambda i: (i, 0, 0, 0)),
                  pl.BlockSpec((t, W_GROUP), lambda i: (0, 0)),
                  pl.BlockSpec((t, W_GROUP), lambda i: (0, 0))],
        out_specs=[pl.BlockSpec((t, W_GROUP), lambda i: (i + first_block, 0)),
                   pl.BlockSpec((1, N_GATE, HEAD_DIM, HEAD_DIM), lambda i: (i, 0, 0, 0))],
        out_shape=[jax.ShapeDtypeStruct((N_TOK, W_GROUP), F32),
                   jax.ShapeDtypeStruct((n_seq, N_GATE, HEAD_DIM, HEAD_DIM), F32)],
        scratch_shapes=[pltpu.VMEM((N_HEADS, t, t), F32)],
        into=into, grid=(n_seq,), compiler_params=_cparams("arbitrary"), name="ret")


def _rope_tables(t):
    pos = np.arange(t)
    row = (pos // GRID_W).astype(np.float32)
    col = (pos % GRID_W).astype(np.float32)
    nf = HEAD_DIM // 4
    inv_freq = jnp.power(ROPE_BASE, -jnp.arange(nf, dtype=F32) / nf)
    ang_r = jnp.asarray(row)[:, None] * inv_freq[None, :]
    ang_c = jnp.asarray(col)[:, None] * inv_freq[None, :]
    cos = jnp.concatenate([jnp.cos(ang_r)] * 2 + [jnp.cos(ang_c)] * 2, axis=1)
    sin = jnp.concatenate([-jnp.sin(ang_r), jnp.sin(ang_r), -jnp.sin(ang_c), jnp.sin(ang_c)], axis=1)
    return jnp.tile(cos, (1, N_HEADS)), jnp.tile(sin, (1, N_HEADS))


def _head_rms(x, w, head_sum):
    return x * lax.rsqrt(_head_sums(x * x, head_sum) * (1.0 / HEAD_DIM) + EPS) * w


def _ctx_attn_kernel(n_ref, qw_ref, kw_ref, o_ref, k_out, v_out, *, n_fill):
    n = n_ref[...]
    head_sum = _head_sum_matrix()
    head = _iota((1, W_GROUP), 1) // HEAD_DIM
    q = (_head_rms(n[:, :W_GROUP], qw_ref[...], head_sum) * (HEAD_DIM ** -0.5)).astype(BF16)
    k = _head_rms(n[:, W_GROUP:2 * W_GROUP], kw_ref[...], head_sum)
    v = n[:, 2 * W_GROUP:]
    k_t = k.T
    k_out[0, 0] = k_t
    v_out[0, 0] = v.T
    for later in range(1, 1 + n_fill):
        k_out[0, later] = jnp.zeros((W_GROUP, SEQ), F32)
        v_out[0, later] = jnp.zeros((W_GROUP, SEQ), F32)
    kt_b = k_t.astype(BF16)
    o = jnp.zeros((SEQ, W_GROUP), F32)
    for h in range(N_HEADS):
        s = _mm(jnp.where(head == h, q, 0.0), kt_b)
        p = jnp.exp(s - jnp.max(s, axis=-1, keepdims=True))
        o = o + _mm(p.astype(BF16), jnp.where(head == h, v, 0.0).astype(BF16)) / jnp.sum(p, axis=-1, keepdims=True)
    o_ref[...] = o


def _ctx_attn(n_all, qw, kw, into, layer, caches):
    slots = DEPTH if caches is None else 1
    cache_spec = pl.BlockSpec((1, slots, W_GROUP, SEQ), lambda i: (i, layer if caches is not None else 0, 0, 0))
    cache_shape = jax.ShapeDtypeStruct((BATCH, DEPTH, W_GROUP, SEQ), F32)
    return _seq_call(
        functools.partial(_ctx_attn_kernel, n_fill=slots - 1), (n_all, qw, kw),
        in_specs=[pl.BlockSpec((SEQ, 3 * W_GROUP), lambda i: (i, 0)),
                  pl.BlockSpec((1, W_GROUP), lambda i: (0, 0)),
                  pl.BlockSpec((1, W_GROUP), lambda i: (0, 0))],
        out_specs=[pl.BlockSpec((SEQ, W_GROUP), lambda i: (i, 0)), cache_spec, cache_spec],
        out_shape=[jax.ShapeDtypeStruct((N_TOK, W_GROUP), F32), cache_shape, cache_shape],
        into=into, more_into=() if caches is None else ((caches[0], 1), (caches[1], 2)),
        grid=(BATCH,), compiler_params=_cparams("arbitrary"), name="ctx_attn")


def _nat_kernel(n_ref, ck_ref, cv_ref, bias_ref, qw_ref, kw_ref, o_ref, q_s, kt_s, ckt_s):
    h = pl.program_id(1)

    @pl.when(h == 0)
    def _():
        head_sum = _head_sum_matrix()
        n = n_ref[...]
        q_s[...] = (_head_rms(n[:, :W_GROUP], qw_ref[...], head_sum) * (HEAD_DIM ** -0.5)).astype(BF16)
        kt_s[...] = _head_rms(n[:, W_GROUP:2 * W_GROUP], kw_ref[...], head_sum).T.astype(BF16)
        ckt_s[...] = ck_ref[0].T.astype(BF16)
        o_ref[...] = jnp.zeros_like(o_ref)

    mine = _iota((1, W_GROUP), 1) // HEAD_DIM == h
    v = jnp.where(mine, n_ref[:, 2 * W_GROUP:], 0.0).astype(BF16)
    cv = jnp.where(mine, cv_ref[0], 0.0).astype(BF16)
    for qt in range(DEC_SEQ // Q_TILE):
        rows = slice(qt * Q_TILE, (qt + 1) * Q_TILE)
        q = jnp.where(mine, q_s[rows, :], 0.0)
        grid_rows = range(qt * Q_TILE // GRID_W, (qt + 1) * Q_TILE // GRID_W)
        s_loc = _mm(q, kt_s[...]) + jnp.concatenate([_nat_bias_strip(bias_ref, row) for row in grid_rows], axis=0)
        s_ctx = _mm(q, ckt_s[...])
        m = jnp.maximum(jnp.max(s_loc, axis=-1, keepdims=True), jnp.max(s_ctx, axis=-1, keepdims=True))
        p_loc = jnp.exp(s_loc - m)
        p_ctx = jnp.exp(s_ctx - m)
        den = jnp.sum(p_loc, axis=-1, keepdims=True) + jnp.sum(p_ctx, axis=-1, keepdims=True)
        o_ref[rows, :] += (_mm(p_loc.astype(BF16), v) + _mm(p_ctx.astype(BF16), cv)) / den


def _nat(n_all, ck, cv, bias, qw, kw, into):
    first_block = N_CTX // DEC_SEQ
    return _seq_call(
        _nat_kernel, (n_all, ck, cv, bias, qw, kw),
        in_specs=[pl.BlockSpec((DEC_SEQ, 3 * W_GROUP), lambda b, h: (b + first_block, 0)),
                  pl.BlockSpec((1, PAST_LEN, W_GROUP), lambda b, h: (b, 0, 0)),
                  pl.BlockSpec((1, PAST_LEN, W_GROUP), lambda b, h: (b, 0, 0)),
                  pl.BlockSpec((1, 3 * N_ROW_OFF - 1, GRID_W, 2 * GRID_W), lambda b, h: (h, 0, 0, 0)),
                  pl.BlockSpec((1, W_GROUP), lambda b, h: (0, 0)),
                  pl.BlockSpec((1, W_GROUP), lambda b, h: (0, 0))],
        out_specs=pl.BlockSpec((DEC_SEQ, W_GROUP), lambda b, h: (b + first_block, 0)),
        out_shape=jax.ShapeDtypeStruct((N_TOK, W_GROUP), F32),
        scratch_shapes=[pltpu.VMEM((DEC_SEQ, W_GROUP), BF16), pltpu.VMEM((W_GROUP, DEC_SEQ), BF16),
                        pltpu.VMEM((W_GROUP, PAST_LEN), BF16)],
        into=into, grid=(DEC_BATCH, N_HEADS), compiler_params=_cparams("arbitrary", "arbitrary"), name="nat")


N_ROW_OFF = 2 * WIN_ROWS - 1
NAT_ROWS = DEC_SEQ // GRID_W
NAT_KH = min(WIN_ROWS, NAT_ROWS)
NAT_PAIR, NAT_LOW, NAT_HIGH = 0, N_ROW_OFF - 1, 2 * N_ROW_OFF - 1


def _nat_tables(rpb):
    c = np.arange(GRID_W)
    c0 = np.clip(c - WIN_COLS // 2, 0, GRID_W - WIN_COLS)
    col_in = (c[None, :] >= c0[:, None]) & (c[None, :] < c0[:, None] + WIN_COLS)
    col_idx = np.clip(c[None, :] - c[:, None], -(WIN_COLS - 1), WIN_COLS - 1) + WIN_COLS - 1
    col_hot = (col_idx[..., None] == np.arange(2 * WIN_COLS - 1)).astype(np.float32)
    tz = jnp.where(col_in, jnp.einsum('hab,qkb->haqk', rpb.astype(F32), col_hot, precision=HI), NEG_BIG)
    neg = jnp.full_like(tz, NEG_BIG)
    return jnp.concatenate([jnp.concatenate([tz[:, :-1], tz[:, 1:]], axis=-1),
                            jnp.concatenate([tz, neg], axis=-1), jnp.concatenate([neg, tz], axis=-1)], axis=1)


def _nat_bias_strip(tab_ref, row):
    first_key = min(max(row - NAT_KH // 2, 0), NAT_ROWS - NAT_KH)
    off = first_key - row + WIN_ROWS - 1
    tiles = {}
    done, key = 0, first_key
    if key % 2 == 1:
        tiles[key // 2] = tab_ref[0, NAT_HIGH + off]
        done, key = 1, key + 1
    while done + 1 < NAT_KH:
        tiles[key // 2] = tab_ref[0, NAT_PAIR + off + done]
        done, key = done + 2, key + 2
    if done < NAT_KH:
        tiles[key // 2] = tab_ref[0, NAT_LOW + off + done]
    outside = jnp.full((GRID_W, 2 * GRID_W), NEG_BIG, F32)
    return jnp.concatenate([tiles.get(i, outside) for i in range(NAT_ROWS // 2)], axis=1)


GSEL_LANE = N_EXPERTS


def _pack_router(we, be, wg, bg):
    pad = LANES - N_EXPERTS - N_GROUPS
    w = jnp.concatenate([we, wg, jnp.zeros((D_MODEL, pad), F32)], axis=1)
    b = jnp.concatenate([be, bg, jnp.zeros((pad,), F32)]).reshape(1, LANES)
    hi = w.astype(BF16)
    lo = (w - hi.astype(F32)).astype(BF16)
    return jnp.concatenate([hi, lo], axis=1), b


def _lane_min_where(mask, lane):
    return jnp.min(jnp.where(mask, lane, LANES), axis=-1, keepdims=True)


def _sconv_tile(i, s_ref, before_ref, after_ref, w_ref):
    s = s_ref[...]
    p = s[:, W_GROUP:2 * W_GROUP] * s[:, 2 * W_GROUP:]
    p_before = before_ref[SUBLANES - 1:, W_GROUP:2 * W_GROUP] * before_ref[SUBLANES - 1:, 2 * W_GROUP:]
    p_after = after_ref[:1, W_GROUP:2 * W_GROUP] * after_ref[:1, 2 * W_GROUP:]
    row = _iota((TOK_TILE, 1), 0)
    seq_len = jnp.where(i < N_CTX_TILES, SEQ, DEC_SEQ)
    pos = (i * TOK_TILE + row) & (seq_len - 1)
    prev = jnp.where(row == 0, p_before, pltpu.roll(p, 1, 0))
    nxt = jnp.where(row == TOK_TILE - 1, p_after, pltpu.roll(p, TOK_TILE - 1, 0))
    prev = jnp.where(pos == 0, 0.0, prev)
    nxt = jnp.where(pos == seq_len - 1, 0.0, nxt)
    return s[:, :W_GROUP] * (w_ref[0:1, :] * prev + w_ref[1:2, :] * p + w_ref[2:3, :] * nxt)


def _outproj_kernel(xc_ref, xl_ref, m0, m1, m2, s_ref, before_ref, after_ref, cw_ref, mod_ref, nw_ref, w_ref,
                    rw_ref, rb_ref, x_out, hf_out, route_out):
    tile_i = pl.program_id(0)
    mixed = [m0[...], m1[...], m2[...], _sconv_tile(tile_i, s_ref, before_ref, after_ref, cw_ref)]
    acc = None
    for i, m in enumerate(mixed):
        part = _mm(m.astype(BF16), w_ref[i * W_GROUP:(i + 1) * W_GROUP, :])
        acc = part if acc is None else acc + part
    x = _token_tile(tile_i, xc_ref, xl_ref) + mod_ref[0, 2:3, :] * acc
    x_out[...] = x
    y = x * lax.rsqrt(jnp.mean(x * x, axis=-1, keepdims=True) + EPS) * nw_ref[...]
    hf = y * (1.0 + mod_ref[0, 4:5, :]) + mod_ref[0, 3:4, :]
    hf_hi = hf.astype(BF16)
    hf_out[...] = hf_hi

    hf_lo = (hf - hf_hi.astype(F32)).astype(BF16)
    both = _mm(hf_hi, rw_ref[...])
    logits = both[:, :LANES] + both[:, LANES:] + _mm(hf_lo, rw_ref[:, :LANES]) + rb_ref[...]
    lane = _iota(logits.shape, 1)
    is_g = (lane >= N_EXPERTS) & (lane < N_EXPERTS + N_GROUPS)
    gl = jnp.where(is_g, logits, NEG_BIG)
    ge = jnp.exp(gl - jnp.max(gl, axis=-1, keepdims=True))
    gp = jnp.where(is_g, ge / jnp.sum(ge, axis=-1, keepdims=True), -1.0)
    gw = jnp.max(gp, axis=-1, keepdims=True)
    gsel = _lane_min_where(gp == gw, lane) - N_EXPERTS
    in_grp = (lane // EXPERTS_PER_GROUP == gsel) & (lane < N_EXPERTS)
    el = jnp.where(in_grp, logits, NEG_BIG)
    ee = jnp.exp(el - jnp.max(el, axis=-1, keepdims=True))
    ep = jnp.where(in_grp, ee / jnp.sum(ee, axis=-1, keepdims=True), -1.0)
    t1 = jnp.max(ep, axis=-1, keepdims=True)
    i1 = _lane_min_where(ep == t1, lane)
    ep2 = jnp.where(lane == i1, -1.0, ep)
    t2 = jnp.max(ep2, axis=-1, keepdims=True)
    i2 = _lane_min_where(ep2 == t2, lane)
    tsum = t1 + t2
    combine = jnp.where(lane == i1, gw * (t1 / tsum), 0.0) + jnp.where(lane == i2, gw * (t2 / tsum), 0.0)
    route_out[...] = jnp.where(lane == GSEL_LANE, gsel.astype(F32), combine)


def _outproj(x_ctx, x_lat, split, mixed, a_sc, sc_w, mod, norm_w, w_out, rw, rb):
    tile = lambda w: pl.BlockSpec((TOK_TILE, w), lambda i: (i, 0))
    whole = lambda a: pl.BlockSpec(a.shape, lambda i: (0,) * a.ndim)
    per_tile = TOK_TILE // SUBLANES
    halo = lambda index: pl.BlockSpec((SUBLANES, 3 * W_GROUP), lambda i: (index(i), 0))
    return pl.pallas_call(
        _outproj_kernel,
        grid=(N_TOK // TOK_TILE,),
        in_specs=_token_specs(split) + [tile(W_GROUP)] * 3
                 + [tile(3 * W_GROUP), halo(lambda i: jnp.maximum(i * per_tile - 1, 0)),
                    halo(lambda i: jnp.minimum((i + 1) * per_tile, N_TOK // SUBLANES - 1)), whole(sc_w)]
                 + [pl.BlockSpec((1, SUBLANES, D_MODEL), lambda i: (_cond_of_tile(i), 0, 0)),
                    pl.BlockSpec((1, D_MODEL), lambda i: (0, 0)), whole(w_out), whole(rw), whole(rb)],
        out_specs=[tile(D_MODEL), tile(D_MODEL), tile(LANES)],
        out_shape=[jax.ShapeDtypeStruct((N_TOK, D_MODEL), F32), jax.ShapeDtypeStruct((N_TOK, D_MODEL), BF16),
                   jax.ShapeDtypeStruct((N_TOK, LANES), F32)],
        compiler_params=_cparams("arbitrary"),
        name="outproj",
    )(x_ctx, x_lat, *mixed, a_sc, a_sc, a_sc, sc_w, mod, norm_w.reshape(1, D_MODEL), w_out, rw, rb)


SEG_BLK = 16
LOCAL_ROWS = TOK_TILE + N_GROUPS * SEG_BLK
N_TOK_TILES = N_TOK // TOK_TILE
MOE_ROWS = -(-(N_TOK + N_TOK_TILES * N_GROUPS * (SEG_BLK - 1) + N_GROUPS * (MOE_TILE - 1)) // MOE_TILE) * MOE_TILE


def _moe_tables(gsel):
    groups = jnp.arange(N_GROUPS, dtype=jnp.int32)
    onehot = (gsel.reshape(N_TOK_TILES, TOK_TILE, 1) == groups).astype(jnp.int32)
    rank = jnp.cumsum(onehot, axis=1) - onehot
    nblk = (jnp.sum(onehot, axis=1) + SEG_BLK - 1) // SEG_BLK
    loc_blk = jnp.cumsum(nblk, axis=1) - nblk
    blocks_per_tile = MOE_TILE // SEG_BLK
    grp_tiles = (jnp.sum(nblk, axis=0) + blocks_per_tile - 1) // blocks_per_tile
    grp_tile_start = jnp.cumsum(grp_tiles) - grp_tiles
    dst_blk = grp_tile_start[None, :] * blocks_per_tile + jnp.cumsum(nblk, axis=0) - nblk
    local_pos = jnp.sum(onehot * (loc_blk[:, None, :] * SEG_BLK + rank), axis=2)
    tile_idx = jnp.arange(MOE_ROWS // MOE_TILE, dtype=jnp.int32)
    tile_group = jnp.clip(jnp.sum(tile_idx[:, None] >= grp_tile_start[None, :], axis=1) - 1, 0, N_GROUPS - 1)
    tile_valid = tile_idx < jnp.sum(grp_tiles)
    flat = lambda a: a.reshape(-1).astype(jnp.int32)
    return local_pos.astype(jnp.int32), flat(nblk), flat(loc_blk), flat(dst_blk), flat(tile_group), flat(tile_valid)


def _segment_copies(t, nblk, loc_blk, dst_blk, make_copies, action):
    for g in range(N_GROUPS):
        k = t * N_GROUPS + g

        @pl.loop(0, nblk[k])
        def _(b):
            local = pl.multiple_of((loc_blk[k] + b) * SEG_BLK, SEG_BLK)
            sorted_row = pl.multiple_of((dst_blk[k] + b) * SEG_BLK, SEG_BLK)
            for cp in make_copies(local, sorted_row):
                action(cp)


def _dispatch_kernel(nblk, loc_blk, dst_blk, hf_ref, rt_ref, lp_ref, xs_in, rs_in, xs_hbm, rs_hbm, xbuf, rbuf, sem):
    t = pl.program_id(0)
    slot = t % 2
    onehot = _iota((LOCAL_ROWS, TOK_TILE), 0) == lp_ref[0]
    xbuf[slot] = _mm(onehot.astype(BF16), hf_ref[...]).astype(BF16)
    rbuf[slot] = _mm(onehot.astype(F32), rt_ref[...], HI)

    def copies_of(s):
        def copies(local, sorted_row):
            return (pltpu.make_async_copy(xbuf.at[s, pl.ds(local, SEG_BLK)], xs_hbm.at[pl.ds(sorted_row, SEG_BLK)], sem.at[s]),
                    pltpu.make_async_copy(rbuf.at[s, pl.ds(local, SEG_BLK)], rs_hbm.at[pl.ds(sorted_row, SEG_BLK)], sem.at[s]))
        return copies

    @pl.when(t > 0)
    def _():
        _segment_copies(t - 1, nblk, loc_blk, dst_blk, copies_of(1 - slot), lambda cp: cp.wait())

    _segment_copies(t, nblk, loc_blk, dst_blk, copies_of(slot), lambda cp: cp.start())

    @pl.when(t == N_TOK_TILES - 1)
    def _():
        _segment_copies(t, nblk, loc_blk, dst_blk, copies_of(slot), lambda cp: cp.wait())


def _dispatch(hf, route, local_pos, nblk, loc_blk, dst_blk):
    grid_spec = pltpu.PrefetchScalarGridSpec(
        num_scalar_prefetch=3,
        grid=(N_TOK_TILES,),
        in_specs=[pl.BlockSpec((TOK_TILE, D_MODEL), lambda t, *_: (t, 0)),
                  pl.BlockSpec((TOK_TILE, LANES), lambda t, *_: (t, 0)),
                  pl.BlockSpec((1, 1, TOK_TILE), lambda t, *_: (t, 0, 0)),
                  pl.BlockSpec(memory_space=pl.ANY), pl.BlockSpec(memory_space=pl.ANY)],
        out_specs=[pl.BlockSpec(memory_space=pl.ANY), pl.BlockSpec(memory_space=pl.ANY)],
        scratch_shapes=[pltpu.VMEM((2, LOCAL_ROWS, D_MODEL), BF16), pltpu.VMEM((2, LOCAL_ROWS, LANES), F32),
                        pltpu.SemaphoreType.DMA((2,))],
    )
    return pl.pallas_call(
        _dispatch_kernel,
        grid_spec=grid_spec,
        out_shape=[jax.ShapeDtypeStruct((MOE_ROWS, D_MODEL), BF16), jax.ShapeDtypeStruct((MOE_ROWS, LANES), F32)],
        input_output_aliases={6: 0, 7: 1},
        compiler_params=_cparams("arbitrary"),
        name="dispatch",
    )(nblk, loc_blk, dst_blk, hf, route, local_pos.reshape(N_TOK_TILES, 1, TOK_TILE),
      jnp.zeros((MOE_ROWS, D_MODEL), BF16), jnp.zeros((MOE_ROWS, LANES), F32))


def _moe_kernel(tile_group, tile_valid, x_ref, r_ref, wg_hbm, wu_hbm, wd_hbm, y_ref,
                wg_b, wu_b, wd_b, stage_g, stage_u, stage_d, sem, *, layer):
    i = pl.program_id(0)
    g = tile_group[i]
    group_row = layer * N_GROUPS + g
    new_group = (i == 0) | (g != tile_group[jnp.maximum(i - 1, 0)])
    valid = tile_valid[i] > 0

    def weight_copies(e):
        slot = e % 2
        return (pltpu.make_async_copy(wg_hbm.at[group_row, e], stage_g.at[slot], sem.at[slot]),
                pltpu.make_async_copy(wu_hbm.at[group_row, e], stage_u.at[slot], sem.at[slot]),
                pltpu.make_async_copy(wd_hbm.at[group_row, e], stage_d.at[slot], sem.at[slot]))

    def run(load_weights):
        x = x_ref[...]
        route = r_ref[...]
        lane = _iota(route.shape, 1)
        acc = jnp.zeros((MOE_TILE, D_MODEL), F32)
        if load_weights:
            for e in range(2):
                for cp in weight_copies(e):
                    cp.start()
        for e in range(EXPERTS_PER_GROUP):
            if load_weights:
                for cp in weight_copies(e):
                    cp.wait()
                wg_b[e] = stage_g[e % 2].astype(BF16)
                wu_b[e] = stage_u[e % 2].astype(BF16)
                wd_b[e] = stage_d[e % 2].astype(BF16)
                if e + 2 < EXPERTS_PER_GROUP:
                    for cp in weight_copies(e + 2):
                        cp.start()
            cw = jnp.sum(jnp.where(lane == g * EXPERTS_PER_GROUP + e, route, 0.0), axis=-1, keepdims=True)
            act = _silu(_mm(x, wg_b[e])) * _mm(x, wu_b[e]) * cw
            acc = acc + _mm(act.astype(BF16), wd_b[e])
        y_ref[...] = acc

    @pl.when(valid & new_group)
    def _():
        run(True)

    @pl.when(valid & jnp.logical_not(new_group))
    def _():
        run(False)

    @pl.when(jnp.logical_not(valid))
    def _():
        y_ref[...] = jnp.zeros_like(y_ref)


def _moe(xs, rs, tile_group, tile_valid, wg, wu, wd, layer):
    any_spec = pl.BlockSpec(memory_space=pl.ANY)
    grid_spec = pltpu.PrefetchScalarGridSpec(
        num_scalar_prefetch=2,
        grid=(MOE_ROWS // MOE_TILE,),
        in_specs=[pl.BlockSpec((MOE_TILE, D_MODEL), lambda i, tg, tv: (i, 0)),
                  pl.BlockSpec((MOE_TILE, LANES), lambda i, tg, tv: (i, 0)),
                  any_spec, any_spec, any_spec],
        out_specs=pl.BlockSpec((MOE_TILE, D_MODEL), lambda i, tg, tv: (i, 0)),
        scratch_shapes=[pltpu.VMEM((EXPERTS_PER_GROUP, D_MODEL, EXPERT_FF), BF16),
                        pltpu.VMEM((EXPERTS_PER_GROUP, D_MODEL, EXPERT_FF), BF16),
                        pltpu.VMEM((EXPERTS_PER_GROUP, EXPERT_FF, D_MODEL), BF16),
                        pltpu.VMEM((2, D_MODEL, EXPERT_FF), F32), pltpu.VMEM((2, D_MODEL, EXPERT_FF), F32),
                        pltpu.VMEM((2, EXPERT_FF, D_MODEL), F32), pltpu.SemaphoreType.DMA((2,))],
    )
    return pl.pallas_call(
        functools.partial(_moe_kernel, layer=layer),
        grid_spec=grid_spec,
        out_shape=jax.ShapeDtypeStruct((MOE_ROWS, D_MODEL), F32),
        compiler_params=_cparams("arbitrary"),
        name="moe",
    )(tile_group, tile_valid, xs, rs, wg, wu, wd)


def _combine_kernel(nblk, loc_blk, dst_blk, x_ref, lp_ref, mod_ref, ys_hbm, *refs, split):
    o_refs, ybuf, sem = refs[:-2], refs[-2], refs[-1]
    t = pl.program_id(0)
    slot = t % 2

    def fetch(tile, s):
        def copies(local, sorted_row):
            return (pltpu.make_async_copy(ys_hbm.at[pl.ds(sorted_row, SEG_BLK)], ybuf.at[s, pl.ds(local, SEG_BLK)], sem.at[s]),)
        ybuf[s] = jnp.zeros((LOCAL_ROWS, D_MODEL), F32)
        _segment_copies(tile, nblk, loc_blk, dst_blk, copies, lambda cp: cp.start())

    @pl.when(t == 0)
    def _():
        fetch(0, 0)

    @pl.when(t + 1 < N_TOK_TILES)
    def _():
        fetch(t + 1, 1 - slot)

    def copies_now(local, sorted_row):
        return (pltpu.make_async_copy(ys_hbm.at[pl.ds(sorted_row, SEG_BLK)], ybuf.at[slot, pl.ds(local, SEG_BLK)], sem.at[slot]),)

    _segment_copies(t, nblk, loc_blk, dst_blk, copies_now, lambda cp: cp.wait())

    onehot = (_iota((TOK_TILE, LOCAL_ROWS), 1) == lp_ref[...]).astype(BF16)
    ys = ybuf[slot]
    hi = ys.astype(BF16)
    lo = (ys - hi.astype(F32)).astype(BF16)
    y = _mm(onehot, hi) + _mm(onehot, lo)
    out = x_ref[...] + mod_ref[0, 5:6, :] * y
    if split:
        @pl.when(t < N_CTX_TILES)
        def _():
            o_refs[0][...] = out

        @pl.when(t >= N_CTX_TILES)
        def _():
            o_refs[1][...] = out
    else:
        o_refs[0][...] = out


def _combine(x, ys, mod, local_pos, nblk, loc_blk, dst_blk, split):
    tile = pl.BlockSpec((TOK_TILE, D_MODEL), lambda t, *_: (t, 0))
    if split:
        out_specs = _token_specs(True)
        out_shape = [jax.ShapeDtypeStruct((N_CTX, D_MODEL), F32), jax.ShapeDtypeStruct((N_LAT, D_MODEL), F32)]
    else:
        out_specs, out_shape = [tile], [jax.ShapeDtypeStruct((N_TOK, D_MODEL), F32)]
    grid_spec = pltpu.PrefetchScalarGridSpec(
        num_scalar_prefetch=3,
        grid=(N_TOK_TILES,),
        in_specs=[tile, pl.BlockSpec((TOK_TILE, 1), lambda t, *_: (t, 0)),
                  pl.BlockSpec((1, SUBLANES, D_MODEL), lambda t, *_: (_cond_of_tile(t), 0, 0)),
                  pl.BlockSpec(memory_space=pl.ANY)],
        out_specs=out_specs,
        scratch_shapes=[pltpu.VMEM((2, LOCAL_ROWS, D_MODEL), F32), pltpu.SemaphoreType.DMA((2,))],
    )
    return pl.pallas_call(
        functools.partial(_combine_kernel, split=split),
        grid_spec=grid_spec,
        out_shape=out_shape,
        compiler_params=_cparams("arbitrary"),
        name="combine",
    )(nblk, loc_blk, dst_blk, x, local_pos.reshape(N_TOK, 1), mod, ys)


def _lane_row(v):
    v = v.reshape(-1).astype(F32)
    return jnp.concatenate([v, jnp.zeros((LANES - v.shape[0],), F32)]).reshape(1, LANES)


def _pad_rows(w):
    return jnp.concatenate([w, jnp.zeros((SUBLANES - w.shape[0], w.shape[1]), w.dtype)], axis=0)


def kernel(x_prompt, x_sample, state_gdn, state_ret, cache_nat_k, cache_nat_v, c, c_ctx, ada_w, ada_b, norm_mix_w, norm_ffn_w, w_in, gdn_conv_w, gdn_a_log, gdn_dt_bias, gdn_norm_w, ret_gamma_logit, nat_q_norm_w, nat_k_norm_w, nat_rpb, sc_conv_w, w_out, router_group_w, router_group_b, router_expert_w, router_expert_b, moe_w_gate, moe_w_up, moe_w_down):
    x_ctx, x_lat, split = x_prompt.reshape(N_CTX, D_MODEL), x_sample.reshape(N_LAT, D_MODEL), True
    cond = jnp.concatenate([c_ctx[None, :], c], axis=0)
    ada = _ada(cond, ada_w, ada_b).reshape(DEPTH, SUBLANES, 6, D_MODEL)
    cos, sin = _rope_tables(DEC_SEQ)
    zero_state = jnp.zeros((BATCH, N_GATE, HEAD_DIM, HEAD_DIM), F32)
    lat_block = N_CTX // DEC_SEQ
    gdn_list, ret_list, caches = [], [], None
    mixed = [jnp.zeros((N_TOK, W_GROUP), F32) for _ in range(3)]
    for l in range(DEPTH):
        mod = jnp.concatenate([ada[l, :1 + DEC_BATCH], jnp.zeros((1 + DEC_BATCH, SUBLANES - 6, D_MODEL), F32)], axis=1)
        a_gdn, a_ret, a_nat, a_sc, a_gate = _inproj(x_ctx, x_lat, split, mod, norm_mix_w[l], w_in, l)

        conv_w = _pad_rows(gdn_conv_w[l])
        a_log, dt_b = _lane_row(gdn_a_log[l]), _lane_row(gdn_dt_bias[l])
        gnw = jnp.tile(gdn_norm_w[l], N_HEADS).reshape(1, W_GROUP)
        o_gdn, s_gdn = _gdn(a_gdn, a_gate, conv_w, a_log, dt_b, gnw, zero_state, SEQ, BATCH, 0, n_sub=2,
                            into=mixed[0])
        s0 = state_gdn[:, l].reshape(DEC_BATCH, N_GATE, HEAD_DIM, HEAD_DIM)
        o_gdn, _ = _gdn(a_gdn, a_gate, conv_w, a_log, dt_b, gnw, s0, DEC_SEQ, DEC_BATCH, lat_block, into=o_gdn)

        logit = jnp.repeat(ret_gamma_logit[l].astype(F32), HEAD_DIM, axis=1)
        o_ret, s_ret = _ret(a_ret, logit, zero_state, cos[:SEQ], sin[:SEQ], SEQ, BATCH, 0, False, into=mixed[1])
        s0 = state_ret[:, l].reshape(DEC_BATCH, N_GATE, HEAD_DIM, HEAD_DIM)
        o_ret, _ = _ret(a_ret, logit, s0, cos, sin, DEC_SEQ, DEC_BATCH, lat_block, True, into=o_ret)

        qw = jnp.tile(nat_q_norm_w[l], N_HEADS).reshape(1, W_GROUP)
        kw = jnp.tile(nat_k_norm_w[l], N_HEADS).reshape(1, W_GROUP)
        o_nat, *caches = _ctx_attn(a_nat, qw, kw, mixed[2], l, caches)
        o_nat = _nat(a_nat, cache_nat_k[:, l].reshape(DEC_BATCH, PAST_LEN, W_GROUP),
                     cache_nat_v[:, l].reshape(DEC_BATCH, PAST_LEN, W_GROUP), _nat_tables(nat_rpb[l]), qw, kw, o_nat)

        mixed = [o_gdn, o_ret, o_nat]
        rw, rb = _pack_router(router_expert_w[l], router_expert_b[l], router_group_w[l], router_group_b[l])
        x_mid, hf, route = _outproj(x_ctx, x_lat, split, mixed, a_sc, _pad_rows(sc_conv_w[l]), mod, norm_ffn_w[l],
                                    w_out[l].astype(BF16), rw, rb)

        local_pos, nblk, loc_blk, dst_blk, tile_group, tile_valid = _moe_tables(route[:, GSEL_LANE].astype(jnp.int32))
        xs, rs = _dispatch(hf, route, local_pos, nblk, loc_blk, dst_blk)
        to_group = lambda w: w.reshape((DEPTH * N_GROUPS, EXPERTS_PER_GROUP) + w.shape[2:])
        ys = _moe(xs, rs, tile_group, tile_valid, to_group(moe_w_gate), to_group(moe_w_up), to_group(moe_w_down), l)
        last = l == DEPTH - 1
        out = _combine(x_mid, ys, mod, local_pos, nblk, loc_blk, dst_blk, split=last)
        x_ctx, x_lat, split = (out[0], out[1], True) if last else (out[0], out[0], False)

        gdn_list.append(s_gdn.reshape(BATCH, 2, N_HEADS, HEAD_DIM, HEAD_DIM))
        ret_list.append(s_ret.reshape(BATCH, 2, N_HEADS, HEAD_DIM, HEAD_DIM))
    new_k, new_v = [a.reshape(BATCH, DEPTH, N_HEADS, HEAD_DIM, SEQ).transpose(0, 1, 4, 2, 3) for a in caches]
    return (x_ctx.reshape(BATCH, SEQ, D_MODEL), x_lat.reshape(DEC_BATCH, DEC_SEQ, D_MODEL),
            jnp.stack(gdn_list, axis=1), jnp.stack(ret_list, axis=1), new_k, new_v)
```

```python
import functools

import numpy as np
import jax
import jax.numpy as jnp
from jax import lax
from jax.experimental import pallas as pl
from jax.experimental.pallas import tpu as pltpu

D_MODEL = 1024
BATCH = 16
SEQ = 256
DEPTH = 2
DEC_BATCH = 2
DEC_SEQ = 1024
PAST_LEN = 256
GRID_W = 64
HEAD_DIM = 64
W_GROUP = D_MODEL // 4
N_HEADS = W_GROUP // HEAD_DIM
CHUNK = 64
WIN_ROWS = 8
WIN_COLS = 16
ROPE_BASE = 10000.0
N_GROUPS = 4
EXPERTS_PER_GROUP = 8
N_EXPERTS = N_GROUPS * EXPERTS_PER_GROUP
EXPERT_FF = 256
GROUP_FF = EXPERTS_PER_GROUP * EXPERT_FF
EPS = 1e-6

N_CTX = BATCH * SEQ
N_LAT = DEC_BATCH * DEC_SEQ
N_TOK = N_CTX + N_LAT
LANES = 128
SUBLANES = 8
TOK_TILE = 512
MOE_TILE = 512
Q_TILE = 256
VMEM_LIMIT = 48 * 1024 * 1024
NEG_BIG = -1e30
N_GATE = 2 * N_HEADS

F32 = jnp.float32
BF16 = jnp.bfloat16
HI = lax.Precision.HIGHEST


def _mm(a, b, prec=None):
    return lax.dot_general(a, b, (((1,), (0,)), ((), ())), precision=prec, preferred_element_type=F32)


def _mm_nt(a, b, prec=None):
    return lax.dot_general(a, b, (((1,), (1,)), ((), ())), precision=prec, preferred_element_type=F32)


def _mm_tn(a, b, prec=None):
    return lax.dot_general(a, b, (((0,), (0,)), ((), ())), precision=prec, preferred_element_type=F32)


def _bmm(a, b):
    return _mm(a.astype(BF16), b.astype(BF16))


def _sigmoid(x):
    return 1.0 / (1.0 + jnp.exp(-x))


def _silu(x):
    return x * _sigmoid(x)


def _softplus(x):
    return jnp.maximum(x, 0.0) + jnp.log(1.0 + jnp.exp(-jnp.abs(x)))


def _iota(shape, dim):
    return lax.broadcasted_iota(jnp.int32, shape, dim)


def _cparams(*sem):
    return pltpu.CompilerParams(dimension_semantics=sem, vmem_limit_bytes=VMEM_LIMIT)


def _cond_of_tile(i):
    n_ctx_tiles = N_CTX // TOK_TILE
    return jnp.where(i < n_ctx_tiles, 0, 1 + (i - n_ctx_tiles) // (DEC_SEQ // TOK_TILE))


def _head_sum_matrix():
    return (_iota((W_GROUP, W_GROUP), 0) // HEAD_DIM == _iota((W_GROUP, W_GROUP), 1) // HEAD_DIM).astype(BF16)


def _head_sums(x, head_sum):
    hi = x.astype(BF16)
    lo = (x - hi.astype(F32)).astype(BF16)
    return _mm(hi, head_sum) + _mm(lo, head_sum)


N_COND = 1 + DEC_BATCH
ADA_TN = 1536


def _ada_kernel(c_ref, w_ref, b_ref, o_ref):
    def slab(s, accs):
        rows = pl.ds(pl.multiple_of(s * SUBLANES, SUBLANES), SUBLANES)
        w = w_ref[0, rows, :]
        return tuple(acc + w * jnp.tile(_silu(c_ref[r, rows, :]), (1, ADA_TN // LANES))
                     for r, acc in enumerate(accs))

    zero = jnp.zeros((SUBLANES, ADA_TN), F32)
    accs = lax.fori_loop(0, D_MODEL // SUBLANES, slab, (zero,) * N_COND, unroll=4)
    out = jnp.concatenate([jnp.sum(acc, axis=0, keepdims=True) for acc in accs]
                          + [jnp.zeros((SUBLANES - N_COND, ADA_TN), F32)], axis=0)
    o_ref[0] = out + b_ref[0]


def _ada(cond, ada_w, ada_b):
    n_out = 6 * D_MODEL
    cond_lanes = jnp.broadcast_to(cond[:, :, None], (N_COND, D_MODEL, LANES))
    return pl.pallas_call(
        _ada_kernel,
        grid=(DEPTH, n_out // ADA_TN),
        in_specs=[pl.BlockSpec((N_COND, D_MODEL, LANES), lambda l, j: (0, 0, 0)),
                  pl.BlockSpec((1, D_MODEL, ADA_TN), lambda l, j: (l, 0, j)),
                  pl.BlockSpec((1, 1, ADA_TN), lambda l, j: (l, 0, j))],
        out_specs=pl.BlockSpec((1, SUBLANES, ADA_TN), lambda l, j: (l, 0, j)),
        out_shape=jax.ShapeDtypeStruct((DEPTH, SUBLANES, n_out), F32),
        compiler_params=_cparams("arbitrary", "arbitrary"),
        name="ada",
    )(cond_lanes, ada_w, ada_b.reshape(DEPTH, 1, n_out))


IN_WIDTHS = (4 * W_GROUP, 4 * W_GROUP, 3 * W_GROUP, 3 * W_GROUP, LANES)
IN_PACKED = sum(IN_WIDTHS)


IN_TOTAL = 3 * W_GROUP + W_GROUP + 2 * N_GATE + 4 * W_GROUP + 3 * W_GROUP + 3 * W_GROUP
IN_GATE_SRC = 4 * W_GROUP
IN_SRC = (0, IN_GATE_SRC + 2 * N_GATE, IN_GATE_SRC + 2 * N_GATE + 4 * W_GROUP,
          IN_GATE_SRC + 2 * N_GATE + 7 * W_GROUP)
N_CTX_TILES = N_CTX // TOK_TILE


def _token_specs(split):
    lat0 = 0 if split else N_CTX_TILES
    return [pl.BlockSpec((TOK_TILE, D_MODEL), lambda i, *_: (jnp.minimum(i, N_CTX_TILES - 1), 0)),
            pl.BlockSpec((TOK_TILE, D_MODEL), lambda i, *_: (jnp.maximum(i, N_CTX_TILES) - N_CTX_TILES + lat0, 0))]


def _token_tile(i, ctx_ref, lat_ref):
    return jnp.where(i < N_CTX_TILES, ctx_ref[...], lat_ref[...])


def _inproj_kernel(xc_ref, xl_ref, mod_ref, nw_ref, w_ref, *refs):
    o_refs, w_s = refs[:-1], refs[-1]
    i = pl.program_id(0)

    @pl.when(i == 0)
    def _():
        piece = 256
        off = 0
        for src, width in zip(IN_SRC, IN_WIDTHS[:-1]):
            for c in range(0, width, piece):
                w_s[:, off + c:off + c + piece] = w_ref[0, src + c:src + c + piece, :].T.astype(BF16)
            off += width
        gate = w_ref[0, IN_GATE_SRC:IN_GATE_SRC + LANES, :].T
        w_s[:, off:] = jnp.where(_iota((D_MODEL, LANES), 1) < 2 * N_GATE, gate, 0.0).astype(BF16)

    x = _token_tile(i, xc_ref, xl_ref)
    y = x * lax.rsqrt(jnp.mean(x * x, axis=-1, keepdims=True) + EPS) * nw_ref[...]
    h = (y * (1.0 + mod_ref[0, 1:2, :]) + mod_ref[0, 0:1, :]).astype(BF16)
    off = 0
    for o_ref, width in zip(o_refs, IN_WIDTHS):
        o_ref[...] = _mm(h, w_s[:, off:off + width])
        off += width


def _inproj(x_ctx, x_lat, split, mod, norm_w, w_in, layer):
    return pl.pallas_call(
        _inproj_kernel,
        grid=(N_TOK // TOK_TILE,),
        in_specs=_token_specs(split)
                 + [pl.BlockSpec((1, SUBLANES, D_MODEL), lambda i: (_cond_of_tile(i), 0, 0)),
                    pl.BlockSpec((1, D_MODEL), lambda i: (0, 0)),
                    pl.BlockSpec((1, IN_TOTAL, D_MODEL), lambda i: (layer, 0, 0), pipeline_mode=pl.Buffered(1))],
        out_specs=[pl.BlockSpec((TOK_TILE, w), lambda i: (i, 0)) for w in IN_WIDTHS],
        out_shape=[jax.ShapeDtypeStruct((N_TOK, w), F32) for w in IN_WIDTHS],
        scratch_shapes=[pltpu.VMEM((D_MODEL, IN_PACKED), BF16)],
        compiler_params=_cparams("arbitrary"),
        name="inproj",
    )(x_ctx, x_lat, mod, norm_w.reshape(1, D_MODEL), jnp.swapaxes(w_in, 1, 2))


def _seq_call(kernel_fn, args, in_specs, out_specs, out_shape, into, more_into=(), **kwargs):
    donors = ([] if into is None else [(into, 0)]) + list(more_into)
    n_in = len(args)
    inner = kernel_fn
    kernel_fn = lambda *refs: inner(*refs[:n_in], *refs[n_in + len(donors):])
    aliases = {n_in + j: out_idx for j, (_, out_idx) in enumerate(donors)}
    args = list(args) + [a for a, _ in donors]
    in_specs = list(in_specs) + [pl.BlockSpec(memory_space=pl.ANY)] * len(donors)
    return pl.pallas_call(kernel_fn, in_specs=in_specs, out_specs=out_specs, out_shape=out_shape,
                          input_output_aliases=aliases, **kwargs)(*args)


def _shift_rows(p, seq_len):
    rows = p.shape[0]
    pos = _iota(p.shape, 0) & (seq_len - 1)
    prev = jnp.where(pos == 0, 0.0, pltpu.roll(p, 1, 0))
    nxt = jnp.where(pos == seq_len - 1, 0.0, pltpu.roll(p, rows - 1, 0))
    return prev, nxt


def _conv3(x, w_ref, seq_len):
    prev, nxt = _shift_rows(x, seq_len)
    return w_ref[0:1, :] * prev + w_ref[1:2, :] * x + w_ref[2:3, :] * nxt


def _chunk_scan(x, reverse):
    t = x.shape[0]
    pos = _iota(x.shape, 0) % CHUNK
    step = 1
    while step < CHUNK:
        if reverse:
            x = x + jnp.where(pos < CHUNK - step, pltpu.roll(x, t - step, 0), 0.0)
        else:
            x = x + jnp.where(pos >= step, pltpu.roll(x, step, 0), 0.0)
        step *= 2
    return x


GDN_GROUP_CHUNKS = 4
GDN_CHAINS = GDN_GROUP_CHUNKS * N_GATE
GDN_PAIRS = GDN_GROUP_CHUNKS * N_HEADS


def _gdn_kernel(a_ref, gate_ref, convw_ref, alog_ref, dtb_ref, nw_ref, s0_ref, o_ref, sfin_ref,
                q_s, kv_s, kt_s, gc_s, eg_s, beta_s, gcrow_s, ekdrow_s, cdec_s, uo_s, wq_s, attn_s, kdt_s,
                st_s, wsqs_s, kk_s, d_s, m1_s, p_s, n2_s, ob_s, d4_s, m3_s, rhs_s, px_s, *, t, n_sub):
    n_chunks = t // CHUNK
    a = a_ref[...]
    qkv = _silu(_conv3(a[:, :3 * W_GROUP], convw_ref, t // n_sub))
    q = qkv[:, :W_GROUP]
    k = qkv[:, W_GROUP:2 * W_GROUP]
    v = qkv[:, 2 * W_GROUP:]
    head_sum = _head_sum_matrix()
    q = q * lax.rsqrt(_head_sums(q * q, head_sum) + EPS) * (HEAD_DIM ** -0.5)
    k = k * lax.rsqrt(_head_sums(k * k, head_sum) + EPS)
    for h in range(N_HEADS):
        hs = slice(h * HEAD_DIM, (h + 1) * HEAD_DIM)
        q_s[h] = q[:, hs]
        kv_s[h] = jnp.concatenate([k[:, hs], v[:, hs]], axis=1)
    k_t = k.T
    for c in range(n_chunks):
        kt_s[c] = k_t[:, c * CHUNK:(c + 1) * CHUNK]

    gates = gate_ref[...]
    log_a = -jnp.exp(alog_ref[...]) * _softplus(gates + dtb_ref[...])
    beta_s[...] = _sigmoid(gates)

    ci = _iota((CHUNK, CHUNK), 0)
    cj = _iota((CHUNK, CHUNK), 1)
    eye = (ci == cj).astype(F32)
    blk_mask = (ci // 16) == (cj // 16)
    low_half = _iota((CHUNK, 2 * HEAD_DIM), 1) < HEAD_DIM

    prefix = _chunk_scan(log_a, reverse=False)
    suffix = _chunk_scan(log_a, reverse=True)
    gc = jnp.where(_iota((t, LANES), 1) < N_HEADS, prefix, suffix)
    gt = prefix + suffix - log_a
    gc_s[...] = gc
    eg_s[...] = jnp.exp(gc)
    gc_t = gc.T
    ekd_t = jnp.exp(gt - gc).T
    cdec_t = jnp.exp(gt).T
    for c in range(n_chunks):
        lanes = slice(c * CHUNK, (c + 1) * CHUNK)
        gcrow_s[c] = gc_t[:N_GATE, lanes]
        ekdrow_s[c] = ekd_t[:N_GATE, lanes]
        cdec_s[c] = jnp.concatenate([cdec_t[:N_GATE, lanes]] * 2, axis=1)

    def solve_group(grp, carry):
        row0 = grp * (GDN_GROUP_CHUNKS * CHUNK)
        chains = [(cl, a_idx) for cl in range(GDN_GROUP_CHUNKS) for a_idx in range(N_GATE)]

        def rows_of(cl):
            return pl.ds(pl.multiple_of(row0 + cl * CHUNK, CHUNK), CHUNK)

        for cl in range(GDN_GROUP_CHUNKS):
            for h in range(N_HEADS):
                rows = rows_of(cl)
                kq = jnp.concatenate([kv_s[h, rows, :HEAD_DIM], q_s[h, rows, :]], axis=0)
                k_t_h = kt_s[grp * GDN_GROUP_CHUNKS + cl, h * HEAD_DIM:(h + 1) * HEAD_DIM, :]
                kk_s[cl * N_HEADS + h] = _bmm(kq, k_t_h)
        for b, (cl, a_idx) in enumerate(chains):
            backward = a_idx >= N_HEADS
            h = a_idx % N_HEADS
            rows, c = rows_of(cl), grp * GDN_GROUP_CHUNKS + cl
            incl = (cj >= ci) if backward else (cj <= ci)
            strict = (cj > ci) if backward else (cj < ci)
            bt = beta_s[rows, N_GATE + a_idx:N_GATE + a_idx + 1]
            decay = jnp.exp(jnp.where(incl, gc_s[rows, a_idx:a_idx + 1] - gcrow_s[c, a_idx:a_idx + 1, :], NEG_BIG))
            low = jnp.where(strict, kk_s[cl * N_HEADS + h, :CHUNK, :] * bt * decay, 0.0)
            attn_s[a_idx, rows, :] = (kk_s[cl * N_HEADS + h, CHUNK:, :] * decay).astype(BF16)
            d = jnp.where(blk_mask, low, 0.0)
            d_s[b] = d
            ob_s[b] = (low - d).astype(BF16)
            rhs_s[b] = (kv_s[h, rows, :] * bt
                        * jnp.where(low_half, eg_s[rows, a_idx:a_idx + 1], 1.0)).astype(BF16)
        for b in range(GDN_CHAINS):
            m1_s[b] = _bmm(d_s[b], d_s[b])
        for b in range(GDN_CHAINS):
            d, d2 = d_s[b], m1_s[b]
            d4_s[b] = _bmm(d2, d2).astype(BF16)
            p_s[b] = eye - d + d2 - _bmm(d, d2)
        for b in range(GDN_CHAINS):
            d4, p = d4_s[b], p_s[b]
            m3_s[b] = _mm(d4, d4).astype(BF16)
            p_s[b] = p + _mm(p.astype(BF16), d4)
        for b in range(GDN_CHAINS):
            p = p_s[b]
            p_s[b] = p + _mm(p.astype(BF16), m3_s[b])
        for b in range(GDN_CHAINS):
            p = p_s[b].astype(BF16)
            m1_s[b] = _mm(p, ob_s[b])
            px_s[b] = _mm(p, rhs_s[b]).astype(BF16)
        for b in range(GDN_CHAINS):
            n2_s[b] = _bmm(m1_s[b], m1_s[b])
        for b in range(GDN_CHAINS):
            n, n2 = m1_s[b], n2_s[b]
            m3_s[b] = (eye - n + n2 - _bmm(n, n2)).astype(BF16)
        for b, (cl, a_idx) in enumerate(chains):
            h = a_idx % N_HEADS
            rows, c = rows_of(cl), grp * GDN_GROUP_CHUNKS + cl
            wu = _mm(m3_s[b], px_s[b])
            uo_s[a_idx, rows, :] = wu
            wq_s[a_idx, c, :CHUNK, :] = wu[:, :HEAD_DIM].astype(BF16)
            wq_s[a_idx, c, CHUNK:, :] = (q_s[h, rows, :] * eg_s[rows, a_idx:a_idx + 1]).astype(BF16)
            k_t_h = kt_s[c, h * HEAD_DIM:(h + 1) * HEAD_DIM, :]
            kdt_s[a_idx, c] = (k_t_h * ekdrow_s[c, a_idx:a_idx + 1, :]).astype(BF16)
        return carry

    lax.fori_loop(0, n_chunks // GDN_GROUP_CHUNKS, solve_group, 0)

    seq_chunks = n_chunks // n_sub
    chains = [(sub, a_idx) for sub in range(n_sub) for a_idx in range(N_GATE)]
    for j, (sub, a_idx) in enumerate(chains):
        st_s[j] = jnp.concatenate([jnp.zeros((HEAD_DIM, HEAD_DIM), F32), s0_ref[sub, a_idx]], axis=1)

    def scan_chunk(c, carry):
        def chunk_of(sub, a_idx):
            return sub * seq_chunks + ((seq_chunks - 1 - c) if a_idx >= N_HEADS else c)

        for j, (sub, a_idx) in enumerate(chains):
            wsqs_s[j] = _mm(wq_s[a_idx, chunk_of(sub, a_idx)], st_s[j].astype(BF16))
        for j, (sub, a_idx) in enumerate(chains):
            cc = chunk_of(sub, a_idx)
            rows = pl.ds(pl.multiple_of(cc * CHUNK, CHUNK), CHUNK)
            v_new = (uo_s[a_idx, rows, :] - wsqs_s[j, :CHUNK, :]).astype(BF16)
            uo_s[a_idx, rows, :] = wsqs_s[j, CHUNK:, :] + _mm(attn_s[a_idx, rows, :], v_new)
            st_s[j] = st_s[j] * cdec_s[cc, a_idx:a_idx + 1, :] + _mm(kdt_s[a_idx, cc], v_new)
        return carry

    lax.fori_loop(0, seq_chunks, scan_chunk, 0)
    for j, (sub, a_idx) in enumerate(chains):
        sfin_ref[sub, a_idx] = st_s[j, :, HEAD_DIM:]

    o = jnp.concatenate([(uo_s[h] + uo_s[N_HEADS + h])[:, HEAD_DIM:] for h in range(N_HEADS)], axis=1)
    ms = _head_sums(o * o, head_sum) * (1.0 / HEAD_DIM)
    o_ref[...] = o * lax.rsqrt(ms + EPS) * nw_ref[...] * _silu(a[:, 3 * W_GROUP:])


def _gdn(a_all, gate_all, conv_w, a_log, dt_bias, norm_w, s0, seq_len, n_seq, first_block, n_sub=1, into=None):
    small = lambda: pl.BlockSpec((1, LANES), lambda i: (0, 0))
    t = n_sub * seq_len
    n_chunks = t // CHUNK
    wide = 2 * HEAD_DIM
    scratch = [pltpu.VMEM((N_HEADS, t, HEAD_DIM), F32),
               pltpu.VMEM((N_HEADS, t, wide), F32),
               pltpu.VMEM((n_chunks, W_GROUP, CHUNK), F32),
               pltpu.VMEM((t, LANES), F32), pltpu.VMEM((t, LANES), F32), pltpu.VMEM((t, LANES), F32),
               pltpu.VMEM((n_chunks, N_GATE, CHUNK), F32), pltpu.VMEM((n_chunks, N_GATE, CHUNK), F32),
               pltpu.VMEM((n_chunks, N_GATE, wide), F32),
               pltpu.VMEM((N_GATE, t, wide), F32),
               pltpu.VMEM((N_GATE, n_chunks, 2 * CHUNK, HEAD_DIM), BF16),
               pltpu.VMEM((N_GATE, t, CHUNK), BF16),
               pltpu.VMEM((N_GATE, n_chunks, HEAD_DIM, CHUNK), BF16),
               pltpu.VMEM((n_sub * N_GATE, HEAD_DIM, wide), F32),
               pltpu.VMEM((n_sub * N_GATE, 2 * CHUNK, wide), F32),
               pltpu.VMEM((GDN_PAIRS, 2 * CHUNK, CHUNK), F32)]
    scratch += [pltpu.VMEM((GDN_CHAINS, CHUNK, CHUNK), F32)] * 4
    scratch += [pltpu.VMEM((GDN_CHAINS, CHUNK, CHUNK), BF16)] * 3
    scratch += [pltpu.VMEM((GDN_CHAINS, CHUNK, wide), BF16)] * 2
    return _seq_call(
        functools.partial(_gdn_kernel, t=t, n_sub=n_sub), (a_all, gate_all, conv_w, a_log, dt_bias, norm_w, s0),
        in_specs=[pl.BlockSpec((t, 4 * W_GROUP), lambda i: (i + first_block, 0)),
                  pl.BlockSpec((t, LANES), lambda i: (i + first_block, 0)),
                  pl.BlockSpec((SUBLANES, 3 * W_GROUP), lambda i: (0, 0)),
                  small(), small(),
                  pl.BlockSpec((1, W_GROUP), lambda i: (0, 0)),
                  pl.BlockSpec((n_sub, N_GATE, HEAD_DIM, HEAD_DIM), lambda i: (i, 0, 0, 0))],
        out_specs=[pl.BlockSpec((t, W_GROUP), lambda i: (i + first_block, 0)),
                   pl.BlockSpec((n_sub, N_GATE, HEAD_DIM, HEAD_DIM), lambda i: (i, 0, 0, 0))],
        out_shape=[jax.ShapeDtypeStruct((N_TOK, W_GROUP), F32),
                   jax.ShapeDtypeStruct((n_seq, N_GATE, HEAD_DIM, HEAD_DIM), F32)],
        scratch_shapes=scratch,
        into=into, grid=(n_seq // n_sub,), compiler_params=_cparams("arbitrary"), name="gdn")


def _swap16(x):
    width = x.shape[-1]
    first = (_iota(x.shape, 1) // 16) % 2 == 0
    return jnp.where(first, pltpu.roll(x, width - 16, 1), pltpu.roll(x, 16, 1))


def _block_diag_heads(s0_ref, first):
    zero = jnp.zeros((HEAD_DIM, HEAD_DIM), F32)
    return jnp.concatenate(
        [jnp.concatenate([s0_ref[0, first + h] if j == h else zero for j in range(N_HEADS)], axis=1)
         for h in range(N_HEADS)], axis=0)


def _ret_kernel(r_ref, lg_ref, s0_ref, cos_ref, sin_ref, o_ref, sfin_ref, *, t, latent):
    r = r_ref[...]
    q = r[:, :W_GROUP]
    k = r[:, W_GROUP:2 * W_GROUP]
    v = r[:, 2 * W_GROUP:3 * W_GROUP]
    if latent:
        q = q * cos_ref[...] + _swap16(q) * sin_ref[...]
        k = k * cos_ref[...] + _swap16(k) * sin_ref[...]
    k = k * (HEAD_DIM ** -0.5)
    lg = -_softplus(-lg_ref[...])
    lgf, lgb = lg[0:1, :], lg[1:2, :]
    head = _iota((1, W_GROUP), 1) // HEAD_DIM
    head_sum = _head_sum_matrix()
    pos = _iota((t, 1), 0).astype(F32)
    q_b = q.astype(BF16)
    kt_b = k.T.astype(BF16)
    v_heads = [jnp.where(head == h, v, 0.0).astype(BF16) for h in range(N_HEADS)]
    if latent:
        s0f = _block_diag_heads(s0_ref, 0)
        s0b = _block_diag_heads(s0_ref, N_HEADS)
    for qt in range(t // Q_TILE):
        rows = slice(qt * Q_TILE, (qt + 1) * Q_TILE)
        diff = (_iota((Q_TILE, t), 0) + qt * Q_TILE - _iota((Q_TILE, t), 1)).astype(F32)
        both = jnp.where(diff == 0, 2.0, 1.0)
        o = jnp.zeros((Q_TILE, W_GROUP), F32)
        for h in range(N_HEADS):
            lgf_h = lgf[:, h * HEAD_DIM:h * HEAD_DIM + 1]
            lgb_h = lgb[:, h * HEAD_DIM:h * HEAD_DIM + 1]
            dmat = jnp.exp(diff * jnp.where(diff >= 0, lgf_h, -lgb_h)) * both
            s = _mm(jnp.where(head == h, q_b[rows], 0.0), kt_b) * dmat
            o = o + _mm(s.astype(BF16), v_heads[h])
        if latent:
            p = pos[rows]
            o = o + jnp.exp((p + 1.0) * lgf) * _bmm(q_b[rows], s0f) + jnp.exp((t - p) * lgb) * _bmm(q_b[rows], s0b)
        oc = o - _head_sums(o, head_sum) * (1.0 / HEAD_DIM)
        on = oc * lax.rsqrt(_head_sums(oc * oc, head_sum) * (1.0 / HEAD_DIM) + EPS)
        o_ref[rows, :] = on * _silu(r[rows, 3 * W_GROUP:])
    v_b = v.astype(BF16)
    sf = _mm_tn((k * jnp.exp((t - 1.0 - pos) * lgf)).astype(BF16), v_b)
    sb = _mm_tn((k * jnp.exp(pos * lgb)).astype(BF16), v_b)
    for h in range(N_HEADS):
        hs = slice(h * HEAD_DIM, (h + 1) * HEAD_DIM)
        sf_h, sb_h = sf[hs, hs], sb[hs, hs]
        if latent:
            sf_h = sf_h + jnp.exp(t * lgf[:, h * HEAD_DIM:h * HEAD_DIM + 1]) * s0_ref[0, h]
            sb_h = sb_h + jnp.exp(t * lgb[:, h * HEAD_DIM:h * HEAD_DIM + 1]) * s0_ref[0, N_HEADS + h]
        sfin_ref[0, h] = sf_h
        sfin_ref[0, N_HEADS + h] = sb_h


def _ret(r_all, logit, s0, cos, sin, t, n_seq, first_block, latent, into=None):
    return _seq_call(
        functools.partial(_ret_kernel, t=t, latent=latent), (r_all, logit, s0, cos, sin),
        in_specs=[pl.BlockSpec((t, 4 * W_GROUP), lambda i: (i + first_block, 0)),
                  pl.BlockSpec((2, W_GROUP), lambda i: (0, 0)),
                  pl.BlockSpec((1, N_GATE, HEAD_DIM, HEAD_DIM), lambda i: (i, 0, 0, 0)),
                  pl.BlockSpec((t, W_GROUP), lambda i: (0, 0)),
                  pl.BlockSpec((t, W_GROUP), lambda i: (0, 0))],
        out_specs=[pl.BlockSpec((t, W_GROUP), lambda i: (i + first_block, 0)),
                   pl.BlockSpec((1, N_GATE, HEAD_DIM, HEAD_DIM), lambda i: (i, 0, 0, 0))],
        out_shape=[jax.ShapeDtypeStruct((N_TOK, W_GROUP), F32),
                   jax.ShapeDtypeStruct((n_seq, N_GATE, HEAD_DIM, HEAD_DIM), F32)],
        into=into, grid=(n_seq,), compiler_params=_cparams("arbitrary"), name="ret")


def _rope_tables(t):
    pos = np.arange(t)
    row = (pos // GRID_W).astype(np.float32)
    col = (pos % GRID_W).astype(np.float32)
    nf = HEAD_DIM // 4
    inv_freq = jnp.power(ROPE_BASE, -jnp.arange(nf, dtype=F32) / nf)
    ang_r = jnp.asarray(row)[:, None] * inv_freq[None, :]
    ang_c = jnp.asarray(col)[:, None] * inv_freq[None, :]
    cos = jnp.concatenate([jnp.cos(ang_r)] * 2 + [jnp.cos(ang_c)] * 2, axis=1)
    sin = jnp.concatenate([-jnp.sin(ang_r), jnp.sin(ang_r), -jnp.sin(ang_c), jnp.sin(ang_c)], axis=1)
    return jnp.tile(cos, (1, N_HEADS)), jnp.tile(sin, (1, N_HEADS))


def _head_rms(x, w, head_sum):
    return x * lax.rsqrt(_head_sums(x * x, head_sum) * (1.0 / HEAD_DIM) + EPS) * w


def _ctx_attn_kernel(n_ref, qw_ref, kw_ref, o_ref, k_out, v_out, *, n_fill):
    n = n_ref[...]
    head_sum = _head_sum_matrix()
    head = _iota((1, W_GROUP), 1) // HEAD_DIM
    q = (_head_rms(n[:, :W_GROUP], qw_ref[...], head_sum) * (HEAD_DIM ** -0.5)).astype(BF16)
    k = _head_rms(n[:, W_GROUP:2 * W_GROUP], kw_ref[...], head_sum)
    v = n[:, 2 * W_GROUP:]
    k_t = k.T
    k_out[0, 0] = k_t
    v_out[0, 0] = v.T
    for later in range(1, 1 + n_fill):
        k_out[0, later] = jnp.zeros((W_GROUP, SEQ), F32)
        v_out[0, later] = jnp.zeros((W_GROUP, SEQ), F32)
    kt_b = k_t.astype(BF16)
    o = jnp.zeros((SEQ, W_GROUP), F32)
    for h in range(N_HEADS):
        s = _mm(jnp.where(head == h, q, 0.0), kt_b)
        p = jnp.exp(s - jnp.max(s, axis=-1, keepdims=True))
        o = o + _mm(p.astype(BF16), jnp.where(head == h, v, 0.0).astype(BF16)) / jnp.sum(p, axis=-1, keepdims=True)
    o_ref[...] = o


def _ctx_attn(n_all, qw, kw, into, layer, caches):
    slots = DEPTH if caches is None else 1
    cache_spec = pl.BlockSpec((1, slots, W_GROUP, SEQ), lambda i: (i, layer if caches is not None else 0, 0, 0))
    cache_shape = jax.ShapeDtypeStruct((BATCH, DEPTH, W_GROUP, SEQ), F32)
    return _seq_call(
        functools.partial(_ctx_attn_kernel, n_fill=slots - 1), (n_all, qw, kw),
        in_specs=[pl.BlockSpec((SEQ, 3 * W_GROUP), lambda i: (i, 0)),
                  pl.BlockSpec((1, W_GROUP), lambda i: (0, 0)),
                  pl.BlockSpec((1, W_GROUP), lambda i: (0, 0))],
        out_specs=[pl.BlockSpec((SEQ, W_GROUP), lambda i: (i, 0)), cache_spec, cache_spec],
        out_shape=[jax.ShapeDtypeStruct((N_TOK, W_GROUP), F32), cache_shape, cache_shape],
        into=into, more_into=() if caches is None else ((caches[0], 1), (caches[1], 2)),
        grid=(BATCH,), compiler_params=_cparams("arbitrary"), name="ctx_attn")


def _nat_kernel(n_ref, ck_ref, cv_ref, bias_ref, qw_ref, kw_ref, o_ref, q_s, kt_s, ckt_s):
    h = pl.program_id(1)

    @pl.when(h == 0)
    def _():
        head_sum = _head_sum_matrix()
        n = n_ref[...]
        q_s[...] = (_head_rms(n[:, :W_GROUP], qw_ref[...], head_sum) * (HEAD_DIM ** -0.5)).astype(BF16)
        kt_s[...] = _head_rms(n[:, W_GROUP:2 * W_GROUP], kw_ref[...], head_sum).T.astype(BF16)
        ckt_s[...] = ck_ref[0].T.astype(BF16)
        o_ref[...] = jnp.zeros_like(o_ref)

    mine = _iota((1, W_GROUP), 1) // HEAD_DIM == h
    v = jnp.where(mine, n_ref[:, 2 * W_GROUP:], 0.0).astype(BF16)
    cv = jnp.where(mine, cv_ref[0], 0.0).astype(BF16)
    for qt in range(DEC_SEQ // Q_TILE):
        rows = slice(qt * Q_TILE, (qt + 1) * Q_TILE)
        q = jnp.where(mine, q_s[rows, :], 0.0)
        grid_rows = range(qt * Q_TILE // GRID_W, (qt + 1) * Q_TILE // GRID_W)
        s_loc = _mm(q, kt_s[...]) + jnp.concatenate([_nat_bias_strip(bias_ref, row) for row in grid_rows], axis=0)
        s_ctx = _mm(q, ckt_s[...])
        m = jnp.maximum(jnp.max(s_loc, axis=-1, keepdims=True), jnp.max(s_ctx, axis=-1, keepdims=True))
        p_loc = jnp.exp(s_loc - m)
        p_ctx = jnp.exp(s_ctx - m)
        den = jnp.sum(p_loc, axis=-1, keepdims=True) + jnp.sum(p_ctx, axis=-1, keepdims=True)
        o_ref[rows, :] += (_mm(p_loc.astype(BF16), v) + _mm(p_ctx.astype(BF16), cv)) / den


def _nat(n_all, ck, cv, bias, qw, kw, into):
    first_block = N_CTX // DEC_SEQ
    return _seq_call(
        _nat_kernel, (n_all, ck, cv, bias, qw, kw),
        in_specs=[pl.BlockSpec((DEC_SEQ, 3 * W_GROUP), lambda b, h: (b + first_block, 0)),
                  pl.BlockSpec((1, PAST_LEN, W_GROUP), lambda b, h: (b, 0, 0)),
                  pl.BlockSpec((1, PAST_LEN, W_GROUP), lambda b, h: (b, 0, 0)),
                  pl.BlockSpec((1, 3 * N_ROW_OFF - 1, GRID_W, 2 * GRID_W), lambda b, h: (h, 0, 0, 0)),
                  pl.BlockSpec((1, W_GROUP), lambda b, h: (0, 0)),
                  pl.BlockSpec((1, W_GROUP), lambda b, h: (0, 0))],
        out_specs=pl.BlockSpec((DEC_SEQ, W_GROUP), lambda b, h: (b + first_block, 0)),
        out_shape=jax.ShapeDtypeStruct((N_TOK, W_GROUP), F32),
        scratch_shapes=[pltpu.VMEM((DEC_SEQ, W_GROUP), BF16), pltpu.VMEM((W_GROUP, DEC_SEQ), BF16),
                        pltpu.VMEM((W_GROUP, PAST_LEN), BF16)],
        into=into, grid=(DEC_BATCH, N_HEADS), compiler_params=_cparams("arbitrary", "arbitrary"), name="nat")


N_ROW_OFF = 2 * WIN_ROWS - 1
NAT_ROWS = DEC_SEQ // GRID_W
NAT_KH = min(WIN_ROWS, NAT_ROWS)
NAT_PAIR, NAT_LOW, NAT_HIGH = 0, N_ROW_OFF - 1, 2 * N_ROW_OFF - 1


def _nat_tables(rpb):
    c = np.arange(GRID_W)
    c0 = np.clip(c - WIN_COLS // 2, 0, GRID_W - WIN_COLS)
    col_in = (c[None, :] >= c0[:, None]) & (c[None, :] < c0[:, None] + WIN_COLS)
    col_idx = np.clip(c[None, :] - c[:, None], -(WIN_COLS - 1), WIN_COLS - 1) + WIN_COLS - 1
    col_hot = (col_idx[..., None] == np.arange(2 * WIN_COLS - 1)).astype(np.float32)
    tz = jnp.where(col_in, jnp.einsum('hab,qkb->haqk', rpb.astype(F32), col_hot, precision=HI), NEG_BIG)
    neg = jnp.full_like(tz, NEG_BIG)
    return jnp.concatenate([jnp.concatenate([tz[:, :-1], tz[:, 1:]], axis=-1),
                            jnp.concatenate([tz, neg], axis=-1), jnp.concatenate([neg, tz], axis=-1)], axis=1)


def _nat_bias_strip(tab_ref, row):
    first_key = min(max(row - NAT_KH // 2, 0), NAT_ROWS - NAT_KH)
    off = first_key - row + WIN_ROWS - 1
    tiles = {}
    done, key = 0, first_key
    if key % 2 == 1:
        tiles[key // 2] = tab_ref[0, NAT_HIGH + off]
        done, key = 1, key + 1
    while done + 1 < NAT_KH:
        tiles[key // 2] = tab_ref[0, NAT_PAIR + off + done]
        done, key = done + 2, key + 2
    if done < NAT_KH:
        tiles[key // 2] = tab_ref[0, NAT_LOW + off + done]
    outside = jnp.full((GRID_W, 2 * GRID_W), NEG_BIG, F32)
    return jnp.concatenate([tiles.get(i, outside) for i in range(NAT_ROWS // 2)], axis=1)


GSEL_LANE = N_EXPERTS


def _pack_router(we, be, wg, bg):
    pad = LANES - N_EXPERTS - N_GROUPS
    w = jnp.concatenate([we, wg, jnp.zeros((D_MODEL, pad), F32)], axis=1)
    b = jnp.concatenate([be, bg, jnp.zeros((pad,), F32)]).reshape(1, LANES)
    hi = w.astype(BF16)
    lo = (w - hi.astype(F32)).astype(BF16)
    return jnp.concatenate([hi, lo], axis=1), b


def _lane_min_where(mask, lane):
    return jnp.min(jnp.where(mask, lane, LANES), axis=-1, keepdims=True)


def _sconv_tile(i, s_ref, before_ref, after_ref, w_ref):
    s = s_ref[...]
    p = s[:, W_GROUP:2 * W_GROUP] * s[:, 2 * W_GROUP:]
    p_before = before_ref[SUBLANES - 1:, W_GROUP:2 * W_GROUP] * before_ref[SUBLANES - 1:, 2 * W_GROUP:]
    p_after = after_ref[:1, W_GROUP:2 * W_GROUP] * after_ref[:1, 2 * W_GROUP:]
    row = _iota((TOK_TILE, 1), 0)
    seq_len = jnp.where(i < N_CTX_TILES, SEQ, DEC_SEQ)
    pos = (i * TOK_TILE + row) & (seq_len - 1)
    prev = jnp.where(row == 0, p_before, pltpu.roll(p, 1, 0))
    nxt = jnp.where(row == TOK_TILE - 1, p_after, pltpu.roll(p, TOK_TILE - 1, 0))
    prev = jnp.where(pos == 0, 0.0, prev)
    nxt = jnp.where(pos == seq_len - 1, 0.0, nxt)
    return s[:, :W_GROUP] * (w_ref[0:1, :] * prev + w_ref[1:2, :] * p + w_ref[2:3, :] * nxt)


def _outproj_kernel(xc_ref, xl_ref, m0, m1, m2, s_ref, before_ref, after_ref, cw_ref, mod_ref, nw_ref, w_ref,
                    rw_ref, rb_ref, x_out, hf_out, route_out):
    tile_i = pl.program_id(0)
    mixed = [m0[...], m1[...], m2[...], _sconv_tile(tile_i, s_ref, before_ref, after_ref, cw_ref)]
    acc = None
    for i, m in enumerate(mixed):
        part = _mm(m.astype(BF16), w_ref[i * W_GROUP:(i + 1) * W_GROUP, :])
        acc = part if acc is None else acc + part
    x = _token_tile(tile_i, xc_ref, xl_ref) + mod_ref[0, 2:3, :] * acc
    x_out[...] = x
    y = x * lax.rsqrt(jnp.mean(x * x, axis=-1, keepdims=True) + EPS) * nw_ref[...]
    hf = y * (1.0 + mod_ref[0, 4:5, :]) + mod_ref[0, 3:4, :]
    hf_hi = hf.astype(BF16)
    hf_out[...] = hf_hi

    hf_lo = (hf - hf_hi.astype(F32)).astype(BF16)
    both = _mm(hf_hi, rw_ref[...])
    logits = both[:, :LANES] + both[:, LANES:] + _mm(hf_lo, rw_ref[:, :LANES]) + rb_ref[...]
    lane = _iota(logits.shape, 1)
    is_g = (lane >= N_EXPERTS) & (lane < N_EXPERTS + N_GROUPS)
    gl = jnp.where(is_g, logits, NEG_BIG)
    ge = jnp.exp(gl - jnp.max(gl, axis=-1, keepdims=True))
    gp = jnp.where(is_g, ge / jnp.sum(ge, axis=-1, keepdims=True), -1.0)
    gw = jnp.max(gp, axis=-1, keepdims=True)
    gsel = _lane_min_where(gp == gw, lane) - N_EXPERTS
    in_grp = (lane // EXPERTS_PER_GROUP == gsel) & (lane < N_EXPERTS)
    el = jnp.where(in_grp, logits, NEG_BIG)
    ee = jnp.exp(el - jnp.max(el, axis=-1, keepdims=True))
    ep = jnp.where(in_grp, ee / jnp.sum(ee, axis=-1, keepdims=True), -1.0)
    t1 = jnp.max(ep, axis=-1, keepdims=True)
    i1 = _lane_min_where(ep == t1, lane)
    ep2 = jnp.where(lane == i1, -1.0, ep)
    t2 = jnp.max(ep2, axis=-1, keepdims=True)
    i2 = _lane_min_where(ep2 == t2, lane)
    tsum = t1 + t2
    combine = jnp.where(lane == i1, gw * (t1 / tsum), 0.0) + jnp.where(lane == i2, gw * (t2 / tsum), 0.0)
    route_out[...] = jnp.where(lane == GSEL_LANE, gsel.astype(F32), combine)


def _outproj(x_ctx, x_lat, split, mixed, a_sc, sc_w, mod, norm_w, w_out, rw, rb):
    tile = lambda w: pl.BlockSpec((TOK_TILE, w), lambda i: (i, 0))
    whole = lambda a: pl.BlockSpec(a.shape, lambda i: (0,) * a.ndim)
    per_tile = TOK_TILE // SUBLANES
    halo = lambda index: pl.BlockSpec((SUBLANES, 3 * W_GROUP), lambda i: (index(i), 0))
    return pl.pallas_call(
        _outproj_kernel,
        grid=(N_TOK // TOK_TILE,),
        in_specs=_token_specs(split) + [tile(W_GROUP)] * 3
                 + [tile(3 * W_GROUP), halo(lambda i: jnp.maximum(i * per_tile - 1, 0)),
                    halo(lambda i: jnp.minimum((i + 1) * per_tile, N_TOK // SUBLANES - 1)), whole(sc_w)]
                 + [pl.BlockSpec((1, SUBLANES, D_MODEL), lambda i: (_cond_of_tile(i), 0, 0)),
                    pl.BlockSpec((1, D_MODEL), lambda i: (0, 0)), whole(w_out), whole(rw), whole(rb)],
        out_specs=[tile(D_MODEL), tile(D_MODEL), tile(LANES)],
        out_shape=[jax.ShapeDtypeStruct((N_TOK, D_MODEL), F32), jax.ShapeDtypeStruct((N_TOK, D_MODEL), BF16),
                   jax.ShapeDtypeStruct((N_TOK, LANES), F32)],
        compiler_params=_cparams("arbitrary"),
        name="outproj",
    )(x_ctx, x_lat, *mixed, a_sc, a_sc, a_sc, sc_w, mod, norm_w.reshape(1, D_MODEL), w_out, rw, rb)


SEG_BLK = 16
LOCAL_ROWS = TOK_TILE + N_GROUPS * SEG_BLK
N_TOK_TILES = N_TOK // TOK_TILE
MOE_ROWS = -(-(N_TOK + N_TOK_TILES * N_GROUPS * (SEG_BLK - 1) + N_GROUPS * (MOE_TILE - 1)) // MOE_TILE) * MOE_TILE


def _moe_tables(gsel):
    groups = jnp.arange(N_GROUPS, dtype=jnp.int32)
    onehot = (gsel.reshape(N_TOK_TILES, TOK_TILE, 1) == groups).astype(jnp.int32)
    rank = jnp.cumsum(onehot, axis=1) - onehot
    nblk = (jnp.sum(onehot, axis=1) + SEG_BLK - 1) // SEG_BLK
    loc_blk = jnp.cumsum(nblk, axis=1) - nblk
    blocks_per_tile = MOE_TILE // SEG_BLK
    grp_tiles = (jnp.sum(nblk, axis=0) + blocks_per_tile - 1) // blocks_per_tile
    grp_tile_start = jnp.cumsum(grp_tiles) - grp_tiles
    dst_blk = grp_tile_start[None, :] * blocks_per_tile + jnp.cumsum(nblk, axis=0) - nblk
    local_pos = jnp.sum(onehot * (loc_blk[:, None, :] * SEG_BLK + rank), axis=2)
    tile_idx = jnp.arange(MOE_ROWS // MOE_TILE, dtype=jnp.int32)
    tile_group = jnp.clip(jnp.sum(tile_idx[:, None] >= grp_tile_start[None, :], axis=1) - 1, 0, N_GROUPS - 1)
    in_group = tile_idx - grp_tile_start[tile_group]
    tile_rows = jnp.clip(jnp.sum(nblk, axis=0)[tile_group] * SEG_BLK - in_group * MOE_TILE, 0, MOE_TILE)
    flat = lambda a: a.reshape(-1).astype(jnp.int32)
    return local_pos.astype(jnp.int32), flat(nblk), flat(loc_blk), flat(dst_blk), flat(tile_group), flat(tile_rows)


def _segment_copies(t, nblk, loc_blk, dst_blk, make_copies, action):
    for g in range(N_GROUPS):
        k = t * N_GROUPS + g

        @pl.loop(0, nblk[k])
        def _(b):
            local = pl.multiple_of((loc_blk[k] + b) * SEG_BLK, SEG_BLK)
            sorted_row = pl.multiple_of((dst_blk[k] + b) * SEG_BLK, SEG_BLK)
            for cp in make_copies(local, sorted_row):
                action(cp)


def _dispatch_kernel(nblk, loc_blk, dst_blk, hf_ref, rt_ref, lp_ref, xs_in, rs_in, xs_hbm, rs_hbm, xbuf, rbuf, sem):
    t = pl.program_id(0)
    slot = t % 2
    onehot = _iota((LOCAL_ROWS, TOK_TILE), 0) == lp_ref[0]
    xbuf[slot] = _mm(onehot.astype(BF16), hf_ref[...]).astype(BF16)
    rbuf[slot] = _mm(onehot.astype(F32), rt_ref[...], HI)

    def copies_of(s):
        def copies(local, sorted_row):
            return (pltpu.make_async_copy(xbuf.at[s, pl.ds(local, SEG_BLK)], xs_hbm.at[pl.ds(sorted_row, SEG_BLK)], sem.at[s]),
                    pltpu.make_async_copy(rbuf.at[s, pl.ds(local, SEG_BLK)], rs_hbm.at[pl.ds(sorted_row, SEG_BLK)], sem.at[s]))
        return copies

    @pl.when(t > 0)
    def _():
        _segment_copies(t - 1, nblk, loc_blk, dst_blk, copies_of(1 - slot), lambda cp: cp.wait())

    _segment_copies(t, nblk, loc_blk, dst_blk, copies_of(slot), lambda cp: cp.start())

    @pl.when(t == N_TOK_TILES - 1)
    def _():
        _segment_copies(t, nblk, loc_blk, dst_blk, copies_of(slot), lambda cp: cp.wait())


def _dispatch(hf, route, local_pos, nblk, loc_blk, dst_blk):
    grid_spec = pltpu.PrefetchScalarGridSpec(
        num_scalar_prefetch=3,
        grid=(N_TOK_TILES,),
        in_specs=[pl.BlockSpec((TOK_TILE, D_MODEL), lambda t, *_: (t, 0)),
                  pl.BlockSpec((TOK_TILE, LANES), lambda t, *_: (t, 0)),
                  pl.BlockSpec((1, 1, TOK_TILE), lambda t, *_: (t, 0, 0)),
                  pl.BlockSpec(memory_space=pl.ANY), pl.BlockSpec(memory_space=pl.ANY)],
        out_specs=[pl.BlockSpec(memory_space=pl.ANY), pl.BlockSpec(memory_space=pl.ANY)],
        scratch_shapes=[pltpu.VMEM((2, LOCAL_ROWS, D_MODEL), BF16), pltpu.VMEM((2, LOCAL_ROWS, LANES), F32),
                        pltpu.SemaphoreType.DMA((2,))],
    )
    return pl.pallas_call(
        _dispatch_kernel,
        grid_spec=grid_spec,
        out_shape=[jax.ShapeDtypeStruct((MOE_ROWS, D_MODEL), BF16), jax.ShapeDtypeStruct((MOE_ROWS, LANES), F32)],
        input_output_aliases={6: 0, 7: 1},
        compiler_params=_cparams("arbitrary"),
        name="dispatch",
    )(nblk, loc_blk, dst_blk, hf, route, local_pos.reshape(N_TOK_TILES, 1, TOK_TILE),
      jnp.zeros((MOE_ROWS, D_MODEL), BF16), jnp.zeros((MOE_ROWS, LANES), F32))


def _moe_kernel(tile_group, tile_rows, x_ref, r_ref, wg_hbm, wu_hbm, wd_hbm, y_ref,
                wg_b, wu_b, wd_b, stage_g, stage_u, stage_d, sem, *, layer):
    i = pl.program_id(0)
    g = tile_group[i]
    group_row = layer * N_GROUPS + g
    new_group = (i == 0) | (g != tile_group[jnp.maximum(i - 1, 0)])
    valid = tile_rows[i] > 0
    half = tile_rows[i] <= MOE_TILE // 2

    def weight_copies(e):
        slot = e % 2
        return (pltpu.make_async_copy(wg_hbm.at[group_row, e], stage_g.at[slot], sem.at[slot]),
                pltpu.make_async_copy(wu_hbm.at[group_row, e], stage_u.at[slot], sem.at[slot]),
                pltpu.make_async_copy(wd_hbm.at[group_row, e], stage_d.at[slot], sem.at[slot]))

    def run(load_weights, rows):
        x = x_ref[:rows, :]
        route = r_ref[:rows, :]
        lane = _iota(route.shape, 1)
        acc = jnp.zeros((rows, D_MODEL), F32)
        if load_weights:
            for e in range(2):
                for cp in weight_copies(e):
                    cp.start()
        for e in range(EXPERTS_PER_GROUP):
            if load_weights:
                for cp in weight_copies(e):
                    cp.wait()
                wg_b[e] = stage_g[e % 2].astype(BF16)
                wu_b[e] = stage_u[e % 2].astype(BF16)
                wd_b[e] = stage_d[e % 2].astype(BF16)
                if e + 2 < EXPERTS_PER_GROUP:
                    for cp in weight_copies(e + 2):
                        cp.start()
            cw = jnp.sum(jnp.where(lane == g * EXPERTS_PER_GROUP + e, route, 0.0), axis=-1, keepdims=True)
            act = _silu(_mm(x, wg_b[e])) * _mm(x, wu_b[e]) * cw
            acc = acc + _mm(act.astype(BF16), wd_b[e])
        y_ref[:rows, :] = acc
        if rows < MOE_TILE:
            y_ref[rows:, :] = jnp.zeros((MOE_TILE - rows, D_MODEL), F32)

    for load_weights in (True, False):
        for rows in (MOE_TILE, MOE_TILE // 2):
            first = new_group if load_weights else jnp.logical_not(new_group)
            fits = half if rows < MOE_TILE else jnp.logical_not(half)

            @pl.when(valid & first & fits)
            def _():
                run(load_weights, rows)

    @pl.when(jnp.logical_not(valid))
    def _():
        y_ref[...] = jnp.zeros_like(y_ref)


def _moe(xs, rs, tile_group, tile_rows, wg, wu, wd, layer):
    any_spec = pl.BlockSpec(memory_space=pl.ANY)
    grid_spec = pltpu.PrefetchScalarGridSpec(
        num_scalar_prefetch=2,
        grid=(MOE_ROWS // MOE_TILE,),
        in_specs=[pl.BlockSpec((MOE_TILE, D_MODEL), lambda i, tg, tv: (i, 0)),
                  pl.BlockSpec((MOE_TILE, LANES), lambda i, tg, tv: (i, 0)),
                  any_spec, any_spec, any_spec],
        out_specs=pl.BlockSpec((MOE_TILE, D_MODEL), lambda i, tg, tv: (i, 0)),
        scratch_shapes=[pltpu.VMEM((EXPERTS_PER_GROUP, D_MODEL, EXPERT_FF), BF16),
                        pltpu.VMEM((EXPERTS_PER_GROUP, D_MODEL, EXPERT_FF), BF16),
                        pltpu.VMEM((EXPERTS_PER_GROUP, EXPERT_FF, D_MODEL), BF16),
                        pltpu.VMEM((2, D_MODEL, EXPERT_FF), F32), pltpu.VMEM((2, D_MODEL, EXPERT_FF), F32),
                        pltpu.VMEM((2, EXPERT_FF, D_MODEL), F32), pltpu.SemaphoreType.DMA((2,))],
    )
    return pl.pallas_call(
        functools.partial(_moe_kernel, layer=layer),
        grid_spec=grid_spec,
        out_shape=jax.ShapeDtypeStruct((MOE_ROWS, D_MODEL), F32),
        compiler_params=_cparams("arbitrary"),
        name="moe",
    )(tile_group, tile_rows, xs, rs, wg, wu, wd)


def _combine_kernel(nblk, loc_blk, dst_blk, x_ref, lp_ref, mod_ref, ys_hbm, *refs, split):
    o_refs, ybuf, sem = refs[:-2], refs[-2], refs[-1]
    t = pl.program_id(0)
    slot = t % 2

    def fetch(tile, s):
        def copies(local, sorted_row):
            return (pltpu.make_async_copy(ys_hbm.at[pl.ds(sorted_row, SEG_BLK)], ybuf.at[s, pl.ds(local, SEG_BLK)], sem.at[s]),)
        ybuf[s] = jnp.zeros((LOCAL_ROWS, D_MODEL), F32)
        _segment_copies(tile, nblk, loc_blk, dst_blk, copies, lambda cp: cp.start())

    @pl.when(t == 0)
    def _():
        fetch(0, 0)

    @pl.when(t + 1 < N_TOK_TILES)
    def _():
        fetch(t + 1, 1 - slot)

    def copies_now(local, sorted_row):
        return (pltpu.make_async_copy(ys_hbm.at[pl.ds(sorted_row, SEG_BLK)], ybuf.at[slot, pl.ds(local, SEG_BLK)], sem.at[slot]),)

    _segment_copies(t, nblk, loc_blk, dst_blk, copies_now, lambda cp: cp.wait())

    onehot = (_iota((TOK_TILE, LOCAL_ROWS), 1) == lp_ref[...]).astype(BF16)
    ys = ybuf[slot]
    hi = ys.astype(BF16)
    lo = (ys - hi.astype(F32)).astype(BF16)
    y = _mm(onehot, hi) + _mm(onehot, lo)
    out = x_ref[...] + mod_ref[0, 5:6, :] * y
    if split:
        @pl.when(t < N_CTX_TILES)
        def _():
            o_refs[0][...] = out

        @pl.when(t >= N_CTX_TILES)
        def _():
            o_refs[1][...] = out
    else:
        o_refs[0][...] = out


def _combine(x, ys, mod, local_pos, nblk, loc_blk, dst_blk, split):
    tile = pl.BlockSpec((TOK_TILE, D_MODEL), lambda t, *_: (t, 0))
    if split:
        out_specs = _token_specs(True)
        out_shape = [jax.ShapeDtypeStruct((N_CTX, D_MODEL), F32), jax.ShapeDtypeStruct((N_LAT, D_MODEL), F32)]
    else:
        out_specs, out_shape = [tile], [jax.ShapeDtypeStruct((N_TOK, D_MODEL), F32)]
    grid_spec = pltpu.PrefetchScalarGridSpec(
        num_scalar_prefetch=3,
        grid=(N_TOK_TILES,),
        in_specs=[tile, pl.BlockSpec((TOK_TILE, 1), lambda t, *_: (t, 0)),
                  pl.BlockSpec((1, SUBLANES, D_MODEL), lambda t, *_: (_cond_of_tile(t), 0, 0)),
                  pl.BlockSpec(memory_space=pl.ANY)],
        out_specs=out_specs,
        scratch_shapes=[pltpu.VMEM((2, LOCAL_ROWS, D_MODEL), F32), pltpu.SemaphoreType.DMA((2,))],
    )
    return pl.pallas_call(
        functools.partial(_combine_kernel, split=split),
        grid_spec=grid_spec,
        out_shape=out_shape,
        compiler_params=_cparams("arbitrary"),
        name="combine",
    )(nblk, loc_blk, dst_blk, x, local_pos.reshape(N_TOK, 1), mod, ys)


def _lane_row(v):
    v = v.reshape(-1).astype(F32)
    return jnp.concatenate([v, jnp.zeros((LANES - v.shape[0],), F32)]).reshape(1, LANES)


def _pad_rows(w):
    return jnp.concatenate([w, jnp.zeros((SUBLANES - w.shape[0], w.shape[1]), w.dtype)], axis=0)


def kernel(x_prompt, x_sample, state_gdn, state_ret, cache_nat_k, cache_nat_v, c, c_ctx, ada_w, ada_b, norm_mix_w, norm_ffn_w, w_in, gdn_conv_w, gdn_a_log, gdn_dt_bias, gdn_norm_w, ret_gamma_logit, nat_q_norm_w, nat_k_norm_w, nat_rpb, sc_conv_w, w_out, router_group_w, router_group_b, router_expert_w, router_expert_b, moe_w_gate, moe_w_up, moe_w_down):
    x_ctx, x_lat, split = x_prompt.reshape(N_CTX, D_MODEL), x_sample.reshape(N_LAT, D_MODEL), True
    cond = jnp.concatenate([c_ctx[None, :], c], axis=0)
    ada = _ada(cond, ada_w, ada_b).reshape(DEPTH, SUBLANES, 6, D_MODEL)
    cos, sin = _rope_tables(DEC_SEQ)
    zero_state = jnp.zeros((BATCH, N_GATE, HEAD_DIM, HEAD_DIM), F32)
    lat_block = N_CTX // DEC_SEQ
    gdn_list, ret_list, caches = [], [], None
    mixed = [jnp.zeros((N_TOK, W_GROUP), F32) for _ in range(3)]
    for l in range(DEPTH):
        mod = jnp.concatenate([ada[l, :1 + DEC_BATCH], jnp.zeros((1 + DEC_BATCH, SUBLANES - 6, D_MODEL), F32)], axis=1)
        a_gdn, a_ret, a_nat, a_sc, a_gate = _inproj(x_ctx, x_lat, split, mod, norm_mix_w[l], w_in, l)

        conv_w = _pad_rows(gdn_conv_w[l])
        a_log, dt_b = _lane_row(gdn_a_log[l]), _lane_row(gdn_dt_bias[l])
        gnw = jnp.tile(gdn_norm_w[l], N_HEADS).reshape(1, W_GROUP)
        o_gdn, s_gdn = _gdn(a_gdn, a_gate, conv_w, a_log, dt_b, gnw, zero_state, SEQ, BATCH, 0, n_sub=2,
                            into=mixed[0])
        s0 = state_gdn[:, l].reshape(DEC_BATCH, N_GATE, HEAD_DIM, HEAD_DIM)
        o_gdn, _ = _gdn(a_gdn, a_gate, conv_w, a_log, dt_b, gnw, s0, DEC_SEQ, DEC_BATCH, lat_block, into=o_gdn)

        logit = jnp.repeat(ret_gamma_logit[l].astype(F32), HEAD_DIM, axis=1)
        o_ret, s_ret = _ret(a_ret, logit, zero_state, cos[:SEQ], sin[:SEQ], SEQ, BATCH, 0, False, into=mixed[1])
        s0 = state_ret[:, l].reshape(DEC_BATCH, N_GATE, HEAD_DIM, HEAD_DIM)
        o_ret, _ = _ret(a_ret, logit, s0, cos, sin, DEC_SEQ, DEC_BATCH, lat_block, True, into=o_ret)

        qw = jnp.tile(nat_q_norm_w[l], N_HEADS).reshape(1, W_GROUP)
        kw = jnp.tile(nat_k_norm_w[l], N_HEADS).reshape(1, W_GROUP)
        o_nat, *caches = _ctx_attn(a_nat, qw, kw, mixed[2], l, caches)
        o_nat = _nat(a_nat, cache_nat_k[:, l].reshape(DEC_BATCH, PAST_LEN, W_GROUP),
                     cache_nat_v[:, l].reshape(DEC_BATCH, PAST_LEN, W_GROUP), _nat_tables(nat_rpb[l]), qw, kw, o_nat)

        mixed = [o_gdn, o_ret, o_nat]
        rw, rb = _pack_router(router_expert_w[l], router_expert_b[l], router_group_w[l], router_group_b[l])
        x_mid, hf, route = _outproj(x_ctx, x_lat, split, mixed, a_sc, _pad_rows(sc_conv_w[l]), mod, norm_ffn_w[l],
                                    w_out[l].astype(BF16), rw, rb)

        local_pos, nblk, loc_blk, dst_blk, tile_group, tile_rows = _moe_tables(route[:, GSEL_LANE].astype(jnp.int32))
        xs, rs = _dispatch(hf, route, local_pos, nblk, loc_blk, dst_blk)
        to_group = lambda w: w.reshape((DEPTH * N_GROUPS, EXPERTS_PER_GROUP) + w.shape[2:])
        ys = _moe(xs, rs, tile_group, tile_rows, to_group(moe_w_gate), to_group(moe_w_up), to_group(moe_w_down), l)
        last = l == DEPTH - 1
        out = _combine(x_mid, ys, mod, local_pos, nblk, loc_blk, dst_blk, split=last)
        x_ctx, x_lat, split = (out[0], out[1], True) if last else (out[0], out[0], False)

        gdn_list.append(s_gdn.reshape(BATCH, 2, N_HEADS, HEAD_DIM, HEAD_DIM))
        ret_list.append(s_ret.reshape(BATCH, 2, N_HEADS, HEAD_DIM, HEAD_DIM))
    new_k, new_v = [a.reshape(BATCH, DEPTH, N_HEADS, HEAD_DIM, SEQ).transpose(0, 1, 4, 2, 3) for a in caches]
    return (x_ctx.reshape(BATCH, SEQ, D_MODEL), x_lat.reshape(DEC_BATCH, DEC_SEQ, D_MODEL),
            jnp.stack(gdn_list, axis=1), jnp.stack(ret_list, axis=1), new_k, new_v)
```

```python
import functools

import numpy as np
import jax
import jax.numpy as jnp
from jax import lax
from jax.experimental import pallas as pl
from jax.experimental.pallas import tpu as pltpu

D_MODEL = 1024
BATCH = 16
SEQ = 256
DEPTH = 2
DEC_BATCH = 2
DEC_SEQ = 1024
PAST_LEN = 256
GRID_W = 64
HEAD_DIM = 64
W_GROUP = D_MODEL // 4
N_HEADS = W_GROUP // HEAD_DIM
CHUNK = 64
WIN_ROWS = 8
WIN_COLS = 16
ROPE_BASE = 10000.0
N_GROUPS = 4
EXPERTS_PER_GROUP = 8
N_EXPERTS = N_GROUPS * EXPERTS_PER_GROUP
EXPERT_FF = 256
GROUP_FF = EXPERTS_PER_GROUP * EXPERT_FF
EPS = 1e-6

N_CTX = BATCH * SEQ
N_LAT = DEC_BATCH * DEC_SEQ
N_TOK = N_CTX + N_LAT
LANES = 128
SUBLANES = 8
TOK_TILE = 512
MOE_TILE = 512
Q_TILE = 256
VMEM_LIMIT = 48 * 1024 * 1024
NEG_BIG = -1e30
N_GATE = 2 * N_HEADS

F32 = jnp.float32
BF16 = jnp.bfloat16
HI = lax.Precision.HIGHEST


def _mm(a, b, prec=None):
    return lax.dot_general(a, b, (((1,), (0,)), ((), ())), precision=prec, preferred_element_type=F32)


def _mm_nt(a, b, prec=None):
    return lax.dot_general(a, b, (((1,), (1,)), ((), ())), precision=prec, preferred_element_type=F32)


def _mm_tn(a, b, prec=None):
    return lax.dot_general(a, b, (((0,), (0,)), ((), ())), precision=prec, preferred_element_type=F32)


def _bmm(a, b):
    return _mm(a.astype(BF16), b.astype(BF16))


def _sigmoid(x):
    return 1.0 / (1.0 + jnp.exp(-x))


def _silu(x):
    return x * _sigmoid(x)


def _softplus(x):
    return jnp.maximum(x, 0.0) + jnp.log(1.0 + jnp.exp(-jnp.abs(x)))


def _iota(shape, dim):
    return lax.broadcasted_iota(jnp.int32, shape, dim)


def _cparams(*sem):
    return pltpu.CompilerParams(dimension_semantics=sem, vmem_limit_bytes=VMEM_LIMIT)


def _cond_of_tile(i):
    n_ctx_tiles = N_CTX // TOK_TILE
    return jnp.where(i < n_ctx_tiles, 0, 1 + (i - n_ctx_tiles) // (DEC_SEQ // TOK_TILE))


def _head_sum_matrix():
    return (_iota((W_GROUP, W_GROUP), 0) // HEAD_DIM == _iota((W_GROUP, W_GROUP), 1) // HEAD_DIM).astype(BF16)


def _head_sums(x, head_sum):
    hi = x.astype(BF16)
    lo = (x - hi.astype(F32)).astype(BF16)
    return _mm(hi, head_sum) + _mm(lo, head_sum)


N_COND = 1 + DEC_BATCH
ADA_TN = 1536


def _ada_kernel(c_ref, w_ref, b_ref, o_ref):
    def slab(s, accs):
        rows = pl.ds(pl.multiple_of(s * SUBLANES, SUBLANES), SUBLANES)
        w = w_ref[0, rows, :]
        return tuple(acc + w * jnp.tile(_silu(c_ref[r, rows, :]), (1, ADA_TN // LANES))
                     for r, acc in enumerate(accs))

    zero = jnp.zeros((SUBLANES, ADA_TN), F32)
    accs = lax.fori_loop(0, D_MODEL // SUBLANES, slab, (zero,) * N_COND, unroll=4)
    out = jnp.concatenate([jnp.sum(acc, axis=0, keepdims=True) for acc in accs]
                          + [jnp.zeros((SUBLANES - N_COND, ADA_TN), F32)], axis=0)
    o_ref[0] = out + b_ref[0]


def _ada(cond, ada_w, ada_b):
    n_out = 6 * D_MODEL
    cond_lanes = jnp.broadcast_to(cond[:, :, None], (N_COND, D_MODEL, LANES))
    return pl.pallas_call(
        _ada_kernel,
        grid=(DEPTH, n_out // ADA_TN),
        in_specs=[pl.BlockSpec((N_COND, D_MODEL, LANES), lambda l, j: (0, 0, 0)),
                  pl.BlockSpec((1, D_MODEL, ADA_TN), lambda l, j: (l, 0, j)),
                  pl.BlockSpec((1, 1, ADA_TN), lambda l, j: (l, 0, j))],
        out_specs=pl.BlockSpec((1, SUBLANES, ADA_TN), lambda l, j: (l, 0, j)),
        out_shape=jax.ShapeDtypeStruct((DEPTH, SUBLANES, n_out), F32),
        compiler_params=_cparams("arbitrary", "arbitrary"),
        name="ada",
    )(cond_lanes, ada_w, ada_b.reshape(DEPTH, 1, n_out))


IN_WIDTHS = (4 * W_GROUP, 4 * W_GROUP, 3 * W_GROUP, 3 * W_GROUP, LANES)
IN_PACKED = sum(IN_WIDTHS)


IN_TOTAL = 3 * W_GROUP + W_GROUP + 2 * N_GATE + 4 * W_GROUP + 3 * W_GROUP + 3 * W_GROUP
IN_GATE_SRC = 4 * W_GROUP
IN_SRC = (0, IN_GATE_SRC + 2 * N_GATE, IN_GATE_SRC + 2 * N_GATE + 4 * W_GROUP,
          IN_GATE_SRC + 2 * N_GATE + 7 * W_GROUP)
N_CTX_TILES = N_CTX // TOK_TILE


def _token_specs(split):
    lat0 = 0 if split else N_CTX_TILES
    return [pl.BlockSpec((TOK_TILE, D_MODEL), lambda i, *_: (jnp.minimum(i, N_CTX_TILES - 1), 0)),
            pl.BlockSpec((TOK_TILE, D_MODEL), lambda i, *_: (jnp.maximum(i, N_CTX_TILES) - N_CTX_TILES + lat0, 0))]


def _token_tile(i, ctx_ref, lat_ref):
    return jnp.where(i < N_CTX_TILES, ctx_ref[...], lat_ref[...])


def _inproj_kernel(xc_ref, xl_ref, mod_ref, nw_ref, w_ref, *refs):
    o_refs, w_s = refs[:-1], refs[-1]
    i = pl.program_id(0)

    @pl.when(i == 0)
    def _():
        piece = 256
        off = 0
        for src, width in zip(IN_SRC, IN_WIDTHS[:-1]):
            for c in range(0, width, piece):
                w_s[:, off + c:off + c + piece] = w_ref[0, src + c:src + c + piece, :].T.astype(BF16)
            off += width
        gate = w_ref[0, IN_GATE_SRC:IN_GATE_SRC + LANES, :].T
        w_s[:, off:] = jnp.where(_iota((D_MODEL, LANES), 1) < 2 * N_GATE, gate, 0.0).astype(BF16)

    x = _token_tile(i, xc_ref, xl_ref)
    y = x * lax.rsqrt(jnp.mean(x * x, axis=-1, keepdims=True) + EPS) * nw_ref[...]
    h = (y * (1.0 + mod_ref[0, 1:2, :]) + mod_ref[0, 0:1, :]).astype(BF16)
    off = 0
    for o_ref, width in zip(o_refs, IN_WIDTHS):
        o_ref[...] = _mm(h, w_s[:, off:off + width])
        off += width


def _inproj(x_ctx, x_lat, split, mod, norm_w, w_in, layer):
    return pl.pallas_call(
        _inproj_kernel,
        grid=(N_TOK // TOK_TILE,),
        in_specs=_token_specs(split)
                 + [pl.BlockSpec((1, SUBLANES, D_MODEL), lambda i: (_cond_of_tile(i), 0, 0)),
                    pl.BlockSpec((1, D_MODEL), lambda i: (0, 0)),
                    pl.BlockSpec((1, IN_TOTAL, D_MODEL), lambda i: (layer, 0, 0), pipeline_mode=pl.Buffered(1))],
        out_specs=[pl.BlockSpec((TOK_TILE, w), lambda i: (i, 0)) for w in IN_WIDTHS],
        out_shape=[jax.ShapeDtypeStruct((N_TOK, w), F32) for w in IN_WIDTHS],
        scratch_shapes=[pltpu.VMEM((D_MODEL, IN_PACKED), BF16)],
        compiler_params=_cparams("arbitrary"),
        name="inproj",
    )(x_ctx, x_lat, mod, norm_w.reshape(1, D_MODEL), jnp.swapaxes(w_in, 1, 2))


def _seq_call(kernel_fn, args, in_specs, out_specs, out_shape, into, more_into=(), **kwargs):
    donors = ([] if into is None else [(into, 0)]) + list(more_into)
    n_in = len(args)
    inner = kernel_fn
    kernel_fn = lambda *refs: inner(*refs[:n_in], *refs[n_in + len(donors):])
    aliases = {n_in + j: out_idx for j, (_, out_idx) in enumerate(donors)}
    args = list(args) + [a for a, _ in donors]
    in_specs = list(in_specs) + [pl.BlockSpec(memory_space=pl.ANY)] * len(donors)
    return pl.pallas_call(kernel_fn, in_specs=in_specs, out_specs=out_specs, out_shape=out_shape,
                          input_output_aliases=aliases, **kwargs)(*args)


def _shift_rows(p, seq_len):
    rows = p.shape[0]
    pos = _iota(p.shape, 0) & (seq_len - 1)
    prev = jnp.where(pos == 0, 0.0, pltpu.roll(p, 1, 0))
    nxt = jnp.where(pos == seq_len - 1, 0.0, pltpu.roll(p, rows - 1, 0))
    return prev, nxt


def _conv3(x, w_ref, seq_len):
    prev, nxt = _shift_rows(x, seq_len)
    return w_ref[0:1, :] * prev + w_ref[1:2, :] * x + w_ref[2:3, :] * nxt


def _chunk_scan(x, reverse):
    t = x.shape[0]
    pos = _iota(x.shape, 0) % CHUNK
    step = 1
    while step < CHUNK:
        if reverse:
            x = x + jnp.where(pos < CHUNK - step, pltpu.roll(x, t - step, 0), 0.0)
        else:
            x = x + jnp.where(pos >= step, pltpu.roll(x, step, 0), 0.0)
        step *= 2
    return x


GDN_GROUP_CHUNKS = 4
GDN_CHAINS = GDN_GROUP_CHUNKS * N_GATE
GDN_PAIRS = GDN_GROUP_CHUNKS * N_HEADS


def _gdn_kernel(a_ref, gate_ref, convw_ref, alog_ref, dtb_ref, nw_ref, s0_ref, o_ref, sfin_ref,
                q_s, kv_s, kt_s, gc_s, eg_s, beta_s, gcrow_s, ekdrow_s, cdec_s, uo_s, wq_s, attn_s, kdt_s,
                st_s, wsqs_s, kk_s, d_s, m1_s, p_s, n2_s, ob_s, d4_s, m3_s, rhs_s, px_s, *, t, n_sub):
    n_chunks = t // CHUNK
    a = a_ref[...]
    qkv = _silu(_conv3(a[:, :3 * W_GROUP], convw_ref, t // n_sub))
    q = qkv[:, :W_GROUP]
    k = qkv[:, W_GROUP:2 * W_GROUP]
    v = qkv[:, 2 * W_GROUP:]
    head_sum = _head_sum_matrix()
    q = q * lax.rsqrt(_head_sums(q * q, head_sum) + EPS) * (HEAD_DIM ** -0.5)
    k = k * lax.rsqrt(_head_sums(k * k, head_sum) + EPS)
    for h in range(N_HEADS):
        hs = slice(h * HEAD_DIM, (h + 1) * HEAD_DIM)
        q_s[h] = q[:, hs]
        kv_s[h] = jnp.concatenate([k[:, hs], v[:, hs]], axis=1)
    k_t = k.T
    for c in range(n_chunks):
        kt_s[c] = k_t[:, c * CHUNK:(c + 1) * CHUNK]

    gates = gate_ref[...]
    log_a = -jnp.exp(alog_ref[...]) * _softplus(gates + dtb_ref[...])
    beta_s[...] = _sigmoid(gates)

    ci = _iota((CHUNK, CHUNK), 0)
    cj = _iota((CHUNK, CHUNK), 1)
    eye = (ci == cj).astype(F32)
    blk_mask = (ci // 16) == (cj // 16)
    low_half = _iota((CHUNK, 2 * HEAD_DIM), 1) < HEAD_DIM

    prefix = _chunk_scan(log_a, reverse=False)
    suffix = _chunk_scan(log_a, reverse=True)
    gc = jnp.where(_iota((t, LANES), 1) < N_HEADS, prefix, suffix)
    gt = prefix + suffix - log_a
    gc_s[...] = gc
    eg_s[...] = jnp.exp(gc)
    gc_t = gc.T
    ekd_t = jnp.exp(gt - gc).T
    cdec_t = jnp.exp(gt).T
    for c in range(n_chunks):
        lanes = slice(c * CHUNK, (c + 1) * CHUNK)
        gcrow_s[c] = gc_t[:N_GATE, lanes]
        ekdrow_s[c] = ekd_t[:N_GATE, lanes]
        cdec_s[c] = jnp.concatenate([cdec_t[:N_GATE, lanes]] * 2, axis=1)

    def solve_group(grp, carry):
        row0 = grp * (GDN_GROUP_CHUNKS * CHUNK)
        chains = [(cl, a_idx) for cl in range(GDN_GROUP_CHUNKS) for a_idx in range(N_GATE)]

        def rows_of(cl):
            return pl.ds(pl.multiple_of(row0 + cl * CHUNK, CHUNK), CHUNK)

        for cl in range(GDN_GROUP_CHUNKS):
            for h in range(N_HEADS):
                rows = rows_of(cl)
                kq = jnp.concatenate([kv_s[h, rows, :HEAD_DIM], q_s[h, rows, :]], axis=0)
                k_t_h = kt_s[grp * GDN_GROUP_CHUNKS + cl, h * HEAD_DIM:(h + 1) * HEAD_DIM, :]
                kk_s[cl * N_HEADS + h] = _bmm(kq, k_t_h)
        for b, (cl, a_idx) in enumerate(chains):
            backward = a_idx >= N_HEADS
            h = a_idx % N_HEADS
            rows, c = rows_of(cl), grp * GDN_GROUP_CHUNKS + cl
            incl = (cj >= ci) if backward else (cj <= ci)
            strict = (cj > ci) if backward else (cj < ci)
            bt = beta_s[rows, N_GATE + a_idx:N_GATE + a_idx + 1]
            decay = jnp.exp(jnp.where(incl, gc_s[rows, a_idx:a_idx + 1] - gcrow_s[c, a_idx:a_idx + 1, :], NEG_BIG))
            low = jnp.where(strict, kk_s[cl * N_HEADS + h, :CHUNK, :] * bt * decay, 0.0)
            attn_s[a_idx, rows, :] = (kk_s[cl * N_HEADS + h, CHUNK:, :] * decay).astype(BF16)
            d = jnp.where(blk_mask, low, 0.0)
            d_s[b] = d
            ob_s[b] = (low - d).astype(BF16)
            rhs_s[b] = (kv_s[h, rows, :] * bt
                        * jnp.where(low_half, eg_s[rows, a_idx:a_idx + 1], 1.0)).astype(BF16)
        for b in range(GDN_CHAINS):
            m1_s[b] = _bmm(d_s[b], d_s[b])
        for b in range(GDN_CHAINS):
            d, d2 = d_s[b], m1_s[b]
            d4_s[b] = _bmm(d2, d2).astype(BF16)
            p_s[b] = eye - d + d2 - _bmm(d, d2)
        for b in range(GDN_CHAINS):
            d4, p = d4_s[b], p_s[b]
            m3_s[b] = _mm(d4, d4).astype(BF16)
            p_s[b] = p + _mm(p.astype(BF16), d4)
        for b in range(GDN_CHAINS):
            p = p_s[b]
            p_s[b] = p + _mm(p.astype(BF16), m3_s[b])
        for b in range(GDN_CHAINS):
            p = p_s[b].astype(BF16)
            m1_s[b] = _mm(p, ob_s[b])
            px_s[b] = _mm(p, rhs_s[b]).astype(BF16)
        for b in range(GDN_CHAINS):
            n2_s[b] = _bmm(m1_s[b], m1_s[b])
        for b in range(GDN_CHAINS):
            n, n2 = m1_s[b], n2_s[b]
            m3_s[b] = (eye - n + n2 - _bmm(n, n2)).astype(BF16)
        for b, (cl, a_idx) in enumerate(chains):
            h = a_idx % N_HEADS
            rows, c = rows_of(cl), grp * GDN_GROUP_CHUNKS + cl
            wu = _mm(m3_s[b], px_s[b])
            uo_s[a_idx, rows, :] = wu
            wq_s[a_idx, c, :CHUNK, :] = wu[:, :HEAD_DIM].astype(BF16)
            wq_s[a_idx, c, CHUNK:, :] = (q_s[h, rows, :] * eg_s[rows, a_idx:a_idx + 1]).astype(BF16)
            k_t_h = kt_s[c, h * HEAD_DIM:(h + 1) * HEAD_DIM, :]
            kdt_s[a_idx, c] = (k_t_h * ekdrow_s[c, a_idx:a_idx + 1, :]).astype(BF16)
        return carry

    lax.fori_loop(0, n_chunks // GDN_GROUP_CHUNKS, solve_group, 0)

    seq_chunks = n_chunks // n_sub
    chains = [(sub, a_idx) for sub in range(n_sub) for a_idx in range(N_GATE)]
    for j, (sub, a_idx) in enumerate(chains):
        st_s[j] = jnp.concatenate([jnp.zeros((HEAD_DIM, HEAD_DIM), F32), s0_ref[sub, a_idx]], axis=1)

    def scan_chunk(c, carry):
        def chunk_of(sub, a_idx):
            return sub * seq_chunks + ((seq_chunks - 1 - c) if a_idx >= N_HEADS else c)

        for j, (sub, a_idx) in enumerate(chains):
            wsqs_s[j] = _mm(wq_s[a_idx, chunk_of(sub, a_idx)], st_s[j].astype(BF16))
        for j, (sub, a_idx) in enumerate(chains):
            cc = chunk_of(sub, a_idx)
            rows = pl.ds(pl.multiple_of(cc * CHUNK, CHUNK), CHUNK)
            v_new = (uo_s[a_idx, rows, :] - wsqs_s[j, :CHUNK, :]).astype(BF16)
            uo_s[a_idx, rows, :] = wsqs_s[j, CHUNK:, :] + _mm(attn_s[a_idx, rows, :], v_new)
            st_s[j] = st_s[j] * cdec_s[cc, a_idx:a_idx + 1, :] + _mm(kdt_s[a_idx, cc], v_new)
        return carry

    lax.fori_loop(0, seq_chunks, scan_chunk, 0)
    for j, (sub, a_idx) in enumerate(chains):
        sfin_ref[sub, a_idx] = st_s[j, :, HEAD_DIM:]

    o = jnp.concatenate([(uo_s[h] + uo_s[N_HEADS + h])[:, HEAD_DIM:] for h in range(N_HEADS)], axis=1)
    ms = _head_sums(o * o, head_sum) * (1.0 / HEAD_DIM)
    o_ref[...] = o * lax.rsqrt(ms + EPS) * nw_ref[...] * _silu(a[:, 3 * W_GROUP:])


def _gdn(a_all, gate_all, conv_w, a_log, dt_bias, norm_w, s0, seq_len, n_seq, first_block, n_sub=1, into=None):
    small = lambda: pl.BlockSpec((1, LANES), lambda i: (0, 0))
    t = n_sub * seq_len
    n_chunks = t // CHUNK
    wide = 2 * HEAD_DIM
    scratch = [pltpu.VMEM((N_HEADS, t, HEAD_DIM), F32),
               pltpu.VMEM((N_HEADS, t, wide), F32),
               pltpu.VMEM((n_chunks, W_GROUP, CHUNK), F32),
               pltpu.VMEM((t, LANES), F32), pltpu.VMEM((t, LANES), F32), pltpu.VMEM((t, LANES), F32),
               pltpu.VMEM((n_chunks, N_GATE, CHUNK), F32), pltpu.VMEM((n_chunks, N_GATE, CHUNK), F32),
               pltpu.VMEM((n_chunks, N_GATE, wide), F32),
               pltpu.VMEM((N_GATE, t, wide), F32),
               pltpu.VMEM((N_GATE, n_chunks, 2 * CHUNK, HEAD_DIM), BF16),
               pltpu.VMEM((N_GATE, t, CHUNK), BF16),
               pltpu.VMEM((N_GATE, n_chunks, HEAD_DIM, CHUNK), BF16),
               pltpu.VMEM((n_sub * N_GATE, HEAD_DIM, wide), F32),
               pltpu.VMEM((n_sub * N_GATE, 2 * CHUNK, wide), F32),
               pltpu.VMEM((GDN_PAIRS, 2 * CHUNK, CHUNK), F32)]
    scratch += [pltpu.VMEM((GDN_CHAINS, CHUNK, CHUNK), F32)] * 4
    scratch += [pltpu.VMEM((GDN_CHAINS, CHUNK, CHUNK), BF16)] * 3
    scratch += [pltpu.VMEM((GDN_CHAINS, CHUNK, wide), BF16)] * 2
    return _seq_call(
        functools.partial(_gdn_kernel, t=t, n_sub=n_sub), (a_all, gate_all, conv_w, a_log, dt_bias, norm_w, s0),
        in_specs=[pl.BlockSpec((t, 4 * W_GROUP), lambda i: (i + first_block, 0)),
                  pl.BlockSpec((t, LANES), lambda i: (i + first_block, 0)),
                  pl.BlockSpec((SUBLANES, 3 * W_GROUP), lambda i: (0, 0)),
                  small(), small(),
                  pl.BlockSpec((1, W_GROUP), lambda i: (0, 0)),
                  pl.BlockSpec((n_sub, N_GATE, HEAD_DIM, HEAD_DIM), lambda i: (i, 0, 0, 0))],
        out_specs=[pl.BlockSpec((t, W_GROUP), lambda i: (i + first_block, 0)),
                   pl.BlockSpec((n_sub, N_GATE, HEAD_DIM, HEAD_DIM), lambda i: (i, 0, 0, 0))],
        out_shape=[jax.ShapeDtypeStruct((N_TOK, W_GROUP), F32),
                   jax.ShapeDtypeStruct((n_seq, N_GATE, HEAD_DIM, HEAD_DIM), F32)],
        scratch_shapes=scratch,
        into=into, grid=(n_seq // n_sub,), compiler_params=_cparams("arbitrary"), name="gdn")


def _swap16(x):
    width = x.shape[-1]
    first = (_iota(x.shape, 1) // 16) % 2 == 0
    return jnp.where(first, pltpu.roll(x, width - 16, 1), pltpu.roll(x, 16, 1))


def _block_diag_heads(s0_ref, first):
    zero = jnp.zeros((HEAD_DIM, HEAD_DIM), F32)
    return jnp.concatenate(
        [jnp.concatenate([s0_ref[0, first + h] if j == h else zero for j in range(N_HEADS)], axis=1)
         for h in range(N_HEADS)], axis=0)


def _ret_kernel(r_ref, lg_ref, s0_ref, cos_ref, sin_ref, o_ref, sfin_ref, *, t, latent):
    r = r_ref[...]
    q = r[:, :W_GROUP]
    k = r[:, W_GROUP:2 * W_GROUP]
    v = r[:, 2 * W_GROUP:3 * W_GROUP]
    if latent:
        q = q * cos_ref[...] + _swap16(q) * sin_ref[...]
        k = k * cos_ref[...] + _swap16(k) * sin_ref[...]
    k = k * (HEAD_DIM ** -0.5)
    lg = -_softplus(-lg_ref[...])
    lgf, lgb = lg[0:1, :], lg[1:2, :]
    head = _iota((1, W_GROUP), 1) // HEAD_DIM
    head_sum = _head_sum_matrix()
    pos = _iota((t, 1), 0).astype(F32)
    q_b = q.astype(BF16)
    kt_b = k.T.astype(BF16)
    v_heads = [jnp.where(head == h, v, 0.0).astype(BF16) for h in range(N_HEADS)]
    if latent:
        s0f = _block_diag_heads(s0_ref, 0)
        s0b = _block_diag_heads(s0_ref, N_HEADS)
    for qt in range(t // Q_TILE):
        rows = slice(qt * Q_TILE, (qt + 1) * Q_TILE)
        diff = (_iota((Q_TILE, t), 0) + qt * Q_TILE - _iota((Q_TILE, t), 1)).astype(F32)
        both = jnp.where(diff == 0, 2.0, 1.0)
        o = jnp.zeros((Q_TILE, W_GROUP), F32)
        for h in range(N_HEADS):
            lgf_h = lgf[:, h * HEAD_DIM:h * HEAD_DIM + 1]
            lgb_h = lgb[:, h * HEAD_DIM:h * HEAD_DIM + 1]
            dmat = jnp.exp(diff * jnp.where(diff >= 0, lgf_h, -lgb_h)) * both
            s = _mm(jnp.where(head == h, q_b[rows], 0.0), kt_b) * dmat
            o = o + _mm(s.astype(BF16), v_heads[h])
        if latent:
            p = pos[rows]
            o = o + jnp.exp((p + 1.0) * lgf) * _bmm(q_b[rows], s0f) + jnp.exp((t - p) * lgb) * _bmm(q_b[rows], s0b)
        oc = o - _head_sums(o, head_sum) * (1.0 / HEAD_DIM)
        on = oc * lax.rsqrt(_head_sums(oc * oc, head_sum) * (1.0 / HEAD_DIM) + EPS)
        o_ref[rows, :] = on * _silu(r[rows, 3 * W_GROUP:])
    v_b = v.astype(BF16)
    sf = _mm_tn((k * jnp.exp((t - 1.0 - pos) * lgf)).astype(BF16), v_b)
    sb = _mm_tn((k * jnp.exp(pos * lgb)).astype(BF16), v_b)
    for h in range(N_HEADS):
        hs = slice(h * HEAD_DIM, (h + 1) * HEAD_DIM)
        sf_h, sb_h = sf[hs, hs], sb[hs, hs]
        if latent:
            sf_h = sf_h + jnp.exp(t * lgf[:, h * HEAD_DIM:h * HEAD_DIM + 1]) * s0_ref[0, h]
            sb_h = sb_h + jnp.exp(t * lgb[:, h * HEAD_DIM:h * HEAD_DIM + 1]) * s0_ref[0, N_HEADS + h]
        sfin_ref[0, h] = sf_h
        sfin_ref[0, N_HEADS + h] = sb_h


def _ret(r_all, logit, s0, cos, sin, t, n_seq, first_block, latent, into=None):
    return _seq_call(
        functools.partial(_ret_kernel, t=t, latent=latent), (r_all, logit, s0, cos, sin),
        in_specs=[pl.BlockSpec((t, 4 * W_GROUP), lambda i: (i + first_block, 0)),
                  pl.BlockSpec((2, W_GROUP), lambda i: (0, 0)),
                  pl.BlockSpec((1, N_GATE, HEAD_DIM, HEAD_DIM), lambda i: (i, 0, 0, 0)),
                  pl.BlockSpec((t, W_GROUP), lambda i: (0, 0)),
                  pl.BlockSpec((t, W_GROUP), lambda i: (0, 0))],
        out_specs=[pl.BlockSpec((t, W_GROUP), lambda i: (i + first_block, 0)),
                   pl.BlockSpec((1, N_GATE, HEAD_DIM, HEAD_DIM), lambda i: (i, 0, 0, 0))],
        out_shape=[jax.ShapeDtypeStruct((N_TOK, W_GROUP), F32),
                   jax.ShapeDtypeStruct((n_seq, N_GATE, HEAD_DIM, HEAD_DIM), F32)],
        into=into, grid=(n_seq,), compiler_params=_cparams("arbitrary"), name="ret")


def _rope_tables(t):
    pos = np.arange(t)
    row = (pos // GRID_W).astype(np.float32)
    col = (pos % GRID_W).astype(np.float32)
    nf = HEAD_DIM // 4
    inv_freq = jnp.power(ROPE_BASE, -jnp.arange(nf, dtype=F32) / nf)
    ang_r = jnp.asarray(row)[:, None] * inv_freq[None, :]
    ang_c = jnp.asarray(col)[:, None] * inv_freq[None, :]
    cos = jnp.concatenate([jnp.cos(ang_r)] * 2 + [jnp.cos(ang_c)] * 2, axis=1)
    sin = jnp.concatenate([-jnp.sin(ang_r), jnp.sin(ang_r), -jnp.sin(ang_c), jnp.sin(ang_c)], axis=1)
    return jnp.tile(cos, (1, N_HEADS)), jnp.tile(sin, (1, N_HEADS))


def _head_rms(x, w, head_sum):
    return x * lax.rsqrt(_head_sums(x * x, head_sum) * (1.0 / HEAD_DIM) + EPS) * w


def _ctx_attn_kernel(n_ref, qw_ref, kw_ref, o_ref, k_out, v_out, *, n_fill):
    n = n_ref[...]
    head_sum = _head_sum_matrix()
    head = _iota((1, W_GROUP), 1) // HEAD_DIM
    q = (_head_rms(n[:, :W_GROUP], qw_ref[...], head_sum) * (HEAD_DIM ** -0.5)).astype(BF16)
    k = _head_rms(n[:, W_GROUP:2 * W_GROUP], kw_ref[...], head_sum)
    v = n[:, 2 * W_GROUP:]
    k_t = k.T
    k_out[0, 0] = k_t
    v_out[0, 0] = v.T
    for later in range(1, 1 + n_fill):
        k_out[0, later] = jnp.zeros((W_GROUP, SEQ), F32)
        v_out[0, later] = jnp.zeros((W_GROUP, SEQ), F32)
    kt_b = k_t.astype(BF16)
    o = jnp.zeros((SEQ, W_GROUP), F32)
    for h in range(N_HEADS):
        s = _mm(jnp.where(head == h, q, 0.0), kt_b)
        p = jnp.exp(s - jnp.max(s, axis=-1, keepdims=True))
        o = o + _mm(p.astype(BF16), jnp.where(head == h, v, 0.0).astype(BF16)) / jnp.sum(p, axis=-1, keepdims=True)
    o_ref[...] = o


def _ctx_attn(n_all, qw, kw, into, layer, caches):
    slots = DEPTH if caches is None else 1
    cache_spec = pl.BlockSpec((1, slots, W_GROUP, SEQ), lambda i: (i, layer if caches is not None else 0, 0, 0))
    cache_shape = jax.ShapeDtypeStruct((BATCH, DEPTH, W_GROUP, SEQ), F32)
    return _seq_call(
        functools.partial(_ctx_attn_kernel, n_fill=slots - 1), (n_all, qw, kw),
        in_specs=[pl.BlockSpec((SEQ, 3 * W_GROUP), lambda i: (i, 0)),
                  pl.BlockSpec((1, W_GROUP), lambda i: (0, 0)),
                  pl.BlockSpec((1, W_GROUP), lambda i: (0, 0))],
        out_specs=[pl.BlockSpec((SEQ, W_GROUP), lambda i: (i, 0)), cache_spec, cache_spec],
        out_shape=[jax.ShapeDtypeStruct((N_TOK, W_GROUP), F32), cache_shape, cache_shape],
        into=into, more_into=() if caches is None else ((caches[0], 1), (caches[1], 2)),
        grid=(BATCH,), compiler_params=_cparams("arbitrary"), name="ctx_attn")


def _nat_kernel(n_ref, ck_ref, cv_ref, bias_ref, qw_ref, kw_ref, o_ref, q_s, kt_s, ckt_s):
    h = pl.program_id(1)

    @pl.when(h == 0)
    def _():
        head_sum = _head_sum_matrix()
        n = n_ref[...]
        q_s[...] = (_head_rms(n[:, :W_GROUP], qw_ref[...], head_sum) * (HEAD_DIM ** -0.5)).astype(BF16)
        kt_s[...] = _head_rms(n[:, W_GROUP:2 * W_GROUP], kw_ref[...], head_sum).T.astype(BF16)
        ckt_s[...] = ck_ref[0].T.astype(BF16)
        o_ref[...] = jnp.zeros_like(o_ref)

    mine = _iota((1, W_GROUP), 1) // HEAD_DIM == h
    v = jnp.where(mine, n_ref[:, 2 * W_GROUP:], 0.0).astype(BF16)
    cv = jnp.where(mine, cv_ref[0], 0.0).astype(BF16)
    for qt in range(DEC_SEQ // Q_TILE):
        rows = slice(qt * Q_TILE, (qt + 1) * Q_TILE)
        q = jnp.where(mine, q_s[rows, :], 0.0)
        grid_rows = range(qt * Q_TILE // GRID_W, (qt + 1) * Q_TILE // GRID_W)
        s_loc = _mm(q, kt_s[...]) + jnp.concatenate([_nat_bias_strip(bias_ref, row) for row in grid_rows], axis=0)
        s_ctx = _mm(q, ckt_s[...])
        m = jnp.maximum(jnp.max(s_loc, axis=-1, keepdims=True), jnp.max(s_ctx, axis=-1, keepdims=True))
        p_loc = jnp.exp(s_loc - m)
        p_ctx = jnp.exp(s_ctx - m)
        den = jnp.sum(p_loc, axis=-1, keepdims=True) + jnp.sum(p_ctx, axis=-1, keepdims=True)
        o_ref[rows, :] += (_mm(p_loc.astype(BF16), v) + _mm(p_ctx.astype(BF16), cv)) / den


def _nat(n_all, ck, cv, bias, qw, kw, into):
    first_block = N_CTX // DEC_SEQ
    return _seq_call(
        _nat_kernel, (n_all, ck, cv, bias, qw, kw),
        in_specs=[pl.BlockSpec((DEC_SEQ, 3 * W_GROUP), lambda b, h: (b + first_block, 0)),
                  pl.BlockSpec((1, PAST_LEN, W_GROUP), lambda b, h: (b, 0, 0)),
                  pl.BlockSpec((1, PAST_LEN, W_GROUP), lambda b, h: (b, 0, 0)),
                  pl.BlockSpec((1, 3 * N_ROW_OFF - 1, GRID_W, 2 * GRID_W), lambda b, h: (h, 0, 0, 0)),
                  pl.BlockSpec((1, W_GROUP), lambda b, h: (0, 0)),
                  pl.BlockSpec((1, W_GROUP), lambda b, h: (0, 0))],
        out_specs=pl.BlockSpec((DEC_SEQ, W_GROUP), lambda b, h: (b + first_block, 0)),
        out_shape=jax.ShapeDtypeStruct((N_TOK, W_GROUP), F32),
        scratch_shapes=[pltpu.VMEM((DEC_SEQ, W_GROUP), BF16), pltpu.VMEM((W_GROUP, DEC_SEQ), BF16),
                        pltpu.VMEM((W_GROUP, PAST_LEN), BF16)],
        into=into, grid=(DEC_BATCH, N_HEADS), compiler_params=_cparams("arbitrary", "arbitrary"), name="nat")


N_ROW_OFF = 2 * WIN_ROWS - 1
NAT_ROWS = DEC_SEQ // GRID_W
NAT_KH = min(WIN_ROWS, NAT_ROWS)
NAT_PAIR, NAT_LOW, NAT_HIGH = 0, N_ROW_OFF - 1, 2 * N_ROW_OFF - 1


def _nat_tables(rpb):
    c = np.arange(GRID_W)
    c0 = np.clip(c - WIN_COLS // 2, 0, GRID_W - WIN_COLS)
    col_in = (c[None, :] >= c0[:, None]) & (c[None, :] < c0[:, None] + WIN_COLS)
    col_idx = np.clip(c[None, :] - c[:, None], -(WIN_COLS - 1), WIN_COLS - 1) + WIN_COLS - 1
    col_hot = (col_idx[..., None] == np.arange(2 * WIN_COLS - 1)).astype(np.float32)
    tz = jnp.where(col_in, jnp.einsum('hab,qkb->haqk', rpb.astype(F32), col_hot, precision=HI), NEG_BIG)
    neg = jnp.full_like(tz, NEG_BIG)
    return jnp.concatenate([jnp.concatenate([tz[:, :-1], tz[:, 1:]], axis=-1),
                            jnp.concatenate([tz, neg], axis=-1), jnp.concatenate([neg, tz], axis=-1)], axis=1)


def _nat_bias_strip(tab_ref, row):
    first_key = min(max(row - NAT_KH // 2, 0), NAT_ROWS - NAT_KH)
    off = first_key - row + WIN_ROWS - 1
    tiles = {}
    done, key = 0, first_key
    if key % 2 == 1:
        tiles[key // 2] = tab_ref[0, NAT_HIGH + off]
        done, key = 1, key + 1
    while done + 1 < NAT_KH:
        tiles[key // 2] = tab_ref[0, NAT_PAIR + off + done]
        done, key = done + 2, key + 2
    if done < NAT_KH:
        tiles[key // 2] = tab_ref[0, NAT_LOW + off + done]
    outside = jnp.full((GRID_W, 2 * GRID_W), NEG_BIG, F32)
    return jnp.concatenate([tiles.get(i, outside) for i in range(NAT_ROWS // 2)], axis=1)


GSEL_LANE = N_EXPERTS


def _pack_router(we, be, wg, bg):
    pad = LANES - N_EXPERTS - N_GROUPS
    w = jnp.concatenate([we, wg, jnp.zeros((D_MODEL, pad), F32)], axis=1)
    b = jnp.concatenate([be, bg, jnp.zeros((pad,), F32)]).reshape(1, LANES)
    hi = w.astype(BF16)
    lo = (w - hi.astype(F32)).astype(BF16)
    return jnp.concatenate([hi, lo], axis=1), b


def _lane_min_where(mask, lane):
    return jnp.min(jnp.where(mask, lane, LANES), axis=-1, keepdims=True)


def _sconv_tile(i, s_ref, before_ref, after_ref, w_ref):
    s = s_ref[...]
    p = s[:, W_GROUP:2 * W_GROUP] * s[:, 2 * W_GROUP:]
    p_before = before_ref[SUBLANES - 1:, W_GROUP:2 * W_GROUP] * before_ref[SUBLANES - 1:, 2 * W_GROUP:]
    p_after = after_ref[:1, W_GROUP:2 * W_GROUP] * after_ref[:1, 2 * W_GROUP:]
    row = _iota((TOK_TILE, 1), 0)
    seq_len = jnp.where(i < N_CTX_TILES, SEQ, DEC_SEQ)
    pos = (i * TOK_TILE + row) & (seq_len - 1)
    prev = jnp.where(row == 0, p_before, pltpu.roll(p, 1, 0))
    nxt = jnp.where(row == TOK_TILE - 1, p_after, pltpu.roll(p, TOK_TILE - 1, 0))
    prev = jnp.where(pos == 0, 0.0, prev)
    nxt = jnp.where(pos == seq_len - 1, 0.0, nxt)
    return s[:, :W_GROUP] * (w_ref[0:1, :] * prev + w_ref[1:2, :] * p + w_ref[2:3, :] * nxt)


def _outproj_kernel(xc_ref, xl_ref, m0, m1, m2, s_ref, before_ref, after_ref, cw_ref, mod_ref, nw_ref, w_ref,
                    rw_ref, rb_ref, x_out, hf_out, route_out, w_s):
    tile_i = pl.program_id(0)

    @pl.when(tile_i == 0)
    def _():
        w_s[...] = w_ref[0].astype(BF16)

    mixed = [m0[...], m1[...], m2[...], _sconv_tile(tile_i, s_ref, before_ref, after_ref, cw_ref)]
    acc = None
    for i, m in enumerate(mixed):
        part = _mm(m.astype(BF16), w_s[i * W_GROUP:(i + 1) * W_GROUP, :])
        acc = part if acc is None else acc + part
    x = _token_tile(tile_i, xc_ref, xl_ref) + mod_ref[0, 2:3, :] * acc
    x_out[...] = x
    y = x * lax.rsqrt(jnp.mean(x * x, axis=-1, keepdims=True) + EPS) * nw_ref[...]
    hf = y * (1.0 + mod_ref[0, 4:5, :]) + mod_ref[0, 3:4, :]
    hf_hi = hf.astype(BF16)
    hf_out[...] = hf_hi

    hf_lo = (hf - hf_hi.astype(F32)).astype(BF16)
    both = _mm(hf_hi, rw_ref[...])
    logits = both[:, :LANES] + both[:, LANES:] + _mm(hf_lo, rw_ref[:, :LANES]) + rb_ref[...]
    lane = _iota(logits.shape, 1)
    is_g = (lane >= N_EXPERTS) & (lane < N_EXPERTS + N_GROUPS)
    gl = jnp.where(is_g, logits, NEG_BIG)
    ge = jnp.exp(gl - jnp.max(gl, axis=-1, keepdims=True))
    gp = jnp.where(is_g, ge / jnp.sum(ge, axis=-1, keepdims=True), -1.0)
    gw = jnp.max(gp, axis=-1, keepdims=True)
    gsel = _lane_min_where(gp == gw, lane) - N_EXPERTS
    in_grp = (lane // EXPERTS_PER_GROUP == gsel) & (lane < N_EXPERTS)
    el = jnp.where(in_grp, logits, NEG_BIG)
    ee = jnp.exp(el - jnp.max(el, axis=-1, keepdims=True))
    ep = jnp.where(in_grp, ee / jnp.sum(ee, axis=-1, keepdims=True), -1.0)
    t1 = jnp.max(ep, axis=-1, keepdims=True)
    i1 = _lane_min_where(ep == t1, lane)
    ep2 = jnp.where(lane == i1, -1.0, ep)
    t2 = jnp.max(ep2, axis=-1, keepdims=True)
    i2 = _lane_min_where(ep2 == t2, lane)
    tsum = t1 + t2
    combine = jnp.where(lane == i1, gw * (t1 / tsum), 0.0) + jnp.where(lane == i2, gw * (t2 / tsum), 0.0)
    route_out[...] = jnp.where(lane == GSEL_LANE, gsel.astype(F32), combine)


def _outproj(x_ctx, x_lat, split, mixed, a_sc, sc_w, mod, norm_w, w_out, layer, rw, rb):
    tile = lambda w: pl.BlockSpec((TOK_TILE, w), lambda i: (i, 0))
    whole = lambda a: pl.BlockSpec(a.shape, lambda i: (0,) * a.ndim)
    per_tile = TOK_TILE // SUBLANES
    halo = lambda index: pl.BlockSpec((SUBLANES, 3 * W_GROUP), lambda i: (index(i), 0))
    return pl.pallas_call(
        _outproj_kernel,
        grid=(N_TOK // TOK_TILE,),
        in_specs=_token_specs(split) + [tile(W_GROUP)] * 3
                 + [tile(3 * W_GROUP), halo(lambda i: jnp.maximum(i * per_tile - 1, 0)),
                    halo(lambda i: jnp.minimum((i + 1) * per_tile, N_TOK // SUBLANES - 1)), whole(sc_w)]
                 + [pl.BlockSpec((1, SUBLANES, D_MODEL), lambda i: (_cond_of_tile(i), 0, 0)),
                    pl.BlockSpec((1, D_MODEL), lambda i: (0, 0)),
                    pl.BlockSpec((1, D_MODEL, D_MODEL), lambda i: (layer, 0, 0), pipeline_mode=pl.Buffered(1)),
                    whole(rw), whole(rb)],
        out_specs=[tile(D_MODEL), tile(D_MODEL), tile(LANES)],
        out_shape=[jax.ShapeDtypeStruct((N_TOK, D_MODEL), F32), jax.ShapeDtypeStruct((N_TOK, D_MODEL), BF16),
                   jax.ShapeDtypeStruct((N_TOK, LANES), F32)],
        scratch_shapes=[pltpu.VMEM((D_MODEL, D_MODEL), BF16)],
        compiler_params=_cparams("arbitrary"),
        name="outproj",
    )(x_ctx, x_lat, *mixed, a_sc, a_sc, a_sc, sc_w, mod, norm_w.reshape(1, D_MODEL), w_out, rw, rb)


SEG_BLK = 16
LOCAL_ROWS = TOK_TILE + N_GROUPS * SEG_BLK
N_TOK_TILES = N_TOK // TOK_TILE
MOE_ROWS = -(-(N_TOK + N_TOK_TILES * N_GROUPS * (SEG_BLK - 1) + N_GROUPS * (MOE_TILE - 1)) // MOE_TILE) * MOE_TILE


def _moe_tables(gsel):
    groups = jnp.arange(N_GROUPS, dtype=jnp.int32)
    onehot = (gsel.reshape(N_TOK_TILES, TOK_TILE, 1) == groups).astype(jnp.int32)
    earlier = jnp.asarray(np.tril(np.ones((TOK_TILE, TOK_TILE), np.float32), -1))
    rank = jnp.einsum('ts,nsg->ntg', earlier, onehot.astype(F32)).astype(jnp.int32)
    nblk = (jnp.sum(onehot, axis=1) + SEG_BLK - 1) // SEG_BLK
    loc_blk = jnp.cumsum(nblk, axis=1) - nblk
    blocks_per_tile = MOE_TILE // SEG_BLK
    grp_tiles = (jnp.sum(nblk, axis=0) + blocks_per_tile - 1) // blocks_per_tile
    grp_tile_start = jnp.cumsum(grp_tiles) - grp_tiles
    dst_blk = grp_tile_start[None, :] * blocks_per_tile + jnp.cumsum(nblk, axis=0) - nblk
    local_pos = jnp.sum(onehot * (loc_blk[:, None, :] * SEG_BLK + rank), axis=2)
    tile_idx = jnp.arange(MOE_ROWS // MOE_TILE, dtype=jnp.int32)
    tile_group = jnp.clip(jnp.sum(tile_idx[:, None] >= grp_tile_start[None, :], axis=1) - 1, 0, N_GROUPS - 1)
    in_group = tile_idx - grp_tile_start[tile_group]
    tile_rows = jnp.clip(jnp.sum(nblk, axis=0)[tile_group] * SEG_BLK - in_group * MOE_TILE, 0, MOE_TILE)
    flat = lambda a: a.reshape(-1).astype(jnp.int32)
    return local_pos.astype(jnp.int32), flat(nblk), flat(loc_blk), flat(dst_blk), flat(tile_group), flat(tile_rows)


def _segment_copies(t, nblk, loc_blk, dst_blk, make_copies, action):
    for g in range(N_GROUPS):
        k = t * N_GROUPS + g

        @pl.loop(0, nblk[k])
        def _(b):
            local = pl.multiple_of((loc_blk[k] + b) * SEG_BLK, SEG_BLK)
            sorted_row = pl.multiple_of((dst_blk[k] + b) * SEG_BLK, SEG_BLK)
            for cp in make_copies(local, sorted_row):
                action(cp)


def _dispatch_kernel(nblk, loc_blk, dst_blk, hf_ref, rt_ref, lp_ref, xs_in, rs_in, xs_hbm, rs_hbm, xbuf, rbuf, sem):
    t = pl.program_id(0)
    slot = t % 2
    onehot = _iota((LOCAL_ROWS, TOK_TILE), 0) == lp_ref[0]
    xbuf[slot] = _mm(onehot.astype(BF16), hf_ref[...]).astype(BF16)
    rbuf[slot] = _mm(onehot.astype(F32), rt_ref[...], HI)

    def copies_of(s):
        def copies(local, sorted_row):
            return (pltpu.make_async_copy(xbuf.at[s, pl.ds(local, SEG_BLK)], xs_hbm.at[pl.ds(sorted_row, SEG_BLK)], sem.at[s]),
                    pltpu.make_async_copy(rbuf.at[s, pl.ds(local, SEG_BLK)], rs_hbm.at[pl.ds(sorted_row, SEG_BLK)], sem.at[s]))
        return copies

    @pl.when(t > 0)
    def _():
        _segment_copies(t - 1, nblk, loc_blk, dst_blk, copies_of(1 - slot), lambda cp: cp.wait())

    _segment_copies(t, nblk, loc_blk, dst_blk, copies_of(slot), lambda cp: cp.start())

    @pl.when(t == N_TOK_TILES - 1)
    def _():
        _segment_copies(t, nblk, loc_blk, dst_blk, copies_of(slot), lambda cp: cp.wait())


def _dispatch(hf, route, local_pos, nblk, loc_blk, dst_blk):
    grid_spec = pltpu.PrefetchScalarGridSpec(
        num_scalar_prefetch=3,
        grid=(N_TOK_TILES,),
        in_specs=[pl.BlockSpec((TOK_TILE, D_MODEL), lambda t, *_: (t, 0)),
                  pl.BlockSpec((TOK_TILE, LANES), lambda t, *_: (t, 0)),
                  pl.BlockSpec((1, 1, TOK_TILE), lambda t, *_: (t, 0, 0)),
                  pl.BlockSpec(memory_space=pl.ANY), pl.BlockSpec(memory_space=pl.ANY)],
        out_specs=[pl.BlockSpec(memory_space=pl.ANY), pl.BlockSpec(memory_space=pl.ANY)],
        scratch_shapes=[pltpu.VMEM((2, LOCAL_ROWS, D_MODEL), BF16), pltpu.VMEM((2, LOCAL_ROWS, LANES), F32),
                        pltpu.SemaphoreType.DMA((2,))],
    )
    return pl.pallas_call(
        _dispatch_kernel,
        grid_spec=grid_spec,
        out_shape=[jax.ShapeDtypeStruct((MOE_ROWS, D_MODEL), BF16), jax.ShapeDtypeStruct((MOE_ROWS, LANES), F32)],
        input_output_aliases={6: 0, 7: 1},
        compiler_params=_cparams("arbitrary"),
        name="dispatch",
    )(nblk, loc_blk, dst_blk, hf, route, local_pos.reshape(N_TOK_TILES, 1, TOK_TILE),
      jnp.zeros((MOE_ROWS, D_MODEL), BF16), jnp.zeros((MOE_ROWS, LANES), F32))


def _moe_kernel(tile_group, tile_rows, x_ref, r_ref, wg_hbm, wu_hbm, wd_hbm, y_ref,
                wg_b, wu_b, wd_b, stage_g, stage_u, stage_d, sem, *, layer):
    i = pl.program_id(0)
    g = tile_group[i]
    group_row = layer * N_GROUPS + g
    new_group = (i == 0) | (g != tile_group[jnp.maximum(i - 1, 0)])
    valid = tile_rows[i] > 0
    half = tile_rows[i] <= MOE_TILE // 2

    def weight_copies(e):
        slot = e % 2
        return (pltpu.make_async_copy(wg_hbm.at[group_row, e], stage_g.at[slot], sem.at[slot]),
                pltpu.make_async_copy(wu_hbm.at[group_row, e], stage_u.at[slot], sem.at[slot]),
                pltpu.make_async_copy(wd_hbm.at[group_row, e], stage_d.at[slot], sem.at[slot]))

    def run(load_weights, rows):
        x = x_ref[:rows, :]
        route = r_ref[:rows, :]
        lane = _iota(route.shape, 1)
        acc = jnp.zeros((rows, D_MODEL), F32)
        if load_weights:
            for e in range(2):
                for cp in weight_copies(e):
                    cp.start()
        for e in range(EXPERTS_PER_GROUP):
            if load_weights:
                for cp in weight_copies(e):
                    cp.wait()
                wg_b[e] = stage_g[e % 2].astype(BF16)
                wu_b[e] = stage_u[e % 2].astype(BF16)
                wd_b[e] = stage_d[e % 2].astype(BF16)
                if e + 2 < EXPERTS_PER_GROUP:
                    for cp in weight_copies(e + 2):
                        cp.start()
            cw = jnp.sum(jnp.where(lane == g * EXPERTS_PER_GROUP + e, route, 0.0), axis=-1, keepdims=True)
            act = _silu(_mm(x, wg_b[e])) * _mm(x, wu_b[e]) * cw
            acc = acc + _mm(act.astype(BF16), wd_b[e])
        y_ref[:rows, :] = acc
        if rows < MOE_TILE:
            y_ref[rows:, :] = jnp.zeros((MOE_TILE - rows, D_MODEL), F32)

    for load_weights in (True, False):
        for rows in (MOE_TILE, MOE_TILE // 2):
            first = new_group if load_weights else jnp.logical_not(new_group)
            fits = half if rows < MOE_TILE else jnp.logical_not(half)

            @pl.when(valid & first & fits)
            def _():
                run(load_weights, rows)

    @pl.when(jnp.logical_not(valid))
    def _():
        y_ref[...] = jnp.zeros_like(y_ref)


def _moe(xs, rs, tile_group, tile_rows, wg, wu, wd, layer):
    any_spec = pl.BlockSpec(memory_space=pl.ANY)
    grid_spec = pltpu.PrefetchScalarGridSpec(
        num_scalar_prefetch=2,
        grid=(MOE_ROWS // MOE_TILE,),
        in_specs=[pl.BlockSpec((MOE_TILE, D_MODEL), lambda i, tg, tv: (i, 0)),
                  pl.BlockSpec((MOE_TILE, LANES), lambda i, tg, tv: (i, 0)),
                  any_spec, any_spec, any_spec],
        out_specs=pl.BlockSpec((MOE_TILE, D_MODEL), lambda i, tg, tv: (i, 0)),
        scratch_shapes=[pltpu.VMEM((EXPERTS_PER_GROUP, D_MODEL, EXPERT_FF), BF16),
                        pltpu.VMEM((EXPERTS_PER_GROUP, D_MODEL, EXPERT_FF), BF16),
                        pltpu.VMEM((EXPERTS_PER_GROUP, EXPERT_FF, D_MODEL), BF16),
                        pltpu.VMEM((2, D_MODEL, EXPERT_FF), F32), pltpu.VMEM((2, D_MODEL, EXPERT_FF), F32),
                        pltpu.VMEM((2, EXPERT_FF, D_MODEL), F32), pltpu.SemaphoreType.DMA((2,))],
    )
    return pl.pallas_call(
        functools.partial(_moe_kernel, layer=layer),
        grid_spec=grid_spec,
        out_shape=jax.ShapeDtypeStruct((MOE_ROWS, D_MODEL), F32),
        compiler_params=_cparams("arbitrary"),
        name="moe",
    )(tile_group, tile_rows, xs, rs, wg, wu, wd)


def _combine_kernel(nblk, loc_blk, dst_blk, x_ref, lp_ref, mod_ref, ys_hbm, *refs, split):
    o_refs, ybuf, sem = refs[:-2], refs[-2], refs[-1]
    t = pl.program_id(0)
    slot = t % 2

    def fetch(tile, s):
        def copies(local, sorted_row):
            return (pltpu.make_async_copy(ys_hbm.at[pl.ds(sorted_row, SEG_BLK)], ybuf.at[s, pl.ds(local, SEG_BLK)], sem.at[s]),)
        ybuf[s] = jnp.zeros((LOCAL_ROWS, D_MODEL), F32)
        _segment_copies(tile, nblk, loc_blk, dst_blk, copies, lambda cp: cp.start())

    @pl.when(t == 0)
    def _():
        fetch(0, 0)

    @pl.when(t + 1 < N_TOK_TILES)
    def _():
        fetch(t + 1, 1 - slot)

    def copies_now(local, sorted_row):
        return (pltpu.make_async_copy(ys_hbm.at[pl.ds(sorted_row, SEG_BLK)], ybuf.at[slot, pl.ds(local, SEG_BLK)], sem.at[slot]),)

    _segment_copies(t, nblk, loc_blk, dst_blk, copies_now, lambda cp: cp.wait())

    onehot = (_iota((TOK_TILE, LOCAL_ROWS), 1) == lp_ref[...]).astype(BF16)
    ys = ybuf[slot]
    hi = ys.astype(BF16)
    lo = (ys - hi.astype(F32)).astype(BF16)
    y = _mm(onehot, hi) + _mm(onehot, lo)
    out = x_ref[...] + mod_ref[0, 5:6, :] * y
    if split:
        @pl.when(t < N_CTX_TILES)
        def _():
            o_refs[0][...] = out

        @pl.when(t >= N_CTX_TILES)
        def _():
            o_refs[1][...] = out
    else:
        o_refs[0][...] = out


def _combine(x, ys, mod, local_pos, nblk, loc_blk, dst_blk, split):
    tile = pl.BlockSpec((TOK_TILE, D_MODEL), lambda t, *_: (t, 0))
    if split:
        out_specs = _token_specs(True)
        out_shape = [jax.ShapeDtypeStruct((N_CTX, D_MODEL), F32), jax.ShapeDtypeStruct((N_LAT, D_MODEL), F32)]
    else:
        out_specs, out_shape = [tile], [jax.ShapeDtypeStruct((N_TOK, D_MODEL), F32)]
    grid_spec = pltpu.PrefetchScalarGridSpec(
        num_scalar_prefetch=3,
        grid=(N_TOK_TILES,),
        in_specs=[tile, pl.BlockSpec((TOK_TILE, 1), lambda t, *_: (t, 0)),
                  pl.BlockSpec((1, SUBLANES, D_MODEL), lambda t, *_: (_cond_of_tile(t), 0, 0)),
                  pl.BlockSpec(memory_space=pl.ANY)],
        out_specs=out_specs,
        scratch_shapes=[pltpu.VMEM((2, LOCAL_ROWS, D_MODEL), F32), pltpu.SemaphoreType.DMA((2,))],
    )
    return pl.pallas_call(
        functools.partial(_combine_kernel, split=split),
        grid_spec=grid_spec,
        out_shape=out_shape,
        compiler_params=_cparams("arbitrary"),
        name="combine",
    )(nblk, loc_blk, dst_blk, x, local_pos.reshape(N_TOK, 1), mod, ys)


def _lane_row(v):
    v = v.reshape(-1).astype(F32)
    return jnp.concatenate([v, jnp.zeros((LANES - v.shape[0],), F32)]).reshape(1, LANES)


def _pad_rows(w):
    return jnp.concatenate([w, jnp.zeros((SUBLANES - w.shape[0], w.shape[1]), w.dtype)], axis=0)


def kernel(x_prompt, x_sample, state_gdn, state_ret, cache_nat_k, cache_nat_v, c, c_ctx, ada_w, ada_b, norm_mix_w, norm_ffn_w, w_in, gdn_conv_w, gdn_a_log, gdn_dt_bias, gdn_norm_w, ret_gamma_logit, nat_q_norm_w, nat_k_norm_w, nat_rpb, sc_conv_w, w_out, router_group_w, router_group_b, router_expert_w, router_expert_b, moe_w_gate, moe_w_up, moe_w_down):
    x_ctx, x_lat, split = x_prompt.reshape(N_CTX, D_MODEL), x_sample.reshape(N_LAT, D_MODEL), True
    cond = jnp.concatenate([c_ctx[None, :], c], axis=0)
    ada = _ada(cond, ada_w, ada_b).reshape(DEPTH, SUBLANES, 6, D_MODEL)
    cos, sin = _rope_tables(DEC_SEQ)
    zero_state = jnp.zeros((BATCH, N_GATE, HEAD_DIM, HEAD_DIM), F32)
    lat_block = N_CTX // DEC_SEQ
    gdn_list, ret_list, caches = [], [], None
    mixed = [jnp.zeros((N_TOK, W_GROUP), F32) for _ in range(3)]
    for l in range(DEPTH):
        mod = jnp.concatenate([ada[l, :1 + DEC_BATCH], jnp.zeros((1 + DEC_BATCH, SUBLANES - 6, D_MODEL), F32)], axis=1)
        a_gdn, a_ret, a_nat, a_sc, a_gate = _inproj(x_ctx, x_lat, split, mod, norm_mix_w[l], w_in, l)

        conv_w = _pad_rows(gdn_conv_w[l])
        a_log, dt_b = _lane_row(gdn_a_log[l]), _lane_row(gdn_dt_bias[l])
        gnw = jnp.tile(gdn_norm_w[l], N_HEADS).reshape(1, W_GROUP)
        o_gdn, s_gdn = _gdn(a_gdn, a_gate, conv_w, a_log, dt_b, gnw, zero_state, SEQ, BATCH, 0, n_sub=2,
                            into=mixed[0])
        s0 = state_gdn[:, l].reshape(DEC_BATCH, N_GATE, HEAD_DIM, HEAD_DIM)
        o_gdn, _ = _gdn(a_gdn, a_gate, conv_w, a_log, dt_b, gnw, s0, DEC_SEQ, DEC_BATCH, lat_block, into=o_gdn)

        logit = jnp.repeat(ret_gamma_logit[l].astype(F32), HEAD_DIM, axis=1)
        o_ret, s_ret = _ret(a_ret, logit, zero_state, cos[:SEQ], sin[:SEQ], SEQ, BATCH, 0, False, into=mixed[1])
        s0 = state_ret[:, l].reshape(DEC_BATCH, N_GATE, HEAD_DIM, HEAD_DIM)
        o_ret, _ = _ret(a_ret, logit, s0, cos, sin, DEC_SEQ, DEC_BATCH, lat_block, True, into=o_ret)

        qw = jnp.tile(nat_q_norm_w[l], N_HEADS).reshape(1, W_GROUP)
        kw = jnp.tile(nat_k_norm_w[l], N_HEADS).reshape(1, W_GROUP)
        o_nat, *caches = _ctx_attn(a_nat, qw, kw, mixed[2], l, caches)
        o_nat = _nat(a_nat, cache_nat_k[:, l].reshape(DEC_BATCH, PAST_LEN, W_GROUP),
                     cache_nat_v[:, l].reshape(DEC_BATCH, PAST_LEN, W_GROUP), _nat_tables(nat_rpb[l]), qw, kw, o_nat)

        mixed = [o_gdn, o_ret, o_nat]
        rw, rb = _pack_router(router_expert_w[l], router_expert_b[l], router_group_w[l], router_group_b[l])
        x_mid, hf, route = _outproj(x_ctx, x_lat, split, mixed, a_sc, _pad_rows(sc_conv_w[l]), mod, norm_ffn_w[l],
                                    w_out, l, rw, rb)

        local_pos, nblk, loc_blk, dst_blk, tile_group, tile_rows = _moe_tables(route[:, GSEL_LANE].astype(jnp.int32))
        xs, rs = _dispatch(hf, route, local_pos, nblk, loc_blk, dst_blk)
        to_group = lambda w: w.reshape((DEPTH * N_GROUPS, EXPERTS_PER_GROUP) + w.shape[2:])
        ys = _moe(xs, rs, tile_group, tile_rows, to_group(moe_w_gate), to_group(moe_w_up), to_group(moe_w_down), l)
        last = l == DEPTH - 1
        out = _combine(x_mid, ys, mod, local_pos, nblk, loc_blk, dst_blk, split=last)
        x_ctx, x_lat, split = (out[0], out[1], True) if last else (out[0], out[0], False)

        gdn_list.append(s_gdn.reshape(BATCH, 2, N_HEADS, HEAD_DIM, HEAD_DIM))
        ret_list.append(s_ret.reshape(BATCH, 2, N_HEADS, HEAD_DIM, HEAD_DIM))
    new_k, new_v = [a.reshape(BATCH, DEPTH, N_HEADS, HEAD_DIM, SEQ).transpose(0, 1, 4, 2, 3) for a in caches]
    return (x_ctx.reshape(BATCH, SEQ, D_MODEL), x_lat.reshape(DEC_BATCH, DEC_SEQ, D_MODEL),
            jnp.stack(gdn_list, axis=1), jnp.stack(ret_list, axis=1), new_k, new_v)
```

```python
import functools

import numpy as np
import jax
import jax.numpy as jnp
from jax import lax
from jax.experimental import pallas as pl
from jax.experimental.pallas import tpu as pltpu

D_MODEL = 1024
BATCH = 16
SEQ = 256
DEPTH = 2
DEC_BATCH = 2
DEC_SEQ = 1024
PAST_LEN = 256
GRID_W = 64
HEAD_DIM = 64
W_GROUP = D_MODEL // 4
N_HEADS = W_GROUP // HEAD_DIM
CHUNK = 64
WIN_ROWS = 8
WIN_COLS = 16
ROPE_BASE = 10000.0
N_GROUPS = 4
EXPERTS_PER_GROUP = 8
N_EXPERTS = N_GROUPS * EXPERTS_PER_GROUP
EXPERT_FF = 256
GROUP_FF = EXPERTS_PER_GROUP * EXPERT_FF
EPS = 1e-6

N_CTX = BATCH * SEQ
N_LAT = DEC_BATCH * DEC_SEQ
N_TOK = N_CTX + N_LAT
LANES = 128
SUBLANES = 8
TOK_TILE = 512
MOE_TILE = 512
Q_TILE = 256
VMEM_LIMIT = 48 * 1024 * 1024
NEG_BIG = -1e30
N_GATE = 2 * N_HEADS

F32 = jnp.float32
BF16 = jnp.bfloat16
HI = lax.Precision.HIGHEST


def _mm(a, b, prec=None):
    return lax.dot_general(a, b, (((1,), (0,)), ((), ())), precision=prec, preferred_element_type=F32)


def _mm_nt(a, b, prec=None):
    return lax.dot_general(a, b, (((1,), (1,)), ((), ())), precision=prec, preferred_element_type=F32)


def _mm_tn(a, b, prec=None):
    return lax.dot_general(a, b, (((0,), (0,)), ((), ())), precision=prec, preferred_element_type=F32)


def _bmm(a, b):
    return _mm(a.astype(BF16), b.astype(BF16))


def _sigmoid(x):
    return 1.0 / (1.0 + jnp.exp(-x))


def _silu(x):
    return x * _sigmoid(x)


def _softplus(x):
    return jnp.maximum(x, 0.0) + jnp.log(1.0 + jnp.exp(-jnp.abs(x)))


def _iota(shape, dim):
    return lax.broadcasted_iota(jnp.int32, shape, dim)


def _cparams(*sem):
    return pltpu.CompilerParams(dimension_semantics=sem, vmem_limit_bytes=VMEM_LIMIT)


def _cond_of_tile(i):
    n_ctx_tiles = N_CTX // TOK_TILE
    return jnp.where(i < n_ctx_tiles, 0, 1 + (i - n_ctx_tiles) // (DEC_SEQ // TOK_TILE))


def _head_sum_matrix():
    return (_iota((W_GROUP, W_GROUP), 0) // HEAD_DIM == _iota((W_GROUP, W_GROUP), 1) // HEAD_DIM).astype(BF16)


def _head_sums(x, head_sum):
    hi = x.astype(BF16)
    lo = (x - hi.astype(F32)).astype(BF16)
    return _mm(hi, head_sum) + _mm(lo, head_sum)


N_COND = 1 + DEC_BATCH
ADA_TN = 1536


def _ada_kernel(c_ref, w_ref, b_ref, o_ref):
    def slab(s, accs):
        rows = pl.ds(pl.multiple_of(s * SUBLANES, SUBLANES), SUBLANES)
        w = w_ref[0, rows, :]
        return tuple(acc + w * jnp.tile(_silu(c_ref[r, rows, :]), (1, ADA_TN // LANES))
                     for r, acc in enumerate(accs))

    zero = jnp.zeros((SUBLANES, ADA_TN), F32)
    accs = lax.fori_loop(0, D_MODEL // SUBLANES, slab, (zero,) * N_COND, unroll=4)
    out = jnp.concatenate([jnp.sum(acc, axis=0, keepdims=True) for acc in accs]
                          + [jnp.zeros((SUBLANES - N_COND, ADA_TN), F32)], axis=0)
    o_ref[0] = out + b_ref[0]


def _ada(cond, ada_w, ada_b):
    n_out = 6 * D_MODEL
    cond_lanes = jnp.broadcast_to(cond[:, :, None], (N_COND, D_MODEL, LANES))
    return pl.pallas_call(
        _ada_kernel,
        grid=(DEPTH, n_out // ADA_TN),
        in_specs=[pl.BlockSpec((N_COND, D_MODEL, LANES), lambda l, j: (0, 0, 0)),
                  pl.BlockSpec((1, D_MODEL, ADA_TN), lambda l, j: (l, 0, j)),
                  pl.BlockSpec((1, 1, ADA_TN), lambda l, j: (l, 0, j))],
        out_specs=pl.BlockSpec((1, SUBLANES, ADA_TN), lambda l, j: (l, 0, j)),
        out_shape=jax.ShapeDtypeStruct((DEPTH, SUBLANES, n_out), F32),
        compiler_params=_cparams("arbitrary", "arbitrary"),
        name="ada",
    )(cond_lanes, ada_w, ada_b.reshape(DEPTH, 1, n_out))


IN_WIDTHS = (4 * W_GROUP, 4 * W_GROUP, 3 * W_GROUP, 3 * W_GROUP, LANES)
IN_PACKED = sum(IN_WIDTHS)


IN_TOTAL = 3 * W_GROUP + W_GROUP + 2 * N_GATE + 4 * W_GROUP + 3 * W_GROUP + 3 * W_GROUP
IN_GATE_SRC = 4 * W_GROUP
IN_SRC = (0, IN_GATE_SRC + 2 * N_GATE, IN_GATE_SRC + 2 * N_GATE + 4 * W_GROUP,
          IN_GATE_SRC + 2 * N_GATE + 7 * W_GROUP)
N_CTX_TILES = N_CTX // TOK_TILE


def _token_specs(split):
    lat0 = 0 if split else N_CTX_TILES
    return [pl.BlockSpec((TOK_TILE, D_MODEL), lambda i, *_: (jnp.minimum(i, N_CTX_TILES - 1), 0)),
            pl.BlockSpec((TOK_TILE, D_MODEL), lambda i, *_: (jnp.maximum(i, N_CTX_TILES) - N_CTX_TILES + lat0, 0))]


def _token_tile(i, ctx_ref, lat_ref):
    return jnp.where(i < N_CTX_TILES, ctx_ref[...], lat_ref[...])


def _inproj_kernel(xc_ref, xl_ref, mod_ref, nw_ref, w_ref, *refs):
    o_refs, w_s = refs[:-1], refs[-1]
    i = pl.program_id(0)

    @pl.when(i == 0)
    def _():
        piece = 256
        off = 0
        for src, width in zip(IN_SRC, IN_WIDTHS[:-1]):
            for c in range(0, width, piece):
                w_s[:, off + c:off + c + piece] = w_ref[0, src + c:src + c + piece, :].T.astype(BF16)
            off += width
        gate = w_ref[0, IN_GATE_SRC:IN_GATE_SRC + LANES, :].T
        w_s[:, off:] = jnp.where(_iota((D_MODEL, LANES), 1) < 2 * N_GATE, gate, 0.0).astype(BF16)

    x = _token_tile(i, xc_ref, xl_ref)
    y = x * lax.rsqrt(jnp.mean(x * x, axis=-1, keepdims=True) + EPS) * nw_ref[...]
    h = (y * (1.0 + mod_ref[0, 1:2, :]) + mod_ref[0, 0:1, :]).astype(BF16)
    off = 0
    for o_ref, width in zip(o_refs, IN_WIDTHS):
        o_ref[...] = _mm(h, w_s[:, off:off + width])
        off += width


def _inproj(x_ctx, x_lat, split, mod, norm_w, w_in, layer):
    return pl.pallas_call(
        _inproj_kernel,
        grid=(N_TOK // TOK_TILE,),
        in_specs=_token_specs(split)
                 + [pl.BlockSpec((1, SUBLANES, D_MODEL), lambda i: (_cond_of_tile(i), 0, 0)),
                    pl.BlockSpec((1, D_MODEL), lambda i: (0, 0)),
                    pl.BlockSpec((1, IN_TOTAL, D_MODEL), lambda i: (layer, 0, 0), pipeline_mode=pl.Buffered(1))],
        out_specs=[pl.BlockSpec((TOK_TILE, w), lambda i: (i, 0)) for w in IN_WIDTHS],
        out_shape=[jax.ShapeDtypeStruct((N_TOK, w), F32) for w in IN_WIDTHS],
        scratch_shapes=[pltpu.VMEM((D_MODEL, IN_PACKED), BF16)],
        compiler_params=_cparams("arbitrary"),
        name="inproj",
    )(x_ctx, x_lat, mod, norm_w.reshape(1, D_MODEL), jnp.swapaxes(w_in, 1, 2))


def _seq_call(kernel_fn, args, in_specs, out_specs, out_shape, into, more_into=(), **kwargs):
    donors = ([] if into is None else [(into, 0)]) + list(more_into)
    n_in = len(args)
    inner = kernel_fn
    kernel_fn = lambda *refs: inner(*refs[:n_in], *refs[n_in + len(donors):])
    aliases = {n_in + j: out_idx for j, (_, out_idx) in enumerate(donors)}
    args = list(args) + [a for a, _ in donors]
    in_specs = list(in_specs) + [pl.BlockSpec(memory_space=pl.ANY)] * len(donors)
    return pl.pallas_call(kernel_fn, in_specs=in_specs, out_specs=out_specs, out_shape=out_shape,
                          input_output_aliases=aliases, **kwargs)(*args)


def _shift_rows(p, seq_len):
    rows = p.shape[0]
    pos = _iota(p.shape, 0) & (seq_len - 1)
    prev = jnp.where(pos == 0, 0.0, pltpu.roll(p, 1, 0))
    nxt = jnp.where(pos == seq_len - 1, 0.0, pltpu.roll(p, rows - 1, 0))
    return prev, nxt


def _conv3(x, w_ref, seq_len):
    prev, nxt = _shift_rows(x, seq_len)
    return w_ref[0:1, :] * prev + w_ref[1:2, :] * x + w_ref[2:3, :] * nxt


def _chunk_scan(x, reverse):
    t = x.shape[0]
    pos = _iota(x.shape, 0) % CHUNK
    step = 1
    while step < CHUNK:
        if reverse:
            x = x + jnp.where(pos < CHUNK - step, pltpu.roll(x, t - step, 0), 0.0)
        else:
            x = x + jnp.where(pos >= step, pltpu.roll(x, step, 0), 0.0)
        step *= 2
    return x


GDN_GROUP_CHUNKS = 4
GDN_BASE_BLOCK = 4
GDN_CHAINS = GDN_GROUP_CHUNKS * N_GATE
GDN_PAIRS = GDN_GROUP_CHUNKS * N_HEADS


def _gdn_kernel(a_ref, gate_ref, convw_ref, alog_ref, dtb_ref, nw_ref, s0_ref, o_ref, sfin_ref,
                q_s, kv_s, kt_s, gc_s, eg_s, beta_s, gcrow_s, ekdrow_s, cdec_s, uo_s, wq_s, attn_s, kdt_s,
                st_s, wsqs_s, kk_s, d_s, m1_s, p_s, low_s, pb_s, rhs_s, *, t, n_sub):
    n_chunks = t // CHUNK
    a = a_ref[...]
    qkv = _silu(_conv3(a[:, :3 * W_GROUP], convw_ref, t // n_sub))
    q = qkv[:, :W_GROUP]
    k = qkv[:, W_GROUP:2 * W_GROUP]
    v = qkv[:, 2 * W_GROUP:]
    head_sum = _head_sum_matrix()
    q = q * lax.rsqrt(_head_sums(q * q, head_sum) + EPS) * (HEAD_DIM ** -0.5)
    k = k * lax.rsqrt(_head_sums(k * k, head_sum) + EPS)
    for h in range(N_HEADS):
        hs = slice(h * HEAD_DIM, (h + 1) * HEAD_DIM)
        q_s[h] = q[:, hs]
        kv_s[h] = jnp.concatenate([k[:, hs], v[:, hs]], axis=1)
    k_t = k.T
    for c in range(n_chunks):
        kt_s[c] = k_t[:, c * CHUNK:(c + 1) * CHUNK]

    gates = gate_ref[...]
    log_a = -jnp.exp(alog_ref[...]) * _softplus(gates + dtb_ref[...])
    beta_s[...] = _sigmoid(gates)

    ci = _iota((CHUNK, CHUNK), 0)
    cj = _iota((CHUNK, CHUNK), 1)
    eye = (ci == cj).astype(F32)
    blk_mask = (ci // GDN_BASE_BLOCK) == (cj // GDN_BASE_BLOCK)
    low_half = _iota((CHUNK, 2 * HEAD_DIM), 1) < HEAD_DIM

    prefix = _chunk_scan(log_a, reverse=False)
    suffix = _chunk_scan(log_a, reverse=True)
    gc = jnp.where(_iota((t, LANES), 1) < N_HEADS, prefix, suffix)
    gt = prefix + suffix - log_a
    gc_s[...] = gc
    eg_s[...] = jnp.exp(gc)
    gc_t = gc.T
    ekd_t = jnp.exp(gt - gc).T
    cdec_t = jnp.exp(gt).T
    for c in range(n_chunks):
        lanes = slice(c * CHUNK, (c + 1) * CHUNK)
        gcrow_s[c] = gc_t[:N_GATE, lanes]
        ekdrow_s[c] = ekd_t[:N_GATE, lanes]
        cdec_s[c] = jnp.concatenate([cdec_t[:N_GATE, lanes]] * 2, axis=1)

    def solve_group(grp, carry):
        row0 = grp * (GDN_GROUP_CHUNKS * CHUNK)
        chains = [(cl, a_idx) for cl in range(GDN_GROUP_CHUNKS) for a_idx in range(N_GATE)]

        def rows_of(cl):
            return pl.ds(pl.multiple_of(row0 + cl * CHUNK, CHUNK), CHUNK)

        for cl in range(GDN_GROUP_CHUNKS):
            for h in range(N_HEADS):
                rows = rows_of(cl)
                kq = jnp.concatenate([kv_s[h, rows, :HEAD_DIM], q_s[h, rows, :]], axis=0)
                k_t_h = kt_s[grp * GDN_GROUP_CHUNKS + cl, h * HEAD_DIM:(h + 1) * HEAD_DIM, :]
                kk_s[cl * N_HEADS + h] = _bmm(kq, k_t_h)
        for b, (cl, a_idx) in enumerate(chains):
            backward = a_idx >= N_HEADS
            h = a_idx % N_HEADS
            rows, c = rows_of(cl), grp * GDN_GROUP_CHUNKS + cl
            incl = (cj >= ci) if backward else (cj <= ci)
            strict = (cj > ci) if backward else (cj < ci)
            bt = beta_s[rows, N_GATE + a_idx:N_GATE + a_idx + 1]
            decay = jnp.exp(jnp.where(incl, gc_s[rows, a_idx:a_idx + 1] - gcrow_s[c, a_idx:a_idx + 1, :], NEG_BIG))
            low = jnp.where(strict, kk_s[cl * N_HEADS + h, :CHUNK, :] * bt * decay, 0.0)
            attn_s[a_idx, rows, :] = (kk_s[cl * N_HEADS + h, CHUNK:, :] * decay).astype(BF16)
            d_s[b] = jnp.where(blk_mask, low, 0.0)
            low_s[b] = low.astype(BF16)
            rhs_s[b] = (kv_s[h, rows, :] * bt
                        * jnp.where(low_half, eg_s[rows, a_idx:a_idx + 1], 1.0)).astype(BF16)
        for b in range(GDN_CHAINS):
            m1_s[b] = _bmm(d_s[b], d_s[b])
        for b in range(GDN_CHAINS):
            d, d2 = d_s[b], m1_s[b]
            p_s[b] = eye - d + d2 - _bmm(d, d2)
        size = GDN_BASE_BLOCK
        while size < CHUNK:
            pair = ((ci // size) != (cj // size)) & ((ci // (2 * size)) == (cj // (2 * size)))
            for b in range(GDN_CHAINS):
                coupling = jnp.where(pair, low_s[b], jnp.zeros((), BF16))
                pb_s[b] = _mm(p_s[b].astype(BF16), coupling).astype(BF16)
            for b in range(GDN_CHAINS):
                p = p_s[b]
                p_s[b] = p - _mm(pb_s[b], p.astype(BF16))
            size *= 2
        for b, (cl, a_idx) in enumerate(chains):
            h = a_idx % N_HEADS
            rows, c = rows_of(cl), grp * GDN_GROUP_CHUNKS + cl
            wu = _mm(p_s[b].astype(BF16), rhs_s[b])
            uo_s[a_idx, rows, :] = wu
            wq_s[a_idx, c, :CHUNK, :] = wu[:, :HEAD_DIM].astype(BF16)
            wq_s[a_idx, c, CHUNK:, :] = (q_s[h, rows, :] * eg_s[rows, a_idx:a_idx + 1]).astype(BF16)
            k_t_h = kt_s[c, h * HEAD_DIM:(h + 1) * HEAD_DIM, :]
            kdt_s[a_idx, c] = (k_t_h * ekdrow_s[c, a_idx:a_idx + 1, :]).astype(BF16)
        return carry

    lax.fori_loop(0, n_chunks // GDN_GROUP_CHUNKS, solve_group, 0)

    seq_chunks = n_chunks // n_sub
    chains = [(sub, a_idx) for sub in range(n_sub) for a_idx in range(N_GATE)]
    for j, (sub, a_idx) in enumerate(chains):
        st_s[j] = jnp.concatenate([jnp.zeros((HEAD_DIM, HEAD_DIM), F32), s0_ref[sub, a_idx]], axis=1)

    def scan_chunk(c, carry):
        def chunk_of(sub, a_idx):
            return sub * seq_chunks + ((seq_chunks - 1 - c) if a_idx >= N_HEADS else c)

        for j, (sub, a_idx) in enumerate(chains):
            wsqs_s[j] = _mm(wq_s[a_idx, chunk_of(sub, a_idx)], st_s[j].astype(BF16))
        for j, (sub, a_idx) in enumerate(chains):
            cc = chunk_of(sub, a_idx)
            rows = pl.ds(pl.multiple_of(cc * CHUNK, CHUNK), CHUNK)
            v_new = (uo_s[a_idx, rows, :] - wsqs_s[j, :CHUNK, :]).astype(BF16)
            uo_s[a_idx, rows, :] = wsqs_s[j, CHUNK:, :] + _mm(attn_s[a_idx, rows, :], v_new)
            st_s[j] = st_s[j] * cdec_s[cc, a_idx:a_idx + 1, :] + _mm(kdt_s[a_idx, cc], v_new)
        return carry

    lax.fori_loop(0, seq_chunks, scan_chunk, 0)
    for j, (sub, a_idx) in enumerate(chains):
        sfin_ref[sub, a_idx] = st_s[j, :, HEAD_DIM:]

    o = jnp.concatenate([(uo_s[h] + uo_s[N_HEADS + h])[:, HEAD_DIM:] for h in range(N_HEADS)], axis=1)
    ms = _head_sums(o * o, head_sum) * (1.0 / HEAD_DIM)
    o_ref[...] = o * lax.rsqrt(ms + EPS) * nw_ref[...] * _silu(a[:, 3 * W_GROUP:])


def _gdn(a_all, gate_all, conv_w, a_log, dt_bias, norm_w, s0, seq_len, n_seq, first_block, n_sub=1, into=None):
    small = lambda: pl.BlockSpec((1, LANES), lambda i: (0, 0))
    t = n_sub * seq_len
    n_chunks = t // CHUNK
    wide = 2 * HEAD_DIM
    scratch = [pltpu.VMEM((N_HEADS, t, HEAD_DIM), F32),
               pltpu.VMEM((N_HEADS, t, wide), F32),
               pltpu.VMEM((n_chunks, W_GROUP, CHUNK), F32),
               pltpu.VMEM((t, LANES), F32), pltpu.VMEM((t, LANES), F32), pltpu.VMEM((t, LANES), F32),
               pltpu.VMEM((n_chunks, N_GATE, CHUNK), F32), pltpu.VMEM((n_chunks, N_GATE, CHUNK), F32),
               pltpu.VMEM((n_chunks, N_GATE, wide), F32),
               pltpu.VMEM((N_GATE, t, wide), F32),
               pltpu.VMEM((N_GATE, n_chunks, 2 * CHUNK, HEAD_DIM), BF16),
               pltpu.VMEM((N_GATE, t, CHUNK), BF16),
               pltpu.VMEM((N_GATE, n_chunks, HEAD_DIM, CHUNK), BF16),
               pltpu.VMEM((n_sub * N_GATE, HEAD_DIM, wide), F32),
               pltpu.VMEM((n_sub * N_GATE, 2 * CHUNK, wide), F32),
               pltpu.VMEM((GDN_PAIRS, 2 * CHUNK, CHUNK), F32)]
    scratch += [pltpu.VMEM((GDN_CHAINS, CHUNK, CHUNK), F32)] * 3
    scratch += [pltpu.VMEM((GDN_CHAINS, CHUNK, CHUNK), BF16)] * 2
    scratch += [pltpu.VMEM((GDN_CHAINS, CHUNK, wide), BF16)]
    return _seq_call(
        functools.partial(_gdn_kernel, t=t, n_sub=n_sub), (a_all, gate_all, conv_w, a_log, dt_bias, norm_w, s0),
        in_specs=[pl.BlockSpec((t, 4 * W_GROUP), lambda i: (i + first_block, 0)),
                  pl.BlockSpec((t, LANES), lambda i: (i + first_block, 0)),
                  pl.BlockSpec((SUBLANES, 3 * W_GROUP), lambda i: (0, 0)),
                  small(), small(),
                  pl.BlockSpec((1, W_GROUP), lambda i: (0, 0)),
                  pl.BlockSpec((n_sub, N_GATE, HEAD_DIM, HEAD_DIM), lambda i: (i, 0, 0, 0))],
        out_specs=[pl.BlockSpec((t, W_GROUP), lambda i: (i + first_block, 0)),
                   pl.BlockSpec((n_sub, N_GATE, HEAD_DIM, HEAD_DIM), lambda i: (i, 0, 0, 0))],
        out_shape=[jax.ShapeDtypeStruct((N_TOK, W_GROUP), F32),
                   jax.ShapeDtypeStruct((n_seq, N_GATE, HEAD_DIM, HEAD_DIM), F32)],
        scratch_shapes=scratch,
        into=into, grid=(n_seq // n_sub,), compiler_params=_cparams("arbitrary"), name="gdn")


def _swap16(x):
    width = x.shape[-1]
    first = (_iota(x.shape, 1) // 16) % 2 == 0
    return jnp.where(first, pltpu.roll(x, width - 16, 1), pltpu.roll(x, 16, 1))


def _block_diag_heads(s0_ref, first):
    zero = jnp.zeros((HEAD_DIM, HEAD_DIM), F32)
    return jnp.concatenate(
        [jnp.concatenate([s0_ref[0, first + h] if j == h else zero for j in range(N_HEADS)], axis=1)
         for h in range(N_HEADS)], axis=0)


def _ret_kernel(r_ref, lg_ref, s0_ref, cos_ref, sin_ref, o_ref, sfin_ref, *, t, latent):
    r = r_ref[...]
    q = r[:, :W_GROUP]
    k = r[:, W_GROUP:2 * W_GROUP]
    v = r[:, 2 * W_GROUP:3 * W_GROUP]
    if latent:
        q = q * cos_ref[...] + _swap16(q) * sin_ref[...]
        k = k * cos_ref[...] + _swap16(k) * sin_ref[...]
    k = k * (HEAD_DIM ** -0.5)
    lg = -_softplus(-lg_ref[...])
    lgf, lgb = lg[0:1, :], lg[1:2, :]
    head = _iota((1, W_GROUP), 1) // HEAD_DIM
    head_sum = _head_sum_matrix()
    pos = _iota((t, 1), 0).astype(F32)
    q_b = q.astype(BF16)
    kt_b = k.T.astype(BF16)
    v_heads = [jnp.where(head == h, v, 0.0).astype(BF16) for h in range(N_HEADS)]
    if latent:
        s0f = _block_diag_heads(s0_ref, 0)
        s0b = _block_diag_heads(s0_ref, N_HEADS)
    for qt in range(t // Q_TILE):
        rows = slice(qt * Q_TILE, (qt + 1) * Q_TILE)
        diff = (_iota((Q_TILE, t), 0) + qt * Q_TILE - _iota((Q_TILE, t), 1)).astype(F32)
        both = jnp.where(diff == 0, 2.0, 1.0)
        o = jnp.zeros((Q_TILE, W_GROUP), F32)
        for h in range(N_HEADS):
            lgf_h = lgf[:, h * HEAD_DIM:h * HEAD_DIM + 1]
            lgb_h = lgb[:, h * HEAD_DIM:h * HEAD_DIM + 1]
            dmat = jnp.exp(diff * jnp.where(diff >= 0, lgf_h, -lgb_h)) * both
            s = _mm(jnp.where(head == h, q_b[rows], 0.0), kt_b) * dmat
            o = o + _mm(s.astype(BF16), v_heads[h])
        if latent:
            p = pos[rows]
            o = o + jnp.exp((p + 1.0) * lgf) * _bmm(q_b[rows], s0f) + jnp.exp((t - p) * lgb) * _bmm(q_b[rows], s0b)
        oc = o - _head_sums(o, head_sum) * (1.0 / HEAD_DIM)
        on = oc * lax.rsqrt(_head_sums(oc * oc, head_sum) * (1.0 / HEAD_DIM) + EPS)
        o_ref[rows, :] = on * _silu(r[rows, 3 * W_GROUP:])
    v_b = v.astype(BF16)
    sf = _mm_tn((k * jnp.exp((t - 1.0 - pos) * lgf)).astype(BF16), v_b)
    sb = _mm_tn((k * jnp.exp(pos * lgb)).astype(BF16), v_b)
    for h in range(N_HEADS):
        hs = slice(h * HEAD_DIM, (h + 1) * HEAD_DIM)
        sf_h, sb_h = sf[hs, hs], sb[hs, hs]
        if latent:
            sf_h = sf_h + jnp.exp(t * lgf[:, h * HEAD_DIM:h * HEAD_DIM + 1]) * s0_ref[0, h]
            sb_h = sb_h + jnp.exp(t * lgb[:, h * HEAD_DIM:h * HEAD_DIM + 1]) * s0_ref[0, N_HEADS + h]
        sfin_ref[0, h] = sf_h
        sfin_ref[0, N_HEADS + h] = sb_h


def _ret(r_all, logit, s0, cos, sin, t, n_seq, first_block, latent, into=None):
    return _seq_call(
        functools.partial(_ret_kernel, t=t, latent=latent), (r_all, logit, s0, cos, sin),
        in_specs=[pl.BlockSpec((t, 4 * W_GROUP), lambda i: (i + first_block, 0)),
                  pl.BlockSpec((2, W_GROUP), lambda i: (0, 0)),
                  pl.BlockSpec((1, N_GATE, HEAD_DIM, HEAD_DIM), lambda i: (i, 0, 0, 0)),
                  pl.BlockSpec((t, W_GROUP), lambda i: (0, 0)),
                  pl.BlockSpec((t, W_GROUP), lambda i: (0, 0))],
        out_specs=[pl.BlockSpec((t, W_GROUP), lambda i: (i + first_block, 0)),
                   pl.BlockSpec((1, N_GATE, HEAD_DIM, HEAD_DIM), lambda i: (i, 0, 0, 0))],
        out_shape=[jax.ShapeDtypeStruct((N_TOK, W_GROUP), F32),
                   jax.ShapeDtypeStruct((n_seq, N_GATE, HEAD_DIM, HEAD_DIM), F32)],
        into=into, grid=(n_seq,), compiler_params=_cparams("arbitrary"), name="ret")


def _rope_tables(t):
    pos = np.arange(t)
    row = (pos // GRID_W).astype(np.float32)
    col = (pos % GRID_W).astype(np.float32)
    nf = HEAD_DIM // 4
    inv_freq = jnp.power(ROPE_BASE, -jnp.arange(nf, dtype=F32) / nf)
    ang_r = jnp.asarray(row)[:, None] * inv_freq[None, :]
    ang_c = jnp.asarray(col)[:, None] * inv_freq[None, :]
    cos = jnp.concatenate([jnp.cos(ang_r)] * 2 + [jnp.cos(ang_c)] * 2, axis=1)
    sin = jnp.concatenate([-jnp.sin(ang_r), jnp.sin(ang_r), -jnp.sin(ang_c), jnp.sin(ang_c)], axis=1)
    return jnp.tile(cos, (1, N_HEADS)), jnp.tile(sin, (1, N_HEADS))


def _head_rms(x, w, head_sum):
    return x * lax.rsqrt(_head_sums(x * x, head_sum) * (1.0 / HEAD_DIM) + EPS) * w


def _ctx_attn_kernel(n_ref, qw_ref, kw_ref, o_ref, k_out, v_out, *, n_fill):
    n = n_ref[...]
    head_sum = _head_sum_matrix()
    head = _iota((1, W_GROUP), 1) // HEAD_DIM
    q = (_head_rms(n[:, :W_GROUP], qw_ref[...], head_sum) * (HEAD_DIM ** -0.5)).astype(BF16)
    k = _head_rms(n[:, W_GROUP:2 * W_GROUP], kw_ref[...], head_sum)
    v = n[:, 2 * W_GROUP:]
    k_t = k.T
    k_out[0, 0] = k_t
    v_out[0, 0] = v.T
    for later in range(1, 1 + n_fill):
        k_out[0, later] = jnp.zeros((W_GROUP, SEQ), F32)
        v_out[0, later] = jnp.zeros((W_GROUP, SEQ), F32)
    kt_b = k_t.astype(BF16)
    o = jnp.zeros((SEQ, W_GROUP), F32)
    for h in range(N_HEADS):
        s = _mm(jnp.where(head == h, q, 0.0), kt_b)
        p = jnp.exp(s - jnp.max(s, axis=-1, keepdims=True))
        o = o + _mm(p.astype(BF16), jnp.where(head == h, v, 0.0).astype(BF16)) / jnp.sum(p, axis=-1, keepdims=True)
    o_ref[...] = o


def _ctx_attn(n_all, qw, kw, into, layer, caches):
    slots = DEPTH if caches is None else 1
    cache_spec = pl.BlockSpec((1, slots, W_GROUP, SEQ), lambda i: (i, layer if caches is not None else 0, 0, 0))
    cache_shape = jax.ShapeDtypeStruct((BATCH, DEPTH, W_GROUP, SEQ), F32)
    return _seq_call(
        functools.partial(_ctx_attn_kernel, n_fill=slots - 1), (n_all, qw, kw),
        in_specs=[pl.BlockSpec((SEQ, 3 * W_GROUP), lambda i: (i, 0)),
                  pl.BlockSpec((1, W_GROUP), lambda i: (0, 0)),
                  pl.BlockSpec((1, W_GROUP), lambda i: (0, 0))],
        out_specs=[pl.BlockSpec((SEQ, W_GROUP), lambda i: (i, 0)), cache_spec, cache_spec],
        out_shape=[jax.ShapeDtypeStruct((N_TOK, W_GROUP), F32), cache_shape, cache_shape],
        into=into, more_into=() if caches is None else ((caches[0], 1), (caches[1], 2)),
        grid=(BATCH,), compiler_params=_cparams("arbitrary"), name="ctx_attn")


def _nat_kernel(n_ref, ck_ref, cv_ref, bias_ref, qw_ref, kw_ref, o_ref, q_s, kt_s, ckt_s):
    h = pl.program_id(1)

    @pl.when(h == 0)
    def _():
        head_sum = _head_sum_matrix()
        n = n_ref[...]
        q_s[...] = (_head_rms(n[:, :W_GROUP], qw_ref[...], head_sum) * (HEAD_DIM ** -0.5)).astype(BF16)
        kt_s[...] = _head_rms(n[:, W_GROUP:2 * W_GROUP], kw_ref[...], head_sum).T.astype(BF16)
        ckt_s[...] = ck_ref[0].T.astype(BF16)
        o_ref[...] = jnp.zeros_like(o_ref)

    mine = _iota((1, W_GROUP), 1) // HEAD_DIM == h
    v = jnp.where(mine, n_ref[:, 2 * W_GROUP:], 0.0).astype(BF16)
    cv = jnp.where(mine, cv_ref[0], 0.0).astype(BF16)
    for qt in range(DEC_SEQ // Q_TILE):
        rows = slice(qt * Q_TILE, (qt + 1) * Q_TILE)
        q = jnp.where(mine, q_s[rows, :], 0.0)
        grid_rows = range(qt * Q_TILE // GRID_W, (qt + 1) * Q_TILE // GRID_W)
        s_loc = _mm(q, kt_s[...]) + jnp.concatenate([_nat_bias_strip(bias_ref, row) for row in grid_rows], axis=0)
        s_ctx = _mm(q, ckt_s[...])
        m = jnp.maximum(jnp.max(s_loc, axis=-1, keepdims=True), jnp.max(s_ctx, axis=-1, keepdims=True))
        p_loc = jnp.exp(s_loc - m)
        p_ctx = jnp.exp(s_ctx - m)
        den = jnp.sum(p_loc, axis=-1, keepdims=True) + jnp.sum(p_ctx, axis=-1, keepdims=True)
        o_ref[rows, :] += (_mm(p_loc.astype(BF16), v) + _mm(p_ctx.astype(BF16), cv)) / den


def _nat(n_all, ck, cv, bias, qw, kw, into):
    first_block = N_CTX // DEC_SEQ
    return _seq_call(
        _nat_kernel, (n_all, ck, cv, bias, qw, kw),
        in_specs=[pl.BlockSpec((DEC_SEQ, 3 * W_GROUP), lambda b, h: (b + first_block, 0)),
                  pl.BlockSpec((1, PAST_LEN, W_GROUP), lambda b, h: (b, 0, 0)),
                  pl.BlockSpec((1, PAST_LEN, W_GROUP), lambda b, h: (b, 0, 0)),
                  pl.BlockSpec((1, 3 * N_ROW_OFF - 1, GRID_W, 2 * GRID_W), lambda b, h: (h, 0, 0, 0)),
                  pl.BlockSpec((1, W_GROUP), lambda b, h: (0, 0)),
                  pl.BlockSpec((1, W_GROUP), lambda b, h: (0, 0))],
        out_specs=pl.BlockSpec((DEC_SEQ, W_GROUP), lambda b, h: (b + first_block, 0)),
        out_shape=jax.ShapeDtypeStruct((N_TOK, W_GROUP), F32),
        scratch_shapes=[pltpu.VMEM((DEC_SEQ, W_GROUP), BF16), pltpu.VMEM((W_GROUP, DEC_SEQ), BF16),
                        pltpu.VMEM((W_GROUP, PAST_LEN), BF16)],
        into=into, grid=(DEC_BATCH, N_HEADS), compiler_params=_cparams("arbitrary", "arbitrary"), name="nat")


N_ROW_OFF = 2 * WIN_ROWS - 1
NAT_ROWS = DEC_SEQ // GRID_W
NAT_KH = min(WIN_ROWS, NAT_ROWS)
NAT_PAIR, NAT_LOW, NAT_HIGH = 0, N_ROW_OFF - 1, 2 * N_ROW_OFF - 1


def _nat_tables(rpb):
    c = np.arange(GRID_W)
    c0 = np.clip(c - WIN_COLS // 2, 0, GRID_W - WIN_COLS)
    col_in = (c[None, :] >= c0[:, None]) & (c[None, :] < c0[:, None] + WIN_COLS)
    col_idx = np.clip(c[None, :] - c[:, None], -(WIN_COLS - 1), WIN_COLS - 1) + WIN_COLS - 1
    col_hot = (col_idx[..., None] == np.arange(2 * WIN_COLS - 1)).astype(np.float32)
    tz = jnp.where(col_in, jnp.einsum('hab,qkb->haqk', rpb.astype(F32), col_hot, precision=HI), NEG_BIG)
    neg = jnp.full_like(tz, NEG_BIG)
    return jnp.concatenate([jnp.concatenate([tz[:, :-1], tz[:, 1:]], axis=-1),
                            jnp.concatenate([tz, neg], axis=-1), jnp.concatenate([neg, tz], axis=-1)], axis=1)


def _nat_bias_strip(tab_ref, row):
    first_key = min(max(row - NAT_KH // 2, 0), NAT_ROWS - NAT_KH)
    off = first_key - row + WIN_ROWS - 1
    tiles = {}
    done, key = 0, first_key
    if key % 2 == 1:
        tiles[key // 2] = tab_ref[0, NAT_HIGH + off]
        done, key = 1, key + 1
    while done + 1 < NAT_KH:
        tiles[key // 2] = tab_ref[0, NAT_PAIR + off + done]
        done, key = done + 2, key + 2
    if done < NAT_KH:
        tiles[key // 2] = tab_ref[0, NAT_LOW + off + done]
    outside = jnp.full((GRID_W, 2 * GRID_W), NEG_BIG, F32)
    return jnp.concatenate([tiles.get(i, outside) for i in range(NAT_ROWS // 2)], axis=1)


GSEL_LANE = N_EXPERTS


def _pack_router(we, be, wg, bg):
    pad = LANES - N_EXPERTS - N_GROUPS
    w = jnp.concatenate([we, wg, jnp.zeros((D_MODEL, pad), F32)], axis=1)
    b = jnp.concatenate([be, bg, jnp.zeros((pad,), F32)]).reshape(1, LANES)
    hi = w.astype(BF16)
    lo = (w - hi.astype(F32)).astype(BF16)
    return jnp.concatenate([hi, lo], axis=1), b


def _lane_min_where(mask, lane):
    return jnp.min(jnp.where(mask, lane, LANES), axis=-1, keepdims=True)


def _sconv_tile(i, s_ref, before_ref, after_ref, w_ref):
    s = s_ref[...]
    p = s[:, W_GROUP:2 * W_GROUP] * s[:, 2 * W_GROUP:]
    p_before = before_ref[SUBLANES - 1:, W_GROUP:2 * W_GROUP] * before_ref[SUBLANES - 1:, 2 * W_GROUP:]
    p_after = after_ref[:1, W_GROUP:2 * W_GROUP] * after_ref[:1, 2 * W_GROUP:]
    row = _iota((TOK_TILE, 1), 0)
    seq_len = jnp.where(i < N_CTX_TILES, SEQ, DEC_SEQ)
    pos = (i * TOK_TILE + row) & (seq_len - 1)
    prev = jnp.where(row == 0, p_before, pltpu.roll(p, 1, 0))
    nxt = jnp.where(row == TOK_TILE - 1, p_after, pltpu.roll(p, TOK_TILE - 1, 0))
    prev = jnp.where(pos == 0, 0.0, prev)
    nxt = jnp.where(pos == seq_len - 1, 0.0, nxt)
    return s[:, :W_GROUP] * (w_ref[0:1, :] * prev + w_ref[1:2, :] * p + w_ref[2:3, :] * nxt)


def _outproj_kernel(xc_ref, xl_ref, m0, m1, m2, s_ref, before_ref, after_ref, cw_ref, mod_ref, nw_ref, w_ref,
                    rw_ref, rb_ref, x_out, hf_out, route_out, w_s):
    tile_i = pl.program_id(0)

    @pl.when(tile_i == 0)
    def _():
        w_s[...] = w_ref[0].astype(BF16)

    mixed = [m0[...], m1[...], m2[...], _sconv_tile(tile_i, s_ref, before_ref, after_ref, cw_ref)]
    acc = None
    for i, m in enumerate(mixed):
        part = _mm(m.astype(BF16), w_s[i * W_GROUP:(i + 1) * W_GROUP, :])
        acc = part if acc is None else acc + part
    x = _token_tile(tile_i, xc_ref, xl_ref) + mod_ref[0, 2:3, :] * acc
    x_out[...] = x
    y = x * lax.rsqrt(jnp.mean(x * x, axis=-1, keepdims=True) + EPS) * nw_ref[...]
    hf = y * (1.0 + mod_ref[0, 4:5, :]) + mod_ref[0, 3:4, :]
    hf_hi = hf.astype(BF16)
    hf_out[...] = hf_hi

    hf_lo = (hf - hf_hi.astype(F32)).astype(BF16)
    both = _mm(hf_hi, rw_ref[...])
    logits = both[:, :LANES] + both[:, LANES:] + _mm(hf_lo, rw_ref[:, :LANES]) + rb_ref[...]
    lane = _iota(logits.shape, 1)
    is_g = (lane >= N_EXPERTS) & (lane < N_EXPERTS + N_GROUPS)
    gl = jnp.where(is_g, logits, NEG_BIG)
    ge = jnp.exp(gl - jnp.max(gl, axis=-1, keepdims=True))
    gp = jnp.where(is_g, ge / jnp.sum(ge, axis=-1, keepdims=True), -1.0)
    gw = jnp.max(gp, axis=-1, keepdims=True)
    gsel = _lane_min_where(gp == gw, lane) - N_EXPERTS
    in_grp = (lane // EXPERTS_PER_GROUP == gsel) & (lane < N_EXPERTS)
    el = jnp.where(in_grp, logits, NEG_BIG)
    ee = jnp.exp(el - jnp.max(el, axis=-1, keepdims=True))
    ep = jnp.where(in_grp, ee / jnp.sum(ee, axis=-1, keepdims=True), -1.0)
    t1 = jnp.max(ep, axis=-1, keepdims=True)
    i1 = _lane_min_where(ep == t1, lane)
    ep2 = jnp.where(lane == i1, -1.0, ep)
    t2 = jnp.max(ep2, axis=-1, keepdims=True)
    i2 = _lane_min_where(ep2 == t2, lane)
    tsum = t1 + t2
    combine = jnp.where(lane == i1, gw * (t1 / tsum), 0.0) + jnp.where(lane == i2, gw * (t2 / tsum), 0.0)
    route_out[...] = jnp.where(lane == GSEL_LANE, gsel.astype(F32), combine)


def _outproj(x_ctx, x_lat, split, mixed, a_sc, sc_w, mod, norm_w, w_out, layer, rw, rb):
    tile = lambda w: pl.BlockSpec((TOK_TILE, w), lambda i: (i, 0))
    whole = lambda a: pl.BlockSpec(a.shape, lambda i: (0,) * a.ndim)
    per_tile = TOK_TILE // SUBLANES
    halo = lambda index: pl.BlockSpec((SUBLANES, 3 * W_GROUP), lambda i: (index(i), 0))
    return pl.pallas_call(
        _outproj_kernel,
        grid=(N_TOK // TOK_TILE,),
        in_specs=_token_specs(split) + [tile(W_GROUP)] * 3
                 + [tile(3 * W_GROUP), halo(lambda i: jnp.maximum(i * per_tile - 1, 0)),
                    halo(lambda i: jnp.minimum((i + 1) * per_tile, N_TOK // SUBLANES - 1)), whole(sc_w)]
                 + [pl.BlockSpec((1, SUBLANES, D_MODEL), lambda i: (_cond_of_tile(i), 0, 0)),
                    pl.BlockSpec((1, D_MODEL), lambda i: (0, 0)),
                    pl.BlockSpec((1, D_MODEL, D_MODEL), lambda i: (layer, 0, 0), pipeline_mode=pl.Buffered(1)),
                    whole(rw), whole(rb)],
        out_specs=[tile(D_MODEL), tile(D_MODEL), tile(LANES)],
        out_shape=[jax.ShapeDtypeStruct((N_TOK, D_MODEL), F32), jax.ShapeDtypeStruct((N_TOK, D_MODEL), BF16),
                   jax.ShapeDtypeStruct((N_TOK, LANES), F32)],
        scratch_shapes=[pltpu.VMEM((D_MODEL, D_MODEL), BF16)],
        compiler_params=_cparams("arbitrary"),
        name="outproj",
    )(x_ctx, x_lat, *mixed, a_sc, a_sc, a_sc, sc_w, mod, norm_w.reshape(1, D_MODEL), w_out, rw, rb)


SEG_BLK = 16
LOCAL_ROWS = TOK_TILE + N_GROUPS * SEG_BLK
N_TOK_TILES = N_TOK // TOK_TILE
MOE_ROWS = -(-(N_TOK + N_TOK_TILES * N_GROUPS * (SEG_BLK - 1) + N_GROUPS * (MOE_TILE - 1)) // MOE_TILE) * MOE_TILE


def _moe_tables(gsel):
    groups = jnp.arange(N_GROUPS, dtype=jnp.int32)
    onehot = (gsel.reshape(N_TOK_TILES, TOK_TILE, 1) == groups).astype(jnp.int32)
    earlier = jnp.asarray(np.tril(np.ones((TOK_TILE, TOK_TILE), np.float32), -1))
    rank = jnp.einsum('ts,nsg->ntg', earlier, onehot.astype(F32)).astype(jnp.int32)
    nblk = (jnp.sum(onehot, axis=1) + SEG_BLK - 1) // SEG_BLK
    loc_blk = jnp.cumsum(nblk, axis=1) - nblk
    blocks_per_tile = MOE_TILE // SEG_BLK
    grp_tiles = (jnp.sum(nblk, axis=0) + blocks_per_tile - 1) // blocks_per_tile
    grp_tile_start = jnp.cumsum(grp_tiles) - grp_tiles
    dst_blk = grp_tile_start[None, :] * blocks_per_tile + jnp.cumsum(nblk, axis=0) - nblk
    local_pos = jnp.sum(onehot * (loc_blk[:, None, :] * SEG_BLK + rank), axis=2)
    tile_idx = jnp.arange(MOE_ROWS // MOE_TILE, dtype=jnp.int32)
    tile_group = jnp.clip(jnp.sum(tile_idx[:, None] >= grp_tile_start[None, :], axis=1) - 1, 0, N_GROUPS - 1)
    in_group = tile_idx - grp_tile_start[tile_group]
    tile_rows = jnp.clip(jnp.sum(nblk, axis=0)[tile_group] * SEG_BLK - in_group * MOE_TILE, 0, MOE_TILE)
    flat = lambda a: a.reshape(-1).astype(jnp.int32)
    return local_pos.astype(jnp.int32), flat(nblk), flat(loc_blk), flat(dst_blk), flat(tile_group), flat(tile_rows)


def _segment_copies(t, nblk, loc_blk, dst_blk, make_copies, action):
    for g in range(N_GROUPS):
        k = t * N_GROUPS + g

        @pl.loop(0, nblk[k])
        def _(b):
            local = pl.multiple_of((loc_blk[k] + b) * SEG_BLK, SEG_BLK)
            sorted_row = pl.multiple_of((dst_blk[k] + b) * SEG_BLK, SEG_BLK)
            for cp in make_copies(local, sorted_row):
                action(cp)


def _dispatch_kernel(nblk, loc_blk, dst_blk, hf_ref, rt_ref, lp_ref, xs_in, rs_in, xs_hbm, rs_hbm, xbuf, rbuf, sem):
    t = pl.program_id(0)
    slot = t % 2
    onehot = _iota((LOCAL_ROWS, TOK_TILE), 0) == lp_ref[0]
    xbuf[slot] = _mm(onehot.astype(BF16), hf_ref[...]).astype(BF16)
    rbuf[slot] = _mm(onehot.astype(F32), rt_ref[...], HI)

    def copies_of(s):
        def copies(local, sorted_row):
            return (pltpu.make_async_copy(xbuf.at[s, pl.ds(local, SEG_BLK)], xs_hbm.at[pl.ds(sorted_row, SEG_BLK)], sem.at[s]),
                    pltpu.make_async_copy(rbuf.at[s, pl.ds(local, SEG_BLK)], rs_hbm.at[pl.ds(sorted_row, SEG_BLK)], sem.at[s]))
        return copies

    @pl.when(t > 0)
    def _():
        _segment_copies(t - 1, nblk, loc_blk, dst_blk, copies_of(1 - slot), lambda cp: cp.wait())

    _segment_copies(t, nblk, loc_blk, dst_blk, copies_of(slot), lambda cp: cp.start())

    @pl.when(t == N_TOK_TILES - 1)
    def _():
        _segment_copies(t, nblk, loc_blk, dst_blk, copies_of(slot), lambda cp: cp.wait())


def _dispatch(hf, route, local_pos, nblk, loc_blk, dst_blk):
    grid_spec = pltpu.PrefetchScalarGridSpec(
        num_scalar_prefetch=3,
        grid=(N_TOK_TILES,),
        in_specs=[pl.BlockSpec((TOK_TILE, D_MODEL), lambda t, *_: (t, 0)),
                  pl.BlockSpec((TOK_TILE, LANES), lambda t, *_: (t, 0)),
                  pl.BlockSpec((1, 1, TOK_TILE), lambda t, *_: (t, 0, 0)),
                  pl.BlockSpec(memory_space=pl.ANY), pl.BlockSpec(memory_space=pl.ANY)],
        out_specs=[pl.BlockSpec(memory_space=pl.ANY), pl.BlockSpec(memory_space=pl.ANY)],
        scratch_shapes=[pltpu.VMEM((2, LOCAL_ROWS, D_MODEL), BF16), pltpu.VMEM((2, LOCAL_ROWS, LANES), F32),
                        pltpu.SemaphoreType.DMA((2,))],
    )
    return pl.pallas_call(
        _dispatch_kernel,
        grid_spec=grid_spec,
        out_shape=[jax.ShapeDtypeStruct((MOE_ROWS, D_MODEL), BF16), jax.ShapeDtypeStruct((MOE_ROWS, LANES), F32)],
        input_output_aliases={6: 0, 7: 1},
        compiler_params=_cparams("arbitrary"),
        name="dispatch",
    )(nblk, loc_blk, dst_blk, hf, route, local_pos.reshape(N_TOK_TILES, 1, TOK_TILE),
      jnp.zeros((MOE_ROWS, D_MODEL), BF16), jnp.zeros((MOE_ROWS, LANES), F32))


def _moe_kernel(tile_group, tile_rows, x_ref, r_ref, wg_hbm, wu_hbm, wd_hbm, y_ref,
                wg_b, wu_b, wd_b, stage_g, stage_u, stage_d, sem, *, layer):
    i = pl.program_id(0)
    g = tile_group[i]
    group_row = layer * N_GROUPS + g
    new_group = (i == 0) | (g != tile_group[jnp.maximum(i - 1, 0)])
    valid = tile_rows[i] > 0
    half = tile_rows[i] <= MOE_TILE // 2

    def weight_copies(e):
        slot = e % 2
        return (pltpu.make_async_copy(wg_hbm.at[group_row, e], stage_g.at[slot], sem.at[slot]),
                pltpu.make_async_copy(wu_hbm.at[group_row, e], stage_u.at[slot], sem.at[slot]),
                pltpu.make_async_copy(wd_hbm.at[group_row, e], stage_d.at[slot], sem.at[slot]))

    def run(load_weights, rows):
        x = x_ref[:rows, :]
        route = r_ref[:rows, :]
        lane = _iota(route.shape, 1)
        acc = jnp.zeros((rows, D_MODEL), F32)
        if load_weights:
            for e in range(2):
                for cp in weight_copies(e):
                    cp.start()
        for e in range(EXPERTS_PER_GROUP):
            if load_weights:
                for cp in weight_copies(e):
                    cp.wait()
                wg_b[e] = stage_g[e % 2].astype(BF16)
                wu_b[e] = stage_u[e % 2].astype(BF16)
                wd_b[e] = stage_d[e % 2].astype(BF16)
                if e + 2 < EXPERTS_PER_GROUP:
                    for cp in weight_copies(e + 2):
                        cp.start()
            cw = jnp.sum(jnp.where(lane == g * EXPERTS_PER_GROUP + e, route, 0.0), axis=-1, keepdims=True)
            act = _silu(_mm(x, wg_b[e])) * _mm(x, wu_b[e]) * cw
            acc = acc + _mm(act.astype(BF16), wd_b[e])
        y_ref[:rows, :] = acc
        if rows < MOE_TILE:
            y_ref[rows:, :] = jnp.zeros((MOE_TILE - rows, D_MODEL), F32)

    for load_weights in (True, False):
        for rows in (MOE_TILE, MOE_TILE // 2):
            first = new_group if load_weights else jnp.logical_not(new_group)
            fits = half if rows < MOE_TILE else jnp.logical_not(half)

            @pl.when(valid & first & fits)
            def _():
                run(load_weights, rows)

    @pl.when(jnp.logical_not(valid))
    def _():
        y_ref[...] = jnp.zeros_like(y_ref)


def _moe(xs, rs, tile_group, tile_rows, wg, wu, wd, layer):
    any_spec = pl.BlockSpec(memory_space=pl.ANY)
    grid_spec = pltpu.PrefetchScalarGridSpec(
        num_scalar_prefetch=2,
        grid=(MOE_ROWS // MOE_TILE,),
        in_specs=[pl.BlockSpec((MOE_TILE, D_MODEL), lambda i, tg, tv: (i, 0)),
                  pl.BlockSpec((MOE_TILE, LANES), lambda i, tg, tv: (i, 0)),
                  any_spec, any_spec, any_spec],
        out_specs=pl.BlockSpec((MOE_TILE, D_MODEL), lambda i, tg, tv: (i, 0)),
        scratch_shapes=[pltpu.VMEM((EXPERTS_PER_GROUP, D_MODEL, EXPERT_FF), BF16),
                        pltpu.VMEM((EXPERTS_PER_GROUP, D_MODEL, EXPERT_FF), BF16),
                        pltpu.VMEM((EXPERTS_PER_GROUP, EXPERT_FF, D_MODEL), BF16),
                        pltpu.VMEM((2, D_MODEL, EXPERT_FF), F32), pltpu.VMEM((2, D_MODEL, EXPERT_FF), F32),
                        pltpu.VMEM((2, EXPERT_FF, D_MODEL), F32), pltpu.SemaphoreType.DMA((2,))],
    )
    return pl.pallas_call(
        functools.partial(_moe_kernel, layer=layer),
        grid_spec=grid_spec,
        out_shape=jax.ShapeDtypeStruct((MOE_ROWS, D_MODEL), F32),
        compiler_params=_cparams("arbitrary"),
        name="moe",
    )(tile_group, tile_rows, xs, rs, wg, wu, wd)


def _combine_kernel(nblk, loc_blk, dst_blk, x_ref, lp_ref, mod_ref, ys_hbm, *refs, split):
    o_refs, ybuf, sem = refs[:-2], refs[-2], refs[-1]
    t = pl.program_id(0)
    slot = t % 2

    def fetch(tile, s):
        def copies(local, sorted_row):
            return (pltpu.make_async_copy(ys_hbm.at[pl.ds(sorted_row, SEG_BLK)], ybuf.at[s, pl.ds(local, SEG_BLK)], sem.at[s]),)
        ybuf[s] = jnp.zeros((LOCAL_ROWS, D_MODEL), F32)
        _segment_copies(tile, nblk, loc_blk, dst_blk, copies, lambda cp: cp.start())

    @pl.when(t == 0)
    def _():
        fetch(0, 0)

    @pl.when(t + 1 < N_TOK_TILES)
    def _():
        fetch(t + 1, 1 - slot)

    def copies_now(local, sorted_row):
        return (pltpu.make_async_copy(ys_hbm.at[pl.ds(sorted_row, SEG_BLK)], ybuf.at[slot, pl.ds(local, SEG_BLK)], sem.at[slot]),)

    _segment_copies(t, nblk, loc_blk, dst_blk, copies_now, lambda cp: cp.wait())

    onehot = (_iota((TOK_TILE, LOCAL_ROWS), 1) == lp_ref[...]).astype(BF16)
    ys = ybuf[slot]
    hi = ys.astype(BF16)
    lo = (ys - hi.astype(F32)).astype(BF16)
    y = _mm(onehot, hi) + _mm(onehot, lo)
    out = x_ref[...] + mod_ref[0, 5:6, :] * y
    if split:
        @pl.when(t < N_CTX_TILES)
        def _():
            o_refs[0][...] = out

        @pl.when(t >= N_CTX_TILES)
        def _():
            o_refs[1][...] = out
    else:
        o_refs[0][...] = out


def _combine(x, ys, mod, local_pos, nblk, loc_blk, dst_blk, split):
    tile = pl.BlockSpec((TOK_TILE, D_MODEL), lambda t, *_: (t, 0))
    if split:
        out_specs = _token_specs(True)
        out_shape = [jax.ShapeDtypeStruct((N_CTX, D_MODEL), F32), jax.ShapeDtypeStruct((N_LAT, D_MODEL), F32)]
    else:
        out_specs, out_shape = [tile], [jax.ShapeDtypeStruct((N_TOK, D_MODEL), F32)]
    grid_spec = pltpu.PrefetchScalarGridSpec(
        num_scalar_prefetch=3,
        grid=(N_TOK_TILES,),
        in_specs=[tile, pl.BlockSpec((TOK_TILE, 1), lambda t, *_: (t, 0)),
                  pl.BlockSpec((1, SUBLANES, D_MODEL), lambda t, *_: (_cond_of_tile(t), 0, 0)),
                  pl.BlockSpec(memory_space=pl.ANY)],
        out_specs=out_specs,
        scratch_shapes=[pltpu.VMEM((2, LOCAL_ROWS, D_MODEL), F32), pltpu.SemaphoreType.DMA((2,))],
    )
    return pl.pallas_call(
        functools.partial(_combine_kernel, split=split),
        grid_spec=grid_spec,
        out_shape=out_shape,
        compiler_params=_cparams("arbitrary"),
        name="combine",
    )(nblk, loc_blk, dst_blk, x, local_pos.reshape(N_TOK, 1), mod, ys)


def _lane_row(v):
    v = v.reshape(-1).astype(F32)
    return jnp.concatenate([v, jnp.zeros((LANES - v.shape[0],), F32)]).reshape(1, LANES)


def _pad_rows(w):
    return jnp.concatenate([w, jnp.zeros((SUBLANES - w.shape[0], w.shape[1]), w.dtype)], axis=0)


def kernel(x_prompt, x_sample, state_gdn, state_ret, cache_nat_k, cache_nat_v, c, c_ctx, ada_w, ada_b, norm_mix_w, norm_ffn_w, w_in, gdn_conv_w, gdn_a_log, gdn_dt_bias, gdn_norm_w, ret_gamma_logit, nat_q_norm_w, nat_k_norm_w, nat_rpb, sc_conv_w, w_out, router_group_w, router_group_b, router_expert_w, router_expert_b, moe_w_gate, moe_w_up, moe_w_down):
    x_ctx, x_lat, split = x_prompt.reshape(N_CTX, D_MODEL), x_sample.reshape(N_LAT, D_MODEL), True
    cond = jnp.concatenate([c_ctx[None, :], c], axis=0)
    ada = _ada(cond, ada_w, ada_b).reshape(DEPTH, SUBLANES, 6, D_MODEL)
    cos, sin = _rope_tables(DEC_SEQ)
    zero_state = jnp.zeros((BATCH, N_GATE, HEAD_DIM, HEAD_DIM), F32)
    lat_block = N_CTX // DEC_SEQ
    gdn_list, ret_list, caches = [], [], None
    mixed = [jnp.zeros((N_TOK, W_GROUP), F32) for _ in range(3)]
    for l in range(DEPTH):
        mod = jnp.concatenate([ada[l, :1 + DEC_BATCH], jnp.zeros((1 + DEC_BATCH, SUBLANES - 6, D_MODEL), F32)], axis=1)
        a_gdn, a_ret, a_nat, a_sc, a_gate = _inproj(x_ctx, x_lat, split, mod, norm_mix_w[l], w_in, l)

        conv_w = _pad_rows(gdn_conv_w[l])
        a_log, dt_b = _lane_row(gdn_a_log[l]), _lane_row(gdn_dt_bias[l])
        gnw = jnp.tile(gdn_norm_w[l], N_HEADS).reshape(1, W_GROUP)
        o_gdn, s_gdn = _gdn(a_gdn, a_gate, conv_w, a_log, dt_b, gnw, zero_state, SEQ, BATCH, 0, n_sub=2,
                            into=mixed[0])
        s0 = state_gdn[:, l].reshape(DEC_BATCH, N_GATE, HEAD_DIM, HEAD_DIM)
        o_gdn, _ = _gdn(a_gdn, a_gate, conv_w, a_log, dt_b, gnw, s0, DEC_SEQ, DEC_BATCH, lat_block, into=o_gdn)

        logit = jnp.repeat(ret_gamma_logit[l].astype(F32), HEAD_DIM, axis=1)
        o_ret, s_ret = _ret(a_ret, logit, zero_state, cos[:SEQ], sin[:SEQ], SEQ, BATCH, 0, False, into=mixed[1])
        s0 = state_ret[:, l].reshape(DEC_BATCH, N_GATE, HEAD_DIM, HEAD_DIM)
        o_ret, _ = _ret(a_ret, logit, s0, cos, sin, DEC_SEQ, DEC_BATCH, lat_block, True, into=o_ret)

        qw = jnp.tile(nat_q_norm_w[l], N_HEADS).reshape(1, W_GROUP)
        kw = jnp.tile(nat_k_norm_w[l], N_HEADS).reshape(1, W_GROUP)
        o_nat, *caches = _ctx_attn(a_nat, qw, kw, mixed[2], l, caches)
        o_nat = _nat(a_nat, cache_nat_k[:, l].reshape(DEC_BATCH, PAST_LEN, W_GROUP),
                     cache_nat_v[:, l].reshape(DEC_BATCH, PAST_LEN, W_GROUP), _nat_tables(nat_rpb[l]), qw, kw, o_nat)

        mixed = [o_gdn, o_ret, o_nat]
        rw, rb = _pack_router(router_expert_w[l], router_expert_b[l], router_group_w[l], router_group_b[l])
        x_mid, hf, route = _outproj(x_ctx, x_lat, split, mixed, a_sc, _pad_rows(sc_conv_w[l]), mod, norm_ffn_w[l],
                                    w_out, l, rw, rb)

        local_pos, nblk, loc_blk, dst_blk, tile_group, tile_rows = _moe_tables(route[:, GSEL_LANE].astype(jnp.int32))
        xs, rs = _dispatch(hf, route, local_pos, nblk, loc_blk, dst_blk)
        to_group = lambda w: w.reshape((DEPTH * N_GROUPS, EXPERTS_PER_GROUP) + w.shape[2:])
        ys = _moe(xs, rs, tile_group, tile_rows, to_group(moe_w_gate), to_group(moe_w_up), to_group(moe_w_down), l)
        last = l == DEPTH - 1
        out = _combine(x_mid, ys, mod, local_pos, nblk, loc_blk, dst_blk, split=last)
        x_ctx, x_lat, split = (out[0], out[1], True) if last else (out[0], out[0], False)

        gdn_list.append(s_gdn.reshape(BATCH, 2, N_HEADS, HEAD_DIM, HEAD_DIM))
        ret_list.append(s_ret.reshape(BATCH, 2, N_HEADS, HEAD_DIM, HEAD_DIM))
    new_k, new_v = [a.reshape(BATCH, DEPTH, N_HEADS, HEAD_DIM, SEQ).transpose(0, 1, 4, 2, 3) for a in caches]
    return (x_ctx.reshape(BATCH, SEQ, D_MODEL), x_lat.reshape(DEC_BATCH, DEC_SEQ, D_MODEL),
            jnp.stack(gdn_list, axis=1), jnp.stack(ret_list, axis=1), new_k, new_v)
```

```python
import functools

import numpy as np
import jax
import jax.numpy as jnp
from jax import lax
from jax.experimental import pallas as pl
from jax.experimental.pallas import tpu as pltpu

D_MODEL = 1024
BATCH = 16
SEQ = 256
DEPTH = 2
DEC_BATCH = 2
DEC_SEQ = 1024
PAST_LEN = 256
GRID_W = 64
HEAD_DIM = 64
W_GROUP = D_MODEL // 4
N_HEADS = W_GROUP // HEAD_DIM
CHUNK = 64
WIN_ROWS = 8
WIN_COLS = 16
ROPE_BASE = 10000.0
N_GROUPS = 4
EXPERTS_PER_GROUP = 8
N_EXPERTS = N_GROUPS * EXPERTS_PER_GROUP
EXPERT_FF = 256
EPS = 1e-6

N_CTX = BATCH * SEQ
N_LAT = DEC_BATCH * DEC_SEQ
N_TOK = N_CTX + N_LAT
LANES = 128
SUBLANES = 8
TOK_TILE = 512
MOE_TILE = 512
Q_TILE = 256
VMEM_LIMIT = 48 * 1024 * 1024
NEG_BIG = -1e30
N_GATE = 2 * N_HEADS

F32 = jnp.float32
BF16 = jnp.bfloat16
HI = lax.Precision.HIGHEST


def _mm(a, b, prec=None):
    return lax.dot_general(a, b, (((1,), (0,)), ((), ())), precision=prec, preferred_element_type=F32)


def _mm_tn(a, b, prec=None):
    return lax.dot_general(a, b, (((0,), (0,)), ((), ())), precision=prec, preferred_element_type=F32)


def _bmm(a, b):
    return _mm(a.astype(BF16), b.astype(BF16))


def _sigmoid(x):
    return 1.0 / (1.0 + jnp.exp(-x))


def _silu(x):
    return x * _sigmoid(x)


def _softplus(x):
    return jnp.maximum(x, 0.0) + jnp.log(1.0 + jnp.exp(-jnp.abs(x)))


def _iota(shape, dim):
    return lax.broadcasted_iota(jnp.int32, shape, dim)


def _cparams(*sem):
    return pltpu.CompilerParams(dimension_semantics=sem, vmem_limit_bytes=VMEM_LIMIT)


def _cond_of_tile(i):
    n_ctx_tiles = N_CTX // TOK_TILE
    return jnp.where(i < n_ctx_tiles, 0, 1 + (i - n_ctx_tiles) // (DEC_SEQ // TOK_TILE))


def _head_sum_matrix():
    return (_iota((W_GROUP, W_GROUP), 0) // HEAD_DIM == _iota((W_GROUP, W_GROUP), 1) // HEAD_DIM).astype(BF16)


def _head_sums(x, head_sum):
    hi = x.astype(BF16)
    lo = (x - hi.astype(F32)).astype(BF16)
    return _mm(hi, head_sum) + _mm(lo, head_sum)


N_COND = 1 + DEC_BATCH
ADA_TN = 1536


def _ada_kernel(c_ref, w_ref, b_ref, o_ref):
    def slab(s, accs):
        rows = pl.ds(pl.multiple_of(s * SUBLANES, SUBLANES), SUBLANES)
        w = w_ref[0, rows, :]
        return tuple(acc + w * jnp.tile(_silu(c_ref[r, rows, :]), (1, ADA_TN // LANES))
                     for r, acc in enumerate(accs))

    zero = jnp.zeros((SUBLANES, ADA_TN), F32)
    accs = lax.fori_loop(0, D_MODEL // SUBLANES, slab, (zero,) * N_COND, unroll=4)
    out = jnp.concatenate([jnp.sum(acc, axis=0, keepdims=True) for acc in accs]
                          + [jnp.zeros((SUBLANES - N_COND, ADA_TN), F32)], axis=0)
    o_ref[0] = out + b_ref[0]


def _ada(cond, ada_w, ada_b):
    n_out = 6 * D_MODEL
    cond_lanes = jnp.broadcast_to(cond[:, :, None], (N_COND, D_MODEL, LANES))
    return pl.pallas_call(
        _ada_kernel,
        grid=(DEPTH, n_out // ADA_TN),
        in_specs=[pl.BlockSpec((N_COND, D_MODEL, LANES), lambda l, j: (0, 0, 0)),
                  pl.BlockSpec((1, D_MODEL, ADA_TN), lambda l, j: (l, 0, j)),
                  pl.BlockSpec((1, 1, ADA_TN), lambda l, j: (l, 0, j))],
        out_specs=pl.BlockSpec((1, SUBLANES, ADA_TN), lambda l, j: (l, 0, j)),
        out_shape=jax.ShapeDtypeStruct((DEPTH, SUBLANES, n_out), F32),
        compiler_params=_cparams("arbitrary", "arbitrary"),
        name="ada",
    )(cond_lanes, ada_w, ada_b.reshape(DEPTH, 1, n_out))


IN_WIDTHS = (4 * W_GROUP, 4 * W_GROUP, 3 * W_GROUP, 3 * W_GROUP, LANES)
IN_PACKED = sum(IN_WIDTHS)


IN_TOTAL = 3 * W_GROUP + W_GROUP + 2 * N_GATE + 4 * W_GROUP + 3 * W_GROUP + 3 * W_GROUP
IN_GATE_SRC = 4 * W_GROUP
IN_SRC = (0, IN_GATE_SRC + 2 * N_GATE, IN_GATE_SRC + 2 * N_GATE + 4 * W_GROUP,
          IN_GATE_SRC + 2 * N_GATE + 7 * W_GROUP)
N_CTX_TILES = N_CTX // TOK_TILE


def _token_specs(split):
    lat0 = 0 if split else N_CTX_TILES
    return [pl.BlockSpec((TOK_TILE, D_MODEL), lambda i, *_: (jnp.minimum(i, N_CTX_TILES - 1), 0)),
            pl.BlockSpec((TOK_TILE, D_MODEL), lambda i, *_: (jnp.maximum(i, N_CTX_TILES) - N_CTX_TILES + lat0, 0))]


def _token_tile(i, ctx_ref, lat_ref):
    return jnp.where(i < N_CTX_TILES, ctx_ref[...], lat_ref[...])


def _inproj_kernel(xc_ref, xl_ref, mod_ref, nw_ref, w_ref, *refs):
    o_refs, w_s = refs[:-1], refs[-1]
    i = pl.program_id(0)

    @pl.when(i == 0)
    def _():
        piece = 256
        off = 0
        for src, width in zip(IN_SRC, IN_WIDTHS[:-1]):
            for c in range(0, width, piece):
                w_s[:, off + c:off + c + piece] = w_ref[0, src + c:src + c + piece, :].T.astype(BF16)
            off += width
        gate = w_ref[0, IN_GATE_SRC:IN_GATE_SRC + LANES, :].T
        w_s[:, off:] = jnp.where(_iota((D_MODEL, LANES), 1) < 2 * N_GATE, gate, 0.0).astype(BF16)

    x = _token_tile(i, xc_ref, xl_ref)
    y = x * lax.rsqrt(jnp.mean(x * x, axis=-1, keepdims=True) + EPS) * nw_ref[...]
    h = (y * (1.0 + mod_ref[0, 1:2, :]) + mod_ref[0, 0:1, :]).astype(BF16)
    off = 0
    for o_ref, width in zip(o_refs, IN_WIDTHS):
        o_ref[...] = _mm(h, w_s[:, off:off + width])
        off += width


def _inproj(x_ctx, x_lat, split, mod, norm_w, w_in, layer):
    return pl.pallas_call(
        _inproj_kernel,
        grid=(N_TOK // TOK_TILE,),
        in_specs=_token_specs(split)
                 + [pl.BlockSpec((1, SUBLANES, D_MODEL), lambda i: (_cond_of_tile(i), 0, 0)),
                    pl.BlockSpec((1, D_MODEL), lambda i: (0, 0)),
                    pl.BlockSpec((1, IN_TOTAL, D_MODEL), lambda i: (layer, 0, 0), pipeline_mode=pl.Buffered(1))],
        out_specs=[pl.BlockSpec((TOK_TILE, w), lambda i: (i, 0)) for w in IN_WIDTHS],
        out_shape=[jax.ShapeDtypeStruct((N_TOK, w), F32) for w in IN_WIDTHS],
        scratch_shapes=[pltpu.VMEM((D_MODEL, IN_PACKED), BF16)],
        compiler_params=_cparams("arbitrary"),
        name="inproj",
    )(x_ctx, x_lat, mod, norm_w.reshape(1, D_MODEL), jnp.swapaxes(w_in, 1, 2))


def _seq_call(kernel_fn, args, in_specs, out_specs, out_shape, into, more_into=(), **kwargs):
    donors = ([] if into is None else [(into, 0)]) + list(more_into)
    n_in = len(args)
    inner = kernel_fn
    kernel_fn = lambda *refs: inner(*refs[:n_in], *refs[n_in + len(donors):])
    aliases = {n_in + j: out_idx for j, (_, out_idx) in enumerate(donors)}
    args = list(args) + [a for a, _ in donors]
    in_specs = list(in_specs) + [pl.BlockSpec(memory_space=pl.ANY)] * len(donors)
    return pl.pallas_call(kernel_fn, in_specs=in_specs, out_specs=out_specs, out_shape=out_shape,
                          input_output_aliases=aliases, **kwargs)(*args)


def _shift_rows(p, seq_len):
    rows = p.shape[0]
    pos = _iota(p.shape, 0) & (seq_len - 1)
    prev = jnp.where(pos == 0, 0.0, pltpu.roll(p, 1, 0))
    nxt = jnp.where(pos == seq_len - 1, 0.0, pltpu.roll(p, rows - 1, 0))
    return prev, nxt


def _conv3(x, w_ref, seq_len):
    prev, nxt = _shift_rows(x, seq_len)
    return w_ref[0:1, :] * prev + w_ref[1:2, :] * x + w_ref[2:3, :] * nxt


def _chunk_scan(x, reverse):
    t = x.shape[0]
    pos = _iota(x.shape, 0) % CHUNK
    step = 1
    while step < CHUNK:
        if reverse:
            x = x + jnp.where(pos < CHUNK - step, pltpu.roll(x, t - step, 0), 0.0)
        else:
            x = x + jnp.where(pos >= step, pltpu.roll(x, step, 0), 0.0)
        step *= 2
    return x


GDN_GROUP_CHUNKS = 4
GDN_BASE_BLOCK = 4
GDN_CHAINS = GDN_GROUP_CHUNKS * N_GATE
GDN_PAIRS = GDN_GROUP_CHUNKS * N_HEADS


def _gdn_kernel(a_ref, gate_ref, convw_ref, alog_ref, dtb_ref, nw_ref, s0_ref, o_ref, sfin_ref,
                q_s, kv_s, kt_s, gc_s, eg_s, beta_s, gcrow_s, ekdrow_s, cdec_s, uo_s, wq_s, attn_s, kdt_s,
                st_s, wsqs_s, kk_s, d_s, m1_s, p_s, low_s, pb_s, rhs_s, *, t, n_sub):
    n_chunks = t // CHUNK
    a = a_ref[...]
    qkv = _silu(_conv3(a[:, :3 * W_GROUP], convw_ref, t // n_sub))
    q = qkv[:, :W_GROUP]
    k = qkv[:, W_GROUP:2 * W_GROUP]
    v = qkv[:, 2 * W_GROUP:]
    head_sum = _head_sum_matrix()
    q = q * lax.rsqrt(_head_sums(q * q, head_sum) + EPS) * (HEAD_DIM ** -0.5)
    k = k * lax.rsqrt(_head_sums(k * k, head_sum) + EPS)
    for h in range(N_HEADS):
        hs = slice(h * HEAD_DIM, (h + 1) * HEAD_DIM)
        q_s[h] = q[:, hs]
        kv_s[h] = jnp.concatenate([k[:, hs], v[:, hs]], axis=1)
    k_t = k.T
    for c in range(n_chunks):
        kt_s[c] = k_t[:, c * CHUNK:(c + 1) * CHUNK]

    gates = gate_ref[...]
    log_a = -jnp.exp(alog_ref[...]) * _softplus(gates + dtb_ref[...])
    beta_s[...] = _sigmoid(gates)

    ci = _iota((CHUNK, CHUNK), 0)
    cj = _iota((CHUNK, CHUNK), 1)
    eye = (ci == cj).astype(F32)
    blk_mask = (ci // GDN_BASE_BLOCK) == (cj // GDN_BASE_BLOCK)
    low_half = _iota((CHUNK, 2 * HEAD_DIM), 1) < HEAD_DIM

    prefix = _chunk_scan(log_a, reverse=False)
    suffix = _chunk_scan(log_a, reverse=True)
    gc = jnp.where(_iota((t, LANES), 1) < N_HEADS, prefix, suffix)
    gt = prefix + suffix - log_a
    gc_s[...] = gc
    eg_s[...] = jnp.exp(gc)
    gc_t = gc.T
    ekd_t = jnp.exp(gt - gc).T
    cdec_t = jnp.exp(gt).T
    for c in range(n_chunks):
        lanes = slice(c * CHUNK, (c + 1) * CHUNK)
        gcrow_s[c] = gc_t[:N_GATE, lanes]
        ekdrow_s[c] = ekd_t[:N_GATE, lanes]
        cdec_s[c] = jnp.concatenate([cdec_t[:N_GATE, lanes]] * 2, axis=1)

    def solve_group(grp, carry):
        row0 = grp * (GDN_GROUP_CHUNKS * CHUNK)
        chains = [(cl, a_idx) for cl in range(GDN_GROUP_CHUNKS) for a_idx in range(N_GATE)]

        def rows_of(cl):
            return pl.ds(pl.multiple_of(row0 + cl * CHUNK, CHUNK), CHUNK)

        for cl in range(GDN_GROUP_CHUNKS):
            for h in range(N_HEADS):
                rows = rows_of(cl)
                kq = jnp.concatenate([kv_s[h, rows, :HEAD_DIM], q_s[h, rows, :]], axis=0)
                k_t_h = kt_s[grp * GDN_GROUP_CHUNKS + cl, h * HEAD_DIM:(h + 1) * HEAD_DIM, :]
                kk_s[cl * N_HEADS + h] = _bmm(kq, k_t_h)
        for b, (cl, a_idx) in enumerate(chains):
            backward = a_idx >= N_HEADS
            h = a_idx % N_HEADS
            rows, c = rows_of(cl), grp * GDN_GROUP_CHUNKS + cl
            incl = (cj >= ci) if backward else (cj <= ci)
            strict = (cj > ci) if backward else (cj < ci)
            bt = beta_s[rows, N_GATE + a_idx:N_GATE + a_idx + 1]
            decay = jnp.exp(jnp.where(incl, gc_s[rows, a_idx:a_idx + 1] - gcrow_s[c, a_idx:a_idx + 1, :], NEG_BIG))
            low = jnp.where(strict, kk_s[cl * N_HEADS + h, :CHUNK, :] * bt * decay, 0.0)
            attn_s[a_idx, rows, :] = (kk_s[cl * N_HEADS + h, CHUNK:, :] * decay).astype(BF16)
            d_s[b] = jnp.where(blk_mask, low, 0.0)
            low_s[b] = low.astype(BF16)
            rhs_s[b] = (kv_s[h, rows, :] * bt
                        * jnp.where(low_half, eg_s[rows, a_idx:a_idx + 1], 1.0)).astype(BF16)
        for b in range(GDN_CHAINS):
            m1_s[b] = _bmm(d_s[b], d_s[b])
        for b in range(GDN_CHAINS):
            d, d2 = d_s[b], m1_s[b]
            p_s[b] = eye - d + d2 - _bmm(d, d2)
        size = GDN_BASE_BLOCK
        while size < CHUNK:
            pair = ((ci // size) != (cj // size)) & ((ci // (2 * size)) == (cj // (2 * size)))
            for b in range(GDN_CHAINS):
                coupling = jnp.where(pair, low_s[b], jnp.zeros((), BF16))
                pb_s[b] = _mm(p_s[b].astype(BF16), coupling).astype(BF16)
            for b in range(GDN_CHAINS):
                p = p_s[b]
                p_s[b] = p - _mm(pb_s[b], p.astype(BF16))
            size *= 2
        for b, (cl, a_idx) in enumerate(chains):
            h = a_idx % N_HEADS
            rows, c = rows_of(cl), grp * GDN_GROUP_CHUNKS + cl
            wu = _mm(p_s[b].astype(BF16), rhs_s[b])
            uo_s[a_idx, rows, :] = wu
            wq_s[a_idx, c, :CHUNK, :] = wu[:, :HEAD_DIM].astype(BF16)
            wq_s[a_idx, c, CHUNK:, :] = (q_s[h, rows, :] * eg_s[rows, a_idx:a_idx + 1]).astype(BF16)
            k_t_h = kt_s[c, h * HEAD_DIM:(h + 1) * HEAD_DIM, :]
            kdt_s[a_idx, c] = (k_t_h * ekdrow_s[c, a_idx:a_idx + 1, :]).astype(BF16)
        return carry

    lax.fori_loop(0, n_chunks // GDN_GROUP_CHUNKS, solve_group, 0)

    seq_chunks = n_chunks // n_sub
    chains = [(sub, a_idx) for sub in range(n_sub) for a_idx in range(N_GATE)]
    for j, (sub, a_idx) in enumerate(chains):
        st_s[j] = jnp.concatenate([jnp.zeros((HEAD_DIM, HEAD_DIM), F32), s0_ref[sub, a_idx]], axis=1)

    def scan_chunk(c, carry):
        def chunk_of(sub, a_idx):
            return sub * seq_chunks + ((seq_chunks - 1 - c) if a_idx >= N_HEADS else c)

        for j, (sub, a_idx) in enumerate(chains):
            wsqs_s[j] = _mm(wq_s[a_idx, chunk_of(sub, a_idx)], st_s[j].astype(BF16))
        for j, (sub, a_idx) in enumerate(chains):
            cc = chunk_of(sub, a_idx)
            rows = pl.ds(pl.multiple_of(cc * CHUNK, CHUNK), CHUNK)
            v_new = (uo_s[a_idx, rows, :] - wsqs_s[j, :CHUNK, :]).astype(BF16)
            uo_s[a_idx, rows, :] = wsqs_s[j, CHUNK:, :] + _mm(attn_s[a_idx, rows, :], v_new)
            st_s[j] = st_s[j] * cdec_s[cc, a_idx:a_idx + 1, :] + _mm(kdt_s[a_idx, cc], v_new)
        return carry

    lax.fori_loop(0, seq_chunks, scan_chunk, 0)
    for j, (sub, a_idx) in enumerate(chains):
        sfin_ref[sub, a_idx] = st_s[j, :, HEAD_DIM:]

    o = jnp.concatenate([(uo_s[h] + uo_s[N_HEADS + h])[:, HEAD_DIM:] for h in range(N_HEADS)], axis=1)
    ms = _head_sums(o * o, head_sum) * (1.0 / HEAD_DIM)
    o_ref[...] = o * lax.rsqrt(ms + EPS) * nw_ref[...] * _silu(a[:, 3 * W_GROUP:])


def _gdn(a_all, gate_all, conv_w, a_log, dt_bias, norm_w, s0, seq_len, n_seq, first_block, n_sub=1, into=None):
    small = lambda: pl.BlockSpec((1, LANES), lambda i: (0, 0))
    t = n_sub * seq_len
    n_chunks = t // CHUNK
    wide = 2 * HEAD_DIM
    scratch = [pltpu.VMEM((N_HEADS, t, HEAD_DIM), F32),
               pltpu.VMEM((N_HEADS, t, wide), F32),
               pltpu.VMEM((n_chunks, W_GROUP, CHUNK), F32),
               pltpu.VMEM((t, LANES), F32), pltpu.VMEM((t, LANES), F32), pltpu.VMEM((t, LANES), F32),
               pltpu.VMEM((n_chunks, N_GATE, CHUNK), F32), pltpu.VMEM((n_chunks, N_GATE, CHUNK), F32),
               pltpu.VMEM((n_chunks, N_GATE, wide), F32),
               pltpu.VMEM((N_GATE, t, wide), F32),
               pltpu.VMEM((N_GATE, n_chunks, 2 * CHUNK, HEAD_DIM), BF16),
               pltpu.VMEM((N_GATE, t, CHUNK), BF16),
               pltpu.VMEM((N_GATE, n_chunks, HEAD_DIM, CHUNK), BF16),
               pltpu.VMEM((n_sub * N_GATE, HEAD_DIM, wide), F32),
               pltpu.VMEM((n_sub * N_GATE, 2 * CHUNK, wide), F32),
               pltpu.VMEM((GDN_PAIRS, 2 * CHUNK, CHUNK), F32)]
    scratch += [pltpu.VMEM((GDN_CHAINS, CHUNK, CHUNK), F32)] * 3
    scratch += [pltpu.VMEM((GDN_CHAINS, CHUNK, CHUNK), BF16)] * 2
    scratch += [pltpu.VMEM((GDN_CHAINS, CHUNK, wide), BF16)]
    return _seq_call(
        functools.partial(_gdn_kernel, t=t, n_sub=n_sub), (a_all, gate_all, conv_w, a_log, dt_bias, norm_w, s0),
        in_specs=[pl.BlockSpec((t, 4 * W_GROUP), lambda i: (i + first_block, 0)),
                  pl.BlockSpec((t, LANES), lambda i: (i + first_block, 0)),
                  pl.BlockSpec((SUBLANES, 3 * W_GROUP), lambda i: (0, 0)),
                  small(), small(),
                  pl.BlockSpec((1, W_GROUP), lambda i: (0, 0)),
                  pl.BlockSpec((n_sub, N_GATE, HEAD_DIM, HEAD_DIM), lambda i: (i, 0, 0, 0))],
        out_specs=[pl.BlockSpec((t, W_GROUP), lambda i: (i + first_block, 0)),
                   pl.BlockSpec((n_sub, N_GATE, HEAD_DIM, HEAD_DIM), lambda i: (i, 0, 0, 0))],
        out_shape=[jax.ShapeDtypeStruct((N_TOK, W_GROUP), F32),
                   jax.ShapeDtypeStruct((n_seq, N_GATE, HEAD_DIM, HEAD_DIM), F32)],
        scratch_shapes=scratch,
        into=into, grid=(n_seq // n_sub,), compiler_params=_cparams("arbitrary"), name="gdn")


def _swap16(x):
    width = x.shape[-1]
    first = (_iota(x.shape, 1) // 16) % 2 == 0
    return jnp.where(first, pltpu.roll(x, width - 16, 1), pltpu.roll(x, 16, 1))


def _block_diag_heads(s0_ref, first):
    zero = jnp.zeros((HEAD_DIM, HEAD_DIM), F32)
    return jnp.concatenate(
        [jnp.concatenate([s0_ref[0, first + h] if j == h else zero for j in range(N_HEADS)], axis=1)
         for h in range(N_HEADS)], axis=0)


def _ret_kernel(r_ref, lg_ref, s0_ref, cos_ref, sin_ref, o_ref, sfin_ref, *, t, latent):
    r = r_ref[...]
    q = r[:, :W_GROUP]
    k = r[:, W_GROUP:2 * W_GROUP]
    v = r[:, 2 * W_GROUP:3 * W_GROUP]
    if latent:
        q = q * cos_ref[...] + _swap16(q) * sin_ref[...]
        k = k * cos_ref[...] + _swap16(k) * sin_ref[...]
    k = k * (HEAD_DIM ** -0.5)
    lg = -_softplus(-lg_ref[...])
    lgf, lgb = lg[0:1, :], lg[1:2, :]
    head = _iota((1, W_GROUP), 1) // HEAD_DIM
    head_sum = _head_sum_matrix()
    pos = _iota((t, 1), 0).astype(F32)
    q_b = q.astype(BF16)
    kt_b = k.T.astype(BF16)
    v_heads = [jnp.where(head == h, v, 0.0).astype(BF16) for h in range(N_HEADS)]
    if latent:
        s0f = _block_diag_heads(s0_ref, 0)
        s0b = _block_diag_heads(s0_ref, N_HEADS)
    for qt in range(t // Q_TILE):
        rows = slice(qt * Q_TILE, (qt + 1) * Q_TILE)
        diff = (_iota((Q_TILE, t), 0) + qt * Q_TILE - _iota((Q_TILE, t), 1)).astype(F32)
        both = jnp.where(diff == 0, 2.0, 1.0)
        o = jnp.zeros((Q_TILE, W_GROUP), F32)
        for h in range(N_HEADS):
            lgf_h = lgf[:, h * HEAD_DIM:h * HEAD_DIM + 1]
            lgb_h = lgb[:, h * HEAD_DIM:h * HEAD_DIM + 1]
            dmat = jnp.exp(diff * jnp.where(diff >= 0, lgf_h, -lgb_h)) * both
            s = _mm(jnp.where(head == h, q_b[rows], 0.0), kt_b) * dmat
            o = o + _mm(s.astype(BF16), v_heads[h])
        if latent:
            p = pos[rows]
            o = o + jnp.exp((p + 1.0) * lgf) * _bmm(q_b[rows], s0f) + jnp.exp((t - p) * lgb) * _bmm(q_b[rows], s0b)
        oc = o - _head_sums(o, head_sum) * (1.0 / HEAD_DIM)
        on = oc * lax.rsqrt(_head_sums(oc * oc, head_sum) * (1.0 / HEAD_DIM) + EPS)
        o_ref[rows, :] = on * _silu(r[rows, 3 * W_GROUP:])
    v_b = v.astype(BF16)
    sf = _mm_tn((k * jnp.exp((t - 1.0 - pos) * lgf)).astype(BF16), v_b)
    sb = _mm_tn((k * jnp.exp(pos * lgb)).astype(BF16), v_b)
    for h in range(N_HEADS):
        hs = slice(h * HEAD_DIM, (h + 1) * HEAD_DIM)
        sf_h, sb_h = sf[hs, hs], sb[hs, hs]
        if latent:
            sf_h = sf_h + jnp.exp(t * lgf[:, h * HEAD_DIM:h * HEAD_DIM + 1]) * s0_ref[0, h]
            sb_h = sb_h + jnp.exp(t * lgb[:, h * HEAD_DIM:h * HEAD_DIM + 1]) * s0_ref[0, N_HEADS + h]
        sfin_ref[0, h] = sf_h
        sfin_ref[0, N_HEADS + h] = sb_h


def _ret(r_all, logit, s0, cos, sin, t, n_seq, first_block, latent, into=None):
    return _seq_call(
        functools.partial(_ret_kernel, t=t, latent=latent), (r_all, logit, s0, cos, sin),
        in_specs=[pl.BlockSpec((t, 4 * W_GROUP), lambda i: (i + first_block, 0)),
                  pl.BlockSpec((2, W_GROUP), lambda i: (0, 0)),
                  pl.BlockSpec((1, N_GATE, HEAD_DIM, HEAD_DIM), lambda i: (i, 0, 0, 0)),
                  pl.BlockSpec((t, W_GROUP), lambda i: (0, 0)),
                  pl.BlockSpec((t, W_GROUP), lambda i: (0, 0))],
        out_specs=[pl.BlockSpec((t, W_GROUP), lambda i: (i + first_block, 0)),
                   pl.BlockSpec((1, N_GATE, HEAD_DIM, HEAD_DIM), lambda i: (i, 0, 0, 0))],
        out_shape=[jax.ShapeDtypeStruct((N_TOK, W_GROUP), F32),
                   jax.ShapeDtypeStruct((n_seq, N_GATE, HEAD_DIM, HEAD_DIM), F32)],
        into=into, grid=(n_seq,), compiler_params=_cparams("arbitrary"), name="ret")


def _rope_tables(t):
    pos = np.arange(t)
    row = (pos // GRID_W).astype(np.float32)
    col = (pos % GRID_W).astype(np.float32)
    nf = HEAD_DIM // 4
    inv_freq = jnp.power(ROPE_BASE, -jnp.arange(nf, dtype=F32) / nf)
    ang_r = jnp.asarray(row)[:, None] * inv_freq[None, :]
    ang_c = jnp.asarray(col)[:, None] * inv_freq[None, :]
    cos = jnp.concatenate([jnp.cos(ang_r)] * 2 + [jnp.cos(ang_c)] * 2, axis=1)
    sin = jnp.concatenate([-jnp.sin(ang_r), jnp.sin(ang_r), -jnp.sin(ang_c), jnp.sin(ang_c)], axis=1)
    return jnp.tile(cos, (1, N_HEADS)), jnp.tile(sin, (1, N_HEADS))


def _head_rms(x, w, head_sum):
    return x * lax.rsqrt(_head_sums(x * x, head_sum) * (1.0 / HEAD_DIM) + EPS) * w


def _ctx_attn_kernel(n_ref, qw_ref, kw_ref, o_ref, k_out, v_out, *, n_fill):
    n = n_ref[...]
    head_sum = _head_sum_matrix()
    head = _iota((1, W_GROUP), 1) // HEAD_DIM
    q = (_head_rms(n[:, :W_GROUP], qw_ref[...], head_sum) * (HEAD_DIM ** -0.5)).astype(BF16)
    k = _head_rms(n[:, W_GROUP:2 * W_GROUP], kw_ref[...], head_sum)
    v = n[:, 2 * W_GROUP:]
    k_t = k.T
    k_out[0, 0] = k_t
    v_out[0, 0] = v.T
    for later in range(1, 1 + n_fill):
        k_out[0, later] = jnp.zeros((W_GROUP, SEQ), F32)
        v_out[0, later] = jnp.zeros((W_GROUP, SEQ), F32)
    kt_b = k_t.astype(BF16)
    o = jnp.zeros((SEQ, W_GROUP), F32)
    for h in range(N_HEADS):
        s = _mm(jnp.where(head == h, q, 0.0), kt_b)
        p = jnp.exp(s - jnp.max(s, axis=-1, keepdims=True))
        o = o + _mm(p.astype(BF16), jnp.where(head == h, v, 0.0).astype(BF16)) / jnp.sum(p, axis=-1, keepdims=True)
    o_ref[...] = o


def _ctx_attn(n_all, qw, kw, into, layer, caches):
    slots = DEPTH if caches is None else 1
    cache_spec = pl.BlockSpec((1, slots, W_GROUP, SEQ), lambda i: (i, layer if caches is not None else 0, 0, 0))
    cache_shape = jax.ShapeDtypeStruct((BATCH, DEPTH, W_GROUP, SEQ), F32)
    return _seq_call(
        functools.partial(_ctx_attn_kernel, n_fill=slots - 1), (n_all, qw, kw),
        in_specs=[pl.BlockSpec((SEQ, 3 * W_GROUP), lambda i: (i, 0)),
                  pl.BlockSpec((1, W_GROUP), lambda i: (0, 0)),
                  pl.BlockSpec((1, W_GROUP), lambda i: (0, 0))],
        out_specs=[pl.BlockSpec((SEQ, W_GROUP), lambda i: (i, 0)), cache_spec, cache_spec],
        out_shape=[jax.ShapeDtypeStruct((N_TOK, W_GROUP), F32), cache_shape, cache_shape],
        into=into, more_into=() if caches is None else ((caches[0], 1), (caches[1], 2)),
        grid=(BATCH,), compiler_params=_cparams("arbitrary"), name="ctx_attn")


def _nat_kernel(n_ref, ck_ref, cv_ref, bias_ref, qw_ref, kw_ref, o_ref, q_s, kt_s, ckt_s):
    h = pl.program_id(1)

    @pl.when(h == 0)
    def _():
        head_sum = _head_sum_matrix()
        n = n_ref[...]
        q_s[...] = (_head_rms(n[:, :W_GROUP], qw_ref[...], head_sum) * (HEAD_DIM ** -0.5)).astype(BF16)
        kt_s[...] = _head_rms(n[:, W_GROUP:2 * W_GROUP], kw_ref[...], head_sum).T.astype(BF16)
        ckt_s[...] = ck_ref[0].T.astype(BF16)
        o_ref[...] = jnp.zeros_like(o_ref)

    mine = _iota((1, W_GROUP), 1) // HEAD_DIM == h
    v = jnp.where(mine, n_ref[:, 2 * W_GROUP:], 0.0).astype(BF16)
    cv = jnp.where(mine, cv_ref[0], 0.0).astype(BF16)
    for qt in range(DEC_SEQ // Q_TILE):
        rows = slice(qt * Q_TILE, (qt + 1) * Q_TILE)
        q = jnp.where(mine, q_s[rows, :], 0.0)
        grid_rows = range(qt * Q_TILE // GRID_W, (qt + 1) * Q_TILE // GRID_W)
        s_loc = _mm(q, kt_s[...]) + jnp.concatenate([_nat_bias_strip(bias_ref, row) for row in grid_rows], axis=0)
        s_ctx = _mm(q, ckt_s[...])
        m = jnp.maximum(jnp.max(s_loc, axis=-1, keepdims=True), jnp.max(s_ctx, axis=-1, keepdims=True))
        p_loc = jnp.exp(s_loc - m)
        p_ctx = jnp.exp(s_ctx - m)
        den = jnp.sum(p_loc, axis=-1, keepdims=True) + jnp.sum(p_ctx, axis=-1, keepdims=True)
        o_ref[rows, :] += (_mm(p_loc.astype(BF16), v) + _mm(p_ctx.astype(BF16), cv)) / den


def _nat(n_all, ck, cv, bias, qw, kw, into):
    first_block = N_CTX // DEC_SEQ
    return _seq_call(
        _nat_kernel, (n_all, ck, cv, bias, qw, kw),
        in_specs=[pl.BlockSpec((DEC_SEQ, 3 * W_GROUP), lambda b, h: (b + first_block, 0)),
                  pl.BlockSpec((1, PAST_LEN, W_GROUP), lambda b, h: (b, 0, 0)),
                  pl.BlockSpec((1, PAST_LEN, W_GROUP), lambda b, h: (b, 0, 0)),
                  pl.BlockSpec((1, 3 * N_ROW_OFF - 1, GRID_W, 2 * GRID_W), lambda b, h: (h, 0, 0, 0)),
                  pl.BlockSpec((1, W_GROUP), lambda b, h: (0, 0)),
                  pl.BlockSpec((1, W_GROUP), lambda b, h: (0, 0))],
        out_specs=pl.BlockSpec((DEC_SEQ, W_GROUP), lambda b, h: (b + first_block, 0)),
        out_shape=jax.ShapeDtypeStruct((N_TOK, W_GROUP), F32),
        scratch_shapes=[pltpu.VMEM((DEC_SEQ, W_GROUP), BF16), pltpu.VMEM((W_GROUP, DEC_SEQ), BF16),
                        pltpu.VMEM((W_GROUP, PAST_LEN), BF16)],
        into=into, grid=(DEC_BATCH, N_HEADS), compiler_params=_cparams("arbitrary", "arbitrary"), name="nat")


N_ROW_OFF = 2 * WIN_ROWS - 1
NAT_ROWS = DEC_SEQ // GRID_W
NAT_KH = min(WIN_ROWS, NAT_ROWS)
NAT_PAIR, NAT_LOW, NAT_HIGH = 0, N_ROW_OFF - 1, 2 * N_ROW_OFF - 1


def _nat_tables(rpb):
    c = np.arange(GRID_W)
    c0 = np.clip(c - WIN_COLS // 2, 0, GRID_W - WIN_COLS)
    col_in = (c[None, :] >= c0[:, None]) & (c[None, :] < c0[:, None] + WIN_COLS)
    col_idx = np.clip(c[None, :] - c[:, None], -(WIN_COLS - 1), WIN_COLS - 1) + WIN_COLS - 1
    col_hot = (col_idx[..., None] == np.arange(2 * WIN_COLS - 1)).astype(np.float32)
    tz = jnp.where(col_in, jnp.einsum('hab,qkb->haqk', rpb.astype(F32), col_hot, precision=HI), NEG_BIG)
    neg = jnp.full_like(tz, NEG_BIG)
    return jnp.concatenate([jnp.concatenate([tz[:, :-1], tz[:, 1:]], axis=-1),
                            jnp.concatenate([tz, neg], axis=-1), jnp.concatenate([neg, tz], axis=-1)], axis=1)


def _nat_bias_strip(tab_ref, row):
    first_key = min(max(row - NAT_KH // 2, 0), NAT_ROWS - NAT_KH)
    off = first_key - row + WIN_ROWS - 1
    tiles = {}
    done, key = 0, first_key
    if key % 2 == 1:
        tiles[key // 2] = tab_ref[0, NAT_HIGH + off]
        done, key = 1, key + 1
    while done + 1 < NAT_KH:
        tiles[key // 2] = tab_ref[0, NAT_PAIR + off + done]
        done, key = done + 2, key + 2
    if done < NAT_KH:
        tiles[key // 2] = tab_ref[0, NAT_LOW + off + done]
    outside = jnp.full((GRID_W, 2 * GRID_W), NEG_BIG, F32)
    return jnp.concatenate([tiles.get(i, outside) for i in range(NAT_ROWS // 2)], axis=1)


GSEL_LANE = N_EXPERTS


def _pack_router(we, be, wg, bg):
    pad = LANES - N_EXPERTS - N_GROUPS
    w = jnp.concatenate([we, wg, jnp.zeros((D_MODEL, pad), F32)], axis=1)
    b = jnp.concatenate([be, bg, jnp.zeros((pad,), F32)]).reshape(1, LANES)
    hi = w.astype(BF16)
    lo = (w - hi.astype(F32)).astype(BF16)
    return jnp.concatenate([hi, lo], axis=1), b


def _lane_min_where(mask, lane):
    return jnp.min(jnp.where(mask, lane, LANES), axis=-1, keepdims=True)


def _sconv_tile(i, s_ref, before_ref, after_ref, w_ref):
    s = s_ref[...]
    p = s[:, W_GROUP:2 * W_GROUP] * s[:, 2 * W_GROUP:]
    p_before = before_ref[SUBLANES - 1:, W_GROUP:2 * W_GROUP] * before_ref[SUBLANES - 1:, 2 * W_GROUP:]
    p_after = after_ref[:1, W_GROUP:2 * W_GROUP] * after_ref[:1, 2 * W_GROUP:]
    row = _iota((TOK_TILE, 1), 0)
    seq_len = jnp.where(i < N_CTX_TILES, SEQ, DEC_SEQ)
    pos = (i * TOK_TILE + row) & (seq_len - 1)
    prev = jnp.where(row == 0, p_before, pltpu.roll(p, 1, 0))
    nxt = jnp.where(row == TOK_TILE - 1, p_after, pltpu.roll(p, TOK_TILE - 1, 0))
    prev = jnp.where(pos == 0, 0.0, prev)
    nxt = jnp.where(pos == seq_len - 1, 0.0, nxt)
    return s[:, :W_GROUP] * (w_ref[0:1, :] * prev + w_ref[1:2, :] * p + w_ref[2:3, :] * nxt)


def _outproj_kernel(xc_ref, xl_ref, m0, m1, m2, s_ref, before_ref, after_ref, cw_ref, mod_ref, nw_ref, w_ref,
                    rw_ref, rb_ref, x_out, hf_out, route_out, w_s):
    tile_i = pl.program_id(0)

    @pl.when(tile_i == 0)
    def _():
        w_s[...] = w_ref[0].astype(BF16)

    mixed = [m0[...], m1[...], m2[...], _sconv_tile(tile_i, s_ref, before_ref, after_ref, cw_ref)]
    acc = None
    for i, m in enumerate(mixed):
        part = _mm(m.astype(BF16), w_s[i * W_GROUP:(i + 1) * W_GROUP, :])
        acc = part if acc is None else acc + part
    x = _token_tile(tile_i, xc_ref, xl_ref) + mod_ref[0, 2:3, :] * acc
    x_out[...] = x
    y = x * lax.rsqrt(jnp.mean(x * x, axis=-1, keepdims=True) + EPS) * nw_ref[...]
    hf = y * (1.0 + mod_ref[0, 4:5, :]) + mod_ref[0, 3:4, :]
    hf_hi = hf.astype(BF16)
    hf_out[...] = hf_hi

    hf_lo = (hf - hf_hi.astype(F32)).astype(BF16)
    both = _mm(hf_hi, rw_ref[...])
    logits = both[:, :LANES] + both[:, LANES:] + _mm(hf_lo, rw_ref[:, :LANES]) + rb_ref[...]
    lane = _iota(logits.shape, 1)
    is_g = (lane >= N_EXPERTS) & (lane < N_EXPERTS + N_GROUPS)
    gl = jnp.where(is_g, logits, NEG_BIG)
    ge = jnp.exp(gl - jnp.max(gl, axis=-1, keepdims=True))
    gp = jnp.where(is_g, ge / jnp.sum(ge, axis=-1, keepdims=True), -1.0)
    gw = jnp.max(gp, axis=-1, keepdims=True)
    gsel = _lane_min_where(gp == gw, lane) - N_EXPERTS
    in_grp = (lane // EXPERTS_PER_GROUP == gsel) & (lane < N_EXPERTS)
    el = jnp.where(in_grp, logits, NEG_BIG)
    ee = jnp.exp(el - jnp.max(el, axis=-1, keepdims=True))
    ep = jnp.where(in_grp, ee / jnp.sum(ee, axis=-1, keepdims=True), -1.0)
    t1 = jnp.max(ep, axis=-1, keepdims=True)
    i1 = _lane_min_where(ep == t1, lane)
    ep2 = jnp.where(lane == i1, -1.0, ep)
    t2 = jnp.max(ep2, axis=-1, keepdims=True)
    i2 = _lane_min_where(ep2 == t2, lane)
    tsum = t1 + t2
    combine = jnp.where(lane == i1, gw * (t1 / tsum), 0.0) + jnp.where(lane == i2, gw * (t2 / tsum), 0.0)
    route_out[...] = jnp.where(lane == GSEL_LANE, gsel.astype(F32), combine)


def _outproj(x_ctx, x_lat, split, mixed, a_sc, sc_w, mod, norm_w, w_out, layer, rw, rb):
    tile = lambda w: pl.BlockSpec((TOK_TILE, w), lambda i: (i, 0))
    whole = lambda a: pl.BlockSpec(a.shape, lambda i: (0,) * a.ndim)
    per_tile = TOK_TILE // SUBLANES
    halo = lambda index: pl.BlockSpec((SUBLANES, 3 * W_GROUP), lambda i: (index(i), 0))
    return pl.pallas_call(
        _outproj_kernel,
        grid=(N_TOK // TOK_TILE,),
        in_specs=_token_specs(split) + [tile(W_GROUP)] * 3
                 + [tile(3 * W_GROUP), halo(lambda i: jnp.maximum(i * per_tile - 1, 0)),
                    halo(lambda i: jnp.minimum((i + 1) * per_tile, N_TOK // SUBLANES - 1)), whole(sc_w)]
                 + [pl.BlockSpec((1, SUBLANES, D_MODEL), lambda i: (_cond_of_tile(i), 0, 0)),
                    pl.BlockSpec((1, D_MODEL), lambda i: (0, 0)),
                    pl.BlockSpec((1, D_MODEL, D_MODEL), lambda i: (layer, 0, 0), pipeline_mode=pl.Buffered(1)),
                    whole(rw), whole(rb)],
        out_specs=[tile(D_MODEL), tile(D_MODEL), tile(LANES)],
        out_shape=[jax.ShapeDtypeStruct((N_TOK, D_MODEL), F32), jax.ShapeDtypeStruct((N_TOK, D_MODEL), BF16),
                   jax.ShapeDtypeStruct((N_TOK, LANES), F32)],
        scratch_shapes=[pltpu.VMEM((D_MODEL, D_MODEL), BF16)],
        compiler_params=_cparams("arbitrary"),
        name="outproj",
    )(x_ctx, x_lat, *mixed, a_sc, a_sc, a_sc, sc_w, mod, norm_w.reshape(1, D_MODEL), w_out, rw, rb)


SEG_BLK = 16
LOCAL_ROWS = TOK_TILE + N_GROUPS * SEG_BLK
N_TOK_TILES = N_TOK // TOK_TILE
MOE_ROWS = -(-(N_TOK + N_TOK_TILES * N_GROUPS * (SEG_BLK - 1) + N_GROUPS * (MOE_TILE - 1)) // MOE_TILE) * MOE_TILE


def _moe_tables(gsel):
    groups = jnp.arange(N_GROUPS, dtype=jnp.int32)
    onehot = (gsel.reshape(N_TOK_TILES, TOK_TILE, 1) == groups).astype(jnp.int32)
    earlier = jnp.asarray(np.tril(np.ones((TOK_TILE, TOK_TILE), np.float32), -1))
    rank = jnp.einsum('ts,nsg->ntg', earlier, onehot.astype(F32)).astype(jnp.int32)
    nblk = (jnp.sum(onehot, axis=1) + SEG_BLK - 1) // SEG_BLK
    loc_blk = jnp.cumsum(nblk, axis=1) - nblk
    blocks_per_tile = MOE_TILE // SEG_BLK
    grp_tiles = (jnp.sum(nblk, axis=0) + blocks_per_tile - 1) // blocks_per_tile
    grp_tile_start = jnp.cumsum(grp_tiles) - grp_tiles
    dst_blk = grp_tile_start[None, :] * blocks_per_tile + jnp.cumsum(nblk, axis=0) - nblk
    local_pos = jnp.sum(onehot * (loc_blk[:, None, :] * SEG_BLK + rank), axis=2)
    tile_idx = jnp.arange(MOE_ROWS // MOE_TILE, dtype=jnp.int32)
    tile_group = jnp.clip(jnp.sum(tile_idx[:, None] >= grp_tile_start[None, :], axis=1) - 1, 0, N_GROUPS - 1)
    in_group = tile_idx - grp_tile_start[tile_group]
    tile_rows = jnp.clip(jnp.sum(nblk, axis=0)[tile_group] * SEG_BLK - in_group * MOE_TILE, 0, MOE_TILE)
    flat = lambda a: a.reshape(-1).astype(jnp.int32)
    return local_pos.astype(jnp.int32), flat(nblk), flat(loc_blk), flat(dst_blk), flat(tile_group), flat(tile_rows)


def _segment_copies(t, nblk, loc_blk, dst_blk, make_copies, action):
    for g in range(N_GROUPS):
        k = t * N_GROUPS + g

        @pl.loop(0, nblk[k])
        def _(b):
            local = pl.multiple_of((loc_blk[k] + b) * SEG_BLK, SEG_BLK)
            sorted_row = pl.multiple_of((dst_blk[k] + b) * SEG_BLK, SEG_BLK)
            for cp in make_copies(local, sorted_row):
                action(cp)


def _dispatch_kernel(nblk, loc_blk, dst_blk, hf_ref, rt_ref, lp_ref, xs_in, rs_in, xs_hbm, rs_hbm, xbuf, rbuf, sem):
    t = pl.program_id(0)
    slot = t % 2
    onehot = _iota((LOCAL_ROWS, TOK_TILE), 0) == lp_ref[0]
    xbuf[slot] = _mm(onehot.astype(BF16), hf_ref[...]).astype(BF16)
    rbuf[slot] = _mm(onehot.astype(F32), rt_ref[...], HI)

    def copies_of(s):
        def copies(local, sorted_row):
            return (pltpu.make_async_copy(xbuf.at[s, pl.ds(local, SEG_BLK)], xs_hbm.at[pl.ds(sorted_row, SEG_BLK)], sem.at[s]),
                    pltpu.make_async_copy(rbuf.at[s, pl.ds(local, SEG_BLK)], rs_hbm.at[pl.ds(sorted_row, SEG_BLK)], sem.at[s]))
        return copies

    @pl.when(t > 0)
    def _():
        _segment_copies(t - 1, nblk, loc_blk, dst_blk, copies_of(1 - slot), lambda cp: cp.wait())

    _segment_copies(t, nblk, loc_blk, dst_blk, copies_of(slot), lambda cp: cp.start())

    @pl.when(t == N_TOK_TILES - 1)
    def _():
        _segment_copies(t, nblk, loc_blk, dst_blk, copies_of(slot), lambda cp: cp.wait())


def _dispatch(hf, route, local_pos, nblk, loc_blk, dst_blk):
    grid_spec = pltpu.PrefetchScalarGridSpec(
        num_scalar_prefetch=3,
        grid=(N_TOK_TILES,),
        in_specs=[pl.BlockSpec((TOK_TILE, D_MODEL), lambda t, *_: (t, 0)),
                  pl.BlockSpec((TOK_TILE, LANES), lambda t, *_: (t, 0)),
                  pl.BlockSpec((1, 1, TOK_TILE), lambda t, *_: (t, 0, 0)),
                  pl.BlockSpec(memory_space=pl.ANY), pl.BlockSpec(memory_space=pl.ANY)],
        out_specs=[pl.BlockSpec(memory_space=pl.ANY), pl.BlockSpec(memory_space=pl.ANY)],
        scratch_shapes=[pltpu.VMEM((2, LOCAL_ROWS, D_MODEL), BF16), pltpu.VMEM((2, LOCAL_ROWS, LANES), F32),
                        pltpu.SemaphoreType.DMA((2,))],
    )
    return pl.pallas_call(
        _dispatch_kernel,
        grid_spec=grid_spec,
        out_shape=[jax.ShapeDtypeStruct((MOE_ROWS, D_MODEL), BF16), jax.ShapeDtypeStruct((MOE_ROWS, LANES), F32)],
        input_output_aliases={6: 0, 7: 1},
        compiler_params=_cparams("arbitrary"),
        name="dispatch",
    )(nblk, loc_blk, dst_blk, hf, route, local_pos.reshape(N_TOK_TILES, 1, TOK_TILE),
      jnp.zeros((MOE_ROWS, D_MODEL), BF16), jnp.zeros((MOE_ROWS, LANES), F32))


def _moe_kernel(tile_group, tile_rows, x_ref, r_ref, wg_hbm, wu_hbm, wd_hbm, y_ref,
                wg_b, wu_b, wd_b, stage_g, stage_u, stage_d, sem, *, layer):
    i = pl.program_id(0)
    g = tile_group[i]
    group_row = layer * N_GROUPS + g
    new_group = (i == 0) | (g != tile_group[jnp.maximum(i - 1, 0)])
    valid = tile_rows[i] > 0
    half = tile_rows[i] <= MOE_TILE // 2

    def weight_copies(e):
        slot = e % 2
        return (pltpu.make_async_copy(wg_hbm.at[group_row, e], stage_g.at[slot], sem.at[slot]),
                pltpu.make_async_copy(wu_hbm.at[group_row, e], stage_u.at[slot], sem.at[slot]),
                pltpu.make_async_copy(wd_hbm.at[group_row, e], stage_d.at[slot], sem.at[slot]))

    def run(load_weights, rows):
        x = x_ref[:rows, :]
        route = r_ref[:rows, :]
        lane = _iota(route.shape, 1)
        acc = jnp.zeros((rows, D_MODEL), F32)
        if load_weights:
            for e in range(2):
                for cp in weight_copies(e):
                    cp.start()
        for e in range(EXPERTS_PER_GROUP):
            if load_weights:
                for cp in weight_copies(e):
                    cp.wait()
                wg_b[e] = stage_g[e % 2].astype(BF16)
                wu_b[e] = stage_u[e % 2].astype(BF16)
                wd_b[e] = stage_d[e % 2].astype(BF16)
                if e + 2 < EXPERTS_PER_GROUP:
                    for cp in weight_copies(e + 2):
                        cp.start()
            cw = jnp.sum(jnp.where(lane == g * EXPERTS_PER_GROUP + e, route, 0.0), axis=-1, keepdims=True)
            act = _silu(_mm(x, wg_b[e])) * _mm(x, wu_b[e]) * cw
            acc = acc + _mm(act.astype(BF16), wd_b[e])
        y_ref[:rows, :] = acc
        if rows < MOE_TILE:
            y_ref[rows:, :] = jnp.zeros((MOE_TILE - rows, D_MODEL), F32)

    for load_weights in (True, False):
        for rows in (MOE_TILE, MOE_TILE // 2):
            first = new_group if load_weights else jnp.logical_not(new_group)
            fits = half if rows < MOE_TILE else jnp.logical_not(half)

            @pl.when(valid & first & fits)
            def _():
                run(load_weights, rows)

    @pl.when(jnp.logical_not(valid))
    def _():
        y_ref[...] = jnp.zeros_like(y_ref)


def _moe(xs, rs, tile_group, tile_rows, wg, wu, wd, layer):
    any_spec = pl.BlockSpec(memory_space=pl.ANY)
    grid_spec = pltpu.PrefetchScalarGridSpec(
        num_scalar_prefetch=2,
        grid=(MOE_ROWS // MOE_TILE,),
        in_specs=[pl.BlockSpec((MOE_TILE, D_MODEL), lambda i, tg, tv: (i, 0)),
                  pl.BlockSpec((MOE_TILE, LANES), lambda i, tg, tv: (i, 0)),
                  any_spec, any_spec, any_spec],
        out_specs=pl.BlockSpec((MOE_TILE, D_MODEL), lambda i, tg, tv: (i, 0)),
        scratch_shapes=[pltpu.VMEM((EXPERTS_PER_GROUP, D_MODEL, EXPERT_FF), BF16),
                        pltpu.VMEM((EXPERTS_PER_GROUP, D_MODEL, EXPERT_FF), BF16),
                        pltpu.VMEM((EXPERTS_PER_GROUP, EXPERT_FF, D_MODEL), BF16),
                        pltpu.VMEM((2, D_MODEL, EXPERT_FF), F32), pltpu.VMEM((2, D_MODEL, EXPERT_FF), F32),
                        pltpu.VMEM((2, EXPERT_FF, D_MODEL), F32), pltpu.SemaphoreType.DMA((2,))],
    )
    return pl.pallas_call(
        functools.partial(_moe_kernel, layer=layer),
        grid_spec=grid_spec,
        out_shape=jax.ShapeDtypeStruct((MOE_ROWS, D_MODEL), F32),
        compiler_params=_cparams("arbitrary"),
        name="moe",
    )(tile_group, tile_rows, xs, rs, wg, wu, wd)


def _combine_kernel(nblk, loc_blk, dst_blk, x_ref, lp_ref, mod_ref, ys_hbm, *refs, split):
    o_refs, ybuf, sem = refs[:-2], refs[-2], refs[-1]
    t = pl.program_id(0)
    slot = t % 2

    def fetch(tile, s):
        def copies(local, sorted_row):
            return (pltpu.make_async_copy(ys_hbm.at[pl.ds(sorted_row, SEG_BLK)], ybuf.at[s, pl.ds(local, SEG_BLK)], sem.at[s]),)
        ybuf[s] = jnp.zeros((LOCAL_ROWS, D_MODEL), F32)
        _segment_copies(tile, nblk, loc_blk, dst_blk, copies, lambda cp: cp.start())

    @pl.when(t == 0)
    def _():
        fetch(0, 0)

    @pl.when(t + 1 < N_TOK_TILES)
    def _():
        fetch(t + 1, 1 - slot)

    def copies_now(local, sorted_row):
        return (pltpu.make_async_copy(ys_hbm.at[pl.ds(sorted_row, SEG_BLK)], ybuf.at[slot, pl.ds(local, SEG_BLK)], sem.at[slot]),)

    _segment_copies(t, nblk, loc_blk, dst_blk, copies_now, lambda cp: cp.wait())

    onehot = (_iota((TOK_TILE, LOCAL_ROWS), 1) == lp_ref[...]).astype(BF16)
    ys = ybuf[slot]
    hi = ys.astype(BF16)
    lo = (ys - hi.astype(F32)).astype(BF16)
    y = _mm(onehot, hi) + _mm(onehot, lo)
    out = x_ref[...] + mod_ref[0, 5:6, :] * y
    if split:
        @pl.when(t < N_CTX_TILES)
        def _():
            o_refs[0][...] = out

        @pl.when(t >= N_CTX_TILES)
        def _():
            o_refs[1][...] = out
    else:
        o_refs[0][...] = out


def _combine(x, ys, mod, local_pos, nblk, loc_blk, dst_blk, split):
    tile = pl.BlockSpec((TOK_TILE, D_MODEL), lambda t, *_: (t, 0))
    if split:
        out_specs = _token_specs(True)
        out_shape = [jax.ShapeDtypeStruct((N_CTX, D_MODEL), F32), jax.ShapeDtypeStruct((N_LAT, D_MODEL), F32)]
    else:
        out_specs, out_shape = [tile], [jax.ShapeDtypeStruct((N_TOK, D_MODEL), F32)]
    grid_spec = pltpu.PrefetchScalarGridSpec(
        num_scalar_prefetch=3,
        grid=(N_TOK_TILES,),
        in_specs=[tile, pl.BlockSpec((TOK_TILE, 1), lambda t, *_: (t, 0)),
                  pl.BlockSpec((1, SUBLANES, D_MODEL), lambda t, *_: (_cond_of_tile(t), 0, 0)),
                  pl.BlockSpec(memory_space=pl.ANY)],
        out_specs=out_specs,
        scratch_shapes=[pltpu.VMEM((2, LOCAL_ROWS, D_MODEL), F32), pltpu.SemaphoreType.DMA((2,))],
    )
    return pl.pallas_call(
        functools.partial(_combine_kernel, split=split),
        grid_spec=grid_spec,
        out_shape=out_shape,
        compiler_params=_cparams("arbitrary"),
        name="combine",
    )(nblk, loc_blk, dst_blk, x, local_pos.reshape(N_TOK, 1), mod, ys)


def _lane_row(v):
    v = v.reshape(-1).astype(F32)
    return jnp.concatenate([v, jnp.zeros((LANES - v.shape[0],), F32)]).reshape(1, LANES)


def _pad_rows(w):
    return jnp.concatenate([w, jnp.zeros((SUBLANES - w.shape[0], w.shape[1]), w.dtype)], axis=0)


def kernel(x_prompt, x_sample, state_gdn, state_ret, cache_nat_k, cache_nat_v, c, c_ctx, ada_w, ada_b, norm_mix_w, norm_ffn_w, w_in, gdn_conv_w, gdn_a_log, gdn_dt_bias, gdn_norm_w, ret_gamma_logit, nat_q_norm_w, nat_k_norm_w, nat_rpb, sc_conv_w, w_out, router_group_w, router_group_b, router_expert_w, router_expert_b, moe_w_gate, moe_w_up, moe_w_down):
    x_ctx, x_lat, split = x_prompt.reshape(N_CTX, D_MODEL), x_sample.reshape(N_LAT, D_MODEL), True
    cond = jnp.concatenate([c_ctx[None, :], c], axis=0)
    ada = _ada(cond, ada_w, ada_b).reshape(DEPTH, SUBLANES, 6, D_MODEL)
    cos, sin = _rope_tables(DEC_SEQ)
    zero_state = jnp.zeros((BATCH, N_GATE, HEAD_DIM, HEAD_DIM), F32)
    lat_block = N_CTX // DEC_SEQ
    gdn_list, ret_list, caches = [], [], None
    mixed = [jnp.zeros((N_TOK, W_GROUP), F32) for _ in range(3)]
    for l in range(DEPTH):
        mod = jnp.concatenate([ada[l, :1 + DEC_BATCH], jnp.zeros((1 + DEC_BATCH, SUBLANES - 6, D_MODEL), F32)], axis=1)
        a_gdn, a_ret, a_nat, a_sc, a_gate = _inproj(x_ctx, x_lat, split, mod, norm_mix_w[l], w_in, l)

        conv_w = _pad_rows(gdn_conv_w[l])
        a_log, dt_b = _lane_row(gdn_a_log[l]), _lane_row(gdn_dt_bias[l])
        gnw = jnp.tile(gdn_norm_w[l], N_HEADS).reshape(1, W_GROUP)
        o_gdn, s_gdn = _gdn(a_gdn, a_gate, conv_w, a_log, dt_b, gnw, zero_state, SEQ, BATCH, 0, n_sub=2,
                            into=mixed[0])
        s0 = state_gdn[:, l].reshape(DEC_BATCH, N_GATE, HEAD_DIM, HEAD_DIM)
        o_gdn, _ = _gdn(a_gdn, a_gate, conv_w, a_log, dt_b, gnw, s0, DEC_SEQ, DEC_BATCH, lat_block, into=o_gdn)

        logit = jnp.repeat(ret_gamma_logit[l].astype(F32), HEAD_DIM, axis=1)
        o_ret, s_ret = _ret(a_ret, logit, zero_state, cos[:SEQ], sin[:SEQ], SEQ, BATCH, 0, False, into=mixed[1])
        s0 = state_ret[:, l].reshape(DEC_BATCH, N_GATE, HEAD_DIM, HEAD_DIM)
        o_ret, _ = _ret(a_ret, logit, s0, cos, sin, DEC_SEQ, DEC_BATCH, lat_block, True, into=o_ret)

        qw = jnp.tile(nat_q_norm_w[l], N_HEADS).reshape(1, W_GROUP)
        kw = jnp.tile(nat_k_norm_w[l], N_HEADS).reshape(1, W_GROUP)
        o_nat, *caches = _ctx_attn(a_nat, qw, kw, mixed[2], l, caches)
        o_nat = _nat(a_nat, cache_nat_k[:, l].reshape(DEC_BATCH, PAST_LEN, W_GROUP),
                     cache_nat_v[:, l].reshape(DEC_BATCH, PAST_LEN, W_GROUP), _nat_tables(nat_rpb[l]), qw, kw, o_nat)

        mixed = [o_gdn, o_ret, o_nat]
        rw, rb = _pack_router(router_expert_w[l], router_expert_b[l], router_group_w[l], router_group_b[l])
        x_mid, hf, route = _outproj(x_ctx, x_lat, split, mixed, a_sc, _pad_rows(sc_conv_w[l]), mod, norm_ffn_w[l],
                                    w_out, l, rw, rb)

        local_pos, nblk, loc_blk, dst_blk, tile_group, tile_rows = _moe_tables(route[:, GSEL_LANE].astype(jnp.int32))
        xs, rs = _dispatch(hf, route, local_pos, nblk, loc_blk, dst_blk)
        to_group = lambda w: w.reshape((DEPTH * N_GROUPS, EXPERTS_PER_GROUP) + w.shape[2:])
        ys = _moe(xs, rs, tile_group, tile_rows, to_group(moe_w_gate), to_group(moe_w_up), to_group(moe_w_down), l)
        last = l == DEPTH - 1
        out = _combine(x_mid, ys, mod, local_pos, nblk, loc_blk, dst_blk, split=last)
        x_ctx, x_lat, split = (out[0], out[1], True) if last else (out[0], out[0], False)

        gdn_list.append(s_gdn.reshape(BATCH, 2, N_HEADS, HEAD_DIM, HEAD_DIM))
        ret_list.append(s_ret.reshape(BATCH, 2, N_HEADS, HEAD_DIM, HEAD_DIM))
    new_k, new_v = [a.reshape(BATCH, DEPTH, N_HEADS, HEAD_DIM, SEQ).transpose(0, 1, 4, 2, 3) for a in caches]
    return (x_ctx.reshape(BATCH, SEQ, D_MODEL), x_lat.reshape(DEC_BATCH, DEC_SEQ, D_MODEL),
            jnp.stack(gdn_list, axis=1), jnp.stack(ret_list, axis=1), new_k, new_v)
```

```python
import functools

import numpy as np
import jax
import jax.numpy as jnp
from jax import lax
from jax.experimental import pallas as pl
from jax.experimental.pallas import tpu as pltpu

D_MODEL = 1024
BATCH = 16
SEQ = 256
DEPTH = 2
DEC_BATCH = 2
DEC_SEQ = 1024
PAST_LEN = 256
GRID_W = 64
HEAD_DIM = 64
W_GROUP = D_MODEL // 4
N_HEADS = W_GROUP // HEAD_DIM
CHUNK = 64
WIN_ROWS = 8
WIN_COLS = 16
ROPE_BASE = 10000.0
N_GROUPS = 4
EXPERTS_PER_GROUP = 8
N_EXPERTS = N_GROUPS * EXPERTS_PER_GROUP
EXPERT_FF = 256
EPS = 1e-6

N_CTX = BATCH * SEQ
N_LAT = DEC_BATCH * DEC_SEQ
N_TOK = N_CTX + N_LAT
LANES = 128
SUBLANES = 8
TOK_TILE = 512
MOE_TILE = 512
Q_TILE = 256
VMEM_LIMIT = 48 * 1024 * 1024
NEG_BIG = -1e30
N_GATE = 2 * N_HEADS

F32 = jnp.float32
BF16 = jnp.bfloat16
HI = lax.Precision.HIGHEST


def _mm(a, b, prec=None):
    return lax.dot_general(a, b, (((1,), (0,)), ((), ())), precision=prec, preferred_element_type=F32)


def _mm_tn(a, b, prec=None):
    return lax.dot_general(a, b, (((0,), (0,)), ((), ())), precision=prec, preferred_element_type=F32)


def _bmm(a, b):
    return _mm(a.astype(BF16), b.astype(BF16))


def _sigmoid(x):
    return 1.0 / (1.0 + jnp.exp(-x))


def _silu(x):
    return x * _sigmoid(x)


def _softplus(x):
    return jnp.maximum(x, 0.0) + jnp.log(1.0 + jnp.exp(-jnp.abs(x)))


def _iota(shape, dim):
    return lax.broadcasted_iota(jnp.int32, shape, dim)


def _cparams(*sem):
    return pltpu.CompilerParams(dimension_semantics=sem, vmem_limit_bytes=VMEM_LIMIT)


def _cond_of_tile(i):
    n_ctx_tiles = N_CTX // TOK_TILE
    return jnp.where(i < n_ctx_tiles, 0, 1 + (i - n_ctx_tiles) // (DEC_SEQ // TOK_TILE))


def _head_sum_matrix():
    return (_iota((W_GROUP, W_GROUP), 0) // HEAD_DIM == _iota((W_GROUP, W_GROUP), 1) // HEAD_DIM).astype(BF16)


def _head_sums(x, head_sum):
    hi = x.astype(BF16)
    lo = (x - hi.astype(F32)).astype(BF16)
    return _mm(hi, head_sum) + _mm(lo, head_sum)


N_COND = 1 + DEC_BATCH
ADA_TN = 1536


def _ada_kernel(c_ref, w_ref, b_ref, o_ref):
    def slab(s, accs):
        rows = pl.ds(pl.multiple_of(s * SUBLANES, SUBLANES), SUBLANES)
        w = w_ref[0, rows, :]
        return tuple(acc + w * jnp.tile(_silu(c_ref[r, rows, :]), (1, ADA_TN // LANES))
                     for r, acc in enumerate(accs))

    zero = jnp.zeros((SUBLANES, ADA_TN), F32)
    accs = lax.fori_loop(0, D_MODEL // SUBLANES, slab, (zero,) * N_COND, unroll=4)
    out = jnp.concatenate([jnp.sum(acc, axis=0, keepdims=True) for acc in accs]
                          + [jnp.zeros((SUBLANES - N_COND, ADA_TN), F32)], axis=0)
    o_ref[0] = out + b_ref[0]


def _ada(cond, ada_w, ada_b):
    n_out = 6 * D_MODEL
    cond_lanes = jnp.broadcast_to(cond[:, :, None], (N_COND, D_MODEL, LANES))
    return pl.pallas_call(
        _ada_kernel,
        grid=(DEPTH, n_out // ADA_TN),
        in_specs=[pl.BlockSpec((N_COND, D_MODEL, LANES), lambda l, j: (0, 0, 0)),
                  pl.BlockSpec((1, D_MODEL, ADA_TN), lambda l, j: (l, 0, j)),
                  pl.BlockSpec((1, 1, ADA_TN), lambda l, j: (l, 0, j))],
        out_specs=pl.BlockSpec((1, SUBLANES, ADA_TN), lambda l, j: (l, 0, j)),
        out_shape=jax.ShapeDtypeStruct((DEPTH, SUBLANES, n_out), F32),
        compiler_params=_cparams("arbitrary", "arbitrary"),
        name="ada",
    )(cond_lanes, ada_w, ada_b.reshape(DEPTH, 1, n_out))


IN_WIDTHS = (4 * W_GROUP, 4 * W_GROUP, 3 * W_GROUP, 3 * W_GROUP, LANES)
IN_PACKED = sum(IN_WIDTHS)


IN_TOTAL = 3 * W_GROUP + W_GROUP + 2 * N_GATE + 4 * W_GROUP + 3 * W_GROUP + 3 * W_GROUP
IN_GATE_SRC = 4 * W_GROUP
IN_SRC = (0, IN_GATE_SRC + 2 * N_GATE, IN_GATE_SRC + 2 * N_GATE + 4 * W_GROUP,
          IN_GATE_SRC + 2 * N_GATE + 7 * W_GROUP)
N_CTX_TILES = N_CTX // TOK_TILE


def _token_specs(split):
    lat0 = 0 if split else N_CTX_TILES
    return [pl.BlockSpec((TOK_TILE, D_MODEL), lambda i, *_: (jnp.minimum(i, N_CTX_TILES - 1), 0)),
            pl.BlockSpec((TOK_TILE, D_MODEL), lambda i, *_: (jnp.maximum(i, N_CTX_TILES) - N_CTX_TILES + lat0, 0))]


def _token_tile(i, ctx_ref, lat_ref):
    return jnp.where(i < N_CTX_TILES, ctx_ref[...], lat_ref[...])


def _inproj_kernel(xc_ref, xl_ref, mod_ref, nw_ref, w_ref, *refs):
    o_refs, w_s = refs[:-1], refs[-1]
    i = pl.program_id(0)

    @pl.when(i == 0)
    def _():
        piece = 256
        off = 0
        for src, width in zip(IN_SRC, IN_WIDTHS[:-1]):
            for c in range(0, width, piece):
                w_s[:, off + c:off + c + piece] = w_ref[0, src + c:src + c + piece, :].T.astype(BF16)
            off += width
        gate = w_ref[0, IN_GATE_SRC:IN_GATE_SRC + LANES, :].T
        w_s[:, off:] = jnp.where(_iota((D_MODEL, LANES), 1) < 2 * N_GATE, gate, 0.0).astype(BF16)

    x = _token_tile(i, xc_ref, xl_ref)
    y = x * lax.rsqrt(jnp.mean(x * x, axis=-1, keepdims=True) + EPS) * nw_ref[...]
    h = (y * (1.0 + mod_ref[0, 1:2, :]) + mod_ref[0, 0:1, :]).astype(BF16)
    off = 0
    for o_ref, width in zip(o_refs, IN_WIDTHS):
        o_ref[...] = _mm(h, w_s[:, off:off + width])
        off += width


def _inproj(x_ctx, x_lat, split, mod, norm_w, w_in, layer):
    return pl.pallas_call(
        _inproj_kernel,
        grid=(N_TOK // TOK_TILE,),
        in_specs=_token_specs(split)
                 + [pl.BlockSpec((1, SUBLANES, D_MODEL), lambda i: (_cond_of_tile(i), 0, 0)),
                    pl.BlockSpec((1, D_MODEL), lambda i: (0, 0)),
                    pl.BlockSpec((1, IN_TOTAL, D_MODEL), lambda i: (layer, 0, 0), pipeline_mode=pl.Buffered(1))],
        out_specs=[pl.BlockSpec((TOK_TILE, w), lambda i: (i, 0)) for w in IN_WIDTHS],
        out_shape=[jax.ShapeDtypeStruct((N_TOK, w), F32) for w in IN_WIDTHS],
        scratch_shapes=[pltpu.VMEM((D_MODEL, IN_PACKED), BF16)],
        compiler_params=_cparams("arbitrary"),
        name="inproj",
    )(x_ctx, x_lat, mod, norm_w.reshape(1, D_MODEL), jnp.swapaxes(w_in, 1, 2))


def _seq_call(kernel_fn, args, in_specs, out_specs, out_shape, into, more_into=(), **kwargs):
    donors = ([] if into is None else [(into, 0)]) + list(more_into)
    n_in = len(args)
    inner = kernel_fn
    kernel_fn = lambda *refs: inner(*refs[:n_in], *refs[n_in + len(donors):])
    aliases = {n_in + j: out_idx for j, (_, out_idx) in enumerate(donors)}
    args = list(args) + [a for a, _ in donors]
    in_specs = list(in_specs) + [pl.BlockSpec(memory_space=pl.ANY)] * len(donors)
    return pl.pallas_call(kernel_fn, in_specs=in_specs, out_specs=out_specs, out_shape=out_shape,
                          input_output_aliases=aliases, **kwargs)(*args)


def _shift_rows(p, seq_len):
    rows = p.shape[0]
    pos = _iota(p.shape, 0) & (seq_len - 1)
    prev = jnp.where(pos == 0, 0.0, pltpu.roll(p, 1, 0))
    nxt = jnp.where(pos == seq_len - 1, 0.0, pltpu.roll(p, rows - 1, 0))
    return prev, nxt


def _conv3(x, w_ref, seq_len):
    prev, nxt = _shift_rows(x, seq_len)
    return w_ref[0:1, :] * prev + w_ref[1:2, :] * x + w_ref[2:3, :] * nxt


def _chunk_scan(x, reverse):
    t = x.shape[0]
    pos = _iota(x.shape, 0) % CHUNK
    step = 1
    while step < CHUNK:
        if reverse:
            x = x + jnp.where(pos < CHUNK - step, pltpu.roll(x, t - step, 0), 0.0)
        else:
            x = x + jnp.where(pos >= step, pltpu.roll(x, step, 0), 0.0)
        step *= 2
    return x


GDN_GROUP_CHUNKS = 4
GDN_BASE_BLOCK = 4
GDN_CHAINS = GDN_GROUP_CHUNKS * N_GATE
GDN_PAIRS = GDN_GROUP_CHUNKS * N_HEADS


def _gdn_kernel(a_ref, gate_ref, convw_ref, alog_ref, dtb_ref, nw_ref, s0_ref, o_ref, sfin_ref,
                q_s, kv_s, kt_s, gc_s, eg_s, beta_s, gcrow_s, ekdrow_s, cdec_s, uo_s, wq_s, attn_s, kdt_s,
                st_s, wsqs_s, kk_s, d_s, m1_s, p_s, low_s, pb_s, rhs_s, *, t, n_sub):
    n_chunks = t // CHUNK
    a = a_ref[...]
    qkv = _silu(_conv3(a[:, :3 * W_GROUP], convw_ref, t // n_sub))
    q = qkv[:, :W_GROUP]
    k = qkv[:, W_GROUP:2 * W_GROUP]
    v = qkv[:, 2 * W_GROUP:]
    head_sum = _head_sum_matrix()
    q = q * lax.rsqrt(_head_sums(q * q, head_sum) + EPS) * (HEAD_DIM ** -0.5)
    k = k * lax.rsqrt(_head_sums(k * k, head_sum) + EPS)
    for h in range(N_HEADS):
        hs = slice(h * HEAD_DIM, (h + 1) * HEAD_DIM)
        q_s[h] = q[:, hs]
        kv_s[h] = jnp.concatenate([k[:, hs], v[:, hs]], axis=1)
    k_t = k.T
    for c in range(n_chunks):
        kt_s[c] = k_t[:, c * CHUNK:(c + 1) * CHUNK]

    gates = gate_ref[...]
    log_a = -jnp.exp(alog_ref[...]) * _softplus(gates + dtb_ref[...])
    beta_s[...] = _sigmoid(gates)

    ci = _iota((CHUNK, CHUNK), 0)
    cj = _iota((CHUNK, CHUNK), 1)
    eye = (ci == cj).astype(F32)
    blk_mask = (ci // GDN_BASE_BLOCK) == (cj // GDN_BASE_BLOCK)
    low_half = _iota((CHUNK, 2 * HEAD_DIM), 1) < HEAD_DIM

    prefix = _chunk_scan(log_a, reverse=False)
    suffix = _chunk_scan(log_a, reverse=True)
    gc = jnp.where(_iota((t, LANES), 1) < N_HEADS, prefix, suffix)
    gt = prefix + suffix - log_a
    gc_s[...] = gc
    eg_s[...] = jnp.exp(gc)
    gc_t = gc.T
    ekd_t = jnp.exp(gt - gc).T
    cdec_t = jnp.exp(gt).T
    for c in range(n_chunks):
        lanes = slice(c * CHUNK, (c + 1) * CHUNK)
        gcrow_s[c] = gc_t[:N_GATE, lanes]
        ekdrow_s[c] = ekd_t[:N_GATE, lanes]
        cdec_s[c] = jnp.concatenate([cdec_t[:N_GATE, lanes]] * 2, axis=1)

    def solve_group(grp, carry):
        row0 = grp * (GDN_GROUP_CHUNKS * CHUNK)
        chains = [(cl, a_idx) for cl in range(GDN_GROUP_CHUNKS) for a_idx in range(N_GATE)]

        def rows_of(cl):
            return pl.ds(pl.multiple_of(row0 + cl * CHUNK, CHUNK), CHUNK)

        for cl in range(GDN_GROUP_CHUNKS):
            for h in range(N_HEADS):
                rows = rows_of(cl)
                kq = jnp.concatenate([kv_s[h, rows, :HEAD_DIM], q_s[h, rows, :]], axis=0)
                k_t_h = kt_s[grp * GDN_GROUP_CHUNKS + cl, h * HEAD_DIM:(h + 1) * HEAD_DIM, :]
                kk_s[cl * N_HEADS + h] = _bmm(kq, k_t_h)
        for b, (cl, a_idx) in enumerate(chains):
            backward = a_idx >= N_HEADS
            h = a_idx % N_HEADS
            rows, c = rows_of(cl), grp * GDN_GROUP_CHUNKS + cl
            incl = (cj >= ci) if backward else (cj <= ci)
            strict = (cj > ci) if backward else (cj < ci)
            bt = beta_s[rows, N_GATE + a_idx:N_GATE + a_idx + 1]
            decay = jnp.exp(jnp.where(incl, gc_s[rows, a_idx:a_idx + 1] - gcrow_s[c, a_idx:a_idx + 1, :], NEG_BIG))
            low = jnp.where(strict, kk_s[cl * N_HEADS + h, :CHUNK, :] * bt * decay, 0.0)
            attn_s[a_idx, rows, :] = (kk_s[cl * N_HEADS + h, CHUNK:, :] * decay).astype(BF16)
            d_s[b] = jnp.where(blk_mask, low, 0.0)
            low_s[b] = low.astype(BF16)
            rhs_s[b] = (kv_s[h, rows, :] * bt
                        * jnp.where(low_half, eg_s[rows, a_idx:a_idx + 1], 1.0)).astype(BF16)
        for b in range(GDN_CHAINS):
            m1_s[b] = _bmm(d_s[b], d_s[b])
        for b in range(GDN_CHAINS):
            d, d2 = d_s[b], m1_s[b]
            p_s[b] = eye - d + d2 - _bmm(d, d2)
        size = GDN_BASE_BLOCK
        while size < CHUNK:
            pair = ((ci // size) != (cj // size)) & ((ci // (2 * size)) == (cj // (2 * size)))
            for b in range(GDN_CHAINS):
                coupling = jnp.where(pair, low_s[b], jnp.zeros((), BF16))
                pb_s[b] = _mm(p_s[b].astype(BF16), coupling).astype(BF16)
            for b in range(GDN_CHAINS):
                p = p_s[b]
                p_s[b] = p - _mm(pb_s[b], p.astype(BF16))
            size *= 2
        for b, (cl, a_idx) in enumerate(chains):
            h = a_idx % N_HEADS
            rows, c = rows_of(cl), grp * GDN_GROUP_CHUNKS + cl
            wu = _mm(p_s[b].astype(BF16), rhs_s[b])
            uo_s[a_idx, rows, :] = wu
            wq_s[a_idx, c, :CHUNK, :] = wu[:, :HEAD_DIM].astype(BF16)
            wq_s[a_idx, c, CHUNK:, :] = (q_s[h, rows, :] * eg_s[rows, a_idx:a_idx + 1]).astype(BF16)
            k_t_h = kt_s[c, h * HEAD_DIM:(h + 1) * HEAD_DIM, :]
            kdt_s[a_idx, c] = (k_t_h * ekdrow_s[c, a_idx:a_idx + 1, :]).astype(BF16)
        return carry

    lax.fori_loop(0, n_chunks // GDN_GROUP_CHUNKS, solve_group, 0)

    seq_chunks = n_chunks // n_sub
    chains = [(sub, a_idx) for sub in range(n_sub) for a_idx in range(N_GATE)]
    for j, (sub, a_idx) in enumerate(chains):
        st_s[j] = jnp.concatenate([jnp.zeros((HEAD_DIM, HEAD_DIM), F32), s0_ref[sub, a_idx]], axis=1)

    def scan_chunk(c, carry):
        def chunk_of(sub, a_idx):
            return sub * seq_chunks + ((seq_chunks - 1 - c) if a_idx >= N_HEADS else c)

        for j, (sub, a_idx) in enumerate(chains):
            wsqs_s[j] = _mm(wq_s[a_idx, chunk_of(sub, a_idx)], st_s[j].astype(BF16))
        for j, (sub, a_idx) in enumerate(chains):
            cc = chunk_of(sub, a_idx)
            rows = pl.ds(pl.multiple_of(cc * CHUNK, CHUNK), CHUNK)
            v_new = (uo_s[a_idx, rows, :] - wsqs_s[j, :CHUNK, :]).astype(BF16)
            uo_s[a_idx, rows, :] = wsqs_s[j, CHUNK:, :] + _mm(attn_s[a_idx, rows, :], v_new)
            st_s[j] = st_s[j] * cdec_s[cc, a_idx:a_idx + 1, :] + _mm(kdt_s[a_idx, cc], v_new)
        return carry

    lax.fori_loop(0, seq_chunks, scan_chunk, 0)
    for j, (sub, a_idx) in enumerate(chains):
        sfin_ref[sub, a_idx] = st_s[j, :, HEAD_DIM:]

    o = jnp.concatenate([(uo_s[h] + uo_s[N_HEADS + h])[:, HEAD_DIM:] for h in range(N_HEADS)], axis=1)
    ms = _head_sums(o * o, head_sum) * (1.0 / HEAD_DIM)
    o_ref[...] = o * lax.rsqrt(ms + EPS) * nw_ref[...] * _silu(a[:, 3 * W_GROUP:])


def _gdn(a_all, gate_all, conv_w, a_log, dt_bias, norm_w, s0, seq_len, n_seq, first_block, n_sub=1, into=None):
    small = lambda: pl.BlockSpec((1, LANES), lambda i: (0, 0))
    t = n_sub * seq_len
    n_chunks = t // CHUNK
    wide = 2 * HEAD_DIM
    scratch = [pltpu.VMEM((N_HEADS, t, HEAD_DIM), F32),
               pltpu.VMEM((N_HEADS, t, wide), F32),
               pltpu.VMEM((n_chunks, W_GROUP, CHUNK), F32),
               pltpu.VMEM((t, LANES), F32), pltpu.VMEM((t, LANES), F32), pltpu.VMEM((t, LANES), F32),
               pltpu.VMEM((n_chunks, N_GATE, CHUNK), F32), pltpu.VMEM((n_chunks, N_GATE, CHUNK), F32),
               pltpu.VMEM((n_chunks, N_GATE, wide), F32),
               pltpu.VMEM((N_GATE, t, wide), F32),
               pltpu.VMEM((N_GATE, n_chunks, 2 * CHUNK, HEAD_DIM), BF16),
               pltpu.VMEM((N_GATE, t, CHUNK), BF16),
               pltpu.VMEM((N_GATE, n_chunks, HEAD_DIM, CHUNK), BF16),
               pltpu.VMEM((n_sub * N_GATE, HEAD_DIM, wide), F32),
               pltpu.VMEM((n_sub * N_GATE, 2 * CHUNK, wide), F32),
               pltpu.VMEM((GDN_PAIRS, 2 * CHUNK, CHUNK), F32)]
    scratch += [pltpu.VMEM((GDN_CHAINS, CHUNK, CHUNK), F32)] * 3
    scratch += [pltpu.VMEM((GDN_CHAINS, CHUNK, CHUNK), BF16)] * 2
    scratch += [pltpu.VMEM((GDN_CHAINS, CHUNK, wide), BF16)]
    return _seq_call(
        functools.partial(_gdn_kernel, t=t, n_sub=n_sub), (a_all, gate_all, conv_w, a_log, dt_bias, norm_w, s0),
        in_specs=[pl.BlockSpec((t, 4 * W_GROUP), lambda i: (i + first_block, 0)),
                  pl.BlockSpec((t, LANES), lambda i: (i + first_block, 0)),
                  pl.BlockSpec((SUBLANES, 3 * W_GROUP), lambda i: (0, 0)),
                  small(), small(),
                  pl.BlockSpec((1, W_GROUP), lambda i: (0, 0)),
                  pl.BlockSpec((n_sub, N_GATE, HEAD_DIM, HEAD_DIM), lambda i: (i, 0, 0, 0))],
        out_specs=[pl.BlockSpec((t, W_GROUP), lambda i: (i + first_block, 0)),
                   pl.BlockSpec((n_sub, N_GATE, HEAD_DIM, HEAD_DIM), lambda i: (i, 0, 0, 0))],
        out_shape=[jax.ShapeDtypeStruct((N_TOK, W_GROUP), F32),
                   jax.ShapeDtypeStruct((n_seq, N_GATE, HEAD_DIM, HEAD_DIM), F32)],
        scratch_shapes=scratch,
        into=into, grid=(n_seq // n_sub,), compiler_params=_cparams("arbitrary"), name="gdn")


def _swap16(x):
    width = x.shape[-1]
    first = (_iota(x.shape, 1) // 16) % 2 == 0
    return jnp.where(first, pltpu.roll(x, width - 16, 1), pltpu.roll(x, 16, 1))


def _block_diag_heads(s0_ref, first):
    zero = jnp.zeros((HEAD_DIM, HEAD_DIM), F32)
    return jnp.concatenate(
        [jnp.concatenate([s0_ref[0, first + h] if j == h else zero for j in range(N_HEADS)], axis=1)
         for h in range(N_HEADS)], axis=0)


def _ret_kernel(r_ref, lg_ref, s0_ref, cos_ref, sin_ref, o_ref, sfin_ref, *, t, latent):
    r = r_ref[...]
    q = r[:, :W_GROUP]
    k = r[:, W_GROUP:2 * W_GROUP]
    v = r[:, 2 * W_GROUP:3 * W_GROUP]
    if latent:
        q = q * cos_ref[...] + _swap16(q) * sin_ref[...]
        k = k * cos_ref[...] + _swap16(k) * sin_ref[...]
    k = k * (HEAD_DIM ** -0.5)
    lg = -_softplus(-lg_ref[...])
    lgf, lgb = lg[0:1, :], lg[1:2, :]
    head = _iota((1, W_GROUP), 1) // HEAD_DIM
    head_sum = _head_sum_matrix()
    pos = _iota((t, 1), 0).astype(F32)
    q_b = q.astype(BF16)
    kt_b = k.T.astype(BF16)
    v_heads = [jnp.where(head == h, v, 0.0).astype(BF16) for h in range(N_HEADS)]
    if latent:
        s0f = _block_diag_heads(s0_ref, 0)
        s0b = _block_diag_heads(s0_ref, N_HEADS)
    for qt in range(t // Q_TILE):
        rows = slice(qt * Q_TILE, (qt + 1) * Q_TILE)
        diff = (_iota((Q_TILE, t), 0) + qt * Q_TILE - _iota((Q_TILE, t), 1)).astype(F32)
        both = jnp.where(diff == 0, 2.0, 1.0)
        o = jnp.zeros((Q_TILE, W_GROUP), F32)
        for h in range(N_HEADS):
            lgf_h = lgf[:, h * HEAD_DIM:h * HEAD_DIM + 1]
            lgb_h = lgb[:, h * HEAD_DIM:h * HEAD_DIM + 1]
            dmat = jnp.exp(diff * jnp.where(diff >= 0, lgf_h, -lgb_h)) * both
            s = _mm(jnp.where(head == h, q_b[rows], 0.0), kt_b) * dmat
            o = o + _mm(s.astype(BF16), v_heads[h])
        if latent:
            p = pos[rows]
            o = o + jnp.exp((p + 1.0) * lgf) * _bmm(q_b[rows], s0f) + jnp.exp((t - p) * lgb) * _bmm(q_b[rows], s0b)
        oc = o - _head_sums(o, head_sum) * (1.0 / HEAD_DIM)
        on = oc * lax.rsqrt(_head_sums(oc * oc, head_sum) * (1.0 / HEAD_DIM) + EPS)
        o_ref[rows, :] = on * _silu(r[rows, 3 * W_GROUP:])
    v_b = v.astype(BF16)
    sf = _mm_tn((k * jnp.exp((t - 1.0 - pos) * lgf)).astype(BF16), v_b)
    sb = _mm_tn((k * jnp.exp(pos * lgb)).astype(BF16), v_b)
    for h in range(N_HEADS):
        hs = slice(h * HEAD_DIM, (h + 1) * HEAD_DIM)
        sf_h, sb_h = sf[hs, hs], sb[hs, hs]
        if latent:
            sf_h = sf_h + jnp.exp(t * lgf[:, h * HEAD_DIM:h * HEAD_DIM + 1]) * s0_ref[0, h]
            sb_h = sb_h + jnp.exp(t * lgb[:, h * HEAD_DIM:h * HEAD_DIM + 1]) * s0_ref[0, N_HEADS + h]
        sfin_ref[0, h] = sf_h
        sfin_ref[0, N_HEADS + h] = sb_h


def _ret(r_all, logit, s0, cos, sin, t, n_seq, first_block, latent, into=None):
    return _seq_call(
        functools.partial(_ret_kernel, t=t, latent=latent), (r_all, logit, s0, cos, sin),
        in_specs=[pl.BlockSpec((t, 4 * W_GROUP), lambda i: (i + first_block, 0)),
                  pl.BlockSpec((2, W_GROUP), lambda i: (0, 0)),
                  pl.BlockSpec((1, N_GATE, HEAD_DIM, HEAD_DIM), lambda i: (i, 0, 0, 0)),
                  pl.BlockSpec((t, W_GROUP), lambda i: (0, 0)),
                  pl.BlockSpec((t, W_GROUP), lambda i: (0, 0))],
        out_specs=[pl.BlockSpec((t, W_GROUP), lambda i: (i + first_block, 0)),
                   pl.BlockSpec((1, N_GATE, HEAD_DIM, HEAD_DIM), lambda i: (i, 0, 0, 0))],
        out_shape=[jax.ShapeDtypeStruct((N_TOK, W_GROUP), F32),
                   jax.ShapeDtypeStruct((n_seq, N_GATE, HEAD_DIM, HEAD_DIM), F32)],
        into=into, grid=(n_seq,), compiler_params=_cparams("arbitrary"), name="ret")


def _rope_tables(t):
    pos = np.arange(t)
    row = (pos // GRID_W).astype(np.float32)
    col = (pos % GRID_W).astype(np.float32)
    nf = HEAD_DIM // 4
    inv_freq = jnp.power(ROPE_BASE, -jnp.arange(nf, dtype=F32) / nf)
    ang_r = jnp.asarray(row)[:, None] * inv_freq[None, :]
    ang_c = jnp.asarray(col)[:, None] * inv_freq[None, :]
    cos = jnp.concatenate([jnp.cos(ang_r)] * 2 + [jnp.cos(ang_c)] * 2, axis=1)
    sin = jnp.concatenate([-jnp.sin(ang_r), jnp.sin(ang_r), -jnp.sin(ang_c), jnp.sin(ang_c)], axis=1)
    return jnp.tile(cos, (1, N_HEADS)), jnp.tile(sin, (1, N_HEADS))


def _head_rms(x, w, head_sum):
    return x * lax.rsqrt(_head_sums(x * x, head_sum) * (1.0 / HEAD_DIM) + EPS) * w


def _ctx_attn_kernel(n_ref, qw_ref, kw_ref, o_ref, k_out, v_out, *, n_fill):
    n = n_ref[...]
    head_sum = _head_sum_matrix()
    head = _iota((1, W_GROUP), 1) // HEAD_DIM
    q = (_head_rms(n[:, :W_GROUP], qw_ref[...], head_sum) * (HEAD_DIM ** -0.5)).astype(BF16)
    k = _head_rms(n[:, W_GROUP:2 * W_GROUP], kw_ref[...], head_sum)
    v = n[:, 2 * W_GROUP:]
    k_t = k.T
    k_out[0, 0] = k_t
    v_out[0, 0] = v.T
    for later in range(1, 1 + n_fill):
        k_out[0, later] = jnp.zeros((W_GROUP, SEQ), F32)
        v_out[0, later] = jnp.zeros((W_GROUP, SEQ), F32)
    kt_b = k_t.astype(BF16)
    o = jnp.zeros((SEQ, W_GROUP), F32)
    for h in range(N_HEADS):
        s = _mm(jnp.where(head == h, q, 0.0), kt_b)
        p = jnp.exp(s - jnp.max(s, axis=-1, keepdims=True))
        o = o + _mm(p.astype(BF16), jnp.where(head == h, v, 0.0).astype(BF16)) / jnp.sum(p, axis=-1, keepdims=True)
    o_ref[...] = o


def _ctx_attn(n_all, qw, kw, into, layer, caches):
    slots = DEPTH if caches is None else 1
    cache_spec = pl.BlockSpec((1, slots, W_GROUP, SEQ), lambda i: (i, layer if caches is not None else 0, 0, 0))
    cache_shape = jax.ShapeDtypeStruct((BATCH, DEPTH, W_GROUP, SEQ), F32)
    return _seq_call(
        functools.partial(_ctx_attn_kernel, n_fill=slots - 1), (n_all, qw, kw),
        in_specs=[pl.BlockSpec((SEQ, 3 * W_GROUP), lambda i: (i, 0)),
                  pl.BlockSpec((1, W_GROUP), lambda i: (0, 0)),
                  pl.BlockSpec((1, W_GROUP), lambda i: (0, 0))],
        out_specs=[pl.BlockSpec((SEQ, W_GROUP), lambda i: (i, 0)), cache_spec, cache_spec],
        out_shape=[jax.ShapeDtypeStruct((N_TOK, W_GROUP), F32), cache_shape, cache_shape],
        into=into, more_into=() if caches is None else ((caches[0], 1), (caches[1], 2)),
        grid=(BATCH,), compiler_params=_cparams("arbitrary"), name="ctx_attn")


def _nat_kernel(n_ref, ck_ref, cv_ref, bias_ref, qw_ref, kw_ref, o_ref, q_s, kt_s, ckt_s):
    h = pl.program_id(1)

    @pl.when(h == 0)
    def _():
        head_sum = _head_sum_matrix()
        n = n_ref[...]
        q_s[...] = (_head_rms(n[:, :W_GROUP], qw_ref[...], head_sum) * (HEAD_DIM ** -0.5)).astype(BF16)
        kt_s[...] = _head_rms(n[:, W_GROUP:2 * W_GROUP], kw_ref[...], head_sum).T.astype(BF16)
        ckt_s[...] = ck_ref[0].T.astype(BF16)
        o_ref[...] = jnp.zeros_like(o_ref)

    mine = _iota((1, W_GROUP), 1) // HEAD_DIM == h
    v = jnp.where(mine, n_ref[:, 2 * W_GROUP:], 0.0).astype(BF16)
    cv = jnp.where(mine, cv_ref[0], 0.0).astype(BF16)
    for qt in range(DEC_SEQ // Q_TILE):
        rows = slice(qt * Q_TILE, (qt + 1) * Q_TILE)
        q = jnp.where(mine, q_s[rows, :], 0.0)
        grid_rows = range(qt * Q_TILE // GRID_W, (qt + 1) * Q_TILE // GRID_W)
        s_loc = _mm(q, kt_s[...]) + jnp.concatenate([_nat_bias_strip(bias_ref, row) for row in grid_rows], axis=0)
        s_ctx = _mm(q, ckt_s[...])
        m = jnp.maximum(jnp.max(s_loc, axis=-1, keepdims=True), jnp.max(s_ctx, axis=-1, keepdims=True))
        p_loc = jnp.exp(s_loc - m)
        p_ctx = jnp.exp(s_ctx - m)
        den = jnp.sum(p_loc, axis=-1, keepdims=True) + jnp.sum(p_ctx, axis=-1, keepdims=True)
        o_ref[rows, :] += (_mm(p_loc.astype(BF16), v) + _mm(p_ctx.astype(BF16), cv)) / den


def _nat(n_all, ck, cv, bias, qw, kw, into):
    first_block = N_CTX // DEC_SEQ
    return _seq_call(
        _nat_kernel, (n_all, ck, cv, bias, qw, kw),
        in_specs=[pl.BlockSpec((DEC_SEQ, 3 * W_GROUP), lambda b, h: (b + first_block, 0)),
                  pl.BlockSpec((1, PAST_LEN, W_GROUP), lambda b, h: (b, 0, 0)),
                  pl.BlockSpec((1, PAST_LEN, W_GROUP), lambda b, h: (b, 0, 0)),
                  pl.BlockSpec((1, 3 * N_ROW_OFF - 1, GRID_W, 2 * GRID_W), lambda b, h: (h, 0, 0, 0)),
                  pl.BlockSpec((1, W_GROUP), lambda b, h: (0, 0)),
                  pl.BlockSpec((1, W_GROUP), lambda b, h: (0, 0))],
        out_specs=pl.BlockSpec((DEC_SEQ, W_GROUP), lambda b, h: (b + first_block, 0)),
        out_shape=jax.ShapeDtypeStruct((N_TOK, W_GROUP), F32),
        scratch_shapes=[pltpu.VMEM((DEC_SEQ, W_GROUP), BF16), pltpu.VMEM((W_GROUP, DEC_SEQ), BF16),
                        pltpu.VMEM((W_GROUP, PAST_LEN), BF16)],
        into=into, grid=(DEC_BATCH, N_HEADS), compiler_params=_cparams("arbitrary", "arbitrary"), name="nat")


N_ROW_OFF = 2 * WIN_ROWS - 1
NAT_ROWS = DEC_SEQ // GRID_W
NAT_KH = min(WIN_ROWS, NAT_ROWS)
NAT_PAIR, NAT_LOW, NAT_HIGH = 0, N_ROW_OFF - 1, 2 * N_ROW_OFF - 1


def _nat_tables(rpb):
    c = np.arange(GRID_W)
    c0 = np.clip(c - WIN_COLS // 2, 0, GRID_W - WIN_COLS)
    col_in = (c[None, :] >= c0[:, None]) & (c[None, :] < c0[:, None] + WIN_COLS)
    col_idx = np.clip(c[None, :] - c[:, None], -(WIN_COLS - 1), WIN_COLS - 1) + WIN_COLS - 1
    col_hot = (col_idx[..., None] == np.arange(2 * WIN_COLS - 1)).astype(np.float32)
    tz = jnp.where(col_in, jnp.einsum('hab,qkb->haqk', rpb.astype(F32), col_hot, precision=HI), NEG_BIG)
    neg = jnp.full_like(tz, NEG_BIG)
    return jnp.concatenate([jnp.concatenate([tz[:, :-1], tz[:, 1:]], axis=-1),
                            jnp.concatenate([tz, neg], axis=-1), jnp.concatenate([neg, tz], axis=-1)], axis=1)


def _nat_bias_strip(tab_ref, row):
    first_key = min(max(row - NAT_KH // 2, 0), NAT_ROWS - NAT_KH)
    off = first_key - row + WIN_ROWS - 1
    tiles = {}
    done, key = 0, first_key
    if key % 2 == 1:
        tiles[key // 2] = tab_ref[0, NAT_HIGH + off]
        done, key = 1, key + 1
    while done + 1 < NAT_KH:
        tiles[key // 2] = tab_ref[0, NAT_PAIR + off + done]
        done, key = done + 2, key + 2
    if done < NAT_KH:
        tiles[key // 2] = tab_ref[0, NAT_LOW + off + done]
    outside = jnp.full((GRID_W, 2 * GRID_W), NEG_BIG, F32)
    return jnp.concatenate([tiles.get(i, outside) for i in range(NAT_ROWS // 2)], axis=1)


GSEL_LANE = N_EXPERTS


def _pack_router(we, be, wg, bg):
    pad = LANES - N_EXPERTS - N_GROUPS
    w = jnp.concatenate([we, wg, jnp.zeros((D_MODEL, pad), F32)], axis=1)
    b = jnp.concatenate([be, bg, jnp.zeros((pad,), F32)]).reshape(1, LANES)
    hi = w.astype(BF16)
    lo = (w - hi.astype(F32)).astype(BF16)
    return jnp.concatenate([hi, lo], axis=1), b


def _lane_min_where(mask, lane):
    return jnp.min(jnp.where(mask, lane, LANES), axis=-1, keepdims=True)


def _sconv_tile(i, s_ref, before_ref, after_ref, w_ref):
    s = s_ref[...]
    p = s[:, W_GROUP:2 * W_GROUP] * s[:, 2 * W_GROUP:]
    p_before = before_ref[SUBLANES - 1:, W_GROUP:2 * W_GROUP] * before_ref[SUBLANES - 1:, 2 * W_GROUP:]
    p_after = after_ref[:1, W_GROUP:2 * W_GROUP] * after_ref[:1, 2 * W_GROUP:]
    row = _iota((TOK_TILE, 1), 0)
    seq_len = jnp.where(i < N_CTX_TILES, SEQ, DEC_SEQ)
    pos = (i * TOK_TILE + row) & (seq_len - 1)
    prev = jnp.where(row == 0, p_before, pltpu.roll(p, 1, 0))
    nxt = jnp.where(row == TOK_TILE - 1, p_after, pltpu.roll(p, TOK_TILE - 1, 0))
    prev = jnp.where(pos == 0, 0.0, prev)
    nxt = jnp.where(pos == seq_len - 1, 0.0, nxt)
    return s[:, :W_GROUP] * (w_ref[0:1, :] * prev + w_ref[1:2, :] * p + w_ref[2:3, :] * nxt)


def _outproj_kernel(xc_ref, xl_ref, m0, m1, m2, s_ref, before_ref, after_ref, cw_ref, mod_ref, nw_ref, w_ref,
                    rw_ref, rb_ref, x_out, hf_out, route_out, w_s):
    tile_i = pl.program_id(0)

    @pl.when(tile_i == 0)
    def _():
        w_s[...] = w_ref[0].astype(BF16)

    mixed = [m0[...], m1[...], m2[...], _sconv_tile(tile_i, s_ref, before_ref, after_ref, cw_ref)]
    acc = None
    for i, m in enumerate(mixed):
        part = _mm(m.astype(BF16), w_s[i * W_GROUP:(i + 1) * W_GROUP, :])
        acc = part if acc is None else acc + part
    x = _token_tile(tile_i, xc_ref, xl_ref) + mod_ref[0, 2:3, :] * acc
    x_out[...] = x
    y = x * lax.rsqrt(jnp.mean(x * x, axis=-1, keepdims=True) + EPS) * nw_ref[...]
    hf = y * (1.0 + mod_ref[0, 4:5, :]) + mod_ref[0, 3:4, :]
    hf_hi = hf.astype(BF16)
    hf_out[...] = hf_hi

    hf_lo = (hf - hf_hi.astype(F32)).astype(BF16)
    both = _mm(hf_hi, rw_ref[...])
    logits = both[:, :LANES] + both[:, LANES:] + _mm(hf_lo, rw_ref[:, :LANES]) + rb_ref[...]
    lane = _iota(logits.shape, 1)
    is_g = (lane >= N_EXPERTS) & (lane < N_EXPERTS + N_GROUPS)
    gl = jnp.where(is_g, logits, NEG_BIG)
    ge = jnp.exp(gl - jnp.max(gl, axis=-1, keepdims=True))
    gp = jnp.where(is_g, ge / jnp.sum(ge, axis=-1, keepdims=True), -1.0)
    gw = jnp.max(gp, axis=-1, keepdims=True)
    gsel = _lane_min_where(gp == gw, lane) - N_EXPERTS
    in_grp = (lane // EXPERTS_PER_GROUP == gsel) & (lane < N_EXPERTS)
    el = jnp.where(in_grp, logits, NEG_BIG)
    ee = jnp.exp(el - jnp.max(el, axis=-1, keepdims=True))
    ep = jnp.where(in_grp, ee / jnp.sum(ee, axis=-1, keepdims=True), -1.0)
    t1 = jnp.max(ep, axis=-1, keepdims=True)
    i1 = _lane_min_where(ep == t1, lane)
    ep2 = jnp.where(lane == i1, -1.0, ep)
    t2 = jnp.max(ep2, axis=-1, keepdims=True)
    i2 = _lane_min_where(ep2 == t2, lane)
    tsum = t1 + t2
    combine = jnp.where(lane == i1, gw * (t1 / tsum), 0.0) + jnp.where(lane == i2, gw * (t2 / tsum), 0.0)
    route_out[...] = jnp.where(lane == GSEL_LANE, gsel.astype(F32), combine)


def _outproj(x_ctx, x_lat, split, mixed, a_sc, sc_w, mod, norm_w, w_out, layer, rw, rb):
    tile = lambda w: pl.BlockSpec((TOK_TILE, w), lambda i: (i, 0))
    whole = lambda a: pl.BlockSpec(a.shape, lambda i: (0,) * a.ndim)
    per_tile = TOK_TILE // SUBLANES
    halo = lambda index: pl.BlockSpec((SUBLANES, 3 * W_GROUP), lambda i: (index(i), 0))
    return pl.pallas_call(
        _outproj_kernel,
        grid=(N_TOK // TOK_TILE,),
        in_specs=_token_specs(split) + [tile(W_GROUP)] * 3
                 + [tile(3 * W_GROUP), halo(lambda i: jnp.maximum(i * per_tile - 1, 0)),
                    halo(lambda i: jnp.minimum((i + 1) * per_tile, N_TOK // SUBLANES - 1)), whole(sc_w)]
                 + [pl.BlockSpec((1, SUBLANES, D_MODEL), lambda i: (_cond_of_tile(i), 0, 0)),
                    pl.BlockSpec((1, D_MODEL), lambda i: (0, 0)),
                    pl.BlockSpec((1, D_MODEL, D_MODEL), lambda i: (layer, 0, 0), pipeline_mode=pl.Buffered(1)),
                    whole(rw), whole(rb)],
        out_specs=[tile(D_MODEL), tile(D_MODEL), tile(LANES)],
        out_shape=[jax.ShapeDtypeStruct((N_TOK, D_MODEL), F32), jax.ShapeDtypeStruct((N_TOK, D_MODEL), BF16),
                   jax.ShapeDtypeStruct((N_TOK, LANES), F32)],
        scratch_shapes=[pltpu.VMEM((D_MODEL, D_MODEL), BF16)],
        compiler_params=_cparams("arbitrary"),
        name="outproj",
    )(x_ctx, x_lat, *mixed, a_sc, a_sc, a_sc, sc_w, mod, norm_w.reshape(1, D_MODEL), w_out, rw, rb)


SEG_BLK = 16
LOCAL_ROWS = TOK_TILE + N_GROUPS * SEG_BLK
N_TOK_TILES = N_TOK // TOK_TILE
MOE_ROWS = -(-(N_TOK + N_TOK_TILES * N_GROUPS * (SEG_BLK - 1) + N_GROUPS * (MOE_TILE - 1)) // MOE_TILE) * MOE_TILE


def _moe_tables(gsel):
    groups = jnp.arange(N_GROUPS, dtype=jnp.int32)
    onehot = (gsel.reshape(N_TOK_TILES, TOK_TILE, 1) == groups).astype(jnp.int32)
    earlier = jnp.asarray(np.tril(np.ones((TOK_TILE, TOK_TILE), np.float32), -1))
    rank = jnp.einsum('ts,nsg->ntg', earlier, onehot.astype(F32)).astype(jnp.int32)
    nblk = (jnp.sum(onehot, axis=1) + SEG_BLK - 1) // SEG_BLK
    loc_blk = jnp.cumsum(nblk, axis=1) - nblk
    blocks_per_tile = MOE_TILE // SEG_BLK
    grp_tiles = (jnp.sum(nblk, axis=0) + blocks_per_tile - 1) // blocks_per_tile
    grp_tile_start = jnp.cumsum(grp_tiles) - grp_tiles
    dst_blk = grp_tile_start[None, :] * blocks_per_tile + jnp.cumsum(nblk, axis=0) - nblk
    local_pos = jnp.sum(onehot * (loc_blk[:, None, :] * SEG_BLK + rank), axis=2)
    tile_idx = jnp.arange(MOE_ROWS // MOE_TILE, dtype=jnp.int32)
    tile_group = jnp.clip(jnp.sum(tile_idx[:, None] >= grp_tile_start[None, :], axis=1) - 1, 0, N_GROUPS - 1)
    in_group = tile_idx - grp_tile_start[tile_group]
    tile_rows = jnp.clip(jnp.sum(nblk, axis=0)[tile_group] * SEG_BLK - in_group * MOE_TILE, 0, MOE_TILE)
    flat = lambda a: a.reshape(-1).astype(jnp.int32)
    return local_pos.astype(jnp.int32), flat(nblk), flat(loc_blk), flat(dst_blk), flat(tile_group), flat(tile_rows)


def _segment_copies(t, nblk, loc_blk, dst_blk, make_copies, action):
    for g in range(N_GROUPS):
        k = t * N_GROUPS + g

        @pl.loop(0, nblk[k])
        def _(b):
            local = pl.multiple_of((loc_blk[k] + b) * SEG_BLK, SEG_BLK)
            sorted_row = pl.multiple_of((dst_blk[k] + b) * SEG_BLK, SEG_BLK)
            for cp in make_copies(local, sorted_row):
                action(cp)


def _dispatch_kernel(nblk, loc_blk, dst_blk, hf_ref, rt_ref, lp_ref, xs_in, rs_in, xs_hbm, rs_hbm, xbuf, rbuf, sem):
    t = pl.program_id(0)
    slot = t % 2
    onehot = _iota((LOCAL_ROWS, TOK_TILE), 0) == lp_ref[0]
    xbuf[slot] = _mm(onehot.astype(BF16), hf_ref[...]).astype(BF16)
    route = rt_ref[...]
    route_hi = route.astype(BF16)
    route_lo = (route - route_hi.astype(F32)).astype(BF16)
    rbuf[slot] = _mm(onehot.astype(BF16), route_hi) + _mm(onehot.astype(BF16), route_lo)

    def copies_of(s):
        def copies(local, sorted_row):
            return (pltpu.make_async_copy(xbuf.at[s, pl.ds(local, SEG_BLK)], xs_hbm.at[pl.ds(sorted_row, SEG_BLK)], sem.at[s]),
                    pltpu.make_async_copy(rbuf.at[s, pl.ds(local, SEG_BLK)], rs_hbm.at[pl.ds(sorted_row, SEG_BLK)], sem.at[s]))
        return copies

    @pl.when(t > 0)
    def _():
        _segment_copies(t - 1, nblk, loc_blk, dst_blk, copies_of(1 - slot), lambda cp: cp.wait())

    _segment_copies(t, nblk, loc_blk, dst_blk, copies_of(slot), lambda cp: cp.start())

    @pl.when(t == N_TOK_TILES - 1)
    def _():
        _segment_copies(t, nblk, loc_blk, dst_blk, copies_of(slot), lambda cp: cp.wait())


def _dispatch(hf, route, local_pos, nblk, loc_blk, dst_blk):
    grid_spec = pltpu.PrefetchScalarGridSpec(
        num_scalar_prefetch=3,
        grid=(N_TOK_TILES,),
        in_specs=[pl.BlockSpec((TOK_TILE, D_MODEL), lambda t, *_: (t, 0)),
                  pl.BlockSpec((TOK_TILE, LANES), lambda t, *_: (t, 0)),
                  pl.BlockSpec((1, 1, TOK_TILE), lambda t, *_: (t, 0, 0)),
                  pl.BlockSpec(memory_space=pl.ANY), pl.BlockSpec(memory_space=pl.ANY)],
        out_specs=[pl.BlockSpec(memory_space=pl.ANY), pl.BlockSpec(memory_space=pl.ANY)],
        scratch_shapes=[pltpu.VMEM((2, LOCAL_ROWS, D_MODEL), BF16), pltpu.VMEM((2, LOCAL_ROWS, LANES), F32),
                        pltpu.SemaphoreType.DMA((2,))],
    )
    return pl.pallas_call(
        _dispatch_kernel,
        grid_spec=grid_spec,
        out_shape=[jax.ShapeDtypeStruct((MOE_ROWS, D_MODEL), BF16), jax.ShapeDtypeStruct((MOE_ROWS, LANES), F32)],
        input_output_aliases={6: 0, 7: 1},
        compiler_params=_cparams("arbitrary"),
        name="dispatch",
    )(nblk, loc_blk, dst_blk, hf, route, local_pos.reshape(N_TOK_TILES, 1, TOK_TILE),
      jnp.zeros((MOE_ROWS, D_MODEL), BF16), jnp.zeros((MOE_ROWS, LANES), F32))


def _moe_kernel(tile_group, tile_rows, x_ref, r_ref, wg_hbm, wu_hbm, wd_hbm, y_ref,
                wg_b, wu_b, wd_b, stage_g, stage_u, stage_d, sem, *, layer):
    i = pl.program_id(0)
    g = tile_group[i]
    group_row = layer * N_GROUPS + g
    new_group = (i == 0) | (g != tile_group[jnp.maximum(i - 1, 0)])
    valid = tile_rows[i] > 0
    half = tile_rows[i] <= MOE_TILE // 2

    def weight_copies(e):
        slot = e % 2
        return (pltpu.make_async_copy(wg_hbm.at[group_row, e], stage_g.at[slot], sem.at[slot]),
                pltpu.make_async_copy(wu_hbm.at[group_row, e], stage_u.at[slot], sem.at[slot]),
                pltpu.make_async_copy(wd_hbm.at[group_row, e], stage_d.at[slot], sem.at[slot]))

    def run(load_weights, rows):
        x = x_ref[:rows, :]
        route = r_ref[:rows, :]
        lane = _iota(route.shape, 1)
        acc = jnp.zeros((rows, D_MODEL), F32)
        if load_weights:
            for e in range(2):
                for cp in weight_copies(e):
                    cp.start()
        for e in range(EXPERTS_PER_GROUP):
            if load_weights:
                for cp in weight_copies(e):
                    cp.wait()
                wg_b[e] = stage_g[e % 2].astype(BF16)
                wu_b[e] = stage_u[e % 2].astype(BF16)
                wd_b[e] = stage_d[e % 2].astype(BF16)
                if e + 2 < EXPERTS_PER_GROUP:
                    for cp in weight_copies(e + 2):
                        cp.start()
            cw = jnp.sum(jnp.where(lane == g * EXPERTS_PER_GROUP + e, route, 0.0), axis=-1, keepdims=True)
            act = _silu(_mm(x, wg_b[e])) * _mm(x, wu_b[e]) * cw
            acc = acc + _mm(act.astype(BF16), wd_b[e])
        y_ref[:rows, :] = acc
        if rows < MOE_TILE:
            y_ref[rows:, :] = jnp.zeros((MOE_TILE - rows, D_MODEL), F32)

    for load_weights in (True, False):
        for rows in (MOE_TILE, MOE_TILE // 2):
            first = new_group if load_weights else jnp.logical_not(new_group)
            fits = half if rows < MOE_TILE else jnp.logical_not(half)

            @pl.when(valid & first & fits)
            def _():
                run(load_weights, rows)

    @pl.when(jnp.logical_not(valid))
    def _():
        y_ref[...] = jnp.zeros_like(y_ref)


def _moe(xs, rs, tile_group, tile_rows, wg, wu, wd, layer):
    any_spec = pl.BlockSpec(memory_space=pl.ANY)
    grid_spec = pltpu.PrefetchScalarGridSpec(
        num_scalar_prefetch=2,
        grid=(MOE_ROWS // MOE_TILE,),
        in_specs=[pl.BlockSpec((MOE_TILE, D_MODEL), lambda i, tg, tv: (i, 0)),
                  pl.BlockSpec((MOE_TILE, LANES), lambda i, tg, tv: (i, 0)),
                  any_spec, any_spec, any_spec],
        out_specs=pl.BlockSpec((MOE_TILE, D_MODEL), lambda i, tg, tv: (i, 0)),
        scratch_shapes=[pltpu.VMEM((EXPERTS_PER_GROUP, D_MODEL, EXPERT_FF), BF16),
                        pltpu.VMEM((EXPERTS_PER_GROUP, D_MODEL, EXPERT_FF), BF16),
                        pltpu.VMEM((EXPERTS_PER_GROUP, EXPERT_FF, D_MODEL), BF16),
                        pltpu.VMEM((2, D_MODEL, EXPERT_FF), F32), pltpu.VMEM((2, D_MODEL, EXPERT_FF), F32),
                        pltpu.VMEM((2, EXPERT_FF, D_MODEL), F32), pltpu.SemaphoreType.DMA((2,))],
    )
    return pl.pallas_call(
        functools.partial(_moe_kernel, layer=layer),
        grid_spec=grid_spec,
        out_shape=jax.ShapeDtypeStruct((MOE_ROWS, D_MODEL), F32),
        compiler_params=_cparams("arbitrary"),
        name="moe",
    )(tile_group, tile_rows, xs, rs, wg, wu, wd)


def _combine_kernel(nblk, loc_blk, dst_blk, x_ref, lp_ref, mod_ref, ys_hbm, *refs, split):
    o_refs, ybuf, sem = refs[:-2], refs[-2], refs[-1]
    t = pl.program_id(0)
    slot = t % 2

    def fetch(tile, s):
        def copies(local, sorted_row):
            return (pltpu.make_async_copy(ys_hbm.at[pl.ds(sorted_row, SEG_BLK)], ybuf.at[s, pl.ds(local, SEG_BLK)], sem.at[s]),)
        ybuf[s] = jnp.zeros((LOCAL_ROWS, D_MODEL), F32)
        _segment_copies(tile, nblk, loc_blk, dst_blk, copies, lambda cp: cp.start())

    @pl.when(t == 0)
    def _():
        fetch(0, 0)

    @pl.when(t + 1 < N_TOK_TILES)
    def _():
        fetch(t + 1, 1 - slot)

    def copies_now(local, sorted_row):
        return (pltpu.make_async_copy(ys_hbm.at[pl.ds(sorted_row, SEG_BLK)], ybuf.at[slot, pl.ds(local, SEG_BLK)], sem.at[slot]),)

    _segment_copies(t, nblk, loc_blk, dst_blk, copies_now, lambda cp: cp.wait())

    onehot = (_iota((TOK_TILE, LOCAL_ROWS), 1) == lp_ref[...]).astype(BF16)
    ys = ybuf[slot]
    hi = ys.astype(BF16)
    lo = (ys - hi.astype(F32)).astype(BF16)
    y = _mm(onehot, hi) + _mm(onehot, lo)
    out = x_ref[...] + mod_ref[0, 5:6, :] * y
    if split:
        @pl.when(t < N_CTX_TILES)
        def _():
            o_refs[0][...] = out

        @pl.when(t >= N_CTX_TILES)
        def _():
            o_refs[1][...] = out
    else:
        o_refs[0][...] = out


def _combine(x, ys, mod, local_pos, nblk, loc_blk, dst_blk, split):
    tile = pl.BlockSpec((TOK_TILE, D_MODEL), lambda t, *_: (t, 0))
    if split:
        out_specs = _token_specs(True)
        out_shape = [jax.ShapeDtypeStruct((N_CTX, D_MODEL), F32), jax.ShapeDtypeStruct((N_LAT, D_MODEL), F32)]
    else:
        out_specs, out_shape = [tile], [jax.ShapeDtypeStruct((N_TOK, D_MODEL), F32)]
    grid_spec = pltpu.PrefetchScalarGridSpec(
        num_scalar_prefetch=3,
        grid=(N_TOK_TILES,),
        in_specs=[tile, pl.BlockSpec((TOK_TILE, 1), lambda t, *_: (t, 0)),
                  pl.BlockSpec((1, SUBLANES, D_MODEL), lambda t, *_: (_cond_of_tile(t), 0, 0)),
                  pl.BlockSpec(memory_space=pl.ANY)],
        out_specs=out_specs,
        scratch_shapes=[pltpu.VMEM((2, LOCAL_ROWS, D_MODEL), F32), pltpu.SemaphoreType.DMA((2,))],
    )
    return pl.pallas_call(
        functools.partial(_combine_kernel, split=split),
        grid_spec=grid_spec,
        out_shape=out_shape,
        compiler_params=_cparams("arbitrary"),
        name="combine",
    )(nblk, loc_blk, dst_blk, x, local_pos.reshape(N_TOK, 1), mod, ys)


def _lane_row(v):
    v = v.reshape(-1).astype(F32)
    return jnp.concatenate([v, jnp.zeros((LANES - v.shape[0],), F32)]).reshape(1, LANES)


def _pad_rows(w):
    return jnp.concatenate([w, jnp.zeros((SUBLANES - w.shape[0], w.shape[1]), w.dtype)], axis=0)


def kernel(x_prompt, x_sample, state_gdn, state_ret, cache_nat_k, cache_nat_v, c, c_ctx, ada_w, ada_b, norm_mix_w, norm_ffn_w, w_in, gdn_conv_w, gdn_a_log, gdn_dt_bias, gdn_norm_w, ret_gamma_logit, nat_q_norm_w, nat_k_norm_w, nat_rpb, sc_conv_w, w_out, router_group_w, router_group_b, router_expert_w, router_expert_b, moe_w_gate, moe_w_up, moe_w_down):
    x_ctx, x_lat, split = x_prompt.reshape(N_CTX, D_MODEL), x_sample.reshape(N_LAT, D_MODEL), True
    cond = jnp.concatenate([c_ctx[None, :], c], axis=0)
    ada = _ada(cond, ada_w, ada_b).reshape(DEPTH, SUBLANES, 6, D_MODEL)
    cos, sin = _rope_tables(DEC_SEQ)
    zero_state = jnp.zeros((BATCH, N_GATE, HEAD_DIM, HEAD_DIM), F32)
    lat_block = N_CTX // DEC_SEQ
    gdn_list, ret_list, caches = [], [], None
    mixed = [jnp.zeros((N_TOK, W_GROUP), F32) for _ in range(3)]
    for l in range(DEPTH):
        mod = jnp.concatenate([ada[l, :1 + DEC_BATCH], jnp.zeros((1 + DEC_BATCH, SUBLANES - 6, D_MODEL), F32)], axis=1)
        a_gdn, a_ret, a_nat, a_sc, a_gate = _inproj(x_ctx, x_lat, split, mod, norm_mix_w[l], w_in, l)

        conv_w = _pad_rows(gdn_conv_w[l])
        a_log, dt_b = _lane_row(gdn_a_log[l]), _lane_row(gdn_dt_bias[l])
        gnw = jnp.tile(gdn_norm_w[l], N_HEADS).reshape(1, W_GROUP)
        o_gdn, s_gdn = _gdn(a_gdn, a_gate, conv_w, a_log, dt_b, gnw, zero_state, SEQ, BATCH, 0, n_sub=2,
                            into=mixed[0])
        s0 = state_gdn[:, l].reshape(DEC_BATCH, N_GATE, HEAD_DIM, HEAD_DIM)
        o_gdn, _ = _gdn(a_gdn, a_gate, conv_w, a_log, dt_b, gnw, s0, DEC_SEQ, DEC_BATCH, lat_block, into=o_gdn)

        logit = jnp.repeat(ret_gamma_logit[l].astype(F32), HEAD_DIM, axis=1)
        o_ret, s_ret = _ret(a_ret, logit, zero_state, cos[:SEQ], sin[:SEQ], SEQ, BATCH, 0, False, into=mixed[1])
        s0 = state_ret[:, l].reshape(DEC_BATCH, N_GATE, HEAD_DIM, HEAD_DIM)
        o_ret, _ = _ret(a_ret, logit, s0, cos, sin, DEC_SEQ, DEC_BATCH, lat_block, True, into=o_ret)

        qw = jnp.tile(nat_q_norm_w[l], N_HEADS).reshape(1, W_GROUP)
        kw = jnp.tile(nat_k_norm_w[l], N_HEADS).reshape(1, W_GROUP)
        o_nat, *caches = _ctx_attn(a_nat, qw, kw, mixed[2], l, caches)
        o_nat = _nat(a_nat, cache_nat_k[:, l].reshape(DEC_BATCH, PAST_LEN, W_GROUP),
                     cache_nat_v[:, l].reshape(DEC_BATCH, PAST_LEN, W_GROUP), _nat_tables(nat_rpb[l]), qw, kw, o_nat)

        mixed = [o_gdn, o_ret, o_nat]
        rw, rb = _pack_router(router_expert_w[l], router_expert_b[l], router_group_w[l], router_group_b[l])
        x_mid, hf, route = _outproj(x_ctx, x_lat, split, mixed, a_sc, _pad_rows(sc_conv_w[l]), mod, norm_ffn_w[l],
                                    w_out, l, rw, rb)

        local_pos, nblk, loc_blk, dst_blk, tile_group, tile_rows = _moe_tables(route[:, GSEL_LANE].astype(jnp.int32))
        xs, rs = _dispatch(hf, route, local_pos, nblk, loc_blk, dst_blk)
        to_group = lambda w: w.reshape((DEPTH * N_GROUPS, EXPERTS_PER_GROUP) + w.shape[2:])
        ys = _moe(xs, rs, tile_group, tile_rows, to_group(moe_w_gate), to_group(moe_w_up), to_group(moe_w_down), l)
        last = l == DEPTH - 1
        out = _combine(x_mid, ys, mod, local_pos, nblk, loc_blk, dst_blk, split=last)
        x_ctx, x_lat, split = (out[0], out[1], True) if last else (out[0], out[0], False)

        gdn_list.append(s_gdn.reshape(BATCH, 2, N_HEADS, HEAD_DIM, HEAD_DIM))
        ret_list.append(s_ret.reshape(BATCH, 2, N_HEADS, HEAD_DIM, HEAD_DIM))
    new_k, new_v = [a.reshape(BATCH, DEPTH, N_HEADS, HEAD_DIM, SEQ).transpose(0, 1, 4, 2, 3) for a in caches]
    return (x_ctx.reshape(BATCH, SEQ, D_MODEL), x_lat.reshape(DEC_BATCH, DEC_SEQ, D_MODEL),
            jnp.stack(gdn_list, axis=1), jnp.stack(ret_list, axis=1), new_k, new_v)
```

```python
import functools

import numpy as np
import jax
import jax.numpy as jnp
from jax import lax
from jax.experimental import pallas as pl
from jax.experimental.pallas import tpu as pltpu

D_MODEL = 1024
BATCH = 16
SEQ = 256
DEPTH = 2
DEC_BATCH = 2
DEC_SEQ = 1024
PAST_LEN = 256
GRID_W = 64
HEAD_DIM = 64
W_GROUP = D_MODEL // 4
N_HEADS = W_GROUP // HEAD_DIM
CHUNK = 64
WIN_ROWS = 8
WIN_COLS = 16
ROPE_BASE = 10000.0
N_GROUPS = 4
EXPERTS_PER_GROUP = 8
N_EXPERTS = N_GROUPS * EXPERTS_PER_GROUP
EXPERT_FF = 256
EPS = 1e-6

N_CTX = BATCH * SEQ
N_LAT = DEC_BATCH * DEC_SEQ
N_TOK = N_CTX + N_LAT
LANES = 128
SUBLANES = 8
TOK_TILE = 512
MOE_TILE = 512
Q_TILE = 256
VMEM_LIMIT = 48 * 1024 * 1024
NEG_BIG = -1e30
N_GATE = 2 * N_HEADS

F32 = jnp.float32
BF16 = jnp.bfloat16
HI = lax.Precision.HIGHEST


def _mm(a, b, prec=None):
    return lax.dot_general(a, b, (((1,), (0,)), ((), ())), precision=prec, preferred_element_type=F32)


def _mm_tn(a, b, prec=None):
    return lax.dot_general(a, b, (((0,), (0,)), ((), ())), precision=prec, preferred_element_type=F32)


def _bmm(a, b):
    return _mm(a.astype(BF16), b.astype(BF16))


def _sigmoid(x):
    return 1.0 / (1.0 + jnp.exp(-x))


def _silu(x):
    return x * _sigmoid(x)


def _softplus(x):
    return jnp.maximum(x, 0.0) + jnp.log(1.0 + jnp.exp(-jnp.abs(x)))


def _iota(shape, dim):
    return lax.broadcasted_iota(jnp.int32, shape, dim)


def _cparams(*sem):
    return pltpu.CompilerParams(dimension_semantics=sem, vmem_limit_bytes=VMEM_LIMIT)


def _cond_of_tile(i):
    n_ctx_tiles = N_CTX // TOK_TILE
    return jnp.where(i < n_ctx_tiles, 0, 1 + (i - n_ctx_tiles) // (DEC_SEQ // TOK_TILE))


def _head_sum_matrix():
    return (_iota((W_GROUP, W_GROUP), 0) // HEAD_DIM == _iota((W_GROUP, W_GROUP), 1) // HEAD_DIM).astype(BF16)


def _head_sums(x, head_sum):
    hi = x.astype(BF16)
    lo = (x - hi.astype(F32)).astype(BF16)
    return _mm(hi, head_sum) + _mm(lo, head_sum)


N_COND = 1 + DEC_BATCH
ADA_TN = 1536


def _ada_kernel(c_ref, w_ref, b_ref, o_ref):
    def slab(s, accs):
        rows = pl.ds(pl.multiple_of(s * SUBLANES, SUBLANES), SUBLANES)
        w = w_ref[0, rows, :]
        return tuple(acc + w * jnp.tile(_silu(c_ref[r, rows, :]), (1, ADA_TN // LANES))
                     for r, acc in enumerate(accs))

    zero = jnp.zeros((SUBLANES, ADA_TN), F32)
    accs = lax.fori_loop(0, D_MODEL // SUBLANES, slab, (zero,) * N_COND, unroll=4)
    out = jnp.concatenate([jnp.sum(acc, axis=0, keepdims=True) for acc in accs]
                          + [jnp.zeros((SUBLANES - N_COND, ADA_TN), F32)], axis=0)
    o_ref[0] = out + b_ref[0]


def _ada(cond, ada_w, ada_b):
    n_out = 6 * D_MODEL
    cond_lanes = jnp.broadcast_to(cond[:, :, None], (N_COND, D_MODEL, LANES))
    return pl.pallas_call(
        _ada_kernel,
        grid=(DEPTH, n_out // ADA_TN),
        in_specs=[pl.BlockSpec((N_COND, D_MODEL, LANES), lambda l, j: (0, 0, 0)),
                  pl.BlockSpec((1, D_MODEL, ADA_TN), lambda l, j: (l, 0, j)),
                  pl.BlockSpec((1, 1, ADA_TN), lambda l, j: (l, 0, j))],
        out_specs=pl.BlockSpec((1, SUBLANES, ADA_TN), lambda l, j: (l, 0, j)),
        out_shape=jax.ShapeDtypeStruct((DEPTH, SUBLANES, n_out), F32),
        compiler_params=_cparams("arbitrary", "arbitrary"),
        name="ada",
    )(cond_lanes, ada_w, ada_b.reshape(DEPTH, 1, n_out))


IN_WIDTHS = (4 * W_GROUP, 4 * W_GROUP, 3 * W_GROUP, 3 * W_GROUP, LANES)
IN_PACKED = sum(IN_WIDTHS)


IN_TOTAL = 3 * W_GROUP + W_GROUP + 2 * N_GATE + 4 * W_GROUP + 3 * W_GROUP + 3 * W_GROUP
IN_GATE_SRC = 4 * W_GROUP
IN_SRC = (0, IN_GATE_SRC + 2 * N_GATE, IN_GATE_SRC + 2 * N_GATE + 4 * W_GROUP,
          IN_GATE_SRC + 2 * N_GATE + 7 * W_GROUP)
N_CTX_TILES = N_CTX // TOK_TILE


def _token_specs(split):
    lat0 = 0 if split else N_CTX_TILES
    return [pl.BlockSpec((TOK_TILE, D_MODEL), lambda i, *_: (jnp.minimum(i, N_CTX_TILES - 1), 0)),
            pl.BlockSpec((TOK_TILE, D_MODEL), lambda i, *_: (jnp.maximum(i, N_CTX_TILES) - N_CTX_TILES + lat0, 0))]


def _token_tile(i, ctx_ref, lat_ref):
    return jnp.where(i < N_CTX_TILES, ctx_ref[...], lat_ref[...])


def _inproj_kernel(xc_ref, xl_ref, mod_ref, nw_ref, w_ref, *refs):
    o_refs, w_s = refs[:-1], refs[-1]
    i = pl.program_id(0)

    @pl.when(i == 0)
    def _():
        piece = 256
        off = 0
        for src, width in zip(IN_SRC, IN_WIDTHS[:-1]):
            for c in range(0, width, piece):
                w_s[:, off + c:off + c + piece] = w_ref[0, src + c:src + c + piece, :].T.astype(BF16)
            off += width
        gate = w_ref[0, IN_GATE_SRC:IN_GATE_SRC + LANES, :].T
        w_s[:, off:] = jnp.where(_iota((D_MODEL, LANES), 1) < 2 * N_GATE, gate, 0.0).astype(BF16)

    x = _token_tile(i, xc_ref, xl_ref)
    y = x * lax.rsqrt(jnp.mean(x * x, axis=-1, keepdims=True) + EPS) * nw_ref[...]
    h = (y * (1.0 + mod_ref[0, 1:2, :]) + mod_ref[0, 0:1, :]).astype(BF16)
    off = 0
    for o_ref, width in zip(o_refs, IN_WIDTHS):
        o_ref[...] = _mm(h, w_s[:, off:off + width])
        off += width


def _inproj(x_ctx, x_lat, split, mod, norm_w, w_in, layer):
    return pl.pallas_call(
        _inproj_kernel,
        grid=(N_TOK // TOK_TILE,),
        in_specs=_token_specs(split)
                 + [pl.BlockSpec((1, SUBLANES, D_MODEL), lambda i: (_cond_of_tile(i), 0, 0)),
                    pl.BlockSpec((1, D_MODEL), lambda i: (0, 0)),
                    pl.BlockSpec((1, IN_TOTAL, D_MODEL), lambda i: (layer, 0, 0), pipeline_mode=pl.Buffered(1))],
        out_specs=[pl.BlockSpec((TOK_TILE, w), lambda i: (i, 0)) for w in IN_WIDTHS],
        out_shape=[jax.ShapeDtypeStruct((N_TOK, w), F32) for w in IN_WIDTHS],
        scratch_shapes=[pltpu.VMEM((D_MODEL, IN_PACKED), BF16)],
        compiler_params=_cparams("arbitrary"),
        name="inproj",
    )(x_ctx, x_lat, mod, norm_w.reshape(1, D_MODEL), jnp.swapaxes(w_in, 1, 2))


def _seq_call(kernel_fn, args, in_specs, out_specs, out_shape, into, more_into=(), **kwargs):
    donors = ([] if into is None else [(into, 0)]) + list(more_into)
    n_in = len(args)
    inner = kernel_fn
    kernel_fn = lambda *refs: inner(*refs[:n_in], *refs[n_in + len(donors):])
    aliases = {n_in + j: out_idx for j, (_, out_idx) in enumerate(donors)}
    args = list(args) + [a for a, _ in donors]
    in_specs = list(in_specs) + [pl.BlockSpec(memory_space=pl.ANY)] * len(donors)
    return pl.pallas_call(kernel_fn, in_specs=in_specs, out_specs=out_specs, out_shape=out_shape,
                          input_output_aliases=aliases, **kwargs)(*args)


def _state_output(n_seq, n_sub, layer, states):
    if layer is None:
        slots, index, rows, donors, n_fill = N_GATE, 0, N_GATE, (), 0
    elif states is None:
        slots, index, rows, donors, n_fill = DEPTH * N_GATE, 0, DEPTH * N_GATE, (), DEPTH - 1
    else:
        slots, index, rows, donors, n_fill = N_GATE, layer, DEPTH * N_GATE, ((states, 1),), 0
    spec = pl.BlockSpec((n_sub, slots, HEAD_DIM, HEAD_DIM), lambda i: (i, index, 0, 0))
    return spec, jax.ShapeDtypeStruct((n_seq, rows, HEAD_DIM, HEAD_DIM), F32), donors, n_fill


def _zero_later_states(sfin_ref, sub, n_fill):
    for later in range(N_GATE, N_GATE * (1 + n_fill)):
        sfin_ref[sub, later] = jnp.zeros((HEAD_DIM, HEAD_DIM), F32)


def _shift_rows(p, seq_len):
    rows = p.shape[0]
    pos = _iota(p.shape, 0) & (seq_len - 1)
    prev = jnp.where(pos == 0, 0.0, pltpu.roll(p, 1, 0))
    nxt = jnp.where(pos == seq_len - 1, 0.0, pltpu.roll(p, rows - 1, 0))
    return prev, nxt


def _conv3(x, w_ref, seq_len):
    prev, nxt = _shift_rows(x, seq_len)
    return w_ref[0:1, :] * prev + w_ref[1:2, :] * x + w_ref[2:3, :] * nxt


def _chunk_scan(x, reverse):
    t = x.shape[0]
    pos = _iota(x.shape, 0) % CHUNK
    step = 1
    while step < CHUNK:
        if reverse:
            x = x + jnp.where(pos < CHUNK - step, pltpu.roll(x, t - step, 0), 0.0)
        else:
            x = x + jnp.where(pos >= step, pltpu.roll(x, step, 0), 0.0)
        step *= 2
    return x


GDN_GROUP_CHUNKS = 4
GDN_BASE_BLOCK = 4
GDN_CHAINS = GDN_GROUP_CHUNKS * N_GATE
GDN_PAIRS = GDN_GROUP_CHUNKS * N_HEADS


def _gdn_kernel(a_ref, gate_ref, convw_ref, alog_ref, dtb_ref, nw_ref, s0_ref, o_ref, sfin_ref,
                q_s, kv_s, kt_s, gc_s, eg_s, beta_s, gcrow_s, ekdrow_s, cdec_s, uo_s, wq_s, attn_s, kdt_s,
                st_s, wsqs_s, kk_s, d_s, m1_s, p_s, low_s, pb_s, rhs_s, *, t, n_sub, n_fill):
    n_chunks = t // CHUNK
    a = a_ref[...]
    qkv = _silu(_conv3(a[:, :3 * W_GROUP], convw_ref, t // n_sub))
    q = qkv[:, :W_GROUP]
    k = qkv[:, W_GROUP:2 * W_GROUP]
    v = qkv[:, 2 * W_GROUP:]
    head_sum = _head_sum_matrix()
    q = q * lax.rsqrt(_head_sums(q * q, head_sum) + EPS) * (HEAD_DIM ** -0.5)
    k = k * lax.rsqrt(_head_sums(k * k, head_sum) + EPS)
    for h in range(N_HEADS):
        hs = slice(h * HEAD_DIM, (h + 1) * HEAD_DIM)
        q_s[h] = q[:, hs]
        kv_s[h] = jnp.concatenate([k[:, hs], v[:, hs]], axis=1)
    k_t = k.T
    for c in range(n_chunks):
        kt_s[c] = k_t[:, c * CHUNK:(c + 1) * CHUNK]

    gates = gate_ref[...]
    log_a = -jnp.exp(alog_ref[...]) * _softplus(gates + dtb_ref[...])
    beta_s[...] = _sigmoid(gates)

    ci = _iota((CHUNK, CHUNK), 0)
    cj = _iota((CHUNK, CHUNK), 1)
    eye = (ci == cj).astype(F32)
    blk_mask = (ci // GDN_BASE_BLOCK) == (cj // GDN_BASE_BLOCK)
    low_half = _iota((CHUNK, 2 * HEAD_DIM), 1) < HEAD_DIM

    prefix = _chunk_scan(log_a, reverse=False)
    suffix = _chunk_scan(log_a, reverse=True)
    gc = jnp.where(_iota((t, LANES), 1) < N_HEADS, prefix, suffix)
    gt = prefix + suffix - log_a
    gc_s[...] = gc
    eg_s[...] = jnp.exp(gc)
    gc_t = gc.T
    ekd_t = jnp.exp(gt - gc).T
    cdec_t = jnp.exp(gt).T
    for c in range(n_chunks):
        lanes = slice(c * CHUNK, (c + 1) * CHUNK)
        gcrow_s[c] = gc_t[:N_GATE, lanes]
        ekdrow_s[c] = ekd_t[:N_GATE, lanes]
        cdec_s[c] = jnp.concatenate([cdec_t[:N_GATE, lanes]] * 2, axis=1)

    def solve_group(grp, carry):
        row0 = grp * (GDN_GROUP_CHUNKS * CHUNK)
        chains = [(cl, a_idx) for cl in range(GDN_GROUP_CHUNKS) for a_idx in range(N_GATE)]

        def rows_of(cl):
            return pl.ds(pl.multiple_of(row0 + cl * CHUNK, CHUNK), CHUNK)

        for cl in range(GDN_GROUP_CHUNKS):
            for h in range(N_HEADS):
                rows = rows_of(cl)
                kq = jnp.concatenate([kv_s[h, rows, :HEAD_DIM], q_s[h, rows, :]], axis=0)
                k_t_h = kt_s[grp * GDN_GROUP_CHUNKS + cl, h * HEAD_DIM:(h + 1) * HEAD_DIM, :]
                kk_s[cl * N_HEADS + h] = _bmm(kq, k_t_h)
        for b, (cl, a_idx) in enumerate(chains):
            backward = a_idx >= N_HEADS
            h = a_idx % N_HEADS
            rows, c = rows_of(cl), grp * GDN_GROUP_CHUNKS + cl
            incl = (cj >= ci) if backward else (cj <= ci)
            strict = (cj > ci) if backward else (cj < ci)
            bt = beta_s[rows, N_GATE + a_idx:N_GATE + a_idx + 1]
            decay = jnp.exp(jnp.where(incl, gc_s[rows, a_idx:a_idx + 1] - gcrow_s[c, a_idx:a_idx + 1, :], NEG_BIG))
            low = jnp.where(strict, kk_s[cl * N_HEADS + h, :CHUNK, :] * bt * decay, 0.0)
            attn_s[a_idx, rows, :] = (kk_s[cl * N_HEADS + h, CHUNK:, :] * decay).astype(BF16)
            d_s[b] = jnp.where(blk_mask, low, 0.0)
            low_s[b] = low.astype(BF16)
            rhs_s[b] = (kv_s[h, rows, :] * bt
                        * jnp.where(low_half, eg_s[rows, a_idx:a_idx + 1], 1.0)).astype(BF16)
        for b in range(GDN_CHAINS):
            m1_s[b] = _bmm(d_s[b], d_s[b])
        for b in range(GDN_CHAINS):
            d, d2 = d_s[b], m1_s[b]
            p_s[b] = eye - d + d2 - _bmm(d, d2)
        size = GDN_BASE_BLOCK
        while size < CHUNK:
            pair = ((ci // size) != (cj // size)) & ((ci // (2 * size)) == (cj // (2 * size)))
            for b in range(GDN_CHAINS):
                coupling = jnp.where(pair, low_s[b], jnp.zeros((), BF16))
                pb_s[b] = _mm(p_s[b].astype(BF16), coupling).astype(BF16)
            for b in range(GDN_CHAINS):
                p = p_s[b]
                p_s[b] = p - _mm(pb_s[b], p.astype(BF16))
            size *= 2
        for b, (cl, a_idx) in enumerate(chains):
            h = a_idx % N_HEADS
            rows, c = rows_of(cl), grp * GDN_GROUP_CHUNKS + cl
            wu = _mm(p_s[b].astype(BF16), rhs_s[b])
            uo_s[a_idx, rows, :] = wu
            wq_s[a_idx, c, :CHUNK, :] = wu[:, :HEAD_DIM].astype(BF16)
            wq_s[a_idx, c, CHUNK:, :] = (q_s[h, rows, :] * eg_s[rows, a_idx:a_idx + 1]).astype(BF16)
            k_t_h = kt_s[c, h * HEAD_DIM:(h + 1) * HEAD_DIM, :]
            kdt_s[a_idx, c] = (k_t_h * ekdrow_s[c, a_idx:a_idx + 1, :]).astype(BF16)
        return carry

    lax.fori_loop(0, n_chunks // GDN_GROUP_CHUNKS, solve_group, 0)

    seq_chunks = n_chunks // n_sub
    chains = [(sub, a_idx) for sub in range(n_sub) for a_idx in range(N_GATE)]
    for j, (sub, a_idx) in enumerate(chains):
        st_s[j] = jnp.concatenate([jnp.zeros((HEAD_DIM, HEAD_DIM), F32), s0_ref[sub, a_idx]], axis=1)

    def scan_chunk(c, carry):
        def chunk_of(sub, a_idx):
            return sub * seq_chunks + ((seq_chunks - 1 - c) if a_idx >= N_HEADS else c)

        for j, (sub, a_idx) in enumerate(chains):
            wsqs_s[j] = _mm(wq_s[a_idx, chunk_of(sub, a_idx)], st_s[j].astype(BF16))
        for j, (sub, a_idx) in enumerate(chains):
            cc = chunk_of(sub, a_idx)
            rows = pl.ds(pl.multiple_of(cc * CHUNK, CHUNK), CHUNK)
            v_new = (uo_s[a_idx, rows, :] - wsqs_s[j, :CHUNK, :]).astype(BF16)
            uo_s[a_idx, rows, :] = wsqs_s[j, CHUNK:, :] + _mm(attn_s[a_idx, rows, :], v_new)
            st_s[j] = st_s[j] * cdec_s[cc, a_idx:a_idx + 1, :] + _mm(kdt_s[a_idx, cc], v_new)
        return carry

    lax.fori_loop(0, seq_chunks, scan_chunk, 0)
    for j, (sub, a_idx) in enumerate(chains):
        sfin_ref[sub, a_idx] = st_s[j, :, HEAD_DIM:]
    for sub in range(n_sub):
        _zero_later_states(sfin_ref, sub, n_fill)

    o = jnp.concatenate([(uo_s[h] + uo_s[N_HEADS + h])[:, HEAD_DIM:] for h in range(N_HEADS)], axis=1)
    ms = _head_sums(o * o, head_sum) * (1.0 / HEAD_DIM)
    o_ref[...] = o * lax.rsqrt(ms + EPS) * nw_ref[...] * _silu(a[:, 3 * W_GROUP:])


def _gdn(a_all, gate_all, conv_w, a_log, dt_bias, norm_w, s0, seq_len, n_seq, first_block, n_sub=1, into=None,
         layer=None, states=None):
    small = lambda: pl.BlockSpec((1, LANES), lambda i: (0, 0))
    t = n_sub * seq_len
    n_chunks = t // CHUNK
    wide = 2 * HEAD_DIM
    scratch = [pltpu.VMEM((N_HEADS, t, HEAD_DIM), F32),
               pltpu.VMEM((N_HEADS, t, wide), F32),
               pltpu.VMEM((n_chunks, W_GROUP, CHUNK), F32),
               pltpu.VMEM((t, LANES), F32), pltpu.VMEM((t, LANES), F32), pltpu.VMEM((t, LANES), F32),
               pltpu.VMEM((n_chunks, N_GATE, CHUNK), F32), pltpu.VMEM((n_chunks, N_GATE, CHUNK), F32),
               pltpu.VMEM((n_chunks, N_GATE, wide), F32),
               pltpu.VMEM((N_GATE, t, wide), F32),
               pltpu.VMEM((N_GATE, n_chunks, 2 * CHUNK, HEAD_DIM), BF16),
               pltpu.VMEM((N_GATE, t, CHUNK), BF16),
               pltpu.VMEM((N_GATE, n_chunks, HEAD_DIM, CHUNK), BF16),
               pltpu.VMEM((n_sub * N_GATE, HEAD_DIM, wide), F32),
               pltpu.VMEM((n_sub * N_GATE, 2 * CHUNK, wide), F32),
               pltpu.VMEM((GDN_PAIRS, 2 * CHUNK, CHUNK), F32)]
    scratch += [pltpu.VMEM((GDN_CHAINS, CHUNK, CHUNK), F32)] * 3
    scratch += [pltpu.VMEM((GDN_CHAINS, CHUNK, CHUNK), BF16)] * 2
    scratch += [pltpu.VMEM((GDN_CHAINS, CHUNK, wide), BF16)]
    state_spec, state_shape, donors, n_fill = _state_output(n_seq, n_sub, layer, states)
    return _seq_call(
        functools.partial(_gdn_kernel, t=t, n_sub=n_sub, n_fill=n_fill),
        (a_all, gate_all, conv_w, a_log, dt_bias, norm_w, s0),
        in_specs=[pl.BlockSpec((t, 4 * W_GROUP), lambda i: (i + first_block, 0)),
                  pl.BlockSpec((t, LANES), lambda i: (i + first_block, 0)),
                  pl.BlockSpec((SUBLANES, 3 * W_GROUP), lambda i: (0, 0)),
                  small(), small(),
                  pl.BlockSpec((1, W_GROUP), lambda i: (0, 0)),
                  pl.BlockSpec((n_sub, N_GATE, HEAD_DIM, HEAD_DIM), lambda i: (i, 0, 0, 0))],
        out_specs=[pl.BlockSpec((t, W_GROUP), lambda i: (i + first_block, 0)), state_spec],
        out_shape=[jax.ShapeDtypeStruct((N_TOK, W_GROUP), F32), state_shape],
        scratch_shapes=scratch,
        into=into, more_into=donors, grid=(n_seq // n_sub,), compiler_params=_cparams("arbitrary"), name="gdn")


def _swap16(x):
    width = x.shape[-1]
    first = (_iota(x.shape, 1) // 16) % 2 == 0
    return jnp.where(first, pltpu.roll(x, width - 16, 1), pltpu.roll(x, 16, 1))


def _block_diag_heads(s0_ref, first):
    zero = jnp.zeros((HEAD_DIM, HEAD_DIM), F32)
    return jnp.concatenate(
        [jnp.concatenate([s0_ref[0, first + h] if j == h else zero for j in range(N_HEADS)], axis=1)
         for h in range(N_HEADS)], axis=0)


def _ret_kernel(r_ref, lg_ref, s0_ref, cos_ref, sin_ref, o_ref, sfin_ref, *, t, latent, n_fill):
    r = r_ref[...]
    q = r[:, :W_GROUP]
    k = r[:, W_GROUP:2 * W_GROUP]
    v = r[:, 2 * W_GROUP:3 * W_GROUP]
    if latent:
        q = q * cos_ref[...] + _swap16(q) * sin_ref[...]
        k = k * cos_ref[...] + _swap16(k) * sin_ref[...]
    k = k * (HEAD_DIM ** -0.5)
    lg = -_softplus(-lg_ref[...])
    lgf, lgb = lg[0:1, :], lg[1:2, :]
    head = _iota((1, W_GROUP), 1) // HEAD_DIM
    head_sum = _head_sum_matrix()
    pos = _iota((t, 1), 0).astype(F32)
    q_b = q.astype(BF16)
    kt_b = k.T.astype(BF16)
    v_heads = [jnp.where(head == h, v, 0.0).astype(BF16) for h in range(N_HEADS)]
    if latent:
        s0f = _block_diag_heads(s0_ref, 0)
        s0b = _block_diag_heads(s0_ref, N_HEADS)
    for qt in range(t // Q_TILE):
        rows = slice(qt * Q_TILE, (qt + 1) * Q_TILE)
        diff = (_iota((Q_TILE, t), 0) + qt * Q_TILE - _iota((Q_TILE, t), 1)).astype(F32)
        both = jnp.where(diff == 0, 2.0, 1.0)
        o = jnp.zeros((Q_TILE, W_GROUP), F32)
        for h in range(N_HEADS):
            lgf_h = lgf[:, h * HEAD_DIM:h * HEAD_DIM + 1]
            lgb_h = lgb[:, h * HEAD_DIM:h * HEAD_DIM + 1]
            dmat = jnp.exp(diff * jnp.where(diff >= 0, lgf_h, -lgb_h)) * both
            s = _mm(jnp.where(head == h, q_b[rows], 0.0), kt_b) * dmat
            o = o + _mm(s.astype(BF16), v_heads[h])
        if latent:
            p = pos[rows]
            o = o + jnp.exp((p + 1.0) * lgf) * _bmm(q_b[rows], s0f) + jnp.exp((t - p) * lgb) * _bmm(q_b[rows], s0b)
        oc = o - _head_sums(o, head_sum) * (1.0 / HEAD_DIM)
        on = oc * lax.rsqrt(_head_sums(oc * oc, head_sum) * (1.0 / HEAD_DIM) + EPS)
        o_ref[rows, :] = on * _silu(r[rows, 3 * W_GROUP:])
    v_b = v.astype(BF16)
    sf = _mm_tn((k * jnp.exp((t - 1.0 - pos) * lgf)).astype(BF16), v_b)
    sb = _mm_tn((k * jnp.exp(pos * lgb)).astype(BF16), v_b)
    for h in range(N_HEADS):
        hs = slice(h * HEAD_DIM, (h + 1) * HEAD_DIM)
        sf_h, sb_h = sf[hs, hs], sb[hs, hs]
        if latent:
            sf_h = sf_h + jnp.exp(t * lgf[:, h * HEAD_DIM:h * HEAD_DIM + 1]) * s0_ref[0, h]
            sb_h = sb_h + jnp.exp(t * lgb[:, h * HEAD_DIM:h * HEAD_DIM + 1]) * s0_ref[0, N_HEADS + h]
        sfin_ref[0, h] = sf_h
        sfin_ref[0, N_HEADS + h] = sb_h
    _zero_later_states(sfin_ref, 0, n_fill)


def _ret(r_all, logit, s0, cos, sin, t, n_seq, first_block, latent, into=None, layer=None, states=None):
    state_spec, state_shape, donors, n_fill = _state_output(n_seq, 1, layer, states)
    return _seq_call(
        functools.partial(_ret_kernel, t=t, latent=latent, n_fill=n_fill), (r_all, logit, s0, cos, sin),
        in_specs=[pl.BlockSpec((t, 4 * W_GROUP), lambda i: (i + first_block, 0)),
                  pl.BlockSpec((2, W_GROUP), lambda i: (0, 0)),
                  pl.BlockSpec((1, N_GATE, HEAD_DIM, HEAD_DIM), lambda i: (i, 0, 0, 0)),
                  pl.BlockSpec((t, W_GROUP), lambda i: (0, 0)),
                  pl.BlockSpec((t, W_GROUP), lambda i: (0, 0))],
        out_specs=[pl.BlockSpec((t, W_GROUP), lambda i: (i + first_block, 0)), state_spec],
        out_shape=[jax.ShapeDtypeStruct((N_TOK, W_GROUP), F32), state_shape],
        into=into, more_into=donors, grid=(n_seq,), compiler_params=_cparams("arbitrary"), name="ret")


def _rope_tables(t):
    pos = np.arange(t)
    row = (pos // GRID_W).astype(np.float32)
    col = (pos % GRID_W).astype(np.float32)
    nf = HEAD_DIM // 4
    inv_freq = jnp.power(ROPE_BASE, -jnp.arange(nf, dtype=F32) / nf)
    ang_r = jnp.asarray(row)[:, None] * inv_freq[None, :]
    ang_c = jnp.asarray(col)[:, None] * inv_freq[None, :]
    cos = jnp.concatenate([jnp.cos(ang_r)] * 2 + [jnp.cos(ang_c)] * 2, axis=1)
    sin = jnp.concatenate([-jnp.sin(ang_r), jnp.sin(ang_r), -jnp.sin(ang_c), jnp.sin(ang_c)], axis=1)
    return jnp.tile(cos, (1, N_HEADS)), jnp.tile(sin, (1, N_HEADS))


def _head_rms(x, w, head_sum):
    return x * lax.rsqrt(_head_sums(x * x, head_sum) * (1.0 / HEAD_DIM) + EPS) * w


def _ctx_attn_kernel(n_ref, qw_ref, kw_ref, o_ref, k_out, v_out, *, n_fill):
    n = n_ref[...]
    head_sum = _head_sum_matrix()
    head = _iota((1, W_GROUP), 1) // HEAD_DIM
    q = (_head_rms(n[:, :W_GROUP], qw_ref[...], head_sum) * (HEAD_DIM ** -0.5)).astype(BF16)
    k = _head_rms(n[:, W_GROUP:2 * W_GROUP], kw_ref[...], head_sum)
    v = n[:, 2 * W_GROUP:]
    k_t = k.T
    k_out[0, 0] = k_t
    v_out[0, 0] = v.T
    for later in range(1, 1 + n_fill):
        k_out[0, later] = jnp.zeros((W_GROUP, SEQ), F32)
        v_out[0, later] = jnp.zeros((W_GROUP, SEQ), F32)
    kt_b = k_t.astype(BF16)
    o = jnp.zeros((SEQ, W_GROUP), F32)
    for h in range(N_HEADS):
        s = _mm(jnp.where(head == h, q, 0.0), kt_b)
        p = jnp.exp(s - jnp.max(s, axis=-1, keepdims=True))
        o = o + _mm(p.astype(BF16), jnp.where(head == h, v, 0.0).astype(BF16)) / jnp.sum(p, axis=-1, keepdims=True)
    o_ref[...] = o


def _ctx_attn(n_all, qw, kw, into, layer, caches):
    slots = DEPTH if caches is None else 1
    cache_spec = pl.BlockSpec((1, slots, W_GROUP, SEQ), lambda i: (i, layer if caches is not None else 0, 0, 0))
    cache_shape = jax.ShapeDtypeStruct((BATCH, DEPTH, W_GROUP, SEQ), F32)
    return _seq_call(
        functools.partial(_ctx_attn_kernel, n_fill=slots - 1), (n_all, qw, kw),
        in_specs=[pl.BlockSpec((SEQ, 3 * W_GROUP), lambda i: (i, 0)),
                  pl.BlockSpec((1, W_GROUP), lambda i: (0, 0)),
                  pl.BlockSpec((1, W_GROUP), lambda i: (0, 0))],
        out_specs=[pl.BlockSpec((SEQ, W_GROUP), lambda i: (i, 0)), cache_spec, cache_spec],
        out_shape=[jax.ShapeDtypeStruct((N_TOK, W_GROUP), F32), cache_shape, cache_shape],
        into=into, more_into=() if caches is None else ((caches[0], 1), (caches[1], 2)),
        grid=(BATCH,), compiler_params=_cparams("arbitrary"), name="ctx_attn")


def _nat_kernel(n_ref, ck_ref, cv_ref, bias_ref, qw_ref, kw_ref, o_ref, q_s, kt_s, ckt_s):
    h = pl.program_id(1)

    @pl.when(h == 0)
    def _():
        head_sum = _head_sum_matrix()
        n = n_ref[...]
        q_s[...] = (_head_rms(n[:, :W_GROUP], qw_ref[...], head_sum) * (HEAD_DIM ** -0.5)).astype(BF16)
        kt_s[...] = _head_rms(n[:, W_GROUP:2 * W_GROUP], kw_ref[...], head_sum).T.astype(BF16)
        ckt_s[...] = ck_ref[0].T.astype(BF16)
        o_ref[...] = jnp.zeros_like(o_ref)

    mine = _iota((1, W_GROUP), 1) // HEAD_DIM == h
    v = jnp.where(mine, n_ref[:, 2 * W_GROUP:], 0.0).astype(BF16)
    cv = jnp.where(mine, cv_ref[0], 0.0).astype(BF16)
    for qt in range(DEC_SEQ // Q_TILE):
        rows = slice(qt * Q_TILE, (qt + 1) * Q_TILE)
        q = jnp.where(mine, q_s[rows, :], 0.0)
        grid_rows = range(qt * Q_TILE // GRID_W, (qt + 1) * Q_TILE // GRID_W)
        s_loc = _mm(q, kt_s[...]) + jnp.concatenate([_nat_bias_strip(bias_ref, row) for row in grid_rows], axis=0)
        s_ctx = _mm(q, ckt_s[...])
        m = jnp.maximum(jnp.max(s_loc, axis=-1, keepdims=True), jnp.max(s_ctx, axis=-1, keepdims=True))
        p_loc = jnp.exp(s_loc - m)
        p_ctx = jnp.exp(s_ctx - m)
        den = jnp.sum(p_loc, axis=-1, keepdims=True) + jnp.sum(p_ctx, axis=-1, keepdims=True)
        o_ref[rows, :] += (_mm(p_loc.astype(BF16), v) + _mm(p_ctx.astype(BF16), cv)) / den


def _nat(n_all, ck, cv, bias, qw, kw, into):
    first_block = N_CTX // DEC_SEQ
    return _seq_call(
        _nat_kernel, (n_all, ck, cv, bias, qw, kw),
        in_specs=[pl.BlockSpec((DEC_SEQ, 3 * W_GROUP), lambda b, h: (b + first_block, 0)),
                  pl.BlockSpec((1, PAST_LEN, W_GROUP), lambda b, h: (b, 0, 0)),
                  pl.BlockSpec((1, PAST_LEN, W_GROUP), lambda b, h: (b, 0, 0)),
                  pl.BlockSpec((1, 3 * N_ROW_OFF - 1, GRID_W, 2 * GRID_W), lambda b, h: (h, 0, 0, 0)),
                  pl.BlockSpec((1, W_GROUP), lambda b, h: (0, 0)),
                  pl.BlockSpec((1, W_GROUP), lambda b, h: (0, 0))],
        out_specs=pl.BlockSpec((DEC_SEQ, W_GROUP), lambda b, h: (b + first_block, 0)),
        out_shape=jax.ShapeDtypeStruct((N_TOK, W_GROUP), F32),
        scratch_shapes=[pltpu.VMEM((DEC_SEQ, W_GROUP), BF16), pltpu.VMEM((W_GROUP, DEC_SEQ), BF16),
                        pltpu.VMEM((W_GROUP, PAST_LEN), BF16)],
        into=into, grid=(DEC_BATCH, N_HEADS), compiler_params=_cparams("arbitrary", "arbitrary"), name="nat")


N_ROW_OFF = 2 * WIN_ROWS - 1
NAT_ROWS = DEC_SEQ // GRID_W
NAT_KH = min(WIN_ROWS, NAT_ROWS)
NAT_PAIR, NAT_LOW, NAT_HIGH = 0, N_ROW_OFF - 1, 2 * N_ROW_OFF - 1


def _nat_tables(rpb):
    c = np.arange(GRID_W)
    c0 = np.clip(c - WIN_COLS // 2, 0, GRID_W - WIN_COLS)
    col_in = (c[None, :] >= c0[:, None]) & (c[None, :] < c0[:, None] + WIN_COLS)
    col_idx = np.clip(c[None, :] - c[:, None], -(WIN_COLS - 1), WIN_COLS - 1) + WIN_COLS - 1
    col_hot = (col_idx[..., None] == np.arange(2 * WIN_COLS - 1)).astype(np.float32)
    tz = jnp.where(col_in, jnp.einsum('hab,qkb->haqk', rpb.astype(F32), col_hot, precision=HI), NEG_BIG)
    neg = jnp.full_like(tz, NEG_BIG)
    return jnp.concatenate([jnp.concatenate([tz[:, :-1], tz[:, 1:]], axis=-1),
                            jnp.concatenate([tz, neg], axis=-1), jnp.concatenate([neg, tz], axis=-1)], axis=1)


def _nat_bias_strip(tab_ref, row):
    first_key = min(max(row - NAT_KH // 2, 0), NAT_ROWS - NAT_KH)
    off = first_key - row + WIN_ROWS - 1
    tiles = {}
    done, key = 0, first_key
    if key % 2 == 1:
        tiles[key // 2] = tab_ref[0, NAT_HIGH + off]
        done, key = 1, key + 1
    while done + 1 < NAT_KH:
        tiles[key // 2] = tab_ref[0, NAT_PAIR + off + done]
        done, key = done + 2, key + 2
    if done < NAT_KH:
        tiles[key // 2] = tab_ref[0, NAT_LOW + off + done]
    outside = jnp.full((GRID_W, 2 * GRID_W), NEG_BIG, F32)
    return jnp.concatenate([tiles.get(i, outside) for i in range(NAT_ROWS // 2)], axis=1)


GSEL_LANE = N_EXPERTS


def _pack_router(we, be, wg, bg):
    pad = LANES - N_EXPERTS - N_GROUPS
    w = jnp.concatenate([we, wg, jnp.zeros((D_MODEL, pad), F32)], axis=1)
    b = jnp.concatenate([be, bg, jnp.zeros((pad,), F32)]).reshape(1, LANES)
    hi = w.astype(BF16)
    lo = (w - hi.astype(F32)).astype(BF16)
    return jnp.concatenate([hi, lo], axis=1), b


def _lane_min_where(mask, lane):
    return jnp.min(jnp.where(mask, lane, LANES), axis=-1, keepdims=True)


def _sconv_tile(i, s_ref, before_ref, after_ref, w_ref):
    s = s_ref[...]
    p = s[:, W_GROUP:2 * W_GROUP] * s[:, 2 * W_GROUP:]
    p_before = before_ref[SUBLANES - 1:, W_GROUP:2 * W_GROUP] * before_ref[SUBLANES - 1:, 2 * W_GROUP:]
    p_after = after_ref[:1, W_GROUP:2 * W_GROUP] * after_ref[:1, 2 * W_GROUP:]
    row = _iota((TOK_TILE, 1), 0)
    seq_len = jnp.where(i < N_CTX_TILES, SEQ, DEC_SEQ)
    pos = (i * TOK_TILE + row) & (seq_len - 1)
    prev = jnp.where(row == 0, p_before, pltpu.roll(p, 1, 0))
    nxt = jnp.where(row == TOK_TILE - 1, p_after, pltpu.roll(p, TOK_TILE - 1, 0))
    prev = jnp.where(pos == 0, 0.0, prev)
    nxt = jnp.where(pos == seq_len - 1, 0.0, nxt)
    return s[:, :W_GROUP] * (w_ref[0:1, :] * prev + w_ref[1:2, :] * p + w_ref[2:3, :] * nxt)


def _outproj_kernel(xc_ref, xl_ref, m0, m1, m2, s_ref, before_ref, after_ref, cw_ref, mod_ref, nw_ref, w_ref,
                    rw_ref, rb_ref, x_out, hf_out, route_out, w_s):
    tile_i = pl.program_id(0)

    @pl.when(tile_i == 0)
    def _():
        w_s[...] = w_ref[0].astype(BF16)

    mixed = [m0[...], m1[...], m2[...], _sconv_tile(tile_i, s_ref, before_ref, after_ref, cw_ref)]
    acc = None
    for i, m in enumerate(mixed):
        part = _mm(m.astype(BF16), w_s[i * W_GROUP:(i + 1) * W_GROUP, :])
        acc = part if acc is None else acc + part
    x = _token_tile(tile_i, xc_ref, xl_ref) + mod_ref[0, 2:3, :] * acc
    x_out[...] = x
    y = x * lax.rsqrt(jnp.mean(x * x, axis=-1, keepdims=True) + EPS) * nw_ref[...]
    hf = y * (1.0 + mod_ref[0, 4:5, :]) + mod_ref[0, 3:4, :]
    hf_hi = hf.astype(BF16)
    hf_out[...] = hf_hi

    hf_lo = (hf - hf_hi.astype(F32)).astype(BF16)
    both = _mm(hf_hi, rw_ref[...])
    logits = both[:, :LANES] + both[:, LANES:] + _mm(hf_lo, rw_ref[:, :LANES]) + rb_ref[...]
    lane = _iota(logits.shape, 1)
    is_g = (lane >= N_EXPERTS) & (lane < N_EXPERTS + N_GROUPS)
    gl = jnp.where(is_g, logits, NEG_BIG)
    ge = jnp.exp(gl - jnp.max(gl, axis=-1, keepdims=True))
    gp = jnp.where(is_g, ge / jnp.sum(ge, axis=-1, keepdims=True), -1.0)
    gw = jnp.max(gp, axis=-1, keepdims=True)
    gsel = _lane_min_where(gp == gw, lane) - N_EXPERTS
    in_grp = (lane // EXPERTS_PER_GROUP == gsel) & (lane < N_EXPERTS)
    el = jnp.where(in_grp, logits, NEG_BIG)
    ee = jnp.exp(el - jnp.max(el, axis=-1, keepdims=True))
    ep = jnp.where(in_grp, ee / jnp.sum(ee, axis=-1, keepdims=True), -1.0)
    t1 = jnp.max(ep, axis=-1, keepdims=True)
    i1 = _lane_min_where(ep == t1, lane)
    ep2 = jnp.where(lane == i1, -1.0, ep)
    t2 = jnp.max(ep2, axis=-1, keepdims=True)
    i2 = _lane_min_where(ep2 == t2, lane)
    tsum = t1 + t2
    combine = jnp.where(lane == i1, gw * (t1 / tsum), 0.0) + jnp.where(lane == i2, gw * (t2 / tsum), 0.0)
    route_out[...] = jnp.where(lane == GSEL_LANE, gsel.astype(F32), combine)


def _outproj(x_ctx, x_lat, split, mixed, a_sc, sc_w, mod, norm_w, w_out, layer, rw, rb):
    tile = lambda w: pl.BlockSpec((TOK_TILE, w), lambda i: (i, 0))
    whole = lambda a: pl.BlockSpec(a.shape, lambda i: (0,) * a.ndim)
    per_tile = TOK_TILE // SUBLANES
    halo = lambda index: pl.BlockSpec((SUBLANES, 3 * W_GROUP), lambda i: (index(i), 0))
    return pl.pallas_call(
        _outproj_kernel,
        grid=(N_TOK // TOK_TILE,),
        in_specs=_token_specs(split) + [tile(W_GROUP)] * 3
                 + [tile(3 * W_GROUP), halo(lambda i: jnp.maximum(i * per_tile - 1, 0)),
                    halo(lambda i: jnp.minimum((i + 1) * per_tile, N_TOK // SUBLANES - 1)), whole(sc_w)]
                 + [pl.BlockSpec((1, SUBLANES, D_MODEL), lambda i: (_cond_of_tile(i), 0, 0)),
                    pl.BlockSpec((1, D_MODEL), lambda i: (0, 0)),
                    pl.BlockSpec((1, D_MODEL, D_MODEL), lambda i: (layer, 0, 0), pipeline_mode=pl.Buffered(1)),
                    whole(rw), whole(rb)],
        out_specs=[tile(D_MODEL), tile(D_MODEL), tile(LANES)],
        out_shape=[jax.ShapeDtypeStruct((N_TOK, D_MODEL), F32), jax.ShapeDtypeStruct((N_TOK, D_MODEL), BF16),
                   jax.ShapeDtypeStruct((N_TOK, LANES), F32)],
        scratch_shapes=[pltpu.VMEM((D_MODEL, D_MODEL), BF16)],
        compiler_params=_cparams("arbitrary"),
        name="outproj",
    )(x_ctx, x_lat, *mixed, a_sc, a_sc, a_sc, sc_w, mod, norm_w.reshape(1, D_MODEL), w_out, rw, rb)


SEG_BLK = 16
LOCAL_ROWS = TOK_TILE + N_GROUPS * SEG_BLK
N_TOK_TILES = N_TOK // TOK_TILE
MOE_ROWS = -(-(N_TOK + N_TOK_TILES * N_GROUPS * (SEG_BLK - 1) + N_GROUPS * (MOE_TILE - 1)) // MOE_TILE) * MOE_TILE


def _moe_tables(gsel):
    groups = jnp.arange(N_GROUPS, dtype=jnp.int32)
    onehot = (gsel.reshape(N_TOK_TILES, TOK_TILE, 1) == groups).astype(jnp.int32)
    earlier = jnp.asarray(np.tril(np.ones((TOK_TILE, TOK_TILE), np.float32), -1))
    rank = jnp.einsum('ts,nsg->ntg', earlier, onehot.astype(F32)).astype(jnp.int32)
    nblk = (jnp.sum(onehot, axis=1) + SEG_BLK - 1) // SEG_BLK
    loc_blk = jnp.cumsum(nblk, axis=1) - nblk
    blocks_per_tile = MOE_TILE // SEG_BLK
    grp_tiles = (jnp.sum(nblk, axis=0) + blocks_per_tile - 1) // blocks_per_tile
    grp_tile_start = jnp.cumsum(grp_tiles) - grp_tiles
    dst_blk = grp_tile_start[None, :] * blocks_per_tile + jnp.cumsum(nblk, axis=0) - nblk
    local_pos = jnp.sum(onehot * (loc_blk[:, None, :] * SEG_BLK + rank), axis=2)
    tile_idx = jnp.arange(MOE_ROWS // MOE_TILE, dtype=jnp.int32)
    tile_group = jnp.clip(jnp.sum(tile_idx[:, None] >= grp_tile_start[None, :], axis=1) - 1, 0, N_GROUPS - 1)
    in_group = tile_idx - grp_tile_start[tile_group]
    tile_rows = jnp.clip(jnp.sum(nblk, axis=0)[tile_group] * SEG_BLK - in_group * MOE_TILE, 0, MOE_TILE)
    flat = lambda a: a.reshape(-1).astype(jnp.int32)
    return local_pos.astype(jnp.int32), flat(nblk), flat(loc_blk), flat(dst_blk), flat(tile_group), flat(tile_rows)


def _segment_copies(t, nblk, loc_blk, dst_blk, make_copies, action):
    for g in range(N_GROUPS):
        k = t * N_GROUPS + g

        @pl.loop(0, nblk[k])
        def _(b):
            local = pl.multiple_of((loc_blk[k] + b) * SEG_BLK, SEG_BLK)
            sorted_row = pl.multiple_of((dst_blk[k] + b) * SEG_BLK, SEG_BLK)
            for cp in make_copies(local, sorted_row):
                action(cp)


def _dispatch_kernel(nblk, loc_blk, dst_blk, hf_ref, rt_ref, lp_ref, xs_in, rs_in, xs_hbm, rs_hbm, xbuf, rbuf, sem):
    t = pl.program_id(0)
    slot = t % 2
    onehot = _iota((LOCAL_ROWS, TOK_TILE), 0) == lp_ref[0]
    xbuf[slot] = _mm(onehot.astype(BF16), hf_ref[...]).astype(BF16)
    route = rt_ref[...]
    route_hi = route.astype(BF16)
    route_lo = (route - route_hi.astype(F32)).astype(BF16)
    rbuf[slot] = _mm(onehot.astype(BF16), route_hi) + _mm(onehot.astype(BF16), route_lo)

    def copies_of(s):
        def copies(local, sorted_row):
            return (pltpu.make_async_copy(xbuf.at[s, pl.ds(local, SEG_BLK)], xs_hbm.at[pl.ds(sorted_row, SEG_BLK)], sem.at[s]),
                    pltpu.make_async_copy(rbuf.at[s, pl.ds(local, SEG_BLK)], rs_hbm.at[pl.ds(sorted_row, SEG_BLK)], sem.at[s]))
        return copies

    @pl.when(t > 0)
    def _():
        _segment_copies(t - 1, nblk, loc_blk, dst_blk, copies_of(1 - slot), lambda cp: cp.wait())

    _segment_copies(t, nblk, loc_blk, dst_blk, copies_of(slot), lambda cp: cp.start())

    @pl.when(t == N_TOK_TILES - 1)
    def _():
        _segment_copies(t, nblk, loc_blk, dst_blk, copies_of(slot), lambda cp: cp.wait())


def _dispatch(hf, route, local_pos, nblk, loc_blk, dst_blk):
    grid_spec = pltpu.PrefetchScalarGridSpec(
        num_scalar_prefetch=3,
        grid=(N_TOK_TILES,),
        in_specs=[pl.BlockSpec((TOK_TILE, D_MODEL), lambda t, *_: (t, 0)),
                  pl.BlockSpec((TOK_TILE, LANES), lambda t, *_: (t, 0)),
                  pl.BlockSpec((1, 1, TOK_TILE), lambda t, *_: (t, 0, 0)),
                  pl.BlockSpec(memory_space=pl.ANY), pl.BlockSpec(memory_space=pl.ANY)],
        out_specs=[pl.BlockSpec(memory_space=pl.ANY), pl.BlockSpec(memory_space=pl.ANY)],
        scratch_shapes=[pltpu.VMEM((2, LOCAL_ROWS, D_MODEL), BF16), pltpu.VMEM((2, LOCAL_ROWS, LANES), F32),
                        pltpu.SemaphoreType.DMA((2,))],
    )
    return pl.pallas_call(
        _dispatch_kernel,
        grid_spec=grid_spec,
        out_shape=[jax.ShapeDtypeStruct((MOE_ROWS, D_MODEL), BF16), jax.ShapeDtypeStruct((MOE_ROWS, LANES), F32)],
        input_output_aliases={6: 0, 7: 1},
        compiler_params=_cparams("arbitrary"),
        name="dispatch",
    )(nblk, loc_blk, dst_blk, hf, route, local_pos.reshape(N_TOK_TILES, 1, TOK_TILE),
      jnp.zeros((MOE_ROWS, D_MODEL), BF16), jnp.zeros((MOE_ROWS, LANES), F32))


def _moe_kernel(tile_group, tile_rows, x_ref, r_ref, wg_hbm, wu_hbm, wd_hbm, y_ref,
                wg_b, wu_b, wd_b, stage_g, stage_u, stage_d, sem, *, layer):
    i = pl.program_id(0)
    g = tile_group[i]
    group_row = layer * N_GROUPS + g
    new_group = (i == 0) | (g != tile_group[jnp.maximum(i - 1, 0)])
    valid = tile_rows[i] > 0
    half = tile_rows[i] <= MOE_TILE // 2

    def weight_copies(e):
        slot = e % 2
        return (pltpu.make_async_copy(wg_hbm.at[group_row, e], stage_g.at[slot], sem.at[slot]),
                pltpu.make_async_copy(wu_hbm.at[group_row, e], stage_u.at[slot], sem.at[slot]),
                pltpu.make_async_copy(wd_hbm.at[group_row, e], stage_d.at[slot], sem.at[slot]))

    def run(load_weights, rows):
        x = x_ref[:rows, :]
        route = r_ref[:rows, :]
        lane = _iota(route.shape, 1)
        acc = jnp.zeros((rows, D_MODEL), F32)
        if load_weights:
            for e in range(2):
                for cp in weight_copies(e):
                    cp.start()
        for e in range(EXPERTS_PER_GROUP):
            if load_weights:
                for cp in weight_copies(e):
                    cp.wait()
                wg_b[e] = stage_g[e % 2].astype(BF16)
                wu_b[e] = stage_u[e % 2].astype(BF16)
                wd_b[e] = stage_d[e % 2].astype(BF16)
                if e + 2 < EXPERTS_PER_GROUP:
                    for cp in weight_copies(e + 2):
                        cp.start()
            cw = jnp.sum(jnp.where(lane == g * EXPERTS_PER_GROUP + e, route, 0.0), axis=-1, keepdims=True)
            act = _silu(_mm(x, wg_b[e])) * _mm(x, wu_b[e]) * cw
            acc = acc + _mm(act.astype(BF16), wd_b[e])
        y_ref[:rows, :] = acc
        if rows < MOE_TILE:
            y_ref[rows:, :] = jnp.zeros((MOE_TILE - rows, D_MODEL), F32)

    for load_weights in (True, False):
        for rows in (MOE_TILE, MOE_TILE // 2):
            first = new_group if load_weights else jnp.logical_not(new_group)
            fits = half if rows < MOE_TILE else jnp.logical_not(half)

            @pl.when(valid & first & fits)
            def _():
                run(load_weights, rows)

    @pl.when(jnp.logical_not(valid))
    def _():
        y_ref[...] = jnp.zeros_like(y_ref)


def _moe(xs, rs, tile_group, tile_rows, wg, wu, wd, layer):
    any_spec = pl.BlockSpec(memory_space=pl.ANY)
    grid_spec = pltpu.PrefetchScalarGridSpec(
        num_scalar_prefetch=2,
        grid=(MOE_ROWS // MOE_TILE,),
        in_specs=[pl.BlockSpec((MOE_TILE, D_MODEL), lambda i, tg, tv: (i, 0)),
                  pl.BlockSpec((MOE_TILE, LANES), lambda i, tg, tv: (i, 0)),
                  any_spec, any_spec, any_spec],
        out_specs=pl.BlockSpec((MOE_TILE, D_MODEL), lambda i, tg, tv: (i, 0)),
        scratch_shapes=[pltpu.VMEM((EXPERTS_PER_GROUP, D_MODEL, EXPERT_FF), BF16),
                        pltpu.VMEM((EXPERTS_PER_GROUP, D_MODEL, EXPERT_FF), BF16),
                        pltpu.VMEM((EXPERTS_PER_GROUP, EXPERT_FF, D_MODEL), BF16),
                        pltpu.VMEM((2, D_MODEL, EXPERT_FF), F32), pltpu.VMEM((2, D_MODEL, EXPERT_FF), F32),
                        pltpu.VMEM((2, EXPERT_FF, D_MODEL), F32), pltpu.SemaphoreType.DMA((2,))],
    )
    return pl.pallas_call(
        functools.partial(_moe_kernel, layer=layer),
        grid_spec=grid_spec,
        out_shape=jax.ShapeDtypeStruct((MOE_ROWS, D_MODEL), F32),
        compiler_params=_cparams("arbitrary"),
        name="moe",
    )(tile_group, tile_rows, xs, rs, wg, wu, wd)


def _combine_kernel(nblk, loc_blk, dst_blk, x_ref, lp_ref, mod_ref, ys_hbm, *refs, split):
    o_refs, ybuf, sem = refs[:-2], refs[-2], refs[-1]
    t = pl.program_id(0)
    slot = t % 2

    def fetch(tile, s):
        def copies(local, sorted_row):
            return (pltpu.make_async_copy(ys_hbm.at[pl.ds(sorted_row, SEG_BLK)], ybuf.at[s, pl.ds(local, SEG_BLK)], sem.at[s]),)
        ybuf[s] = jnp.zeros((LOCAL_ROWS, D_MODEL), F32)
        _segment_copies(tile, nblk, loc_blk, dst_blk, copies, lambda cp: cp.start())

    @pl.when(t == 0)
    def _():
        fetch(0, 0)

    @pl.when(t + 1 < N_TOK_TILES)
    def _():
        fetch(t + 1, 1 - slot)

    def copies_now(local, sorted_row):
        return (pltpu.make_async_copy(ys_hbm.at[pl.ds(sorted_row, SEG_BLK)], ybuf.at[slot, pl.ds(local, SEG_BLK)], sem.at[slot]),)

    _segment_copies(t, nblk, loc_blk, dst_blk, copies_now, lambda cp: cp.wait())

    onehot = (_iota((TOK_TILE, LOCAL_ROWS), 1) == lp_ref[...]).astype(BF16)
    ys = ybuf[slot]
    hi = ys.astype(BF16)
    lo = (ys - hi.astype(F32)).astype(BF16)
    y = _mm(onehot, hi) + _mm(onehot, lo)
    out = x_ref[...] + mod_ref[0, 5:6, :] * y
    if split:
        @pl.when(t < N_CTX_TILES)
        def _():
            o_refs[0][...] = out

        @pl.when(t >= N_CTX_TILES)
        def _():
            o_refs[1][...] = out
    else:
        o_refs[0][...] = out


def _combine(x, ys, mod, local_pos, nblk, loc_blk, dst_blk, split):
    tile = pl.BlockSpec((TOK_TILE, D_MODEL), lambda t, *_: (t, 0))
    if split:
        out_specs = _token_specs(True)
        out_shape = [jax.ShapeDtypeStruct((N_CTX, D_MODEL), F32), jax.ShapeDtypeStruct((N_LAT, D_MODEL), F32)]
    else:
        out_specs, out_shape = [tile], [jax.ShapeDtypeStruct((N_TOK, D_MODEL), F32)]
    grid_spec = pltpu.PrefetchScalarGridSpec(
        num_scalar_prefetch=3,
        grid=(N_TOK_TILES,),
        in_specs=[tile, pl.BlockSpec((TOK_TILE, 1), lambda t, *_: (t, 0)),
                  pl.BlockSpec((1, SUBLANES, D_MODEL), lambda t, *_: (_cond_of_tile(t), 0, 0)),
                  pl.BlockSpec(memory_space=pl.ANY)],
        out_specs=out_specs,
        scratch_shapes=[pltpu.VMEM((2, LOCAL_ROWS, D_MODEL), F32), pltpu.SemaphoreType.DMA((2,))],
    )
    return pl.pallas_call(
        functools.partial(_combine_kernel, split=split),
        grid_spec=grid_spec,
        out_shape=out_shape,
        compiler_params=_cparams("arbitrary"),
        name="combine",
    )(nblk, loc_blk, dst_blk, x, local_pos.reshape(N_TOK, 1), mod, ys)


def _lane_row(v):
    v = v.reshape(-1).astype(F32)
    return jnp.concatenate([v, jnp.zeros((LANES - v.shape[0],), F32)]).reshape(1, LANES)


def _pad_rows(w):
    return jnp.concatenate([w, jnp.zeros((SUBLANES - w.shape[0], w.shape[1]), w.dtype)], axis=0)


def kernel(x_prompt, x_sample, state_gdn, state_ret, cache_nat_k, cache_nat_v, c, c_ctx, ada_w, ada_b, norm_mix_w, norm_ffn_w, w_in, gdn_conv_w, gdn_a_log, gdn_dt_bias, gdn_norm_w, ret_gamma_logit, nat_q_norm_w, nat_k_norm_w, nat_rpb, sc_conv_w, w_out, router_group_w, router_group_b, router_expert_w, router_expert_b, moe_w_gate, moe_w_up, moe_w_down):
    x_ctx, x_lat, split = x_prompt.reshape(N_CTX, D_MODEL), x_sample.reshape(N_LAT, D_MODEL), True
    cond = jnp.concatenate([c_ctx[None, :], c], axis=0)
    ada = _ada(cond, ada_w, ada_b).reshape(DEPTH, SUBLANES, 6, D_MODEL)
    cos, sin = _rope_tables(DEC_SEQ)
    zero_state = jnp.zeros((BATCH, N_GATE, HEAD_DIM, HEAD_DIM), F32)
    lat_block = N_CTX // DEC_SEQ
    s_gdn, s_ret, caches = None, None, None
    mixed = [jnp.zeros((N_TOK, W_GROUP), F32) for _ in range(3)]
    for l in range(DEPTH):
        mod = jnp.concatenate([ada[l, :1 + DEC_BATCH], jnp.zeros((1 + DEC_BATCH, SUBLANES - 6, D_MODEL), F32)], axis=1)
        a_gdn, a_ret, a_nat, a_sc, a_gate = _inproj(x_ctx, x_lat, split, mod, norm_mix_w[l], w_in, l)

        conv_w = _pad_rows(gdn_conv_w[l])
        a_log, dt_b = _lane_row(gdn_a_log[l]), _lane_row(gdn_dt_bias[l])
        gnw = jnp.tile(gdn_norm_w[l], N_HEADS).reshape(1, W_GROUP)
        o_gdn, s_gdn = _gdn(a_gdn, a_gate, conv_w, a_log, dt_b, gnw, zero_state, SEQ, BATCH, 0, n_sub=2,
                            into=mixed[0], layer=l, states=s_gdn)
        s0 = state_gdn[:, l].reshape(DEC_BATCH, N_GATE, HEAD_DIM, HEAD_DIM)
        o_gdn, _ = _gdn(a_gdn, a_gate, conv_w, a_log, dt_b, gnw, s0, DEC_SEQ, DEC_BATCH, lat_block, into=o_gdn)

        logit = jnp.repeat(ret_gamma_logit[l].astype(F32), HEAD_DIM, axis=1)
        o_ret, s_ret = _ret(a_ret, logit, zero_state, cos[:SEQ], sin[:SEQ], SEQ, BATCH, 0, False, into=mixed[1],
                            layer=l, states=s_ret)
        s0 = state_ret[:, l].reshape(DEC_BATCH, N_GATE, HEAD_DIM, HEAD_DIM)
        o_ret, _ = _ret(a_ret, logit, s0, cos, sin, DEC_SEQ, DEC_BATCH, lat_block, True, into=o_ret)

        qw = jnp.tile(nat_q_norm_w[l], N_HEADS).reshape(1, W_GROUP)
        kw = jnp.tile(nat_k_norm_w[l], N_HEADS).reshape(1, W_GROUP)
        o_nat, *caches = _ctx_attn(a_nat, qw, kw, mixed[2], l, caches)
        o_nat = _nat(a_nat, cache_nat_k[:, l].reshape(DEC_BATCH, PAST_LEN, W_GROUP),
                     cache_nat_v[:, l].reshape(DEC_BATCH, PAST_LEN, W_GROUP), _nat_tables(nat_rpb[l]), qw, kw, o_nat)

        mixed = [o_gdn, o_ret, o_nat]
        rw, rb = _pack_router(router_expert_w[l], router_expert_b[l], router_group_w[l], router_group_b[l])
        x_mid, hf, route = _outproj(x_ctx, x_lat, split, mixed, a_sc, _pad_rows(sc_conv_w[l]), mod, norm_ffn_w[l],
                                    w_out, l, rw, rb)

        local_pos, nblk, loc_blk, dst_blk, tile_group, tile_rows = _moe_tables(route[:, GSEL_LANE].astype(jnp.int32))
        xs, rs = _dispatch(hf, route, local_pos, nblk, loc_blk, dst_blk)
        to_group = lambda w: w.reshape((DEPTH * N_GROUPS, EXPERTS_PER_GROUP) + w.shape[2:])
        ys = _moe(xs, rs, tile_group, tile_rows, to_group(moe_w_gate), to_group(moe_w_up), to_group(moe_w_down), l)
        last = l == DEPTH - 1
        out = _combine(x_mid, ys, mod, local_pos, nblk, loc_blk, dst_blk, split=last)
        x_ctx, x_lat, split = (out[0], out[1], True) if last else (out[0], out[0], False)

    stacked = lambda s: s.reshape(BATCH, DEPTH, 2, N_HEADS, HEAD_DIM, HEAD_DIM)
    new_k, new_v = [a.reshape(BATCH, DEPTH, N_HEADS, HEAD_DIM, SEQ).transpose(0, 1, 4, 2, 3) for a in caches]
    return (x_ctx.reshape(BATCH, SEQ, D_MODEL), x_lat.reshape(DEC_BATCH, DEC_SEQ, D_MODEL),
            stacked(s_gdn), stacked(s_ret), new_k, new_v)
```

```python
import functools

import numpy as np
import jax
import jax.numpy as jnp
from jax import lax
from jax.experimental import pallas as pl
from jax.experimental.pallas import tpu as pltpu

D_MODEL = 1024
BATCH = 16
SEQ = 256
DEPTH = 2
DEC_BATCH = 2
DEC_SEQ = 1024
PAST_LEN = 256
GRID_W = 64
HEAD_DIM = 64
W_GROUP = D_MODEL // 4
N_HEADS = W_GROUP // HEAD_DIM
CHUNK = 64
WIN_ROWS = 8
WIN_COLS = 16
ROPE_BASE = 10000.0
N_GROUPS = 4
EXPERTS_PER_GROUP = 8
N_EXPERTS = N_GROUPS * EXPERTS_PER_GROUP
EXPERT_FF = 256
EPS = 1e-6

N_CTX = BATCH * SEQ
N_LAT = DEC_BATCH * DEC_SEQ
N_TOK = N_CTX + N_LAT
LANES = 128
SUBLANES = 8
TOK_TILE = 512
MOE_TILE = 512
Q_TILE = 256
VMEM_LIMIT = 48 * 1024 * 1024
NEG_BIG = -1e30
N_GATE = 2 * N_HEADS

F32 = jnp.float32
BF16 = jnp.bfloat16
HI = lax.Precision.HIGHEST


def _mm(a, b, prec=None):
    return lax.dot_general(a, b, (((1,), (0,)), ((), ())), precision=prec, preferred_element_type=F32)


def _mm_tn(a, b, prec=None):
    return lax.dot_general(a, b, (((0,), (0,)), ((), ())), precision=prec, preferred_element_type=F32)


def _bmm(a, b):
    return _mm(a.astype(BF16), b.astype(BF16))


def _sigmoid(x):
    return 1.0 / (1.0 + jnp.exp(-x))


def _silu(x):
    return x * _sigmoid(x)


def _softplus(x):
    return jnp.maximum(x, 0.0) + jnp.log(1.0 + jnp.exp(-jnp.abs(x)))


def _iota(shape, dim):
    return lax.broadcasted_iota(jnp.int32, shape, dim)


def _cparams(*sem):
    return pltpu.CompilerParams(dimension_semantics=sem, vmem_limit_bytes=VMEM_LIMIT)


def _cond_of_tile(i):
    n_ctx_tiles = N_CTX // TOK_TILE
    return jnp.where(i < n_ctx_tiles, 0, 1 + (i - n_ctx_tiles) // (DEC_SEQ // TOK_TILE))


def _head_sum_matrix():
    return (_iota((W_GROUP, W_GROUP), 0) // HEAD_DIM == _iota((W_GROUP, W_GROUP), 1) // HEAD_DIM).astype(BF16)


def _head_sums(x, head_sum):
    hi = x.astype(BF16)
    lo = (x - hi.astype(F32)).astype(BF16)
    return _mm(hi, head_sum) + _mm(lo, head_sum)


N_COND = 1 + DEC_BATCH
ADA_TN = 1536


ADA_BUFS = 3
ADA_CHUNKS = 6 * D_MODEL // ADA_TN


def _ada_kernel(c_ref, w_hbm, b_ref, o_ref, wbuf, sem):
    steps = [(l, j) for l in range(DEPTH) for j in range(ADA_CHUNKS)]

    def chunk_copy(step):
        l, j = steps[step]
        slot = step % ADA_BUFS
        return pltpu.make_async_copy(w_hbm.at[l, :, pl.ds(j * ADA_TN, ADA_TN)], wbuf.at[slot], sem.at[slot])

    for step in range(min(ADA_BUFS, len(steps))):
        chunk_copy(step).start()
    for step, (l, j) in enumerate(steps):
        slot = step % ADA_BUFS
        chunk_copy(step).wait()

        def slab(s, accs):
            rows = pl.ds(pl.multiple_of(s * SUBLANES, SUBLANES), SUBLANES)
            w = wbuf[slot, rows, :]
            return tuple(acc + w * jnp.tile(_silu(c_ref[r, rows, :]), (1, ADA_TN // LANES))
                         for r, acc in enumerate(accs))

        zero = jnp.zeros((SUBLANES, ADA_TN), F32)
        accs = lax.fori_loop(0, D_MODEL // SUBLANES, slab, (zero,) * N_COND, unroll=4)
        out = jnp.concatenate([jnp.sum(acc, axis=0, keepdims=True) for acc in accs]
                              + [jnp.zeros((SUBLANES - N_COND, ADA_TN), F32)], axis=0)
        cols = slice(j * ADA_TN, (j + 1) * ADA_TN)
        o_ref[l, :, cols] = out + b_ref[l, :, cols]
        if step + ADA_BUFS < len(steps):
            chunk_copy(step + ADA_BUFS).start()


def _ada(cond, ada_w, ada_b):
    n_out = 6 * D_MODEL
    cond_lanes = jnp.broadcast_to(cond[:, :, None], (N_COND, D_MODEL, LANES))
    whole = lambda shape: pl.BlockSpec(shape, lambda: (0,) * len(shape))
    return pl.pallas_call(
        _ada_kernel,
        in_specs=[whole((N_COND, D_MODEL, LANES)), pl.BlockSpec(memory_space=pl.ANY), whole((DEPTH, 1, n_out))],
        out_specs=whole((DEPTH, SUBLANES, n_out)),
        out_shape=jax.ShapeDtypeStruct((DEPTH, SUBLANES, n_out), F32),
        scratch_shapes=[pltpu.VMEM((ADA_BUFS, D_MODEL, ADA_TN), F32), pltpu.SemaphoreType.DMA((ADA_BUFS,))],
        compiler_params=pltpu.CompilerParams(vmem_limit_bytes=VMEM_LIMIT),
        name="ada",
    )(cond_lanes, ada_w, ada_b.reshape(DEPTH, 1, n_out))


IN_WIDTHS = (4 * W_GROUP, 4 * W_GROUP, 3 * W_GROUP, 3 * W_GROUP, LANES)
IN_PACKED = sum(IN_WIDTHS)


IN_TOTAL = 3 * W_GROUP + W_GROUP + 2 * N_GATE + 4 * W_GROUP + 3 * W_GROUP + 3 * W_GROUP
IN_GATE_SRC = 4 * W_GROUP
IN_SRC = (0, IN_GATE_SRC + 2 * N_GATE, IN_GATE_SRC + 2 * N_GATE + 4 * W_GROUP,
          IN_GATE_SRC + 2 * N_GATE + 7 * W_GROUP)
N_CTX_TILES = N_CTX // TOK_TILE


def _token_specs(split):
    lat0 = 0 if split else N_CTX_TILES
    return [pl.BlockSpec((TOK_TILE, D_MODEL), lambda i, *_: (jnp.minimum(i, N_CTX_TILES - 1), 0)),
            pl.BlockSpec((TOK_TILE, D_MODEL), lambda i, *_: (jnp.maximum(i, N_CTX_TILES) - N_CTX_TILES + lat0, 0))]


def _token_tile(i, ctx_ref, lat_ref):
    return jnp.where(i < N_CTX_TILES, ctx_ref[...], lat_ref[...])


def _inproj_kernel(xc_ref, xl_ref, mod_ref, nw_ref, w_ref, *refs):
    o_refs, w_s = refs[:-1], refs[-1]
    i = pl.program_id(0)

    @pl.when(i == 0)
    def _():
        piece = 256
        off = 0
        for src, width in zip(IN_SRC, IN_WIDTHS[:-1]):
            for c in range(0, width, piece):
                w_s[:, off + c:off + c + piece] = w_ref[0, src + c:src + c + piece, :].T.astype(BF16)
            off += width
        gate = w_ref[0, IN_GATE_SRC:IN_GATE_SRC + LANES, :].T
        w_s[:, off:] = jnp.where(_iota((D_MODEL, LANES), 1) < 2 * N_GATE, gate, 0.0).astype(BF16)

    x = _token_tile(i, xc_ref, xl_ref)
    y = x * lax.rsqrt(jnp.mean(x * x, axis=-1, keepdims=True) + EPS) * nw_ref[...]
    h = (y * (1.0 + mod_ref[0, 1:2, :]) + mod_ref[0, 0:1, :]).astype(BF16)
    off = 0
    for o_ref, width in zip(o_refs, IN_WIDTHS):
        o_ref[...] = _mm(h, w_s[:, off:off + width])
        off += width


def _inproj(x_ctx, x_lat, split, mod, norm_w, w_in, layer):
    return pl.pallas_call(
        _inproj_kernel,
        grid=(N_TOK // TOK_TILE,),
        in_specs=_token_specs(split)
                 + [pl.BlockSpec((1, SUBLANES, D_MODEL), lambda i: (_cond_of_tile(i), 0, 0)),
                    pl.BlockSpec((1, D_MODEL), lambda i: (0, 0)),
                    pl.BlockSpec((1, IN_TOTAL, D_MODEL), lambda i: (layer, 0, 0), pipeline_mode=pl.Buffered(1))],
        out_specs=[pl.BlockSpec((TOK_TILE, w), lambda i: (i, 0)) for w in IN_WIDTHS],
        out_shape=[jax.ShapeDtypeStruct((N_TOK, w), F32) for w in IN_WIDTHS],
        scratch_shapes=[pltpu.VMEM((D_MODEL, IN_PACKED), BF16)],
        compiler_params=_cparams("arbitrary"),
        name="inproj",
    )(x_ctx, x_lat, mod, norm_w.reshape(1, D_MODEL), jnp.swapaxes(w_in, 1, 2))


def _seq_call(kernel_fn, args, in_specs, out_specs, out_shape, into, more_into=(), **kwargs):
    donors = ([] if into is None else [(into, 0)]) + list(more_into)
    n_in = len(args)
    inner = kernel_fn
    kernel_fn = lambda *refs: inner(*refs[:n_in], *refs[n_in + len(donors):])
    aliases = {n_in + j: out_idx for j, (_, out_idx) in enumerate(donors)}
    args = list(args) + [a for a, _ in donors]
    in_specs = list(in_specs) + [pl.BlockSpec(memory_space=pl.ANY)] * len(donors)
    return pl.pallas_call(kernel_fn, in_specs=in_specs, out_specs=out_specs, out_shape=out_shape,
                          input_output_aliases=aliases, **kwargs)(*args)


def _shift_rows(p, seq_len):
    rows = p.shape[0]
    pos = _iota(p.shape, 0) & (seq_len - 1)
    prev = jnp.where(pos == 0, 0.0, pltpu.roll(p, 1, 0))
    nxt = jnp.where(pos == seq_len - 1, 0.0, pltpu.roll(p, rows - 1, 0))
    return prev, nxt


def _conv3(x, w_ref, seq_len):
    prev, nxt = _shift_rows(x, seq_len)
    return w_ref[0:1, :] * prev + w_ref[1:2, :] * x + w_ref[2:3, :] * nxt


def _chunk_scan(x, reverse):
    t = x.shape[0]
    pos = _iota(x.shape, 0) % CHUNK
    step = 1
    while step < CHUNK:
        if reverse:
            x = x + jnp.where(pos < CHUNK - step, pltpu.roll(x, t - step, 0), 0.0)
        else:
            x = x + jnp.where(pos >= step, pltpu.roll(x, step, 0), 0.0)
        step *= 2
    return x


GDN_GROUP_CHUNKS = 4
GDN_BASE_BLOCK = 4
GDN_CHAINS = GDN_GROUP_CHUNKS * N_GATE
GDN_PAIRS = GDN_GROUP_CHUNKS * N_HEADS


def _gdn_kernel(a_ref, gate_ref, convw_ref, alog_ref, dtb_ref, nw_ref, s0_ref, o_ref, sfin_ref,
                q_s, kv_s, kt_s, gc_s, eg_s, beta_s, gcrow_s, ekdrow_s, cdec_s, uo_s, wq_s, attn_s, kdt_s,
                st_s, wsqs_s, kk_s, d_s, m1_s, p_s, low_s, pb_s, rhs_s, *, t, n_sub):
    n_chunks = t // CHUNK
    a = a_ref[...]
    qkv = _silu(_conv3(a[:, :3 * W_GROUP], convw_ref, t // n_sub))
    q = qkv[:, :W_GROUP]
    k = qkv[:, W_GROUP:2 * W_GROUP]
    v = qkv[:, 2 * W_GROUP:]
    head_sum = _head_sum_matrix()
    q = q * lax.rsqrt(_head_sums(q * q, head_sum) + EPS) * (HEAD_DIM ** -0.5)
    k = k * lax.rsqrt(_head_sums(k * k, head_sum) + EPS)
    for h in range(N_HEADS):
        hs = slice(h * HEAD_DIM, (h + 1) * HEAD_DIM)
        q_s[h] = q[:, hs]
        kv_s[h] = jnp.concatenate([k[:, hs], v[:, hs]], axis=1)
    k_t = k.T
    for c in range(n_chunks):
        kt_s[c] = k_t[:, c * CHUNK:(c + 1) * CHUNK]

    gates = gate_ref[...]
    log_a = -jnp.exp(alog_ref[...]) * _softplus(gates + dtb_ref[...])
    beta_s[...] = _sigmoid(gates)

    ci = _iota((CHUNK, CHUNK), 0)
    cj = _iota((CHUNK, CHUNK), 1)
    eye = (ci == cj).astype(F32)
    blk_mask = (ci // GDN_BASE_BLOCK) == (cj // GDN_BASE_BLOCK)
    low_half = _iota((CHUNK, 2 * HEAD_DIM), 1) < HEAD_DIM

    prefix = _chunk_scan(log_a, reverse=False)
    suffix = _chunk_scan(log_a, reverse=True)
    gc = jnp.where(_iota((t, LANES), 1) < N_HEADS, prefix, suffix)
    gt = prefix + suffix - log_a
    gc_s[...] = gc
    eg_s[...] = jnp.exp(gc)
    gc_t = gc.T
    ekd_t = jnp.exp(gt - gc).T
    cdec_t = jnp.exp(gt).T
    for c in range(n_chunks):
        lanes = slice(c * CHUNK, (c + 1) * CHUNK)
        gcrow_s[c] = gc_t[:N_GATE, lanes]
        ekdrow_s[c] = ekd_t[:N_GATE, lanes]
        cdec_s[c] = jnp.concatenate([cdec_t[:N_GATE, lanes]] * 2, axis=1)

    def solve_group(grp, carry):
        row0 = grp * (GDN_GROUP_CHUNKS * CHUNK)
        chains = [(cl, a_idx) for cl in range(GDN_GROUP_CHUNKS) for a_idx in range(N_GATE)]

        def rows_of(cl):
            return pl.ds(pl.multiple_of(row0 + cl * CHUNK, CHUNK), CHUNK)

        for cl in range(GDN_GROUP_CHUNKS):
            for h in range(N_HEADS):
                rows = rows_of(cl)
                kq = jnp.concatenate([kv_s[h, rows, :HEAD_DIM], q_s[h, rows, :]], axis=0)
                k_t_h = kt_s[grp * GDN_GROUP_CHUNKS + cl, h * HEAD_DIM:(h + 1) * HEAD_DIM, :]
                kk_s[cl * N_HEADS + h] = _bmm(kq, k_t_h)
        for b, (cl, a_idx) in enumerate(chains):
            backward = a_idx >= N_HEADS
            h = a_idx % N_HEADS
            rows, c = rows_of(cl), grp * GDN_GROUP_CHUNKS + cl
            incl = (cj >= ci) if backward else (cj <= ci)
            strict = (cj > ci) if backward else (cj < ci)
            bt = beta_s[rows, N_GATE + a_idx:N_GATE + a_idx + 1]
            decay = jnp.exp(jnp.where(incl, gc_s[rows, a_idx:a_idx + 1] - gcrow_s[c, a_idx:a_idx + 1, :], NEG_BIG))
            low = jnp.where(strict, kk_s[cl * N_HEADS + h, :CHUNK, :] * bt * decay, 0.0)
            attn_s[a_idx, rows, :] = (kk_s[cl * N_HEADS + h, CHUNK:, :] * decay).astype(BF16)
            d_s[b] = jnp.where(blk_mask, low, 0.0)
            low_s[b] = low.astype(BF16)
            rhs_s[b] = (kv_s[h, rows, :] * bt
                        * jnp.where(low_half, eg_s[rows, a_idx:a_idx + 1], 1.0)).astype(BF16)
        for b in range(GDN_CHAINS):
            m1_s[b] = _bmm(d_s[b], d_s[b])
        for b in range(GDN_CHAINS):
            d, d2 = d_s[b], m1_s[b]
            p_s[b] = eye - d + d2 - _bmm(d, d2)
        size = GDN_BASE_BLOCK
        while size < CHUNK:
            pair = ((ci // size) != (cj // size)) & ((ci // (2 * size)) == (cj // (2 * size)))
            for b in range(GDN_CHAINS):
                coupling = jnp.where(pair, low_s[b], jnp.zeros((), BF16))
                pb_s[b] = _mm(p_s[b].astype(BF16), coupling).astype(BF16)
            for b in range(GDN_CHAINS):
                p = p_s[b]
                p_s[b] = p - _mm(pb_s[b], p.astype(BF16))
            size *= 2
        for b, (cl, a_idx) in enumerate(chains):
            h = a_idx % N_HEADS
            rows, c = rows_of(cl), grp * GDN_GROUP_CHUNKS + cl
            wu = _mm(p_s[b].astype(BF16), rhs_s[b])
            uo_s[a_idx, rows, :] = wu
            wq_s[a_idx, c, :CHUNK, :] = wu[:, :HEAD_DIM].astype(BF16)
            wq_s[a_idx, c, CHUNK:, :] = (q_s[h, rows, :] * eg_s[rows, a_idx:a_idx + 1]).astype(BF16)
            k_t_h = kt_s[c, h * HEAD_DIM:(h + 1) * HEAD_DIM, :]
            kdt_s[a_idx, c] = (k_t_h * ekdrow_s[c, a_idx:a_idx + 1, :]).astype(BF16)
        return carry

    lax.fori_loop(0, n_chunks // GDN_GROUP_CHUNKS, solve_group, 0)

    seq_chunks = n_chunks // n_sub
    chains = [(sub, a_idx) for sub in range(n_sub) for a_idx in range(N_GATE)]
    for j, (sub, a_idx) in enumerate(chains):
        st_s[j] = jnp.concatenate([jnp.zeros((HEAD_DIM, HEAD_DIM), F32), s0_ref[sub, a_idx]], axis=1)

    def scan_chunk(c, carry):
        def chunk_of(sub, a_idx):
            return sub * seq_chunks + ((seq_chunks - 1 - c) if a_idx >= N_HEADS else c)

        for j, (sub, a_idx) in enumerate(chains):
            wsqs_s[j] = _mm(wq_s[a_idx, chunk_of(sub, a_idx)], st_s[j].astype(BF16))
        for j, (sub, a_idx) in enumerate(chains):
            cc = chunk_of(sub, a_idx)
            rows = pl.ds(pl.multiple_of(cc * CHUNK, CHUNK), CHUNK)
            v_new = (uo_s[a_idx, rows, :] - wsqs_s[j, :CHUNK, :]).astype(BF16)
            uo_s[a_idx, rows, :] = wsqs_s[j, CHUNK:, :] + _mm(attn_s[a_idx, rows, :], v_new)
            st_s[j] = st_s[j] * cdec_s[cc, a_idx:a_idx + 1, :] + _mm(kdt_s[a_idx, cc], v_new)
        return carry

    lax.fori_loop(0, seq_chunks, scan_chunk, 0)
    for j, (sub, a_idx) in enumerate(chains):
        sfin_ref[sub, a_idx] = st_s[j, :, HEAD_DIM:]

    o = jnp.concatenate([(uo_s[h] + uo_s[N_HEADS + h])[:, HEAD_DIM:] for h in range(N_HEADS)], axis=1)
    ms = _head_sums(o * o, head_sum) * (1.0 / HEAD_DIM)
    o_ref[...] = o * lax.rsqrt(ms + EPS) * nw_ref[...] * _silu(a[:, 3 * W_GROUP:])


def _gdn(a_all, gate_all, conv_w, a_log, dt_bias, norm_w, s0, seq_len, n_seq, first_block, n_sub=1, into=None):
    small = lambda: pl.BlockSpec((1, LANES), lambda i: (0, 0))
    t = n_sub * seq_len
    n_chunks = t // CHUNK
    wide = 2 * HEAD_DIM
    scratch = [pltpu.VMEM((N_HEADS, t, HEAD_DIM), F32),
               pltpu.VMEM((N_HEADS, t, wide), F32),
               pltpu.VMEM((n_chunks, W_GROUP, CHUNK), F32),
               pltpu.VMEM((t, LANES), F32), pltpu.VMEM((t, LANES), F32), pltpu.VMEM((t, LANES), F32),
               pltpu.VMEM((n_chunks, N_GATE, CHUNK), F32), pltpu.VMEM((n_chunks, N_GATE, CHUNK), F32),
               pltpu.VMEM((n_chunks, N_GATE, wide), F32),
               pltpu.VMEM((N_GATE, t, wide), F32),
               pltpu.VMEM((N_GATE, n_chunks, 2 * CHUNK, HEAD_DIM), BF16),
               pltpu.VMEM((N_GATE, t, CHUNK), BF16),
               pltpu.VMEM((N_GATE, n_chunks, HEAD_DIM, CHUNK), BF16),
               pltpu.VMEM((n_sub * N_GATE, HEAD_DIM, wide), F32),
               pltpu.VMEM((n_sub * N_GATE, 2 * CHUNK, wide), F32),
               pltpu.VMEM((GDN_PAIRS, 2 * CHUNK, CHUNK), F32)]
    scratch += [pltpu.VMEM((GDN_CHAINS, CHUNK, CHUNK), F32)] * 3
    scratch += [pltpu.VMEM((GDN_CHAINS, CHUNK, CHUNK), BF16)] * 2
    scratch += [pltpu.VMEM((GDN_CHAINS, CHUNK, wide), BF16)]
    return _seq_call(
        functools.partial(_gdn_kernel, t=t, n_sub=n_sub), (a_all, gate_all, conv_w, a_log, dt_bias, norm_w, s0),
        in_specs=[pl.BlockSpec((t, 4 * W_GROUP), lambda i: (i + first_block, 0)),
                  pl.BlockSpec((t, LANES), lambda i: (i + first_block, 0)),
                  pl.BlockSpec((SUBLANES, 3 * W_GROUP), lambda i: (0, 0)),
                  small(), small(),
                  pl.BlockSpec((1, W_GROUP), lambda i: (0, 0)),
                  pl.BlockSpec((n_sub, N_GATE, HEAD_DIM, HEAD_DIM), lambda i: (i, 0, 0, 0))],
        out_specs=[pl.BlockSpec((t, W_GROUP), lambda i: (i + first_block, 0)),
                   pl.BlockSpec((n_sub, N_GATE, HEAD_DIM, HEAD_DIM), lambda i: (i, 0, 0, 0))],
        out_shape=[jax.ShapeDtypeStruct((N_TOK, W_GROUP), F32),
                   jax.ShapeDtypeStruct((n_seq, N_GATE, HEAD_DIM, HEAD_DIM), F32)],
        scratch_shapes=scratch,
        into=into, grid=(n_seq // n_sub,), compiler_params=_cparams("arbitrary"), name="gdn")


def _swap16(x):
    width = x.shape[-1]
    first = (_iota(x.shape, 1) // 16) % 2 == 0
    return jnp.where(first, pltpu.roll(x, width - 16, 1), pltpu.roll(x, 16, 1))


def _block_diag_heads(s0_ref, first):
    zero = jnp.zeros((HEAD_DIM, HEAD_DIM), F32)
    return jnp.concatenate(
        [jnp.concatenate([s0_ref[0, first + h] if j == h else zero for j in range(N_HEADS)], axis=1)
         for h in range(N_HEADS)], axis=0)


def _ret_kernel(r_ref, lg_ref, s0_ref, cos_ref, sin_ref, o_ref, sfin_ref, *, t, latent):
    r = r_ref[...]
    q = r[:, :W_GROUP]
    k = r[:, W_GROUP:2 * W_GROUP]
    v = r[:, 2 * W_GROUP:3 * W_GROUP]
    if latent:
        q = q * cos_ref[...] + _swap16(q) * sin_ref[...]
        k = k * cos_ref[...] + _swap16(k) * sin_ref[...]
    k = k * (HEAD_DIM ** -0.5)
    lg = -_softplus(-lg_ref[...])
    lgf, lgb = lg[0:1, :], lg[1:2, :]
    head = _iota((1, W_GROUP), 1) // HEAD_DIM
    head_sum = _head_sum_matrix()
    pos = _iota((t, 1), 0).astype(F32)
    q_b = q.astype(BF16)
    kt_b = k.T.astype(BF16)
    v_heads = [jnp.where(head == h, v, 0.0).astype(BF16) for h in range(N_HEADS)]
    if latent:
        s0f = _block_diag_heads(s0_ref, 0)
        s0b = _block_diag_heads(s0_ref, N_HEADS)
    for qt in range(t // Q_TILE):
        rows = slice(qt * Q_TILE, (qt + 1) * Q_TILE)
        diff = (_iota((Q_TILE, t), 0) + qt * Q_TILE - _iota((Q_TILE, t), 1)).astype(F32)
        both = jnp.where(diff == 0, 2.0, 1.0)
        o = jnp.zeros((Q_TILE, W_GROUP), F32)
        for h in range(N_HEADS):
            lgf_h = lgf[:, h * HEAD_DIM:h * HEAD_DIM + 1]
            lgb_h = lgb[:, h * HEAD_DIM:h * HEAD_DIM + 1]
            dmat = jnp.exp(diff * jnp.where(diff >= 0, lgf_h, -lgb_h)) * both
            s = _mm(jnp.where(head == h, q_b[rows], 0.0), kt_b) * dmat
            o = o + _mm(s.astype(BF16), v_heads[h])
        if latent:
            p = pos[rows]
            o = o + jnp.exp((p + 1.0) * lgf) * _bmm(q_b[rows], s0f) + jnp.exp((t - p) * lgb) * _bmm(q_b[rows], s0b)
        oc = o - _head_sums(o, head_sum) * (1.0 / HEAD_DIM)
        on = oc * lax.rsqrt(_head_sums(oc * oc, head_sum) * (1.0 / HEAD_DIM) + EPS)
        o_ref[rows, :] = on * _silu(r[rows, 3 * W_GROUP:])
    v_b = v.astype(BF16)
    sf = _mm_tn((k * jnp.exp((t - 1.0 - pos) * lgf)).astype(BF16), v_b)
    sb = _mm_tn((k * jnp.exp(pos * lgb)).astype(BF16), v_b)
    for h in range(N_HEADS):
        hs = slice(h * HEAD_DIM, (h + 1) * HEAD_DIM)
        sf_h, sb_h = sf[hs, hs], sb[hs, hs]
        if latent:
            sf_h = sf_h + jnp.exp(t * lgf[:, h * HEAD_DIM:h * HEAD_DIM + 1]) * s0_ref[0, h]
            sb_h = sb_h + jnp.exp(t * lgb[:, h * HEAD_DIM:h * HEAD_DIM + 1]) * s0_ref[0, N_HEADS + h]
        sfin_ref[0, h] = sf_h
        sfin_ref[0, N_HEADS + h] = sb_h


def _ret(r_all, logit, s0, cos, sin, t, n_seq, first_block, latent, into=None):
    return _seq_call(
        functools.partial(_ret_kernel, t=t, latent=latent), (r_all, logit, s0, cos, sin),
        in_specs=[pl.BlockSpec((t, 4 * W_GROUP), lambda i: (i + first_block, 0)),
                  pl.BlockSpec((2, W_GROUP), lambda i: (0, 0)),
                  pl.BlockSpec((1, N_GATE, HEAD_DIM, HEAD_DIM), lambda i: (i, 0, 0, 0)),
                  pl.BlockSpec((t, W_GROUP), lambda i: (0, 0)),
                  pl.BlockSpec((t, W_GROUP), lambda i: (0, 0))],
        out_specs=[pl.BlockSpec((t, W_GROUP), lambda i: (i + first_block, 0)),
                   pl.BlockSpec((1, N_GATE, HEAD_DIM, HEAD_DIM), lambda i: (i, 0, 0, 0))],
        out_shape=[jax.ShapeDtypeStruct((N_TOK, W_GROUP), F32),
                   jax.ShapeDtypeStruct((n_seq, N_GATE, HEAD_DIM, HEAD_DIM), F32)],
        into=into, grid=(n_seq,), compiler_params=_cparams("arbitrary"), name="ret")


def _rope_tables(t):
    pos = np.arange(t)
    row = (pos // GRID_W).astype(np.float32)
    col = (pos % GRID_W).astype(np.float32)
    nf = HEAD_DIM // 4
    inv_freq = jnp.power(ROPE_BASE, -jnp.arange(nf, dtype=F32) / nf)
    ang_r = jnp.asarray(row)[:, None] * inv_freq[None, :]
    ang_c = jnp.asarray(col)[:, None] * inv_freq[None, :]
    cos = jnp.concatenate([jnp.cos(ang_r)] * 2 + [jnp.cos(ang_c)] * 2, axis=1)
    sin = jnp.concatenate([-jnp.sin(ang_r), jnp.sin(ang_r), -jnp.sin(ang_c), jnp.sin(ang_c)], axis=1)
    return jnp.tile(cos, (1, N_HEADS)), jnp.tile(sin, (1, N_HEADS))


def _head_rms(x, w, head_sum):
    return x * lax.rsqrt(_head_sums(x * x, head_sum) * (1.0 / HEAD_DIM) + EPS) * w


def _ctx_attn_kernel(n_ref, qw_ref, kw_ref, o_ref, k_out, v_out, *, n_fill):
    n = n_ref[...]
    head_sum = _head_sum_matrix()
    head = _iota((1, W_GROUP), 1) // HEAD_DIM
    q = (_head_rms(n[:, :W_GROUP], qw_ref[...], head_sum) * (HEAD_DIM ** -0.5)).astype(BF16)
    k = _head_rms(n[:, W_GROUP:2 * W_GROUP], kw_ref[...], head_sum)
    v = n[:, 2 * W_GROUP:]
    k_t = k.T
    k_out[0, 0] = k_t
    v_out[0, 0] = v.T
    for later in range(1, 1 + n_fill):
        k_out[0, later] = jnp.zeros((W_GROUP, SEQ), F32)
        v_out[0, later] = jnp.zeros((W_GROUP, SEQ), F32)
    kt_b = k_t.astype(BF16)
    o = jnp.zeros((SEQ, W_GROUP), F32)
    for h in range(N_HEADS):
        s = _mm(jnp.where(head == h, q, 0.0), kt_b)
        p = jnp.exp(s - jnp.max(s, axis=-1, keepdims=True))
        o = o + _mm(p.astype(BF16), jnp.where(head == h, v, 0.0).astype(BF16)) / jnp.sum(p, axis=-1, keepdims=True)
    o_ref[...] = o


def _ctx_attn(n_all, qw, kw, into, layer, caches):
    slots = DEPTH if caches is None else 1
    cache_spec = pl.BlockSpec((1, slots, W_GROUP, SEQ), lambda i: (i, layer if caches is not None else 0, 0, 0))
    cache_shape = jax.ShapeDtypeStruct((BATCH, DEPTH, W_GROUP, SEQ), F32)
    return _seq_call(
        functools.partial(_ctx_attn_kernel, n_fill=slots - 1), (n_all, qw, kw),
        in_specs=[pl.BlockSpec((SEQ, 3 * W_GROUP), lambda i: (i, 0)),
                  pl.BlockSpec((1, W_GROUP), lambda i: (0, 0)),
                  pl.BlockSpec((1, W_GROUP), lambda i: (0, 0))],
        out_specs=[pl.BlockSpec((SEQ, W_GROUP), lambda i: (i, 0)), cache_spec, cache_spec],
        out_shape=[jax.ShapeDtypeStruct((N_TOK, W_GROUP), F32), cache_shape, cache_shape],
        into=into, more_into=() if caches is None else ((caches[0], 1), (caches[1], 2)),
        grid=(BATCH,), compiler_params=_cparams("arbitrary"), name="ctx_attn")


def _nat_kernel(n_ref, ck_ref, cv_ref, bias_ref, qw_ref, kw_ref, o_ref, q_s, kt_s, ckt_s):
    h = pl.program_id(1)

    @pl.when(h == 0)
    def _():
        head_sum = _head_sum_matrix()
        n = n_ref[...]
        q_s[...] = (_head_rms(n[:, :W_GROUP], qw_ref[...], head_sum) * (HEAD_DIM ** -0.5)).astype(BF16)
        kt_s[...] = _head_rms(n[:, W_GROUP:2 * W_GROUP], kw_ref[...], head_sum).T.astype(BF16)
        ckt_s[...] = ck_ref[0].T.astype(BF16)
        o_ref[...] = jnp.zeros_like(o_ref)

    mine = _iota((1, W_GROUP), 1) // HEAD_DIM == h
    v = jnp.where(mine, n_ref[:, 2 * W_GROUP:], 0.0).astype(BF16)
    cv = jnp.where(mine, cv_ref[0], 0.0).astype(BF16)
    for qt in range(DEC_SEQ // Q_TILE):
        rows = slice(qt * Q_TILE, (qt + 1) * Q_TILE)
        q = jnp.where(mine, q_s[rows, :], 0.0)
        grid_rows = range(qt * Q_TILE // GRID_W, (qt + 1) * Q_TILE // GRID_W)
        s_loc = _mm(q, kt_s[...]) + jnp.concatenate([_nat_bias_strip(bias_ref, row) for row in grid_rows], axis=0)
        s_ctx = _mm(q, ckt_s[...])
        m = jnp.maximum(jnp.max(s_loc, axis=-1, keepdims=True), jnp.max(s_ctx, axis=-1, keepdims=True))
        p_loc = jnp.exp(s_loc - m)
        p_ctx = jnp.exp(s_ctx - m)
        den = jnp.sum(p_loc, axis=-1, keepdims=True) + jnp.sum(p_ctx, axis=-1, keepdims=True)
        o_ref[rows, :] += (_mm(p_loc.astype(BF16), v) + _mm(p_ctx.astype(BF16), cv)) / den


def _nat(n_all, ck, cv, bias, qw, kw, into):
    first_block = N_CTX // DEC_SEQ
    return _seq_call(
        _nat_kernel, (n_all, ck, cv, bias, qw, kw),
        in_specs=[pl.BlockSpec((DEC_SEQ, 3 * W_GROUP), lambda b, h: (b + first_block, 0)),
                  pl.BlockSpec((1, PAST_LEN, W_GROUP), lambda b, h: (b, 0, 0)),
                  pl.BlockSpec((1, PAST_LEN, W_GROUP), lambda b, h: (b, 0, 0)),
                  pl.BlockSpec((1, 3 * N_ROW_OFF - 1, GRID_W, 2 * GRID_W), lambda b, h: (h, 0, 0, 0)),
                  pl.BlockSpec((1, W_GROUP), lambda b, h: (0, 0)),
                  pl.BlockSpec((1, W_GROUP), lambda b, h: (0, 0))],
        out_specs=pl.BlockSpec((DEC_SEQ, W_GROUP), lambda b, h: (b + first_block, 0)),
        out_shape=jax.ShapeDtypeStruct((N_TOK, W_GROUP), F32),
        scratch_shapes=[pltpu.VMEM((DEC_SEQ, W_GROUP), BF16), pltpu.VMEM((W_GROUP, DEC_SEQ), BF16),
                        pltpu.VMEM((W_GROUP, PAST_LEN), BF16)],
        into=into, grid=(DEC_BATCH, N_HEADS), compiler_params=_cparams("arbitrary", "arbitrary"), name="nat")


N_ROW_OFF = 2 * WIN_ROWS - 1
NAT_ROWS = DEC_SEQ // GRID_W
NAT_KH = min(WIN_ROWS, NAT_ROWS)
NAT_PAIR, NAT_LOW, NAT_HIGH = 0, N_ROW_OFF - 1, 2 * N_ROW_OFF - 1


def _nat_tables(rpb):
    c = np.arange(GRID_W)
    c0 = np.clip(c - WIN_COLS // 2, 0, GRID_W - WIN_COLS)
    col_in = (c[None, :] >= c0[:, None]) & (c[None, :] < c0[:, None] + WIN_COLS)
    col_idx = np.clip(c[None, :] - c[:, None], -(WIN_COLS - 1), WIN_COLS - 1) + WIN_COLS - 1
    col_hot = (col_idx[..., None] == np.arange(2 * WIN_COLS - 1)).astype(np.float32)
    tz = jnp.where(col_in, jnp.einsum('hab,qkb->haqk', rpb.astype(F32), col_hot, precision=HI), NEG_BIG)
    neg = jnp.full_like(tz, NEG_BIG)
    return jnp.concatenate([jnp.concatenate([tz[:, :-1], tz[:, 1:]], axis=-1),
                            jnp.concatenate([tz, neg], axis=-1), jnp.concatenate([neg, tz], axis=-1)], axis=1)


def _nat_bias_strip(tab_ref, row):
    first_key = min(max(row - NAT_KH // 2, 0), NAT_ROWS - NAT_KH)
    off = first_key - row + WIN_ROWS - 1
    tiles = {}
    done, key = 0, first_key
    if key % 2 == 1:
        tiles[key // 2] = tab_ref[0, NAT_HIGH + off]
        done, key = 1, key + 1
    while done + 1 < NAT_KH:
        tiles[key // 2] = tab_ref[0, NAT_PAIR + off + done]
        done, key = done + 2, key + 2
    if done < NAT_KH:
        tiles[key // 2] = tab_ref[0, NAT_LOW + off + done]
    outside = jnp.full((GRID_W, 2 * GRID_W), NEG_BIG, F32)
    return jnp.concatenate([tiles.get(i, outside) for i in range(NAT_ROWS // 2)], axis=1)


GSEL_LANE = N_EXPERTS


def _pack_router(we, be, wg, bg):
    pad = LANES - N_EXPERTS - N_GROUPS
    w = jnp.concatenate([we, wg, jnp.zeros((D_MODEL, pad), F32)], axis=1)
    b = jnp.concatenate([be, bg, jnp.zeros((pad,), F32)]).reshape(1, LANES)
    hi = w.astype(BF16)
    lo = (w - hi.astype(F32)).astype(BF16)
    return jnp.concatenate([hi, lo], axis=1), b


def _lane_min_where(mask, lane):
    return jnp.min(jnp.where(mask, lane, LANES), axis=-1, keepdims=True)


def _sconv_tile(i, s_ref, before_ref, after_ref, w_ref):
    s = s_ref[...]
    p = s[:, W_GROUP:2 * W_GROUP] * s[:, 2 * W_GROUP:]
    p_before = before_ref[SUBLANES - 1:, W_GROUP:2 * W_GROUP] * before_ref[SUBLANES - 1:, 2 * W_GROUP:]
    p_after = after_ref[:1, W_GROUP:2 * W_GROUP] * after_ref[:1, 2 * W_GROUP:]
    row = _iota((TOK_TILE, 1), 0)
    seq_len = jnp.where(i < N_CTX_TILES, SEQ, DEC_SEQ)
    pos = (i * TOK_TILE + row) & (seq_len - 1)
    prev = jnp.where(row == 0, p_before, pltpu.roll(p, 1, 0))
    nxt = jnp.where(row == TOK_TILE - 1, p_after, pltpu.roll(p, TOK_TILE - 1, 0))
    prev = jnp.where(pos == 0, 0.0, prev)
    nxt = jnp.where(pos == seq_len - 1, 0.0, nxt)
    return s[:, :W_GROUP] * (w_ref[0:1, :] * prev + w_ref[1:2, :] * p + w_ref[2:3, :] * nxt)


def _outproj_kernel(xc_ref, xl_ref, m0, m1, m2, s_ref, before_ref, after_ref, cw_ref, mod_ref, nw_ref, w_ref,
                    rw_ref, rb_ref, x_out, hf_out, route_out, w_s):
    tile_i = pl.program_id(0)

    @pl.when(tile_i == 0)
    def _():
        w_s[...] = w_ref[0].astype(BF16)

    mixed = [m0[...], m1[...], m2[...], _sconv_tile(tile_i, s_ref, before_ref, after_ref, cw_ref)]
    acc = None
    for i, m in enumerate(mixed):
        part = _mm(m.astype(BF16), w_s[i * W_GROUP:(i + 1) * W_GROUP, :])
        acc = part if acc is None else acc + part
    x = _token_tile(tile_i, xc_ref, xl_ref) + mod_ref[0, 2:3, :] * acc
    x_out[...] = x
    y = x * lax.rsqrt(jnp.mean(x * x, axis=-1, keepdims=True) + EPS) * nw_ref[...]
    hf = y * (1.0 + mod_ref[0, 4:5, :]) + mod_ref[0, 3:4, :]
    hf_hi = hf.astype(BF16)
    hf_out[...] = hf_hi

    hf_lo = (hf - hf_hi.astype(F32)).astype(BF16)
    both = _mm(hf_hi, rw_ref[...])
    logits = both[:, :LANES] + both[:, LANES:] + _mm(hf_lo, rw_ref[:, :LANES]) + rb_ref[...]
    lane = _iota(logits.shape, 1)
    is_g = (lane >= N_EXPERTS) & (lane < N_EXPERTS + N_GROUPS)
    gl = jnp.where(is_g, logits, NEG_BIG)
    ge = jnp.exp(gl - jnp.max(gl, axis=-1, keepdims=True))
    gp = jnp.where(is_g, ge / jnp.sum(ge, axis=-1, keepdims=True), -1.0)
    gw = jnp.max(gp, axis=-1, keepdims=True)
    gsel = _lane_min_where(gp == gw, lane) - N_EXPERTS
    in_grp = (lane // EXPERTS_PER_GROUP == gsel) & (lane < N_EXPERTS)
    el = jnp.where(in_grp, logits, NEG_BIG)
    ee = jnp.exp(el - jnp.max(el, axis=-1, keepdims=True))
    ep = jnp.where(in_grp, ee / jnp.sum(ee, axis=-1, keepdims=True), -1.0)
    t1 = jnp.max(ep, axis=-1, keepdims=True)
    i1 = _lane_min_where(ep == t1, lane)
    ep2 = jnp.where(lane == i1, -1.0, ep)
    t2 = jnp.max(ep2, axis=-1, keepdims=True)
    i2 = _lane_min_where(ep2 == t2, lane)
    tsum = t1 + t2
    combine = jnp.where(lane == i1, gw * (t1 / tsum), 0.0) + jnp.where(lane == i2, gw * (t2 / tsum), 0.0)
    route_out[...] = jnp.where(lane == GSEL_LANE, gsel.astype(F32), combine)


def _outproj(x_ctx, x_lat, split, mixed, a_sc, sc_w, mod, norm_w, w_out, layer, rw, rb):
    tile = lambda w: pl.BlockSpec((TOK_TILE, w), lambda i: (i, 0))
    whole = lambda a: pl.BlockSpec(a.shape, lambda i: (0,) * a.ndim)
    per_tile = TOK_TILE // SUBLANES
    halo = lambda index: pl.BlockSpec((SUBLANES, 3 * W_GROUP), lambda i: (index(i), 0))
    return pl.pallas_call(
        _outproj_kernel,
        grid=(N_TOK // TOK_TILE,),
        in_specs=_token_specs(split) + [tile(W_GROUP)] * 3
                 + [tile(3 * W_GROUP), halo(lambda i: jnp.maximum(i * per_tile - 1, 0)),
                    halo(lambda i: jnp.minimum((i + 1) * per_tile, N_TOK // SUBLANES - 1)), whole(sc_w)]
                 + [pl.BlockSpec((1, SUBLANES, D_MODEL), lambda i: (_cond_of_tile(i), 0, 0)),
                    pl.BlockSpec((1, D_MODEL), lambda i: (0, 0)),
                    pl.BlockSpec((1, D_MODEL, D_MODEL), lambda i: (layer, 0, 0), pipeline_mode=pl.Buffered(1)),
                    whole(rw), whole(rb)],
        out_specs=[tile(D_MODEL), tile(D_MODEL), tile(LANES)],
        out_shape=[jax.ShapeDtypeStruct((N_TOK, D_MODEL), F32), jax.ShapeDtypeStruct((N_TOK, D_MODEL), BF16),
                   jax.ShapeDtypeStruct((N_TOK, LANES), F32)],
        scratch_shapes=[pltpu.VMEM((D_MODEL, D_MODEL), BF16)],
        compiler_params=_cparams("arbitrary"),
        name="outproj",
    )(x_ctx, x_lat, *mixed, a_sc, a_sc, a_sc, sc_w, mod, norm_w.reshape(1, D_MODEL), w_out, rw, rb)


SEG_BLK = 16
LOCAL_ROWS = TOK_TILE + N_GROUPS * SEG_BLK
N_TOK_TILES = N_TOK // TOK_TILE
MOE_ROWS = -(-(N_TOK + N_TOK_TILES * N_GROUPS * (SEG_BLK - 1) + N_GROUPS * (MOE_TILE - 1)) // MOE_TILE) * MOE_TILE


def _moe_tables(gsel):
    groups = jnp.arange(N_GROUPS, dtype=jnp.int32)
    onehot = (gsel.reshape(N_TOK_TILES, TOK_TILE, 1) == groups).astype(jnp.int32)
    earlier = jnp.asarray(np.tril(np.ones((TOK_TILE, TOK_TILE), np.float32), -1))
    rank = jnp.einsum('ts,nsg->ntg', earlier, onehot.astype(F32)).astype(jnp.int32)
    nblk = (jnp.sum(onehot, axis=1) + SEG_BLK - 1) // SEG_BLK
    loc_blk = jnp.cumsum(nblk, axis=1) - nblk
    blocks_per_tile = MOE_TILE // SEG_BLK
    grp_tiles = (jnp.sum(nblk, axis=0) + blocks_per_tile - 1) // blocks_per_tile
    grp_tile_start = jnp.cumsum(grp_tiles) - grp_tiles
    dst_blk = grp_tile_start[None, :] * blocks_per_tile + jnp.cumsum(nblk, axis=0) - nblk
    local_pos = jnp.sum(onehot * (loc_blk[:, None, :] * SEG_BLK + rank), axis=2)
    tile_idx = jnp.arange(MOE_ROWS // MOE_TILE, dtype=jnp.int32)
    tile_group = jnp.clip(jnp.sum(tile_idx[:, None] >= grp_tile_start[None, :], axis=1) - 1, 0, N_GROUPS - 1)
    in_group = tile_idx - grp_tile_start[tile_group]
    tile_rows = jnp.clip(jnp.sum(nblk, axis=0)[tile_group] * SEG_BLK - in_group * MOE_TILE, 0, MOE_TILE)
    flat = lambda a: a.reshape(-1).astype(jnp.int32)
    return local_pos.astype(jnp.int32), flat(nblk), flat(loc_blk), flat(dst_blk), flat(tile_group), flat(tile_rows)


def _segment_copies(t, nblk, loc_blk, dst_blk, make_copies, action):
    for g in range(N_GROUPS):
        k = t * N_GROUPS + g

        @pl.loop(0, nblk[k])
        def _(b):
            local = pl.multiple_of((loc_blk[k] + b) * SEG_BLK, SEG_BLK)
            sorted_row = pl.multiple_of((dst_blk[k] + b) * SEG_BLK, SEG_BLK)
            for cp in make_copies(local, sorted_row):
                action(cp)


def _dispatch_kernel(nblk, loc_blk, dst_blk, hf_ref, rt_ref, lp_ref, xs_in, rs_in, xs_hbm, rs_hbm, xbuf, rbuf, sem):
    t = pl.program_id(0)
    slot = t % 2
    onehot = _iota((LOCAL_ROWS, TOK_TILE), 0) == lp_ref[0]
    xbuf[slot] = _mm(onehot.astype(BF16), hf_ref[...]).astype(BF16)
    route = rt_ref[...]
    route_hi = route.astype(BF16)
    route_lo = (route - route_hi.astype(F32)).astype(BF16)
    rbuf[slot] = _mm(onehot.astype(BF16), route_hi) + _mm(onehot.astype(BF16), route_lo)

    def copies_of(s):
        def copies(local, sorted_row):
            return (pltpu.make_async_copy(xbuf.at[s, pl.ds(local, SEG_BLK)], xs_hbm.at[pl.ds(sorted_row, SEG_BLK)], sem.at[s]),
                    pltpu.make_async_copy(rbuf.at[s, pl.ds(local, SEG_BLK)], rs_hbm.at[pl.ds(sorted_row, SEG_BLK)], sem.at[s]))
        return copies

    @pl.when(t > 0)
    def _():
        _segment_copies(t - 1, nblk, loc_blk, dst_blk, copies_of(1 - slot), lambda cp: cp.wait())

    _segment_copies(t, nblk, loc_blk, dst_blk, copies_of(slot), lambda cp: cp.start())

    @pl.when(t == N_TOK_TILES - 1)
    def _():
        _segment_copies(t, nblk, loc_blk, dst_blk, copies_of(slot), lambda cp: cp.wait())


def _dispatch(hf, route, local_pos, nblk, loc_blk, dst_blk):
    grid_spec = pltpu.PrefetchScalarGridSpec(
        num_scalar_prefetch=3,
        grid=(N_TOK_TILES,),
        in_specs=[pl.BlockSpec((TOK_TILE, D_MODEL), lambda t, *_: (t, 0)),
                  pl.BlockSpec((TOK_TILE, LANES), lambda t, *_: (t, 0)),
                  pl.BlockSpec((1, 1, TOK_TILE), lambda t, *_: (t, 0, 0)),
                  pl.BlockSpec(memory_space=pl.ANY), pl.BlockSpec(memory_space=pl.ANY)],
        out_specs=[pl.BlockSpec(memory_space=pl.ANY), pl.BlockSpec(memory_space=pl.ANY)],
        scratch_shapes=[pltpu.VMEM((2, LOCAL_ROWS, D_MODEL), BF16), pltpu.VMEM((2, LOCAL_ROWS, LANES), F32),
                        pltpu.SemaphoreType.DMA((2,))],
    )
    return pl.pallas_call(
        _dispatch_kernel,
        grid_spec=grid_spec,
        out_shape=[jax.ShapeDtypeStruct((MOE_ROWS, D_MODEL), BF16), jax.ShapeDtypeStruct((MOE_ROWS, LANES), F32)],
        input_output_aliases={6: 0, 7: 1},
        compiler_params=_cparams("arbitrary"),
        name="dispatch",
    )(nblk, loc_blk, dst_blk, hf, route, local_pos.reshape(N_TOK_TILES, 1, TOK_TILE),
      jnp.zeros((MOE_ROWS, D_MODEL), BF16), jnp.zeros((MOE_ROWS, LANES), F32))


def _moe_kernel(tile_group, tile_rows, x_ref, r_ref, wg_hbm, wu_hbm, wd_hbm, y_ref,
                wg_b, wu_b, wd_b, stage_g, stage_u, stage_d, sem, *, layer):
    i = pl.program_id(0)
    g = tile_group[i]
    group_row = layer * N_GROUPS + g
    new_group = (i == 0) | (g != tile_group[jnp.maximum(i - 1, 0)])
    valid = tile_rows[i] > 0
    half = tile_rows[i] <= MOE_TILE // 2

    def weight_copies(e):
        slot = e % 2
        return (pltpu.make_async_copy(wg_hbm.at[group_row, e], stage_g.at[slot], sem.at[slot]),
                pltpu.make_async_copy(wu_hbm.at[group_row, e], stage_u.at[slot], sem.at[slot]),
                pltpu.make_async_copy(wd_hbm.at[group_row, e], stage_d.at[slot], sem.at[slot]))

    def run(load_weights, rows):
        x = x_ref[:rows, :]
        route = r_ref[:rows, :]
        lane = _iota(route.shape, 1)
        acc = jnp.zeros((rows, D_MODEL), F32)
        if load_weights:
            for e in range(2):
                for cp in weight_copies(e):
                    cp.start()
        for e in range(EXPERTS_PER_GROUP):
            if load_weights:
                for cp in weight_copies(e):
                    cp.wait()
                wg_b[e] = stage_g[e % 2].astype(BF16)
                wu_b[e] = stage_u[e % 2].astype(BF16)
                wd_b[e] = stage_d[e % 2].astype(BF16)
                if e + 2 < EXPERTS_PER_GROUP:
                    for cp in weight_copies(e + 2):
                        cp.start()
            cw = jnp.sum(jnp.where(lane == g * EXPERTS_PER_GROUP + e, route, 0.0), axis=-1, keepdims=True)
            act = _silu(_mm(x, wg_b[e])) * _mm(x, wu_b[e]) * cw
            acc = acc + _mm(act.astype(BF16), wd_b[e])
        y_ref[:rows, :] = acc
        if rows < MOE_TILE:
            y_ref[rows:, :] = jnp.zeros((MOE_TILE - rows, D_MODEL), F32)

    for load_weights in (True, False):
        for rows in (MOE_TILE, MOE_TILE // 2):
            first = new_group if load_weights else jnp.logical_not(new_group)
            fits = half if rows < MOE_TILE else jnp.logical_not(half)

            @pl.when(valid & first & fits)
            def _():
                run(load_weights, rows)

    @pl.when(jnp.logical_not(valid))
    def _():
        y_ref[...] = jnp.zeros_like(y_ref)


def _moe(xs, rs, tile_group, tile_rows, wg, wu, wd, layer):
    any_spec = pl.BlockSpec(memory_space=pl.ANY)
    grid_spec = pltpu.PrefetchScalarGridSpec(
        num_scalar_prefetch=2,
        grid=(MOE_ROWS // MOE_TILE,),
        in_specs=[pl.BlockSpec((MOE_TILE, D_MODEL), lambda i, tg, tv: (i, 0)),
                  pl.BlockSpec((MOE_TILE, LANES), lambda i, tg, tv: (i, 0)),
                  any_spec, any_spec, any_spec],
        out_specs=pl.BlockSpec((MOE_TILE, D_MODEL), lambda i, tg, tv: (i, 0)),
        scratch_shapes=[pltpu.VMEM((EXPERTS_PER_GROUP, D_MODEL, EXPERT_FF), BF16),
                        pltpu.VMEM((EXPERTS_PER_GROUP, D_MODEL, EXPERT_FF), BF16),
                        pltpu.VMEM((EXPERTS_PER_GROUP, EXPERT_FF, D_MODEL), BF16),
                        pltpu.VMEM((2, D_MODEL, EXPERT_FF), F32), pltpu.VMEM((2, D_MODEL, EXPERT_FF), F32),
                        pltpu.VMEM((2, EXPERT_FF, D_MODEL), F32), pltpu.SemaphoreType.DMA((2,))],
    )
    return pl.pallas_call(
        functools.partial(_moe_kernel, layer=layer),
        grid_spec=grid_spec,
        out_shape=jax.ShapeDtypeStruct((MOE_ROWS, D_MODEL), F32),
        compiler_params=_cparams("arbitrary"),
        name="moe",
    )(tile_group, tile_rows, xs, rs, wg, wu, wd)


def _combine_kernel(nblk, loc_blk, dst_blk, x_ref, lp_ref, mod_ref, ys_hbm, *refs, split):
    o_refs, ybuf, sem = refs[:-2], refs[-2], refs[-1]
    t = pl.program_id(0)
    slot = t % 2

    def fetch(tile, s):
        def copies(local, sorted_row):
            return (pltpu.make_async_copy(ys_hbm.at[pl.ds(sorted_row, SEG_BLK)], ybuf.at[s, pl.ds(local, SEG_BLK)], sem.at[s]),)
        ybuf[s] = jnp.zeros((LOCAL_ROWS, D_MODEL), F32)
        _segment_copies(tile, nblk, loc_blk, dst_blk, copies, lambda cp: cp.start())

    @pl.when(t == 0)
    def _():
        fetch(0, 0)

    @pl.when(t + 1 < N_TOK_TILES)
    def _():
        fetch(t + 1, 1 - slot)

    def copies_now(local, sorted_row):
        return (pltpu.make_async_copy(ys_hbm.at[pl.ds(sorted_row, SEG_BLK)], ybuf.at[slot, pl.ds(local, SEG_BLK)], sem.at[slot]),)

    _segment_copies(t, nblk, loc_blk, dst_blk, copies_now, lambda cp: cp.wait())

    onehot = (_iota((TOK_TILE, LOCAL_ROWS), 1) == lp_ref[...]).astype(BF16)
    ys = ybuf[slot]
    hi = ys.astype(BF16)
    lo = (ys - hi.astype(F32)).astype(BF16)
    y = _mm(onehot, hi) + _mm(onehot, lo)
    out = x_ref[...] + mod_ref[0, 5:6, :] * y
    if split:
        @pl.when(t < N_CTX_TILES)
        def _():
            o_refs[0][...] = out

        @pl.when(t >= N_CTX_TILES)
        def _():
            o_refs[1][...] = out
    else:
        o_refs[0][...] = out


def _combine(x, ys, mod, local_pos, nblk, loc_blk, dst_blk, split):
    tile = pl.BlockSpec((TOK_TILE, D_MODEL), lambda t, *_: (t, 0))
    if split:
        out_specs = _token_specs(True)
        out_shape = [jax.ShapeDtypeStruct((N_CTX, D_MODEL), F32), jax.ShapeDtypeStruct((N_LAT, D_MODEL), F32)]
    else:
        out_specs, out_shape = [tile], [jax.ShapeDtypeStruct((N_TOK, D_MODEL), F32)]
    grid_spec = pltpu.PrefetchScalarGridSpec(
        num_scalar_prefetch=3,
        grid=(N_TOK_TILES,),
        in_specs=[tile, pl.BlockSpec((TOK_TILE, 1), lambda t, *_: (t, 0)),
                  pl.BlockSpec((1, SUBLANES, D_MODEL), lambda t, *_: (_cond_of_tile(t), 0, 0)),
                  pl.BlockSpec(memory_space=pl.ANY)],
        out_specs=out_specs,
        scratch_shapes=[pltpu.VMEM((2, LOCAL_ROWS, D_MODEL), F32), pltpu.SemaphoreType.DMA((2,))],
    )
    return pl.pallas_call(
        functools.partial(_combine_kernel, split=split),
        grid_spec=grid_spec,
        out_shape=out_shape,
        compiler_params=_cparams("arbitrary"),
        name="combine",
    )(nblk, loc_blk, dst_blk, x, local_pos.reshape(N_TOK, 1), mod, ys)


def _lane_row(v):
    v = v.reshape(-1).astype(F32)
    return jnp.concatenate([v, jnp.zeros((LANES - v.shape[0],), F32)]).reshape(1, LANES)


def _pad_rows(w):
    return jnp.concatenate([w, jnp.zeros((SUBLANES - w.shape[0], w.shape[1]), w.dtype)], axis=0)


def kernel(x_prompt, x_sample, state_gdn, state_ret, cache_nat_k, cache_nat_v, c, c_ctx, ada_w, ada_b, norm_mix_w, norm_ffn_w, w_in, gdn_conv_w, gdn_a_log, gdn_dt_bias, gdn_norm_w, ret_gamma_logit, nat_q_norm_w, nat_k_norm_w, nat_rpb, sc_conv_w, w_out, router_group_w, router_group_b, router_expert_w, router_expert_b, moe_w_gate, moe_w_up, moe_w_down):
    x_ctx, x_lat, split = x_prompt.reshape(N_CTX, D_MODEL), x_sample.reshape(N_LAT, D_MODEL), True
    cond = jnp.concatenate([c_ctx[None, :], c], axis=0)
    ada = _ada(cond, ada_w, ada_b).reshape(DEPTH, SUBLANES, 6, D_MODEL)
    cos, sin = _rope_tables(DEC_SEQ)
    zero_state = jnp.zeros((BATCH, N_GATE, HEAD_DIM, HEAD_DIM), F32)
    lat_block = N_CTX // DEC_SEQ
    gdn_list, ret_list, caches = [], [], None
    mixed = [jnp.zeros((N_TOK, W_GROUP), F32) for _ in range(3)]
    for l in range(DEPTH):
        mod = jnp.concatenate([ada[l, :1 + DEC_BATCH], jnp.zeros((1 + DEC_BATCH, SUBLANES - 6, D_MODEL), F32)], axis=1)
        a_gdn, a_ret, a_nat, a_sc, a_gate = _inproj(x_ctx, x_lat, split, mod, norm_mix_w[l], w_in, l)

        conv_w = _pad_rows(gdn_conv_w[l])
        a_log, dt_b = _lane_row(gdn_a_log[l]), _lane_row(gdn_dt_bias[l])
        gnw = jnp.tile(gdn_norm_w[l], N_HEADS).reshape(1, W_GROUP)
        o_gdn, s_gdn = _gdn(a_gdn, a_gate, conv_w, a_log, dt_b, gnw, zero_state, SEQ, BATCH, 0, n_sub=2,
                            into=mixed[0])
        s0 = state_gdn[:, l].reshape(DEC_BATCH, N_GATE, HEAD_DIM, HEAD_DIM)
        o_gdn, _ = _gdn(a_gdn, a_gate, conv_w, a_log, dt_b, gnw, s0, DEC_SEQ, DEC_BATCH, lat_block, into=o_gdn)

        logit = jnp.repeat(ret_gamma_logit[l].astype(F32), HEAD_DIM, axis=1)
        o_ret, s_ret = _ret(a_ret, logit, zero_state, cos[:SEQ], sin[:SEQ], SEQ, BATCH, 0, False, into=mixed[1])
        s0 = state_ret[:, l].reshape(DEC_BATCH, N_GATE, HEAD_DIM, HEAD_DIM)
        o_ret, _ = _ret(a_ret, logit, s0, cos, sin, DEC_SEQ, DEC_BATCH, lat_block, True, into=o_ret)

        qw = jnp.tile(nat_q_norm_w[l], N_HEADS).reshape(1, W_GROUP)
        kw = jnp.tile(nat_k_norm_w[l], N_HEADS).reshape(1, W_GROUP)
        o_nat, *caches = _ctx_attn(a_nat, qw, kw, mixed[2], l, caches)
        o_nat = _nat(a_nat, cache_nat_k[:, l].reshape(DEC_BATCH, PAST_LEN, W_GROUP),
                     cache_nat_v[:, l].reshape(DEC_BATCH, PAST_LEN, W_GROUP), _nat_tables(nat_rpb[l]), qw, kw, o_nat)

        mixed = [o_gdn, o_ret, o_nat]
        rw, rb = _pack_router(router_expert_w[l], router_expert_b[l], router_group_w[l], router_group_b[l])
        x_mid, hf, route = _outproj(x_ctx, x_lat, split, mixed, a_sc, _pad_rows(sc_conv_w[l]), mod, norm_ffn_w[l],
                                    w_out, l, rw, rb)

        local_pos, nblk, loc_blk, dst_blk, tile_group, tile_rows = _moe_tables(route[:, GSEL_LANE].astype(jnp.int32))
        xs, rs = _dispatch(hf, route, local_pos, nblk, loc_blk, dst_blk)
        to_group = lambda w: w.reshape((DEPTH * N_GROUPS, EXPERTS_PER_GROUP) + w.shape[2:])
        ys = _moe(xs, rs, tile_group, tile_rows, to_group(moe_w_gate), to_group(moe_w_up), to_group(moe_w_down), l)
        last = l == DEPTH - 1
        out = _combine(x_mid, ys, mod, local_pos, nblk, loc_blk, dst_blk, split=last)
        x_ctx, x_lat, split = (out[0], out[1], True) if last else (out[0], out[0], False)

        gdn_list.append(s_gdn.reshape(BATCH, 2, N_HEADS, HEAD_DIM, HEAD_DIM))
        ret_list.append(s_ret.reshape(BATCH, 2, N_HEADS, HEAD_DIM, HEAD_DIM))
    new_k, new_v = [a.reshape(BATCH, DEPTH, N_HEADS, HEAD_DIM, SEQ).transpose(0, 1, 4, 2, 3) for a in caches]
    return (x_ctx.reshape(BATCH, SEQ, D_MODEL), x_lat.reshape(DEC_BATCH, DEC_SEQ, D_MODEL),
            jnp.stack(gdn_list, axis=1), jnp.stack(ret_list, axis=1), new_k, new_v)
```

```python
import functools

import numpy as np
import jax
import jax.numpy as jnp
from jax import lax
from jax.experimental import pallas as pl
from jax.experimental.pallas import tpu as pltpu

D_MODEL = 1024
BATCH = 16
SEQ = 256
DEPTH = 2
DEC_BATCH = 2
DEC_SEQ = 1024
PAST_LEN = 256
GRID_W = 64
HEAD_DIM = 64
W_GROUP = D_MODEL // 4
N_HEADS = W_GROUP // HEAD_DIM
CHUNK = 64
WIN_ROWS = 8
WIN_COLS = 16
ROPE_BASE = 10000.0
N_GROUPS = 4
EXPERTS_PER_GROUP = 8
N_EXPERTS = N_GROUPS * EXPERTS_PER_GROUP
EXPERT_FF = 256
EPS = 1e-6

N_CTX = BATCH * SEQ
N_LAT = DEC_BATCH * DEC_SEQ
N_TOK = N_CTX + N_LAT
LANES = 128
SUBLANES = 8
TOK_TILE = 512
MOE_TILE = 512
Q_TILE = 256
VMEM_LIMIT = 48 * 1024 * 1024
NEG_BIG = -1e30
N_GATE = 2 * N_HEADS

F32 = jnp.float32
BF16 = jnp.bfloat16
HI = lax.Precision.HIGHEST


def _mm(a, b, prec=None):
    return lax.dot_general(a, b, (((1,), (0,)), ((), ())), precision=prec, preferred_element_type=F32)


def _mm_tn(a, b, prec=None):
    return lax.dot_general(a, b, (((0,), (0,)), ((), ())), precision=prec, preferred_element_type=F32)


def _bmm(a, b):
    return _mm(a.astype(BF16), b.astype(BF16))


def _sigmoid(x):
    return 1.0 / (1.0 + jnp.exp(-x))


def _silu(x):
    return x * _sigmoid(x)


def _softplus(x):
    return jnp.maximum(x, 0.0) + jnp.log(1.0 + jnp.exp(-jnp.abs(x)))


def _iota(shape, dim):
    return lax.broadcasted_iota(jnp.int32, shape, dim)


def _cparams(*sem):
    return pltpu.CompilerParams(dimension_semantics=sem, vmem_limit_bytes=VMEM_LIMIT)


def _cond_of_tile(i):
    n_ctx_tiles = N_CTX // TOK_TILE
    return jnp.where(i < n_ctx_tiles, 0, 1 + (i - n_ctx_tiles) // (DEC_SEQ // TOK_TILE))


def _head_sum_matrix():
    return (_iota((W_GROUP, W_GROUP), 0) // HEAD_DIM == _iota((W_GROUP, W_GROUP), 1) // HEAD_DIM).astype(BF16)


def _head_sums(x, head_sum):
    hi = x.astype(BF16)
    lo = (x - hi.astype(F32)).astype(BF16)
    return _mm(hi, head_sum) + _mm(lo, head_sum)


N_COND = 1 + DEC_BATCH
ADA_TN = 1536


ADA_BUFS = 3
ADA_CHUNKS = 6 * D_MODEL // ADA_TN


def _ada_kernel(c_ref, w_hbm, b_ref, o_ref, wbuf, sem):
    steps = [(l, j) for l in range(DEPTH) for j in range(ADA_CHUNKS)]

    def chunk_copy(step):
        l, j = steps[step]
        slot = step % ADA_BUFS
        return pltpu.make_async_copy(w_hbm.at[l, :, pl.ds(j * ADA_TN, ADA_TN)], wbuf.at[slot], sem.at[slot])

    for step in range(min(ADA_BUFS, len(steps))):
        chunk_copy(step).start()
    for step, (l, j) in enumerate(steps):
        slot = step % ADA_BUFS
        chunk_copy(step).wait()

        def slab(s, accs):
            rows = pl.ds(pl.multiple_of(s * SUBLANES, SUBLANES), SUBLANES)
            w = wbuf[slot, rows, :]
            return tuple(acc + w * jnp.tile(_silu(c_ref[r, rows, :]), (1, ADA_TN // LANES))
                         for r, acc in enumerate(accs))

        zero = jnp.zeros((SUBLANES, ADA_TN), F32)
        accs = lax.fori_loop(0, D_MODEL // SUBLANES, slab, (zero,) * N_COND, unroll=4)
        out = jnp.concatenate([jnp.sum(acc, axis=0, keepdims=True) for acc in accs]
                              + [jnp.zeros((SUBLANES - N_COND, ADA_TN), F32)], axis=0)
        cols = slice(j * ADA_TN, (j + 1) * ADA_TN)
        o_ref[l, :, cols] = out + b_ref[l, :, cols]
        if step + ADA_BUFS < len(steps):
            chunk_copy(step + ADA_BUFS).start()


def _ada(cond, ada_w, ada_b):
    n_out = 6 * D_MODEL
    cond_lanes = jnp.broadcast_to(cond[:, :, None], (N_COND, D_MODEL, LANES))
    whole = lambda shape: pl.BlockSpec(shape, lambda: (0,) * len(shape))
    return pl.pallas_call(
        _ada_kernel,
        in_specs=[whole((N_COND, D_MODEL, LANES)), pl.BlockSpec(memory_space=pl.ANY), whole((DEPTH, 1, n_out))],
        out_specs=whole((DEPTH, SUBLANES, n_out)),
        out_shape=jax.ShapeDtypeStruct((DEPTH, SUBLANES, n_out), F32),
        scratch_shapes=[pltpu.VMEM((ADA_BUFS, D_MODEL, ADA_TN), F32), pltpu.SemaphoreType.DMA((ADA_BUFS,))],
        compiler_params=pltpu.CompilerParams(vmem_limit_bytes=VMEM_LIMIT),
        name="ada",
    )(cond_lanes, ada_w, ada_b.reshape(DEPTH, 1, n_out))


IN_WIDTHS = (4 * W_GROUP, 4 * W_GROUP, 3 * W_GROUP, 3 * W_GROUP, LANES)
IN_PACKED = sum(IN_WIDTHS)


IN_TOTAL = 3 * W_GROUP + W_GROUP + 2 * N_GATE + 4 * W_GROUP + 3 * W_GROUP + 3 * W_GROUP
IN_GATE_SRC = 4 * W_GROUP
IN_SRC = (0, IN_GATE_SRC + 2 * N_GATE, IN_GATE_SRC + 2 * N_GATE + 4 * W_GROUP,
          IN_GATE_SRC + 2 * N_GATE + 7 * W_GROUP)
N_CTX_TILES = N_CTX // TOK_TILE


def _token_specs(split):
    lat0 = 0 if split else N_CTX_TILES
    return [pl.BlockSpec((TOK_TILE, D_MODEL), lambda i, *_: (jnp.minimum(i, N_CTX_TILES - 1), 0)),
            pl.BlockSpec((TOK_TILE, D_MODEL), lambda i, *_: (jnp.maximum(i, N_CTX_TILES) - N_CTX_TILES + lat0, 0))]


def _token_tile(i, ctx_ref, lat_ref):
    return jnp.where(i < N_CTX_TILES, ctx_ref[...], lat_ref[...])


def _inproj_kernel(xc_ref, xl_ref, mod_ref, nw_ref, w_ref, *refs):
    o_refs, w_s = refs[:-1], refs[-1]
    i = pl.program_id(0)

    @pl.when(i == 0)
    def _():
        piece = 256
        off = 0
        for src, width in zip(IN_SRC, IN_WIDTHS[:-1]):
            for c in range(0, width, piece):
                w_s[:, off + c:off + c + piece] = w_ref[0, src + c:src + c + piece, :].T.astype(BF16)
            off += width
        gate = w_ref[0, IN_GATE_SRC:IN_GATE_SRC + LANES, :].T
        w_s[:, off:] = jnp.where(_iota((D_MODEL, LANES), 1) < 2 * N_GATE, gate, 0.0).astype(BF16)

    x = _token_tile(i, xc_ref, xl_ref)
    y = x * lax.rsqrt(jnp.mean(x * x, axis=-1, keepdims=True) + EPS) * nw_ref[...]
    h = (y * (1.0 + mod_ref[0, 1:2, :]) + mod_ref[0, 0:1, :]).astype(BF16)
    off = 0
    for o_ref, width in zip(o_refs, IN_WIDTHS):
        o_ref[...] = _mm(h, w_s[:, off:off + width])
        off += width


def _inproj(x_ctx, x_lat, split, mod, norm_w, w_in, layer):
    return pl.pallas_call(
        _inproj_kernel,
        grid=(N_TOK // TOK_TILE,),
        in_specs=_token_specs(split)
                 + [pl.BlockSpec((1, SUBLANES, D_MODEL), lambda i: (_cond_of_tile(i), 0, 0)),
                    pl.BlockSpec((1, D_MODEL), lambda i: (0, 0)),
                    pl.BlockSpec((1, IN_TOTAL, D_MODEL), lambda i: (layer, 0, 0), pipeline_mode=pl.Buffered(1))],
        out_specs=[pl.BlockSpec((TOK_TILE, w), lambda i: (i, 0)) for w in IN_WIDTHS],
        out_shape=[jax.ShapeDtypeStruct((N_TOK, w), F32) for w in IN_WIDTHS],
        scratch_shapes=[pltpu.VMEM((D_MODEL, IN_PACKED), BF16)],
        compiler_params=_cparams("arbitrary"),
        name="inproj",
    )(x_ctx, x_lat, mod, norm_w.reshape(1, D_MODEL), jnp.swapaxes(w_in, 1, 2))


def _seq_call(kernel_fn, args, in_specs, out_specs, out_shape, into, more_into=(), **kwargs):
    donors = ([] if into is None else [(into, 0)]) + list(more_into)
    n_in = len(args)
    inner = kernel_fn
    kernel_fn = lambda *refs: inner(*refs[:n_in], *refs[n_in + len(donors):])
    aliases = {n_in + j: out_idx for j, (_, out_idx) in enumerate(donors)}
    args = list(args) + [a for a, _ in donors]
    in_specs = list(in_specs) + [pl.BlockSpec(memory_space=pl.ANY)] * len(donors)
    return pl.pallas_call(kernel_fn, in_specs=in_specs, out_specs=out_specs, out_shape=out_shape,
                          input_output_aliases=aliases, **kwargs)(*args)


def _shift_rows(p, seq_len):
    rows = p.shape[0]
    pos = _iota(p.shape, 0) & (seq_len - 1)
    prev = jnp.where(pos == 0, 0.0, pltpu.roll(p, 1, 0))
    nxt = jnp.where(pos == seq_len - 1, 0.0, pltpu.roll(p, rows - 1, 0))
    return prev, nxt


def _conv3(x, w_ref, seq_len):
    prev, nxt = _shift_rows(x, seq_len)
    return w_ref[0:1, :] * prev + w_ref[1:2, :] * x + w_ref[2:3, :] * nxt


def _chunk_scan(x, reverse):
    t = x.shape[0]
    pos = _iota(x.shape, 0) % CHUNK
    step = 1
    while step < CHUNK:
        if reverse:
            x = x + jnp.where(pos < CHUNK - step, pltpu.roll(x, t - step, 0), 0.0)
        else:
            x = x + jnp.where(pos >= step, pltpu.roll(x, step, 0), 0.0)
        step *= 2
    return x


GDN_GROUP_CHUNKS = 4
GDN_BASE_BLOCK = 4
GDN_CHAINS = GDN_GROUP_CHUNKS * N_GATE
GDN_PAIRS = GDN_GROUP_CHUNKS * N_HEADS


def _gdn_kernel(a_ref, gate_ref, convw_ref, alog_ref, dtb_ref, nw_ref, s0_ref, o_ref, sfin_ref,
                q_s, kv_s, kt_s, gc_s, eg_s, beta_s, gcrow_s, ekdrow_s, cdec_s, uo_s, wq_s, attn_s, kdt_s,
                st_s, wsqs_s, kk_s, d_s, m1_s, p_s, low_s, pb_s, rhs_s, *, t, n_sub):
    n_chunks = t // CHUNK
    a = a_ref[...]
    qkv = _silu(_conv3(a[:, :3 * W_GROUP], convw_ref, t // n_sub))
    q = qkv[:, :W_GROUP]
    k = qkv[:, W_GROUP:2 * W_GROUP]
    v = qkv[:, 2 * W_GROUP:]
    head_sum = _head_sum_matrix()
    q = q * lax.rsqrt(_head_sums(q * q, head_sum) + EPS) * (HEAD_DIM ** -0.5)
    k = k * lax.rsqrt(_head_sums(k * k, head_sum) + EPS)
    for h in range(N_HEADS):
        hs = slice(h * HEAD_DIM, (h + 1) * HEAD_DIM)
        q_s[h] = q[:, hs]
        kv_s[h] = jnp.concatenate([k[:, hs], v[:, hs]], axis=1)
    k_t = k.T
    for c in range(n_chunks):
        kt_s[c] = k_t[:, c * CHUNK:(c + 1) * CHUNK]

    gates = gate_ref[...]
    log_a = -jnp.exp(alog_ref[...]) * _softplus(gates + dtb_ref[...])
    beta_s[...] = _sigmoid(gates)

    ci = _iota((CHUNK, CHUNK), 0)
    cj = _iota((CHUNK, CHUNK), 1)
    eye = (ci == cj).astype(F32)
    blk_mask = (ci // GDN_BASE_BLOCK) == (cj // GDN_BASE_BLOCK)
    low_half = _iota((CHUNK, 2 * HEAD_DIM), 1) < HEAD_DIM

    prefix = _chunk_scan(log_a, reverse=False)
    suffix = _chunk_scan(log_a, reverse=True)
    gc = jnp.where(_iota((t, LANES), 1) < N_HEADS, prefix, suffix)
    gt = prefix + suffix - log_a
    gc_s[...] = gc
    eg_s[...] = jnp.exp(gc)
    gc_t = gc.T
    ekd_t = jnp.exp(gt - gc).T
    cdec_t = jnp.exp(gt).T
    for c in range(n_chunks):
        lanes = slice(c * CHUNK, (c + 1) * CHUNK)
        gcrow_s[c] = gc_t[:N_GATE, lanes]
        ekdrow_s[c] = ekd_t[:N_GATE, lanes]
        cdec_s[c] = jnp.concatenate([cdec_t[:N_GATE, lanes]] * 2, axis=1)

    def solve_group(grp, carry):
        row0 = grp * (GDN_GROUP_CHUNKS * CHUNK)
        chains = [(cl, a_idx) for cl in range(GDN_GROUP_CHUNKS) for a_idx in range(N_GATE)]

        def rows_of(cl):
            return pl.ds(pl.multiple_of(row0 + cl * CHUNK, CHUNK), CHUNK)

        for cl in range(GDN_GROUP_CHUNKS):
            for h in range(N_HEADS):
                rows = rows_of(cl)
                kq = jnp.concatenate([kv_s[h, rows, :HEAD_DIM], q_s[h, rows, :]], axis=0)
                k_t_h = kt_s[grp * GDN_GROUP_CHUNKS + cl, h * HEAD_DIM:(h + 1) * HEAD_DIM, :]
                kk_s[cl * N_HEADS + h] = _bmm(kq, k_t_h)
        for b, (cl, a_idx) in enumerate(chains):
            backward = a_idx >= N_HEADS
            h = a_idx % N_HEADS
            rows, c = rows_of(cl), grp * GDN_GROUP_CHUNKS + cl
            incl = (cj >= ci) if backward else (cj <= ci)
            strict = (cj > ci) if backward else (cj < ci)
            bt = beta_s[rows, N_GATE + a_idx:N_GATE + a_idx + 1]
            decay = jnp.exp(jnp.where(incl, gc_s[rows, a_idx:a_idx + 1] - gcrow_s[c, a_idx:a_idx + 1, :], NEG_BIG))
            low = jnp.where(strict, kk_s[cl * N_HEADS + h, :CHUNK, :] * bt * decay, 0.0)
            attn_s[a_idx, rows, :] = (kk_s[cl * N_HEADS + h, CHUNK:, :] * decay).astype(BF16)
            d_s[b] = jnp.where(blk_mask, low, 0.0)
            low_s[b] = low.astype(BF16)
            rhs_s[b] = (kv_s[h, rows, :] * bt
                        * jnp.where(low_half, eg_s[rows, a_idx:a_idx + 1], 1.0)).astype(BF16)
        for b in range(GDN_CHAINS):
            m1_s[b] = _bmm(d_s[b], d_s[b])
        for b in range(GDN_CHAINS):
            d, d2 = d_s[b], m1_s[b]
            p_s[b] = eye - d + d2 - _bmm(d, d2)
        size = GDN_BASE_BLOCK
        while size < CHUNK:
            pair = ((ci // size) != (cj // size)) & ((ci // (2 * size)) == (cj // (2 * size)))
            for b in range(GDN_CHAINS):
                coupling = jnp.where(pair, low_s[b], jnp.zeros((), BF16))
                pb_s[b] = _mm(p_s[b].astype(BF16), coupling).astype(BF16)
            for b in range(GDN_CHAINS):
                p = p_s[b]
                p_s[b] = p - _mm(pb_s[b], p.astype(BF16))
            size *= 2
        for b, (cl, a_idx) in enumerate(chains):
            h = a_idx % N_HEADS
            rows, c = rows_of(cl), grp * GDN_GROUP_CHUNKS + cl
            wu = _mm(p_s[b].astype(BF16), rhs_s[b])
            uo_s[a_idx, rows, :] = wu
            wq_s[a_idx, c, :CHUNK, :] = wu[:, :HEAD_DIM].astype(BF16)
            wq_s[a_idx, c, CHUNK:, :] = (q_s[h, rows, :] * eg_s[rows, a_idx:a_idx + 1]).astype(BF16)
            k_t_h = kt_s[c, h * HEAD_DIM:(h + 1) * HEAD_DIM, :]
            kdt_s[a_idx, c] = (k_t_h * ekdrow_s[c, a_idx:a_idx + 1, :]).astype(BF16)
        return carry

    lax.fori_loop(0, n_chunks // GDN_GROUP_CHUNKS, solve_group, 0)

    seq_chunks = n_chunks // n_sub
    chains = [(sub, a_idx) for sub in range(n_sub) for a_idx in range(N_GATE)]
    for j, (sub, a_idx) in enumerate(chains):
        st_s[j] = jnp.concatenate([jnp.zeros((HEAD_DIM, HEAD_DIM), F32), s0_ref[sub, a_idx]], axis=1)

    def scan_chunk(c, carry):
        def chunk_of(sub, a_idx):
            return sub * seq_chunks + ((seq_chunks - 1 - c) if a_idx >= N_HEADS else c)

        for j, (sub, a_idx) in enumerate(chains):
            wsqs_s[j] = _mm(wq_s[a_idx, chunk_of(sub, a_idx)], st_s[j].astype(BF16))
        for j, (sub, a_idx) in enumerate(chains):
            cc = chunk_of(sub, a_idx)
            rows = pl.ds(pl.multiple_of(cc * CHUNK, CHUNK), CHUNK)
            v_new = (uo_s[a_idx, rows, :] - wsqs_s[j, :CHUNK, :]).astype(BF16)
            uo_s[a_idx, rows, :] = wsqs_s[j, CHUNK:, :] + _mm(attn_s[a_idx, rows, :], v_new)
            st_s[j] = st_s[j] * cdec_s[cc, a_idx:a_idx + 1, :] + _mm(kdt_s[a_idx, cc], v_new)
        return carry

    lax.fori_loop(0, seq_chunks, scan_chunk, 0)
    for j, (sub, a_idx) in enumerate(chains):
        sfin_ref[sub, a_idx] = st_s[j, :, HEAD_DIM:]

    o = jnp.concatenate([(uo_s[h] + uo_s[N_HEADS + h])[:, HEAD_DIM:] for h in range(N_HEADS)], axis=1)
    ms = _head_sums(o * o, head_sum) * (1.0 / HEAD_DIM)
    o_ref[...] = o * lax.rsqrt(ms + EPS) * nw_ref[...] * _silu(a[:, 3 * W_GROUP:])


def _gdn(a_all, gate_all, conv_w, a_log, dt_bias, norm_w, s0, seq_len, n_seq, first_block, n_sub=1, into=None):
    small = lambda: pl.BlockSpec((1, LANES), lambda i: (0, 0))
    t = n_sub * seq_len
    n_chunks = t // CHUNK
    wide = 2 * HEAD_DIM
    scratch = [pltpu.VMEM((N_HEADS, t, HEAD_DIM), F32),
               pltpu.VMEM((N_HEADS, t, wide), F32),
               pltpu.VMEM((n_chunks, W_GROUP, CHUNK), F32),
               pltpu.VMEM((t, LANES), F32), pltpu.VMEM((t, LANES), F32), pltpu.VMEM((t, LANES), F32),
               pltpu.VMEM((n_chunks, N_GATE, CHUNK), F32), pltpu.VMEM((n_chunks, N_GATE, CHUNK), F32),
               pltpu.VMEM((n_chunks, N_GATE, wide), F32),
               pltpu.VMEM((N_GATE, t, wide), F32),
               pltpu.VMEM((N_GATE, n_chunks, 2 * CHUNK, HEAD_DIM), BF16),
               pltpu.VMEM((N_GATE, t, CHUNK), BF16),
               pltpu.VMEM((N_GATE, n_chunks, HEAD_DIM, CHUNK), BF16),
               pltpu.VMEM((n_sub * N_GATE, HEAD_DIM, wide), F32),
               pltpu.VMEM((n_sub * N_GATE, 2 * CHUNK, wide), F32),
               pltpu.VMEM((GDN_PAIRS, 2 * CHUNK, CHUNK), F32)]
    scratch += [pltpu.VMEM((GDN_CHAINS, CHUNK, CHUNK), F32)] * 3
    scratch += [pltpu.VMEM((GDN_CHAINS, CHUNK, CHUNK), BF16)] * 2
    scratch += [pltpu.VMEM((GDN_CHAINS, CHUNK, wide), BF16)]
    return _seq_call(
        functools.partial(_gdn_kernel, t=t, n_sub=n_sub), (a_all, gate_all, conv_w, a_log, dt_bias, norm_w, s0),
        in_specs=[pl.BlockSpec((t, 4 * W_GROUP), lambda i: (i + first_block, 0)),
                  pl.BlockSpec((t, LANES), lambda i: (i + first_block, 0)),
                  pl.BlockSpec((SUBLANES, 3 * W_GROUP), lambda i: (0, 0)),
                  small(), small(),
                  pl.BlockSpec((1, W_GROUP), lambda i: (0, 0)),
                  pl.BlockSpec((n_sub, N_GATE, HEAD_DIM, HEAD_DIM), lambda i: (i, 0, 0, 0))],
        out_specs=[pl.BlockSpec((t, W_GROUP), lambda i: (i + first_block, 0)),
                   pl.BlockSpec((n_sub, N_GATE, HEAD_DIM, HEAD_DIM), lambda i: (i, 0, 0, 0))],
        out_shape=[jax.ShapeDtypeStruct((N_TOK, W_GROUP), F32),
                   jax.ShapeDtypeStruct((n_seq, N_GATE, HEAD_DIM, HEAD_DIM), F32)],
        scratch_shapes=scratch,
        into=into, grid=(n_seq // n_sub,), compiler_params=_cparams("arbitrary"), name="gdn")


def _swap16(x):
    width = x.shape[-1]
    first = (_iota(x.shape, 1) // 16) % 2 == 0
    return jnp.where(first, pltpu.roll(x, width - 16, 1), pltpu.roll(x, 16, 1))


def _block_diag_heads(s0_ref, first):
    zero = jnp.zeros((HEAD_DIM, HEAD_DIM), F32)
    return jnp.concatenate(
        [jnp.concatenate([s0_ref[0, first + h] if j == h else zero for j in range(N_HEADS)], axis=1)
         for h in range(N_HEADS)], axis=0)


def _ret_kernel(r_ref, lg_ref, s0_ref, cos_ref, sin_ref, o_ref, sfin_ref, *, t, latent):
    r = r_ref[...]
    q = r[:, :W_GROUP]
    k = r[:, W_GROUP:2 * W_GROUP]
    v = r[:, 2 * W_GROUP:3 * W_GROUP]
    if latent:
        q = q * cos_ref[...] + _swap16(q) * sin_ref[...]
        k = k * cos_ref[...] + _swap16(k) * sin_ref[...]
    k = k * (HEAD_DIM ** -0.5)
    lg = -_softplus(-lg_ref[...])
    lgf, lgb = lg[0:1, :], lg[1:2, :]
    head = _iota((1, W_GROUP), 1) // HEAD_DIM
    head_sum = _head_sum_matrix()
    pos = _iota((t, 1), 0).astype(F32)
    q_b = q.astype(BF16)
    kt_b = k.T.astype(BF16)
    v_heads = [jnp.where(head == h, v, 0.0).astype(BF16) for h in range(N_HEADS)]
    if latent:
        s0f = _block_diag_heads(s0_ref, 0)
        s0b = _block_diag_heads(s0_ref, N_HEADS)
    for qt in range(t // Q_TILE):
        rows = slice(qt * Q_TILE, (qt + 1) * Q_TILE)
        diff = (_iota((Q_TILE, t), 0) + qt * Q_TILE - _iota((Q_TILE, t), 1)).astype(F32)
        both = jnp.where(diff == 0, 2.0, 1.0)
        o = jnp.zeros((Q_TILE, W_GROUP), F32)
        for h in range(N_HEADS):
            lgf_h = lgf[:, h * HEAD_DIM:h * HEAD_DIM + 1]
            lgb_h = lgb[:, h * HEAD_DIM:h * HEAD_DIM + 1]
            dmat = jnp.exp(diff * jnp.where(diff >= 0, lgf_h, -lgb_h)) * both
            s = _mm(jnp.where(head == h, q_b[rows], 0.0), kt_b) * dmat
            o = o + _mm(s.astype(BF16), v_heads[h])
        if latent:
            p = pos[rows]
            o = o + jnp.exp((p + 1.0) * lgf) * _bmm(q_b[rows], s0f) + jnp.exp((t - p) * lgb) * _bmm(q_b[rows], s0b)
        oc = o - _head_sums(o, head_sum) * (1.0 / HEAD_DIM)
        on = oc * lax.rsqrt(_head_sums(oc * oc, head_sum) * (1.0 / HEAD_DIM) + EPS)
        o_ref[rows, :] = on * _silu(r[rows, 3 * W_GROUP:])
    v_b = v.astype(BF16)
    sf = _mm_tn((k * jnp.exp((t - 1.0 - pos) * lgf)).astype(BF16), v_b)
    sb = _mm_tn((k * jnp.exp(pos * lgb)).astype(BF16), v_b)
    for h in range(N_HEADS):
        hs = slice(h * HEAD_DIM, (h + 1) * HEAD_DIM)
        sf_h, sb_h = sf[hs, hs], sb[hs, hs]
        if latent:
            sf_h = sf_h + jnp.exp(t * lgf[:, h * HEAD_DIM:h * HEAD_DIM + 1]) * s0_ref[0, h]
            sb_h = sb_h + jnp.exp(t * lgb[:, h * HEAD_DIM:h * HEAD_DIM + 1]) * s0_ref[0, N_HEADS + h]
        sfin_ref[0, h] = sf_h
        sfin_ref[0, N_HEADS + h] = sb_h


def _ret(r_all, logit, s0, cos, sin, t, n_seq, first_block, latent, into=None):
    return _seq_call(
        functools.partial(_ret_kernel, t=t, latent=latent), (r_all, logit, s0, cos, sin),
        in_specs=[pl.BlockSpec((t, 4 * W_GROUP), lambda i: (i + first_block, 0)),
                  pl.BlockSpec((2, W_GROUP), lambda i: (0, 0)),
                  pl.BlockSpec((1, N_GATE, HEAD_DIM, HEAD_DIM), lambda i: (i, 0, 0, 0)),
                  pl.BlockSpec((t, W_GROUP), lambda i: (0, 0)),
                  pl.BlockSpec((t, W_GROUP), lambda i: (0, 0))],
        out_specs=[pl.BlockSpec((t, W_GROUP), lambda i: (i + first_block, 0)),
                   pl.BlockSpec((1, N_GATE, HEAD_DIM, HEAD_DIM), lambda i: (i, 0, 0, 0))],
        out_shape=[jax.ShapeDtypeStruct((N_TOK, W_GROUP), F32),
                   jax.ShapeDtypeStruct((n_seq, N_GATE, HEAD_DIM, HEAD_DIM), F32)],
        into=into, grid=(n_seq,), compiler_params=_cparams("arbitrary"), name="ret")


def _rope_tables(t):
    pos = np.arange(t)
    row = (pos // GRID_W).astype(np.float32)
    col = (pos % GRID_W).astype(np.float32)
    nf = HEAD_DIM // 4
    inv_freq = jnp.power(ROPE_BASE, -jnp.arange(nf, dtype=F32) / nf)
    ang_r = jnp.asarray(row)[:, None] * inv_freq[None, :]
    ang_c = jnp.asarray(col)[:, None] * inv_freq[None, :]
    cos = jnp.concatenate([jnp.cos(ang_r)] * 2 + [jnp.cos(ang_c)] * 2, axis=1)
    sin = jnp.concatenate([-jnp.sin(ang_r), jnp.sin(ang_r), -jnp.sin(ang_c), jnp.sin(ang_c)], axis=1)
    return jnp.tile(cos, (1, N_HEADS)), jnp.tile(sin, (1, N_HEADS))


def _head_rms(x, w, head_sum):
    return x * lax.rsqrt(_head_sums(x * x, head_sum) * (1.0 / HEAD_DIM) + EPS) * w


def _ctx_attn_kernel(n_ref, qw_ref, kw_ref, o_ref, k_out, v_out, *, n_fill):
    n = n_ref[...]
    head_sum = _head_sum_matrix()
    head = _iota((1, W_GROUP), 1) // HEAD_DIM
    q = (_head_rms(n[:, :W_GROUP], qw_ref[...], head_sum) * (HEAD_DIM ** -0.5)).astype(BF16)
    k = _head_rms(n[:, W_GROUP:2 * W_GROUP], kw_ref[...], head_sum)
    v = n[:, 2 * W_GROUP:]
    k_t = k.T
    k_out[0, 0] = k_t
    v_out[0, 0] = v.T
    for later in range(1, 1 + n_fill):
        k_out[0, later] = jnp.zeros((W_GROUP, SEQ), F32)
        v_out[0, later] = jnp.zeros((W_GROUP, SEQ), F32)
    kt_b = k_t.astype(BF16)
    o = jnp.zeros((SEQ, W_GROUP), F32)
    for h in range(N_HEADS):
        s = _mm(jnp.where(head == h, q, 0.0), kt_b)
        p = jnp.exp(s - jnp.max(s, axis=-1, keepdims=True))
        o = o + _mm(p.astype(BF16), jnp.where(head == h, v, 0.0).astype(BF16)) / jnp.sum(p, axis=-1, keepdims=True)
    o_ref[...] = o


def _ctx_attn(n_all, qw, kw, into, layer, caches):
    slots = DEPTH if caches is None else 1
    cache_spec = pl.BlockSpec((1, slots, W_GROUP, SEQ), lambda i: (i, layer if caches is not None else 0, 0, 0))
    cache_shape = jax.ShapeDtypeStruct((BATCH, DEPTH, W_GROUP, SEQ), F32)
    return _seq_call(
        functools.partial(_ctx_attn_kernel, n_fill=slots - 1), (n_all, qw, kw),
        in_specs=[pl.BlockSpec((SEQ, 3 * W_GROUP), lambda i: (i, 0)),
                  pl.BlockSpec((1, W_GROUP), lambda i: (0, 0)),
                  pl.BlockSpec((1, W_GROUP), lambda i: (0, 0))],
        out_specs=[pl.BlockSpec((SEQ, W_GROUP), lambda i: (i, 0)), cache_spec, cache_spec],
        out_shape=[jax.ShapeDtypeStruct((N_TOK, W_GROUP), F32), cache_shape, cache_shape],
        into=into, more_into=() if caches is None else ((caches[0], 1), (caches[1], 2)),
        grid=(BATCH,), compiler_params=_cparams("arbitrary"), name="ctx_attn")


def _nat_kernel(n_ref, ck_ref, cv_ref, bias_ref, qw_ref, kw_ref, o_ref, q_s, kt_s, ckt_s):
    h = pl.program_id(1)

    @pl.when(h == 0)
    def _():
        head_sum = _head_sum_matrix()
        n = n_ref[...]
        q_s[...] = (_head_rms(n[:, :W_GROUP], qw_ref[...], head_sum) * (HEAD_DIM ** -0.5)).astype(BF16)
        kt_s[...] = _head_rms(n[:, W_GROUP:2 * W_GROUP], kw_ref[...], head_sum).T.astype(BF16)
        ckt_s[...] = ck_ref[0].T.astype(BF16)
        o_ref[...] = jnp.zeros_like(o_ref)

    mine = _iota((1, W_GROUP), 1) // HEAD_DIM == h
    v = jnp.where(mine, n_ref[:, 2 * W_GROUP:], 0.0).astype(BF16)
    cv = jnp.where(mine, cv_ref[0], 0.0).astype(BF16)
    for qt in range(DEC_SEQ // Q_TILE):
        rows = slice(qt * Q_TILE, (qt + 1) * Q_TILE)
        q = jnp.where(mine, q_s[rows, :], 0.0)
        grid_rows = range(qt * Q_TILE // GRID_W, (qt + 1) * Q_TILE // GRID_W)
        s_loc = _mm(q, kt_s[...]) + jnp.concatenate([_nat_bias_strip(bias_ref, row) for row in grid_rows], axis=0)
        s_ctx = _mm(q, ckt_s[...])
        m = jnp.maximum(jnp.max(s_loc, axis=-1, keepdims=True), jnp.max(s_ctx, axis=-1, keepdims=True))
        p_loc = jnp.exp(s_loc - m)
        p_ctx = jnp.exp(s_ctx - m)
        den = jnp.sum(p_loc, axis=-1, keepdims=True) + jnp.sum(p_ctx, axis=-1, keepdims=True)
        o_ref[rows, :] += (_mm(p_loc.astype(BF16), v) + _mm(p_ctx.astype(BF16), cv)) / den


def _nat(n_all, ck, cv, bias, qw, kw, into):
    first_block = N_CTX // DEC_SEQ
    return _seq_call(
        _nat_kernel, (n_all, ck, cv, bias, qw, kw),
        in_specs=[pl.BlockSpec((DEC_SEQ, 3 * W_GROUP), lambda b, h: (b + first_block, 0)),
                  pl.BlockSpec((1, PAST_LEN, W_GROUP), lambda b, h: (b, 0, 0)),
                  pl.BlockSpec((1, PAST_LEN, W_GROUP), lambda b, h: (b, 0, 0)),
                  pl.BlockSpec((1, 3 * N_ROW_OFF - 1, GRID_W, 2 * GRID_W), lambda b, h: (h, 0, 0, 0)),
                  pl.BlockSpec((1, W_GROUP), lambda b, h: (0, 0)),
                  pl.BlockSpec((1, W_GROUP), lambda b, h: (0, 0))],
        out_specs=pl.BlockSpec((DEC_SEQ, W_GROUP), lambda b, h: (b + first_block, 0)),
        out_shape=jax.ShapeDtypeStruct((N_TOK, W_GROUP), F32),
        scratch_shapes=[pltpu.VMEM((DEC_SEQ, W_GROUP), BF16), pltpu.VMEM((W_GROUP, DEC_SEQ), BF16),
                        pltpu.VMEM((W_GROUP, PAST_LEN), BF16)],
        into=into, grid=(DEC_BATCH, N_HEADS), compiler_params=_cparams("arbitrary", "arbitrary"), name="nat")


N_ROW_OFF = 2 * WIN_ROWS - 1
NAT_ROWS = DEC_SEQ // GRID_W
NAT_KH = min(WIN_ROWS, NAT_ROWS)
NAT_PAIR, NAT_LOW, NAT_HIGH = 0, N_ROW_OFF - 1, 2 * N_ROW_OFF - 1


def _nat_tables(rpb):
    c = np.arange(GRID_W)
    c0 = np.clip(c - WIN_COLS // 2, 0, GRID_W - WIN_COLS)
    col_in = (c[None, :] >= c0[:, None]) & (c[None, :] < c0[:, None] + WIN_COLS)
    col_idx = np.clip(c[None, :] - c[:, None], -(WIN_COLS - 1), WIN_COLS - 1) + WIN_COLS - 1
    col_hot = (col_idx[..., None] == np.arange(2 * WIN_COLS - 1)).astype(np.float32)
    tz = jnp.where(col_in, jnp.einsum('hab,qkb->haqk', rpb.astype(F32), col_hot, precision=HI), NEG_BIG)
    neg = jnp.full_like(tz, NEG_BIG)
    return jnp.concatenate([jnp.concatenate([tz[:, :-1], tz[:, 1:]], axis=-1),
                            jnp.concatenate([tz, neg], axis=-1), jnp.concatenate([neg, tz], axis=-1)], axis=1)


def _nat_bias_strip(tab_ref, row):
    first_key = min(max(row - NAT_KH // 2, 0), NAT_ROWS - NAT_KH)
    off = first_key - row + WIN_ROWS - 1
    tiles = {}
    done, key = 0, first_key
    if key % 2 == 1:
        tiles[key // 2] = tab_ref[0, NAT_HIGH + off]
        done, key = 1, key + 1
    while done + 1 < NAT_KH:
        tiles[key // 2] = tab_ref[0, NAT_PAIR + off + done]
        done, key = done + 2, key + 2
    if done < NAT_KH:
        tiles[key // 2] = tab_ref[0, NAT_LOW + off + done]
    outside = jnp.full((GRID_W, 2 * GRID_W), NEG_BIG, F32)
    return jnp.concatenate([tiles.get(i, outside) for i in range(NAT_ROWS // 2)], axis=1)


GSEL_LANE = N_EXPERTS


def _pack_router(we, be, wg, bg):
    pad = LANES - N_EXPERTS - N_GROUPS
    w = jnp.concatenate([we, wg, jnp.zeros((D_MODEL, pad), F32)], axis=1)
    b = jnp.concatenate([be, bg, jnp.zeros((pad,), F32)]).reshape(1, LANES)
    hi = w.astype(BF16)
    lo = (w - hi.astype(F32)).astype(BF16)
    return jnp.concatenate([hi, lo], axis=1), b


def _lane_min_where(mask, lane):
    return jnp.min(jnp.where(mask, lane, LANES), axis=-1, keepdims=True)


def _sconv_tile(i, s_ref, before_ref, after_ref, w_ref):
    s = s_ref[...]
    p = s[:, W_GROUP:2 * W_GROUP] * s[:, 2 * W_GROUP:]
    p_before = before_ref[SUBLANES - 1:, W_GROUP:2 * W_GROUP] * before_ref[SUBLANES - 1:, 2 * W_GROUP:]
    p_after = after_ref[:1, W_GROUP:2 * W_GROUP] * after_ref[:1, 2 * W_GROUP:]
    row = _iota((TOK_TILE, 1), 0)
    seq_len = jnp.where(i < N_CTX_TILES, SEQ, DEC_SEQ)
    pos = (i * TOK_TILE + row) & (seq_len - 1)
    prev = jnp.where(row == 0, p_before, pltpu.roll(p, 1, 0))
    nxt = jnp.where(row == TOK_TILE - 1, p_after, pltpu.roll(p, TOK_TILE - 1, 0))
    prev = jnp.where(pos == 0, 0.0, prev)
    nxt = jnp.where(pos == seq_len - 1, 0.0, nxt)
    return s[:, :W_GROUP] * (w_ref[0:1, :] * prev + w_ref[1:2, :] * p + w_ref[2:3, :] * nxt)


def _outproj_kernel(xc_ref, xl_ref, m0, m1, m2, s_ref, before_ref, after_ref, cw_ref, mod_ref, nw_ref, w_ref,
                    rw_ref, rb_ref, x_out, hf_out, route_out, w_s):
    tile_i = pl.program_id(0)

    @pl.when(tile_i == 0)
    def _():
        w_s[...] = w_ref[0].astype(BF16)

    mixed = [m0[...], m1[...], m2[...], _sconv_tile(tile_i, s_ref, before_ref, after_ref, cw_ref)]
    acc = None
    for i, m in enumerate(mixed):
        part = _mm(m.astype(BF16), w_s[i * W_GROUP:(i + 1) * W_GROUP, :])
        acc = part if acc is None else acc + part
    x = _token_tile(tile_i, xc_ref, xl_ref) + mod_ref[0, 2:3, :] * acc
    x_out[...] = x
    y = x * lax.rsqrt(jnp.mean(x * x, axis=-1, keepdims=True) + EPS) * nw_ref[...]
    hf = y * (1.0 + mod_ref[0, 4:5, :]) + mod_ref[0, 3:4, :]
    hf_hi = hf.astype(BF16)
    hf_out[...] = hf_hi

    hf_lo = (hf - hf_hi.astype(F32)).astype(BF16)
    both = _mm(hf_hi, rw_ref[...])
    logits = both[:, :LANES] + both[:, LANES:] + _mm(hf_lo, rw_ref[:, :LANES]) + rb_ref[...]
    lane = _iota(logits.shape, 1)
    is_g = (lane >= N_EXPERTS) & (lane < N_EXPERTS + N_GROUPS)
    gl = jnp.where(is_g, logits, NEG_BIG)
    ge = jnp.exp(gl - jnp.max(gl, axis=-1, keepdims=True))
    gp = jnp.where(is_g, ge / jnp.sum(ge, axis=-1, keepdims=True), -1.0)
    gw = jnp.max(gp, axis=-1, keepdims=True)
    gsel = _lane_min_where(gp == gw, lane) - N_EXPERTS
    in_grp = (lane // EXPERTS_PER_GROUP == gsel) & (lane < N_EXPERTS)
    el = jnp.where(in_grp, logits, NEG_BIG)
    ee = jnp.exp(el - jnp.max(el, axis=-1, keepdims=True))
    ep = jnp.where(in_grp, ee / jnp.sum(ee, axis=-1, keepdims=True), -1.0)
    t1 = jnp.max(ep, axis=-1, keepdims=True)
    i1 = _lane_min_where(ep == t1, lane)
    ep2 = jnp.where(lane == i1, -1.0, ep)
    t2 = jnp.max(ep2, axis=-1, keepdims=True)
    i2 = _lane_min_where(ep2 == t2, lane)
    tsum = t1 + t2
    combine = jnp.where(lane == i1, gw * (t1 / tsum), 0.0) + jnp.where(lane == i2, gw * (t2 / tsum), 0.0)
    route_out[...] = jnp.where(lane == GSEL_LANE, gsel.astype(F32), combine)


def _outproj(x_ctx, x_lat, split, mixed, a_sc, sc_w, mod, norm_w, w_out, layer, rw, rb):
    tile = lambda w: pl.BlockSpec((TOK_TILE, w), lambda i: (i, 0))
    whole = lambda a: pl.BlockSpec(a.shape, lambda i: (0,) * a.ndim)
    per_tile = TOK_TILE // SUBLANES
    halo = lambda index: pl.BlockSpec((SUBLANES, 3 * W_GROUP), lambda i: (index(i), 0))
    return pl.pallas_call(
        _outproj_kernel,
        grid=(N_TOK // TOK_TILE,),
        in_specs=_token_specs(split) + [tile(W_GROUP)] * 3
                 + [tile(3 * W_GROUP), halo(lambda i: jnp.maximum(i * per_tile - 1, 0)),
                    halo(lambda i: jnp.minimum((i + 1) * per_tile, N_TOK // SUBLANES - 1)), whole(sc_w)]
                 + [pl.BlockSpec((1, SUBLANES, D_MODEL), lambda i: (_cond_of_tile(i), 0, 0)),
                    pl.BlockSpec((1, D_MODEL), lambda i: (0, 0)),
                    pl.BlockSpec((1, D_MODEL, D_MODEL), lambda i: (layer, 0, 0), pipeline_mode=pl.Buffered(1)),
                    whole(rw), whole(rb)],
        out_specs=[tile(D_MODEL), tile(D_MODEL), tile(LANES)],
        out_shape=[jax.ShapeDtypeStruct((N_TOK, D_MODEL), F32), jax.ShapeDtypeStruct((N_TOK, D_MODEL), BF16),
                   jax.ShapeDtypeStruct((N_TOK, LANES), F32)],
        scratch_shapes=[pltpu.VMEM((D_MODEL, D_MODEL), BF16)],
        compiler_params=_cparams("arbitrary"),
        name="outproj",
    )(x_ctx, x_lat, *mixed, a_sc, a_sc, a_sc, sc_w, mod, norm_w.reshape(1, D_MODEL), w_out, rw, rb)


SEG_BLK = 16
LOCAL_ROWS = TOK_TILE + N_GROUPS * SEG_BLK
N_TOK_TILES = N_TOK // TOK_TILE
MOE_ROWS = -(-(N_TOK + N_TOK_TILES * N_GROUPS * (SEG_BLK - 1) + N_GROUPS * (MOE_TILE - 1)) // MOE_TILE) * MOE_TILE


def _moe_tables(gsel):
    groups = jnp.arange(N_GROUPS, dtype=jnp.int32)
    onehot = (gsel.reshape(N_TOK_TILES, TOK_TILE, 1) == groups).astype(jnp.int32)
    earlier = jnp.asarray(np.tril(np.ones((TOK_TILE, TOK_TILE), np.float32), -1))
    rank = jnp.einsum('ts,nsg->ntg', earlier, onehot.astype(F32)).astype(jnp.int32)
    nblk = (jnp.sum(onehot, axis=1) + SEG_BLK - 1) // SEG_BLK
    loc_blk = jnp.cumsum(nblk, axis=1) - nblk
    blocks_per_tile = MOE_TILE // SEG_BLK
    grp_tiles = (jnp.sum(nblk, axis=0) + blocks_per_tile - 1) // blocks_per_tile
    grp_tile_start = jnp.cumsum(grp_tiles) - grp_tiles
    dst_blk = grp_tile_start[None, :] * blocks_per_tile + jnp.cumsum(nblk, axis=0) - nblk
    local_pos = jnp.sum(onehot * (loc_blk[:, None, :] * SEG_BLK + rank), axis=2)
    tile_idx = jnp.arange(MOE_ROWS // MOE_TILE, dtype=jnp.int32)
    tile_group = jnp.clip(jnp.sum(tile_idx[:, None] >= grp_tile_start[None, :], axis=1) - 1, 0, N_GROUPS - 1)
    in_group = tile_idx - grp_tile_start[tile_group]
    tile_rows = jnp.clip(jnp.sum(nblk, axis=0)[tile_group] * SEG_BLK - in_group * MOE_TILE, 0, MOE_TILE)
    flat = lambda a: a.reshape(-1).astype(jnp.int32)
    return local_pos.astype(jnp.int32), flat(nblk), flat(loc_blk), flat(dst_blk), flat(tile_group), flat(tile_rows)


def _segment_copies(t, nblk, loc_blk, dst_blk, make_copies, start):
    for g in range(N_GROUPS):
        k = t * N_GROUPS + g

        @pl.loop(0, nblk[k])
        def _(b):
            local = pl.multiple_of((loc_blk[k] + b) * SEG_BLK, SEG_BLK)
            sorted_row = pl.multiple_of((dst_blk[k] + b) * SEG_BLK, SEG_BLK)
            for cp in make_copies(local, sorted_row):
                if start:
                    cp.start(priority=g % 2)
                else:
                    cp.wait()


def _dispatch_kernel(nblk, loc_blk, dst_blk, hf_ref, rt_ref, lp_ref, xs_in, rs_in, xs_hbm, rs_hbm, xbuf, rbuf, sem):
    t = pl.program_id(0)
    slot = t % 2
    onehot = _iota((LOCAL_ROWS, TOK_TILE), 0) == lp_ref[0]
    xbuf[slot] = _mm(onehot.astype(BF16), hf_ref[...]).astype(BF16)
    route = rt_ref[...]
    route_hi = route.astype(BF16)
    route_lo = (route - route_hi.astype(F32)).astype(BF16)
    rbuf[slot] = _mm(onehot.astype(BF16), route_hi) + _mm(onehot.astype(BF16), route_lo)

    def copies_of(s):
        def copies(local, sorted_row):
            return (pltpu.make_async_copy(xbuf.at[s, pl.ds(local, SEG_BLK)], xs_hbm.at[pl.ds(sorted_row, SEG_BLK)], sem.at[s]),
                    pltpu.make_async_copy(rbuf.at[s, pl.ds(local, SEG_BLK)], rs_hbm.at[pl.ds(sorted_row, SEG_BLK)], sem.at[s]))
        return copies

    @pl.when(t > 0)
    def _():
        _segment_copies(t - 1, nblk, loc_blk, dst_blk, copies_of(1 - slot), start=False)

    _segment_copies(t, nblk, loc_blk, dst_blk, copies_of(slot), start=True)

    @pl.when(t == N_TOK_TILES - 1)
    def _():
        _segment_copies(t, nblk, loc_blk, dst_blk, copies_of(slot), start=False)


def _dispatch(hf, route, local_pos, nblk, loc_blk, dst_blk):
    grid_spec = pltpu.PrefetchScalarGridSpec(
        num_scalar_prefetch=3,
        grid=(N_TOK_TILES,),
        in_specs=[pl.BlockSpec((TOK_TILE, D_MODEL), lambda t, *_: (t, 0)),
                  pl.BlockSpec((TOK_TILE, LANES), lambda t, *_: (t, 0)),
                  pl.BlockSpec((1, 1, TOK_TILE), lambda t, *_: (t, 0, 0)),
                  pl.BlockSpec(memory_space=pl.ANY), pl.BlockSpec(memory_space=pl.ANY)],
        out_specs=[pl.BlockSpec(memory_space=pl.ANY), pl.BlockSpec(memory_space=pl.ANY)],
        scratch_shapes=[pltpu.VMEM((2, LOCAL_ROWS, D_MODEL), BF16), pltpu.VMEM((2, LOCAL_ROWS, LANES), F32),
                        pltpu.SemaphoreType.DMA((2,))],
    )
    return pl.pallas_call(
        _dispatch_kernel,
        grid_spec=grid_spec,
        out_shape=[jax.ShapeDtypeStruct((MOE_ROWS, D_MODEL), BF16), jax.ShapeDtypeStruct((MOE_ROWS, LANES), F32)],
        input_output_aliases={6: 0, 7: 1},
        compiler_params=_cparams("arbitrary"),
        name="dispatch",
    )(nblk, loc_blk, dst_blk, hf, route, local_pos.reshape(N_TOK_TILES, 1, TOK_TILE),
      jnp.zeros((MOE_ROWS, D_MODEL), BF16), jnp.zeros((MOE_ROWS, LANES), F32))


def _moe_kernel(tile_group, tile_rows, x_ref, r_ref, wg_hbm, wu_hbm, wd_hbm, y_ref,
                wg_b, wu_b, wd_b, stage_g, stage_u, stage_d, sem, *, layer):
    i = pl.program_id(0)
    g = tile_group[i]
    group_row = layer * N_GROUPS + g
    new_group = (i == 0) | (g != tile_group[jnp.maximum(i - 1, 0)])
    valid = tile_rows[i] > 0
    half = tile_rows[i] <= MOE_TILE // 2

    def weight_copies(e):
        slot = e % 2
        return (pltpu.make_async_copy(wg_hbm.at[group_row, e], stage_g.at[slot], sem.at[slot]),
                pltpu.make_async_copy(wu_hbm.at[group_row, e], stage_u.at[slot], sem.at[slot]),
                pltpu.make_async_copy(wd_hbm.at[group_row, e], stage_d.at[slot], sem.at[slot]))

    def run(load_weights, rows):
        x = x_ref[:rows, :]
        route = r_ref[:rows, :]
        lane = _iota(route.shape, 1)
        acc = jnp.zeros((rows, D_MODEL), F32)
        if load_weights:
            for e in range(2):
                for cp in weight_copies(e):
                    cp.start()
        for e in range(EXPERTS_PER_GROUP):
            if load_weights:
                for cp in weight_copies(e):
                    cp.wait()
                wg_b[e] = stage_g[e % 2].astype(BF16)
                wu_b[e] = stage_u[e % 2].astype(BF16)
                wd_b[e] = stage_d[e % 2].astype(BF16)
                if e + 2 < EXPERTS_PER_GROUP:
                    for cp in weight_copies(e + 2):
                        cp.start()
            cw = jnp.sum(jnp.where(lane == g * EXPERTS_PER_GROUP + e, route, 0.0), axis=-1, keepdims=True)
            act = _silu(_mm(x, wg_b[e])) * _mm(x, wu_b[e]) * cw
            acc = acc + _mm(act.astype(BF16), wd_b[e])
        y_ref[:rows, :] = acc
        if rows < MOE_TILE:
            y_ref[rows:, :] = jnp.zeros((MOE_TILE - rows, D_MODEL), F32)

    for load_weights in (True, False):
        for rows in (MOE_TILE, MOE_TILE // 2):
            first = new_group if load_weights else jnp.logical_not(new_group)
            fits = half if rows < MOE_TILE else jnp.logical_not(half)

            @pl.when(valid & first & fits)
            def _():
                run(load_weights, rows)

    @pl.when(jnp.logical_not(valid))
    def _():
        y_ref[...] = jnp.zeros_like(y_ref)


def _moe(xs, rs, tile_group, tile_rows, wg, wu, wd, layer):
    any_spec = pl.BlockSpec(memory_space=pl.ANY)
    grid_spec = pltpu.PrefetchScalarGridSpec(
        num_scalar_prefetch=2,
        grid=(MOE_ROWS // MOE_TILE,),
        in_specs=[pl.BlockSpec((MOE_TILE, D_MODEL), lambda i, tg, tv: (i, 0)),
                  pl.BlockSpec((MOE_TILE, LANES), lambda i, tg, tv: (i, 0)),
                  any_spec, any_spec, any_spec],
        out_specs=pl.BlockSpec((MOE_TILE, D_MODEL), lambda i, tg, tv: (i, 0)),
        scratch_shapes=[pltpu.VMEM((EXPERTS_PER_GROUP, D_MODEL, EXPERT_FF), BF16),
                        pltpu.VMEM((EXPERTS_PER_GROUP, D_MODEL, EXPERT_FF), BF16),
                        pltpu.VMEM((EXPERTS_PER_GROUP, EXPERT_FF, D_MODEL), BF16),
                        pltpu.VMEM((2, D_MODEL, EXPERT_FF), F32), pltpu.VMEM((2, D_MODEL, EXPERT_FF), F32),
                        pltpu.VMEM((2, EXPERT_FF, D_MODEL), F32), pltpu.SemaphoreType.DMA((2,))],
    )
    return pl.pallas_call(
        functools.partial(_moe_kernel, layer=layer),
        grid_spec=grid_spec,
        out_shape=jax.ShapeDtypeStruct((MOE_ROWS, D_MODEL), F32),
        compiler_params=_cparams("arbitrary"),
        name="moe",
    )(tile_group, tile_rows, xs, rs, wg, wu, wd)


def _combine_kernel(nblk, loc_blk, dst_blk, x_ref, lp_ref, mod_ref, ys_hbm, *refs, split):
    o_refs, ybuf, sem = refs[:-2], refs[-2], refs[-1]
    t = pl.program_id(0)
    slot = t % 2

    def fetch(tile, s):
        def copies(local, sorted_row):
            return (pltpu.make_async_copy(ys_hbm.at[pl.ds(sorted_row, SEG_BLK)], ybuf.at[s, pl.ds(local, SEG_BLK)], sem.at[s]),)
        ybuf[s] = jnp.zeros((LOCAL_ROWS, D_MODEL), F32)
        _segment_copies(tile, nblk, loc_blk, dst_blk, copies, start=True)

    @pl.when(t == 0)
    def _():
        fetch(0, 0)

    @pl.when(t + 1 < N_TOK_TILES)
    def _():
        fetch(t + 1, 1 - slot)

    def copies_now(local, sorted_row):
        return (pltpu.make_async_copy(ys_hbm.at[pl.ds(sorted_row, SEG_BLK)], ybuf.at[slot, pl.ds(local, SEG_BLK)], sem.at[slot]),)

    _segment_copies(t, nblk, loc_blk, dst_blk, copies_now, start=False)

    onehot = (_iota((TOK_TILE, LOCAL_ROWS), 1) == lp_ref[...]).astype(BF16)
    ys = ybuf[slot]
    hi = ys.astype(BF16)
    lo = (ys - hi.astype(F32)).astype(BF16)
    y = _mm(onehot, hi) + _mm(onehot, lo)
    out = x_ref[...] + mod_ref[0, 5:6, :] * y
    if split:
        @pl.when(t < N_CTX_TILES)
        def _():
            o_refs[0][...] = out

        @pl.when(t >= N_CTX_TILES)
        def _():
            o_refs[1][...] = out
    else:
        o_refs[0][...] = out


def _combine(x, ys, mod, local_pos, nblk, loc_blk, dst_blk, split):
    tile = pl.BlockSpec((TOK_TILE, D_MODEL), lambda t, *_: (t, 0))
    if split:
        out_specs = _token_specs(True)
        out_shape = [jax.ShapeDtypeStruct((N_CTX, D_MODEL), F32), jax.ShapeDtypeStruct((N_LAT, D_MODEL), F32)]
    else:
        out_specs, out_shape = [tile], [jax.ShapeDtypeStruct((N_TOK, D_MODEL), F32)]
    grid_spec = pltpu.PrefetchScalarGridSpec(
        num_scalar_prefetch=3,
        grid=(N_TOK_TILES,),
        in_specs=[tile, pl.BlockSpec((TOK_TILE, 1), lambda t, *_: (t, 0)),
                  pl.BlockSpec((1, SUBLANES, D_MODEL), lambda t, *_: (_cond_of_tile(t), 0, 0)),
                  pl.BlockSpec(memory_space=pl.ANY)],
        out_specs=out_specs,
        scratch_shapes=[pltpu.VMEM((2, LOCAL_ROWS, D_MODEL), F32), pltpu.SemaphoreType.DMA((2,))],
    )
    return pl.pallas_call(
        functools.partial(_combine_kernel, split=split),
        grid_spec=grid_spec,
        out_shape=out_shape,
        compiler_params=_cparams("arbitrary"),
        name="combine",
    )(nblk, loc_blk, dst_blk, x, local_pos.reshape(N_TOK, 1), mod, ys)


def _lane_row(v):
    v = v.reshape(-1).astype(F32)
    return jnp.concatenate([v, jnp.zeros((LANES - v.shape[0],), F32)]).reshape(1, LANES)


def _pad_rows(w):
    return jnp.concatenate([w, jnp.zeros((SUBLANES - w.shape[0], w.shape[1]), w.dtype)], axis=0)


def kernel(x_prompt, x_sample, state_gdn, state_ret, cache_nat_k, cache_nat_v, c, c_ctx, ada_w, ada_b, norm_mix_w, norm_ffn_w, w_in, gdn_conv_w, gdn_a_log, gdn_dt_bias, gdn_norm_w, ret_gamma_logit, nat_q_norm_w, nat_k_norm_w, nat_rpb, sc_conv_w, w_out, router_group_w, router_group_b, router_expert_w, router_expert_b, moe_w_gate, moe_w_up, moe_w_down):
    x_ctx, x_lat, split = x_prompt.reshape(N_CTX, D_MODEL), x_sample.reshape(N_LAT, D_MODEL), True
    cond = jnp.concatenate([c_ctx[None, :], c], axis=0)
    ada = _ada(cond, ada_w, ada_b).reshape(DEPTH, SUBLANES, 6, D_MODEL)
    cos, sin = _rope_tables(DEC_SEQ)
    zero_state = jnp.zeros((BATCH, N_GATE, HEAD_DIM, HEAD_DIM), F32)
    lat_block = N_CTX // DEC_SEQ
    gdn_list, ret_list, caches = [], [], None
    mixed = [jnp.zeros((N_TOK, W_GROUP), F32) for _ in range(3)]
    for l in range(DEPTH):
        mod = jnp.concatenate([ada[l, :1 + DEC_BATCH], jnp.zeros((1 + DEC_BATCH, SUBLANES - 6, D_MODEL), F32)], axis=1)
        a_gdn, a_ret, a_nat, a_sc, a_gate = _inproj(x_ctx, x_lat, split, mod, norm_mix_w[l], w_in, l)

        conv_w = _pad_rows(gdn_conv_w[l])
        a_log, dt_b = _lane_row(gdn_a_log[l]), _lane_row(gdn_dt_bias[l])
        gnw = jnp.tile(gdn_norm_w[l], N_HEADS).reshape(1, W_GROUP)
        o_gdn, s_gdn = _gdn(a_gdn, a_gate, conv_w, a_log, dt_b, gnw, zero_state, SEQ, BATCH, 0, n_sub=2,
                            into=mixed[0])
        s0 = state_gdn[:, l].reshape(DEC_BATCH, N_GATE, HEAD_DIM, HEAD_DIM)
        o_gdn, _ = _gdn(a_gdn, a_gate, conv_w, a_log, dt_b, gnw, s0, DEC_SEQ, DEC_BATCH, lat_block, into=o_gdn)

        logit = jnp.repeat(ret_gamma_logit[l].astype(F32), HEAD_DIM, axis=1)
        o_ret, s_ret = _ret(a_ret, logit, zero_state, cos[:SEQ], sin[:SEQ], SEQ, BATCH, 0, False, into=mixed[1])
        s0 = state_ret[:, l].reshape(DEC_BATCH, N_GATE, HEAD_DIM, HEAD_DIM)
        o_ret, _ = _ret(a_ret, logit, s0, cos, sin, DEC_SEQ, DEC_BATCH, lat_block, True, into=o_ret)

        qw = jnp.tile(nat_q_norm_w[l], N_HEADS).reshape(1, W_GROUP)
        kw = jnp.tile(nat_k_norm_w[l], N_HEADS).reshape(1, W_GROUP)
        o_nat, *caches = _ctx_attn(a_nat, qw, kw, mixed[2], l, caches)
        o_nat = _nat(a_nat, cache_nat_k[:, l].reshape(DEC_BATCH, PAST_LEN, W_GROUP),
                     cache_nat_v[:, l].reshape(DEC_BATCH, PAST_LEN, W_GROUP), _nat_tables(nat_rpb[l]), qw, kw, o_nat)

        mixed = [o_gdn, o_ret, o_nat]
        rw, rb = _pack_router(router_expert_w[l], router_expert_b[l], router_group_w[l], router_group_b[l])
        x_mid, hf, route = _outproj(x_ctx, x_lat, split, mixed, a_sc, _pad_rows(sc_conv_w[l]), mod, norm_ffn_w[l],
                                    w_out, l, rw, rb)

        local_pos, nblk, loc_blk, dst_blk, tile_group, tile_rows = _moe_tables(route[:, GSEL_LANE].astype(jnp.int32))
        xs, rs = _dispatch(hf, route, local_pos, nblk, loc_blk, dst_blk)
        to_group = lambda w: w.reshape((DEPTH * N_GROUPS, EXPERTS_PER_GROUP) + w.shape[2:])
        ys = _moe(xs, rs, tile_group, tile_rows, to_group(moe_w_gate), to_group(moe_w_up), to_group(moe_w_down), l)
        last = l == DEPTH - 1
        out = _combine(x_mid, ys, mod, local_pos, nblk, loc_blk, dst_blk, split=last)
        x_ctx, x_lat, split = (out[0], out[1], True) if last else (out[0], out[0], False)

        gdn_list.append(s_gdn.reshape(BATCH, 2, N_HEADS, HEAD_DIM, HEAD_DIM))
        ret_list.append(s_ret.reshape(BATCH, 2, N_HEADS, HEAD_DIM, HEAD_DIM))
    new_k, new_v = [a.reshape(BATCH, DEPTH, N_HEADS, HEAD_DIM, SEQ).transpose(0, 1, 4, 2, 3) for a in caches]
    return (x_ctx.reshape(BATCH, SEQ, D_MODEL), x_lat.reshape(DEC_BATCH, DEC_SEQ, D_MODEL),
            jnp.stack(gdn_list, axis=1), jnp.stack(ret_list, axis=1), new_k, new_v)
```
